```python
import math
import jax
import jax.numpy as jnp
from jax import lax
import numpy as np


D_MODEL = 1024
BATCH = 8
SEQ = 2048
DEPTH = 2

CTX_LEN = 256
GRID_W = 64
HEAD_DIM = 64
ROPE_AXIS_DIM = HEAD_DIM // 2
ROPE_THETA = 10000.0
Q_BLOCK = 128
WINDOW = 128
A_Q_HEADS = 8
A_KV_HEADS = 2
A_GROUP = A_Q_HEADS // A_KV_HEADS
A_WIDTH = A_Q_HEADS * HEAD_DIM
B_HEADS = 4
B_V_DIM = 2 * HEAD_DIM
B_WIDTH = B_HEADS * B_V_DIM
IN_SIZES = (A_WIDTH, A_KV_HEADS * HEAD_DIM, A_KV_HEADS * HEAD_DIM,
            B_HEADS * 2 * HEAD_DIM, B_HEADS * 2 * HEAD_DIM, B_WIDTH)
RW_HEADS = D_MODEL // HEAD_DIM
DECAY_LORA = 64
ICLR_LORA = 64
GATE_LORA = 160
LNX_EPS = 64e-5
N_EXPERTS = 256
TOP_K = 8
N_GROUPS = 8
TOPK_GROUPS = 4
EXPERT_FF = 256
SHARED_FF = 256
ROUTED_SCALE = 2.5
MOE_BLOCK = 128
LN_EPS = 1e-5
SUBLN_EPS = 1e-5
N_ATT = (DEPTH + 1) // 2
N_RWKV = DEPTH // 2
DEEPNORM_ALPHA = (2 * DEPTH) ** 0.25
DEEPNORM_BETA = (8 * DEPTH) ** -0.25
NEG_INF = -1e30

kernel_name = 'hybrid_swa_diffattn_rwkv7_moe_dit'


def layer_norm(x, g, b):
    xf = x.astype(jnp.float32)
    mu = jnp.mean(xf, -1, keepdims=True)
    var = jnp.mean(jnp.square(xf - mu), -1, keepdims=True)
    return ((xf - mu) * lax.rsqrt(var + LN_EPS)).astype(x.dtype) * g + b


def modulate(x, shift, scale):
    return x * (1.0 + scale) + shift


def axial_rope(n):
    rows = n // GRID_W
    row = jnp.repeat(jnp.arange(rows), GRID_W).astype(jnp.float32)
    col = jnp.tile(jnp.arange(GRID_W), rows).astype(jnp.float32)
    inv = ROPE_THETA ** (-jnp.arange(0, ROPE_AXIS_DIM, 2, dtype=jnp.float32) / ROPE_AXIS_DIM)
    ar = row[:, None] * inv
    ac = col[:, None] * inv
    ang = jnp.concatenate([ar, ar, ac, ac], -1)
    return jnp.cos(ang), jnp.sin(ang)


def rot_half(x):
    x1, x2 = jnp.split(x, 2, axis=-1)
    return jnp.concatenate([-x2, x1], -1)


def apply_axial_rope(x, cos, sin):
    shape = (x.shape[1],) + (1,) * (x.ndim - 3) + (HEAD_DIM,)
    cos = cos.reshape(shape).astype(x.dtype)
    sin = sin.reshape(shape).astype(x.dtype)
    rot = jnp.concatenate([rot_half(x[..., :ROPE_AXIS_DIM]), rot_half(x[..., ROPE_AXIS_DIM:])], -1)
    return x * cos + rot * sin


def window_sink_attention(q, k, v, k_ctx, v_ctx, sink):
    Bn, S = q.shape[0], q.shape[1]
    nb = S // Q_BLOCK
    L = k_ctx.shape[1]
    scale = HEAD_DIM ** -0.5
    qb = q.reshape(Bn, nb, Q_BLOCK, A_KV_HEADS, A_GROUP, HEAD_DIM)

    def band(t):
        tp = jnp.pad(t, ((0, 0), (Q_BLOCK, Q_BLOCK), (0, 0), (0, 0)))
        tp = tp.reshape(Bn, nb + 2, Q_BLOCK, A_KV_HEADS, HEAD_DIM)
        return jnp.concatenate([tp[:, :-2], tp[:, 1:-1], tp[:, 2:]], axis=2)

    kw, vw = band(k), band(v)
    s_win = jnp.einsum('bnqhgd,bnkhd->bnhgqk', qb, kw).astype(jnp.float32) * scale
    s_ctx = jnp.einsum('bnqhgd,bchd->bnhgqc', qb, k_ctx).astype(jnp.float32) * scale
    qi = jnp.arange(nb)[:, None, None] * Q_BLOCK + jnp.arange(Q_BLOCK)[None, :, None]
    kj = (jnp.arange(nb)[:, None, None] - 1) * Q_BLOCK + jnp.arange(3 * Q_BLOCK)[None, None, :]
    valid = (jnp.abs(qi - kj) <= WINDOW) & (kj >= 0) & (kj < S)
    s_win = jnp.where(valid[None, :, None, None], s_win, NEG_INF)
    sink_col = jnp.broadcast_to(sink.astype(jnp.float32)[None, None, :, :, None, None], s_ctx.shape[:-1] + (1,))
    p = jax.nn.softmax(jnp.concatenate([sink_col, s_ctx, s_win], -1), axis=-1).astype(v.dtype)
    o = (jnp.einsum('bnhgqc,bchd->bnqhgd', p[..., 1:1 + L], v_ctx)
         + jnp.einsum('bnhgqk,bnkhd->bnqhgd', p[..., 1 + L:], vw))
    return o.reshape(Bn, S, A_WIDTH)


def ctx_sink_attention(q, k, v, sink):
    Bn, L = q.shape[0], q.shape[1]
    s = jnp.einsum('bqhgd,bkhd->bhgqk', q, k).astype(jnp.float32) * HEAD_DIM ** -0.5
    sink_col = jnp.broadcast_to(sink.astype(jnp.float32)[None, :, :, None, None], s.shape[:-1] + (1,))
    p = jax.nn.softmax(jnp.concatenate([sink_col, s], -1), axis=-1)[..., 1:].astype(v.dtype)
    return jnp.einsum('bhgqk,bkhd->bqhgd', p, v).reshape(Bn, L, A_WIDTH)


def diff_lambda(lam_vecs, lam_init):
    lv = lam_vecs.astype(jnp.float32)
    return jnp.exp(jnp.sum(lv[0] * lv[1])) - jnp.exp(jnp.sum(lv[2] * lv[3])) + lam_init


def diff_attend(q, k, v, lam):
    s = jnp.einsum('bqhmd,bkhmd->bhmqk', q, k).astype(jnp.float32) * HEAD_DIM ** -0.5
    p = jax.nn.softmax(s, axis=-1)
    a = (p[:, :, 0] - lam * p[:, :, 1]).astype(v.dtype)
    return jnp.einsum('bhqk,bkhe->bqhe', a, v)


def diff_attention_blocks(q, k, v, lam):
    Bn, S = q.shape[0], q.shape[1]
    nb = S // Q_BLOCK
    qb = jnp.moveaxis(q.reshape(Bn, nb, Q_BLOCK, B_HEADS, 2, HEAD_DIM), 1, 0)
    o = lax.map(lambda qq: diff_attend(qq, k, v, lam), qb)
    return jnp.moveaxis(o, 0, 1).reshape(Bn, S, B_HEADS, B_V_DIM)


def diff_head_norm(o, g, lam_init):
    of = o.astype(jnp.float32)
    of = of * lax.rsqrt(jnp.mean(of * of, -1, keepdims=True) + SUBLN_EPS)
    return of.astype(o.dtype) * g * (1.0 - lam_init)


def split_cols(h, sizes):
    offs = np.cumsum(sizes)[:-1]
    return jnp.split(h, [int(o) for o in offs], axis=-1)


def attn_mixer(u_ctx, u_lat, w_in, w_out, sink, lam_vecs, subln_g, lam_init, cos, sin, emit_ctx):
    Bn, L = u_ctx.shape[0], u_ctx.shape[1]
    S = u_lat.shape[1]
    N = L + S
    h = jnp.concatenate([u_ctx, u_lat], 1) @ w_in
    qa, ka, va, qb, kb, vb = split_cols(h, IN_SIZES)
    qa = qa.reshape(Bn, N, A_KV_HEADS, A_GROUP, HEAD_DIM)
    ka = ka.reshape(Bn, N, A_KV_HEADS, HEAD_DIM)
    va = va.reshape(Bn, N, A_KV_HEADS, HEAD_DIM)
    qb = qb.reshape(Bn, N, B_HEADS, 2, HEAD_DIM)
    kb = kb.reshape(Bn, N, B_HEADS, 2, HEAD_DIM)
    vb = vb.reshape(Bn, N, B_HEADS, B_V_DIM)
    qa_l = apply_axial_rope(qa[:, L:], cos, sin)
    ka_l = apply_axial_rope(ka[:, L:], cos, sin)
    qb_l = apply_axial_rope(qb[:, L:], cos, sin)
    kb_l = apply_axial_rope(kb[:, L:], cos, sin)
    ka_c, va_c, kb_c = ka[:, :L], va[:, :L], kb[:, :L]
    lam = diff_lambda(lam_vecs, lam_init)
    oa_l = window_sink_attention(qa_l, ka_l, va[:, L:], ka_c, va_c, sink)
    kb_all = jnp.concatenate([kb_c, kb_l], 1)
    ob_l = diff_head_norm(diff_attention_blocks(qb_l, kb_all, vb, lam), subln_g, lam_init)
    o_lat = jnp.concatenate([oa_l, ob_l.reshape(Bn, S, B_WIDTH)], -1) @ w_out
    if not emit_ctx:
        return None, o_lat
    oa_c = ctx_sink_attention(qa[:, :L], ka_c, va_c, sink)
    ob_c = diff_head_norm(diff_attend(qb[:, :L], kb_c, vb[:, :L], lam), subln_g, lam_init)
    o_ctx = jnp.concatenate([oa_c, ob_c.reshape(Bn, L, B_WIDTH)], -1) @ w_out
    return o_ctx, o_lat


def centred_shift_delta(x):
    xp = jnp.pad(x, ((0, 0), (1, 1), (0, 0)))
    return 0.5 * (xp[:, :-2] + xp[:, 2:]) - x


def wkv_scan(state0, seq, reverse, emit):
    def step(st, inp):
        r, w, k, v, a, b = inp
        sa = jnp.einsum('bhvk,bhk->bhv', st, a)
        st = st * w[:, :, None, :] + sa[..., None] * b[:, :, None, :] + v[..., None] * k[:, :, None, :]
        y = jnp.einsum('bhvk,bhk->bhv', st, r) if emit else None
        return st, y
    return lax.scan(step, state0, seq, reverse=reverse)


def rwkv7_bidir_mixer(u_ctx, u_lat, mu, w_rkv, w_out, dec0, dec1, dec2, icl0, icl1, icl2,
                      gate1, gate2, k_k, k_a, r_k, lnx, emit_ctx):
    Bn, L, D = u_ctx.shape
    u = jnp.concatenate([u_ctx, u_lat], 1)
    Nt = u.shape[1]
    dx = jnp.concatenate([centred_shift_delta(u_ctx), centred_shift_delta(u_lat)], 1)
    xr, xw, xk, xv, xa, xg = [u + dx * mu[m] for m in range(6)]
    r, k, v = xr @ w_rkv[0], xk @ w_rkv[1], xv @ w_rkv[2]
    g = jax.nn.sigmoid(xg @ gate1) @ gate2

    def heads(t):
        return t.astype(jnp.float32).reshape(Bn, Nt, RW_HEADS, HEAD_DIM)

    r_h, k_h, v_h = heads(r), heads(k), heads(v)
    kk = heads(k * k_k)
    kk = kk * lax.rsqrt(jnp.maximum(jnp.sum(kk * kk, -1, keepdims=True), 1e-24))
    k_a_h = k_a.astype(jnp.float32).reshape(RW_HEADS, HEAD_DIM)
    state0 = jnp.zeros((Bn, RW_HEADS, HEAD_DIM, HEAD_DIM), jnp.float32)
    y_lat_dirs, y_ctx_dirs, k_dirs = [], [], []
    for d in range(2):
        logw = -jax.nn.softplus(-(dec0[d] + jnp.tanh(xw @ dec1[d]) @ dec2[d])) - 0.5
        decay = jnp.exp(-jnp.exp(heads(logw)))
        a = heads(jax.nn.sigmoid(icl0[d] + (xa @ icl1[d]) @ icl2[d]))
        k_d = k_h * (1.0 + (a - 1.0) * k_a_h)
        k_dirs.append(k_d)
        seq = tuple(jnp.swapaxes(t, 0, 1) for t in (r_h, decay, k_d, v_h, -kk, kk * a))
        rev = d == 1
        s_ctx, y_c = wkv_scan(state0, tuple(t[:L] for t in seq), rev, emit_ctx)
        _, y_l = wkv_scan(s_ctx, tuple(t[L:] for t in seq), rev, True)
        y_lat_dirs.append(y_l)
        y_ctx_dirs.append(y_c)
    y_lat = y_lat_dirs[0] + y_lat_dirs[1]
    if emit_ctx:
        y = jnp.concatenate([y_ctx_dirs[0] + y_ctx_dirs[1], y_lat], 0)
        lo = 0
    else:
        y = y_lat
        lo = L
    y = jnp.swapaxes(y, 0, 1)
    ym = jnp.mean(y, -1, keepdims=True)
    yv = jnp.mean(jnp.square(y - ym), -1, keepdims=True)
    n_out = y.shape[1]
    yn = ((y - ym) * lax.rsqrt(yv + LNX_EPS)).reshape(Bn, n_out, D) * lnx[0] + lnx[1]
    k_sum = k_dirs[0] + k_dirs[1]
    bonus = jnp.sum(r_h[:, lo:] * k_sum[:, lo:] * r_k, -1, keepdims=True) * v_h[:, lo:]
    o = ((yn + bonus.reshape(Bn, n_out, D)).astype(u.dtype) * g[:, lo:]) @ w_out
    if emit_ctx:
        return o[:, :L], o[:, L:]
    return None, o


def moe_ffn(u, router, bias, w_in, w_out, ws_in, ws_out):
    Bn, N, D = u.shape
    T = Bn * N
    xt = u.reshape(T, D)
    scores = jax.nn.sigmoid((xt @ router).astype(jnp.float32))
    sel = scores + bias.astype(jnp.float32)
    per_group = N_EXPERTS // N_GROUPS
    grp_score = lax.top_k(sel.reshape(T, N_GROUPS, per_group), 2)[0].sum(-1)
    _, gidx = lax.top_k(grp_score, TOPK_GROUPS)
    gmask = jnp.any(gidx[:, :, None] == jnp.arange(N_GROUPS)[None, None, :], axis=1)
    sel = jnp.where(jnp.repeat(gmask, per_group, axis=1), sel, NEG_INF)
    _, eidx = lax.top_k(sel, TOP_K)
    gw = jnp.take_along_axis(scores, eidx, axis=1)
    gw = gw / jnp.sum(gw, -1, keepdims=True) * ROUTED_SCALE
    TK = T * TOP_K
    flat_e = eidx.reshape(-1)
    order = jnp.argsort(flat_e)
    sorted_e = flat_e[order]
    counts = jnp.bincount(flat_e, length=N_EXPERTS)
    padded = (counts + MOE_BLOCK - 1) // MOE_BLOCK * MOE_BLOCK
    start = jnp.cumsum(counts) - counts
    pstart = jnp.cumsum(padded) - padded
    dest = pstart[sorted_e] + jnp.arange(TK) - start[sorted_e]
    n_blocks = -(-(TK + N_EXPERTS * (MOE_BLOCK - 1)) // MOE_BLOCK)
    P = n_blocks * MOE_BLOCK
    row_tok = jnp.full((P,), T, jnp.int32).at[dest].set((order // TOP_K).astype(jnp.int32))
    row_w = jnp.zeros((P,), jnp.float32).at[dest].set(gw.reshape(-1)[order])
    block_exp = jnp.minimum(
        jnp.searchsorted(jnp.cumsum(padded), jnp.arange(n_blocks) * MOE_BLOCK, side='right'),
        N_EXPERTS - 1)
    x_pad = jnp.concatenate([xt, jnp.zeros((1, D), xt.dtype)], 0)

    def block(acc, blk):
        rows, e, rw = blk
        xb = x_pad[rows]
        gate, up = jnp.split(xb @ w_in[e], 2, axis=-1)
        out = (jax.nn.silu(gate) * up) @ w_out[e]
        return acc.at[rows].add(out.astype(jnp.float32) * rw[:, None]), None

    acc, _ = lax.scan(block, jnp.zeros((T + 1, D), jnp.float32),
                      (row_tok.reshape(n_blocks, MOE_BLOCK), block_exp, row_w.reshape(n_blocks, MOE_BLOCK)))
    sg, su = jnp.split(xt @ ws_in, 2, axis=-1)
    shared = (jax.nn.silu(sg) * su) @ ws_out
    return (acc[:T].astype(u.dtype) + shared).reshape(Bn, N, D)


def setup_inputs(seed: int = 0) -> dict:
    key = jax.random.key(seed)
    ks = iter(jax.random.split(key, 64))
    f32 = jnp.float32
    D = D_MODEL
    sd = D ** -0.5
    beta = DEEPNORM_BETA

    def nrm(shape, scale):
        return jax.random.normal(next(ks), shape, f32) * scale

    x = nrm((BATCH, SEQ, D), 1.0)
    c = nrm((BATCH, D), 1.0)
    ctx = nrm((BATCH, CTX_LEN, D), 1.0)
    c_ctx = nrm((D,), 1.0)
    ada_w = nrm((DEPTH, D, 6 * D), 0.5 * sd)
    ada_b = nrm((DEPTH, 6 * D), 0.02)
    post_ln_g = 1.0 + nrm((DEPTH, 2, D), 0.02)
    post_ln_b = nrm((DEPTH, 2, D), 0.02)
    att_w_in = jnp.concatenate([
        nrm((N_ATT, D, IN_SIZES[0]), sd),
        nrm((N_ATT, D, IN_SIZES[1]), sd),
        nrm((N_ATT, D, IN_SIZES[2]), sd * beta),
        nrm((N_ATT, D, IN_SIZES[3]), sd),
        nrm((N_ATT, D, IN_SIZES[4]), sd),
        nrm((N_ATT, D, IN_SIZES[5]), sd * beta)], axis=-1)
    att_w_out = nrm((N_ATT, A_WIDTH + B_WIDTH, D), (A_WIDTH + B_WIDTH) ** -0.5 * beta)
    att_sink = nrm((N_ATT, A_KV_HEADS, A_GROUP), 0.5)
    diff_lambda_vecs = nrm((N_ATT, 4, HEAD_DIM), 0.1)
    diff_subln_g = 1.0 + nrm((N_ATT, B_V_DIM), 0.02)
    rk_mu = jax.random.uniform(next(ks), (N_RWKV, 6, D), f32, 0.2, 0.8)
    rk_w_rkv = nrm((N_RWKV, 3, D, D), sd) * jnp.array([1.0, 1.0, beta], f32)[:, None, None]
    rk_w_out = nrm((N_RWKV, D, D), sd * beta)
    ramp = -6.5 + 5.0 * (jnp.arange(D, dtype=f32) / (D - 1)) ** 0.85
    rk_decay0 = ramp + nrm((N_RWKV, 2, D), 0.1)
    rk_decay1 = nrm((N_RWKV, 2, D, DECAY_LORA), sd)
    rk_decay2 = nrm((N_RWKV, 2, DECAY_LORA, D), 0.1 * DECAY_LORA ** -0.5)
    rk_iclr0 = nrm((N_RWKV, 2, D), 0.1)
    rk_iclr1 = nrm((N_RWKV, 2, D, ICLR_LORA), sd)
    rk_iclr2 = nrm((N_RWKV, 2, ICLR_LORA, D), 0.5 * ICLR_LORA ** -0.5)
    rk_gate1 = nrm((N_RWKV, D, GATE_LORA), sd)
    rk_gate2 = nrm((N_RWKV, GATE_LORA, D), GATE_LORA ** -0.5)
    rk_k_k = 0.85 + nrm((N_RWKV, D), 0.02)
    rk_k_a = 1.0 + nrm((N_RWKV, D), 0.02)
    rk_r_k = -0.04 + nrm((N_RWKV, RW_HEADS, HEAD_DIM), 0.02)
    rk_lnx = jnp.stack([1.0 + nrm((N_RWKV, D), 0.02), nrm((N_RWKV, D), 0.02)], axis=1)
    moe_router = nrm((DEPTH, D, N_EXPERTS), sd)
    moe_bias = nrm((DEPTH, N_EXPERTS), 0.01)
    moe_w_in = nrm((DEPTH, N_EXPERTS, D, 2 * EXPERT_FF), sd * beta)
    moe_w_out = nrm((DEPTH, N_EXPERTS, EXPERT_FF, D), EXPERT_FF ** -0.5 * beta)
    moe_ws_in = nrm((DEPTH, D, 2 * SHARED_FF), sd * beta)
    moe_ws_out = nrm((DEPTH, SHARED_FF, D), SHARED_FF ** -0.5 * beta)
    return {'x': x, 'c': c, 'ctx': ctx, 'c_ctx': c_ctx,
            'ada_w': ada_w, 'ada_b': ada_b, 'post_ln_g': post_ln_g, 'post_ln_b': post_ln_b,
            'att_w_in': att_w_in, 'att_w_out': att_w_out, 'att_sink': att_sink,
            'diff_lambda_vecs': diff_lambda_vecs, 'diff_subln_g': diff_subln_g,
            'rk_mu': rk_mu, 'rk_w_rkv': rk_w_rkv, 'rk_w_out': rk_w_out,
            'rk_decay0': rk_decay0, 'rk_decay1': rk_decay1, 'rk_decay2': rk_decay2,
            'rk_iclr0': rk_iclr0, 'rk_iclr1': rk_iclr1, 'rk_iclr2': rk_iclr2,
            'rk_gate1': rk_gate1, 'rk_gate2': rk_gate2, 'rk_k_k': rk_k_k, 'rk_k_a': rk_k_a,
            'rk_r_k': rk_r_k, 'rk_lnx': rk_lnx,
            'moe_router': moe_router, 'moe_bias': moe_bias, 'moe_w_in': moe_w_in,
            'moe_w_out': moe_w_out, 'moe_ws_in': moe_ws_in, 'moe_ws_out': moe_ws_out}


def reference(x, c, ctx, c_ctx, ada_w, ada_b, post_ln_g, post_ln_b,
              att_w_in, att_w_out, att_sink, diff_lambda_vecs, diff_subln_g,
              rk_mu, rk_w_rkv, rk_w_out, rk_decay0, rk_decay1, rk_decay2,
              rk_iclr0, rk_iclr1, rk_iclr2, rk_gate1, rk_gate2, rk_k_k, rk_k_a, rk_r_k, rk_lnx,
              moe_router, moe_bias, moe_w_in, moe_w_out, moe_ws_in, moe_ws_out):
    S = x.shape[1]
    L = ctx.shape[1]
    cos, sin = axial_rope(S)
    c_act = jax.nn.silu(c)
    cc_act = jax.nn.silu(c_ctx)
    h_lat, h_ctx = x, ctx
    for i in range(DEPTH):
        last = i == DEPTH - 1
        j = i // 2
        m_lat = jnp.split((c_act @ ada_w[i] + ada_b[i])[:, None, :], 6, axis=-1)
        m_ctx = jnp.split(cc_act @ ada_w[i] + ada_b[i], 6, axis=-1)
        u_lat = modulate(h_lat, m_lat[0], m_lat[1])
        u_ctx = modulate(h_ctx, m_ctx[0], m_ctx[1])
        if i % 2 == 0:
            lam_init = 0.8 - 0.6 * math.exp(-0.3 * i)
            o_ctx, o_lat = attn_mixer(u_ctx, u_lat, att_w_in[j], att_w_out[j], att_sink[j],
                                      diff_lambda_vecs[j], diff_subln_g[j], lam_init, cos, sin,
                                      not last)
        else:
            o_ctx, o_lat = rwkv7_bidir_mixer(u_ctx, u_lat, rk_mu[j], rk_w_rkv[j], rk_w_out[j],
                                             rk_decay0[j], rk_decay1[j], rk_decay2[j],
                                             rk_iclr0[j], rk_iclr1[j], rk_iclr2[j],
                                             rk_gate1[j], rk_gate2[j], rk_k_k[j], rk_k_a[j],
                                             rk_r_k[j], rk_lnx[j], not last)
        h_lat = layer_norm(DEEPNORM_ALPHA * h_lat + m_lat[2] * o_lat, post_ln_g[i, 0], post_ln_b[i, 0])
        u_lat = modulate(h_lat, m_lat[3], m_lat[4])
        if last:
            f_lat = moe_ffn(u_lat, moe_router[i], moe_bias[i], moe_w_in[i], moe_w_out[i],
                            moe_ws_in[i], moe_ws_out[i])
        else:
            h_ctx = layer_norm(DEEPNORM_ALPHA * h_ctx + m_ctx[2] * o_ctx, post_ln_g[i, 0], post_ln_b[i, 0])
            u_ctx = modulate(h_ctx, m_ctx[3], m_ctx[4])
            f = moe_ffn(jnp.concatenate([u_ctx, u_lat], 1), moe_router[i], moe_bias[i],
                        moe_w_in[i], moe_w_out[i], moe_ws_in[i], moe_ws_out[i])
            f_ctx, f_lat = f[:, :L], f[:, L:]
            h_ctx = layer_norm(DEEPNORM_ALPHA * h_ctx + m_ctx[5] * f_ctx, post_ln_g[i, 1], post_ln_b[i, 1])
        h_lat = layer_norm(DEEPNORM_ALPHA * h_lat + m_lat[5] * f_lat, post_ln_g[i, 1], post_ln_b[i, 1])
    return h_lat
```

```python
import functools
import math

import numpy as np
import jax
import jax.numpy as jnp
from jax import lax
from jax.experimental import pallas as pl
from jax.experimental.pallas import tpu as pltpu

F32 = jnp.float32
BF16 = jnp.bfloat16
I32 = jnp.int32

HEAD_DIM = 64
GRID_W = 64
ROPE_AXIS_DIM = HEAD_DIM // 2
ROPE_THETA = 10000.0
Q_BLOCK = 128
A_Q_HEADS = 8
A_KV_HEADS = 2
A_GROUP = A_Q_HEADS // A_KV_HEADS
A_WIDTH = A_Q_HEADS * HEAD_DIM
A_KV_WIDTH = A_KV_HEADS * HEAD_DIM
B_HEADS = 4
B_V_DIM = 2 * HEAD_DIM
B_WIDTH = B_HEADS * B_V_DIM
LNX_EPS = 64e-5
N_EXPERTS = 256
TOP_K = 8
N_GROUPS = 8
TOPK_GROUPS = 4
ROUTED_SCALE = 2.5
MOE_BLOCK = 128
LN_EPS = 1e-5
SUBLN_EPS = 1e-5
NEG_INF = -1e30
DEPTH = 2
DEEPNORM_ALPHA = (2 * DEPTH) ** 0.25

LANES = 128
ROW_TILE = 256
PROJ_TILE = 128
SCAN_CHUNK = 64
SCAN_COLS = 512
VMEM_LIMIT = 56 * 1024 * 1024


def _cparams(sem):
    return pltpu.CompilerParams(dimension_semantics=sem, vmem_limit_bytes=VMEM_LIMIT)


def _silu(x):
    return x * jax.nn.sigmoid(x)


def _layer_norm(z, g, b):
    mu = jnp.mean(z, -1, keepdims=True)
    zc = z - mu
    var = jnp.mean(zc * zc, -1, keepdims=True)
    return zc * lax.rsqrt(var + LN_EPS) * g + b


def _dot(a, b):
    return jnp.dot(a, b, preferred_element_type=F32)


def _dot_nt(a, b):
    return lax.dot_general(a, b, (((1,), (1,)), ((), ())), preferred_element_type=F32)


def _ada_kernel(c_ref, w_ref, b_ref, o_ref):
    c = c_ref[...]
    o_ref[...] = jnp.dot(_silu(c), w_ref[...], preferred_element_type=F32,
                         precision=lax.Precision.HIGHEST) + b_ref[...]


def _ada_mod(cvec, w, bias):
    R, D = cvec.shape
    n_out = w.shape[1]
    tn = 768
    return pl.pallas_call(
        _ada_kernel,
        grid=(n_out // tn,),
        in_specs=[pl.BlockSpec((R, D), lambda j: (0, 0)),
                  pl.BlockSpec((D, tn), lambda j: (0, j)),
                  pl.BlockSpec((1, tn), lambda j: (0, j))],
        out_specs=pl.BlockSpec((R, tn), lambda j: (0, j)),
        out_shape=jax.ShapeDtypeStruct((R, n_out), F32),
        compiler_params=_cparams(("arbitrary",)),
    )(cvec, w, bias.reshape(1, n_out))


def _mod_table(m, batch, d):
    m_lat = m[:batch].reshape(batch, 6, d)
    m_ctx = jnp.broadcast_to(m[batch].reshape(1, 6, d), (batch, 6, d))
    return jnp.stack([m_ctx, m_lat], axis=1)


def _mod_spec(d, ctx_tiles):
    return pl.BlockSpec((1, 1, 6, d), lambda b, i: (b, jnp.minimum(i // ctx_tiles, 1), 0, 0))


def _rope_tables(n_ctx, n_lat):
    rows = n_lat // GRID_W
    row = np.repeat(np.arange(rows), GRID_W).astype(np.float32)
    col = np.tile(np.arange(GRID_W), rows).astype(np.float32)
    inv = (ROPE_THETA ** (-np.arange(0, ROPE_AXIS_DIM, 2, dtype=np.float32) / ROPE_AXIS_DIM)).astype(np.float32)
    ar = row[:, None] * inv
    ac = col[:, None] * inv
    ang = np.concatenate([ar, ar, ac, ac], -1)
    cos = np.cos(ang).astype(np.float32)
    sin = np.sin(ang).astype(np.float32)
    lower = (np.arange(HEAD_DIM) % ROPE_AXIS_DIM) < (ROPE_AXIS_DIM // 2)
    sin_up = np.where(lower[None, :], -sin, 0.0)
    sin_dn = np.where(lower[None, :], 0.0, sin)

    def full(t, ctx_fill):
        t = np.concatenate([np.full((n_ctx, HEAD_DIM), ctx_fill, np.float32), t], 0)
        return jnp.asarray(np.tile(t, (1, LANES // HEAD_DIM)))

    return full(cos, 1.0), full(sin_up, 0.0), full(sin_dn, 0.0)


def _inproj_kernel(h_ref, mod_ref, w_ref, cos_ref, su_ref, sd_ref,
                   qa_ref, ka_ref, va_ref, qb_ref, kb_ref, vb_ref):
    h = h_ref[0]
    shift = mod_ref[0, 0, 0:1, :]
    scale = mod_ref[0, 0, 1:2, :]
    u = (h * (1.0 + scale) + shift).astype(BF16)
    y = _dot(u, w_ref[...])
    cos, s_up, s_dn = cos_ref[...], su_ref[...], sd_ref[...]
    q_scale = HEAD_DIM ** -0.5

    def rope(xc):
        half = ROPE_AXIS_DIM // 2
        return xc * cos + pltpu.roll(xc, LANES - half, 1) * s_up + pltpu.roll(xc, half, 1) * s_dn

    def emit(out_ref, col0, width, roped, mul):
        for j in range(width // LANES):
            xc = y[:, col0 + j * LANES: col0 + (j + 1) * LANES]
            if roped:
                xc = rope(xc)
            if mul != 1.0:
                xc = xc * mul
            out_ref[0, :, j * LANES:(j + 1) * LANES] = xc.astype(out_ref.dtype)

    c = 0
    emit(qa_ref, c, A_WIDTH, True, q_scale); c += A_WIDTH
    emit(ka_ref, c, A_KV_WIDTH, True, 1.0); c += A_KV_WIDTH
    emit(va_ref, c, A_KV_WIDTH, False, 1.0); c += A_KV_WIDTH
    emit(qb_ref, c, B_WIDTH, True, q_scale); c += B_WIDTH
    emit(kb_ref, c, B_WIDTH, True, 1.0); c += B_WIDTH
    emit(vb_ref, c, B_WIDTH, False, 1.0)


def _attn_inproj(h, mod, w_in, n_ctx):
    B, N, D = h.shape
    tm = ROW_TILE
    cos, s_up, s_dn = _rope_tables(n_ctx, N - n_ctx)
    widths = (A_WIDTH, A_KV_WIDTH, A_KV_WIDTH, B_WIDTH, B_WIDTH, B_WIDTH)
    tab_spec = pl.BlockSpec((tm, LANES), lambda b, i: (i, 0))
    return pl.pallas_call(
        _inproj_kernel,
        grid=(B, N // tm),
        in_specs=[pl.BlockSpec((1, tm, D), lambda b, i: (b, i, 0)),
                  _mod_spec(D, n_ctx // tm),
                  pl.BlockSpec(w_in.shape, lambda b, i: (0, 0)),
                  tab_spec, tab_spec, tab_spec],
        out_specs=[pl.BlockSpec((1, tm, w), lambda b, i: (b, i, 0)) for w in widths],
        out_shape=[jax.ShapeDtypeStruct((B, N, w), BF16) for w in widths],
        compiler_params=_cparams(("parallel", "arbitrary")),
    )(h, mod, w_in.astype(BF16), cos, s_up, s_dn)


def _win_attn_kernel(n_ctx_blocks, n_blocks, q_ref, kc_ref, vc_ref, kl_ref, km_ref, kr_ref,
                     vl_ref, vm_ref, vr_ref, sink_ref, o_ref):
    j = pl.program_id(1)
    is_lat = j >= n_ctx_blocks
    qb = Q_BLOCK
    n_c = kc_ref.shape[1]
    rows = A_GROUP * qb
    n_keys = n_c + 3 * qb
    far = 1 << 20
    r_idx = lax.broadcasted_iota(I32, (rows, n_keys), 0) % qb
    cw = lax.broadcasted_iota(I32, (rows, n_keys), 1) - n_c
    off_l = jnp.where(jnp.logical_and(is_lat, j > n_ctx_blocks), 0, far)
    end_m = jnp.where(is_lat, 2 * qb, qb)
    off_r = jnp.where(jnp.logical_and(is_lat, j < n_blocks - 1), 0, far)
    valid = ((cw < 0)
             | ((cw >= 0) & (cw < qb) & (cw >= r_idx + off_l))
             | ((cw >= qb) & (cw < end_m))
             | ((cw >= 2 * qb) & (cw - 2 * qb + off_r <= r_idx)))
    outs = []
    for kv in range(A_KV_HEADS):
        cols = slice(kv * HEAD_DIM, (kv + 1) * HEAD_DIM)
        k_all = jnp.concatenate([kc_ref[0, :, cols], kl_ref[0, :, cols], km_ref[0, :, cols],
                                 kr_ref[0, :, cols]], axis=0)
        v_all = jnp.concatenate([vc_ref[0, :, cols], vl_ref[0, :, cols], vm_ref[0, :, cols],
                                 vr_ref[0, :, cols]], axis=0)
        q0 = kv * A_GROUP
        q = jnp.concatenate([q_ref[0, :, (q0 + g) * HEAD_DIM:(q0 + g + 1) * HEAD_DIM]
                             for g in range(A_GROUP)], axis=0)
        sink = jnp.concatenate([jnp.broadcast_to(sink_ref[q0 + g:q0 + g + 1, 0:1], (qb, 1))
                                for g in range(A_GROUP)], axis=0)
        s = jnp.where(valid, _dot_nt(q, k_all), NEG_INF)
        m = jnp.maximum(jnp.max(s, -1, keepdims=True), sink)
        e = jnp.exp(s - m)
        denom = jnp.sum(e, -1, keepdims=True) + jnp.exp(sink - m)
        p = (e / denom).astype(BF16)
        o = _dot(p, v_all)
        outs += [o[g * qb:(g + 1) * qb] for g in range(A_GROUP)]
    for j2 in range(A_Q_HEADS // 2):
        pair = jnp.concatenate([outs[2 * j2], outs[2 * j2 + 1]], axis=1)
        o_ref[0, :, j2 * LANES:(j2 + 1) * LANES] = pair.astype(o_ref.dtype)


def _win_attn(qa, ka, va, sink, n_ctx):
    B, N, _ = qa.shape
    qb = Q_BLOCK
    nb = N // qb
    ncb = n_ctx // qb
    sink_pad = jnp.broadcast_to(sink.reshape(A_Q_HEADS, 1).astype(F32), (A_Q_HEADS, LANES))

    def left(b, j):
        return (b, jnp.clip(j - 1, ncb, nb - 1), 0)

    def mid(b, j):
        return (b, jnp.clip(j, ncb, nb - 1), 0)

    def right(b, j):
        return (b, jnp.clip(j + 1, ncb, nb - 1), 0)

    kv_blk = lambda im: pl.BlockSpec((1, qb, A_KV_WIDTH), im)
    ctx_blk = pl.BlockSpec((1, n_ctx, A_KV_WIDTH), lambda b, j: (b, 0, 0))
    return pl.pallas_call(
        functools.partial(_win_attn_kernel, ncb, nb),
        grid=(B, nb),
        in_specs=[pl.BlockSpec((1, qb, A_WIDTH), lambda b, j: (b, j, 0)),
                  ctx_blk, ctx_blk,
                  kv_blk(left), kv_blk(mid), kv_blk(right),
                  kv_blk(left), kv_blk(mid), kv_blk(right),
                  pl.BlockSpec((A_Q_HEADS, LANES), lambda b, j: (0, 0))],
        out_specs=pl.BlockSpec((1, qb, A_WIDTH), lambda b, j: (b, j, 0)),
        out_shape=jax.ShapeDtypeStruct((B, N, A_WIDTH), BF16),
        compiler_params=_cparams(("parallel", "arbitrary")),
    )(qa, ka, va, ka, ka, ka, va, va, va, sink_pad)


def _diff_attn_kernel(n_ctx, lam_init, q_ref, k_ref, v_ref, lv_ref, g_ref, o_ref):
    j = pl.program_id(1)
    lv = lv_ref[...]
    lam = (jnp.exp(jnp.sum(lv[0:1] * lv[1:2], -1, keepdims=True))
           - jnp.exp(jnp.sum(lv[2:3] * lv[3:4], -1, keepdims=True)) + lam_init)
    gain = g_ref[...] * (1.0 - lam_init)

    def run(n_keys):
        for hd in range(B_HEADS):
            parts = []
            for mm in range(2):
                c0 = (hd * 2 + mm) * HEAD_DIM
                q = q_ref[0, :, c0:c0 + HEAD_DIM]
                k = k_ref[0, :n_keys, c0:c0 + HEAD_DIM]
                s = _dot_nt(q, k)
                e = jnp.exp(s - jnp.max(s, -1, keepdims=True))
                parts.append((e, jnp.sum(e, -1, keepdims=True)))
            (e0, l0), (e1, l1) = parts
            a = (e0 * (1.0 / l0) - e1 * (lam / l1)).astype(BF16)
            o = _dot(a, v_ref[0, :n_keys, hd * B_V_DIM:(hd + 1) * B_V_DIM])
            o = o * lax.rsqrt(jnp.mean(o * o, -1, keepdims=True) + SUBLN_EPS) * gain
            o_ref[0, :, hd * B_V_DIM:(hd + 1) * B_V_DIM] = o.astype(o_ref.dtype)

    @pl.when(j == 0)
    def _():
        run(n_ctx)

    @pl.when(j > 0)
    def _():
        run(k_ref.shape[1])


def _diff_attn(qb, kb, vb, lam_vecs, subln_g, lam_init, n_ctx):
    B, N, _ = qb.shape
    tq = n_ctx
    return pl.pallas_call(
        functools.partial(_diff_attn_kernel, n_ctx, lam_init),
        grid=(B, N // tq),
        in_specs=[pl.BlockSpec((1, tq, B_WIDTH), lambda b, j: (b, j, 0)),
                  pl.BlockSpec((1, N, B_WIDTH), lambda b, j: (b, 0, 0)),
                  pl.BlockSpec((1, N, B_WIDTH), lambda b, j: (b, 0, 0)),
                  pl.BlockSpec((4, HEAD_DIM), lambda b, j: (0, 0)),
                  pl.BlockSpec((1, B_V_DIM), lambda b, j: (0, 0))],
        out_specs=pl.BlockSpec((1, tq, B_WIDTH), lambda b, j: (b, j, 0)),
        out_shape=jax.ShapeDtypeStruct((B, N, B_WIDTH), BF16),
        compiler_params=_cparams(("parallel", "arbitrary")),
    )(qb, kb, vb, lam_vecs.astype(F32), subln_g.reshape(1, B_V_DIM).astype(F32))


def _mix_out_kernel(n_in, *refs):
    xs = refs[:n_in]
    ws = refs[n_in:2 * n_in]
    h_ref, mod_ref, g_ref, b_ref, hn_ref, u_ref = refs[2 * n_in:]
    o = _dot(xs[0][0], ws[0][...])
    for x_ref, w_ref in zip(xs[1:], ws[1:]):
        o = o + _dot(x_ref[0], w_ref[...])
    z = DEEPNORM_ALPHA * h_ref[0] + mod_ref[0, 0, 2:3, :] * o
    hn = _layer_norm(z, g_ref[...], b_ref[...])
    hn_ref[0] = hn
    u_ref[0] = hn * (1.0 + mod_ref[0, 0, 4:5, :]) + mod_ref[0, 0, 3:4, :]


def _mix_out(xs, ws, h, mod, ln_g, ln_b, n_ctx, row0):
    B, N, D = h.shape
    tm = ROW_TILE
    t0 = row0 // tm
    n_out = N - row0
    row_spec = lambda w: pl.BlockSpec((1, tm, w), lambda b, i: (b, i + t0, 0))
    out_spec = pl.BlockSpec((1, tm, D), lambda b, i: (b, i, 0))
    vec_spec = pl.BlockSpec((1, D), lambda b, i: (0, 0))
    return pl.pallas_call(
        functools.partial(_mix_out_kernel, len(xs)),
        grid=(B, n_out // tm),
        in_specs=([row_spec(x.shape[-1]) for x in xs]
                  + [pl.BlockSpec(w.shape, lambda b, i: (0, 0)) for w in ws]
                  + [row_spec(D),
                     pl.BlockSpec((1, 1, 6, D), lambda b, i: (b, jnp.minimum((i + t0) // (n_ctx // tm), 1), 0, 0)),
                     vec_spec, vec_spec]),
        out_specs=[out_spec, out_spec],
        out_shape=[jax.ShapeDtypeStruct((B, n_out, D), F32)] * 2,
        compiler_params=_cparams(("parallel", "arbitrary")),
    )(*xs, *[w.astype(BF16) for w in ws], h, mod, ln_g.reshape(1, D), ln_b.reshape(1, D))


def _router_kernel(u_ref, rt_ref, bias_ref, tri_ref, e_ref, gw_ref, rank_ref, cnt_ref, carry_ref):
    i = pl.program_id(0)

    @pl.when(i == 0)
    def _():
        carry_ref[...] = jnp.zeros_like(carry_ref)

    tm = u_ref.shape[0]
    per_group = N_EXPERTS // N_GROUPS
    neg = -jnp.inf
    logits = _dot_nt(rt_ref[...], u_ref[...].astype(BF16))
    scores = jax.nn.sigmoid(logits)
    sel = scores + bias_ref[...]
    io_in = lax.broadcasted_iota(I32, (per_group, tm), 0)
    grp_rows = []
    for gi in range(N_GROUPS):
        sg = sel[gi * per_group:(gi + 1) * per_group]
        m1 = jnp.max(sg, axis=0, keepdims=True)
        i1 = jnp.min(jnp.where(sg == m1, io_in, per_group), axis=0, keepdims=True)
        m2 = jnp.max(jnp.where(io_in == i1, neg, sg), axis=0, keepdims=True)
        grp_rows.append(m1 + m2)
    grp = jnp.concatenate(grp_rows, axis=0)
    io_g = lax.broadcasted_iota(I32, grp.shape, 0)
    g_sel = jnp.zeros(grp.shape, F32)
    for _ in range(TOPK_GROUPS):
        m = jnp.max(grp, axis=0, keepdims=True)
        hit = io_g == jnp.min(jnp.where(grp == m, io_g, N_GROUPS), axis=0, keepdims=True)
        g_sel = jnp.where(hit, 1.0, g_sel)
        grp = jnp.where(hit, neg, grp)
    selm = jnp.concatenate(
        [jnp.where(g_sel[gi:gi + 1] > 0.5, sel[gi * per_group:(gi + 1) * per_group], NEG_INF)
         for gi in range(N_GROUPS)], axis=0)
    io_e = lax.broadcasted_iota(I32, selm.shape, 0)
    chosen_f = jnp.zeros(selm.shape, F32)
    idx, gws = [], []
    for _ in range(TOP_K):
        m = jnp.max(selm, axis=0, keepdims=True)
        ik = jnp.min(jnp.where(selm == m, io_e, N_EXPERTS), axis=0, keepdims=True)
        hit = io_e == ik
        idx.append(ik)
        gws.append(jnp.sum(jnp.where(hit, scores, 0.0), axis=0, keepdims=True))
        chosen_f = jnp.where(hit, 1.0, chosen_f)
        selm = jnp.where(hit, neg, selm)
    gw = jnp.concatenate(gws, axis=0)
    gw_ref[...] = gw / jnp.sum(gw, axis=0, keepdims=True) * ROUTED_SCALE
    e_ref[...] = jnp.concatenate(idx, axis=0)
    before = _dot(chosen_f.astype(BF16), tri_ref[...]) + carry_ref[...]
    ranks = [jnp.sum(jnp.where(io_e == ik, before, 0.0), axis=0, keepdims=True) for ik in idx]
    rank_ref[...] = jnp.concatenate(ranks, axis=0).astype(I32)
    carry_ref[...] = carry_ref[...] + jnp.sum(chosen_f, axis=1, keepdims=True)
    cnt_ref[...] = carry_ref[...].astype(I32)


def _router(u, router, bias):
    T, D = u.shape
    tm = ROW_TILE
    tri = jnp.asarray(np.triu(np.ones((tm, tm), np.float32), 1), BF16)
    tok_spec = pl.BlockSpec((TOP_K, tm), lambda i: (0, i))
    return pl.pallas_call(
        _router_kernel,
        grid=(T // tm,),
        in_specs=[pl.BlockSpec((tm, D), lambda i: (i, 0)),
                  pl.BlockSpec((N_EXPERTS, D), lambda i: (0, 0)),
                  pl.BlockSpec((N_EXPERTS, 1), lambda i: (0, 0)),
                  pl.BlockSpec((tm, tm), lambda i: (0, 0))],
        out_specs=[tok_spec, tok_spec, tok_spec, pl.BlockSpec((N_EXPERTS, 1), lambda i: (0, 0))],
        out_shape=[jax.ShapeDtypeStruct((TOP_K, T), I32), jax.ShapeDtypeStruct((TOP_K, T), F32),
                   jax.ShapeDtypeStruct((TOP_K, T), I32), jax.ShapeDtypeStruct((N_EXPERTS, 1), I32)],
        scratch_shapes=[pltpu.VMEM((N_EXPERTS, 1), F32)],
        compiler_params=_cparams(("arbitrary",)),
    )(u, router.T.astype(BF16), bias.reshape(N_EXPERTS, 1).astype(F32), tri)


def _dispatch_kernel(dest_ref, x_hbm, xs_in, xs_hbm, sem):
    del xs_in
    i = pl.program_id(0)
    tm = dest_ref.shape[1]

    def row_copy(t, k):
        return pltpu.make_async_copy(x_hbm.at[pl.ds(i * tm + t, 1)],
                                     xs_hbm.at[pl.ds(dest_ref[k, t], 1)], sem)

    def start(t, carry):
        for k in range(TOP_K):
            row_copy(t, k).start()
        return carry

    def wait(t, carry):
        for k in range(TOP_K):
            row_copy(t, k).wait()
        return carry

    lax.fori_loop(0, tm, start, 0)
    lax.fori_loop(0, tm, wait, 0)


def _dispatch(x, dest, n_rows):
    T, D = x.shape
    tm = ROW_TILE
    xs0 = jnp.zeros((n_rows, D), x.dtype)
    return pl.pallas_call(
        _dispatch_kernel,
        grid=(T // tm,),
        in_specs=[pl.BlockSpec((TOP_K, tm), lambda i: (0, i), memory_space=pltpu.SMEM),
                  pl.BlockSpec(memory_space=pl.ANY),
                  pl.BlockSpec(memory_space=pl.ANY)],
        out_specs=pl.BlockSpec(memory_space=pl.ANY),
        out_shape=jax.ShapeDtypeStruct((n_rows, D), x.dtype),
        scratch_shapes=[pltpu.SemaphoreType.DMA(())],
        input_output_aliases={2: 0},
        compiler_params=_cparams(("arbitrary",)),
    )(dest, x, xs0)


def _expert_kernel(be_ref, nu_ref, x_ref, wi_ref, wo_ref, y_ref):
    i = pl.program_id(0)

    @pl.when(i < nu_ref[0])
    def _():
        ff = wo_ref.shape[1]
        hcat = _dot(x_ref[...].astype(BF16), wi_ref[0].astype(BF16))
        act = (_silu(hcat[:, :ff]) * hcat[:, ff:]).astype(BF16)
        y_ref[...] = _dot(act, wo_ref[0].astype(BF16))

    @pl.when(i >= nu_ref[0])
    def _():
        y_ref[...] = jnp.zeros_like(y_ref)


def _experts(xs, block_exp, n_used, w_in, w_out):
    P, D = xs.shape
    E, _, ff2 = w_in.shape
    nblk = P // MOE_BLOCK
    row_map = lambda i, be, nu: (i, 0)
    return pl.pallas_call(
        _expert_kernel,
        grid_spec=pltpu.PrefetchScalarGridSpec(
            num_scalar_prefetch=2,
            grid=(nblk,),
            in_specs=[pl.BlockSpec((MOE_BLOCK, D), row_map),
                      pl.BlockSpec((1, D, ff2), lambda i, be, nu: (be[i], 0, 0)),
                      pl.BlockSpec((1, ff2 // 2, D), lambda i, be, nu: (be[i], 0, 0))],
            out_specs=pl.BlockSpec((MOE_BLOCK, D), row_map)),
        out_shape=jax.ShapeDtypeStruct((P, D), F32),
        compiler_params=_cparams(("arbitrary",)),
    )(block_exp, n_used, xs, w_in, w_out)


def _combine_kernel(dest_ref, y_hbm, gw_ref, u_ref, wsi_ref, wso_ref, h_ref, mod_ref, g_ref, b_ref,
                    o_ref, buf, sem):
    tm = u_ref.shape[0]

    def row_copy(t, k):
        return pltpu.make_async_copy(y_hbm.at[pl.ds(dest_ref[k, t], 1)], buf.at[k, pl.ds(t, 1)], sem)

    def start(t, carry):
        for k in range(TOP_K):
            row_copy(t, k).start()
        return carry

    def wait(t, carry):
        for k in range(TOP_K):
            row_copy(t, k).wait()
        return carry

    lax.fori_loop(0, tm, start, 0)
    ff = wso_ref.shape[0]
    hcat = _dot(u_ref[...].astype(BF16), wsi_ref[...])
    shared = _dot((_silu(hcat[:, :ff]) * hcat[:, ff:]).astype(BF16), wso_ref[...])
    lax.fori_loop(0, tm, wait, 0)
    routed = buf[0] * gw_ref[:, 0:1]
    for k in range(1, TOP_K):
        routed = routed + buf[k] * gw_ref[:, k:k + 1]
    z = DEEPNORM_ALPHA * h_ref[...] + mod_ref[0] * (routed + shared)
    o_ref[...] = _layer_norm(z, g_ref[...], b_ref[...])


def _combine(y, dest, gw_t, u, ws_in, ws_out, h, gate, gate_index, ln_g, ln_b):
    T, D = u.shape
    tm = ROW_TILE
    vec_spec = pl.BlockSpec((1, D), lambda i: (0, 0))
    row_spec = pl.BlockSpec((tm, D), lambda i: (i, 0))
    return pl.pallas_call(
        _combine_kernel,
        grid=(T // tm,),
        in_specs=[pl.BlockSpec((TOP_K, tm), lambda i: (0, i), memory_space=pltpu.SMEM),
                  pl.BlockSpec(memory_space=pl.ANY),
                  pl.BlockSpec((tm, TOP_K), lambda i: (i, 0)),
                  row_spec,
                  pl.BlockSpec(ws_in.shape, lambda i: (0, 0)),
                  pl.BlockSpec(ws_out.shape, lambda i: (0, 0)),
                  row_spec,
                  pl.BlockSpec((1,) + gate.shape[1:], lambda i: (gate_index(i), 0, 0)),
                  vec_spec, vec_spec],
        out_specs=row_spec,
        out_shape=jax.ShapeDtypeStruct((T, D), F32),
        scratch_shapes=[pltpu.VMEM((TOP_K, tm, D), F32), pltpu.SemaphoreType.DMA(())],
        compiler_params=_cparams(("arbitrary",)),
    )(dest, y, gw_t, u, ws_in.astype(BF16), ws_out.astype(BF16), h, gate,
      ln_g.reshape(1, D), ln_b.reshape(1, D))


def _moe_layer(u, h, gate, gate_index, router, bias, w_in, w_out, ws_in, ws_out, ln_g, ln_b):
    T, D = u.shape
    eidx, gw, rank, counts = _router(u, router, bias)
    counts = counts[:, 0]
    padded = (counts + MOE_BLOCK - 1) // MOE_BLOCK * MOE_BLOCK
    pend = jnp.cumsum(padded)
    pstart = pend - padded
    dest = (pstart[eidx] + rank).astype(I32)
    n_blocks = -(-(T * TOP_K + N_EXPERTS * (MOE_BLOCK - 1)) // MOE_BLOCK)
    block_exp = jnp.minimum(jnp.searchsorted(pend, jnp.arange(n_blocks, dtype=I32) * MOE_BLOCK, side='right'),
                            N_EXPERTS - 1).astype(I32)
    n_used = (pend[-1:] // MOE_BLOCK).astype(I32)
    xs = _dispatch(u, dest, n_blocks * MOE_BLOCK)
    y = _experts(xs, block_exp, n_used, w_in, w_out)
    return _combine(y, dest, gw.T, u, ws_in, ws_out, h, gate, gate_index, ln_g, ln_b)


def _seg_ones(width=LANES):
    idx = np.arange(width) // HEAD_DIM
    return jnp.asarray((idx[:, None] == idx[None, :]).astype(np.float32), BF16)


def _head_sum(x, ones_ref):
    outs = []
    for j in range(x.shape[1] // LANES):
        xc = x[:, j * LANES:(j + 1) * LANES]
        hi = xc.astype(BF16)
        lo = (xc - hi.astype(F32)).astype(BF16)
        outs.append(_dot(hi, ones_ref[...]) + _dot(lo, ones_ref[...]))
    return jnp.concatenate(outs, axis=1)


def _rwkv_proj_kernel(u_ref, dx_ref, mu_ref, wrkv_ref, g1_ref, g2_ref, d1_ref, d2_ref, d0_ref,
                      i1_ref, i2_ref, i0_ref, kk_ref, ka_ref, rk_ref, ones_ref,
                      r_ref, v_ref, a_ref, g_ref, bonus_ref, w_ref, k_ref, b_ref):
    u = u_ref[...]
    dx = dx_ref[...]
    mix = lambda m: (u + dx * mu_ref[m:m + 1, :])
    xr, xw, xk, xv, xa, xg = [mix(m) for m in range(6)]
    r = _dot(xr.astype(BF16), wrkv_ref[0])
    k = _dot(xk.astype(BF16), wrkv_ref[1])
    v = _dot(xv.astype(BF16), wrkv_ref[2])
    g = _dot(jax.nn.sigmoid(_dot(xg.astype(BF16), g1_ref[...])).astype(BF16), g2_ref[...])
    kk = k * kk_ref[...]
    kk = kk * lax.rsqrt(jnp.maximum(_head_sum(kk * kk, ones_ref), 1e-24))
    r_ref[...] = r
    v_ref[...] = v
    a_ref[...] = -kk
    g_ref[...] = g
    k_sum = None
    xw_b = xw.astype(BF16)
    xa_b = xa.astype(BF16)
    for d in range(2):
        lw = d0_ref[d:d + 1, :] + _dot(jnp.tanh(_dot(xw_b, d1_ref[d])).astype(BF16), d2_ref[d])
        softplus = jnp.maximum(-lw, 0.0) + jnp.log(1.0 + jnp.exp(-jnp.abs(lw)))
        logw = -softplus - 0.5
        w_ref[d] = jnp.exp(-jnp.exp(logw))
        eta = jax.nn.sigmoid(i0_ref[d:d + 1, :] + _dot(_dot(xa_b, i1_ref[d]).astype(BF16), i2_ref[d]))
        k_d = k * (1.0 + (eta - 1.0) * ka_ref[...])
        k_ref[d] = k_d
        b_ref[d] = kk * eta
        k_sum = k_d if k_sum is None else k_sum + k_d
    bonus_ref[...] = _head_sum(r * k_sum * rk_ref[...], ones_ref) * v


def _rwkv_proj(u, dx, p):
    T, D = u.shape
    tm = PROJ_TILE
    row = pl.BlockSpec((tm, D), lambda i: (i, 0))
    row2 = pl.BlockSpec((2, tm, D), lambda i: (0, i, 0))
    full = lambda a: pl.BlockSpec(a.shape, lambda i: (0,) * a.ndim)
    bf = lambda a: a.astype(BF16)
    consts = [p['mu'], bf(p['w_rkv']), bf(p['gate1']), bf(p['gate2']), bf(p['dec1']), bf(p['dec2']), p['dec0'],
              bf(p['icl1']), bf(p['icl2']), p['icl0'], p['k_k'].reshape(1, D), p['k_a'].reshape(1, D),
              p['r_k'].reshape(1, D), _seg_ones()]
    one = jax.ShapeDtypeStruct((T, D), F32)
    two = jax.ShapeDtypeStruct((2, T, D), F32)
    return pl.pallas_call(
        _rwkv_proj_kernel,
        grid=(T // tm,),
        in_specs=[row, row] + [full(a) for a in consts],
        out_specs=[row, row, row, row, row, row2, row2, row2],
        out_shape=[one, one, one, one, one, two, two, two],
        compiler_params=_cparams(("parallel",)),
    )(u, dx, *consts)


def _scan_kernel(r_ref, w_ref, k_ref, v_ref, a_ref, b_ref, ones_ref, vsel_ref, hsel_ref,
                 y_ref, s_ref, vt_ref):
    d = pl.program_id(0)
    c = pl.program_id(2)
    tc, nb = r_ref.shape[0], r_ref.shape[1]
    n_pairs = r_ref.shape[2] // LANES

    @pl.when(c == 0)
    def _():
        s_ref[...] = jnp.zeros_like(s_ref)

    for bb in range(nb):
        for hp in range(n_pairs):
            vt = v_ref[:, bb, hp * LANES:(hp + 1) * LANES].T
            vt_ref[bb * n_pairs + hp] = jnp.concatenate([vt[:HEAD_DIM], vt[HEAD_DIM:]], axis=1).astype(BF16)

    lane2 = lax.broadcasted_iota(I32, (HEAD_DIM, 2 * tc), 1) % tc

    def step(s_i, carry):
        t = jnp.where(d == 0, s_i, tc - 1 - s_i)
        at_t = lane2 == t
        for bb in range(nb):
            for hp in range(n_pairs):
                tile = bb * n_pairs + hp
                cols = slice(hp * LANES, (hp + 1) * LANES)
                row = lambda ref: ref[t, bb:bb + 1, cols]
                row2 = lambda ref: ref[0, t, bb:bb + 1, cols]
                st = s_ref[tile]
                sa = _dot((st * row(a_ref)).astype(BF16), ones_ref[...])
                vcol = _dot(jnp.where(at_t, vt_ref[tile], jnp.zeros((), BF16)), vsel_ref[...])
                st = st * row2(w_ref) + sa * row2(b_ref) + vcol * row2(k_ref)
                s_ref[tile] = st
                yh = _dot_nt(hsel_ref[...], (st * row(r_ref)).astype(BF16))
                y_ref[0, t, bb:bb + 1, cols] = jnp.concatenate([yh[0:1], yh[1:2]], axis=1)
        return carry

    lax.fori_loop(0, tc, step, 0)


def _wkv_scan(r, w, k, v, a, b, n_ctx):
    N, B, D = r.shape
    tc = SCAN_CHUNK
    wc = SCAN_COLS
    n_pairs = wc // LANES
    nc = N // tc
    ncc = n_ctx // tc

    def chunk(d, c):
        rev = jnp.where(c < ncc, ncc - 1 - c, nc - 1 - (c - ncc))
        return jnp.where(d == 0, c, rev)

    one = pl.BlockSpec((tc, B, wc), lambda d, g, c: (chunk(d, c), 0, g))
    two = pl.BlockSpec((1, tc, B, wc), lambda d, g, c: (d, chunk(d, c), 0, g))
    seg = np.arange(LANES) // HEAD_DIM
    vsel = np.zeros((2 * tc, LANES), np.float32)
    hsel = np.zeros((8, LANES), np.float32)
    for hh in range(2):
        vsel[hh * tc:(hh + 1) * tc, seg == hh] = 1.0
        hsel[hh, seg == hh] = 1.0
    const = lambda a_: pl.BlockSpec(a_.shape, lambda d, g, c: (0, 0))
    consts = [_seg_ones(), jnp.asarray(vsel, BF16), jnp.asarray(hsel, BF16)]
    return pl.pallas_call(
        _scan_kernel,
        grid=(2, D // wc, nc),
        in_specs=[one, two, two, one, one, two] + [const(a_) for a_ in consts],
        out_specs=two,
        out_shape=jax.ShapeDtypeStruct((2, N, B, D), F32),
        scratch_shapes=[pltpu.VMEM((B * n_pairs, HEAD_DIM, LANES), F32),
                        pltpu.VMEM((B * n_pairs, HEAD_DIM, 2 * tc), BF16)],
        compiler_params=_cparams(("arbitrary", "arbitrary", "arbitrary")),
    )(r, w, k, v, a, b, *consts)


def _rwkv_out_kernel(y0_ref, y1_ref, bonus_ref, g_ref, lnx_ref, ones_ref, w_ref, h_ref, mod_ref, lg_ref, lb_ref,
                     hn_ref, u_ref):
    y = y0_ref[0] + y1_ref[0]
    ym = _head_sum(y, ones_ref) * (1.0 / HEAD_DIM)
    yc = y - ym
    yv = _head_sum(yc * yc, ones_ref) * (1.0 / HEAD_DIM)
    yn = yc * lax.rsqrt(yv + LNX_EPS) * lnx_ref[0:1, :] + lnx_ref[1:2, :]
    x = ((yn + bonus_ref[...]) * g_ref[...]).astype(BF16)
    o = _dot(x, w_ref[...])
    z = DEEPNORM_ALPHA * h_ref[...] + mod_ref[0] * o
    hn = _layer_norm(z, lg_ref[...], lb_ref[...])
    hn_ref[...] = hn
    u_ref[...] = hn * (1.0 + mod_ref[2]) + mod_ref[1]


def _rwkv_out(y, bonus, g, lnx, w_out, h, mod_rows, ln_g, ln_b, row0):
    T, D = h.shape
    tm = ROW_TILE
    t0 = row0 // tm
    off = pl.BlockSpec((tm, D), lambda i: (i + t0, 0))
    out = pl.BlockSpec((tm, D), lambda i: (i, 0))
    full = lambda a: pl.BlockSpec(a.shape, lambda i: (0,) * a.ndim)
    vec = pl.BlockSpec((1, D), lambda i: (0, 0))
    ones = _seg_ones()
    w_b = w_out.astype(BF16)
    return pl.pallas_call(
        _rwkv_out_kernel,
        grid=((T - row0) // tm,),
        in_specs=[pl.BlockSpec((1, tm, D), lambda i: (0, i + t0, 0)),
                  pl.BlockSpec((1, tm, D), lambda i: (1, i + t0, 0)),
                  off, off, full(lnx), full(ones), full(w_b), off, full(mod_rows), vec, vec],
        out_specs=[out, out],
        out_shape=[jax.ShapeDtypeStruct((T - row0, D), F32)] * 2,
        compiler_params=_cparams(("parallel",)),
    )(y, y, bonus, g, lnx, ones, w_b, h, mod_rows, ln_g.reshape(1, D), ln_b.reshape(1, D))


def _shift_delta(u, n_ctx):
    def seg(x):
        xp = jnp.pad(x, ((1, 1), (0, 0), (0, 0)))
        return 0.5 * (xp[:-2] + xp[2:]) - x
    return jnp.concatenate([seg(u[:n_ctx]), seg(u[n_ctx:])], axis=0)


def kernel(x, c, ctx, c_ctx, ada_w, ada_b, post_ln_g, post_ln_b, att_w_in, att_w_out, att_sink, diff_lambda_vecs, diff_subln_g, rk_mu, rk_w_rkv, rk_w_out, rk_decay0, rk_decay1, rk_decay2, rk_iclr0, rk_iclr1, rk_iclr2, rk_gate1, rk_gate2, rk_k_k, rk_k_a, rk_r_k, rk_lnx, moe_router, moe_bias, moe_w_in, moe_w_out, moe_ws_in, moe_ws_out):
    B, S, D = x.shape
    L = ctx.shape[1]
    N = L + S
    tm = ROW_TILE
    assert L % tm == 0 and S % tm == 0 and L % SCAN_CHUNK == 0 and S % SCAN_CHUNK == 0
    assert tm % B == 0 and D % SCAN_COLS == 0

    rows = -(-(B + 1) // 8) * 8
    cvec = jnp.concatenate([c, c_ctx[None, :], jnp.zeros((rows - B - 1, D), F32)], axis=0)
    mods = [_mod_table(_ada_mod(cvec, ada_w[i], ada_b[i]), B, D) for i in range(DEPTH)]

    h0 = jnp.concatenate([ctx, x], axis=1)
    lam_init = 0.8 - 0.6 * math.exp(-0.3 * 0)
    qa, ka, va, qb, kb, vb = _attn_inproj(h0, mods[0], att_w_in[0], L)
    oa = _win_attn(qa, ka, va, att_sink[0], L)
    ob = _diff_attn(qb, kb, vb, diff_lambda_vecs[0], diff_subln_g[0], lam_init, L)
    h1, u1 = _mix_out([oa, ob], [att_w_out[0][:A_WIDTH], att_w_out[0][A_WIDTH:]], h0, mods[0],
                      post_ln_g[0, 0], post_ln_b[0, 0], L, 0)
    tiles_b, tiles_c = N // tm, L // tm
    gate0 = mods[0][:, :, 5].reshape(B * 2, 1, D)
    gate0_index = lambda i: (i // tiles_b) * 2 + jnp.minimum((i % tiles_b) // tiles_c, 1)
    h2 = _moe_layer(u1.reshape(B * N, D), h1.reshape(B * N, D), gate0, gate0_index, moe_router[0], moe_bias[0],
                    moe_w_in[0], moe_w_out[0], moe_ws_in[0], moe_ws_out[0],
                    post_ln_g[0, 1], post_ln_b[0, 1]).reshape(B, N, D)

    m_ctx, m_lat = mods[1][:, 0], mods[1][:, 1]
    h2_t = jnp.swapaxes(h2, 0, 1)
    is_lat = (jnp.arange(N) >= L)[:, None, None]
    u = h2_t * (1.0 + jnp.where(is_lat, m_lat[:, 1], m_ctx[:, 1])) + jnp.where(is_lat, m_lat[:, 0], m_ctx[:, 0])
    dx = _shift_delta(u, L)
    params = dict(mu=rk_mu[0], w_rkv=rk_w_rkv[0], gate1=rk_gate1[0], gate2=rk_gate2[0],
                  dec0=rk_decay0[0], dec1=rk_decay1[0], dec2=rk_decay2[0],
                  icl0=rk_iclr0[0], icl1=rk_iclr1[0], icl2=rk_iclr2[0],
                  k_k=rk_k_k[0], k_a=rk_k_a[0], r_k=rk_r_k[0])
    r, v, a, g, bonus, w2, k2, b2 = _rwkv_proj(u.reshape(N * B, D), dx.reshape(N * B, D), params)
    tmaj = lambda t: t.reshape(t.shape[:-2] + (N, B, D))
    y = _wkv_scan(tmaj(r), tmaj(w2), tmaj(k2), tmaj(v), tmaj(a), tmaj(b2), L)
    lat_rows = lambda j: jnp.tile(m_lat[:, j], (tm // B, 1))
    h3, u3 = _rwkv_out(y.reshape(2, N * B, D), bonus, g, rk_lnx[0], rk_w_out[0], h2_t.reshape(N * B, D),
                       jnp.stack([lat_rows(2), lat_rows(3), lat_rows(4)]),
                       post_ln_g[1, 0], post_ln_b[1, 0], L * B)
    out = _moe_layer(u3, h3, lat_rows(5)[None], lambda i: 0, moe_router[1], moe_bias[1],
                     moe_w_in[1], moe_w_out[1], moe_ws_in[1], moe_ws_out[1],
                     post_ln_g[1, 1], post_ln_b[1, 1])
    return jnp.swapaxes(out.reshape(S, B, D), 0, 1)
```

```python
import functools
import math

import numpy as np
import jax
import jax.numpy as jnp
from jax import lax
from jax.experimental import pallas as pl
from jax.experimental.pallas import tpu as pltpu

F32 = jnp.float32
BF16 = jnp.bfloat16
I32 = jnp.int32

HEAD_DIM = 64
GRID_W = 64
ROPE_AXIS_DIM = HEAD_DIM // 2
ROPE_THETA = 10000.0
Q_BLOCK = 128
A_Q_HEADS = 8
A_KV_HEADS = 2
A_GROUP = A_Q_HEADS // A_KV_HEADS
A_WIDTH = A_Q_HEADS * HEAD_DIM
A_KV_WIDTH = A_KV_HEADS * HEAD_DIM
B_HEADS = 4
B_V_DIM = 2 * HEAD_DIM
B_WIDTH = B_HEADS * B_V_DIM
LNX_EPS = 64e-5
N_EXPERTS = 256
TOP_K = 8
N_GROUPS = 8
TOPK_GROUPS = 4
ROUTED_SCALE = 2.5
MOE_BLOCK = 128
LN_EPS = 1e-5
SUBLN_EPS = 1e-5
NEG_INF = -1e30
DEPTH = 2
DEEPNORM_ALPHA = (2 * DEPTH) ** 0.25

LANES = 128
ROW_TILE = 256
PROJ_TILE = 128
SCAN_CHUNK = 64
SCAN_COLS = 512
SCAN_GROUP = 16
VMEM_LIMIT = 56 * 1024 * 1024


def _cparams(sem):
    return pltpu.CompilerParams(dimension_semantics=sem, vmem_limit_bytes=VMEM_LIMIT)


def _silu(x):
    return x * jax.nn.sigmoid(x)


def _layer_norm(z, g, b):
    mu = jnp.mean(z, -1, keepdims=True)
    zc = z - mu
    var = jnp.mean(zc * zc, -1, keepdims=True)
    return zc * lax.rsqrt(var + LN_EPS) * g + b


def _dot(a, b):
    return jnp.dot(a, b, preferred_element_type=F32)


def _dot_nt(a, b):
    return lax.dot_general(a, b, (((1,), (1,)), ((), ())), preferred_element_type=F32)


def _ada_kernel(c_ref, w_ref, b_ref, o_ref):
    c = c_ref[...]
    o_ref[...] = jnp.dot(_silu(c), w_ref[...], preferred_element_type=F32,
                         precision=lax.Precision.HIGHEST) + b_ref[...]


def _ada_mod(cvec, w, bias):
    R, D = cvec.shape
    n_out = w.shape[1]
    tn = 768
    return pl.pallas_call(
        _ada_kernel,
        grid=(n_out // tn,),
        in_specs=[pl.BlockSpec((R, D), lambda j: (0, 0)),
                  pl.BlockSpec((D, tn), lambda j: (0, j)),
                  pl.BlockSpec((1, tn), lambda j: (0, j))],
        out_specs=pl.BlockSpec((R, tn), lambda j: (0, j)),
        out_shape=jax.ShapeDtypeStruct((R, n_out), F32),
        compiler_params=_cparams(("arbitrary",)),
    )(cvec, w, bias.reshape(1, n_out))


def _mod_table(m, batch, d):
    m_lat = m[:batch].reshape(batch, 6, d)
    m_ctx = jnp.broadcast_to(m[batch].reshape(1, 6, d), (batch, 6, d))
    return jnp.stack([m_ctx, m_lat], axis=1)


def _mod_spec(d, ctx_tiles):
    return pl.BlockSpec((1, 1, 6, d), lambda b, i: (b, jnp.minimum(i // ctx_tiles, 1), 0, 0))


def _rope_tables(n_ctx, n_lat):
    rows = n_lat // GRID_W
    row = np.repeat(np.arange(rows), GRID_W).astype(np.float32)
    col = np.tile(np.arange(GRID_W), rows).astype(np.float32)
    inv = (ROPE_THETA ** (-np.arange(0, ROPE_AXIS_DIM, 2, dtype=np.float32) / ROPE_AXIS_DIM)).astype(np.float32)
    ar = row[:, None] * inv
    ac = col[:, None] * inv
    ang = np.concatenate([ar, ar, ac, ac], -1)
    cos = np.cos(ang).astype(np.float32)
    sin = np.sin(ang).astype(np.float32)
    lower = (np.arange(HEAD_DIM) % ROPE_AXIS_DIM) < (ROPE_AXIS_DIM // 2)
    sin_up = np.where(lower[None, :], -sin, 0.0)
    sin_dn = np.where(lower[None, :], 0.0, sin)

    def full(t, ctx_fill):
        t = np.concatenate([np.full((n_ctx, HEAD_DIM), ctx_fill, np.float32), t], 0)
        return jnp.asarray(np.tile(t, (1, LANES // HEAD_DIM)))

    return full(cos, 1.0), full(sin_up, 0.0), full(sin_dn, 0.0)


def _inproj_kernel(h_ref, mod_ref, w_ref, cos_ref, su_ref, sd_ref,
                   qa_ref, ka_ref, va_ref, qb_ref, kb_ref, vb_ref):
    h = h_ref[0]
    shift = mod_ref[0, 0, 0:1, :]
    scale = mod_ref[0, 0, 1:2, :]
    u = (h * (1.0 + scale) + shift).astype(BF16)
    y = _dot(u, w_ref[...])
    cos, s_up, s_dn = cos_ref[...], su_ref[...], sd_ref[...]
    q_scale = HEAD_DIM ** -0.5

    def rope(xc):
        half = ROPE_AXIS_DIM // 2
        return xc * cos + pltpu.roll(xc, LANES - half, 1) * s_up + pltpu.roll(xc, half, 1) * s_dn

    def emit(out_ref, col0, width, roped, mul):
        for j in range(width // LANES):
            xc = y[:, col0 + j * LANES: col0 + (j + 1) * LANES]
            if roped:
                xc = rope(xc)
            if mul != 1.0:
                xc = xc * mul
            out_ref[0, :, j * LANES:(j + 1) * LANES] = xc.astype(out_ref.dtype)

    c = 0
    emit(qa_ref, c, A_WIDTH, True, q_scale); c += A_WIDTH
    emit(ka_ref, c, A_KV_WIDTH, True, 1.0); c += A_KV_WIDTH
    emit(va_ref, c, A_KV_WIDTH, False, 1.0); c += A_KV_WIDTH
    emit(qb_ref, c, B_WIDTH, True, q_scale); c += B_WIDTH
    emit(kb_ref, c, B_WIDTH, True, 1.0); c += B_WIDTH
    emit(vb_ref, c, B_WIDTH, False, 1.0)


def _attn_inproj(h, mod, w_in, n_ctx):
    B, N, D = h.shape
    tm = ROW_TILE
    cos, s_up, s_dn = _rope_tables(n_ctx, N - n_ctx)
    widths = (A_WIDTH, A_KV_WIDTH, A_KV_WIDTH, B_WIDTH, B_WIDTH, B_WIDTH)
    tab_spec = pl.BlockSpec((tm, LANES), lambda b, i: (i, 0))
    return pl.pallas_call(
        _inproj_kernel,
        grid=(B, N // tm),
        in_specs=[pl.BlockSpec((1, tm, D), lambda b, i: (b, i, 0)),
                  _mod_spec(D, n_ctx // tm),
                  pl.BlockSpec(w_in.shape, lambda b, i: (0, 0)),
                  tab_spec, tab_spec, tab_spec],
        out_specs=[pl.BlockSpec((1, tm, w), lambda b, i: (b, i, 0)) for w in widths],
        out_shape=[jax.ShapeDtypeStruct((B, N, w), BF16) for w in widths],
        compiler_params=_cparams(("arbitrary", "arbitrary")),
    )(h, mod, w_in.astype(BF16), cos, s_up, s_dn)


def _win_attn_kernel(n_ctx_blocks, n_blocks, q_ref, kc_ref, vc_ref, kl_ref, km_ref, kr_ref,
                     vl_ref, vm_ref, vr_ref, sink_ref, o_ref):
    j = pl.program_id(1)
    is_lat = j >= n_ctx_blocks
    qb = Q_BLOCK
    n_c = kc_ref.shape[1]
    rows = A_GROUP * qb
    n_keys = n_c + 3 * qb
    far = 1 << 20
    r_idx = lax.broadcasted_iota(I32, (rows, n_keys), 0) % qb
    cw = lax.broadcasted_iota(I32, (rows, n_keys), 1) - n_c
    off_l = jnp.where(jnp.logical_and(is_lat, j > n_ctx_blocks), 0, far)
    end_m = jnp.where(is_lat, 2 * qb, qb)
    off_r = jnp.where(jnp.logical_and(is_lat, j < n_blocks - 1), 0, far)
    valid = ((cw < 0)
             | ((cw >= 0) & (cw < qb) & (cw >= r_idx + off_l))
             | ((cw >= qb) & (cw < end_m))
             | ((cw >= 2 * qb) & (cw - 2 * qb + off_r <= r_idx)))
    outs = []
    for kv in range(A_KV_HEADS):
        cols = slice(kv * HEAD_DIM, (kv + 1) * HEAD_DIM)
        k_all = jnp.concatenate([kc_ref[0, :, cols], kl_ref[0, :, cols], km_ref[0, :, cols],
                                 kr_ref[0, :, cols]], axis=0)
        v_all = jnp.concatenate([vc_ref[0, :, cols], vl_ref[0, :, cols], vm_ref[0, :, cols],
                                 vr_ref[0, :, cols]], axis=0)
        q0 = kv * A_GROUP
        q = jnp.concatenate([q_ref[0, :, (q0 + g) * HEAD_DIM:(q0 + g + 1) * HEAD_DIM]
                             for g in range(A_GROUP)], axis=0)
        sink = jnp.concatenate([jnp.broadcast_to(sink_ref[q0 + g:q0 + g + 1, 0:1], (qb, 1))
                                for g in range(A_GROUP)], axis=0)
        s = jnp.where(valid, _dot_nt(q, k_all), NEG_INF)
        m = jnp.maximum(jnp.max(s, -1, keepdims=True), sink)
        e = jnp.exp(s - m)
        denom = jnp.sum(e, -1, keepdims=True) + jnp.exp(sink - m)
        p = (e / denom).astype(BF16)
        o = _dot(p, v_all)
        outs += [o[g * qb:(g + 1) * qb] for g in range(A_GROUP)]
    for j2 in range(A_Q_HEADS // 2):
        pair = jnp.concatenate([outs[2 * j2], outs[2 * j2 + 1]], axis=1)
        o_ref[0, :, j2 * LANES:(j2 + 1) * LANES] = pair.astype(o_ref.dtype)


def _win_attn(qa, ka, va, sink, n_ctx):
    B, N, _ = qa.shape
    qb = Q_BLOCK
    nb = N // qb
    ncb = n_ctx // qb
    sink_pad = jnp.broadcast_to(sink.reshape(A_Q_HEADS, 1).astype(F32), (A_Q_HEADS, LANES))

    def left(b, j):
        return (b, jnp.clip(j - 1, ncb, nb - 1), 0)

    def mid(b, j):
        return (b, jnp.clip(j, ncb, nb - 1), 0)

    def right(b, j):
        return (b, jnp.clip(j + 1, ncb, nb - 1), 0)

    kv_blk = lambda im: pl.BlockSpec((1, qb, A_KV_WIDTH), im)
    ctx_blk = pl.BlockSpec((1, n_ctx, A_KV_WIDTH), lambda b, j: (b, 0, 0))
    return pl.pallas_call(
        functools.partial(_win_attn_kernel, ncb, nb),
        grid=(B, nb),
        in_specs=[pl.BlockSpec((1, qb, A_WIDTH), lambda b, j: (b, j, 0)),
                  ctx_blk, ctx_blk,
                  kv_blk(left), kv_blk(mid), kv_blk(right),
                  kv_blk(left), kv_blk(mid), kv_blk(right),
                  pl.BlockSpec((A_Q_HEADS, LANES), lambda b, j: (0, 0))],
        out_specs=pl.BlockSpec((1, qb, A_WIDTH), lambda b, j: (b, j, 0)),
        out_shape=jax.ShapeDtypeStruct((B, N, A_WIDTH), BF16),
        compiler_params=_cparams(("arbitrary", "arbitrary")),
    )(qa, ka, va, ka, ka, ka, va, va, va, sink_pad)


def _diff_attn_kernel(n_ctx, lam_init, q_ref, k_ref, v_ref, lv_ref, g_ref, o_ref):
    j = pl.program_id(1)
    lv = lv_ref[...]
    lam = (jnp.exp(jnp.sum(lv[0:1] * lv[1:2], -1, keepdims=True))
           - jnp.exp(jnp.sum(lv[2:3] * lv[3:4], -1, keepdims=True)) + lam_init)
    gain = g_ref[...] * (1.0 - lam_init)

    def run(n_keys):
        for hd in range(B_HEADS):
            parts = []
            for mm in range(2):
                c0 = (hd * 2 + mm) * HEAD_DIM
                q = q_ref[0, :, c0:c0 + HEAD_DIM]
                k = k_ref[0, :n_keys, c0:c0 + HEAD_DIM]
                s = _dot_nt(q, k)
                e = jnp.exp(s - jnp.max(s, -1, keepdims=True))
                parts.append((e, jnp.sum(e, -1, keepdims=True)))
            (e0, l0), (e1, l1) = parts
            a = (e0 * (1.0 / l0) - e1 * (lam / l1)).astype(BF16)
            o = _dot(a, v_ref[0, :n_keys, hd * B_V_DIM:(hd + 1) * B_V_DIM])
            o = o * lax.rsqrt(jnp.mean(o * o, -1, keepdims=True) + SUBLN_EPS) * gain
            o_ref[0, :, hd * B_V_DIM:(hd + 1) * B_V_DIM] = o.astype(o_ref.dtype)

    @pl.when(j == 0)
    def _():
        run(n_ctx)

    @pl.when(j > 0)
    def _():
        run(k_ref.shape[1])


def _diff_attn(qb, kb, vb, lam_vecs, subln_g, lam_init, n_ctx):
    B, N, _ = qb.shape
    tq = n_ctx
    return pl.pallas_call(
        functools.partial(_diff_attn_kernel, n_ctx, lam_init),
        grid=(B, N // tq),
        in_specs=[pl.BlockSpec((1, tq, B_WIDTH), lambda b, j: (b, j, 0)),
                  pl.BlockSpec((1, N, B_WIDTH), lambda b, j: (b, 0, 0)),
                  pl.BlockSpec((1, N, B_WIDTH), lambda b, j: (b, 0, 0)),
                  pl.BlockSpec((4, HEAD_DIM), lambda b, j: (0, 0)),
                  pl.BlockSpec((1, B_V_DIM), lambda b, j: (0, 0))],
        out_specs=pl.BlockSpec((1, tq, B_WIDTH), lambda b, j: (b, j, 0)),
        out_shape=jax.ShapeDtypeStruct((B, N, B_WIDTH), BF16),
        compiler_params=_cparams(("arbitrary", "arbitrary")),
    )(qb, kb, vb, lam_vecs.astype(F32), subln_g.reshape(1, B_V_DIM).astype(F32))


def _mix_out_kernel(n_in, *refs):
    xs = refs[:n_in]
    ws = refs[n_in:2 * n_in]
    h_ref, mod_ref, g_ref, b_ref, hn_ref, u_ref = refs[2 * n_in:]
    o = _dot(xs[0][0], ws[0][...])
    for x_ref, w_ref in zip(xs[1:], ws[1:]):
        o = o + _dot(x_ref[0], w_ref[...])
    z = DEEPNORM_ALPHA * h_ref[0] + mod_ref[0, 0, 2:3, :] * o
    hn = _layer_norm(z, g_ref[...], b_ref[...])
    hn_ref[0] = hn
    u_ref[0] = hn * (1.0 + mod_ref[0, 0, 4:5, :]) + mod_ref[0, 0, 3:4, :]


def _mix_out(xs, ws, h, mod, ln_g, ln_b, n_ctx, row0):
    B, N, D = h.shape
    tm = ROW_TILE
    t0 = row0 // tm
    n_out = N - row0
    row_spec = lambda w: pl.BlockSpec((1, tm, w), lambda b, i: (b, i + t0, 0))
    out_spec = pl.BlockSpec((1, tm, D), lambda b, i: (b, i, 0))
    vec_spec = pl.BlockSpec((1, D), lambda b, i: (0, 0))
    return pl.pallas_call(
        functools.partial(_mix_out_kernel, len(xs)),
        grid=(B, n_out // tm),
        in_specs=([row_spec(x.shape[-1]) for x in xs]
                  + [pl.BlockSpec(w.shape, lambda b, i: (0, 0)) for w in ws]
                  + [row_spec(D),
                     pl.BlockSpec((1, 1, 6, D), lambda b, i: (b, jnp.minimum((i + t0) // (n_ctx // tm), 1), 0, 0)),
                     vec_spec, vec_spec]),
        out_specs=[out_spec, out_spec],
        out_shape=[jax.ShapeDtypeStruct((B, n_out, D), F32)] * 2,
        compiler_params=_cparams(("arbitrary", "arbitrary")),
    )(*xs, *[w.astype(BF16) for w in ws], h, mod, ln_g.reshape(1, D), ln_b.reshape(1, D))


def _router_kernel(u_ref, rt_ref, bias_ref, tri_ref, e_ref, gw_ref, rank_ref, cnt_ref, carry_ref):
    i = pl.program_id(0)

    @pl.when(i == 0)
    def _():
        carry_ref[...] = jnp.zeros_like(carry_ref)

    tm = u_ref.shape[0]
    per_group = N_EXPERTS // N_GROUPS
    neg = -jnp.inf
    logits = _dot_nt(rt_ref[...], u_ref[...].astype(BF16))
    scores = jax.nn.sigmoid(logits)
    sel = scores + bias_ref[...]
    io_in = lax.broadcasted_iota(I32, (per_group, tm), 0)
    grp_rows = []
    for gi in range(N_GROUPS):
        sg = sel[gi * per_group:(gi + 1) * per_group]
        m1 = jnp.max(sg, axis=0, keepdims=True)
        i1 = jnp.min(jnp.where(sg == m1, io_in, per_group), axis=0, keepdims=True)
        m2 = jnp.max(jnp.where(io_in == i1, neg, sg), axis=0, keepdims=True)
        grp_rows.append(m1 + m2)
    grp = jnp.concatenate(grp_rows, axis=0)
    io_g = lax.broadcasted_iota(I32, grp.shape, 0)
    g_sel = jnp.zeros(grp.shape, F32)
    for _ in range(TOPK_GROUPS):
        m = jnp.max(grp, axis=0, keepdims=True)
        hit = io_g == jnp.min(jnp.where(grp == m, io_g, N_GROUPS), axis=0, keepdims=True)
        g_sel = jnp.where(hit, 1.0, g_sel)
        grp = jnp.where(hit, neg, grp)
    selm = jnp.concatenate(
        [jnp.where(g_sel[gi:gi + 1] > 0.5, sel[gi * per_group:(gi + 1) * per_group], NEG_INF)
         for gi in range(N_GROUPS)], axis=0)
    io_e = lax.broadcasted_iota(I32, selm.shape, 0)
    chosen_f = jnp.zeros(selm.shape, F32)
    idx, gws = [], []
    for _ in range(TOP_K):
        m = jnp.max(selm, axis=0, keepdims=True)
        ik = jnp.min(jnp.where(selm == m, io_e, N_EXPERTS), axis=0, keepdims=True)
        hit = io_e == ik
        idx.append(ik)
        gws.append(jnp.sum(jnp.where(hit, scores, 0.0), axis=0, keepdims=True))
        chosen_f = jnp.where(hit, 1.0, chosen_f)
        selm = jnp.where(hit, neg, selm)
    gw = jnp.concatenate(gws, axis=0)
    gw_ref[...] = gw / jnp.sum(gw, axis=0, keepdims=True) * ROUTED_SCALE
    e_ref[...] = jnp.concatenate(idx, axis=0)
    before = _dot(chosen_f.astype(BF16), tri_ref[...]) + carry_ref[...]
    ranks = [jnp.sum(jnp.where(io_e == ik, before, 0.0), axis=0, keepdims=True) for ik in idx]
    rank_ref[...] = jnp.concatenate(ranks, axis=0).astype(I32)
    carry_ref[...] = carry_ref[...] + jnp.sum(chosen_f, axis=1, keepdims=True)
    cnt_ref[...] = carry_ref[...].astype(I32)


def _router(u, router, bias):
    T, D = u.shape
    tm = ROW_TILE
    tri = jnp.asarray(np.triu(np.ones((tm, tm), np.float32), 1), BF16)
    tok_spec = pl.BlockSpec((TOP_K, tm), lambda i: (0, i))
    return pl.pallas_call(
        _router_kernel,
        grid=(T // tm,),
        in_specs=[pl.BlockSpec((tm, D), lambda i: (i, 0)),
                  pl.BlockSpec((N_EXPERTS, D), lambda i: (0, 0)),
                  pl.BlockSpec((N_EXPERTS, 1), lambda i: (0, 0)),
                  pl.BlockSpec((tm, tm), lambda i: (0, 0))],
        out_specs=[tok_spec, tok_spec, tok_spec, pl.BlockSpec((N_EXPERTS, 1), lambda i: (0, 0))],
        out_shape=[jax.ShapeDtypeStruct((TOP_K, T), I32), jax.ShapeDtypeStruct((TOP_K, T), F32),
                   jax.ShapeDtypeStruct((TOP_K, T), I32), jax.ShapeDtypeStruct((N_EXPERTS, 1), I32)],
        scratch_shapes=[pltpu.VMEM((N_EXPERTS, 1), F32)],
        compiler_params=_cparams(("arbitrary",)),
    )(u, router.T.astype(BF16), bias.reshape(N_EXPERTS, 1).astype(F32), tri)


def _dispatch_kernel(dest_ref, x_ref, xs_in, xs_hbm, sem):
    del xs_in
    tm = dest_ref.shape[1]

    def row_copy(t, k):
        return pltpu.make_async_copy(x_ref.at[pl.ds(t, 1)], xs_hbm.at[pl.ds(dest_ref[k, t], 1)], sem)

    def start(t, carry):
        for k in range(TOP_K):
            row_copy(t, k).start()
        return carry

    def wait(t, carry):
        for k in range(TOP_K):
            row_copy(t, k).wait()
        return carry

    lax.fori_loop(0, tm, start, 0)
    lax.fori_loop(0, tm, wait, 0)


def _dispatch(x, dest, n_rows):
    T, D = x.shape
    tm = ROW_TILE
    xs0 = jnp.zeros((n_rows, D), x.dtype)
    return pl.pallas_call(
        _dispatch_kernel,
        grid=(T // tm,),
        in_specs=[pl.BlockSpec((TOP_K, tm), lambda i: (0, i), memory_space=pltpu.SMEM),
                  pl.BlockSpec((tm, D), lambda i: (i, 0)),
                  pl.BlockSpec(memory_space=pl.ANY)],
        out_specs=pl.BlockSpec(memory_space=pl.ANY),
        out_shape=jax.ShapeDtypeStruct((n_rows, D), x.dtype),
        scratch_shapes=[pltpu.SemaphoreType.DMA(())],
        input_output_aliases={2: 0},
        compiler_params=_cparams(("arbitrary",)),
    )(dest, x, xs0)


def _expert_kernel(be_ref, nu_ref, x_ref, wi_ref, wo_ref, y_ref):
    i = pl.program_id(0)

    @pl.when(i < nu_ref[0])
    def _():
        ff = wo_ref.shape[2]
        hcat = _dot(x_ref[...].astype(BF16), wi_ref[0, 0].astype(BF16))
        act = (_silu(hcat[:, :ff]) * hcat[:, ff:]).astype(BF16)
        y_ref[...] = _dot(act, wo_ref[0, 0].astype(BF16))

    @pl.when(i >= nu_ref[0])
    def _():
        y_ref[...] = jnp.zeros_like(y_ref)


def _experts(xs, block_exp, n_used, w_in, w_out, layer):
    P, D = xs.shape
    ff2 = w_in.shape[-1]
    nblk = P // MOE_BLOCK
    row_map = lambda i, be, nu: (i, 0)
    return pl.pallas_call(
        _expert_kernel,
        grid_spec=pltpu.PrefetchScalarGridSpec(
            num_scalar_prefetch=2,
            grid=(nblk,),
            in_specs=[pl.BlockSpec((MOE_BLOCK, D), row_map),
                      pl.BlockSpec((1, 1, D, ff2), lambda i, be, nu: (layer, be[i], 0, 0)),
                      pl.BlockSpec((1, 1, ff2 // 2, D), lambda i, be, nu: (layer, be[i], 0, 0))],
            out_specs=pl.BlockSpec((MOE_BLOCK, D), row_map)),
        out_shape=jax.ShapeDtypeStruct((P, D), F32),
        compiler_params=_cparams(("arbitrary",)),
    )(block_exp, n_used, xs, w_in, w_out)


def _combine_kernel(dest_ref, y_hbm, gw_ref, u_ref, wsi_ref, wso_ref, h_ref, mod_ref, g_ref, b_ref,
                    o_ref, buf, sem):
    tm = u_ref.shape[0]

    def row_copy(t, k):
        return pltpu.make_async_copy(y_hbm.at[pl.ds(dest_ref[k, t], 1)], buf.at[k, pl.ds(t, 1)], sem)

    def start(t, carry):
        for k in range(TOP_K):
            row_copy(t, k).start()
        return carry

    def wait(t, carry):
        for k in range(TOP_K):
            row_copy(t, k).wait()
        return carry

    lax.fori_loop(0, tm, start, 0)
    ff = wso_ref.shape[0]
    hcat = _dot(u_ref[...].astype(BF16), wsi_ref[...])
    shared = _dot((_silu(hcat[:, :ff]) * hcat[:, ff:]).astype(BF16), wso_ref[...])
    lax.fori_loop(0, tm, wait, 0)
    routed = buf[0] * gw_ref[:, 0:1]
    for k in range(1, TOP_K):
        routed = routed + buf[k] * gw_ref[:, k:k + 1]
    z = DEEPNORM_ALPHA * h_ref[...] + mod_ref[0] * (routed + shared)
    o_ref[...] = _layer_norm(z, g_ref[...], b_ref[...])


def _combine(y, dest, gw_t, u, ws_in, ws_out, h, gate, gate_index, ln_g, ln_b):
    T, D = u.shape
    tm = ROW_TILE
    vec_spec = pl.BlockSpec((1, D), lambda i: (0, 0))
    row_spec = pl.BlockSpec((tm, D), lambda i: (i, 0))
    return pl.pallas_call(
        _combine_kernel,
        grid=(T // tm,),
        in_specs=[pl.BlockSpec((TOP_K, tm), lambda i: (0, i), memory_space=pltpu.SMEM),
                  pl.BlockSpec(memory_space=pl.ANY),
                  pl.BlockSpec((tm, TOP_K), lambda i: (i, 0)),
                  row_spec,
                  pl.BlockSpec(ws_in.shape, lambda i: (0, 0)),
                  pl.BlockSpec(ws_out.shape, lambda i: (0, 0)),
                  row_spec,
                  pl.BlockSpec((1,) + gate.shape[1:], lambda i: (gate_index(i), 0, 0)),
                  vec_spec, vec_spec],
        out_specs=row_spec,
        out_shape=jax.ShapeDtypeStruct((T, D), F32),
        scratch_shapes=[pltpu.VMEM((TOP_K, tm, D), F32), pltpu.SemaphoreType.DMA(())],
        compiler_params=_cparams(("arbitrary",)),
    )(dest, y, gw_t, u, ws_in.astype(BF16), ws_out.astype(BF16), h, gate,
      ln_g.reshape(1, D), ln_b.reshape(1, D))


def _slots_kernel(e_ref, rank_ref, start_ref, dest_ref):
    io_e = lax.broadcasted_iota(I32, (N_EXPERTS, e_ref.shape[1]), 0)
    rows = [jnp.sum(jnp.where(io_e == e_ref[k:k + 1, :], start_ref[...], 0), axis=0, keepdims=True)
            for k in range(TOP_K)]
    dest_ref[...] = jnp.concatenate(rows, axis=0) + rank_ref[...]


def _slots(eidx, rank, pstart):
    T = eidx.shape[1]
    tm = ROW_TILE
    tok_spec = pl.BlockSpec((TOP_K, tm), lambda i: (0, i))
    return pl.pallas_call(
        _slots_kernel,
        grid=(T // tm,),
        in_specs=[tok_spec, tok_spec, pl.BlockSpec((N_EXPERTS, 1), lambda i: (0, 0))],
        out_specs=tok_spec,
        out_shape=jax.ShapeDtypeStruct((TOP_K, T), I32),
        compiler_params=_cparams(("arbitrary",)),
    )(eidx, rank, pstart.reshape(N_EXPERTS, 1))


def _moe_layer(u, h, gate, gate_index, router, bias, w_in, w_out, ws_in, ws_out, ln_g, ln_b, layer):
    T, D = u.shape
    eidx, gw, rank, counts = _router(u, router, bias)
    counts = counts[:, 0]
    padded = (counts + MOE_BLOCK - 1) // MOE_BLOCK * MOE_BLOCK
    pend = jnp.cumsum(padded)
    pstart = (pend - padded).astype(I32)
    dest = _slots(eidx, rank, pstart)
    n_blocks = -(-(T * TOP_K + N_EXPERTS * (MOE_BLOCK - 1)) // MOE_BLOCK)
    block_exp = jnp.minimum(jnp.searchsorted(pend, jnp.arange(n_blocks, dtype=I32) * MOE_BLOCK, side='right'),
                            N_EXPERTS - 1).astype(I32)
    n_used = (pend[-1:] // MOE_BLOCK).astype(I32)
    xs = _dispatch(u, dest, n_blocks * MOE_BLOCK)
    y = _experts(xs, block_exp, n_used, w_in, w_out, layer)
    return _combine(y, dest, gw.T, u, ws_in, ws_out, h, gate, gate_index, ln_g, ln_b)


def _seg_ones(width=LANES):
    idx = np.arange(width) // HEAD_DIM
    return jnp.asarray((idx[:, None] == idx[None, :]).astype(np.float32), BF16)


def _head_sum(x, ones_ref):
    outs = []
    for j in range(x.shape[1] // LANES):
        xc = x[:, j * LANES:(j + 1) * LANES]
        hi = xc.astype(BF16)
        lo = (xc - hi.astype(F32)).astype(BF16)
        outs.append(_dot(hi, ones_ref[...]) + _dot(lo, ones_ref[...]))
    return jnp.concatenate(outs, axis=1)


def _rwkv_proj_kernel(u_ref, dx_ref, mu_ref, wrkv_ref, g1_ref, g2_ref, d1_ref, d2_ref, d0_ref,
                      i1_ref, i2_ref, i0_ref, kk_ref, ka_ref, rk_ref, ones_ref,
                      r_ref, v_ref, a_ref, g_ref, bonus_ref, w_ref, k_ref, b_ref):
    u = u_ref[...]
    dx = dx_ref[...]
    mix = lambda m: (u + dx * mu_ref[m:m + 1, :])
    xr, xw, xk, xv, xa, xg = [mix(m) for m in range(6)]
    r = _dot(xr.astype(BF16), wrkv_ref[0])
    k = _dot(xk.astype(BF16), wrkv_ref[1])
    v = _dot(xv.astype(BF16), wrkv_ref[2])
    g = _dot(jax.nn.sigmoid(_dot(xg.astype(BF16), g1_ref[...])).astype(BF16), g2_ref[...])
    kk = k * kk_ref[...]
    kk = kk * lax.rsqrt(jnp.maximum(_head_sum(kk * kk, ones_ref), 1e-24))
    r_ref[...] = r
    v_ref[...] = v
    a_ref[...] = -kk
    g_ref[...] = g
    k_sum = None
    xw_b = xw.astype(BF16)
    xa_b = xa.astype(BF16)
    for d in range(2):
        lw = d0_ref[d:d + 1, :] + _dot(jnp.tanh(_dot(xw_b, d1_ref[d])).astype(BF16), d2_ref[d])
        softplus = jnp.maximum(-lw, 0.0) + jnp.log(1.0 + jnp.exp(-jnp.abs(lw)))
        logw = -softplus - 0.5
        w_ref[d] = jnp.exp(-jnp.exp(logw))
        eta = jax.nn.sigmoid(i0_ref[d:d + 1, :] + _dot(_dot(xa_b, i1_ref[d]).astype(BF16), i2_ref[d]))
        k_d = k * (1.0 + (eta - 1.0) * ka_ref[...])
        k_ref[d] = k_d
        b_ref[d] = kk * eta
        k_sum = k_d if k_sum is None else k_sum + k_d
    bonus_ref[...] = _head_sum(r * k_sum * rk_ref[...], ones_ref) * v


def _rwkv_proj(u, dx, p):
    T, D = u.shape
    tm = PROJ_TILE
    row = pl.BlockSpec((tm, D), lambda i: (i, 0))
    row2 = pl.BlockSpec((2, tm, D), lambda i: (0, i, 0))
    full = lambda a: pl.BlockSpec(a.shape, lambda i: (0,) * a.ndim)
    bf = lambda a: a.astype(BF16)
    consts = [p['mu'], bf(p['w_rkv']), bf(p['gate1']), bf(p['gate2']), bf(p['dec1']), bf(p['dec2']), p['dec0'],
              bf(p['icl1']), bf(p['icl2']), p['icl0'], p['k_k'].reshape(1, D), p['k_a'].reshape(1, D),
              p['r_k'].reshape(1, D), _seg_ones()]
    one = jax.ShapeDtypeStruct((T, D), F32)
    two = jax.ShapeDtypeStruct((2, T, D), F32)
    return pl.pallas_call(
        _rwkv_proj_kernel,
        grid=(T // tm,),
        in_specs=[row, row] + [full(a) for a in consts],
        out_specs=[row, row, row, row, row, row2, row2, row2],
        out_shape=[one, one, one, one, one, two, two, two],
        compiler_params=_cparams(("arbitrary",)),
    )(u, dx, *consts)


def _scan_kernel(r_ref, w_ref, k_ref, v_ref, a_ref, b_ref, ones_ref, vsel_ref, hsel_ref,
                 y_ref, s_ref, vt_ref):
    d = pl.program_id(0)
    c = pl.program_id(2)
    tc, nb = r_ref.shape[0], r_ref.shape[1]
    n_pairs = r_ref.shape[2] // LANES

    @pl.when(c == 0)
    def _():
        s_ref[...] = jnp.zeros_like(s_ref)

    for bb in range(nb):
        for hp in range(n_pairs):
            vt = v_ref[:, bb, hp * LANES:(hp + 1) * LANES].T
            vt_ref[bb * n_pairs + hp] = jnp.concatenate([vt[:HEAD_DIM], vt[HEAD_DIM:]], axis=1).astype(BF16)

    lane2 = lax.broadcasted_iota(I32, (HEAD_DIM, 2 * tc), 1) % tc
    low_half = lax.broadcasted_iota(I32, (1, LANES), 1) < HEAD_DIM
    tiles = [(bb, hp) for bb in range(nb) for hp in range(n_pairs)]
    groups = [tiles[i:i + SCAN_GROUP] for i in range(0, len(tiles), SCAN_GROUP)]

    def step(s_i, carry):
        t = jnp.where(d == 0, s_i, tc - 1 - s_i)
        at_t = lane2 == t
        for grp in groups:
            def stack(get):
                return jnp.concatenate([jnp.broadcast_to(get(bb, slice(hp * LANES, (hp + 1) * LANES)),
                                                         (HEAD_DIM, LANES)) for bb, hp in grp], axis=0)
            one = lambda ref: stack(lambda bb, cols: ref[t, bb:bb + 1, cols])
            two = lambda ref: stack(lambda bb, cols: ref[0, t, bb:bb + 1, cols])
            st = jnp.concatenate([s_ref[bb * n_pairs + hp] for bb, hp in grp], axis=0)
            sa = _dot((st * one(a_ref)).astype(BF16), ones_ref[...])
            vm = jnp.concatenate([jnp.where(at_t, vt_ref[bb * n_pairs + hp], jnp.zeros((), BF16))
                                  for bb, hp in grp], axis=0)
            vcol = _dot(vm, vsel_ref[...])
            st = st * two(w_ref) + sa * two(b_ref) + vcol * two(k_ref)
            for j, (bb, hp) in enumerate(grp):
                s_ref[bb * n_pairs + hp] = st[j * HEAD_DIM:(j + 1) * HEAD_DIM]
            yh = _dot_nt(hsel_ref[...], (st * one(r_ref)).astype(BF16))
            width = yh.shape[1]
            y_up = pltpu.roll(yh, HEAD_DIM, 1)
            y_dn = pltpu.roll(yh, width - HEAD_DIM, 1)
            for j in range(0, len(grp), 2):
                lanes = slice(j * HEAD_DIM, j * HEAD_DIM + LANES)
                (b0, p0), (b1, p1) = grp[j], grp[j + 1]
                y_ref[0, t, b0:b0 + 1, p0 * LANES:(p0 + 1) * LANES] = jnp.where(
                    low_half, yh[0:1, lanes], y_up[1:2, lanes])
                y_ref[0, t, b1:b1 + 1, p1 * LANES:(p1 + 1) * LANES] = jnp.where(
                    low_half, y_dn[0:1, lanes], yh[1:2, lanes])
        return carry

    lax.fori_loop(0, tc, step, 0)


def _wkv_scan(r, w, k, v, a, b, n_ctx):
    N, B, D = r.shape
    tc = SCAN_CHUNK
    wc = SCAN_COLS
    n_pairs = wc // LANES
    nc = N // tc
    ncc = n_ctx // tc

    def chunk(d, c):
        rev = jnp.where(c < ncc, ncc - 1 - c, nc - 1 - (c - ncc))
        return jnp.where(d == 0, c, rev)

    one = pl.BlockSpec((tc, B, wc), lambda d, g, c: (chunk(d, c), 0, g))
    two = pl.BlockSpec((1, tc, B, wc), lambda d, g, c: (d, chunk(d, c), 0, g))
    seg = np.arange(LANES) // HEAD_DIM
    vsel = np.zeros((2 * tc, LANES), np.float32)
    hsel = np.zeros((8, LANES), np.float32)
    for hh in range(2):
        vsel[hh * tc:(hh + 1) * tc, seg == hh] = 1.0
        hsel[hh, seg == hh] = 1.0
    const = lambda a_: pl.BlockSpec(a_.shape, lambda d, g, c: (0, 0))
    consts = [_seg_ones(), jnp.asarray(vsel, BF16), jnp.asarray(hsel, BF16)]
    return pl.pallas_call(
        _scan_kernel,
        grid=(2, D // wc, nc),
        in_specs=[one, two, two, one, one, two] + [const(a_) for a_ in consts],
        out_specs=two,
        out_shape=jax.ShapeDtypeStruct((2, N, B, D), F32),
        scratch_shapes=[pltpu.VMEM((B * n_pairs, HEAD_DIM, LANES), F32),
                        pltpu.VMEM((B * n_pairs, HEAD_DIM, 2 * tc), BF16)],
        compiler_params=_cparams(("arbitrary", "arbitrary", "arbitrary")),
    )(r, w, k, v, a, b, *consts)


def _rwkv_out_kernel(y0_ref, y1_ref, bonus_ref, g_ref, lnx_ref, ones_ref, w_ref, h_ref, mod_ref, lg_ref, lb_ref,
                     hn_ref, u_ref):
    y = y0_ref[0] + y1_ref[0]
    ym = _head_sum(y, ones_ref) * (1.0 / HEAD_DIM)
    yc = y - ym
    yv = _head_sum(yc * yc, ones_ref) * (1.0 / HEAD_DIM)
    yn = yc * lax.rsqrt(yv + LNX_EPS) * lnx_ref[0:1, :] + lnx_ref[1:2, :]
    x = ((yn + bonus_ref[...]) * g_ref[...]).astype(BF16)
    o = _dot(x, w_ref[...])
    z = DEEPNORM_ALPHA * h_ref[...] + mod_ref[0] * o
    hn = _layer_norm(z, lg_ref[...], lb_ref[...])
    hn_ref[...] = hn
    u_ref[...] = hn * (1.0 + mod_ref[2]) + mod_ref[1]


def _rwkv_out(y, bonus, g, lnx, w_out, h, mod_rows, ln_g, ln_b, row0):
    T, D = h.shape
    tm = ROW_TILE
    t0 = row0 // tm
    off = pl.BlockSpec((tm, D), lambda i: (i + t0, 0))
    out = pl.BlockSpec((tm, D), lambda i: (i, 0))
    full = lambda a: pl.BlockSpec(a.shape, lambda i: (0,) * a.ndim)
    vec = pl.BlockSpec((1, D), lambda i: (0, 0))
    ones = _seg_ones()
    w_b = w_out.astype(BF16)
    return pl.pallas_call(
        _rwkv_out_kernel,
        grid=((T - row0) // tm,),
        in_specs=[pl.BlockSpec((1, tm, D), lambda i: (0, i + t0, 0)),
                  pl.BlockSpec((1, tm, D), lambda i: (1, i + t0, 0)),
                  off, off, full(lnx), full(ones), full(w_b), off, full(mod_rows), vec, vec],
        out_specs=[out, out],
        out_shape=[jax.ShapeDtypeStruct((T - row0, D), F32)] * 2,
        compiler_params=_cparams(("arbitrary",)),
    )(y, y, bonus, g, lnx, ones, w_b, h, mod_rows, ln_g.reshape(1, D), ln_b.reshape(1, D))


def _shift_delta(u, n_ctx):
    def seg(x):
        xp = jnp.pad(x, ((1, 1), (0, 0), (0, 0)))
        return 0.5 * (xp[:-2] + xp[2:]) - x
    return jnp.concatenate([seg(u[:n_ctx]), seg(u[n_ctx:])], axis=0)


def kernel(x, c, ctx, c_ctx, ada_w, ada_b, post_ln_g, post_ln_b, att_w_in, att_w_out, att_sink, diff_lambda_vecs, diff_subln_g, rk_mu, rk_w_rkv, rk_w_out, rk_decay0, rk_decay1, rk_decay2, rk_iclr0, rk_iclr1, rk_iclr2, rk_gate1, rk_gate2, rk_k_k, rk_k_a, rk_r_k, rk_lnx, moe_router, moe_bias, moe_w_in, moe_w_out, moe_ws_in, moe_ws_out):
    B, S, D = x.shape
    L = ctx.shape[1]
    N = L + S
    tm = ROW_TILE
    assert L % tm == 0 and S % tm == 0 and L % SCAN_CHUNK == 0 and S % SCAN_CHUNK == 0
    assert tm % B == 0 and D % SCAN_COLS == 0

    rows = -(-(B + 1) // 8) * 8
    cvec = jnp.concatenate([c, c_ctx[None, :], jnp.zeros((rows - B - 1, D), F32)], axis=0)
    mods = [_mod_table(_ada_mod(cvec, ada_w[i], ada_b[i]), B, D) for i in range(DEPTH)]

    h0 = jnp.concatenate([ctx, x], axis=1)
    lam_init = 0.8 - 0.6 * math.exp(-0.3 * 0)
    qa, ka, va, qb, kb, vb = _attn_inproj(h0, mods[0], att_w_in[0], L)
    oa = _win_attn(qa, ka, va, att_sink[0], L)
    ob = _diff_attn(qb, kb, vb, diff_lambda_vecs[0], diff_subln_g[0], lam_init, L)
    h1, u1 = _mix_out([oa, ob], [att_w_out[0][:A_WIDTH], att_w_out[0][A_WIDTH:]], h0, mods[0],
                      post_ln_g[0, 0], post_ln_b[0, 0], L, 0)
    tiles_b, tiles_c = N // tm, L // tm
    gate0 = mods[0][:, :, 5].reshape(B * 2, 1, D)
    gate0_index = lambda i: (i // tiles_b) * 2 + jnp.minimum((i % tiles_b) // tiles_c, 1)
    h2 = _moe_layer(u1.reshape(B * N, D), h1.reshape(B * N, D), gate0, gate0_index, moe_router[0], moe_bias[0],
                    moe_w_in, moe_w_out, moe_ws_in[0], moe_ws_out[0],
                    post_ln_g[0, 1], post_ln_b[0, 1], 0).reshape(B, N, D)

    m_ctx, m_lat = mods[1][:, 0], mods[1][:, 1]
    h2_t = jnp.swapaxes(h2, 0, 1)
    is_lat = (jnp.arange(N) >= L)[:, None, None]
    u = h2_t * (1.0 + jnp.where(is_lat, m_lat[:, 1], m_ctx[:, 1])) + jnp.where(is_lat, m_lat[:, 0], m_ctx[:, 0])
    dx = _shift_delta(u, L)
    params = dict(mu=rk_mu[0], w_rkv=rk_w_rkv[0], gate1=rk_gate1[0], gate2=rk_gate2[0],
                  dec0=rk_decay0[0], dec1=rk_decay1[0], dec2=rk_decay2[0],
                  icl0=rk_iclr0[0], icl1=rk_iclr1[0], icl2=rk_iclr2[0],
                  k_k=rk_k_k[0], k_a=rk_k_a[0], r_k=rk_r_k[0])
    r, v, a, g, bonus, w2, k2, b2 = _rwkv_proj(u.reshape(N * B, D), dx.reshape(N * B, D), params)
    tmaj = lambda t: t.reshape(t.shape[:-2] + (N, B, D))
    y = _wkv_scan(tmaj(r), tmaj(w2), tmaj(k2), tmaj(v), tmaj(a), tmaj(b2), L)
    lat_rows = lambda j: jnp.tile(m_lat[:, j], (tm // B, 1))
    h3, u3 = _rwkv_out(y.reshape(2, N * B, D), bonus, g, rk_lnx[0], rk_w_out[0], h2_t.reshape(N * B, D),
                       jnp.stack([lat_rows(2), lat_rows(3), lat_rows(4)]),
                       post_ln_g[1, 0], post_ln_b[1, 0], L * B)
    out = _moe_layer(u3, h3, lat_rows(5)[None], lambda i: 0, moe_router[1], moe_bias[1],
                     moe_w_in, moe_w_out, moe_ws_in[1], moe_ws_out[1],
                     post_ln_g[1, 1], post_ln_b[1, 1], 1)
    return jnp.swapaxes(out.reshape(S, B, D), 0, 1)
```

```python
import functools
import math

import numpy as np
import jax
import jax.numpy as jnp
from jax import lax
from jax.experimental import pallas as pl
from jax.experimental.pallas import tpu as pltpu

F32 = jnp.float32
BF16 = jnp.bfloat16
I32 = jnp.int32

HEAD_DIM = 64
GRID_W = 64
ROPE_AXIS_DIM = HEAD_DIM // 2
ROPE_THETA = 10000.0
Q_BLOCK = 128
A_Q_HEADS = 8
A_KV_HEADS = 2
A_GROUP = A_Q_HEADS // A_KV_HEADS
A_WIDTH = A_Q_HEADS * HEAD_DIM
A_KV_WIDTH = A_KV_HEADS * HEAD_DIM
B_HEADS = 4
B_V_DIM = 2 * HEAD_DIM
B_WIDTH = B_HEADS * B_V_DIM
LNX_EPS = 64e-5
N_EXPERTS = 256
TOP_K = 8
N_GROUPS = 8
TOPK_GROUPS = 4
ROUTED_SCALE = 2.5
MOE_BLOCK = 128
LN_EPS = 1e-5
SUBLN_EPS = 1e-5
NEG_INF = -1e30
DEPTH = 2
DEEPNORM_ALPHA = (2 * DEPTH) ** 0.25

LANES = 128
ROW_TILE = 256
PROJ_TILE = 128
SCAN_CHUNK = 64
SCAN_COLS = 512
SCAN_GROUP = 16
VMEM_LIMIT = 56 * 1024 * 1024


def _cparams(sem):
    return pltpu.CompilerParams(dimension_semantics=sem, vmem_limit_bytes=VMEM_LIMIT)


def _silu(x):
    return x * jax.nn.sigmoid(x)


def _layer_norm(z, g, b):
    mu = jnp.mean(z, -1, keepdims=True)
    zc = z - mu
    var = jnp.mean(zc * zc, -1, keepdims=True)
    return zc * lax.rsqrt(var + LN_EPS) * g + b


def _dot(a, b):
    return jnp.dot(a, b, preferred_element_type=F32)


def _dot_nt(a, b):
    return lax.dot_general(a, b, (((1,), (1,)), ((), ())), preferred_element_type=F32)


def _pack_halves(x):
    half = x.shape[1] // 2
    bits = lambda v: lax.bitcast_convert_type(v.astype(BF16).astype(F32), I32)
    return lax.shift_right_logical(bits(x[:, :half]), 16) | bits(x[:, half:])


def _unpack_halves(p):
    lo = lax.bitcast_convert_type(lax.shift_left(p, 16), F32)
    hi = lax.bitcast_convert_type(p & jnp.int32(-65536), F32)
    return lo, hi


def _dot_halves(p, w_ref_or_array):
    lo, hi = _unpack_halves(p)
    half = p.shape[1]
    return _dot(lo.astype(BF16), w_ref_or_array[:half]) + _dot(hi.astype(BF16), w_ref_or_array[half:])


def _ada_kernel(c_ref, w_ref, b_ref, o_ref):
    c = c_ref[...]
    o_ref[...] = jnp.dot(_silu(c), w_ref[...], preferred_element_type=F32,
                         precision=lax.Precision.HIGHEST) + b_ref[...]


def _ada_mod(cvec, w, bias):
    R, D = cvec.shape
    n_out = w.shape[1]
    tn = 768
    return pl.pallas_call(
        _ada_kernel,
        grid=(n_out // tn,),
        in_specs=[pl.BlockSpec((R, D), lambda j: (0, 0)),
                  pl.BlockSpec((D, tn), lambda j: (0, j)),
                  pl.BlockSpec((1, tn), lambda j: (0, j))],
        out_specs=pl.BlockSpec((R, tn), lambda j: (0, j)),
        out_shape=jax.ShapeDtypeStruct((R, n_out), F32),
        compiler_params=_cparams(("arbitrary",)),
    )(cvec, w, bias.reshape(1, n_out))


def _mod_table(m, batch, d):
    m_lat = m[:batch].reshape(batch, 6, d)
    m_ctx = jnp.broadcast_to(m[batch].reshape(1, 6, d), (batch, 6, d))
    return jnp.stack([m_ctx, m_lat], axis=1)


def _mod_spec(d, ctx_tiles):
    return pl.BlockSpec((1, 1, 6, d), lambda b, i: (b, jnp.minimum(i // ctx_tiles, 1), 0, 0))


def _rope_tables(n_ctx, n_lat):
    rows = n_lat // GRID_W
    row = np.repeat(np.arange(rows), GRID_W).astype(np.float32)
    col = np.tile(np.arange(GRID_W), rows).astype(np.float32)
    inv = (ROPE_THETA ** (-np.arange(0, ROPE_AXIS_DIM, 2, dtype=np.float32) / ROPE_AXIS_DIM)).astype(np.float32)
    ar = row[:, None] * inv
    ac = col[:, None] * inv
    ang = np.concatenate([ar, ar, ac, ac], -1)
    cos = np.cos(ang).astype(np.float32)
    sin = np.sin(ang).astype(np.float32)
    lower = (np.arange(HEAD_DIM) % ROPE_AXIS_DIM) < (ROPE_AXIS_DIM // 2)
    sin_up = np.where(lower[None, :], -sin, 0.0)
    sin_dn = np.where(lower[None, :], 0.0, sin)

    def full(t, ctx_fill):
        t = np.concatenate([np.full((n_ctx, HEAD_DIM), ctx_fill, np.float32), t], 0)
        return jnp.asarray(np.tile(t, (1, LANES // HEAD_DIM)))

    return full(cos, 1.0), full(sin_up, 0.0), full(sin_dn, 0.0)


def _inproj_kernel(h_ref, mod_ref, w_ref, cos_ref, su_ref, sd_ref,
                   qa_ref, ka_ref, va_ref, qb_ref, kb_ref, vb_ref):
    h = h_ref[0]
    shift = mod_ref[0, 0, 0:1, :]
    scale = mod_ref[0, 0, 1:2, :]
    u = (h * (1.0 + scale) + shift).astype(BF16)
    y = _dot(u, w_ref[...])
    cos, s_up, s_dn = cos_ref[...], su_ref[...], sd_ref[...]
    q_scale = HEAD_DIM ** -0.5

    def rope(xc):
        half = ROPE_AXIS_DIM // 2
        return xc * cos + pltpu.roll(xc, LANES - half, 1) * s_up + pltpu.roll(xc, half, 1) * s_dn

    def emit(out_ref, col0, width, roped, mul):
        for j in range(width // LANES):
            xc = y[:, col0 + j * LANES: col0 + (j + 1) * LANES]
            if roped:
                xc = rope(xc)
            if mul != 1.0:
                xc = xc * mul
            out_ref[0, :, j * LANES:(j + 1) * LANES] = xc.astype(out_ref.dtype)

    c = 0
    emit(qa_ref, c, A_WIDTH, True, q_scale); c += A_WIDTH
    emit(ka_ref, c, A_KV_WIDTH, True, 1.0); c += A_KV_WIDTH
    emit(va_ref, c, A_KV_WIDTH, False, 1.0); c += A_KV_WIDTH
    emit(qb_ref, c, B_WIDTH, True, q_scale); c += B_WIDTH
    emit(kb_ref, c, B_WIDTH, True, 1.0); c += B_WIDTH
    emit(vb_ref, c, B_WIDTH, False, 1.0)


def _attn_inproj(h, mod, w_in, n_ctx):
    B, N, D = h.shape
    tm = ROW_TILE
    cos, s_up, s_dn = _rope_tables(n_ctx, N - n_ctx)
    widths = (A_WIDTH, A_KV_WIDTH, A_KV_WIDTH, B_WIDTH, B_WIDTH, B_WIDTH)
    tab_spec = pl.BlockSpec((tm, LANES), lambda b, i: (i, 0))
    return pl.pallas_call(
        _inproj_kernel,
        grid=(B, N // tm),
        in_specs=[pl.BlockSpec((1, tm, D), lambda b, i: (b, i, 0)),
                  _mod_spec(D, n_ctx // tm),
                  pl.BlockSpec(w_in.shape, lambda b, i: (0, 0)),
                  tab_spec, tab_spec, tab_spec],
        out_specs=[pl.BlockSpec((1, tm, w), lambda b, i: (b, i, 0)) for w in widths],
        out_shape=[jax.ShapeDtypeStruct((B, N, w), BF16) for w in widths],
        compiler_params=_cparams(("arbitrary", "arbitrary")),
    )(h, mod, w_in.astype(BF16), cos, s_up, s_dn)


def _win_attn_kernel(n_ctx_blocks, n_blocks, q_ref, kc_ref, vc_ref, kl_ref, km_ref, kr_ref,
                     vl_ref, vm_ref, vr_ref, sink_ref, o_ref):
    j = pl.program_id(1)
    is_lat = j >= n_ctx_blocks
    qb = Q_BLOCK
    n_c = kc_ref.shape[1]
    rows = A_GROUP * qb
    n_keys = n_c + 3 * qb
    far = 1 << 20
    r_idx = lax.broadcasted_iota(I32, (rows, n_keys), 0) % qb
    cw = lax.broadcasted_iota(I32, (rows, n_keys), 1) - n_c
    off_l = jnp.where(jnp.logical_and(is_lat, j > n_ctx_blocks), 0, far)
    end_m = jnp.where(is_lat, 2 * qb, qb)
    off_r = jnp.where(jnp.logical_and(is_lat, j < n_blocks - 1), 0, far)
    valid = ((cw < 0)
             | ((cw >= 0) & (cw < qb) & (cw >= r_idx + off_l))
             | ((cw >= qb) & (cw < end_m))
             | ((cw >= 2 * qb) & (cw - 2 * qb + off_r <= r_idx)))
    outs = []
    for kv in range(A_KV_HEADS):
        cols = slice(kv * HEAD_DIM, (kv + 1) * HEAD_DIM)
        k_all = jnp.concatenate([kc_ref[0, :, cols], kl_ref[0, :, cols], km_ref[0, :, cols],
                                 kr_ref[0, :, cols]], axis=0)
        v_all = jnp.concatenate([vc_ref[0, :, cols], vl_ref[0, :, cols], vm_ref[0, :, cols],
                                 vr_ref[0, :, cols]], axis=0)
        q0 = kv * A_GROUP
        q = jnp.concatenate([q_ref[0, :, (q0 + g) * HEAD_DIM:(q0 + g + 1) * HEAD_DIM]
                             for g in range(A_GROUP)], axis=0)
        sink = jnp.concatenate([jnp.broadcast_to(sink_ref[q0 + g:q0 + g + 1, 0:1], (qb, 1))
                                for g in range(A_GROUP)], axis=0)
        s = jnp.where(valid, _dot_nt(q, k_all), NEG_INF)
        m = jnp.maximum(jnp.max(s, -1, keepdims=True), sink)
        e = jnp.exp(s - m)
        denom = jnp.sum(e, -1, keepdims=True) + jnp.exp(sink - m)
        p = (e / denom).astype(BF16)
        o = _dot(p, v_all)
        outs += [o[g * qb:(g + 1) * qb] for g in range(A_GROUP)]
    for j2 in range(A_Q_HEADS // 2):
        pair = jnp.concatenate([outs[2 * j2], outs[2 * j2 + 1]], axis=1)
        o_ref[0, :, j2 * LANES:(j2 + 1) * LANES] = pair.astype(o_ref.dtype)


def _win_attn(qa, ka, va, sink, n_ctx):
    B, N, _ = qa.shape
    qb = Q_BLOCK
    nb = N // qb
    ncb = n_ctx // qb
    sink_pad = jnp.broadcast_to(sink.reshape(A_Q_HEADS, 1).astype(F32), (A_Q_HEADS, LANES))

    def left(b, j):
        return (b, jnp.clip(j - 1, ncb, nb - 1), 0)

    def mid(b, j):
        return (b, jnp.clip(j, ncb, nb - 1), 0)

    def right(b, j):
        return (b, jnp.clip(j + 1, ncb, nb - 1), 0)

    kv_blk = lambda im: pl.BlockSpec((1, qb, A_KV_WIDTH), im)
    ctx_blk = pl.BlockSpec((1, n_ctx, A_KV_WIDTH), lambda b, j: (b, 0, 0))
    return pl.pallas_call(
        functools.partial(_win_attn_kernel, ncb, nb),
        grid=(B, nb),
        in_specs=[pl.BlockSpec((1, qb, A_WIDTH), lambda b, j: (b, j, 0)),
                  ctx_blk, ctx_blk,
                  kv_blk(left), kv_blk(mid), kv_blk(right),
                  kv_blk(left), kv_blk(mid), kv_blk(right),
                  pl.BlockSpec((A_Q_HEADS, LANES), lambda b, j: (0, 0))],
        out_specs=pl.BlockSpec((1, qb, A_WIDTH), lambda b, j: (b, j, 0)),
        out_shape=jax.ShapeDtypeStruct((B, N, A_WIDTH), BF16),
        compiler_params=_cparams(("arbitrary", "arbitrary")),
    )(qa, ka, va, ka, ka, ka, va, va, va, sink_pad)


def _diff_attn_kernel(n_ctx, lam_init, q_ref, k_ref, v_ref, lv_ref, g_ref, o_ref):
    j = pl.program_id(1)
    lv = lv_ref[...]
    lam = (jnp.exp(jnp.sum(lv[0:1] * lv[1:2], -1, keepdims=True))
           - jnp.exp(jnp.sum(lv[2:3] * lv[3:4], -1, keepdims=True)) + lam_init)
    gain = g_ref[...] * (1.0 - lam_init)

    def run(n_keys):
        for hd in range(B_HEADS):
            parts = []
            for mm in range(2):
                c0 = (hd * 2 + mm) * HEAD_DIM
                q = q_ref[0, :, c0:c0 + HEAD_DIM]
                k = k_ref[0, :n_keys, c0:c0 + HEAD_DIM]
                s = _dot_nt(q, k)
                e = jnp.exp(s - jnp.max(s, -1, keepdims=True))
                parts.append((e, jnp.sum(e, -1, keepdims=True)))
            (e0, l0), (e1, l1) = parts
            a = (e0 * (1.0 / l0) - e1 * (lam / l1)).astype(BF16)
            o = _dot(a, v_ref[0, :n_keys, hd * B_V_DIM:(hd + 1) * B_V_DIM])
            o = o * lax.rsqrt(jnp.mean(o * o, -1, keepdims=True) + SUBLN_EPS) * gain
            o_ref[0, :, hd * B_V_DIM:(hd + 1) * B_V_DIM] = o.astype(o_ref.dtype)

    @pl.when(j == 0)
    def _():
        run(n_ctx)

    @pl.when(j > 0)
    def _():
        run(k_ref.shape[1])


def _diff_attn(qb, kb, vb, lam_vecs, subln_g, lam_init, n_ctx):
    B, N, _ = qb.shape
    tq = n_ctx
    return pl.pallas_call(
        functools.partial(_diff_attn_kernel, n_ctx, lam_init),
        grid=(B, N // tq),
        in_specs=[pl.BlockSpec((1, tq, B_WIDTH), lambda b, j: (b, j, 0)),
                  pl.BlockSpec((1, N, B_WIDTH), lambda b, j: (b, 0, 0)),
                  pl.BlockSpec((1, N, B_WIDTH), lambda b, j: (b, 0, 0)),
                  pl.BlockSpec((4, HEAD_DIM), lambda b, j: (0, 0)),
                  pl.BlockSpec((1, B_V_DIM), lambda b, j: (0, 0))],
        out_specs=pl.BlockSpec((1, tq, B_WIDTH), lambda b, j: (b, j, 0)),
        out_shape=jax.ShapeDtypeStruct((B, N, B_WIDTH), BF16),
        compiler_params=_cparams(("arbitrary", "arbitrary")),
    )(qb, kb, vb, lam_vecs.astype(F32), subln_g.reshape(1, B_V_DIM).astype(F32))


def _mix_out_kernel(n_in, *refs):
    xs = refs[:n_in]
    ws = refs[n_in:2 * n_in]
    h_ref, mod_ref, g_ref, b_ref, hn_ref, u_ref = refs[2 * n_in:]
    o = _dot(xs[0][0], ws[0][...])
    for x_ref, w_ref in zip(xs[1:], ws[1:]):
        o = o + _dot(x_ref[0], w_ref[...])
    z = DEEPNORM_ALPHA * h_ref[0] + mod_ref[0, 0, 2:3, :] * o
    hn = _layer_norm(z, g_ref[...], b_ref[...])
    hn_ref[0] = hn
    u_ref[0] = _pack_halves(hn * (1.0 + mod_ref[0, 0, 4:5, :]) + mod_ref[0, 0, 3:4, :])


def _mix_out(xs, ws, h, mod, ln_g, ln_b, n_ctx, row0):
    B, N, D = h.shape
    tm = ROW_TILE
    t0 = row0 // tm
    n_out = N - row0
    row_spec = lambda w: pl.BlockSpec((1, tm, w), lambda b, i: (b, i + t0, 0))
    out_spec = pl.BlockSpec((1, tm, D), lambda b, i: (b, i, 0))
    vec_spec = pl.BlockSpec((1, D), lambda b, i: (0, 0))
    return pl.pallas_call(
        functools.partial(_mix_out_kernel, len(xs)),
        grid=(B, n_out // tm),
        in_specs=([row_spec(x.shape[-1]) for x in xs]
                  + [pl.BlockSpec(w.shape, lambda b, i: (0, 0)) for w in ws]
                  + [row_spec(D),
                     pl.BlockSpec((1, 1, 6, D), lambda b, i: (b, jnp.minimum((i + t0) // (n_ctx // tm), 1), 0, 0)),
                     vec_spec, vec_spec]),
        out_specs=[out_spec, pl.BlockSpec((1, tm, D // 2), lambda b, i: (b, i, 0))],
        out_shape=[jax.ShapeDtypeStruct((B, n_out, D), F32), jax.ShapeDtypeStruct((B, n_out, D // 2), I32)],
        compiler_params=_cparams(("arbitrary", "arbitrary")),
    )(*xs, *[w.astype(BF16) for w in ws], h, mod, ln_g.reshape(1, D), ln_b.reshape(1, D))


def _router_kernel(u_ref, rt_ref, bias_ref, tri_ref, e_ref, gw_ref, rank_ref, cnt_ref, carry_ref):
    i = pl.program_id(0)

    @pl.when(i == 0)
    def _():
        carry_ref[...] = jnp.zeros_like(carry_ref)

    tm = u_ref.shape[0]
    per_group = N_EXPERTS // N_GROUPS
    neg = -jnp.inf
    u_lo, u_hi = _unpack_halves(u_ref[...])
    half = u_ref.shape[1]
    logits = (_dot_nt(rt_ref[:, :half], u_lo.astype(BF16))
              + _dot_nt(rt_ref[:, half:], u_hi.astype(BF16)))
    scores = jax.nn.sigmoid(logits)
    sel = scores + bias_ref[...]
    io_in = lax.broadcasted_iota(I32, (per_group, tm), 0)
    grp_rows = []
    for gi in range(N_GROUPS):
        sg = sel[gi * per_group:(gi + 1) * per_group]
        m1 = jnp.max(sg, axis=0, keepdims=True)
        i1 = jnp.min(jnp.where(sg == m1, io_in, per_group), axis=0, keepdims=True)
        m2 = jnp.max(jnp.where(io_in == i1, neg, sg), axis=0, keepdims=True)
        grp_rows.append(m1 + m2)
    grp = jnp.concatenate(grp_rows, axis=0)
    io_g = lax.broadcasted_iota(I32, grp.shape, 0)
    g_sel = jnp.zeros(grp.shape, F32)
    for _ in range(TOPK_GROUPS):
        m = jnp.max(grp, axis=0, keepdims=True)
        hit = io_g == jnp.min(jnp.where(grp == m, io_g, N_GROUPS), axis=0, keepdims=True)
        g_sel = jnp.where(hit, 1.0, g_sel)
        grp = jnp.where(hit, neg, grp)
    selm = jnp.concatenate(
        [jnp.where(g_sel[gi:gi + 1] > 0.5, sel[gi * per_group:(gi + 1) * per_group], NEG_INF)
         for gi in range(N_GROUPS)], axis=0)
    io_e = lax.broadcasted_iota(I32, selm.shape, 0)
    chosen_f = jnp.zeros(selm.shape, F32)
    idx, gws = [], []
    for _ in range(TOP_K):
        m = jnp.max(selm, axis=0, keepdims=True)
        ik = jnp.min(jnp.where(selm == m, io_e, N_EXPERTS), axis=0, keepdims=True)
        hit = io_e == ik
        idx.append(ik)
        gws.append(jnp.sum(jnp.where(hit, scores, 0.0), axis=0, keepdims=True))
        chosen_f = jnp.where(hit, 1.0, chosen_f)
        selm = jnp.where(hit, neg, selm)
    gw = jnp.concatenate(gws, axis=0)
    gw_ref[...] = gw / jnp.sum(gw, axis=0, keepdims=True) * ROUTED_SCALE
    e_ref[...] = jnp.concatenate(idx, axis=0)
    before = _dot(chosen_f.astype(BF16), tri_ref[...]) + carry_ref[...]
    ranks = [jnp.sum(jnp.where(io_e == ik, before, 0.0), axis=0, keepdims=True) for ik in idx]
    rank_ref[...] = jnp.concatenate(ranks, axis=0).astype(I32)
    carry_ref[...] = carry_ref[...] + jnp.sum(chosen_f, axis=1, keepdims=True)
    cnt_ref[...] = carry_ref[...].astype(I32)


def _router(u, router, bias):
    T = u.shape[0]
    D = router.shape[0]
    tm = ROW_TILE
    tri = jnp.asarray(np.triu(np.ones((tm, tm), np.float32), 1), BF16)
    tok_spec = pl.BlockSpec((TOP_K, tm), lambda i: (0, i))
    return pl.pallas_call(
        _router_kernel,
        grid=(T // tm,),
        in_specs=[pl.BlockSpec((tm, D // 2), lambda i: (i, 0)),
                  pl.BlockSpec((N_EXPERTS, D), lambda i: (0, 0)),
                  pl.BlockSpec((N_EXPERTS, 1), lambda i: (0, 0)),
                  pl.BlockSpec((tm, tm), lambda i: (0, 0))],
        out_specs=[tok_spec, tok_spec, tok_spec, pl.BlockSpec((N_EXPERTS, 1), lambda i: (0, 0))],
        out_shape=[jax.ShapeDtypeStruct((TOP_K, T), I32), jax.ShapeDtypeStruct((TOP_K, T), F32),
                   jax.ShapeDtypeStruct((TOP_K, T), I32), jax.ShapeDtypeStruct((N_EXPERTS, 1), I32)],
        scratch_shapes=[pltpu.VMEM((N_EXPERTS, 1), F32)],
        compiler_params=_cparams(("arbitrary",)),
    )(u, router.T.astype(BF16), bias.reshape(N_EXPERTS, 1).astype(F32), tri)


def _dispatch_kernel(dest_ref, x_ref, xs_in, xs_hbm, sem):
    del xs_in
    tm = dest_ref.shape[1]

    def row_copy(t, k):
        return pltpu.make_async_copy(x_ref.at[pl.ds(t, 1)], xs_hbm.at[pl.ds(dest_ref[k, t], 1)], sem)

    def start(t, carry):
        for k in range(TOP_K):
            row_copy(t, k).start()
        return carry

    def wait(t, carry):
        for k in range(TOP_K):
            row_copy(t, k).wait()
        return carry

    lax.fori_loop(0, tm, start, 0)
    lax.fori_loop(0, tm, wait, 0)


def _dispatch(x, dest, n_rows):
    T, D = x.shape
    tm = ROW_TILE
    xs0 = jnp.zeros((n_rows, D), x.dtype)
    return pl.pallas_call(
        _dispatch_kernel,
        grid=(T // tm,),
        in_specs=[pl.BlockSpec((TOP_K, tm), lambda i: (0, i), memory_space=pltpu.SMEM),
                  pl.BlockSpec((tm, D), lambda i: (i, 0)),
                  pl.BlockSpec(memory_space=pl.ANY)],
        out_specs=pl.BlockSpec(memory_space=pl.ANY),
        out_shape=jax.ShapeDtypeStruct((n_rows, D), x.dtype),
        scratch_shapes=[pltpu.SemaphoreType.DMA(())],
        input_output_aliases={2: 0},
        compiler_params=_cparams(("arbitrary",)),
    )(dest, x, xs0)


def _expert_kernel(be_ref, nu_ref, x_ref, wi_ref, wo_ref, y_ref, wi_b, wo_b):
    i = pl.program_id(0)

    @pl.when(i < nu_ref[0])
    def _():
        @pl.when(jnp.logical_or(i == 0, be_ref[i] != be_ref[jnp.maximum(i - 1, 0)]))
        def _():
            wi_b[...] = wi_ref[0, 0].astype(BF16)
            wo_b[...] = wo_ref[0, 0].astype(BF16)

        ff = wo_b.shape[0]
        hcat = _dot_halves(x_ref[...], wi_b)
        act = (_silu(hcat[:, :ff]) * hcat[:, ff:]).astype(BF16)
        y_ref[...] = _pack_halves(_dot(act, wo_b[...]))

    @pl.when(i >= nu_ref[0])
    def _():
        y_ref[...] = jnp.zeros_like(y_ref)


def _experts(xs, block_exp, n_used, w_in, w_out, layer):
    P, half = xs.shape
    D, ff2 = w_in.shape[-2:]
    nblk = P // MOE_BLOCK
    row_spec = pl.BlockSpec((MOE_BLOCK, half), lambda i, be, nu: (i, 0))
    return pl.pallas_call(
        _expert_kernel,
        grid_spec=pltpu.PrefetchScalarGridSpec(
            num_scalar_prefetch=2,
            grid=(nblk,),
            in_specs=[row_spec,
                      pl.BlockSpec((1, 1, D, ff2), lambda i, be, nu: (layer, be[i], 0, 0)),
                      pl.BlockSpec((1, 1, ff2 // 2, D), lambda i, be, nu: (layer, be[i], 0, 0))],
            out_specs=row_spec,
            scratch_shapes=[pltpu.VMEM((D, ff2), BF16), pltpu.VMEM((ff2 // 2, D), BF16)]),
        out_shape=jax.ShapeDtypeStruct((P, half), I32),
        compiler_params=_cparams(("arbitrary",)),
    )(block_exp, n_used, xs, w_in, w_out)


def _combine_kernel(dest_ref, y_hbm, gw_ref, u_ref, wsi_ref, wso_ref, h_ref, mod_ref, g_ref, b_ref,
                    o_ref, buf, sem):
    tm = u_ref.shape[0]

    def row_copy(t, k):
        return pltpu.make_async_copy(y_hbm.at[pl.ds(dest_ref[k, t], 1)], buf.at[k, pl.ds(t, 1)], sem)

    def start(t, carry):
        for k in range(TOP_K):
            row_copy(t, k).start()
        return carry

    def wait(t, carry):
        for k in range(TOP_K):
            row_copy(t, k).wait()
        return carry

    lax.fori_loop(0, tm, start, 0)
    ff = wso_ref.shape[0]
    hcat = _dot_halves(u_ref[...], wsi_ref)
    shared = _dot((_silu(hcat[:, :ff]) * hcat[:, ff:]).astype(BF16), wso_ref[...])
    lax.fori_loop(0, tm, wait, 0)
    lo, hi = None, None
    for k in range(TOP_K):
        y_lo, y_hi = _unpack_halves(buf[k])
        gk = gw_ref[:, k:k + 1]
        lo = y_lo * gk if lo is None else lo + y_lo * gk
        hi = y_hi * gk if hi is None else hi + y_hi * gk
    routed = jnp.concatenate([lo, hi], axis=1)
    z = DEEPNORM_ALPHA * h_ref[...] + mod_ref[0] * (routed + shared)
    o_ref[...] = _layer_norm(z, g_ref[...], b_ref[...])


def _combine(y, dest, gw_t, u, ws_in, ws_out, h, gate, gate_index, ln_g, ln_b):
    T, D = h.shape
    tm = ROW_TILE
    vec_spec = pl.BlockSpec((1, D), lambda i: (0, 0))
    row_spec = pl.BlockSpec((tm, D), lambda i: (i, 0))
    packed_spec = pl.BlockSpec((tm, D // 2), lambda i: (i, 0))
    return pl.pallas_call(
        _combine_kernel,
        grid=(T // tm,),
        in_specs=[pl.BlockSpec((TOP_K, tm), lambda i: (0, i), memory_space=pltpu.SMEM),
                  pl.BlockSpec(memory_space=pl.ANY),
                  pl.BlockSpec((tm, TOP_K), lambda i: (i, 0)),
                  packed_spec,
                  pl.BlockSpec(ws_in.shape, lambda i: (0, 0)),
                  pl.BlockSpec(ws_out.shape, lambda i: (0, 0)),
                  row_spec,
                  pl.BlockSpec((1,) + gate.shape[1:], lambda i: (gate_index(i), 0, 0)),
                  vec_spec, vec_spec],
        out_specs=row_spec,
        out_shape=jax.ShapeDtypeStruct((T, D), F32),
        scratch_shapes=[pltpu.VMEM((TOP_K, tm, D // 2), I32), pltpu.SemaphoreType.DMA(())],
        compiler_params=_cparams(("arbitrary",)),
    )(dest, y, gw_t, u, ws_in.astype(BF16), ws_out.astype(BF16), h, gate,
      ln_g.reshape(1, D), ln_b.reshape(1, D))


def _slots_kernel(e_ref, rank_ref, start_ref, dest_ref):
    io_e = lax.broadcasted_iota(I32, (N_EXPERTS, e_ref.shape[1]), 0)
    rows = [jnp.sum(jnp.where(io_e == e_ref[k:k + 1, :], start_ref[...], 0), axis=0, keepdims=True)
            for k in range(TOP_K)]
    dest_ref[...] = jnp.concatenate(rows, axis=0) + rank_ref[...]


def _slots(eidx, rank, pstart):
    T = eidx.shape[1]
    tm = ROW_TILE
    tok_spec = pl.BlockSpec((TOP_K, tm), lambda i: (0, i))
    return pl.pallas_call(
        _slots_kernel,
        grid=(T // tm,),
        in_specs=[tok_spec, tok_spec, pl.BlockSpec((N_EXPERTS, 1), lambda i: (0, 0))],
        out_specs=tok_spec,
        out_shape=jax.ShapeDtypeStruct((TOP_K, T), I32),
        compiler_params=_cparams(("arbitrary",)),
    )(eidx, rank, pstart.reshape(N_EXPERTS, 1))


def _moe_layer(u, h, gate, gate_index, router, bias, w_in, w_out, ws_in, ws_out, ln_g, ln_b, layer):
    T = u.shape[0]
    eidx, gw, rank, counts = _router(u, router, bias)
    counts = counts[:, 0]
    padded = (counts + MOE_BLOCK - 1) // MOE_BLOCK * MOE_BLOCK
    pend = jnp.cumsum(padded)
    pstart = (pend - padded).astype(I32)
    dest = _slots(eidx, rank, pstart)
    n_blocks = -(-(T * TOP_K + N_EXPERTS * (MOE_BLOCK - 1)) // MOE_BLOCK)
    block_exp = jnp.minimum(jnp.searchsorted(pend, jnp.arange(n_blocks, dtype=I32) * MOE_BLOCK, side='right'),
                            N_EXPERTS - 1).astype(I32)
    n_used = (pend[-1:] // MOE_BLOCK).astype(I32)
    xs = _dispatch(u, dest, n_blocks * MOE_BLOCK)
    y = _experts(xs, block_exp, n_used, w_in, w_out, layer)
    return _combine(y, dest, gw.T, u, ws_in, ws_out, h, gate, gate_index, ln_g, ln_b)


def _seg_ones(width=LANES):
    idx = np.arange(width) // HEAD_DIM
    return jnp.asarray((idx[:, None] == idx[None, :]).astype(np.float32), BF16)


def _head_sum(x, ones_ref):
    outs = []
    for j in range(x.shape[1] // LANES):
        xc = x[:, j * LANES:(j + 1) * LANES]
        hi = xc.astype(BF16)
        lo = (xc - hi.astype(F32)).astype(BF16)
        outs.append(_dot(hi, ones_ref[...]) + _dot(lo, ones_ref[...]))
    return jnp.concatenate(outs, axis=1)


def _rwkv_proj_kernel(u_ref, dx_ref, mu_ref, wrkv_ref, g1_ref, g2_ref, d1_ref, d2_ref, d0_ref,
                      i1_ref, i2_ref, i0_ref, kk_ref, ka_ref, rk_ref, ones_ref,
                      r_ref, v_ref, a_ref, g_ref, bonus_ref, w_ref, k_ref, b_ref):
    u = u_ref[...]
    dx = dx_ref[...]
    mix = lambda m: (u + dx * mu_ref[m:m + 1, :])
    xr, xw, xk, xv, xa, xg = [mix(m) for m in range(6)]
    r = _dot(xr.astype(BF16), wrkv_ref[0])
    k = _dot(xk.astype(BF16), wrkv_ref[1])
    v = _dot(xv.astype(BF16), wrkv_ref[2])
    g = _dot(jax.nn.sigmoid(_dot(xg.astype(BF16), g1_ref[...])).astype(BF16), g2_ref[...])
    kk = k * kk_ref[...]
    kk = kk * lax.rsqrt(jnp.maximum(_head_sum(kk * kk, ones_ref), 1e-24))
    r_ref[...] = r
    v_ref[...] = v
    a_ref[...] = -kk
    g_ref[...] = g
    k_sum = None
    xw_b = xw.astype(BF16)
    xa_b = xa.astype(BF16)
    for d in range(2):
        lw = d0_ref[d:d + 1, :] + _dot(jnp.tanh(_dot(xw_b, d1_ref[d])).astype(BF16), d2_ref[d])
        softplus = jnp.maximum(-lw, 0.0) + jnp.log(1.0 + jnp.exp(-jnp.abs(lw)))
        logw = -softplus - 0.5
        w_ref[d] = jnp.exp(-jnp.exp(logw))
        eta = jax.nn.sigmoid(i0_ref[d:d + 1, :] + _dot(_dot(xa_b, i1_ref[d]).astype(BF16), i2_ref[d]))
        k_d = k * (1.0 + (eta - 1.0) * ka_ref[...])
        k_ref[d] = k_d
        b_ref[d] = kk * eta
        k_sum = k_d if k_sum is None else k_sum + k_d
    bonus_ref[...] = _head_sum(r * k_sum * rk_ref[...], ones_ref) * v


def _rwkv_proj(u, dx, p):
    T, D = u.shape
    tm = PROJ_TILE
    row = pl.BlockSpec((tm, D), lambda i: (i, 0))
    row2 = pl.BlockSpec((2, tm, D), lambda i: (0, i, 0))
    full = lambda a: pl.BlockSpec(a.shape, lambda i: (0,) * a.ndim)
    bf = lambda a: a.astype(BF16)
    consts = [p['mu'], bf(p['w_rkv']), bf(p['gate1']), bf(p['gate2']), bf(p['dec1']), bf(p['dec2']), p['dec0'],
              bf(p['icl1']), bf(p['icl2']), p['icl0'], p['k_k'].reshape(1, D), p['k_a'].reshape(1, D),
              p['r_k'].reshape(1, D), _seg_ones()]
    one = jax.ShapeDtypeStruct((T, D), F32)
    two = jax.ShapeDtypeStruct((2, T, D), F32)
    return pl.pallas_call(
        _rwkv_proj_kernel,
        grid=(T // tm,),
        in_specs=[row, row] + [full(a) for a in consts],
        out_specs=[row, row, row, row, row, row2, row2, row2],
        out_shape=[one, one, one, one, one, two, two, two],
        compiler_params=_cparams(("arbitrary",)),
    )(u, dx, *consts)


def _scan_kernel(r_ref, w_ref, k_ref, v_ref, a_ref, b_ref, mix_ref, hsel_ref,
                 y_ref, s_ref, vt_ref):
    d = pl.program_id(0)
    c = pl.program_id(2)
    tc, nb = r_ref.shape[0], r_ref.shape[1]
    n_pairs = r_ref.shape[2] // LANES

    @pl.when(c == 0)
    def _():
        s_ref[...] = jnp.zeros_like(s_ref)

    for bb in range(nb):
        for hp in range(n_pairs):
            vt = v_ref[:, bb, hp * LANES:(hp + 1) * LANES].T
            vt_ref[bb * n_pairs + hp] = jnp.concatenate([vt[:HEAD_DIM], vt[HEAD_DIM:]], axis=1).astype(BF16)

    lane2 = lax.broadcasted_iota(I32, (HEAD_DIM, 2 * tc), 1) % tc
    low_half = lax.broadcasted_iota(I32, (1, LANES), 1) < HEAD_DIM
    tiles = [(bb, hp) for bb in range(nb) for hp in range(n_pairs)]
    groups = [tiles[i:i + SCAN_GROUP] for i in range(0, len(tiles), SCAN_GROUP)]

    def stacked(grp, get):
        return jnp.concatenate([jnp.broadcast_to(get(bb, slice(hp * LANES, (hp + 1) * LANES)), (HEAD_DIM, LANES))
                                for bb, hp in grp], axis=0)

    def load_state(grp):
        return jnp.concatenate([s_ref[bb * n_pairs + hp] for bb, hp in grp], axis=0)

    def emit_y(grp, st, t_y):
        r_rows = stacked(grp, lambda bb, cols: r_ref[t_y, bb:bb + 1, cols])
        yh = _dot_nt(hsel_ref[...], (st * r_rows).astype(BF16))
        y_up = pltpu.roll(yh, HEAD_DIM, 1)
        y_dn = pltpu.roll(yh, yh.shape[1] - HEAD_DIM, 1)
        for j in range(0, len(grp), 2):
            lanes = slice(j * HEAD_DIM, j * HEAD_DIM + LANES)
            (b0, p0), (b1, p1) = grp[j], grp[j + 1]
            y_ref[0, t_y, b0:b0 + 1, p0 * LANES:(p0 + 1) * LANES] = jnp.where(
                low_half, yh[0:1, lanes], y_up[1:2, lanes])
            y_ref[0, t_y, b1:b1 + 1, p1 * LANES:(p1 + 1) * LANES] = jnp.where(
                low_half, y_dn[0:1, lanes], yh[1:2, lanes])

    def step(s_i, carry):
        t = jnp.where(d == 0, s_i, tc - 1 - s_i)
        t_prev = jnp.where(s_i == 0, t, jnp.where(d == 0, t - 1, t + 1))
        at_t = lane2 == t
        for grp in groups:
            one = lambda ref: stacked(grp, lambda bb, cols: ref[t, bb:bb + 1, cols])
            two = lambda ref: stacked(grp, lambda bb, cols: ref[0, t, bb:bb + 1, cols])
            st = load_state(grp)
            vm = jnp.concatenate([jnp.where(at_t, vt_ref[bb * n_pairs + hp], jnp.zeros((), BF16))
                                  for bb, hp in grp], axis=0)
            lhs = jnp.concatenate([(st * one(a_ref)).astype(BF16), vm], axis=1)
            both = _dot(lhs, mix_ref[...])
            emit_y(grp, st, t_prev)
            st = st * two(w_ref) + both[:, :LANES] * two(b_ref) + both[:, LANES:] * two(k_ref)
            for j, (bb, hp) in enumerate(grp):
                s_ref[bb * n_pairs + hp] = st[j * HEAD_DIM:(j + 1) * HEAD_DIM]
        return carry

    lax.fori_loop(0, tc, step, 0)
    t_last = jnp.where(d == 0, tc - 1, 0)
    for grp in groups:
        emit_y(grp, load_state(grp), t_last)


def _wkv_scan(r, w, k, v, a, b, n_ctx):
    N, B, D = r.shape
    tc = SCAN_CHUNK
    wc = SCAN_COLS
    n_pairs = wc // LANES
    nc = N // tc
    ncc = n_ctx // tc

    def chunk(d, c):
        rev = jnp.where(c < ncc, ncc - 1 - c, nc - 1 - (c - ncc))
        return jnp.where(d == 0, c, rev)

    one = pl.BlockSpec((tc, B, wc), lambda d, g, c: (chunk(d, c), 0, g))
    two = pl.BlockSpec((1, tc, B, wc), lambda d, g, c: (d, chunk(d, c), 0, g))
    seg = np.arange(LANES) // HEAD_DIM
    mix = np.zeros((LANES + 2 * tc, 2 * LANES), np.float32)
    mix[:LANES, :LANES] = seg[:, None] == seg[None, :]
    hsel = np.zeros((8, LANES), np.float32)
    for hh in range(2):
        mix[LANES + hh * tc:LANES + (hh + 1) * tc, LANES + np.flatnonzero(seg == hh)] = 1.0
        hsel[hh, seg == hh] = 1.0
    const = lambda a_: pl.BlockSpec(a_.shape, lambda d, g, c: (0, 0))
    consts = [jnp.asarray(mix, BF16), jnp.asarray(hsel, BF16)]
    return pl.pallas_call(
        _scan_kernel,
        grid=(2, D // wc, nc),
        in_specs=[one, two, two, one, one, two] + [const(a_) for a_ in consts],
        out_specs=two,
        out_shape=jax.ShapeDtypeStruct((2, N, B, D), F32),
        scratch_shapes=[pltpu.VMEM((B * n_pairs, HEAD_DIM, LANES), F32),
                        pltpu.VMEM((B * n_pairs, HEAD_DIM, 2 * tc), BF16)],
        compiler_params=_cparams(("arbitrary", "arbitrary", "arbitrary")),
    )(r, w, k, v, a, b, *consts)


def _rwkv_out_kernel(y0_ref, y1_ref, bonus_ref, g_ref, lnx_ref, ones_ref, w_ref, h_ref, mod_ref, lg_ref, lb_ref,
                     hn_ref, u_ref):
    y = y0_ref[0] + y1_ref[0]
    ym = _head_sum(y, ones_ref) * (1.0 / HEAD_DIM)
    yc = y - ym
    yv = _head_sum(yc * yc, ones_ref) * (1.0 / HEAD_DIM)
    yn = yc * lax.rsqrt(yv + LNX_EPS) * lnx_ref[0:1, :] + lnx_ref[1:2, :]
    x = ((yn + bonus_ref[...]) * g_ref[...]).astype(BF16)
    o = _dot(x, w_ref[...])
    z = DEEPNORM_ALPHA * h_ref[...] + mod_ref[0] * o
    hn = _layer_norm(z, lg_ref[...], lb_ref[...])
    hn_ref[...] = hn
    u_ref[...] = _pack_halves(hn * (1.0 + mod_ref[2]) + mod_ref[1])


def _rwkv_out(y, bonus, g, lnx, w_out, h, mod_rows, ln_g, ln_b, row0):
    T, D = h.shape
    tm = ROW_TILE
    t0 = row0 // tm
    off = pl.BlockSpec((tm, D), lambda i: (i + t0, 0))
    out = pl.BlockSpec((tm, D), lambda i: (i, 0))
    full = lambda a: pl.BlockSpec(a.shape, lambda i: (0,) * a.ndim)
    vec = pl.BlockSpec((1, D), lambda i: (0, 0))
    ones = _seg_ones()
    w_b = w_out.astype(BF16)
    return pl.pallas_call(
        _rwkv_out_kernel,
        grid=((T - row0) // tm,),
        in_specs=[pl.BlockSpec((1, tm, D), lambda i: (0, i + t0, 0)),
                  pl.BlockSpec((1, tm, D), lambda i: (1, i + t0, 0)),
                  off, off, full(lnx), full(ones), full(w_b), off, full(mod_rows), vec, vec],
        out_specs=[out, pl.BlockSpec((tm, D // 2), lambda i: (i, 0))],
        out_shape=[jax.ShapeDtypeStruct((T - row0, D), F32), jax.ShapeDtypeStruct((T - row0, D // 2), I32)],
        compiler_params=_cparams(("arbitrary",)),
    )(y, y, bonus, g, lnx, ones, w_b, h, mod_rows, ln_g.reshape(1, D), ln_b.reshape(1, D))


def _shift_delta(u, n_ctx):
    def seg(x):
        xp = jnp.pad(x, ((1, 1), (0, 0), (0, 0)))
        return 0.5 * (xp[:-2] + xp[2:]) - x
    return jnp.concatenate([seg(u[:n_ctx]), seg(u[n_ctx:])], axis=0)


def kernel(x, c, ctx, c_ctx, ada_w, ada_b, post_ln_g, post_ln_b, att_w_in, att_w_out, att_sink, diff_lambda_vecs, diff_subln_g, rk_mu, rk_w_rkv, rk_w_out, rk_decay0, rk_decay1, rk_decay2, rk_iclr0, rk_iclr1, rk_iclr2, rk_gate1, rk_gate2, rk_k_k, rk_k_a, rk_r_k, rk_lnx, moe_router, moe_bias, moe_w_in, moe_w_out, moe_ws_in, moe_ws_out):
    B, S, D = x.shape
    L = ctx.shape[1]
    N = L + S
    tm = ROW_TILE
    assert L % tm == 0 and S % tm == 0 and L % SCAN_CHUNK == 0 and S % SCAN_CHUNK == 0
    assert tm % B == 0 and D % SCAN_COLS == 0

    rows = -(-(B + 1) // 8) * 8
    cvec = jnp.concatenate([c, c_ctx[None, :], jnp.zeros((rows - B - 1, D), F32)], axis=0)
    mods = [_mod_table(_ada_mod(cvec, ada_w[i], ada_b[i]), B, D) for i in range(DEPTH)]

    h0 = jnp.concatenate([ctx, x], axis=1)
    lam_init = 0.8 - 0.6 * math.exp(-0.3 * 0)
    qa, ka, va, qb, kb, vb = _attn_inproj(h0, mods[0], att_w_in[0], L)
    oa = _win_attn(qa, ka, va, att_sink[0], L)
    ob = _diff_attn(qb, kb, vb, diff_lambda_vecs[0], diff_subln_g[0], lam_init, L)
    h1, u1 = _mix_out([oa, ob], [att_w_out[0][:A_WIDTH], att_w_out[0][A_WIDTH:]], h0, mods[0],
                      post_ln_g[0, 0], post_ln_b[0, 0], L, 0)
    tiles_b, tiles_c = N // tm, L // tm
    gate0 = mods[0][:, :, 5].reshape(B * 2, 1, D)
    gate0_index = lambda i: (i // tiles_b) * 2 + jnp.minimum((i % tiles_b) // tiles_c, 1)
    h2 = _moe_layer(u1.reshape(B * N, D // 2), h1.reshape(B * N, D), gate0, gate0_index, moe_router[0], moe_bias[0],
                    moe_w_in, moe_w_out, moe_ws_in[0], moe_ws_out[0],
                    post_ln_g[0, 1], post_ln_b[0, 1], 0).reshape(B, N, D)

    m_ctx, m_lat = mods[1][:, 0], mods[1][:, 1]
    h2_t = jnp.swapaxes(h2, 0, 1)
    is_lat = (jnp.arange(N) >= L)[:, None, None]
    u = h2_t * (1.0 + jnp.where(is_lat, m_lat[:, 1], m_ctx[:, 1])) + jnp.where(is_lat, m_lat[:, 0], m_ctx[:, 0])
    dx = _shift_delta(u, L)
    params = dict(mu=rk_mu[0], w_rkv=rk_w_rkv[0], gate1=rk_gate1[0], gate2=rk_gate2[0],
                  dec0=rk_decay0[0], dec1=rk_decay1[0], dec2=rk_decay2[0],
                  icl0=rk_iclr0[0], icl1=rk_iclr1[0], icl2=rk_iclr2[0],
                  k_k=rk_k_k[0], k_a=rk_k_a[0], r_k=rk_r_k[0])
    r, v, a, g, bonus, w2, k2, b2 = _rwkv_proj(u.reshape(N * B, D), dx.reshape(N * B, D), params)
    tmaj = lambda t: t.reshape(t.shape[:-2] + (N, B, D))
    y = _wkv_scan(tmaj(r), tmaj(w2), tmaj(k2), tmaj(v), tmaj(a), tmaj(b2), L)
    lat_rows = lambda j: jnp.tile(m_lat[:, j], (tm // B, 1))
    h3, u3 = _rwkv_out(y.reshape(2, N * B, D), bonus, g, rk_lnx[0], rk_w_out[0], h2_t.reshape(N * B, D),
                       jnp.stack([lat_rows(2), lat_rows(3), lat_rows(4)]),
                       post_ln_g[1, 0], post_ln_b[1, 0], L * B)
    out = _moe_layer(u3, h3, lat_rows(5)[None], lambda i: 0, moe_router[1], moe_bias[1],
                     moe_w_in, moe_w_out, moe_ws_in[1], moe_ws_out[1],
                     post_ln_g[1, 1], post_ln_b[1, 1], 1)
    return jnp.swapaxes(out.reshape(S, B, D), 0, 1)
```

```python
import functools
import math

import numpy as np
import jax
import jax.numpy as jnp
from jax import lax
from jax.experimental import pallas as pl
from jax.experimental.pallas import tpu as pltpu

F32 = jnp.float32
BF16 = jnp.bfloat16
I32 = jnp.int32

HEAD_DIM = 64
GRID_W = 64
ROPE_AXIS_DIM = HEAD_DIM // 2
ROPE_THETA = 10000.0
Q_BLOCK = 128
A_Q_HEADS = 8
A_KV_HEADS = 2
A_GROUP = A_Q_HEADS // A_KV_HEADS
A_WIDTH = A_Q_HEADS * HEAD_DIM
A_KV_WIDTH = A_KV_HEADS * HEAD_DIM
B_HEADS = 4
B_V_DIM = 2 * HEAD_DIM
B_WIDTH = B_HEADS * B_V_DIM
LNX_EPS = 64e-5
N_EXPERTS = 256
TOP_K = 8
N_GROUPS = 8
TOPK_GROUPS = 4
ROUTED_SCALE = 2.5
MOE_BLOCK = 256
LN_EPS = 1e-5
SUBLN_EPS = 1e-5
NEG_INF = -1e30
DEPTH = 2
DEEPNORM_ALPHA = (2 * DEPTH) ** 0.25

LANES = 128
ROW_TILE = 256
PROJ_TILE = 128
SCAN_CHUNK = 64
SCAN_COLS = 1024
SCAN_TILE = 256
SCAN_GROUP = 8
SCAN_UNROLL = 4
VMEM_LIMIT = 56 * 1024 * 1024


def _cparams(sem):
    return pltpu.CompilerParams(dimension_semantics=sem, vmem_limit_bytes=VMEM_LIMIT)


def _silu(x):
    return x * jax.nn.sigmoid(x)


def _layer_norm(z, g, b):
    mu = jnp.mean(z, -1, keepdims=True)
    zc = z - mu
    var = jnp.mean(zc * zc, -1, keepdims=True)
    return zc * lax.rsqrt(var + LN_EPS) * g + b


def _dot(a, b):
    return jnp.dot(a, b, preferred_element_type=F32)


def _dot_nt(a, b):
    return lax.dot_general(a, b, (((1,), (1,)), ((), ())), preferred_element_type=F32)


def _pack_halves(x):
    half = x.shape[1] // 2
    bits = lambda v: lax.bitcast_convert_type(v.astype(BF16).astype(F32), I32)
    return lax.shift_right_logical(bits(x[:, :half]), 16) | bits(x[:, half:])


def _unpack_halves(p):
    lo = lax.bitcast_convert_type(lax.shift_left(p, 16), F32)
    hi = lax.bitcast_convert_type(p & jnp.int32(-65536), F32)
    return lo, hi


def _dot_halves(p, w_ref_or_array):
    lo, hi = _unpack_halves(p)
    half = p.shape[1]
    return _dot(lo.astype(BF16), w_ref_or_array[:half]) + _dot(hi.astype(BF16), w_ref_or_array[half:])


def _ada_kernel(c_ref, w_ref, b_ref, o_ref):
    c = c_ref[...]
    o_ref[...] = jnp.dot(_silu(c), w_ref[...], preferred_element_type=F32,
                         precision=lax.Precision.HIGHEST) + b_ref[...]


def _ada_mod(cvec, w, bias):
    R, D = cvec.shape
    n_out = w.shape[1]
    tn = 768
    return pl.pallas_call(
        _ada_kernel,
        grid=(n_out // tn,),
        in_specs=[pl.BlockSpec((R, D), lambda j: (0, 0)),
                  pl.BlockSpec((D, tn), lambda j: (0, j)),
                  pl.BlockSpec((1, tn), lambda j: (0, j))],
        out_specs=pl.BlockSpec((R, tn), lambda j: (0, j)),
        out_shape=jax.ShapeDtypeStruct((R, n_out), F32),
        compiler_params=_cparams(("arbitrary",)),
    )(cvec, w, bias.reshape(1, n_out))


def _mod_table(m, batch, d):
    m_lat = m[:batch].reshape(batch, 6, d)
    m_ctx = jnp.broadcast_to(m[batch].reshape(1, 6, d), (batch, 6, d))
    return jnp.stack([m_ctx, m_lat], axis=1)


def _mod_spec(d, ctx_tiles):
    return pl.BlockSpec((1, 1, 6, d), lambda b, i: (b, jnp.minimum(i // ctx_tiles, 1), 0, 0))


def _rope_tables(n_ctx, n_lat):
    rows = n_lat // GRID_W
    row = np.repeat(np.arange(rows), GRID_W).astype(np.float32)
    col = np.tile(np.arange(GRID_W), rows).astype(np.float32)
    inv = (ROPE_THETA ** (-np.arange(0, ROPE_AXIS_DIM, 2, dtype=np.float32) / ROPE_AXIS_DIM)).astype(np.float32)
    ar = row[:, None] * inv
    ac = col[:, None] * inv
    ang = np.concatenate([ar, ar, ac, ac], -1)
    cos = np.cos(ang).astype(np.float32)
    sin = np.sin(ang).astype(np.float32)
    lower = (np.arange(HEAD_DIM) % ROPE_AXIS_DIM) < (ROPE_AXIS_DIM // 2)
    sin_up = np.where(lower[None, :], -sin, 0.0)
    sin_dn = np.where(lower[None, :], 0.0, sin)

    def full(t, ctx_fill):
        t = np.concatenate([np.full((n_ctx, HEAD_DIM), ctx_fill, np.float32), t], 0)
        return jnp.asarray(np.tile(t, (1, LANES // HEAD_DIM)))

    return full(cos, 1.0), full(sin_up, 0.0), full(sin_dn, 0.0)


def _inproj_kernel(h_ref, mod_ref, w_ref, cos_ref, su_ref, sd_ref,
                   qa_ref, ka_ref, va_ref, qb_ref, kb_ref, vb_ref):
    h = h_ref[0]
    shift = mod_ref[0, 0, 0:1, :]
    scale = mod_ref[0, 0, 1:2, :]
    u = (h * (1.0 + scale) + shift).astype(BF16)
    y = _dot(u, w_ref[...])
    cos, s_up, s_dn = cos_ref[...], su_ref[...], sd_ref[...]
    q_scale = HEAD_DIM ** -0.5

    def rope(xc):
        half = ROPE_AXIS_DIM // 2
        return xc * cos + pltpu.roll(xc, LANES - half, 1) * s_up + pltpu.roll(xc, half, 1) * s_dn

    def emit(out_ref, col0, width, roped, mul):
        for j in range(width // LANES):
            xc = y[:, col0 + j * LANES: col0 + (j + 1) * LANES]
            if roped:
                xc = rope(xc)
            if mul != 1.0:
                xc = xc * mul
            out_ref[0, :, j * LANES:(j + 1) * LANES] = xc.astype(out_ref.dtype)

    c = 0
    emit(qa_ref, c, A_WIDTH, True, q_scale); c += A_WIDTH
    emit(ka_ref, c, A_KV_WIDTH, True, 1.0); c += A_KV_WIDTH
    emit(va_ref, c, A_KV_WIDTH, False, 1.0); c += A_KV_WIDTH
    emit(qb_ref, c, B_WIDTH, True, q_scale); c += B_WIDTH
    emit(kb_ref, c, B_WIDTH, True, 1.0); c += B_WIDTH
    emit(vb_ref, c, B_WIDTH, False, 1.0)


def _attn_inproj(h, mod, w_in, n_ctx):
    B, N, D = h.shape
    tm = ROW_TILE
    cos, s_up, s_dn = _rope_tables(n_ctx, N - n_ctx)
    widths = (A_WIDTH, A_KV_WIDTH, A_KV_WIDTH, B_WIDTH, B_WIDTH, B_WIDTH)
    tab_spec = pl.BlockSpec((tm, LANES), lambda b, i: (i, 0))
    return pl.pallas_call(
        _inproj_kernel,
        grid=(B, N // tm),
        in_specs=[pl.BlockSpec((1, tm, D), lambda b, i: (b, i, 0)),
                  _mod_spec(D, n_ctx // tm),
                  pl.BlockSpec(w_in.shape, lambda b, i: (0, 0)),
                  tab_spec, tab_spec, tab_spec],
        out_specs=[pl.BlockSpec((1, tm, w), lambda b, i: (b, i, 0)) for w in widths],
        out_shape=[jax.ShapeDtypeStruct((B, N, w), BF16) for w in widths],
        compiler_params=_cparams(("arbitrary", "arbitrary")),
    )(h, mod, w_in.astype(BF16), cos, s_up, s_dn)


def _win_attn_kernel(n_ctx_blocks, n_blocks, q_ref, kc_ref, vc_ref, kl_ref, km_ref, kr_ref,
                     vl_ref, vm_ref, vr_ref, sink_ref, o_ref):
    j = pl.program_id(1)
    is_lat = j >= n_ctx_blocks
    qb = Q_BLOCK
    n_c = kc_ref.shape[1]
    rows = A_GROUP * qb
    n_keys = n_c + 3 * qb
    far = 1 << 20
    r_idx = lax.broadcasted_iota(I32, (rows, n_keys), 0) % qb
    cw = lax.broadcasted_iota(I32, (rows, n_keys), 1) - n_c
    off_l = jnp.where(jnp.logical_and(is_lat, j > n_ctx_blocks), 0, far)
    end_m = jnp.where(is_lat, 2 * qb, qb)
    off_r = jnp.where(jnp.logical_and(is_lat, j < n_blocks - 1), 0, far)
    valid = ((cw < 0)
             | ((cw >= 0) & (cw < qb) & (cw >= r_idx + off_l))
             | ((cw >= qb) & (cw < end_m))
             | ((cw >= 2 * qb) & (cw - 2 * qb + off_r <= r_idx)))
    outs = []
    for kv in range(A_KV_HEADS):
        cols = slice(kv * HEAD_DIM, (kv + 1) * HEAD_DIM)
        k_all = jnp.concatenate([kc_ref[0, :, cols], kl_ref[0, :, cols], km_ref[0, :, cols],
                                 kr_ref[0, :, cols]], axis=0)
        v_all = jnp.concatenate([vc_ref[0, :, cols], vl_ref[0, :, cols], vm_ref[0, :, cols],
                                 vr_ref[0, :, cols]], axis=0)
        q0 = kv * A_GROUP
        q = jnp.concatenate([q_ref[0, :, (q0 + g) * HEAD_DIM:(q0 + g + 1) * HEAD_DIM]
                             for g in range(A_GROUP)], axis=0)
        sink = jnp.concatenate([jnp.broadcast_to(sink_ref[q0 + g:q0 + g + 1, 0:1], (qb, 1))
                                for g in range(A_GROUP)], axis=0)
        s = jnp.where(valid, _dot_nt(q, k_all), NEG_INF)
        m = jnp.maximum(jnp.max(s, -1, keepdims=True), sink)
        e = jnp.exp(s - m)
        denom = jnp.sum(e, -1, keepdims=True) + jnp.exp(sink - m)
        p = (e / denom).astype(BF16)
        o = _dot(p, v_all)
        outs += [o[g * qb:(g + 1) * qb] for g in range(A_GROUP)]
    for j2 in range(A_Q_HEADS // 2):
        pair = jnp.concatenate([outs[2 * j2], outs[2 * j2 + 1]], axis=1)
        o_ref[0, :, j2 * LANES:(j2 + 1) * LANES] = pair.astype(o_ref.dtype)


def _win_attn(qa, ka, va, sink, n_ctx):
    B, N, _ = qa.shape
    qb = Q_BLOCK
    nb = N // qb
    ncb = n_ctx // qb
    sink_pad = jnp.broadcast_to(sink.reshape(A_Q_HEADS, 1).astype(F32), (A_Q_HEADS, LANES))

    def left(b, j):
        return (b, jnp.clip(j - 1, ncb, nb - 1), 0)

    def mid(b, j):
        return (b, jnp.clip(j, ncb, nb - 1), 0)

    def right(b, j):
        return (b, jnp.clip(j + 1, ncb, nb - 1), 0)

    kv_blk = lambda im: pl.BlockSpec((1, qb, A_KV_WIDTH), im)
    ctx_blk = pl.BlockSpec((1, n_ctx, A_KV_WIDTH), lambda b, j: (b, 0, 0))
    return pl.pallas_call(
        functools.partial(_win_attn_kernel, ncb, nb),
        grid=(B, nb),
        in_specs=[pl.BlockSpec((1, qb, A_WIDTH), lambda b, j: (b, j, 0)),
                  ctx_blk, ctx_blk,
                  kv_blk(left), kv_blk(mid), kv_blk(right),
                  kv_blk(left), kv_blk(mid), kv_blk(right),
                  pl.BlockSpec((A_Q_HEADS, LANES), lambda b, j: (0, 0))],
        out_specs=pl.BlockSpec((1, qb, A_WIDTH), lambda b, j: (b, j, 0)),
        out_shape=jax.ShapeDtypeStruct((B, N, A_WIDTH), BF16),
        compiler_params=_cparams(("arbitrary", "arbitrary")),
    )(qa, ka, va, ka, ka, ka, va, va, va, sink_pad)


def _diff_attn_kernel(n_ctx, lam_init, q_ref, k_ref, v_ref, lv_ref, g_ref, o_ref):
    j = pl.program_id(1)
    lv = lv_ref[...]
    lam = (jnp.exp(jnp.sum(lv[0:1] * lv[1:2], -1, keepdims=True))
           - jnp.exp(jnp.sum(lv[2:3] * lv[3:4], -1, keepdims=True)) + lam_init)
    gain = g_ref[...] * (1.0 - lam_init)

    def run(n_keys):
        for hd in range(B_HEADS):
            parts = []
            for mm in range(2):
                c0 = (hd * 2 + mm) * HEAD_DIM
                q = q_ref[0, :, c0:c0 + HEAD_DIM]
                k = k_ref[0, :n_keys, c0:c0 + HEAD_DIM]
                s = _dot_nt(q, k)
                e = jnp.exp(s - jnp.max(s, -1, keepdims=True))
                parts.append((e, jnp.sum(e, -1, keepdims=True)))
            (e0, l0), (e1, l1) = parts
            a = (e0 * (1.0 / l0) - e1 * (lam / l1)).astype(BF16)
            o = _dot(a, v_ref[0, :n_keys, hd * B_V_DIM:(hd + 1) * B_V_DIM])
            o = o * lax.rsqrt(jnp.mean(o * o, -1, keepdims=True) + SUBLN_EPS) * gain
            o_ref[0, :, hd * B_V_DIM:(hd + 1) * B_V_DIM] = o.astype(o_ref.dtype)

    @pl.when(j == 0)
    def _():
        run(n_ctx)

    @pl.when(j > 0)
    def _():
        run(k_ref.shape[1])


def _diff_attn(qb, kb, vb, lam_vecs, subln_g, lam_init, n_ctx):
    B, N, _ = qb.shape
    tq = n_ctx
    return pl.pallas_call(
        functools.partial(_diff_attn_kernel, n_ctx, lam_init),
        grid=(B, N // tq),
        in_specs=[pl.BlockSpec((1, tq, B_WIDTH), lambda b, j: (b, j, 0)),
                  pl.BlockSpec((1, N, B_WIDTH), lambda b, j: (b, 0, 0)),
                  pl.BlockSpec((1, N, B_WIDTH), lambda b, j: (b, 0, 0)),
                  pl.BlockSpec((4, HEAD_DIM), lambda b, j: (0, 0)),
                  pl.BlockSpec((1, B_V_DIM), lambda b, j: (0, 0))],
        out_specs=pl.BlockSpec((1, tq, B_WIDTH), lambda b, j: (b, j, 0)),
        out_shape=jax.ShapeDtypeStruct((B, N, B_WIDTH), BF16),
        compiler_params=_cparams(("arbitrary", "arbitrary")),
    )(qb, kb, vb, lam_vecs.astype(F32), subln_g.reshape(1, B_V_DIM).astype(F32))


def _mix_out_kernel(n_in, *refs):
    xs = refs[:n_in]
    ws = refs[n_in:2 * n_in]
    h_ref, mod_ref, g_ref, b_ref, hn_ref, u_ref = refs[2 * n_in:]
    o = _dot(xs[0][0], ws[0][...])
    for x_ref, w_ref in zip(xs[1:], ws[1:]):
        o = o + _dot(x_ref[0], w_ref[...])
    z = DEEPNORM_ALPHA * h_ref[0] + mod_ref[0, 0, 2:3, :] * o
    hn = _layer_norm(z, g_ref[...], b_ref[...])
    hn_ref[0] = hn
    u_ref[0] = _pack_halves(hn * (1.0 + mod_ref[0, 0, 4:5, :]) + mod_ref[0, 0, 3:4, :])


def _mix_out(xs, ws, h, mod, ln_g, ln_b, n_ctx, row0):
    B, N, D = h.shape
    tm = ROW_TILE
    t0 = row0 // tm
    n_out = N - row0
    row_spec = lambda w: pl.BlockSpec((1, tm, w), lambda b, i: (b, i + t0, 0))
    out_spec = pl.BlockSpec((1, tm, D), lambda b, i: (b, i, 0))
    vec_spec = pl.BlockSpec((1, D), lambda b, i: (0, 0))
    return pl.pallas_call(
        functools.partial(_mix_out_kernel, len(xs)),
        grid=(B, n_out // tm),
        in_specs=([row_spec(x.shape[-1]) for x in xs]
                  + [pl.BlockSpec(w.shape, lambda b, i: (0, 0)) for w in ws]
                  + [row_spec(D),
                     pl.BlockSpec((1, 1, 6, D), lambda b, i: (b, jnp.minimum((i + t0) // (n_ctx // tm), 1), 0, 0)),
                     vec_spec, vec_spec]),
        out_specs=[out_spec, pl.BlockSpec((1, tm, D // 2), lambda b, i: (b, i, 0))],
        out_shape=[jax.ShapeDtypeStruct((B, n_out, D), F32), jax.ShapeDtypeStruct((B, n_out, D // 2), I32)],
        compiler_params=_cparams(("arbitrary", "arbitrary")),
    )(*xs, *[w.astype(BF16) for w in ws], h, mod, ln_g.reshape(1, D), ln_b.reshape(1, D))


def _router_kernel(u_ref, rt_ref, bias_ref, tri_ref, e_ref, gw_ref, rank_ref, cnt_ref, carry_ref):
    i = pl.program_id(0)

    @pl.when(i == 0)
    def _():
        carry_ref[...] = jnp.zeros_like(carry_ref)

    tm = u_ref.shape[0]
    per_group = N_EXPERTS // N_GROUPS
    neg = -jnp.inf
    u_lo, u_hi = _unpack_halves(u_ref[...])
    half = u_ref.shape[1]
    logits = (_dot_nt(rt_ref[:, :half], u_lo.astype(BF16))
              + _dot_nt(rt_ref[:, half:], u_hi.astype(BF16)))
    scores = jax.nn.sigmoid(logits)
    sel = scores + bias_ref[...]
    io_in = lax.broadcasted_iota(I32, (per_group, tm), 0)
    grp_rows = []
    for gi in range(N_GROUPS):
        sg = sel[gi * per_group:(gi + 1) * per_group]
        m1 = jnp.max(sg, axis=0, keepdims=True)
        i1 = jnp.min(jnp.where(sg == m1, io_in, per_group), axis=0, keepdims=True)
        m2 = jnp.max(jnp.where(io_in == i1, neg, sg), axis=0, keepdims=True)
        grp_rows.append(m1 + m2)
    grp = jnp.concatenate(grp_rows, axis=0)
    io_g = lax.broadcasted_iota(I32, grp.shape, 0)
    g_sel = jnp.zeros(grp.shape, F32)
    for _ in range(TOPK_GROUPS):
        m = jnp.max(grp, axis=0, keepdims=True)
        hit = io_g == jnp.min(jnp.where(grp == m, io_g, N_GROUPS), axis=0, keepdims=True)
        g_sel = jnp.where(hit, 1.0, g_sel)
        grp = jnp.where(hit, neg, grp)
    selm = jnp.concatenate(
        [jnp.where(g_sel[gi:gi + 1] > 0.5, sel[gi * per_group:(gi + 1) * per_group], NEG_INF)
         for gi in range(N_GROUPS)], axis=0)
    io_e = lax.broadcasted_iota(I32, selm.shape, 0)
    chosen_f = jnp.zeros(selm.shape, F32)
    idx, gws = [], []
    for _ in range(TOP_K):
        m = jnp.max(selm, axis=0, keepdims=True)
        ik = jnp.min(jnp.where(selm == m, io_e, N_EXPERTS), axis=0, keepdims=True)
        hit = io_e == ik
        idx.append(ik)
        gws.append(jnp.sum(jnp.where(hit, scores, 0.0), axis=0, keepdims=True))
        chosen_f = jnp.where(hit, 1.0, chosen_f)
        selm = jnp.where(hit, neg, selm)
    gw = jnp.concatenate(gws, axis=0)
    gw_ref[...] = gw / jnp.sum(gw, axis=0, keepdims=True) * ROUTED_SCALE
    e_ref[...] = jnp.concatenate(idx, axis=0)
    before = _dot(chosen_f.astype(BF16), tri_ref[...]) + carry_ref[...]
    ranks = [jnp.sum(jnp.where(io_e == ik, before, 0.0), axis=0, keepdims=True) for ik in idx]
    rank_ref[...] = jnp.concatenate(ranks, axis=0).astype(I32)
    carry_ref[...] = carry_ref[...] + jnp.sum(chosen_f, axis=1, keepdims=True)
    cnt_ref[...] = carry_ref[...].astype(I32)


def _router(u, router, bias):
    T = u.shape[0]
    D = router.shape[0]
    tm = ROW_TILE
    tri = jnp.asarray(np.triu(np.ones((tm, tm), np.float32), 1), BF16)
    tok_spec = pl.BlockSpec((TOP_K, tm), lambda i: (0, i))
    return pl.pallas_call(
        _router_kernel,
        grid=(T // tm,),
        in_specs=[pl.BlockSpec((tm, D // 2), lambda i: (i, 0)),
                  pl.BlockSpec((N_EXPERTS, D), lambda i: (0, 0)),
                  pl.BlockSpec((N_EXPERTS, 1), lambda i: (0, 0)),
                  pl.BlockSpec((tm, tm), lambda i: (0, 0))],
        out_specs=[tok_spec, tok_spec, tok_spec, pl.BlockSpec((N_EXPERTS, 1), lambda i: (0, 0))],
        out_shape=[jax.ShapeDtypeStruct((TOP_K, T), I32), jax.ShapeDtypeStruct((TOP_K, T), F32),
                   jax.ShapeDtypeStruct((TOP_K, T), I32), jax.ShapeDtypeStruct((N_EXPERTS, 1), I32)],
        scratch_shapes=[pltpu.VMEM((N_EXPERTS, 1), F32)],
        compiler_params=_cparams(("arbitrary",)),
    )(u, router.T.astype(BF16), bias.reshape(N_EXPERTS, 1).astype(F32), tri)


def _dispatch_kernel(dest_ref, x_ref, xs_in, xs_hbm, sem):
    del xs_in
    tm = dest_ref.shape[1]

    def row_copy(t, k):
        return pltpu.make_async_copy(x_ref.at[pl.ds(t, 1)], xs_hbm.at[pl.ds(dest_ref[k, t], 1)], sem)

    def start(t, carry):
        for k in range(TOP_K):
            row_copy(t, k).start()
        return carry

    def wait(t, carry):
        for k in range(TOP_K):
            row_copy(t, k).wait()
        return carry

    lax.fori_loop(0, tm, start, 0)
    lax.fori_loop(0, tm, wait, 0)


def _dispatch(x, dest, n_rows):
    T, D = x.shape
    tm = ROW_TILE
    xs0 = jnp.zeros((n_rows, D), x.dtype)
    return pl.pallas_call(
        _dispatch_kernel,
        grid=(T // tm,),
        in_specs=[pl.BlockSpec((TOP_K, tm), lambda i: (0, i), memory_space=pltpu.SMEM),
                  pl.BlockSpec((tm, D), lambda i: (i, 0)),
                  pl.BlockSpec(memory_space=pl.ANY)],
        out_specs=pl.BlockSpec(memory_space=pl.ANY),
        out_shape=jax.ShapeDtypeStruct((n_rows, D), x.dtype),
        scratch_shapes=[pltpu.SemaphoreType.DMA(())],
        input_output_aliases={2: 0},
        compiler_params=_cparams(("arbitrary",)),
    )(dest, x, xs0)


def _expert_kernel(be_ref, nu_ref, x_ref, wi_ref, wo_ref, y_ref, wi_b, wo_b):
    i = pl.program_id(0)

    @pl.when(i < nu_ref[0])
    def _():
        @pl.when(jnp.logical_or(i == 0, be_ref[i] != be_ref[jnp.maximum(i - 1, 0)]))
        def _():
            wi_b[...] = wi_ref[0, 0].astype(BF16)
            wo_b[...] = wo_ref[0, 0].astype(BF16)

        ff = wo_b.shape[0]
        hcat = _dot_halves(x_ref[...], wi_b)
        act = (_silu(hcat[:, :ff]) * hcat[:, ff:]).astype(BF16)
        y_ref[...] = _pack_halves(_dot(act, wo_b[...]))

    @pl.when(i >= nu_ref[0])
    def _():
        y_ref[...] = jnp.zeros_like(y_ref)


def _experts(xs, block_exp, n_used, w_in, w_out, layer):
    P, half = xs.shape
    D, ff2 = w_in.shape[-2:]
    nblk = P // MOE_BLOCK
    row_spec = pl.BlockSpec((MOE_BLOCK, half), lambda i, be, nu: (i, 0))
    return pl.pallas_call(
        _expert_kernel,
        grid_spec=pltpu.PrefetchScalarGridSpec(
            num_scalar_prefetch=2,
            grid=(nblk,),
            in_specs=[row_spec,
                      pl.BlockSpec((1, 1, D, ff2), lambda i, be, nu: (layer, be[i], 0, 0)),
                      pl.BlockSpec((1, 1, ff2 // 2, D), lambda i, be, nu: (layer, be[i], 0, 0))],
            out_specs=row_spec,
            scratch_shapes=[pltpu.VMEM((D, ff2), BF16), pltpu.VMEM((ff2 // 2, D), BF16)]),
        out_shape=jax.ShapeDtypeStruct((P, half), I32),
        compiler_params=_cparams(("arbitrary",)),
    )(block_exp, n_used, xs, w_in, w_out)


def _combine_kernel(dest_ref, y_hbm, gw_ref, u_ref, wsi_ref, wso_ref, h_ref, mod_ref, g_ref, b_ref,
                    o_ref, buf, sem):
    tm = u_ref.shape[0]

    def row_copy(t, k):
        return pltpu.make_async_copy(y_hbm.at[pl.ds(dest_ref[k, t], 1)], buf.at[k, pl.ds(t, 1)], sem)

    def start(t, carry):
        for k in range(TOP_K):
            row_copy(t, k).start()
        return carry

    def wait(t, carry):
        for k in range(TOP_K):
            row_copy(t, k).wait()
        return carry

    lax.fori_loop(0, tm, start, 0)
    ff = wso_ref.shape[0]
    hcat = _dot_halves(u_ref[...], wsi_ref)
    shared = _dot((_silu(hcat[:, :ff]) * hcat[:, ff:]).astype(BF16), wso_ref[...])
    lax.fori_loop(0, tm, wait, 0)
    lo, hi = None, None
    for k in range(TOP_K):
        y_lo, y_hi = _unpack_halves(buf[k])
        gk = gw_ref[:, k:k + 1]
        lo = y_lo * gk if lo is None else lo + y_lo * gk
        hi = y_hi * gk if hi is None else hi + y_hi * gk
    routed = jnp.concatenate([lo, hi], axis=1)
    z = DEEPNORM_ALPHA * h_ref[...] + mod_ref[0] * (routed + shared)
    o_ref[...] = _layer_norm(z, g_ref[...], b_ref[...])


def _combine(y, dest, gw_t, u, ws_in, ws_out, h, gate, gate_index, ln_g, ln_b):
    T, D = h.shape
    tm = ROW_TILE
    vec_spec = pl.BlockSpec((1, D), lambda i: (0, 0))
    row_spec = pl.BlockSpec((tm, D), lambda i: (i, 0))
    packed_spec = pl.BlockSpec((tm, D // 2), lambda i: (i, 0))
    return pl.pallas_call(
        _combine_kernel,
        grid=(T // tm,),
        in_specs=[pl.BlockSpec((TOP_K, tm), lambda i: (0, i), memory_space=pltpu.SMEM),
                  pl.BlockSpec(memory_space=pl.ANY),
                  pl.BlockSpec((tm, TOP_K), lambda i: (i, 0)),
                  packed_spec,
                  pl.BlockSpec(ws_in.shape, lambda i: (0, 0)),
                  pl.BlockSpec(ws_out.shape, lambda i: (0, 0)),
                  row_spec,
                  pl.BlockSpec((1,) + gate.shape[1:], lambda i: (gate_index(i), 0, 0)),
                  vec_spec, vec_spec],
        out_specs=row_spec,
        out_shape=jax.ShapeDtypeStruct((T, D), F32),
        scratch_shapes=[pltpu.VMEM((TOP_K, tm, D // 2), I32), pltpu.SemaphoreType.DMA(())],
        compiler_params=_cparams(("arbitrary",)),
    )(dest, y, gw_t, u, ws_in.astype(BF16), ws_out.astype(BF16), h, gate,
      ln_g.reshape(1, D), ln_b.reshape(1, D))


def _slots_kernel(e_ref, rank_ref, start_ref, dest_ref):
    io_e = lax.broadcasted_iota(I32, (N_EXPERTS, e_ref.shape[1]), 0)
    rows = [jnp.sum(jnp.where(io_e == e_ref[k:k + 1, :], start_ref[...], 0), axis=0, keepdims=True)
            for k in range(TOP_K)]
    dest_ref[...] = jnp.concatenate(rows, axis=0) + rank_ref[...]


def _slots(eidx, rank, pstart):
    T = eidx.shape[1]
    tm = ROW_TILE
    tok_spec = pl.BlockSpec((TOP_K, tm), lambda i: (0, i))
    return pl.pallas_call(
        _slots_kernel,
        grid=(T // tm,),
        in_specs=[tok_spec, tok_spec, pl.BlockSpec((N_EXPERTS, 1), lambda i: (0, 0))],
        out_specs=tok_spec,
        out_shape=jax.ShapeDtypeStruct((TOP_K, T), I32),
        compiler_params=_cparams(("arbitrary",)),
    )(eidx, rank, pstart.reshape(N_EXPERTS, 1))


def _moe_layer(u, h, gate, gate_index, router, bias, w_in, w_out, ws_in, ws_out, ln_g, ln_b, layer):
    T = u.shape[0]
    eidx, gw, rank, counts = _router(u, router, bias)
    counts = counts[:, 0]
    padded = (counts + MOE_BLOCK - 1) // MOE_BLOCK * MOE_BLOCK
    pend = jnp.cumsum(padded)
    pstart = (pend - padded).astype(I32)
    dest = _slots(eidx, rank, pstart)
    n_blocks = -(-(T * TOP_K + N_EXPERTS * (MOE_BLOCK - 1)) // MOE_BLOCK)
    block_exp = jnp.minimum(jnp.searchsorted(pend, jnp.arange(n_blocks, dtype=I32) * MOE_BLOCK, side='right'),
                            N_EXPERTS - 1).astype(I32)
    n_used = (pend[-1:] // MOE_BLOCK).astype(I32)
    xs = _dispatch(u, dest, n_blocks * MOE_BLOCK)
    y = _experts(xs, block_exp, n_used, w_in, w_out, layer)
    return _combine(y, dest, gw.T, u, ws_in, ws_out, h, gate, gate_index, ln_g, ln_b)


def _seg_ones(width=LANES):
    idx = np.arange(width) // HEAD_DIM
    return jnp.asarray((idx[:, None] == idx[None, :]).astype(np.float32), BF16)


def _head_sum(x, ones_ref):
    outs = []
    for j in range(x.shape[1] // LANES):
        xc = x[:, j * LANES:(j + 1) * LANES]
        hi = xc.astype(BF16)
        lo = (xc - hi.astype(F32)).astype(BF16)
        outs.append(_dot(hi, ones_ref[...]) + _dot(lo, ones_ref[...]))
    return jnp.concatenate(outs, axis=1)


def _rwkv_proj_kernel(u_ref, dx_ref, mu_ref, wrkv_ref, g1_ref, g2_ref, d1_ref, d2_ref, d0_ref,
                      i1_ref, i2_ref, i0_ref, kk_ref, ka_ref, rk_ref, ones_ref,
                      r_ref, v_ref, a_ref, g_ref, bonus_ref, w_ref, k_ref, b_ref):
    u = u_ref[...]
    dx = dx_ref[...]
    mix = lambda m: (u + dx * mu_ref[m:m + 1, :])
    xr, xw, xk, xv, xa, xg = [mix(m) for m in range(6)]
    r = _dot(xr.astype(BF16), wrkv_ref[0])
    k = _dot(xk.astype(BF16), wrkv_ref[1])
    v = _dot(xv.astype(BF16), wrkv_ref[2])
    g = _dot(jax.nn.sigmoid(_dot(xg.astype(BF16), g1_ref[...])).astype(BF16), g2_ref[...])
    kk = k * kk_ref[...]
    kk = kk * lax.rsqrt(jnp.maximum(_head_sum(kk * kk, ones_ref), 1e-24))
    r_ref[...] = r
    v_ref[...] = v
    a_ref[...] = -kk
    g_ref[...] = g
    k_sum = None
    xw_b = xw.astype(BF16)
    xa_b = xa.astype(BF16)
    for d in range(2):
        lw = d0_ref[d:d + 1, :] + _dot(jnp.tanh(_dot(xw_b, d1_ref[d])).astype(BF16), d2_ref[d])
        softplus = jnp.maximum(-lw, 0.0) + jnp.log(1.0 + jnp.exp(-jnp.abs(lw)))
        logw = -softplus - 0.5
        w_ref[d] = jnp.exp(-jnp.exp(logw))
        eta = jax.nn.sigmoid(i0_ref[d:d + 1, :] + _dot(_dot(xa_b, i1_ref[d]).astype(BF16), i2_ref[d]))
        k_d = k * (1.0 + (eta - 1.0) * ka_ref[...])
        k_ref[d] = k_d
        b_ref[d] = kk * eta
        k_sum = k_d if k_sum is None else k_sum + k_d
    bonus_ref[...] = _head_sum(r * k_sum * rk_ref[...], ones_ref) * v


def _rwkv_proj(u, dx, p):
    T, D = u.shape
    tm = PROJ_TILE
    row = pl.BlockSpec((tm, D), lambda i: (i, 0))
    row2 = pl.BlockSpec((2, tm, D), lambda i: (0, i, 0))
    full = lambda a: pl.BlockSpec(a.shape, lambda i: (0,) * a.ndim)
    bf = lambda a: a.astype(BF16)
    consts = [p['mu'], bf(p['w_rkv']), bf(p['gate1']), bf(p['gate2']), bf(p['dec1']), bf(p['dec2']), p['dec0'],
              bf(p['icl1']), bf(p['icl2']), p['icl0'], p['k_k'].reshape(1, D), p['k_a'].reshape(1, D),
              p['r_k'].reshape(1, D), _seg_ones()]
    one = jax.ShapeDtypeStruct((T, D), F32)
    two = jax.ShapeDtypeStruct((2, T, D), F32)
    return pl.pallas_call(
        _rwkv_proj_kernel,
        grid=(T // tm,),
        in_specs=[row, row] + [full(a) for a in consts],
        out_specs=[row, row, row, row, row, row2, row2, row2],
        out_shape=[one, one, one, one, one, two, two, two],
        compiler_params=_cparams(("arbitrary",)),
    )(u, dx, *consts)


def _scan_kernel(r_ref, w_ref, k_ref, v_ref, a_ref, b_ref, ones_ref, hsel_ref,
                 y_ref, s_ref, vt_ref):
    d = pl.program_id(0)
    c = pl.program_id(2)
    tc, nb = r_ref.shape[0], r_ref.shape[1]
    tw = SCAN_TILE
    n_wide = r_ref.shape[2] // tw
    heads = tw // HEAD_DIM
    assert heads * tc == tw

    @pl.when(c == 0)
    def _():
        s_ref[...] = jnp.zeros_like(s_ref)

    for bb in range(nb):
        for q in range(n_wide):
            vt = v_ref[:, bb, q * tw:(q + 1) * tw].T
            vt_ref[bb * n_wide + q] = jnp.concatenate(
                [vt[h * HEAD_DIM:(h + 1) * HEAD_DIM] for h in range(heads)], axis=1)

    head_base = (lax.broadcasted_iota(I32, (HEAD_DIM, LANES), 1) // HEAD_DIM) * tc
    tiles = [(bb, q) for bb in range(nb) for q in range(n_wide)]
    groups = [tiles[i:i + SCAN_GROUP] for i in range(0, len(tiles), SCAN_GROUP)]

    def stacked(grp, get):
        def wide(bb, q):
            return jnp.concatenate(
                [jnp.broadcast_to(get(bb, slice(q * tw + hf * LANES, q * tw + (hf + 1) * LANES)), (HEAD_DIM, LANES))
                 for hf in range(tw // LANES)], axis=1)
        return jnp.concatenate([wide(bb, q) for bb, q in grp], axis=0)

    def load_state(grp):
        return jnp.concatenate([s_ref[bb * n_wide + q] for bb, q in grp], axis=0)

    def emit_y(grp, st, t_y):
        r_rows = stacked(grp, lambda bb, cols: r_ref[t_y, bb:bb + 1, cols])
        yh = _dot_nt(hsel_ref[...], (st * r_rows).astype(BF16))
        first = tiles.index(grp[0])
        y_ref[0, t_y, :, first * HEAD_DIM:(first + len(grp)) * HEAD_DIM] = yh

    def step(s_i, carry):
        t = jnp.where(d == 0, s_i, tc - 1 - s_i)
        t_prev = jnp.where(s_i == 0, t, jnp.where(d == 0, t - 1, t + 1))
        pick = head_base + t
        for grp in groups:
            one = lambda ref: stacked(grp, lambda bb, cols: ref[t, bb:bb + 1, cols])
            two = lambda ref: stacked(grp, lambda bb, cols: ref[0, t, bb:bb + 1, cols])
            st = load_state(grp)
            sa = _dot((st * one(a_ref)).astype(BF16), ones_ref[...])
            emit_y(grp, st, t_prev)
            vcol = jnp.concatenate(
                [jnp.concatenate([jnp.take_along_axis(vt_ref[bb * n_wide + q, :, hf * LANES:(hf + 1) * LANES],
                                                      pick, axis=1) for hf in range(tw // LANES)], axis=1)
                 for bb, q in grp], axis=0)
            st = st * two(w_ref) + sa * two(b_ref) + vcol * two(k_ref)
            for j, (bb, q) in enumerate(grp):
                s_ref[bb * n_wide + q] = st[j * HEAD_DIM:(j + 1) * HEAD_DIM]
        return carry

    lax.fori_loop(0, tc, step, 0, unroll=SCAN_UNROLL)
    t_last = jnp.where(d == 0, tc - 1, 0)
    for grp in groups:
        emit_y(grp, load_state(grp), t_last)


def _wkv_scan(r, w, k, v, a, b, n_ctx):
    N, B, D = r.shape
    tc = SCAN_CHUNK
    wc = SCAN_COLS
    n_wide = wc // SCAN_TILE
    nc = N // tc
    ncc = n_ctx // tc

    def chunk(d, c):
        rev = jnp.where(c < ncc, ncc - 1 - c, nc - 1 - (c - ncc))
        return jnp.where(d == 0, c, rev)

    one = pl.BlockSpec((tc, B, wc), lambda d, g, c: (chunk(d, c), 0, g))
    two = pl.BlockSpec((1, tc, B, wc), lambda d, g, c: (d, chunk(d, c), 0, g))
    seg = np.arange(SCAN_TILE) // HEAD_DIM
    hsel = np.zeros((8, SCAN_TILE), np.float32)
    for hh in range(SCAN_TILE // HEAD_DIM):
        hsel[hh, seg == hh] = 1.0
    const = lambda a_: pl.BlockSpec(a_.shape, lambda d, g, c: (0, 0))
    consts = [_seg_ones(SCAN_TILE), jnp.asarray(hsel, BF16)]
    heads = SCAN_TILE // HEAD_DIM
    ncg = D // wc
    y = pl.pallas_call(
        _scan_kernel,
        grid=(2, ncg, nc),
        in_specs=[one, two, two, one, one, two] + [const(a_) for a_ in consts],
        out_specs=pl.BlockSpec((1, tc, 8, B * n_wide * HEAD_DIM), lambda d, g, c: (d, chunk(d, c), 0, g)),
        out_shape=jax.ShapeDtypeStruct((2, N, 8, ncg * B * n_wide * HEAD_DIM), F32),
        scratch_shapes=[pltpu.VMEM((B * n_wide, HEAD_DIM, SCAN_TILE), F32),
                        pltpu.VMEM((B * n_wide, HEAD_DIM, SCAN_TILE), F32)],
        compiler_params=_cparams(("arbitrary", "arbitrary", "arbitrary")),
    )(r, w, k, v, a, b, *consts)
    y = y[:, :, :heads].reshape(2, N, heads, ncg, B, n_wide, HEAD_DIM)
    return jnp.transpose(y, (0, 1, 4, 3, 5, 2, 6)).reshape(2, N, B, D)


def _rwkv_out_kernel(y0_ref, y1_ref, bonus_ref, g_ref, lnx_ref, ones_ref, w_ref, h_ref, mod_ref, lg_ref, lb_ref,
                     hn_ref, u_ref):
    y = y0_ref[0] + y1_ref[0]
    ym = _head_sum(y, ones_ref) * (1.0 / HEAD_DIM)
    yc = y - ym
    yv = _head_sum(yc * yc, ones_ref) * (1.0 / HEAD_DIM)
    yn = yc * lax.rsqrt(yv + LNX_EPS) * lnx_ref[0:1, :] + lnx_ref[1:2, :]
    x = ((yn + bonus_ref[...]) * g_ref[...]).astype(BF16)
    o = _dot(x, w_ref[...])
    z = DEEPNORM_ALPHA * h_ref[...] + mod_ref[0] * o
    hn = _layer_norm(z, lg_ref[...], lb_ref[...])
    hn_ref[...] = hn
    u_ref[...] = _pack_halves(hn * (1.0 + mod_ref[2]) + mod_ref[1])


def _rwkv_out(y, bonus, g, lnx, w_out, h, mod_rows, ln_g, ln_b, row0):
    T, D = h.shape
    tm = ROW_TILE
    t0 = row0 // tm
    off = pl.BlockSpec((tm, D), lambda i: (i + t0, 0))
    out = pl.BlockSpec((tm, D), lambda i: (i, 0))
    full = lambda a: pl.BlockSpec(a.shape, lambda i: (0,) * a.ndim)
    vec = pl.BlockSpec((1, D), lambda i: (0, 0))
    ones = _seg_ones()
    w_b = w_out.astype(BF16)
    return pl.pallas_call(
        _rwkv_out_kernel,
        grid=((T - row0) // tm,),
        in_specs=[pl.BlockSpec((1, tm, D), lambda i: (0, i + t0, 0)),
                  pl.BlockSpec((1, tm, D), lambda i: (1, i + t0, 0)),
                  off, off, full(lnx), full(ones), full(w_b), off, full(mod_rows), vec, vec],
        out_specs=[out, pl.BlockSpec((tm, D // 2), lambda i: (i, 0))],
        out_shape=[jax.ShapeDtypeStruct((T - row0, D), F32), jax.ShapeDtypeStruct((T - row0, D // 2), I32)],
        compiler_params=_cparams(("arbitrary",)),
    )(y, y, bonus, g, lnx, ones, w_b, h, mod_rows, ln_g.reshape(1, D), ln_b.reshape(1, D))


def _shift_delta(u, n_ctx):
    def seg(x):
        xp = jnp.pad(x, ((1, 1), (0, 0), (0, 0)))
        return 0.5 * (xp[:-2] + xp[2:]) - x
    return jnp.concatenate([seg(u[:n_ctx]), seg(u[n_ctx:])], axis=0)


def kernel(x, c, ctx, c_ctx, ada_w, ada_b, post_ln_g, post_ln_b, att_w_in, att_w_out, att_sink, diff_lambda_vecs, diff_subln_g, rk_mu, rk_w_rkv, rk_w_out, rk_decay0, rk_decay1, rk_decay2, rk_iclr0, rk_iclr1, rk_iclr2, rk_gate1, rk_gate2, rk_k_k, rk_k_a, rk_r_k, rk_lnx, moe_router, moe_bias, moe_w_in, moe_w_out, moe_ws_in, moe_ws_out):
    B, S, D = x.shape
    L = ctx.shape[1]
    N = L + S
    tm = ROW_TILE
    assert L % tm == 0 and S % tm == 0 and L % SCAN_CHUNK == 0 and S % SCAN_CHUNK == 0
    assert tm % B == 0 and D % SCAN_COLS == 0

    rows = -(-(B + 1) // 8) * 8
    cvec = jnp.concatenate([c, c_ctx[None, :], jnp.zeros((rows - B - 1, D), F32)], axis=0)
    mods = [_mod_table(_ada_mod(cvec, ada_w[i], ada_b[i]), B, D) for i in range(DEPTH)]

    h0 = jnp.concatenate([ctx, x], axis=1)
    lam_init = 0.8 - 0.6 * math.exp(-0.3 * 0)
    qa, ka, va, qb, kb, vb = _attn_inproj(h0, mods[0], att_w_in[0], L)
    oa = _win_attn(qa, ka, va, att_sink[0], L)
    ob = _diff_attn(qb, kb, vb, diff_lambda_vecs[0], diff_subln_g[0], lam_init, L)
    h1, u1 = _mix_out([oa, ob], [att_w_out[0][:A_WIDTH], att_w_out[0][A_WIDTH:]], h0, mods[0],
                      post_ln_g[0, 0], post_ln_b[0, 0], L, 0)
    tiles_b, tiles_c = N // tm, L // tm
    gate0 = mods[0][:, :, 5].reshape(B * 2, 1, D)
    gate0_index = lambda i: (i // tiles_b) * 2 + jnp.minimum((i % tiles_b) // tiles_c, 1)
    h2 = _moe_layer(u1.reshape(B * N, D // 2), h1.reshape(B * N, D), gate0, gate0_index, moe_router[0], moe_bias[0],
                    moe_w_in, moe_w_out, moe_ws_in[0], moe_ws_out[0],
                    post_ln_g[0, 1], post_ln_b[0, 1], 0).reshape(B, N, D)

    m_ctx, m_lat = mods[1][:, 0], mods[1][:, 1]
    h2_t = jnp.swapaxes(h2, 0, 1)
    is_lat = (jnp.arange(N) >= L)[:, None, None]
    u = h2_t * (1.0 + jnp.where(is_lat, m_lat[:, 1], m_ctx[:, 1])) + jnp.where(is_lat, m_lat[:, 0], m_ctx[:, 0])
    dx = _shift_delta(u, L)
    params = dict(mu=rk_mu[0], w_rkv=rk_w_rkv[0], gate1=rk_gate1[0], gate2=rk_gate2[0],
                  dec0=rk_decay0[0], dec1=rk_decay1[0], dec2=rk_decay2[0],
                  icl0=rk_iclr0[0], icl1=rk_iclr1[0], icl2=rk_iclr2[0],
                  k_k=rk_k_k[0], k_a=rk_k_a[0], r_k=rk_r_k[0])
    r, v, a, g, bonus, w2, k2, b2 = _rwkv_proj(u.reshape(N * B, D), dx.reshape(N * B, D), params)
    tmaj = lambda t: t.reshape(t.shape[:-2] + (N, B, D))
    y = _wkv_scan(tmaj(r), tmaj(w2), tmaj(k2), tmaj(v), tmaj(a), tmaj(b2), L)
    lat_rows = lambda j: jnp.tile(m_lat[:, j], (tm // B, 1))
    h3, u3 = _rwkv_out(y.reshape(2, N * B, D), bonus, g, rk_lnx[0], rk_w_out[0], h2_t.reshape(N * B, D),
                       jnp.stack([lat_rows(2), lat_rows(3), lat_rows(4)]),
                       post_ln_g[1, 0], post_ln_b[1, 0], L * B)
    out = _moe_layer(u3, h3, lat_rows(5)[None], lambda i: 0, moe_router[1], moe_bias[1],
                     moe_w_in, moe_w_out, moe_ws_in[1], moe_ws_out[1],
                     post_ln_g[1, 1], post_ln_b[1, 1], 1)
    return jnp.swapaxes(out.reshape(S, B, D), 0, 1)
```

```python
import functools
import math

import numpy as np
import jax
import jax.numpy as jnp
from jax import lax
from jax.experimental import pallas as pl
from jax.experimental.pallas import tpu as pltpu
from jax.experimental.pallas import tpu_sc as plsc

F32 = jnp.float32
BF16 = jnp.bfloat16
I32 = jnp.int32

HEAD_DIM = 64
GRID_W = 64
ROPE_AXIS_DIM = HEAD_DIM // 2
ROPE_THETA = 10000.0
Q_BLOCK = 128
A_Q_HEADS = 8
A_KV_HEADS = 2
A_GROUP = A_Q_HEADS // A_KV_HEADS
A_WIDTH = A_Q_HEADS * HEAD_DIM
A_KV_WIDTH = A_KV_HEADS * HEAD_DIM
B_HEADS = 4
B_V_DIM = 2 * HEAD_DIM
B_WIDTH = B_HEADS * B_V_DIM
LNX_EPS = 64e-5
N_EXPERTS = 256
TOP_K = 8
N_GROUPS = 8
TOPK_GROUPS = 4
ROUTED_SCALE = 2.5
MOE_BLOCK = 256
LN_EPS = 1e-5
SUBLN_EPS = 1e-5
NEG_INF = -1e30
DEPTH = 2
DEEPNORM_ALPHA = (2 * DEPTH) ** 0.25

LANES = 128
ROW_TILE = 256
PROJ_TILE = 128
SC_WINDOW = 128
SCAN_CHUNK = 64
SCAN_COLS = 1024
SCAN_TILE = 256
SCAN_GROUP = 8
SCAN_UNROLL = 4
VMEM_LIMIT = 56 * 1024 * 1024


def _cparams(sem):
    return pltpu.CompilerParams(dimension_semantics=sem, vmem_limit_bytes=VMEM_LIMIT)


def _silu(x):
    return x * jax.nn.sigmoid(x)


def _layer_norm(z, g, b):
    mu = jnp.mean(z, -1, keepdims=True)
    zc = z - mu
    var = jnp.mean(zc * zc, -1, keepdims=True)
    return zc * lax.rsqrt(var + LN_EPS) * g + b


def _dot(a, b):
    return jnp.dot(a, b, preferred_element_type=F32)


def _dot_nt(a, b):
    return lax.dot_general(a, b, (((1,), (1,)), ((), ())), preferred_element_type=F32)


def _pack_halves(x):
    half = x.shape[1] // 2
    bits = lambda v: lax.bitcast_convert_type(v.astype(BF16).astype(F32), I32)
    return lax.shift_right_logical(bits(x[:, :half]), 16) | bits(x[:, half:])


def _unpack_halves(p):
    lo = lax.bitcast_convert_type(lax.shift_left(p, 16), F32)
    hi = lax.bitcast_convert_type(p & jnp.int32(-65536), F32)
    return lo, hi


def _dot_halves(p, w_ref_or_array):
    lo, hi = _unpack_halves(p)
    half = p.shape[1]
    return _dot(lo.astype(BF16), w_ref_or_array[:half]) + _dot(hi.astype(BF16), w_ref_or_array[half:])


def _ada_kernel(c_ref, w_ref, b_ref, o_ref):
    c = c_ref[...]
    o_ref[...] = jnp.dot(_silu(c), w_ref[...], preferred_element_type=F32,
                         precision=lax.Precision.HIGHEST) + b_ref[...]


def _ada_mod(cvec, w, bias):
    R, D = cvec.shape
    n_out = w.shape[1]
    tn = 768
    return pl.pallas_call(
        _ada_kernel,
        grid=(n_out // tn,),
        in_specs=[pl.BlockSpec((R, D), lambda j: (0, 0)),
                  pl.BlockSpec((D, tn), lambda j: (0, j)),
                  pl.BlockSpec((1, tn), lambda j: (0, j))],
        out_specs=pl.BlockSpec((R, tn), lambda j: (0, j)),
        out_shape=jax.ShapeDtypeStruct((R, n_out), F32),
        compiler_params=_cparams(("arbitrary",)),
    )(cvec, w, bias.reshape(1, n_out))


def _mod_table(m, batch, d):
    m_lat = m[:batch].reshape(batch, 6, d)
    m_ctx = jnp.broadcast_to(m[batch].reshape(1, 6, d), (batch, 6, d))
    return jnp.stack([m_ctx, m_lat], axis=1)


def _mod_spec(d, ctx_tiles):
    return pl.BlockSpec((1, 1, 6, d), lambda b, i: (b, jnp.minimum(i // ctx_tiles, 1), 0, 0))


def _rope_tables(n_ctx, n_lat):
    rows = n_lat // GRID_W
    row = np.repeat(np.arange(rows), GRID_W).astype(np.float32)
    col = np.tile(np.arange(GRID_W), rows).astype(np.float32)
    inv = (ROPE_THETA ** (-np.arange(0, ROPE_AXIS_DIM, 2, dtype=np.float32) / ROPE_AXIS_DIM)).astype(np.float32)
    ar = row[:, None] * inv
    ac = col[:, None] * inv
    ang = np.concatenate([ar, ar, ac, ac], -1)
    cos = np.cos(ang).astype(np.float32)
    sin = np.sin(ang).astype(np.float32)
    lower = (np.arange(HEAD_DIM) % ROPE_AXIS_DIM) < (ROPE_AXIS_DIM // 2)
    sin_up = np.where(lower[None, :], -sin, 0.0)
    sin_dn = np.where(lower[None, :], 0.0, sin)

    def full(t, ctx_fill):
        t = np.concatenate([np.full((n_ctx, HEAD_DIM), ctx_fill, np.float32), t], 0)
        return jnp.asarray(np.tile(t, (1, LANES // HEAD_DIM)))

    return full(cos, 1.0), full(sin_up, 0.0), full(sin_dn, 0.0)


def _inproj_kernel(h_ref, mod_ref, w_ref, cos_ref, su_ref, sd_ref,
                   qa_ref, ka_ref, va_ref, qb_ref, kb_ref, vb_ref):
    h = h_ref[0]
    shift = mod_ref[0, 0, 0:1, :]
    scale = mod_ref[0, 0, 1:2, :]
    u = (h * (1.0 + scale) + shift).astype(BF16)
    y = _dot(u, w_ref[...])
    cos, s_up, s_dn = cos_ref[...], su_ref[...], sd_ref[...]
    q_scale = HEAD_DIM ** -0.5

    def rope(xc):
        half = ROPE_AXIS_DIM // 2
        return xc * cos + pltpu.roll(xc, LANES - half, 1) * s_up + pltpu.roll(xc, half, 1) * s_dn

    def emit(out_ref, col0, width, roped, mul):
        for j in range(width // LANES):
            xc = y[:, col0 + j * LANES: col0 + (j + 1) * LANES]
            if roped:
                xc = rope(xc)
            if mul != 1.0:
                xc = xc * mul
            out_ref[0, :, j * LANES:(j + 1) * LANES] = xc.astype(out_ref.dtype)

    c = 0
    emit(qa_ref, c, A_WIDTH, True, q_scale); c += A_WIDTH
    emit(ka_ref, c, A_KV_WIDTH, True, 1.0); c += A_KV_WIDTH
    emit(va_ref, c, A_KV_WIDTH, False, 1.0); c += A_KV_WIDTH
    emit(qb_ref, c, B_WIDTH, True, q_scale); c += B_WIDTH
    emit(kb_ref, c, B_WIDTH, True, 1.0); c += B_WIDTH
    emit(vb_ref, c, B_WIDTH, False, 1.0)


def _attn_inproj(h, mod, w_in, n_ctx):
    B, N, D = h.shape
    tm = ROW_TILE
    cos, s_up, s_dn = _rope_tables(n_ctx, N - n_ctx)
    widths = (A_WIDTH, A_KV_WIDTH, A_KV_WIDTH, B_WIDTH, B_WIDTH, B_WIDTH)
    tab_spec = pl.BlockSpec((tm, LANES), lambda b, i: (i, 0))
    return pl.pallas_call(
        _inproj_kernel,
        grid=(B, N // tm),
        in_specs=[pl.BlockSpec((1, tm, D), lambda b, i: (b, i, 0)),
                  _mod_spec(D, n_ctx // tm),
                  pl.BlockSpec(w_in.shape, lambda b, i: (0, 0)),
                  tab_spec, tab_spec, tab_spec],
        out_specs=[pl.BlockSpec((1, tm, w), lambda b, i: (b, i, 0)) for w in widths],
        out_shape=[jax.ShapeDtypeStruct((B, N, w), BF16) for w in widths],
        compiler_params=_cparams(("arbitrary", "arbitrary")),
    )(h, mod, w_in.astype(BF16), cos, s_up, s_dn)


def _win_attn_kernel(n_ctx_blocks, n_blocks, q_ref, kc_ref, vc_ref, kl_ref, km_ref, kr_ref,
                     vl_ref, vm_ref, vr_ref, sink_ref, o_ref):
    j = pl.program_id(1)
    is_lat = j >= n_ctx_blocks
    qb = Q_BLOCK
    n_c = kc_ref.shape[1]
    rows = A_GROUP * qb
    n_keys = n_c + 3 * qb
    far = 1 << 20
    r_idx = lax.broadcasted_iota(I32, (rows, n_keys), 0) % qb
    cw = lax.broadcasted_iota(I32, (rows, n_keys), 1) - n_c
    off_l = jnp.where(jnp.logical_and(is_lat, j > n_ctx_blocks), 0, far)
    end_m = jnp.where(is_lat, 2 * qb, qb)
    off_r = jnp.where(jnp.logical_and(is_lat, j < n_blocks - 1), 0, far)
    valid = ((cw < 0)
             | ((cw >= 0) & (cw < qb) & (cw >= r_idx + off_l))
             | ((cw >= qb) & (cw < end_m))
             | ((cw >= 2 * qb) & (cw - 2 * qb + off_r <= r_idx)))
    outs = []
    for kv in range(A_KV_HEADS):
        cols = slice(kv * HEAD_DIM, (kv + 1) * HEAD_DIM)
        k_all = jnp.concatenate([kc_ref[0, :, cols], kl_ref[0, :, cols], km_ref[0, :, cols],
                                 kr_ref[0, :, cols]], axis=0)
        v_all = jnp.concatenate([vc_ref[0, :, cols], vl_ref[0, :, cols], vm_ref[0, :, cols],
                                 vr_ref[0, :, cols]], axis=0)
        q0 = kv * A_GROUP
        q = jnp.concatenate([q_ref[0, :, (q0 + g) * HEAD_DIM:(q0 + g + 1) * HEAD_DIM]
                             for g in range(A_GROUP)], axis=0)
        sink = jnp.concatenate([jnp.broadcast_to(sink_ref[q0 + g:q0 + g + 1, 0:1], (qb, 1))
                                for g in range(A_GROUP)], axis=0)
        s = jnp.where(valid, _dot_nt(q, k_all), NEG_INF)
        m = jnp.maximum(jnp.max(s, -1, keepdims=True), sink)
        e = jnp.exp(s - m)
        denom = jnp.sum(e, -1, keepdims=True) + jnp.exp(sink - m)
        p = (e / denom).astype(BF16)
        o = _dot(p, v_all)
        outs += [o[g * qb:(g + 1) * qb] for g in range(A_GROUP)]
    for j2 in range(A_Q_HEADS // 2):
        pair = jnp.concatenate([outs[2 * j2], outs[2 * j2 + 1]], axis=1)
        o_ref[0, :, j2 * LANES:(j2 + 1) * LANES] = pair.astype(o_ref.dtype)


def _win_attn(qa, ka, va, sink, n_ctx):
    B, N, _ = qa.shape
    qb = Q_BLOCK
    nb = N // qb
    ncb = n_ctx // qb
    sink_pad = jnp.broadcast_to(sink.reshape(A_Q_HEADS, 1).astype(F32), (A_Q_HEADS, LANES))

    def left(b, j):
        return (b, jnp.clip(j - 1, ncb, nb - 1), 0)

    def mid(b, j):
        return (b, jnp.clip(j, ncb, nb - 1), 0)

    def right(b, j):
        return (b, jnp.clip(j + 1, ncb, nb - 1), 0)

    kv_blk = lambda im: pl.BlockSpec((1, qb, A_KV_WIDTH), im)
    ctx_blk = pl.BlockSpec((1, n_ctx, A_KV_WIDTH), lambda b, j: (b, 0, 0))
    return pl.pallas_call(
        functools.partial(_win_attn_kernel, ncb, nb),
        grid=(B, nb),
        in_specs=[pl.BlockSpec((1, qb, A_WIDTH), lambda b, j: (b, j, 0)),
                  ctx_blk, ctx_blk,
                  kv_blk(left), kv_blk(mid), kv_blk(right),
                  kv_blk(left), kv_blk(mid), kv_blk(right),
                  pl.BlockSpec((A_Q_HEADS, LANES), lambda b, j: (0, 0))],
        out_specs=pl.BlockSpec((1, qb, A_WIDTH), lambda b, j: (b, j, 0)),
        out_shape=jax.ShapeDtypeStruct((B, N, A_WIDTH), BF16),
        compiler_params=_cparams(("arbitrary", "arbitrary")),
    )(qa, ka, va, ka, ka, ka, va, va, va, sink_pad)


def _diff_attn_kernel(n_ctx, lam_init, q_ref, k_ref, v_ref, lv_ref, g_ref, o_ref):
    j = pl.program_id(1)
    lv = lv_ref[...]
    lam = (jnp.exp(jnp.sum(lv[0:1] * lv[1:2], -1, keepdims=True))
           - jnp.exp(jnp.sum(lv[2:3] * lv[3:4], -1, keepdims=True)) + lam_init)
    gain = g_ref[...] * (1.0 - lam_init)

    def run(n_keys):
        for hd in range(B_HEADS):
            parts = []
            for mm in range(2):
                c0 = (hd * 2 + mm) * HEAD_DIM
                q = q_ref[0, :, c0:c0 + HEAD_DIM]
                k = k_ref[0, :n_keys, c0:c0 + HEAD_DIM]
                s = _dot_nt(q, k)
                e = jnp.exp(s - jnp.max(s, -1, keepdims=True))
                parts.append((e, jnp.sum(e, -1, keepdims=True)))
            (e0, l0), (e1, l1) = parts
            a = (e0 * (1.0 / l0) - e1 * (lam / l1)).astype(BF16)
            o = _dot(a, v_ref[0, :n_keys, hd * B_V_DIM:(hd + 1) * B_V_DIM])
            o = o * lax.rsqrt(jnp.mean(o * o, -1, keepdims=True) + SUBLN_EPS) * gain
            o_ref[0, :, hd * B_V_DIM:(hd + 1) * B_V_DIM] = o.astype(o_ref.dtype)

    @pl.when(j == 0)
    def _():
        run(n_ctx)

    @pl.when(j > 0)
    def _():
        run(k_ref.shape[1])


def _diff_attn(qb, kb, vb, lam_vecs, subln_g, lam_init, n_ctx):
    B, N, _ = qb.shape
    tq = n_ctx
    return pl.pallas_call(
        functools.partial(_diff_attn_kernel, n_ctx, lam_init),
        grid=(B, N // tq),
        in_specs=[pl.BlockSpec((1, tq, B_WIDTH), lambda b, j: (b, j, 0)),
                  pl.BlockSpec((1, N, B_WIDTH), lambda b, j: (b, 0, 0)),
                  pl.BlockSpec((1, N, B_WIDTH), lambda b, j: (b, 0, 0)),
                  pl.BlockSpec((4, HEAD_DIM), lambda b, j: (0, 0)),
                  pl.BlockSpec((1, B_V_DIM), lambda b, j: (0, 0))],
        out_specs=pl.BlockSpec((1, tq, B_WIDTH), lambda b, j: (b, j, 0)),
        out_shape=jax.ShapeDtypeStruct((B, N, B_WIDTH), BF16),
        compiler_params=_cparams(("arbitrary", "arbitrary")),
    )(qb, kb, vb, lam_vecs.astype(F32), subln_g.reshape(1, B_V_DIM).astype(F32))


def _mix_out_kernel(n_in, *refs):
    xs = refs[:n_in]
    ws = refs[n_in:2 * n_in]
    h_ref, mod_ref, g_ref, b_ref, hn_ref, u_ref = refs[2 * n_in:]
    o = _dot(xs[0][0], ws[0][...])
    for x_ref, w_ref in zip(xs[1:], ws[1:]):
        o = o + _dot(x_ref[0], w_ref[...])
    z = DEEPNORM_ALPHA * h_ref[0] + mod_ref[0, 0, 2:3, :] * o
    hn = _layer_norm(z, g_ref[...], b_ref[...])
    hn_ref[0] = hn
    u_ref[0] = _pack_halves(hn * (1.0 + mod_ref[0, 0, 4:5, :]) + mod_ref[0, 0, 3:4, :])


def _mix_out(xs, ws, h, mod, ln_g, ln_b, n_ctx, row0):
    B, N, D = h.shape
    tm = ROW_TILE
    t0 = row0 // tm
    n_out = N - row0
    row_spec = lambda w: pl.BlockSpec((1, tm, w), lambda b, i: (b, i + t0, 0))
    out_spec = pl.BlockSpec((1, tm, D), lambda b, i: (b, i, 0))
    vec_spec = pl.BlockSpec((1, D), lambda b, i: (0, 0))
    return pl.pallas_call(
        functools.partial(_mix_out_kernel, len(xs)),
        grid=(B, n_out // tm),
        in_specs=([row_spec(x.shape[-1]) for x in xs]
                  + [pl.BlockSpec(w.shape, lambda b, i: (0, 0)) for w in ws]
                  + [row_spec(D),
                     pl.BlockSpec((1, 1, 6, D), lambda b, i: (b, jnp.minimum((i + t0) // (n_ctx // tm), 1), 0, 0)),
                     vec_spec, vec_spec]),
        out_specs=[out_spec, pl.BlockSpec((1, tm, D // 2), lambda b, i: (b, i, 0))],
        out_shape=[jax.ShapeDtypeStruct((B, n_out, D), F32), jax.ShapeDtypeStruct((B, n_out, D // 2), I32)],
        compiler_params=_cparams(("arbitrary", "arbitrary")),
    )(*xs, *[w.astype(BF16) for w in ws], h, mod, ln_g.reshape(1, D), ln_b.reshape(1, D))


def _router_kernel(u_ref, rt_ref, bias_ref, tri_ref, e_ref, gw_ref, rank_ref, cnt_ref, carry_ref):
    i = pl.program_id(0)

    @pl.when(i == 0)
    def _():
        carry_ref[...] = jnp.zeros_like(carry_ref)

    tm = u_ref.shape[0]
    per_group = N_EXPERTS // N_GROUPS
    neg = -jnp.inf
    u_lo, u_hi = _unpack_halves(u_ref[...])
    half = u_ref.shape[1]
    logits = (_dot_nt(rt_ref[:, :half], u_lo.astype(BF16))
              + _dot_nt(rt_ref[:, half:], u_hi.astype(BF16)))
    scores = jax.nn.sigmoid(logits)
    sel = scores + bias_ref[...]
    io_in = lax.broadcasted_iota(I32, (per_group, tm), 0)
    grp_rows = []
    for gi in range(N_GROUPS):
        sg = sel[gi * per_group:(gi + 1) * per_group]
        m1 = jnp.max(sg, axis=0, keepdims=True)
        i1 = jnp.min(jnp.where(sg == m1, io_in, per_group), axis=0, keepdims=True)
        m2 = jnp.max(jnp.where(io_in == i1, neg, sg), axis=0, keepdims=True)
        grp_rows.append(m1 + m2)
    grp = jnp.concatenate(grp_rows, axis=0)
    io_g = lax.broadcasted_iota(I32, grp.shape, 0)
    g_sel = jnp.zeros(grp.shape, F32)
    for _ in range(TOPK_GROUPS):
        m = jnp.max(grp, axis=0, keepdims=True)
        hit = io_g == jnp.min(jnp.where(grp == m, io_g, N_GROUPS), axis=0, keepdims=True)
        g_sel = jnp.where(hit, 1.0, g_sel)
        grp = jnp.where(hit, neg, grp)
    selm = jnp.concatenate(
        [jnp.where(g_sel[gi:gi + 1] > 0.5, sel[gi * per_group:(gi + 1) * per_group], NEG_INF)
         for gi in range(N_GROUPS)], axis=0)
    io_e = lax.broadcasted_iota(I32, selm.shape, 0)
    chosen_f = jnp.zeros(selm.shape, F32)
    idx, gws = [], []
    for _ in range(TOP_K):
        m = jnp.max(selm, axis=0, keepdims=True)
        ik = jnp.min(jnp.where(selm == m, io_e, N_EXPERTS), axis=0, keepdims=True)
        hit = io_e == ik
        idx.append(ik)
        gws.append(jnp.sum(jnp.where(hit, scores, 0.0), axis=0, keepdims=True))
        chosen_f = jnp.where(hit, 1.0, chosen_f)
        selm = jnp.where(hit, neg, selm)
    gw = jnp.concatenate(gws, axis=0)
    gw_ref[...] = gw / jnp.sum(gw, axis=0, keepdims=True) * ROUTED_SCALE
    e_ref[...] = jnp.concatenate(idx, axis=0)
    before = _dot(chosen_f.astype(BF16), tri_ref[...]) + carry_ref[...]
    ranks = [jnp.sum(jnp.where(io_e == ik, before, 0.0), axis=0, keepdims=True) for ik in idx]
    rank_ref[...] = jnp.concatenate(ranks, axis=0).astype(I32)
    carry_ref[...] = carry_ref[...] + jnp.sum(chosen_f, axis=1, keepdims=True)
    cnt_ref[...] = carry_ref[...].astype(I32)


def _router(u, router, bias):
    T = u.shape[0]
    D = router.shape[0]
    tm = ROW_TILE
    tri = jnp.asarray(np.triu(np.ones((tm, tm), np.float32), 1), BF16)
    tok_spec = pl.BlockSpec((TOP_K, tm), lambda i: (0, i))
    return pl.pallas_call(
        _router_kernel,
        grid=(T // tm,),
        in_specs=[pl.BlockSpec((tm, D // 2), lambda i: (i, 0)),
                  pl.BlockSpec((N_EXPERTS, D), lambda i: (0, 0)),
                  pl.BlockSpec((N_EXPERTS, 1), lambda i: (0, 0)),
                  pl.BlockSpec((tm, tm), lambda i: (0, 0))],
        out_specs=[tok_spec, tok_spec, tok_spec, pl.BlockSpec((N_EXPERTS, 1), lambda i: (0, 0))],
        out_shape=[jax.ShapeDtypeStruct((TOP_K, T), I32), jax.ShapeDtypeStruct((TOP_K, T), F32),
                   jax.ShapeDtypeStruct((TOP_K, T), I32), jax.ShapeDtypeStruct((N_EXPERTS, 1), I32)],
        scratch_shapes=[pltpu.VMEM((N_EXPERTS, 1), F32)],
        compiler_params=_cparams(("arbitrary",)),
    )(u, router.T.astype(BF16), bias.reshape(N_EXPERTS, 1).astype(F32), tri)


def _dispatch_kernel(dest_ref, x_ref, xs_in, xs_hbm, sem):
    del xs_in
    tm = dest_ref.shape[1]

    def row_copy(t, k):
        return pltpu.make_async_copy(x_ref.at[pl.ds(t, 1)], xs_hbm.at[pl.ds(dest_ref[k, t], 1)], sem)

    def start(t, carry):
        for k in range(TOP_K):
            row_copy(t, k).start()
        return carry

    def wait(t, carry):
        for k in range(TOP_K):
            row_copy(t, k).wait()
        return carry

    lax.fori_loop(0, tm, start, 0)
    lax.fori_loop(0, tm, wait, 0)


def _dispatch(x, dest, n_rows):
    T, D = x.shape
    tm = ROW_TILE
    xs0 = jnp.zeros((n_rows, D), x.dtype)
    return pl.pallas_call(
        _dispatch_kernel,
        grid=(T // tm,),
        in_specs=[pl.BlockSpec((TOP_K, tm), lambda i: (0, i), memory_space=pltpu.SMEM),
                  pl.BlockSpec((tm, D), lambda i: (i, 0)),
                  pl.BlockSpec(memory_space=pl.ANY)],
        out_specs=pl.BlockSpec(memory_space=pl.ANY),
        out_shape=jax.ShapeDtypeStruct((n_rows, D), x.dtype),
        scratch_shapes=[pltpu.SemaphoreType.DMA(())],
        input_output_aliases={2: 0},
        compiler_params=_cparams(("arbitrary",)),
    )(dest, x, xs0)


def _sc_mesh():
    return plsc.VectorSubcoreMesh(core_axis_name="c", subcore_axis_name="s")


def _sc_scatter_rows(x, dest, n_rows):
    T, W = x.shape
    K = dest.shape[0]
    win = SC_WINDOW
    n_win = T // win

    @functools.partial(pl.kernel, out_type=jax.ShapeDtypeStruct((n_rows, W), x.dtype), mesh=_sc_mesh(),
                       scratch_types=[])
    def scatter(x_hbm, i_hbm, o_hbm):
        def body(x_vmem, i_vmem):
            pltpu.sync_copy(x_vmem, o_hbm.at[i_vmem.at[0]])

        pltpu.emit_pipeline(
            body,
            grid=(K * n_win,),
            in_specs=[pl.BlockSpec((win, W), lambda j: (j % n_win, 0), pipeline_mode=pl.Buffered(1)),
                      pl.BlockSpec((1, win), lambda j: (0, j))],
            out_specs=[],
            core_axis_name=("c", "s"),
            dimension_semantics=(pltpu.PARALLEL,),
        )(x_hbm, i_hbm)

    return scatter(x, dest.reshape(1, K * T))


def _sc_gather_rows(y, dest):
    K, T = dest.shape
    W = y.shape[1]
    win = SC_WINDOW

    @functools.partial(pl.kernel, out_type=jax.ShapeDtypeStruct((K * T, W), y.dtype), mesh=_sc_mesh(),
                       scratch_types=[])
    def gather(y_hbm, i_hbm, o_hbm):
        def body(i_vmem, o_vmem):
            pltpu.sync_copy(y_hbm.at[i_vmem.at[0]], o_vmem)

        pltpu.emit_pipeline(
            body,
            grid=(K * T // win,),
            in_specs=[pl.BlockSpec((1, win), lambda j: (0, j))],
            out_specs=[pl.BlockSpec((win, W), lambda j: (j, 0), pipeline_mode=pl.Buffered(1))],
            core_axis_name=("c", "s"),
            dimension_semantics=(pltpu.PARALLEL,),
        )(i_hbm, o_hbm)

    return gather(y, dest.reshape(1, K * T)).reshape(K, T, W)


def _expert_kernel(be_ref, nu_ref, x_ref, wi_ref, wo_ref, y_ref, wi_b, wo_b):
    i = pl.program_id(0)

    @pl.when(i < nu_ref[0])
    def _():
        @pl.when(jnp.logical_or(i == 0, be_ref[i] != be_ref[jnp.maximum(i - 1, 0)]))
        def _():
            wi_b[...] = wi_ref[0, 0].astype(BF16)
            wo_b[...] = wo_ref[0, 0].astype(BF16)

        ff = wo_b.shape[0]
        hcat = _dot_halves(x_ref[...], wi_b)
        act = (_silu(hcat[:, :ff]) * hcat[:, ff:]).astype(BF16)
        y_ref[...] = _pack_halves(_dot(act, wo_b[...]))

    @pl.when(i >= nu_ref[0])
    def _():
        y_ref[...] = jnp.zeros_like(y_ref)


def _experts(xs, block_exp, n_used, w_in, w_out, layer):
    P, half = xs.shape
    D, ff2 = w_in.shape[-2:]
    nblk = P // MOE_BLOCK
    row_spec = pl.BlockSpec((MOE_BLOCK, half), lambda i, be, nu: (i, 0))
    return pl.pallas_call(
        _expert_kernel,
        grid_spec=pltpu.PrefetchScalarGridSpec(
            num_scalar_prefetch=2,
            grid=(nblk,),
            in_specs=[row_spec,
                      pl.BlockSpec((1, 1, D, ff2), lambda i, be, nu: (layer, be[i], 0, 0)),
                      pl.BlockSpec((1, 1, ff2 // 2, D), lambda i, be, nu: (layer, be[i], 0, 0))],
            out_specs=row_spec,
            scratch_shapes=[pltpu.VMEM((D, ff2), BF16), pltpu.VMEM((ff2 // 2, D), BF16)]),
        out_shape=jax.ShapeDtypeStruct((P, half), I32),
        compiler_params=_cparams(("arbitrary",)),
    )(block_exp, n_used, xs, w_in, w_out)


def _combine_kernel(yg_ref, gw_ref, u_ref, wsi_ref, wso_ref, h_ref, mod_ref, g_ref, b_ref, o_ref):
    ff = wso_ref.shape[0]
    hcat = _dot_halves(u_ref[...], wsi_ref)
    shared = _dot((_silu(hcat[:, :ff]) * hcat[:, ff:]).astype(BF16), wso_ref[...])
    lo, hi = None, None
    for k in range(TOP_K):
        y_lo, y_hi = _unpack_halves(yg_ref[k])
        gk = gw_ref[:, k:k + 1]
        lo = y_lo * gk if lo is None else lo + y_lo * gk
        hi = y_hi * gk if hi is None else hi + y_hi * gk
    routed = jnp.concatenate([lo, hi], axis=1)
    z = DEEPNORM_ALPHA * h_ref[...] + mod_ref[0] * (routed + shared)
    o_ref[...] = _layer_norm(z, g_ref[...], b_ref[...])


def _combine(yg, gw_t, u, ws_in, ws_out, h, gate, gate_index, ln_g, ln_b):
    T, D = h.shape
    tm = ROW_TILE
    vec_spec = pl.BlockSpec((1, D), lambda i: (0, 0))
    row_spec = pl.BlockSpec((tm, D), lambda i: (i, 0))
    packed_spec = pl.BlockSpec((tm, D // 2), lambda i: (i, 0))
    return pl.pallas_call(
        _combine_kernel,
        grid=(T // tm,),
        in_specs=[pl.BlockSpec((TOP_K, tm, D // 2), lambda i: (0, i, 0)),
                  pl.BlockSpec((tm, TOP_K), lambda i: (i, 0)),
                  packed_spec,
                  pl.BlockSpec(ws_in.shape, lambda i: (0, 0)),
                  pl.BlockSpec(ws_out.shape, lambda i: (0, 0)),
                  row_spec,
                  pl.BlockSpec((1,) + gate.shape[1:], lambda i: (gate_index(i), 0, 0)),
                  vec_spec, vec_spec],
        out_specs=row_spec,
        out_shape=jax.ShapeDtypeStruct((T, D), F32),
        compiler_params=_cparams(("arbitrary",)),
    )(yg, gw_t, u, ws_in.astype(BF16), ws_out.astype(BF16), h, gate,
      ln_g.reshape(1, D), ln_b.reshape(1, D))


def _slots_kernel(e_ref, rank_ref, start_ref, dest_ref):
    io_e = lax.broadcasted_iota(I32, (N_EXPERTS, e_ref.shape[1]), 0)
    rows = [jnp.sum(jnp.where(io_e == e_ref[k:k + 1, :], start_ref[...], 0), axis=0, keepdims=True)
            for k in range(TOP_K)]
    dest_ref[...] = jnp.concatenate(rows, axis=0) + rank_ref[...]


def _slots(eidx, rank, pstart):
    T = eidx.shape[1]
    tm = ROW_TILE
    tok_spec = pl.BlockSpec((TOP_K, tm), lambda i: (0, i))
    return pl.pallas_call(
        _slots_kernel,
        grid=(T // tm,),
        in_specs=[tok_spec, tok_spec, pl.BlockSpec((N_EXPERTS, 1), lambda i: (0, 0))],
        out_specs=tok_spec,
        out_shape=jax.ShapeDtypeStruct((TOP_K, T), I32),
        compiler_params=_cparams(("arbitrary",)),
    )(eidx, rank, pstart.reshape(N_EXPERTS, 1))


def _moe_layer(u, h, gate, gate_index, router, bias, w_in, w_out, ws_in, ws_out, ln_g, ln_b, layer):
    T = u.shape[0]
    eidx, gw, rank, counts = _router(u, router, bias)
    counts = counts[:, 0]
    padded = (counts + MOE_BLOCK - 1) // MOE_BLOCK * MOE_BLOCK
    pend = jnp.cumsum(padded)
    pstart = (pend - padded).astype(I32)
    dest = _slots(eidx, rank, pstart)
    n_blocks = -(-(T * TOP_K + N_EXPERTS * (MOE_BLOCK - 1)) // MOE_BLOCK)
    block_exp = jnp.minimum(jnp.searchsorted(pend, jnp.arange(n_blocks, dtype=I32) * MOE_BLOCK, side='right'),
                            N_EXPERTS - 1).astype(I32)
    n_used = (pend[-1:] // MOE_BLOCK).astype(I32)
    xs = _sc_scatter_rows(u, dest, n_blocks * MOE_BLOCK)
    y = _experts(xs, block_exp, n_used, w_in, w_out, layer)
    return _combine(_sc_gather_rows(y, dest), gw.T, u, ws_in, ws_out, h, gate, gate_index, ln_g, ln_b)


def _seg_ones(width=LANES):
    idx = np.arange(width) // HEAD_DIM
    return jnp.asarray((idx[:, None] == idx[None, :]).astype(np.float32), BF16)


def _head_sum(x, ones_ref):
    outs = []
    for j in range(x.shape[1] // LANES):
        xc = x[:, j * LANES:(j + 1) * LANES]
        hi = xc.astype(BF16)
        lo = (xc - hi.astype(F32)).astype(BF16)
        outs.append(_dot(hi, ones_ref[...]) + _dot(lo, ones_ref[...]))
    return jnp.concatenate(outs, axis=1)


def _rwkv_proj_kernel(u_ref, dx_ref, mu_ref, wrkv_ref, g1_ref, g2_ref, d1_ref, d2_ref, d0_ref,
                      i1_ref, i2_ref, i0_ref, kk_ref, ka_ref, rk_ref, ones_ref,
                      r_ref, v_ref, a_ref, g_ref, bonus_ref, w_ref, k_ref, b_ref):
    u = u_ref[...]
    dx = dx_ref[...]
    mix = lambda m: (u + dx * mu_ref[m:m + 1, :])
    xr, xw, xk, xv, xa, xg = [mix(m) for m in range(6)]
    r = _dot(xr.astype(BF16), wrkv_ref[0])
    k = _dot(xk.astype(BF16), wrkv_ref[1])
    v = _dot(xv.astype(BF16), wrkv_ref[2])
    g = _dot(jax.nn.sigmoid(_dot(xg.astype(BF16), g1_ref[...])).astype(BF16), g2_ref[...])
    kk = k * kk_ref[...]
    kk = kk * lax.rsqrt(jnp.maximum(_head_sum(kk * kk, ones_ref), 1e-24))
    r_ref[...] = r
    v_ref[...] = v
    a_ref[...] = -kk
    g_ref[...] = g
    k_sum = None
    xw_b = xw.astype(BF16)
    xa_b = xa.astype(BF16)
    for d in range(2):
        lw = d0_ref[d:d + 1, :] + _dot(jnp.tanh(_dot(xw_b, d1_ref[d])).astype(BF16), d2_ref[d])
        softplus = jnp.maximum(-lw, 0.0) + jnp.log(1.0 + jnp.exp(-jnp.abs(lw)))
        logw = -softplus - 0.5
        w_ref[d] = jnp.exp(-jnp.exp(logw))
        eta = jax.nn.sigmoid(i0_ref[d:d + 1, :] + _dot(_dot(xa_b, i1_ref[d]).astype(BF16), i2_ref[d]))
        k_d = k * (1.0 + (eta - 1.0) * ka_ref[...])
        k_ref[d] = k_d
        b_ref[d] = kk * eta
        k_sum = k_d if k_sum is None else k_sum + k_d
    bonus_ref[...] = _head_sum(r * k_sum * rk_ref[...], ones_ref) * v


def _rwkv_proj(u, dx, p):
    T, D = u.shape
    tm = PROJ_TILE
    row = pl.BlockSpec((tm, D), lambda i: (i, 0))
    row2 = pl.BlockSpec((2, tm, D), lambda i: (0, i, 0))
    full = lambda a: pl.BlockSpec(a.shape, lambda i: (0,) * a.ndim)
    bf = lambda a: a.astype(BF16)
    consts = [p['mu'], bf(p['w_rkv']), bf(p['gate1']), bf(p['gate2']), bf(p['dec1']), bf(p['dec2']), p['dec0'],
              bf(p['icl1']), bf(p['icl2']), p['icl0'], p['k_k'].reshape(1, D), p['k_a'].reshape(1, D),
              p['r_k'].reshape(1, D), _seg_ones()]
    one = jax.ShapeDtypeStruct((T, D), F32)
    two = jax.ShapeDtypeStruct((2, T, D), F32)
    return pl.pallas_call(
        _rwkv_proj_kernel,
        grid=(T // tm,),
        in_specs=[row, row] + [full(a) for a in consts],
        out_specs=[row, row, row, row, row, row2, row2, row2],
        out_shape=[one, one, one, one, one, two, two, two],
        compiler_params=_cparams(("arbitrary",)),
    )(u, dx, *consts)


def _scan_kernel(r_ref, w_ref, k_ref, v_ref, a_ref, b_ref, ones_ref, hsel_ref,
                 y_ref, s_ref, vt_ref):
    d = pl.program_id(0)
    c = pl.program_id(2)
    tc, nb = r_ref.shape[0], r_ref.shape[1]
    tw = SCAN_TILE
    n_wide = r_ref.shape[2] // tw
    heads = tw // HEAD_DIM
    assert heads * tc == tw

    @pl.when(c == 0)
    def _():
        s_ref[...] = jnp.zeros_like(s_ref)

    for bb in range(nb):
        for q in range(n_wide):
            vt = v_ref[:, bb, q * tw:(q + 1) * tw].T
            vt_ref[bb * n_wide + q] = jnp.concatenate(
                [vt[h * HEAD_DIM:(h + 1) * HEAD_DIM] for h in range(heads)], axis=1)

    head_base = (lax.broadcasted_iota(I32, (HEAD_DIM, LANES), 1) // HEAD_DIM) * tc
    tiles = [(bb, q) for bb in range(nb) for q in range(n_wide)]
    groups = [tiles[i:i + SCAN_GROUP] for i in range(0, len(tiles), SCAN_GROUP)]

    def stacked(grp, get):
        def wide(bb, q):
            return jnp.concatenate(
                [jnp.broadcast_to(get(bb, slice(q * tw + hf * LANES, q * tw + (hf + 1) * LANES)), (HEAD_DIM, LANES))
                 for hf in range(tw // LANES)], axis=1)
        return jnp.concatenate([wide(bb, q) for bb, q in grp], axis=0)

    def load_state(grp):
        return jnp.concatenate([s_ref[bb * n_wide + q] for bb, q in grp], axis=0)

    def emit_y(grp, st, t_y):
        r_rows = stacked(grp, lambda bb, cols: r_ref[t_y, bb:bb + 1, cols])
        yh = _dot_nt(hsel_ref[...], (st * r_rows).astype(BF16))
        first = tiles.index(grp[0])
        y_ref[0, t_y, :, first * HEAD_DIM:(first + len(grp)) * HEAD_DIM] = yh

    def step(s_i, carry):
        t = jnp.where(d == 0, s_i, tc - 1 - s_i)
        t_prev = jnp.where(s_i == 0, t, jnp.where(d == 0, t - 1, t + 1))
        pick = head_base + t
        for grp in groups:
            one = lambda ref: stacked(grp, lambda bb, cols: ref[t, bb:bb + 1, cols])
            two = lambda ref: stacked(grp, lambda bb, cols: ref[0, t, bb:bb + 1, cols])
            st = load_state(grp)
            sa = _dot((st * one(a_ref)).astype(BF16), ones_ref[...])
            emit_y(grp, st, t_prev)
            vcol = jnp.concatenate(
                [jnp.concatenate([jnp.take_along_axis(vt_ref[bb * n_wide + q, :, hf * LANES:(hf + 1) * LANES],
                                                      pick, axis=1) for hf in range(tw // LANES)], axis=1)
                 for bb, q in grp], axis=0)
            st = st * two(w_ref) + sa * two(b_ref) + vcol * two(k_ref)
            for j, (bb, q) in enumerate(grp):
                s_ref[bb * n_wide + q] = st[j * HEAD_DIM:(j + 1) * HEAD_DIM]
        return carry

    lax.fori_loop(0, tc, step, 0, unroll=SCAN_UNROLL)
    t_last = jnp.where(d == 0, tc - 1, 0)
    for grp in groups:
        emit_y(grp, load_state(grp), t_last)


def _wkv_scan(r, w, k, v, a, b, n_ctx):
    N, B, D = r.shape
    tc = SCAN_CHUNK
    wc = SCAN_COLS
    n_wide = wc // SCAN_TILE
    nc = N // tc
    ncc = n_ctx // tc

    def chunk(d, c):
        rev = jnp.where(c < ncc, ncc - 1 - c, nc - 1 - (c - ncc))
        return jnp.where(d == 0, c, rev)

    one = pl.BlockSpec((tc, B, wc), lambda d, g, c: (chunk(d, c), 0, g))
    two = pl.BlockSpec((1, tc, B, wc), lambda d, g, c: (d, chunk(d, c), 0, g))
    seg = np.arange(SCAN_TILE) // HEAD_DIM
    hsel = np.zeros((8, SCAN_TILE), np.float32)
    for hh in range(SCAN_TILE // HEAD_DIM):
        hsel[hh, seg == hh] = 1.0
    const = lambda a_: pl.BlockSpec(a_.shape, lambda d, g, c: (0, 0))
    consts = [_seg_ones(SCAN_TILE), jnp.asarray(hsel, BF16)]
    heads = SCAN_TILE // HEAD_DIM
    ncg = D // wc
    y = pl.pallas_call(
        _scan_kernel,
        grid=(2, ncg, nc),
        in_specs=[one, two, two, one, one, two] + [const(a_) for a_ in consts],
        out_specs=pl.BlockSpec((1, tc, 8, B * n_wide * HEAD_DIM), lambda d, g, c: (d, chunk(d, c), 0, g)),
        out_shape=jax.ShapeDtypeStruct((2, N, 8, ncg * B * n_wide * HEAD_DIM), F32),
        scratch_shapes=[pltpu.VMEM((B * n_wide, HEAD_DIM, SCAN_TILE), F32),
                        pltpu.VMEM((B * n_wide, HEAD_DIM, SCAN_TILE), F32)],
        compiler_params=_cparams(("arbitrary", "arbitrary", "arbitrary")),
    )(r, w, k, v, a, b, *consts)
    y = y[:, :, :heads].reshape(2, N, heads, ncg, B, n_wide, HEAD_DIM)
    return jnp.transpose(y, (0, 1, 4, 3, 5, 2, 6)).reshape(2, N, B, D)


def _rwkv_out_kernel(y0_ref, y1_ref, bonus_ref, g_ref, lnx_ref, ones_ref, w_ref, h_ref, mod_ref, lg_ref, lb_ref,
                     hn_ref, u_ref):
    y = y0_ref[0] + y1_ref[0]
    ym = _head_sum(y, ones_ref) * (1.0 / HEAD_DIM)
    yc = y - ym
    yv = _head_sum(yc * yc, ones_ref) * (1.0 / HEAD_DIM)
    yn = yc * lax.rsqrt(yv + LNX_EPS) * lnx_ref[0:1, :] + lnx_ref[1:2, :]
    x = ((yn + bonus_ref[...]) * g_ref[...]).astype(BF16)
    o = _dot(x, w_ref[...])
    z = DEEPNORM_ALPHA * h_ref[...] + mod_ref[0] * o
    hn = _layer_norm(z, lg_ref[...], lb_ref[...])
    hn_ref[...] = hn
    u_ref[...] = _pack_halves(hn * (1.0 + mod_ref[2]) + mod_ref[1])


def _rwkv_out(y, bonus, g, lnx, w_out, h, mod_rows, ln_g, ln_b, row0):
    T, D = h.shape
    tm = ROW_TILE
    t0 = row0 // tm
    off = pl.BlockSpec((tm, D), lambda i: (i + t0, 0))
    out = pl.BlockSpec((tm, D), lambda i: (i, 0))
    full = lambda a: pl.BlockSpec(a.shape, lambda i: (0,) * a.ndim)
    vec = pl.BlockSpec((1, D), lambda i: (0, 0))
    ones = _seg_ones()
    w_b = w_out.astype(BF16)
    return pl.pallas_call(
        _rwkv_out_kernel,
        grid=((T - row0) // tm,),
        in_specs=[pl.BlockSpec((1, tm, D), lambda i: (0, i + t0, 0)),
                  pl.BlockSpec((1, tm, D), lambda i: (1, i + t0, 0)),
                  off, off, full(lnx), full(ones), full(w_b), off, full(mod_rows), vec, vec],
        out_specs=[out, pl.BlockSpec((tm, D // 2), lambda i: (i, 0))],
        out_shape=[jax.ShapeDtypeStruct((T - row0, D), F32), jax.ShapeDtypeStruct((T - row0, D // 2), I32)],
        compiler_params=_cparams(("arbitrary",)),
    )(y, y, bonus, g, lnx, ones, w_b, h, mod_rows, ln_g.reshape(1, D), ln_b.reshape(1, D))


def _shift_delta(u, n_ctx):
    def seg(x):
        xp = jnp.pad(x, ((1, 1), (0, 0), (0, 0)))
        return 0.5 * (xp[:-2] + xp[2:]) - x
    return jnp.concatenate([seg(u[:n_ctx]), seg(u[n_ctx:])], axis=0)


def kernel(x, c, ctx, c_ctx, ada_w, ada_b, post_ln_g, post_ln_b, att_w_in, att_w_out, att_sink, diff_lambda_vecs, diff_subln_g, rk_mu, rk_w_rkv, rk_w_out, rk_decay0, rk_decay1, rk_decay2, rk_iclr0, rk_iclr1, rk_iclr2, rk_gate1, rk_gate2, rk_k_k, rk_k_a, rk_r_k, rk_lnx, moe_router, moe_bias, moe_w_in, moe_w_out, moe_ws_in, moe_ws_out):
    B, S, D = x.shape
    L = ctx.shape[1]
    N = L + S
    tm = ROW_TILE
    assert L % tm == 0 and S % tm == 0 and L % SCAN_CHUNK == 0 and S % SCAN_CHUNK == 0
    assert tm % B == 0 and D % SCAN_COLS == 0

    rows = -(-(B + 1) // 8) * 8
    cvec = jnp.concatenate([c, c_ctx[None, :], jnp.zeros((rows - B - 1, D), F32)], axis=0)
    mods = [_mod_table(_ada_mod(cvec, ada_w[i], ada_b[i]), B, D) for i in range(DEPTH)]

    h0 = jnp.concatenate([ctx, x], axis=1)
    lam_init = 0.8 - 0.6 * math.exp(-0.3 * 0)
    qa, ka, va, qb, kb, vb = _attn_inproj(h0, mods[0], att_w_in[0], L)
    oa = _win_attn(qa, ka, va, att_sink[0], L)
    ob = _diff_attn(qb, kb, vb, diff_lambda_vecs[0], diff_subln_g[0], lam_init, L)
    h1, u1 = _mix_out([oa, ob], [att_w_out[0][:A_WIDTH], att_w_out[0][A_WIDTH:]], h0, mods[0],
                      post_ln_g[0, 0], post_ln_b[0, 0], L, 0)
    tiles_b, tiles_c = N // tm, L // tm
    gate0 = mods[0][:, :, 5].reshape(B * 2, 1, D)
    gate0_index = lambda i: (i // tiles_b) * 2 + jnp.minimum((i % tiles_b) // tiles_c, 1)
    h2 = _moe_layer(u1.reshape(B * N, D // 2), h1.reshape(B * N, D), gate0, gate0_index, moe_router[0], moe_bias[0],
                    moe_w_in, moe_w_out, moe_ws_in[0], moe_ws_out[0],
                    post_ln_g[0, 1], post_ln_b[0, 1], 0).reshape(B, N, D)

    m_ctx, m_lat = mods[1][:, 0], mods[1][:, 1]
    h2_t = jnp.swapaxes(h2, 0, 1)
    is_lat = (jnp.arange(N) >= L)[:, None, None]
    u = h2_t * (1.0 + jnp.where(is_lat, m_lat[:, 1], m_ctx[:, 1])) + jnp.where(is_lat, m_lat[:, 0], m_ctx[:, 0])
    dx = _shift_delta(u, L)
    params = dict(mu=rk_mu[0], w_rkv=rk_w_rkv[0], gate1=rk_gate1[0], gate2=rk_gate2[0],
                  dec0=rk_decay0[0], dec1=rk_decay1[0], dec2=rk_decay2[0],
                  icl0=rk_iclr0[0], icl1=rk_iclr1[0], icl2=rk_iclr2[0],
                  k_k=rk_k_k[0], k_a=rk_k_a[0], r_k=rk_r_k[0])
    r, v, a, g, bonus, w2, k2, b2 = _rwkv_proj(u.reshape(N * B, D), dx.reshape(N * B, D), params)
    tmaj = lambda t: t.reshape(t.shape[:-2] + (N, B, D))
    y = _wkv_scan(tmaj(r), tmaj(w2), tmaj(k2), tmaj(v), tmaj(a), tmaj(b2), L)
    lat_rows = lambda j: jnp.tile(m_lat[:, j], (tm // B, 1))
    h3, u3 = _rwkv_out(y.reshape(2, N * B, D), bonus, g, rk_lnx[0], rk_w_out[0], h2_t.reshape(N * B, D),
                       jnp.stack([lat_rows(2), lat_rows(3), lat_rows(4)]),
                       post_ln_g[1, 0], post_ln_b[1, 0], L * B)
    out = _moe_layer(u3, h3, lat_rows(5)[None], lambda i: 0, moe_router[1], moe_bias[1],
                     moe_w_in, moe_w_out, moe_ws_in[1], moe_ws_out[1],
                     post_ln_g[1, 1], post_ln_b[1, 1], 1)
    return jnp.swapaxes(out.reshape(S, B, D), 0, 1)
```

```python
import functools
import math

import numpy as np
import jax
import jax.numpy as jnp
from jax import lax
from jax.experimental import pallas as pl
from jax.experimental.pallas import tpu as pltpu
from jax.experimental.pallas import tpu_sc as plsc

F32 = jnp.float32
BF16 = jnp.bfloat16
I32 = jnp.int32

HEAD_DIM = 64
GRID_W = 64
ROPE_AXIS_DIM = HEAD_DIM // 2
ROPE_THETA = 10000.0
Q_BLOCK = 128
A_Q_HEADS = 8
A_KV_HEADS = 2
A_GROUP = A_Q_HEADS // A_KV_HEADS
A_WIDTH = A_Q_HEADS * HEAD_DIM
A_KV_WIDTH = A_KV_HEADS * HEAD_DIM
B_HEADS = 4
B_V_DIM = 2 * HEAD_DIM
B_WIDTH = B_HEADS * B_V_DIM
LNX_EPS = 64e-5
N_EXPERTS = 256
TOP_K = 8
N_GROUPS = 8
TOPK_GROUPS = 4
ROUTED_SCALE = 2.5
MOE_BLOCK = 256
LN_EPS = 1e-5
SUBLN_EPS = 1e-5
NEG_INF = -1e30
DEPTH = 2
DEEPNORM_ALPHA = (2 * DEPTH) ** 0.25

LANES = 128
ROW_TILE = 256
PROJ_TILE = 128
SC_WINDOW = 128
SCAN_CHUNK = 64
SCAN_COLS = 1024
SCAN_TILE = 256
SCAN_GROUP = 8
SCAN_UNROLL = 4
VMEM_LIMIT = 56 * 1024 * 1024


def _cparams(sem):
    return pltpu.CompilerParams(dimension_semantics=sem, vmem_limit_bytes=VMEM_LIMIT)


def _silu(x):
    return x * jax.nn.sigmoid(x)


def _layer_norm(z, g, b):
    mu = jnp.mean(z, -1, keepdims=True)
    zc = z - mu
    var = jnp.mean(zc * zc, -1, keepdims=True)
    return zc * lax.rsqrt(var + LN_EPS) * g + b


def _dot(a, b):
    return jnp.dot(a, b, preferred_element_type=F32)


def _dot_nt(a, b):
    return lax.dot_general(a, b, (((1,), (1,)), ((), ())), preferred_element_type=F32)


def _pack_halves(x):
    half = x.shape[1] // 2
    bits = lambda v: lax.bitcast_convert_type(v.astype(BF16).astype(F32), I32)
    return lax.shift_right_logical(bits(x[:, :half]), 16) | bits(x[:, half:])


def _unpack_halves(p):
    lo = lax.bitcast_convert_type(lax.shift_left(p, 16), F32)
    hi = lax.bitcast_convert_type(p & jnp.int32(-65536), F32)
    return lo, hi


def _dot_halves(p, w_ref_or_array):
    lo, hi = _unpack_halves(p)
    half = p.shape[1]
    return _dot(lo.astype(BF16), w_ref_or_array[:half]) + _dot(hi.astype(BF16), w_ref_or_array[half:])


def _ada_kernel(c_ref, w_ref, b_ref, o_ref):
    c = c_ref[...]
    o_ref[...] = jnp.dot(_silu(c), w_ref[...], preferred_element_type=F32,
                         precision=lax.Precision.HIGHEST) + b_ref[...]


def _ada_mod(cvec, w, bias):
    R, D = cvec.shape
    n_out = w.shape[1]
    tn = 768
    return pl.pallas_call(
        _ada_kernel,
        grid=(n_out // tn,),
        in_specs=[pl.BlockSpec((R, D), lambda j: (0, 0)),
                  pl.BlockSpec((D, tn), lambda j: (0, j)),
                  pl.BlockSpec((1, tn), lambda j: (0, j))],
        out_specs=pl.BlockSpec((R, tn), lambda j: (0, j)),
        out_shape=jax.ShapeDtypeStruct((R, n_out), F32),
        compiler_params=_cparams(("arbitrary",)),
    )(cvec, w, bias.reshape(1, n_out))


def _mod_table(m, batch, d):
    m_lat = m[:batch].reshape(batch, 6, d)
    m_ctx = jnp.broadcast_to(m[batch].reshape(1, 6, d), (batch, 6, d))
    return jnp.stack([m_ctx, m_lat], axis=1)


def _mod_spec(d, ctx_tiles):
    return pl.BlockSpec((1, 1, 6, d), lambda b, i: (b, jnp.minimum(i // ctx_tiles, 1), 0, 0))


def _rope_tables(n_ctx, n_lat):
    rows = n_lat // GRID_W
    row = np.repeat(np.arange(rows), GRID_W).astype(np.float32)
    col = np.tile(np.arange(GRID_W), rows).astype(np.float32)
    inv = (ROPE_THETA ** (-np.arange(0, ROPE_AXIS_DIM, 2, dtype=np.float32) / ROPE_AXIS_DIM)).astype(np.float32)
    ar = row[:, None] * inv
    ac = col[:, None] * inv
    ang = np.concatenate([ar, ar, ac, ac], -1)
    cos = np.cos(ang).astype(np.float32)
    sin = np.sin(ang).astype(np.float32)
    lower = (np.arange(HEAD_DIM) % ROPE_AXIS_DIM) < (ROPE_AXIS_DIM // 2)
    sin_up = np.where(lower[None, :], -sin, 0.0)
    sin_dn = np.where(lower[None, :], 0.0, sin)

    def full(t, ctx_fill):
        t = np.concatenate([np.full((n_ctx, HEAD_DIM), ctx_fill, np.float32), t], 0)
        return jnp.asarray(np.tile(t, (1, LANES // HEAD_DIM)))

    return full(cos, 1.0), full(sin_up, 0.0), full(sin_dn, 0.0)


def _inproj_kernel(h_ref, mod_ref, w_ref, cos_ref, su_ref, sd_ref,
                   qa_ref, ka_ref, va_ref, qb_ref, kb_ref, vb_ref):
    h = h_ref[0]
    shift = mod_ref[0, 0, 0:1, :]
    scale = mod_ref[0, 0, 1:2, :]
    u = (h * (1.0 + scale) + shift).astype(BF16)
    y = _dot(u, w_ref[...])
    cos, s_up, s_dn = cos_ref[...], su_ref[...], sd_ref[...]
    q_scale = HEAD_DIM ** -0.5

    def rope(xc):
        half = ROPE_AXIS_DIM // 2
        return xc * cos + pltpu.roll(xc, LANES - half, 1) * s_up + pltpu.roll(xc, half, 1) * s_dn

    def emit(out_ref, col0, width, roped, mul):
        for j in range(width // LANES):
            xc = y[:, col0 + j * LANES: col0 + (j + 1) * LANES]
            if roped:
                xc = rope(xc)
            if mul != 1.0:
                xc = xc * mul
            out_ref[0, :, j * LANES:(j + 1) * LANES] = xc.astype(out_ref.dtype)

    c = 0
    emit(qa_ref, c, A_WIDTH, True, q_scale); c += A_WIDTH
    emit(ka_ref, c, A_KV_WIDTH, True, 1.0); c += A_KV_WIDTH
    emit(va_ref, c, A_KV_WIDTH, False, 1.0); c += A_KV_WIDTH
    emit(qb_ref, c, B_WIDTH, True, q_scale); c += B_WIDTH
    emit(kb_ref, c, B_WIDTH, True, 1.0); c += B_WIDTH
    emit(vb_ref, c, B_WIDTH, False, 1.0)


def _attn_inproj(h, mod, w_in, n_ctx):
    B, N, D = h.shape
    tm = ROW_TILE
    cos, s_up, s_dn = _rope_tables(n_ctx, N - n_ctx)
    widths = (A_WIDTH, A_KV_WIDTH, A_KV_WIDTH, B_WIDTH, B_WIDTH, B_WIDTH)
    tab_spec = pl.BlockSpec((tm, LANES), lambda b, i: (i, 0))
    return pl.pallas_call(
        _inproj_kernel,
        grid=(B, N // tm),
        in_specs=[pl.BlockSpec((1, tm, D), lambda b, i: (b, i, 0)),
                  _mod_spec(D, n_ctx // tm),
                  pl.BlockSpec(w_in.shape, lambda b, i: (0, 0)),
                  tab_spec, tab_spec, tab_spec],
        out_specs=[pl.BlockSpec((1, tm, w), lambda b, i: (b, i, 0)) for w in widths],
        out_shape=[jax.ShapeDtypeStruct((B, N, w), BF16) for w in widths],
        compiler_params=_cparams(("arbitrary", "arbitrary")),
    )(h, mod, w_in.astype(BF16), cos, s_up, s_dn)


def _win_attn_kernel(n_ctx_blocks, n_blocks, q_ref, kc_ref, vc_ref, kl_ref, km_ref, kr_ref,
                     vl_ref, vm_ref, vr_ref, sink_ref, o_ref):
    j = pl.program_id(1)
    is_lat = j >= n_ctx_blocks
    qb = Q_BLOCK
    n_c = kc_ref.shape[1]
    rows = A_GROUP * qb
    n_keys = n_c + 3 * qb
    far = 1 << 20
    r_idx = lax.broadcasted_iota(I32, (rows, n_keys), 0) % qb
    cw = lax.broadcasted_iota(I32, (rows, n_keys), 1) - n_c
    off_l = jnp.where(jnp.logical_and(is_lat, j > n_ctx_blocks), 0, far)
    end_m = jnp.where(is_lat, 2 * qb, qb)
    off_r = jnp.where(jnp.logical_and(is_lat, j < n_blocks - 1), 0, far)
    valid = ((cw < 0)
             | ((cw >= 0) & (cw < qb) & (cw >= r_idx + off_l))
             | ((cw >= qb) & (cw < end_m))
             | ((cw >= 2 * qb) & (cw - 2 * qb + off_r <= r_idx)))
    outs = []
    for kv in range(A_KV_HEADS):
        cols = slice(kv * HEAD_DIM, (kv + 1) * HEAD_DIM)
        k_all = jnp.concatenate([kc_ref[0, :, cols], kl_ref[0, :, cols], km_ref[0, :, cols],
                                 kr_ref[0, :, cols]], axis=0)
        v_all = jnp.concatenate([vc_ref[0, :, cols], vl_ref[0, :, cols], vm_ref[0, :, cols],
                                 vr_ref[0, :, cols]], axis=0)
        q0 = kv * A_GROUP
        q = jnp.concatenate([q_ref[0, :, (q0 + g) * HEAD_DIM:(q0 + g + 1) * HEAD_DIM]
                             for g in range(A_GROUP)], axis=0)
        sink = jnp.concatenate([jnp.broadcast_to(sink_ref[q0 + g:q0 + g + 1, 0:1], (qb, 1))
                                for g in range(A_GROUP)], axis=0)
        s = jnp.where(valid, _dot_nt(q, k_all), NEG_INF)
        m = jnp.maximum(jnp.max(s, -1, keepdims=True), sink)
        e = jnp.exp(s - m)
        denom = jnp.sum(e, -1, keepdims=True) + jnp.exp(sink - m)
        o = _dot(e.astype(BF16), v_all) * (1.0 / denom)
        outs += [o[g * qb:(g + 1) * qb] for g in range(A_GROUP)]
    for j2 in range(A_Q_HEADS // 2):
        pair = jnp.concatenate([outs[2 * j2], outs[2 * j2 + 1]], axis=1)
        o_ref[0, :, j2 * LANES:(j2 + 1) * LANES] = pair.astype(o_ref.dtype)


def _win_attn(qa, ka, va, sink, n_ctx):
    B, N, _ = qa.shape
    qb = Q_BLOCK
    nb = N // qb
    ncb = n_ctx // qb
    sink_pad = jnp.broadcast_to(sink.reshape(A_Q_HEADS, 1).astype(F32), (A_Q_HEADS, LANES))

    def left(b, j):
        return (b, jnp.clip(j - 1, ncb, nb - 1), 0)

    def mid(b, j):
        return (b, jnp.clip(j, ncb, nb - 1), 0)

    def right(b, j):
        return (b, jnp.clip(j + 1, ncb, nb - 1), 0)

    kv_blk = lambda im: pl.BlockSpec((1, qb, A_KV_WIDTH), im)
    ctx_blk = pl.BlockSpec((1, n_ctx, A_KV_WIDTH), lambda b, j: (b, 0, 0))
    return pl.pallas_call(
        functools.partial(_win_attn_kernel, ncb, nb),
        grid=(B, nb),
        in_specs=[pl.BlockSpec((1, qb, A_WIDTH), lambda b, j: (b, j, 0)),
                  ctx_blk, ctx_blk,
                  kv_blk(left), kv_blk(mid), kv_blk(right),
                  kv_blk(left), kv_blk(mid), kv_blk(right),
                  pl.BlockSpec((A_Q_HEADS, LANES), lambda b, j: (0, 0))],
        out_specs=pl.BlockSpec((1, qb, A_WIDTH), lambda b, j: (b, j, 0)),
        out_shape=jax.ShapeDtypeStruct((B, N, A_WIDTH), BF16),
        compiler_params=_cparams(("arbitrary", "arbitrary")),
    )(qa, ka, va, ka, ka, ka, va, va, va, sink_pad)


def _diff_attn_kernel(n_ctx, lam_init, q_ref, k_ref, v_ref, lv_ref, g_ref, o_ref):
    j = pl.program_id(1)
    lv = lv_ref[...]
    lam = (jnp.exp(jnp.sum(lv[0:1] * lv[1:2], -1, keepdims=True))
           - jnp.exp(jnp.sum(lv[2:3] * lv[3:4], -1, keepdims=True)) + lam_init)
    gain = g_ref[...] * (1.0 - lam_init)

    def run(n_keys):
        for hd in range(B_HEADS):
            parts = []
            for mm in range(2):
                c0 = (hd * 2 + mm) * HEAD_DIM
                q = q_ref[0, :, c0:c0 + HEAD_DIM]
                k = k_ref[0, :n_keys, c0:c0 + HEAD_DIM]
                s = _dot_nt(q, k)
                e = jnp.exp(s - jnp.max(s, -1, keepdims=True))
                parts.append((e, jnp.sum(e, -1, keepdims=True)))
            (e0, l0), (e1, l1) = parts
            v = v_ref[0, :n_keys, hd * B_V_DIM:(hd + 1) * B_V_DIM]
            o = _dot(e0.astype(BF16), v) * (1.0 / l0) - _dot(e1.astype(BF16), v) * (lam / l1)
            o = o * lax.rsqrt(jnp.mean(o * o, -1, keepdims=True) + SUBLN_EPS) * gain
            o_ref[0, :, hd * B_V_DIM:(hd + 1) * B_V_DIM] = o.astype(o_ref.dtype)

    @pl.when(j == 0)
    def _():
        run(n_ctx)

    @pl.when(j > 0)
    def _():
        run(k_ref.shape[1])


def _diff_attn(qb, kb, vb, lam_vecs, subln_g, lam_init, n_ctx):
    B, N, _ = qb.shape
    tq = n_ctx
    return pl.pallas_call(
        functools.partial(_diff_attn_kernel, n_ctx, lam_init),
        grid=(B, N // tq),
        in_specs=[pl.BlockSpec((1, tq, B_WIDTH), lambda b, j: (b, j, 0)),
                  pl.BlockSpec((1, N, B_WIDTH), lambda b, j: (b, 0, 0)),
                  pl.BlockSpec((1, N, B_WIDTH), lambda b, j: (b, 0, 0)),
                  pl.BlockSpec((4, HEAD_DIM), lambda b, j: (0, 0)),
                  pl.BlockSpec((1, B_V_DIM), lambda b, j: (0, 0))],
        out_specs=pl.BlockSpec((1, tq, B_WIDTH), lambda b, j: (b, j, 0)),
        out_shape=jax.ShapeDtypeStruct((B, N, B_WIDTH), BF16),
        compiler_params=_cparams(("arbitrary", "arbitrary")),
    )(qb, kb, vb, lam_vecs.astype(F32), subln_g.reshape(1, B_V_DIM).astype(F32))


def _mix_out_kernel(n_in, *refs):
    xs = refs[:n_in]
    ws = refs[n_in:2 * n_in]
    h_ref, mod_ref, g_ref, b_ref, hn_ref, u_ref = refs[2 * n_in:]
    o = _dot(xs[0][0], ws[0][...])
    for x_ref, w_ref in zip(xs[1:], ws[1:]):
        o = o + _dot(x_ref[0], w_ref[...])
    z = DEEPNORM_ALPHA * h_ref[0] + mod_ref[0, 0, 2:3, :] * o
    hn = _layer_norm(z, g_ref[...], b_ref[...])
    hn_ref[0] = hn
    u_ref[0] = _pack_halves(hn * (1.0 + mod_ref[0, 0, 4:5, :]) + mod_ref[0, 0, 3:4, :])


def _mix_out(xs, ws, h, mod, ln_g, ln_b, n_ctx, row0):
    B, N, D = h.shape
    tm = ROW_TILE
    t0 = row0 // tm
    n_out = N - row0
    row_spec = lambda w: pl.BlockSpec((1, tm, w), lambda b, i: (b, i + t0, 0))
    out_spec = pl.BlockSpec((1, tm, D), lambda b, i: (b, i, 0))
    vec_spec = pl.BlockSpec((1, D), lambda b, i: (0, 0))
    return pl.pallas_call(
        functools.partial(_mix_out_kernel, len(xs)),
        grid=(B, n_out // tm),
        in_specs=([row_spec(x.shape[-1]) for x in xs]
                  + [pl.BlockSpec(w.shape, lambda b, i: (0, 0)) for w in ws]
                  + [row_spec(D),
                     pl.BlockSpec((1, 1, 6, D), lambda b, i: (b, jnp.minimum((i + t0) // (n_ctx // tm), 1), 0, 0)),
                     vec_spec, vec_spec]),
        out_specs=[out_spec, pl.BlockSpec((1, tm, D // 2), lambda b, i: (b, i, 0))],
        out_shape=[jax.ShapeDtypeStruct((B, n_out, D), F32), jax.ShapeDtypeStruct((B, n_out, D // 2), I32)],
        compiler_params=_cparams(("arbitrary", "arbitrary")),
    )(*xs, *[w.astype(BF16) for w in ws], h, mod, ln_g.reshape(1, D), ln_b.reshape(1, D))


def _router_kernel(u_ref, rt_ref, bias_ref, tri_ref, e_ref, gw_ref, rank_ref, cnt_ref, carry_ref):
    i = pl.program_id(0)

    @pl.when(i == 0)
    def _():
        carry_ref[...] = jnp.zeros_like(carry_ref)

    tm = u_ref.shape[0]
    per_group = N_EXPERTS // N_GROUPS
    neg = -jnp.inf
    u_lo, u_hi = _unpack_halves(u_ref[...])
    half = u_ref.shape[1]
    logits = (_dot_nt(rt_ref[:, :half], u_lo.astype(BF16))
              + _dot_nt(rt_ref[:, half:], u_hi.astype(BF16)))
    scores = jax.nn.sigmoid(logits)
    sel = scores + bias_ref[...]
    io_in = lax.broadcasted_iota(I32, (per_group, tm), 0)
    grp_rows = []
    for gi in range(N_GROUPS):
        sg = sel[gi * per_group:(gi + 1) * per_group]
        m1 = jnp.max(sg, axis=0, keepdims=True)
        i1 = jnp.min(jnp.where(sg == m1, io_in, per_group), axis=0, keepdims=True)
        m2 = jnp.max(jnp.where(io_in == i1, neg, sg), axis=0, keepdims=True)
        grp_rows.append(m1 + m2)
    grp = jnp.concatenate(grp_rows, axis=0)
    io_g = lax.broadcasted_iota(I32, grp.shape, 0)
    g_sel = jnp.zeros(grp.shape, F32)
    for _ in range(TOPK_GROUPS):
        m = jnp.max(grp, axis=0, keepdims=True)
        hit = io_g == jnp.min(jnp.where(grp == m, io_g, N_GROUPS), axis=0, keepdims=True)
        g_sel = jnp.where(hit, 1.0, g_sel)
        grp = jnp.where(hit, neg, grp)
    selm = jnp.concatenate(
        [jnp.where(g_sel[gi:gi + 1] > 0.5, sel[gi * per_group:(gi + 1) * per_group], NEG_INF)
         for gi in range(N_GROUPS)], axis=0)
    io_e = lax.broadcasted_iota(I32, selm.shape, 0)
    chosen_f = jnp.zeros(selm.shape, F32)
    idx, gws = [], []
    for _ in range(TOP_K):
        m = jnp.max(selm, axis=0, keepdims=True)
        ik = jnp.min(jnp.where(selm == m, io_e, N_EXPERTS), axis=0, keepdims=True)
        hit = io_e == ik
        idx.append(ik)
        gws.append(jnp.sum(jnp.where(hit, scores, 0.0), axis=0, keepdims=True))
        chosen_f = jnp.where(hit, 1.0, chosen_f)
        selm = jnp.where(hit, neg, selm)
    gw = jnp.concatenate(gws, axis=0)
    gw_ref[...] = gw / jnp.sum(gw, axis=0, keepdims=True) * ROUTED_SCALE
    e_ref[...] = jnp.concatenate(idx, axis=0)
    before = _dot(chosen_f.astype(BF16), tri_ref[...]) + carry_ref[...]
    ranks = [jnp.sum(jnp.where(io_e == ik, before, 0.0), axis=0, keepdims=True) for ik in idx]
    rank_ref[...] = jnp.concatenate(ranks, axis=0).astype(I32)
    carry_ref[...] = carry_ref[...] + jnp.sum(chosen_f, axis=1, keepdims=True)
    cnt_ref[...] = carry_ref[...].astype(I32)


def _router(u, router, bias):
    T = u.shape[0]
    D = router.shape[0]
    tm = ROW_TILE
    tri = jnp.asarray(np.triu(np.ones((tm, tm), np.float32), 1), BF16)
    tok_spec = pl.BlockSpec((TOP_K, tm), lambda i: (0, i))
    return pl.pallas_call(
        _router_kernel,
        grid=(T // tm,),
        in_specs=[pl.BlockSpec((tm, D // 2), lambda i: (i, 0)),
                  pl.BlockSpec((N_EXPERTS, D), lambda i: (0, 0)),
                  pl.BlockSpec((N_EXPERTS, 1), lambda i: (0, 0)),
                  pl.BlockSpec((tm, tm), lambda i: (0, 0))],
        out_specs=[tok_spec, tok_spec, tok_spec, pl.BlockSpec((N_EXPERTS, 1), lambda i: (0, 0))],
        out_shape=[jax.ShapeDtypeStruct((TOP_K, T), I32), jax.ShapeDtypeStruct((TOP_K, T), F32),
                   jax.ShapeDtypeStruct((TOP_K, T), I32), jax.ShapeDtypeStruct((N_EXPERTS, 1), I32)],
        scratch_shapes=[pltpu.VMEM((N_EXPERTS, 1), F32)],
        compiler_params=_cparams(("arbitrary",)),
    )(u, router.T.astype(BF16), bias.reshape(N_EXPERTS, 1).astype(F32), tri)


def _sc_mesh():
    return plsc.VectorSubcoreMesh(core_axis_name="c", subcore_axis_name="s")


def _sc_scatter_rows(x, dest, n_rows):
    T, W = x.shape
    K = dest.shape[0]
    win = SC_WINDOW
    n_win = T // win

    @functools.partial(pl.kernel, out_type=jax.ShapeDtypeStruct((n_rows, W), x.dtype), mesh=_sc_mesh(),
                       scratch_types=[])
    def scatter(x_hbm, i_hbm, o_hbm):
        def body(x_vmem, i_vmem):
            pltpu.sync_copy(x_vmem, o_hbm.at[i_vmem.at[0]])

        pltpu.emit_pipeline(
            body,
            grid=(K * n_win,),
            in_specs=[pl.BlockSpec((win, W), lambda j: (j % n_win, 0), pipeline_mode=pl.Buffered(1)),
                      pl.BlockSpec((1, win), lambda j: (0, j))],
            out_specs=[],
            core_axis_name=("c", "s"),
            dimension_semantics=(pltpu.PARALLEL,),
        )(x_hbm, i_hbm)

    return scatter(x, dest.reshape(1, K * T))


def _sc_gather_rows(y, dest):
    K, T = dest.shape
    W = y.shape[1]
    win = SC_WINDOW

    @functools.partial(pl.kernel, out_type=jax.ShapeDtypeStruct((K * T, W), y.dtype), mesh=_sc_mesh(),
                       scratch_types=[])
    def gather(y_hbm, i_hbm, o_hbm):
        def body(i_vmem, o_vmem):
            pltpu.sync_copy(y_hbm.at[i_vmem.at[0]], o_vmem)

        pltpu.emit_pipeline(
            body,
            grid=(K * T // win,),
            in_specs=[pl.BlockSpec((1, win), lambda j: (0, j))],
            out_specs=[pl.BlockSpec((win, W), lambda j: (j, 0), pipeline_mode=pl.Buffered(1))],
            core_axis_name=("c", "s"),
            dimension_semantics=(pltpu.PARALLEL,),
        )(i_hbm, o_hbm)

    return gather(y, dest.reshape(1, K * T)).reshape(K, T, W)


def _expert_kernel(be_ref, nu_ref, x_ref, wi_ref, wo_ref, y_ref, wi_b, wo_b):
    i = pl.program_id(0)

    @pl.when(i < nu_ref[0])
    def _():
        @pl.when(jnp.logical_or(i == 0, be_ref[i] != be_ref[jnp.maximum(i - 1, 0)]))
        def _():
            wi_b[...] = wi_ref[0, 0].astype(BF16)
            wo_b[...] = wo_ref[0, 0].astype(BF16)

        ff = wo_b.shape[0]
        hcat = _dot_halves(x_ref[...], wi_b)
        act = (_silu(hcat[:, :ff]) * hcat[:, ff:]).astype(BF16)
        y_ref[...] = _pack_halves(_dot(act, wo_b[...]))

    @pl.when(i >= nu_ref[0])
    def _():
        y_ref[...] = jnp.zeros_like(y_ref)


def _experts(xs, block_exp, n_used, w_in, w_out, layer):
    P, half = xs.shape
    D, ff2 = w_in.shape[-2:]
    nblk = P // MOE_BLOCK
    row_spec = pl.BlockSpec((MOE_BLOCK, half), lambda i, be, nu: (i, 0))
    return pl.pallas_call(
        _expert_kernel,
        grid_spec=pltpu.PrefetchScalarGridSpec(
            num_scalar_prefetch=2,
            grid=(nblk,),
            in_specs=[row_spec,
                      pl.BlockSpec((1, 1, D, ff2), lambda i, be, nu: (layer, be[i], 0, 0)),
                      pl.BlockSpec((1, 1, ff2 // 2, D), lambda i, be, nu: (layer, be[i], 0, 0))],
            out_specs=row_spec,
            scratch_shapes=[pltpu.VMEM((D, ff2), BF16), pltpu.VMEM((ff2 // 2, D), BF16)]),
        out_shape=jax.ShapeDtypeStruct((P, half), I32),
        compiler_params=_cparams(("arbitrary",)),
    )(block_exp, n_used, xs, w_in, w_out)


def _combine_kernel(yg_ref, gw_ref, u_ref, wsi_ref, wso_ref, h_ref, mod_ref, g_ref, b_ref, o_ref):
    ff = wso_ref.shape[0]
    hcat = _dot_halves(u_ref[...], wsi_ref)
    shared = _dot((_silu(hcat[:, :ff]) * hcat[:, ff:]).astype(BF16), wso_ref[...])
    lo, hi = None, None
    for k in range(TOP_K):
        y_lo, y_hi = _unpack_halves(yg_ref[k])
        gk = gw_ref[:, k:k + 1]
        lo = y_lo * gk if lo is None else lo + y_lo * gk
        hi = y_hi * gk if hi is None else hi + y_hi * gk
    routed = jnp.concatenate([lo, hi], axis=1)
    z = DEEPNORM_ALPHA * h_ref[...] + mod_ref[0] * (routed + shared)
    o_ref[...] = _layer_norm(z, g_ref[...], b_ref[...])


def _combine(yg, gw_t, u, ws_in, ws_out, h, gate, gate_index, ln_g, ln_b):
    T, D = h.shape
    tm = ROW_TILE
    vec_spec = pl.BlockSpec((1, D), lambda i: (0, 0))
    row_spec = pl.BlockSpec((tm, D), lambda i: (i, 0))
    packed_spec = pl.BlockSpec((tm, D // 2), lambda i: (i, 0))
    return pl.pallas_call(
        _combine_kernel,
        grid=(T // tm,),
        in_specs=[pl.BlockSpec((TOP_K, tm, D // 2), lambda i: (0, i, 0)),
                  pl.BlockSpec((tm, TOP_K), lambda i: (i, 0)),
                  packed_spec,
                  pl.BlockSpec(ws_in.shape, lambda i: (0, 0)),
                  pl.BlockSpec(ws_out.shape, lambda i: (0, 0)),
                  row_spec,
                  pl.BlockSpec((1,) + gate.shape[1:], lambda i: (gate_index(i), 0, 0)),
                  vec_spec, vec_spec],
        out_specs=row_spec,
        out_shape=jax.ShapeDtypeStruct((T, D), F32),
        compiler_params=_cparams(("arbitrary",)),
    )(yg, gw_t, u, ws_in.astype(BF16), ws_out.astype(BF16), h, gate,
      ln_g.reshape(1, D), ln_b.reshape(1, D))


def _slots_kernel(e_ref, rank_ref, start_ref, dest_ref):
    io_e = lax.broadcasted_iota(I32, (N_EXPERTS, e_ref.shape[1]), 0)
    rows = [jnp.sum(jnp.where(io_e == e_ref[k:k + 1, :], start_ref[...], 0), axis=0, keepdims=True)
            for k in range(TOP_K)]
    dest_ref[...] = jnp.concatenate(rows, axis=0) + rank_ref[...]


def _slots(eidx, rank, pstart):
    T = eidx.shape[1]
    tm = ROW_TILE
    tok_spec = pl.BlockSpec((TOP_K, tm), lambda i: (0, i))
    return pl.pallas_call(
        _slots_kernel,
        grid=(T // tm,),
        in_specs=[tok_spec, tok_spec, pl.BlockSpec((N_EXPERTS, 1), lambda i: (0, 0))],
        out_specs=tok_spec,
        out_shape=jax.ShapeDtypeStruct((TOP_K, T), I32),
        compiler_params=_cparams(("arbitrary",)),
    )(eidx, rank, pstart.reshape(N_EXPERTS, 1))


def _moe_layer(u, h, gate, gate_index, router, bias, w_in, w_out, ws_in, ws_out, ln_g, ln_b, layer):
    T = u.shape[0]
    eidx, gw, rank, counts = _router(u, router, bias)
    counts = counts[:, 0]
    padded = (counts + MOE_BLOCK - 1) // MOE_BLOCK * MOE_BLOCK
    pend = jnp.cumsum(padded)
    pstart = (pend - padded).astype(I32)
    dest = _slots(eidx, rank, pstart)
    n_blocks = -(-(T * TOP_K + N_EXPERTS * (MOE_BLOCK - 1)) // MOE_BLOCK)
    block_exp = jnp.minimum(jnp.searchsorted(pend, jnp.arange(n_blocks, dtype=I32) * MOE_BLOCK, side='right'),
                            N_EXPERTS - 1).astype(I32)
    n_used = (pend[-1:] // MOE_BLOCK).astype(I32)
    xs = _sc_scatter_rows(u, dest, n_blocks * MOE_BLOCK)
    y = _experts(xs, block_exp, n_used, w_in, w_out, layer)
    return _combine(_sc_gather_rows(y, dest), gw.T, u, ws_in, ws_out, h, gate, gate_index, ln_g, ln_b)


def _seg_ones(width=LANES):
    idx = np.arange(width) // HEAD_DIM
    return jnp.asarray((idx[:, None] == idx[None, :]).astype(np.float32), BF16)


def _head_sum(x, ones_ref):
    outs = []
    for j in range(x.shape[1] // LANES):
        xc = x[:, j * LANES:(j + 1) * LANES]
        hi = xc.astype(BF16)
        lo = (xc - hi.astype(F32)).astype(BF16)
        outs.append(_dot(hi, ones_ref[...]) + _dot(lo, ones_ref[...]))
    return jnp.concatenate(outs, axis=1)


def _rwkv_proj_kernel(u_ref, dx_ref, mu_ref, wrkv_ref, g1_ref, g2_ref, d1_ref, d2_ref, d0_ref,
                      i1_ref, i2_ref, i0_ref, kk_ref, ka_ref, rk_ref, ones_ref,
                      r_ref, v_ref, a_ref, g_ref, bonus_ref, w_ref, k_ref, b_ref):
    u = u_ref[...]
    dx = dx_ref[...]
    mix = lambda m: (u + dx * mu_ref[m:m + 1, :])
    xr, xw, xk, xv, xa, xg = [mix(m) for m in range(6)]
    r = _dot(xr.astype(BF16), wrkv_ref[0])
    k = _dot(xk.astype(BF16), wrkv_ref[1])
    v = _dot(xv.astype(BF16), wrkv_ref[2])
    g = _dot(jax.nn.sigmoid(_dot(xg.astype(BF16), g1_ref[...])).astype(BF16), g2_ref[...])
    kk = k * kk_ref[...]
    kk = kk * lax.rsqrt(jnp.maximum(_head_sum(kk * kk, ones_ref), 1e-24))
    r_ref[...] = r
    v_ref[...] = v
    a_ref[...] = -kk
    g_ref[...] = g
    k_sum = None
    xw_b = xw.astype(BF16)
    xa_b = xa.astype(BF16)
    for d in range(2):
        lw = d0_ref[d:d + 1, :] + _dot(jnp.tanh(_dot(xw_b, d1_ref[d])).astype(BF16), d2_ref[d])
        softplus = jnp.maximum(-lw, 0.0) + jnp.log(1.0 + jnp.exp(-jnp.abs(lw)))
        logw = -softplus - 0.5
        w_ref[d] = jnp.exp(-jnp.exp(logw))
        eta = jax.nn.sigmoid(i0_ref[d:d + 1, :] + _dot(_dot(xa_b, i1_ref[d]).astype(BF16), i2_ref[d]))
        k_d = k * (1.0 + (eta - 1.0) * ka_ref[...])
        k_ref[d] = k_d
        b_ref[d] = kk * eta
        k_sum = k_d if k_sum is None else k_sum + k_d
    bonus_ref[...] = _head_sum(r * k_sum * rk_ref[...], ones_ref) * v


def _rwkv_proj(u, dx, p):
    T, D = u.shape
    tm = PROJ_TILE
    row = pl.BlockSpec((tm, D), lambda i: (i, 0))
    row2 = pl.BlockSpec((2, tm, D), lambda i: (0, i, 0))
    full = lambda a: pl.BlockSpec(a.shape, lambda i: (0,) * a.ndim)
    bf = lambda a: a.astype(BF16)
    consts = [p['mu'], bf(p['w_rkv']), bf(p['gate1']), bf(p['gate2']), bf(p['dec1']), bf(p['dec2']), p['dec0'],
              bf(p['icl1']), bf(p['icl2']), p['icl0'], p['k_k'].reshape(1, D), p['k_a'].reshape(1, D),
              p['r_k'].reshape(1, D), _seg_ones()]
    one = jax.ShapeDtypeStruct((T, D), F32)
    two = jax.ShapeDtypeStruct((2, T, D), F32)
    return pl.pallas_call(
        _rwkv_proj_kernel,
        grid=(T // tm,),
        in_specs=[row, row] + [full(a) for a in consts],
        out_specs=[row, row, row, row, row, row2, row2, row2],
        out_shape=[one, one, one, one, one, two, two, two],
        compiler_params=_cparams(("arbitrary",)),
    )(u, dx, *consts)


def _scan_kernel(r_ref, w_ref, k_ref, v_ref, a_ref, b_ref, ones_ref, hsel_ref,
                 y_ref, s_ref, vt_ref):
    d = pl.program_id(0)
    c = pl.program_id(2)
    tc, nb = r_ref.shape[0], r_ref.shape[1]
    tw = SCAN_TILE
    n_wide = r_ref.shape[2] // tw
    heads = tw // HEAD_DIM
    assert heads * tc == tw

    @pl.when(c == 0)
    def _():
        s_ref[...] = jnp.zeros_like(s_ref)

    for bb in range(nb):
        for q in range(n_wide):
            vt = v_ref[:, bb, q * tw:(q + 1) * tw].T
            vt_ref[bb * n_wide + q] = jnp.concatenate(
                [vt[h * HEAD_DIM:(h + 1) * HEAD_DIM] for h in range(heads)], axis=1)

    head_base = (lax.broadcasted_iota(I32, (HEAD_DIM, LANES), 1) // HEAD_DIM) * tc
    tiles = [(bb, q) for bb in range(nb) for q in range(n_wide)]
    groups = [tiles[i:i + SCAN_GROUP] for i in range(0, len(tiles), SCAN_GROUP)]

    def stacked(grp, get, dtype=F32):
        def wide(bb, q):
            return jnp.concatenate(
                [jnp.broadcast_to(get(bb, slice(q * tw + hf * LANES, q * tw + (hf + 1) * LANES)).astype(dtype),
                                  (HEAD_DIM, LANES)) for hf in range(tw // LANES)], axis=1)
        return jnp.concatenate([wide(bb, q) for bb, q in grp], axis=0)

    def load_state(grp):
        return jnp.concatenate([s_ref[bb * n_wide + q] for bb, q in grp], axis=0)

    def emit_y(grp, st_b, t_y):
        r_rows = stacked(grp, lambda bb, cols: r_ref[t_y, bb:bb + 1, cols], BF16)
        yh = _dot_nt(hsel_ref[...], st_b * r_rows)
        first = tiles.index(grp[0])
        y_ref[0, t_y, :, first * HEAD_DIM:(first + len(grp)) * HEAD_DIM] = yh[:heads]

    def step(s_i, carry):
        t = jnp.where(d == 0, s_i, tc - 1 - s_i)
        t_prev = jnp.where(s_i == 0, t, jnp.where(d == 0, t - 1, t + 1))
        pick = head_base + t
        for grp in groups:
            one = lambda ref: stacked(grp, lambda bb, cols: ref[t, bb:bb + 1, cols])
            two = lambda ref: stacked(grp, lambda bb, cols: ref[0, t, bb:bb + 1, cols])
            st = load_state(grp)
            st_b = st.astype(BF16)
            a_rows = stacked(grp, lambda bb, cols: a_ref[t, bb:bb + 1, cols], BF16)
            sa = _dot(st_b * a_rows, ones_ref[...])
            emit_y(grp, st_b, t_prev)
            vcol = jnp.concatenate(
                [jnp.concatenate([jnp.take_along_axis(vt_ref[bb * n_wide + q, :, hf * LANES:(hf + 1) * LANES],
                                                      pick, axis=1) for hf in range(tw // LANES)], axis=1)
                 for bb, q in grp], axis=0)
            st = st * two(w_ref) + sa * two(b_ref) + vcol * two(k_ref)
            for j, (bb, q) in enumerate(grp):
                s_ref[bb * n_wide + q] = st[j * HEAD_DIM:(j + 1) * HEAD_DIM]
        return carry

    lax.fori_loop(0, tc, step, 0, unroll=SCAN_UNROLL)
    t_last = jnp.where(d == 0, tc - 1, 0)
    for grp in groups:
        emit_y(grp, load_state(grp).astype(BF16), t_last)


def _wkv_scan(r, w, k, v, a, b, n_ctx):
    N, B, D = r.shape
    tc = SCAN_CHUNK
    wc = SCAN_COLS
    n_wide = wc // SCAN_TILE
    nc = N // tc
    ncc = n_ctx // tc

    def chunk(d, c):
        rev = jnp.where(c < ncc, ncc - 1 - c, nc - 1 - (c - ncc))
        return jnp.where(d == 0, c, rev)

    one = pl.BlockSpec((tc, B, wc), lambda d, g, c: (chunk(d, c), 0, g))
    two = pl.BlockSpec((1, tc, B, wc), lambda d, g, c: (d, chunk(d, c), 0, g))
    seg = np.arange(SCAN_TILE) // HEAD_DIM
    hsel = np.zeros((8, SCAN_TILE), np.float32)
    for hh in range(SCAN_TILE // HEAD_DIM):
        hsel[hh, seg == hh] = 1.0
    const = lambda a_: pl.BlockSpec(a_.shape, lambda d, g, c: (0, 0))
    consts = [_seg_ones(SCAN_TILE), jnp.asarray(hsel, BF16)]
    heads = SCAN_TILE // HEAD_DIM
    ncg = D // wc
    y = pl.pallas_call(
        _scan_kernel,
        grid=(2, ncg, nc),
        in_specs=[one, two, two, one, one, two] + [const(a_) for a_ in consts],
        out_specs=pl.BlockSpec((1, tc, heads, B * n_wide * HEAD_DIM), lambda d, g, c: (d, chunk(d, c), 0, g)),
        out_shape=jax.ShapeDtypeStruct((2, N, heads, ncg * B * n_wide * HEAD_DIM), F32),
        scratch_shapes=[pltpu.VMEM((B * n_wide, HEAD_DIM, SCAN_TILE), F32),
                        pltpu.VMEM((B * n_wide, HEAD_DIM, SCAN_TILE), F32)],
        compiler_params=_cparams(("arbitrary", "arbitrary", "arbitrary")),
    )(r, w, k, v, a, b, *consts)
    y = y.reshape(2, N, heads, ncg, B, n_wide * HEAD_DIM)
    return jnp.transpose(y, (0, 1, 4, 2, 3, 5)).reshape(2, N, B, D)


def _scan_head_order(d):
    heads = SCAN_TILE // HEAD_DIM
    n_wide = SCAN_COLS // SCAN_TILE
    ncg = d // SCAN_COLS
    order = []
    for g in range(ncg):
        for q in range(n_wide):
            for h in range(heads):
                order.append((h * ncg + g) * n_wide + q)
    return tuple(order)


def _rwkv_out_kernel(head_order, y0_ref, y1_ref, bonus_ref, g_ref, lnx_ref, ones_ref, w_ref, h_ref, mod_ref,
                     lg_ref, lb_ref, hn_ref, u_ref):
    y_in = y0_ref[0] + y1_ref[0]
    y = jnp.concatenate([y_in[:, p * HEAD_DIM:(p + 1) * HEAD_DIM] for p in head_order], axis=1)
    ym = _head_sum(y, ones_ref) * (1.0 / HEAD_DIM)
    yc = y - ym
    yv = _head_sum(yc * yc, ones_ref) * (1.0 / HEAD_DIM)
    yn = yc * lax.rsqrt(yv + LNX_EPS) * lnx_ref[0:1, :] + lnx_ref[1:2, :]
    x = ((yn + bonus_ref[...]) * g_ref[...]).astype(BF16)
    o = _dot(x, w_ref[...])
    z = DEEPNORM_ALPHA * h_ref[...] + mod_ref[0] * o
    hn = _layer_norm(z, lg_ref[...], lb_ref[...])
    hn_ref[...] = hn
    u_ref[...] = _pack_halves(hn * (1.0 + mod_ref[2]) + mod_ref[1])


def _rwkv_out(y, bonus, g, lnx, w_out, h, mod_rows, ln_g, ln_b, row0):
    T, D = h.shape
    tm = ROW_TILE
    t0 = row0 // tm
    off = pl.BlockSpec((tm, D), lambda i: (i + t0, 0))
    out = pl.BlockSpec((tm, D), lambda i: (i, 0))
    full = lambda a: pl.BlockSpec(a.shape, lambda i: (0,) * a.ndim)
    vec = pl.BlockSpec((1, D), lambda i: (0, 0))
    ones = _seg_ones()
    w_b = w_out.astype(BF16)
    return pl.pallas_call(
        functools.partial(_rwkv_out_kernel, _scan_head_order(D)),
        grid=((T - row0) // tm,),
        in_specs=[pl.BlockSpec((1, tm, D), lambda i: (0, i + t0, 0)),
                  pl.BlockSpec((1, tm, D), lambda i: (1, i + t0, 0)),
                  off, off, full(lnx), full(ones), full(w_b), off, full(mod_rows), vec, vec],
        out_specs=[out, pl.BlockSpec((tm, D // 2), lambda i: (i, 0))],
        out_shape=[jax.ShapeDtypeStruct((T - row0, D), F32), jax.ShapeDtypeStruct((T - row0, D // 2), I32)],
        compiler_params=_cparams(("arbitrary",)),
    )(y, y, bonus, g, lnx, ones, w_b, h, mod_rows, ln_g.reshape(1, D), ln_b.reshape(1, D))


def _shift_delta(u, n_ctx):
    def seg(x):
        xp = jnp.pad(x, ((1, 1), (0, 0), (0, 0)))
        return 0.5 * (xp[:-2] + xp[2:]) - x
    return jnp.concatenate([seg(u[:n_ctx]), seg(u[n_ctx:])], axis=0)


def kernel(x, c, ctx, c_ctx, ada_w, ada_b, post_ln_g, post_ln_b, att_w_in, att_w_out, att_sink, diff_lambda_vecs, diff_subln_g, rk_mu, rk_w_rkv, rk_w_out, rk_decay0, rk_decay1, rk_decay2, rk_iclr0, rk_iclr1, rk_iclr2, rk_gate1, rk_gate2, rk_k_k, rk_k_a, rk_r_k, rk_lnx, moe_router, moe_bias, moe_w_in, moe_w_out, moe_ws_in, moe_ws_out):
    B, S, D = x.shape
    L = ctx.shape[1]
    N = L + S
    tm = ROW_TILE
    assert L % tm == 0 and S % tm == 0 and L % SCAN_CHUNK == 0 and S % SCAN_CHUNK == 0
    assert tm % B == 0 and D % SCAN_COLS == 0

    rows = -(-(B + 1) // 8) * 8
    cvec = jnp.concatenate([c, c_ctx[None, :], jnp.zeros((rows - B - 1, D), F32)], axis=0)
    mods = [_mod_table(_ada_mod(cvec, ada_w[i], ada_b[i]), B, D) for i in range(DEPTH)]

    h0 = jnp.concatenate([ctx, x], axis=1)
    lam_init = 0.8 - 0.6 * math.exp(-0.3 * 0)
    qa, ka, va, qb, kb, vb = _attn_inproj(h0, mods[0], att_w_in[0], L)
    oa = _win_attn(qa, ka, va, att_sink[0], L)
    ob = _diff_attn(qb, kb, vb, diff_lambda_vecs[0], diff_subln_g[0], lam_init, L)
    h1, u1 = _mix_out([oa, ob], [att_w_out[0][:A_WIDTH], att_w_out[0][A_WIDTH:]], h0, mods[0],
                      post_ln_g[0, 0], post_ln_b[0, 0], L, 0)
    tiles_b, tiles_c = N // tm, L // tm
    gate0 = mods[0][:, :, 5].reshape(B * 2, 1, D)
    gate0_index = lambda i: (i // tiles_b) * 2 + jnp.minimum((i % tiles_b) // tiles_c, 1)
    h2 = _moe_layer(u1.reshape(B * N, D // 2), h1.reshape(B * N, D), gate0, gate0_index, moe_router[0], moe_bias[0],
                    moe_w_in, moe_w_out, moe_ws_in[0], moe_ws_out[0],
                    post_ln_g[0, 1], post_ln_b[0, 1], 0).reshape(B, N, D)

    m_ctx, m_lat = mods[1][:, 0], mods[1][:, 1]
    h2_t = jnp.swapaxes(h2, 0, 1)
    is_lat = (jnp.arange(N) >= L)[:, None, None]
    u = h2_t * (1.0 + jnp.where(is_lat, m_lat[:, 1], m_ctx[:, 1])) + jnp.where(is_lat, m_lat[:, 0], m_ctx[:, 0])
    dx = _shift_delta(u, L)
    params = dict(mu=rk_mu[0], w_rkv=rk_w_rkv[0], gate1=rk_gate1[0], gate2=rk_gate2[0],
                  dec0=rk_decay0[0], dec1=rk_decay1[0], dec2=rk_decay2[0],
                  icl0=rk_iclr0[0], icl1=rk_iclr1[0], icl2=rk_iclr2[0],
                  k_k=rk_k_k[0], k_a=rk_k_a[0], r_k=rk_r_k[0])
    r, v, a, g, bonus, w2, k2, b2 = _rwkv_proj(u.reshape(N * B, D), dx.reshape(N * B, D), params)
    tmaj = lambda t: t.reshape(t.shape[:-2] + (N, B, D))
    y = _wkv_scan(tmaj(r), tmaj(w2), tmaj(k2), tmaj(v), tmaj(a), tmaj(b2), L)
    lat_rows = lambda j: jnp.tile(m_lat[:, j], (tm // B, 1))
    h3, u3 = _rwkv_out(y.reshape(2, N * B, D), bonus, g, rk_lnx[0], rk_w_out[0], h2_t.reshape(N * B, D),
                       jnp.stack([lat_rows(2), lat_rows(3), lat_rows(4)]),
                       post_ln_g[1, 0], post_ln_b[1, 0], L * B)
    out = _moe_layer(u3, h3, lat_rows(5)[None], lambda i: 0, moe_router[1], moe_bias[1],
                     moe_w_in, moe_w_out, moe_ws_in[1], moe_ws_out[1],
                     post_ln_g[1, 1], post_ln_b[1, 1], 1)
    return jnp.swapaxes(out.reshape(S, B, D), 0, 1)
```

```python
import functools
import math

import numpy as np
import jax
import jax.numpy as jnp
from jax import lax
from jax.experimental import pallas as pl
from jax.experimental.pallas import tpu as pltpu
from jax.experimental.pallas import tpu_sc as plsc

F32 = jnp.float32
BF16 = jnp.bfloat16
I32 = jnp.int32

HEAD_DIM = 64
GRID_W = 64
ROPE_AXIS_DIM = HEAD_DIM // 2
ROPE_THETA = 10000.0
Q_BLOCK = 128
A_Q_HEADS = 8
A_KV_HEADS = 2
A_GROUP = A_Q_HEADS // A_KV_HEADS
A_WIDTH = A_Q_HEADS * HEAD_DIM
A_KV_WIDTH = A_KV_HEADS * HEAD_DIM
B_HEADS = 4
B_V_DIM = 2 * HEAD_DIM
B_WIDTH = B_HEADS * B_V_DIM
LNX_EPS = 64e-5
N_EXPERTS = 256
TOP_K = 8
N_GROUPS = 8
TOPK_GROUPS = 4
ROUTED_SCALE = 2.5
MOE_BLOCK = 256
LN_EPS = 1e-5
SUBLN_EPS = 1e-5
NEG_INF = -1e30
DEPTH = 2
DEEPNORM_ALPHA = (2 * DEPTH) ** 0.25

LANES = 128
ROW_TILE = 256
PROJ_TILE = 128
SC_WINDOW = 128
SCAN_CHUNK = 64
SCAN_COLS = 1024
SCAN_TILE = 256
SCAN_GROUP = 8
SCAN_UNROLL = 4
VMEM_LIMIT = 56 * 1024 * 1024


def _cparams(sem):
    return pltpu.CompilerParams(dimension_semantics=sem, vmem_limit_bytes=VMEM_LIMIT)


def _silu(x):
    return x * jax.nn.sigmoid(x)


def _layer_norm(z, g, b):
    mu = jnp.mean(z, -1, keepdims=True)
    zc = z - mu
    var = jnp.mean(zc * zc, -1, keepdims=True)
    return zc * lax.rsqrt(var + LN_EPS) * g + b


def _dot(a, b):
    return jnp.dot(a, b, preferred_element_type=F32)


def _dot_nt(a, b):
    return lax.dot_general(a, b, (((1,), (1,)), ((), ())), preferred_element_type=F32)


def _pack_halves(x):
    half = x.shape[1] // 2
    bits = lambda v: lax.bitcast_convert_type(v.astype(BF16).astype(F32), I32)
    return lax.shift_right_logical(bits(x[:, :half]), 16) | bits(x[:, half:])


def _unpack_halves(p):
    lo = lax.bitcast_convert_type(lax.shift_left(p, 16), F32)
    hi = lax.bitcast_convert_type(p & jnp.int32(-65536), F32)
    return lo, hi


def _dot_halves(p, w_ref_or_array):
    lo, hi = _unpack_halves(p)
    half = p.shape[1]
    return _dot(lo.astype(BF16), w_ref_or_array[:half]) + _dot(hi.astype(BF16), w_ref_or_array[half:])


def _ada_kernel(c_ref, w_ref, b_ref, o_ref):
    c = c_ref[...]
    o_ref[...] = _dot(_silu(c).astype(BF16), w_ref[...].astype(BF16)) + b_ref[...]


def _ada_mod(cvec, w, bias):
    R, D = cvec.shape
    n_out = w.shape[1]
    tn = 768
    return pl.pallas_call(
        _ada_kernel,
        grid=(n_out // tn,),
        in_specs=[pl.BlockSpec((R, D), lambda j: (0, 0)),
                  pl.BlockSpec((D, tn), lambda j: (0, j)),
                  pl.BlockSpec((1, tn), lambda j: (0, j))],
        out_specs=pl.BlockSpec((R, tn), lambda j: (0, j)),
        out_shape=jax.ShapeDtypeStruct((R, n_out), F32),
        compiler_params=_cparams(("arbitrary",)),
    )(cvec, w, bias.reshape(1, n_out))


def _mod_table(m, batch, d):
    m_lat = m[:batch].reshape(batch, 6, d)
    m_ctx = jnp.broadcast_to(m[batch].reshape(1, 6, d), (batch, 6, d))
    return jnp.stack([m_ctx, m_lat], axis=1)


def _mod_spec(d, ctx_tiles):
    return pl.BlockSpec((1, 1, 6, d), lambda b, i: (b, jnp.minimum(i // ctx_tiles, 1), 0, 0))


def _rope_tables(n_ctx, n_lat):
    rows = n_lat // GRID_W
    row = np.repeat(np.arange(rows), GRID_W).astype(np.float32)
    col = np.tile(np.arange(GRID_W), rows).astype(np.float32)
    inv = (ROPE_THETA ** (-np.arange(0, ROPE_AXIS_DIM, 2, dtype=np.float32) / ROPE_AXIS_DIM)).astype(np.float32)
    ar = row[:, None] * inv
    ac = col[:, None] * inv
    ang = np.concatenate([ar, ar, ac, ac], -1)
    cos = np.cos(ang).astype(np.float32)
    sin = np.sin(ang).astype(np.float32)
    lower = (np.arange(HEAD_DIM) % ROPE_AXIS_DIM) < (ROPE_AXIS_DIM // 2)
    sin_up = np.where(lower[None, :], -sin, 0.0)
    sin_dn = np.where(lower[None, :], 0.0, sin)

    def full(t, ctx_fill):
        t = np.concatenate([np.full((n_ctx, HEAD_DIM), ctx_fill, np.float32), t], 0)
        return jnp.asarray(np.tile(t, (1, LANES // HEAD_DIM)))

    return full(cos, 1.0), full(sin_up, 0.0), full(sin_dn, 0.0)


def _inproj_kernel(h_ref, mod_ref, w_ref, cos_ref, su_ref, sd_ref,
                   qa_ref, ka_ref, va_ref, qb_ref, kb_ref, vb_ref):
    h = h_ref[0]
    shift = mod_ref[0, 0, 0:1, :]
    scale = mod_ref[0, 0, 1:2, :]
    u = (h * (1.0 + scale) + shift).astype(BF16)
    y = _dot(u, w_ref[...])
    cos, s_up, s_dn = cos_ref[...], su_ref[...], sd_ref[...]
    q_scale = HEAD_DIM ** -0.5

    def rope(xc):
        half = ROPE_AXIS_DIM // 2
        return xc * cos + pltpu.roll(xc, LANES - half, 1) * s_up + pltpu.roll(xc, half, 1) * s_dn

    def emit(out_ref, col0, width, roped, mul):
        for j in range(width // LANES):
            xc = y[:, col0 + j * LANES: col0 + (j + 1) * LANES]
            if roped:
                xc = rope(xc)
            if mul != 1.0:
                xc = xc * mul
            out_ref[0, :, j * LANES:(j + 1) * LANES] = xc.astype(out_ref.dtype)

    c = 0
    emit(qa_ref, c, A_WIDTH, True, q_scale); c += A_WIDTH
    emit(ka_ref, c, A_KV_WIDTH, True, 1.0); c += A_KV_WIDTH
    emit(va_ref, c, A_KV_WIDTH, False, 1.0); c += A_KV_WIDTH
    emit(qb_ref, c, B_WIDTH, True, q_scale); c += B_WIDTH
    emit(kb_ref, c, B_WIDTH, True, 1.0); c += B_WIDTH
    emit(vb_ref, c, B_WIDTH, False, 1.0)


def _attn_inproj(h, mod, w_in, n_ctx):
    B, N, D = h.shape
    tm = ROW_TILE
    cos, s_up, s_dn = _rope_tables(n_ctx, N - n_ctx)
    widths = (A_WIDTH, A_KV_WIDTH, A_KV_WIDTH, B_WIDTH, B_WIDTH, B_WIDTH)
    tab_spec = pl.BlockSpec((tm, LANES), lambda b, i: (i, 0))
    return pl.pallas_call(
        _inproj_kernel,
        grid=(B, N // tm),
        in_specs=[pl.BlockSpec((1, tm, D), lambda b, i: (b, i, 0)),
                  _mod_spec(D, n_ctx // tm),
                  pl.BlockSpec(w_in.shape, lambda b, i: (0, 0)),
                  tab_spec, tab_spec, tab_spec],
        out_specs=[pl.BlockSpec((1, tm, w), lambda b, i: (b, i, 0)) for w in widths],
        out_shape=[jax.ShapeDtypeStruct((B, N, w), BF16) for w in widths],
        compiler_params=_cparams(("arbitrary", "arbitrary")),
    )(h, mod, w_in.astype(BF16), cos, s_up, s_dn)


def _win_attn_kernel(n_ctx_blocks, n_blocks, q_ref, kc_ref, vc_ref, kl_ref, km_ref, kr_ref,
                     vl_ref, vm_ref, vr_ref, sink_ref, o_ref):
    j = pl.program_id(1)
    is_lat = j >= n_ctx_blocks
    qb = Q_BLOCK
    n_c = kc_ref.shape[1]
    rows = A_GROUP * qb
    n_keys = n_c + 3 * qb
    far = 1 << 20
    r_idx = lax.broadcasted_iota(I32, (rows, n_keys), 0) % qb
    cw = lax.broadcasted_iota(I32, (rows, n_keys), 1) - n_c
    off_l = jnp.where(jnp.logical_and(is_lat, j > n_ctx_blocks), 0, far)
    end_m = jnp.where(is_lat, 2 * qb, qb)
    off_r = jnp.where(jnp.logical_and(is_lat, j < n_blocks - 1), 0, far)
    valid = ((cw < 0)
             | ((cw >= 0) & (cw < qb) & (cw >= r_idx + off_l))
             | ((cw >= qb) & (cw < end_m))
             | ((cw >= 2 * qb) & (cw - 2 * qb + off_r <= r_idx)))
    outs = []
    for kv in range(A_KV_HEADS):
        cols = slice(kv * HEAD_DIM, (kv + 1) * HEAD_DIM)
        k_all = jnp.concatenate([kc_ref[0, :, cols], kl_ref[0, :, cols], km_ref[0, :, cols],
                                 kr_ref[0, :, cols]], axis=0)
        v_all = jnp.concatenate([vc_ref[0, :, cols], vl_ref[0, :, cols], vm_ref[0, :, cols],
                                 vr_ref[0, :, cols]], axis=0)
        q0 = kv * A_GROUP
        q = jnp.concatenate([q_ref[0, :, (q0 + g) * HEAD_DIM:(q0 + g + 1) * HEAD_DIM]
                             for g in range(A_GROUP)], axis=0)
        sink = jnp.concatenate([jnp.broadcast_to(sink_ref[q0 + g:q0 + g + 1, 0:1], (qb, 1))
                                for g in range(A_GROUP)], axis=0)
        s = jnp.where(valid, _dot_nt(q, k_all), NEG_INF)
        m = jnp.maximum(jnp.max(s, -1, keepdims=True), sink)
        e = jnp.exp(s - m)
        denom = jnp.sum(e, -1, keepdims=True) + jnp.exp(sink - m)
        o = _dot(e.astype(BF16), v_all) * (1.0 / denom)
        outs += [o[g * qb:(g + 1) * qb] for g in range(A_GROUP)]
    for j2 in range(A_Q_HEADS // 2):
        pair = jnp.concatenate([outs[2 * j2], outs[2 * j2 + 1]], axis=1)
        o_ref[0, :, j2 * LANES:(j2 + 1) * LANES] = pair.astype(o_ref.dtype)


def _win_attn(qa, ka, va, sink, n_ctx):
    B, N, _ = qa.shape
    qb = Q_BLOCK
    nb = N // qb
    ncb = n_ctx // qb
    sink_pad = jnp.broadcast_to(sink.reshape(A_Q_HEADS, 1).astype(F32), (A_Q_HEADS, LANES))

    def left(b, j):
        return (b, jnp.clip(j - 1, ncb, nb - 1), 0)

    def mid(b, j):
        return (b, jnp.clip(j, ncb, nb - 1), 0)

    def right(b, j):
        return (b, jnp.clip(j + 1, ncb, nb - 1), 0)

    kv_blk = lambda im: pl.BlockSpec((1, qb, A_KV_WIDTH), im)
    ctx_blk = pl.BlockSpec((1, n_ctx, A_KV_WIDTH), lambda b, j: (b, 0, 0))
    return pl.pallas_call(
        functools.partial(_win_attn_kernel, ncb, nb),
        grid=(B, nb),
        in_specs=[pl.BlockSpec((1, qb, A_WIDTH), lambda b, j: (b, j, 0)),
                  ctx_blk, ctx_blk,
                  kv_blk(left), kv_blk(mid), kv_blk(right),
                  kv_blk(left), kv_blk(mid), kv_blk(right),
                  pl.BlockSpec((A_Q_HEADS, LANES), lambda b, j: (0, 0))],
        out_specs=pl.BlockSpec((1, qb, A_WIDTH), lambda b, j: (b, j, 0)),
        out_shape=jax.ShapeDtypeStruct((B, N, A_WIDTH), BF16),
        compiler_params=_cparams(("arbitrary", "arbitrary")),
    )(qa, ka, va, ka, ka, ka, va, va, va, sink_pad)


def _diff_attn_kernel(n_ctx, lam_init, q_ref, k_ref, v_ref, lv_ref, g_ref, o_ref):
    j = pl.program_id(1)
    lv = lv_ref[...]
    lam = (jnp.exp(jnp.sum(lv[0:1] * lv[1:2], -1, keepdims=True))
           - jnp.exp(jnp.sum(lv[2:3] * lv[3:4], -1, keepdims=True)) + lam_init)
    gain = g_ref[...] * (1.0 - lam_init)

    def run(n_keys):
        for hd in range(B_HEADS):
            parts = []
            for mm in range(2):
                c0 = (hd * 2 + mm) * HEAD_DIM
                q = q_ref[0, :, c0:c0 + HEAD_DIM]
                k = k_ref[0, :n_keys, c0:c0 + HEAD_DIM]
                s = _dot_nt(q, k)
                e = jnp.exp(s - jnp.max(s, -1, keepdims=True))
                parts.append((e, jnp.sum(e, -1, keepdims=True)))
            (e0, l0), (e1, l1) = parts
            v = v_ref[0, :n_keys, hd * B_V_DIM:(hd + 1) * B_V_DIM]
            o = _dot(e0.astype(BF16), v) * (1.0 / l0) - _dot(e1.astype(BF16), v) * (lam / l1)
            o = o * lax.rsqrt(jnp.mean(o * o, -1, keepdims=True) + SUBLN_EPS) * gain
            o_ref[0, :, hd * B_V_DIM:(hd + 1) * B_V_DIM] = o.astype(o_ref.dtype)

    @pl.when(j == 0)
    def _():
        run(n_ctx)

    @pl.when(j > 0)
    def _():
        run(k_ref.shape[1])


def _diff_attn(qb, kb, vb, lam_vecs, subln_g, lam_init, n_ctx):
    B, N, _ = qb.shape
    tq = n_ctx
    return pl.pallas_call(
        functools.partial(_diff_attn_kernel, n_ctx, lam_init),
        grid=(B, N // tq),
        in_specs=[pl.BlockSpec((1, tq, B_WIDTH), lambda b, j: (b, j, 0)),
                  pl.BlockSpec((1, N, B_WIDTH), lambda b, j: (b, 0, 0)),
                  pl.BlockSpec((1, N, B_WIDTH), lambda b, j: (b, 0, 0)),
                  pl.BlockSpec((4, HEAD_DIM), lambda b, j: (0, 0)),
                  pl.BlockSpec((1, B_V_DIM), lambda b, j: (0, 0))],
        out_specs=pl.BlockSpec((1, tq, B_WIDTH), lambda b, j: (b, j, 0)),
        out_shape=jax.ShapeDtypeStruct((B, N, B_WIDTH), BF16),
        compiler_params=_cparams(("arbitrary", "arbitrary")),
    )(qb, kb, vb, lam_vecs.astype(F32), subln_g.reshape(1, B_V_DIM).astype(F32))


def _mix_out_kernel(n_in, *refs):
    xs = refs[:n_in]
    ws = refs[n_in:2 * n_in]
    h_ref, mod_ref, g_ref, b_ref, hn_ref, u_ref = refs[2 * n_in:]
    o = _dot(xs[0][0], ws[0][...])
    for x_ref, w_ref in zip(xs[1:], ws[1:]):
        o = o + _dot(x_ref[0], w_ref[...])
    z = DEEPNORM_ALPHA * h_ref[0] + mod_ref[0, 0, 2:3, :] * o
    hn = _layer_norm(z, g_ref[...], b_ref[...])
    hn_ref[0] = hn
    u_ref[0] = _pack_halves(hn * (1.0 + mod_ref[0, 0, 4:5, :]) + mod_ref[0, 0, 3:4, :])


def _mix_out(xs, ws, h, mod, ln_g, ln_b, n_ctx, row0):
    B, N, D = h.shape
    tm = ROW_TILE
    t0 = row0 // tm
    n_out = N - row0
    row_spec = lambda w: pl.BlockSpec((1, tm, w), lambda b, i: (b, i + t0, 0))
    out_spec = pl.BlockSpec((1, tm, D), lambda b, i: (b, i, 0))
    vec_spec = pl.BlockSpec((1, D), lambda b, i: (0, 0))
    return pl.pallas_call(
        functools.partial(_mix_out_kernel, len(xs)),
        grid=(B, n_out // tm),
        in_specs=([row_spec(x.shape[-1]) for x in xs]
                  + [pl.BlockSpec(w.shape, lambda b, i: (0, 0)) for w in ws]
                  + [row_spec(D),
                     pl.BlockSpec((1, 1, 6, D), lambda b, i: (b, jnp.minimum((i + t0) // (n_ctx // tm), 1), 0, 0)),
                     vec_spec, vec_spec]),
        out_specs=[out_spec, pl.BlockSpec((1, tm, D // 2), lambda b, i: (b, i, 0))],
        out_shape=[jax.ShapeDtypeStruct((B, n_out, D), F32), jax.ShapeDtypeStruct((B, n_out, D // 2), I32)],
        compiler_params=_cparams(("arbitrary", "arbitrary")),
    )(*xs, *[w.astype(BF16) for w in ws], h, mod, ln_g.reshape(1, D), ln_b.reshape(1, D))


def _router_kernel(u_ref, rt_ref, bias_ref, tri_ref, e_ref, gw_ref, rank_ref, cnt_ref, carry_ref):
    i = pl.program_id(0)

    @pl.when(i == 0)
    def _():
        carry_ref[...] = jnp.zeros_like(carry_ref)

    tm = u_ref.shape[0]
    per_group = N_EXPERTS // N_GROUPS
    neg = -jnp.inf
    u_lo, u_hi = _unpack_halves(u_ref[...])
    half = u_ref.shape[1]
    logits = (_dot_nt(rt_ref[:, :half], u_lo.astype(BF16))
              + _dot_nt(rt_ref[:, half:], u_hi.astype(BF16)))
    scores = jax.nn.sigmoid(logits)
    sel = scores + bias_ref[...]
    io_in = lax.broadcasted_iota(I32, (per_group, tm), 0)
    grp_rows = []
    for gi in range(N_GROUPS):
        sg = sel[gi * per_group:(gi + 1) * per_group]
        m1 = jnp.max(sg, axis=0, keepdims=True)
        i1 = jnp.min(jnp.where(sg == m1, io_in, per_group), axis=0, keepdims=True)
        m2 = jnp.max(jnp.where(io_in == i1, neg, sg), axis=0, keepdims=True)
        grp_rows.append(m1 + m2)
    grp = jnp.concatenate(grp_rows, axis=0)
    io_g = lax.broadcasted_iota(I32, grp.shape, 0)
    g_sel = jnp.zeros(grp.shape, F32)
    for _ in range(TOPK_GROUPS):
        m = jnp.max(grp, axis=0, keepdims=True)
        hit = io_g == jnp.min(jnp.where(grp == m, io_g, N_GROUPS), axis=0, keepdims=True)
        g_sel = jnp.where(hit, 1.0, g_sel)
        grp = jnp.where(hit, neg, grp)
    selm = jnp.concatenate(
        [jnp.where(g_sel[gi:gi + 1] > 0.5, sel[gi * per_group:(gi + 1) * per_group], NEG_INF)
         for gi in range(N_GROUPS)], axis=0)
    io_e = lax.broadcasted_iota(I32, selm.shape, 0)
    chosen_f = jnp.zeros(selm.shape, F32)
    idx, gws = [], []
    for _ in range(TOP_K):
        m = jnp.max(selm, axis=0, keepdims=True)
        ik = jnp.min(jnp.where(selm == m, io_e, N_EXPERTS), axis=0, keepdims=True)
        hit = io_e == ik
        idx.append(ik)
        gws.append(jnp.sum(jnp.where(hit, scores, 0.0), axis=0, keepdims=True))
        chosen_f = jnp.where(hit, 1.0, chosen_f)
        selm = jnp.where(hit, neg, selm)
    gw = jnp.concatenate(gws, axis=0)
    gw_ref[...] = gw / jnp.sum(gw, axis=0, keepdims=True) * ROUTED_SCALE
    e_ref[...] = jnp.concatenate(idx, axis=0)
    before = _dot(chosen_f.astype(BF16), tri_ref[...]) + carry_ref[...]
    ranks = [jnp.sum(jnp.where(io_e == ik, before, 0.0), axis=0, keepdims=True) for ik in idx]
    rank_ref[...] = jnp.concatenate(ranks, axis=0).astype(I32)
    carry_ref[...] = carry_ref[...] + jnp.sum(chosen_f, axis=1, keepdims=True)
    cnt_ref[...] = carry_ref[...].astype(I32)


def _router(u, router, bias):
    T = u.shape[0]
    D = router.shape[0]
    tm = ROW_TILE
    tri = jnp.asarray(np.triu(np.ones((tm, tm), np.float32), 1), BF16)
    tok_spec = pl.BlockSpec((TOP_K, tm), lambda i: (0, i))
    return pl.pallas_call(
        _router_kernel,
        grid=(T // tm,),
        in_specs=[pl.BlockSpec((tm, D // 2), lambda i: (i, 0)),
                  pl.BlockSpec((N_EXPERTS, D), lambda i: (0, 0)),
                  pl.BlockSpec((N_EXPERTS, 1), lambda i: (0, 0)),
                  pl.BlockSpec((tm, tm), lambda i: (0, 0))],
        out_specs=[tok_spec, tok_spec, tok_spec, pl.BlockSpec((N_EXPERTS, 1), lambda i: (0, 0))],
        out_shape=[jax.ShapeDtypeStruct((TOP_K, T), I32), jax.ShapeDtypeStruct((TOP_K, T), F32),
                   jax.ShapeDtypeStruct((TOP_K, T), I32), jax.ShapeDtypeStruct((N_EXPERTS, 1), I32)],
        scratch_shapes=[pltpu.VMEM((N_EXPERTS, 1), F32)],
        compiler_params=_cparams(("arbitrary",)),
    )(u, router.T.astype(BF16), bias.reshape(N_EXPERTS, 1).astype(F32), tri)


def _sc_mesh():
    return plsc.VectorSubcoreMesh(core_axis_name="c", subcore_axis_name="s")


def _sc_scatter_rows(x, dest, n_rows):
    T, W = x.shape
    K = dest.shape[0]
    win = SC_WINDOW
    n_win = T // win

    @functools.partial(pl.kernel, out_type=jax.ShapeDtypeStruct((n_rows, W), x.dtype), mesh=_sc_mesh(),
                       scratch_types=[])
    def scatter(x_hbm, i_hbm, o_hbm):
        def body(x_vmem, i_vmem):
            pltpu.sync_copy(x_vmem, o_hbm.at[i_vmem.at[0]])

        pltpu.emit_pipeline(
            body,
            grid=(K * n_win,),
            in_specs=[pl.BlockSpec((win, W), lambda j: (j % n_win, 0), pipeline_mode=pl.Buffered(1)),
                      pl.BlockSpec((1, win), lambda j: (0, j))],
            out_specs=[],
            core_axis_name=("c", "s"),
            dimension_semantics=(pltpu.PARALLEL,),
        )(x_hbm, i_hbm)

    return scatter(x, dest.reshape(1, K * T))


def _sc_gather_rows(y, dest):
    K, T = dest.shape
    W = y.shape[1]
    win = SC_WINDOW

    @functools.partial(pl.kernel, out_type=jax.ShapeDtypeStruct((K * T, W), y.dtype), mesh=_sc_mesh(),
                       scratch_types=[])
    def gather(y_hbm, i_hbm, o_hbm):
        def body(i_vmem, o_vmem):
            pltpu.sync_copy(y_hbm.at[i_vmem.at[0]], o_vmem)

        pltpu.emit_pipeline(
            body,
            grid=(K * T // win,),
            in_specs=[pl.BlockSpec((1, win), lambda j: (0, j))],
            out_specs=[pl.BlockSpec((win, W), lambda j: (j, 0), pipeline_mode=pl.Buffered(1))],
            core_axis_name=("c", "s"),
            dimension_semantics=(pltpu.PARALLEL,),
        )(i_hbm, o_hbm)

    return gather(y, dest.reshape(1, K * T)).reshape(K, T, W)


def _expert_kernel(be_ref, nu_ref, x_ref, wi_ref, wo_ref, y_ref, wi_b, wo_b):
    i = pl.program_id(0)

    @pl.when(i < nu_ref[0])
    def _():
        @pl.when(jnp.logical_or(i == 0, be_ref[i] != be_ref[jnp.maximum(i - 1, 0)]))
        def _():
            wi_b[...] = wi_ref[0, 0].astype(BF16)
            wo_b[...] = wo_ref[0, 0].astype(BF16)

        ff = wo_b.shape[0]
        hcat = _dot_halves(x_ref[...], wi_b)
        act = (_silu(hcat[:, :ff]) * hcat[:, ff:]).astype(BF16)
        y_ref[...] = _pack_halves(_dot(act, wo_b[...]))

    @pl.when(i >= nu_ref[0])
    def _():
        y_ref[...] = jnp.zeros_like(y_ref)


def _experts(xs, block_exp, n_used, w_in, w_out, layer):
    P, half = xs.shape
    D, ff2 = w_in.shape[-2:]
    nblk = P // MOE_BLOCK
    row_spec = pl.BlockSpec((MOE_BLOCK, half), lambda i, be, nu: (i, 0))
    return pl.pallas_call(
        _expert_kernel,
        grid_spec=pltpu.PrefetchScalarGridSpec(
            num_scalar_prefetch=2,
            grid=(nblk,),
            in_specs=[row_spec,
                      pl.BlockSpec((1, 1, D, ff2), lambda i, be, nu: (layer, be[i], 0, 0)),
                      pl.BlockSpec((1, 1, ff2 // 2, D), lambda i, be, nu: (layer, be[i], 0, 0))],
            out_specs=row_spec,
            scratch_shapes=[pltpu.VMEM((D, ff2), BF16), pltpu.VMEM((ff2 // 2, D), BF16)]),
        out_shape=jax.ShapeDtypeStruct((P, half), I32),
        compiler_params=_cparams(("arbitrary",)),
    )(block_exp, n_used, xs, w_in, w_out)


def _combine_kernel(yg_ref, gw_ref, u_ref, wsi_ref, wso_ref, h_ref, mod_ref, g_ref, b_ref, o_ref):
    ff = wso_ref.shape[0]
    hcat = _dot_halves(u_ref[...], wsi_ref)
    shared = _dot((_silu(hcat[:, :ff]) * hcat[:, ff:]).astype(BF16), wso_ref[...])
    lo, hi = None, None
    for k in range(TOP_K):
        y_lo, y_hi = _unpack_halves(yg_ref[k])
        gk = gw_ref[:, k:k + 1]
        lo = y_lo * gk if lo is None else lo + y_lo * gk
        hi = y_hi * gk if hi is None else hi + y_hi * gk
    routed = jnp.concatenate([lo, hi], axis=1)
    z = DEEPNORM_ALPHA * h_ref[...] + mod_ref[0] * (routed + shared)
    o_ref[...] = _layer_norm(z, g_ref[...], b_ref[...])


def _combine(yg, gw_t, u, ws_in, ws_out, h, gate, gate_index, ln_g, ln_b):
    T, D = h.shape
    tm = ROW_TILE
    vec_spec = pl.BlockSpec((1, D), lambda i: (0, 0))
    row_spec = pl.BlockSpec((tm, D), lambda i: (i, 0))
    packed_spec = pl.BlockSpec((tm, D // 2), lambda i: (i, 0))
    return pl.pallas_call(
        _combine_kernel,
        grid=(T // tm,),
        in_specs=[pl.BlockSpec((TOP_K, tm, D // 2), lambda i: (0, i, 0)),
                  pl.BlockSpec((tm, TOP_K), lambda i: (i, 0)),
                  packed_spec,
                  pl.BlockSpec(ws_in.shape, lambda i: (0, 0)),
                  pl.BlockSpec(ws_out.shape, lambda i: (0, 0)),
                  row_spec,
                  pl.BlockSpec((1,) + gate.shape[1:], lambda i: (gate_index(i), 0, 0)),
                  vec_spec, vec_spec],
        out_specs=row_spec,
        out_shape=jax.ShapeDtypeStruct((T, D), F32),
        compiler_params=_cparams(("arbitrary",)),
    )(yg, gw_t, u, ws_in.astype(BF16), ws_out.astype(BF16), h, gate,
      ln_g.reshape(1, D), ln_b.reshape(1, D))


def _slots_kernel(e_ref, rank_ref, start_ref, dest_ref):
    io_e = lax.broadcasted_iota(I32, (N_EXPERTS, e_ref.shape[1]), 0)
    rows = [jnp.sum(jnp.where(io_e == e_ref[k:k + 1, :], start_ref[...], 0), axis=0, keepdims=True)
            for k in range(TOP_K)]
    dest_ref[...] = jnp.concatenate(rows, axis=0) + rank_ref[...]


def _slots(eidx, rank, pstart):
    T = eidx.shape[1]
    tm = ROW_TILE
    tok_spec = pl.BlockSpec((TOP_K, tm), lambda i: (0, i))
    return pl.pallas_call(
        _slots_kernel,
        grid=(T // tm,),
        in_specs=[tok_spec, tok_spec, pl.BlockSpec((N_EXPERTS, 1), lambda i: (0, 0))],
        out_specs=tok_spec,
        out_shape=jax.ShapeDtypeStruct((TOP_K, T), I32),
        compiler_params=_cparams(("arbitrary",)),
    )(eidx, rank, pstart.reshape(N_EXPERTS, 1))


def _moe_layer(u, h, gate, gate_index, router, bias, w_in, w_out, ws_in, ws_out, ln_g, ln_b, layer):
    T = u.shape[0]
    eidx, gw, rank, counts = _router(u, router, bias)
    counts = counts[:, 0]
    padded = (counts + MOE_BLOCK - 1) // MOE_BLOCK * MOE_BLOCK
    pend = jnp.cumsum(padded)
    pstart = (pend - padded).astype(I32)
    dest = _slots(eidx, rank, pstart)
    n_blocks = -(-(T * TOP_K + N_EXPERTS * (MOE_BLOCK - 1)) // MOE_BLOCK)
    block_exp = jnp.minimum(jnp.searchsorted(pend, jnp.arange(n_blocks, dtype=I32) * MOE_BLOCK, side='right'),
                            N_EXPERTS - 1).astype(I32)
    n_used = (pend[-1:] // MOE_BLOCK).astype(I32)
    xs = _sc_scatter_rows(u, dest, n_blocks * MOE_BLOCK)
    y = _experts(xs, block_exp, n_used, w_in, w_out, layer)
    return _combine(_sc_gather_rows(y, dest), gw.T, u, ws_in, ws_out, h, gate, gate_index, ln_g, ln_b)


def _seg_ones(width=LANES):
    idx = np.arange(width) // HEAD_DIM
    return jnp.asarray((idx[:, None] == idx[None, :]).astype(np.float32), BF16)


def _head_sum(x, ones_ref):
    outs = []
    for j in range(x.shape[1] // LANES):
        xc = x[:, j * LANES:(j + 1) * LANES]
        hi = xc.astype(BF16)
        lo = (xc - hi.astype(F32)).astype(BF16)
        outs.append(_dot(hi, ones_ref[...]) + _dot(lo, ones_ref[...]))
    return jnp.concatenate(outs, axis=1)


def _rwkv_proj_kernel(seg_tiles, h_ref, hp_ref, hn_ref, mod_ref, mu_ref, wrkv_ref, g1_ref, g2_ref, d1_ref, d2_ref,
                      d0_ref, i1_ref, i2_ref, i0_ref, kk_ref, ka_ref, rk_ref, ones_ref,
                      r_ref, v_ref, a_ref, g_ref, bonus_ref, w_ref, k_ref, b_ref):
    i = pl.program_id(0)
    nb = hp_ref.shape[0]
    shift, scale = mod_ref[0, 0], mod_ref[0, 1]
    u = h_ref[...] * (1.0 + scale) + shift
    starts = jnp.logical_or(i == 0, i == seg_tiles)
    ends = jnp.logical_or(i == seg_tiles - 1, i == pl.num_programs(0) - 1)
    u_before = (hp_ref[...] * (1.0 + scale[:nb]) + shift[:nb]) * jnp.where(starts, 0.0, 1.0)
    u_after = (hn_ref[...] * (1.0 + scale[:nb]) + shift[:nb]) * jnp.where(ends, 0.0, 1.0)
    dx = 0.5 * (jnp.concatenate([u_before, u[:-nb]], axis=0) + jnp.concatenate([u[nb:], u_after], axis=0)) - u
    mix = lambda m: (u + dx * mu_ref[m:m + 1, :])
    xr, xw, xk, xv, xa, xg = [mix(m) for m in range(6)]
    r = _dot(xr.astype(BF16), wrkv_ref[0])
    k = _dot(xk.astype(BF16), wrkv_ref[1])
    v = _dot(xv.astype(BF16), wrkv_ref[2])
    g = _dot(jax.nn.sigmoid(_dot(xg.astype(BF16), g1_ref[...])).astype(BF16), g2_ref[...])
    kk = k * kk_ref[...]
    kk = kk * lax.rsqrt(jnp.maximum(_head_sum(kk * kk, ones_ref), 1e-24))
    r_ref[...] = r
    v_ref[...] = v
    a_ref[...] = -kk
    g_ref[...] = g
    k_sum = None
    xw_b = xw.astype(BF16)
    xa_b = xa.astype(BF16)
    for d in range(2):
        lw = d0_ref[d:d + 1, :] + _dot(jnp.tanh(_dot(xw_b, d1_ref[d])).astype(BF16), d2_ref[d])
        softplus = jnp.maximum(-lw, 0.0) + jnp.log(1.0 + jnp.exp(-jnp.abs(lw)))
        logw = -softplus - 0.5
        w_ref[d] = jnp.exp(-jnp.exp(logw))
        eta = jax.nn.sigmoid(i0_ref[d:d + 1, :] + _dot(_dot(xa_b, i1_ref[d]).astype(BF16), i2_ref[d]))
        k_d = k * (1.0 + (eta - 1.0) * ka_ref[...])
        k_ref[d] = k_d
        b_ref[d] = kk * eta
        k_sum = k_d if k_sum is None else k_sum + k_d
    bonus_ref[...] = _head_sum(r * k_sum * rk_ref[...], ones_ref) * v


def _rwkv_proj(h, mod_rows, batch, n_ctx, p):
    T, D = h.shape
    tm = PROJ_TILE
    per_tile = tm // batch
    seg_tiles = n_ctx // per_tile
    n_steps = T // batch
    row = pl.BlockSpec((tm, D), lambda i: (i, 0))
    before = pl.BlockSpec((batch, D), lambda i: (jnp.maximum(i * per_tile - 1, 0), 0))
    after = pl.BlockSpec((batch, D), lambda i: (jnp.minimum((i + 1) * per_tile, n_steps - 1), 0))
    mod_spec = pl.BlockSpec((1, 2, tm, D), lambda i: (jnp.minimum(i // seg_tiles, 1), 0, 0, 0))
    row2 = pl.BlockSpec((2, tm, D), lambda i: (0, i, 0))
    full = lambda a: pl.BlockSpec(a.shape, lambda i: (0,) * a.ndim)
    bf = lambda a: a.astype(BF16)
    consts = [p['mu'], bf(p['w_rkv']), bf(p['gate1']), bf(p['gate2']), bf(p['dec1']), bf(p['dec2']), p['dec0'],
              bf(p['icl1']), bf(p['icl2']), p['icl0'], p['k_k'].reshape(1, D), p['k_a'].reshape(1, D),
              p['r_k'].reshape(1, D), _seg_ones()]
    one = jax.ShapeDtypeStruct((T, D), F32)
    two = jax.ShapeDtypeStruct((2, T, D), F32)
    return pl.pallas_call(
        functools.partial(_rwkv_proj_kernel, seg_tiles),
        grid=(T // tm,),
        in_specs=[row, before, after, mod_spec] + [full(a) for a in consts],
        out_specs=[row, row, row, row, row, row2, row2, row2],
        out_shape=[one, one, one, one, one, two, two, two],
        compiler_params=_cparams(("arbitrary",)),
    )(h, h, h, mod_rows, *consts)


def _scan_kernel(r_ref, w_ref, k_ref, v_ref, a_ref, b_ref, ones_ref, hsel_ref,
                 y_ref, s_ref, vt_ref):
    d = pl.program_id(0)
    c = pl.program_id(2)
    tc, nb = r_ref.shape[0], r_ref.shape[1]
    tw = SCAN_TILE
    n_wide = r_ref.shape[2] // tw
    heads = tw // HEAD_DIM
    assert heads * tc == tw

    @pl.when(c == 0)
    def _():
        s_ref[...] = jnp.zeros_like(s_ref)

    for bb in range(nb):
        for q in range(n_wide):
            vt = v_ref[:, bb, q * tw:(q + 1) * tw].T
            vt_ref[bb * n_wide + q] = jnp.concatenate(
                [vt[h * HEAD_DIM:(h + 1) * HEAD_DIM] for h in range(heads)], axis=1)

    head_base = (lax.broadcasted_iota(I32, (HEAD_DIM, LANES), 1) // HEAD_DIM) * tc
    tiles = [(bb, q) for bb in range(nb) for q in range(n_wide)]
    groups = [tiles[i:i + SCAN_GROUP] for i in range(0, len(tiles), SCAN_GROUP)]

    def stacked(grp, get, dtype=F32):
        def wide(bb, q):
            return jnp.concatenate(
                [jnp.broadcast_to(get(bb, slice(q * tw + hf * LANES, q * tw + (hf + 1) * LANES)).astype(dtype),
                                  (HEAD_DIM, LANES)) for hf in range(tw // LANES)], axis=1)
        return jnp.concatenate([wide(bb, q) for bb, q in grp], axis=0)

    def load_state(grp):
        return jnp.concatenate([s_ref[bb * n_wide + q] for bb, q in grp], axis=0)

    def emit_y(grp, st_b, t_y):
        r_rows = stacked(grp, lambda bb, cols: r_ref[t_y, bb:bb + 1, cols], BF16)
        yh = _dot_nt(hsel_ref[...], st_b * r_rows)
        first = tiles.index(grp[0])
        y_ref[0, t_y, :, first * HEAD_DIM:(first + len(grp)) * HEAD_DIM] = yh[:heads]

    def step(s_i, carry):
        t = jnp.where(d == 0, s_i, tc - 1 - s_i)
        t_prev = jnp.where(s_i == 0, t, jnp.where(d == 0, t - 1, t + 1))
        pick = head_base + t
        for grp in groups:
            one = lambda ref: stacked(grp, lambda bb, cols: ref[t, bb:bb + 1, cols])
            two = lambda ref: stacked(grp, lambda bb, cols: ref[0, t, bb:bb + 1, cols])
            st = load_state(grp)
            st_b = st.astype(BF16)
            a_rows = stacked(grp, lambda bb, cols: a_ref[t, bb:bb + 1, cols], BF16)
            sa = _dot(st_b * a_rows, ones_ref[...])
            emit_y(grp, st_b, t_prev)
            vcol = jnp.concatenate(
                [jnp.concatenate([jnp.take_along_axis(vt_ref[bb * n_wide + q, :, hf * LANES:(hf + 1) * LANES],
                                                      pick, axis=1) for hf in range(tw // LANES)], axis=1)
                 for bb, q in grp], axis=0)
            st = st * two(w_ref) + sa * two(b_ref) + vcol * two(k_ref)
            for j, (bb, q) in enumerate(grp):
                s_ref[bb * n_wide + q] = st[j * HEAD_DIM:(j + 1) * HEAD_DIM]
        return carry

    lax.fori_loop(0, tc, step, 0, unroll=SCAN_UNROLL)
    t_last = jnp.where(d == 0, tc - 1, 0)
    for grp in groups:
        emit_y(grp, load_state(grp).astype(BF16), t_last)


def _wkv_scan(r, w, k, v, a, b, n_ctx):
    N, B, D = r.shape
    tc = SCAN_CHUNK
    wc = SCAN_COLS
    n_wide = wc // SCAN_TILE
    nc = N // tc
    ncc = n_ctx // tc

    def chunk(d, c):
        rev = jnp.where(c < ncc, ncc - 1 - c, nc - 1 - (c - ncc))
        return jnp.where(d == 0, c, rev)

    one = pl.BlockSpec((tc, B, wc), lambda d, g, c: (chunk(d, c), 0, g))
    two = pl.BlockSpec((1, tc, B, wc), lambda d, g, c: (d, chunk(d, c), 0, g))
    seg = np.arange(SCAN_TILE) // HEAD_DIM
    hsel = np.zeros((8, SCAN_TILE), np.float32)
    for hh in range(SCAN_TILE // HEAD_DIM):
        hsel[hh, seg == hh] = 1.0
    const = lambda a_: pl.BlockSpec(a_.shape, lambda d, g, c: (0, 0))
    consts = [_seg_ones(SCAN_TILE), jnp.asarray(hsel, BF16)]
    heads = SCAN_TILE // HEAD_DIM
    ncg = D // wc
    y = pl.pallas_call(
        _scan_kernel,
        grid=(2, ncg, nc),
        in_specs=[one, two, two, one, one, two] + [const(a_) for a_ in consts],
        out_specs=pl.BlockSpec((1, tc, heads, B * n_wide * HEAD_DIM), lambda d, g, c: (d, chunk(d, c), 0, g)),
        out_shape=jax.ShapeDtypeStruct((2, N, heads, ncg * B * n_wide * HEAD_DIM), F32),
        scratch_shapes=[pltpu.VMEM((B * n_wide, HEAD_DIM, SCAN_TILE), F32),
                        pltpu.VMEM((B * n_wide, HEAD_DIM, SCAN_TILE), F32)],
        compiler_params=_cparams(("arbitrary", "arbitrary", "arbitrary")),
    )(r, w, k, v, a, b, *consts)
    y = y.reshape(2, N, heads, ncg, B, n_wide * HEAD_DIM)
    return jnp.transpose(y, (0, 1, 4, 2, 3, 5)).reshape(2, N, B, D)


def _scan_head_order(d):
    heads = SCAN_TILE // HEAD_DIM
    n_wide = SCAN_COLS // SCAN_TILE
    ncg = d // SCAN_COLS
    order = []
    for g in range(ncg):
        for q in range(n_wide):
            for h in range(heads):
                order.append((h * ncg + g) * n_wide + q)
    return tuple(order)


def _rwkv_out_kernel(head_order, y0_ref, y1_ref, bonus_ref, g_ref, lnx_ref, ones_ref, w_ref, h_ref, mod_ref,
                     lg_ref, lb_ref, hn_ref, u_ref):
    y_in = y0_ref[0] + y1_ref[0]
    y = jnp.concatenate([y_in[:, p * HEAD_DIM:(p + 1) * HEAD_DIM] for p in head_order], axis=1)
    ym = _head_sum(y, ones_ref) * (1.0 / HEAD_DIM)
    yc = y - ym
    yv = _head_sum(yc * yc, ones_ref) * (1.0 / HEAD_DIM)
    yn = yc * lax.rsqrt(yv + LNX_EPS) * lnx_ref[0:1, :] + lnx_ref[1:2, :]
    x = ((yn + bonus_ref[...]) * g_ref[...]).astype(BF16)
    o = _dot(x, w_ref[...])
    z = DEEPNORM_ALPHA * h_ref[...] + mod_ref[0] * o
    hn = _layer_norm(z, lg_ref[...], lb_ref[...])
    hn_ref[...] = hn
    u_ref[...] = _pack_halves(hn * (1.0 + mod_ref[2]) + mod_ref[1])


def _rwkv_out(y, bonus, g, lnx, w_out, h, mod_rows, ln_g, ln_b, row0):
    T, D = h.shape
    tm = ROW_TILE
    t0 = row0 // tm
    off = pl.BlockSpec((tm, D), lambda i: (i + t0, 0))
    out = pl.BlockSpec((tm, D), lambda i: (i, 0))
    full = lambda a: pl.BlockSpec(a.shape, lambda i: (0,) * a.ndim)
    vec = pl.BlockSpec((1, D), lambda i: (0, 0))
    ones = _seg_ones()
    w_b = w_out.astype(BF16)
    return pl.pallas_call(
        functools.partial(_rwkv_out_kernel, _scan_head_order(D)),
        grid=((T - row0) // tm,),
        in_specs=[pl.BlockSpec((1, tm, D), lambda i: (0, i + t0, 0)),
                  pl.BlockSpec((1, tm, D), lambda i: (1, i + t0, 0)),
                  off, off, full(lnx), full(ones), full(w_b), off, full(mod_rows), vec, vec],
        out_specs=[out, pl.BlockSpec((tm, D // 2), lambda i: (i, 0))],
        out_shape=[jax.ShapeDtypeStruct((T - row0, D), F32), jax.ShapeDtypeStruct((T - row0, D // 2), I32)],
        compiler_params=_cparams(("arbitrary",)),
    )(y, y, bonus, g, lnx, ones, w_b, h, mod_rows, ln_g.reshape(1, D), ln_b.reshape(1, D))


def kernel(x, c, ctx, c_ctx, ada_w, ada_b, post_ln_g, post_ln_b, att_w_in, att_w_out, att_sink, diff_lambda_vecs, diff_subln_g, rk_mu, rk_w_rkv, rk_w_out, rk_decay0, rk_decay1, rk_decay2, rk_iclr0, rk_iclr1, rk_iclr2, rk_gate1, rk_gate2, rk_k_k, rk_k_a, rk_r_k, rk_lnx, moe_router, moe_bias, moe_w_in, moe_w_out, moe_ws_in, moe_ws_out):
    B, S, D = x.shape
    L = ctx.shape[1]
    N = L + S
    tm = ROW_TILE
    assert L % tm == 0 and S % tm == 0 and L % SCAN_CHUNK == 0 and S % SCAN_CHUNK == 0
    assert tm % B == 0 and PROJ_TILE % B == 0 and L % (PROJ_TILE // B) == 0 and D % SCAN_COLS == 0

    rows = -(-(B + 1) // 8) * 8
    cvec = jnp.concatenate([c, c_ctx[None, :], jnp.zeros((rows - B - 1, D), F32)], axis=0)
    mods = [_mod_table(_ada_mod(cvec, ada_w[i], ada_b[i]), B, D) for i in range(DEPTH)]

    h0 = jnp.concatenate([ctx, x], axis=1)
    lam_init = 0.8 - 0.6 * math.exp(-0.3 * 0)
    qa, ka, va, qb, kb, vb = _attn_inproj(h0, mods[0], att_w_in[0], L)
    oa = _win_attn(qa, ka, va, att_sink[0], L)
    ob = _diff_attn(qb, kb, vb, diff_lambda_vecs[0], diff_subln_g[0], lam_init, L)
    h1, u1 = _mix_out([oa, ob], [att_w_out[0][:A_WIDTH], att_w_out[0][A_WIDTH:]], h0, mods[0],
                      post_ln_g[0, 0], post_ln_b[0, 0], L, 0)
    tiles_b, tiles_c = N // tm, L // tm
    gate0 = mods[0][:, :, 5].reshape(B * 2, 1, D)
    gate0_index = lambda i: (i // tiles_b) * 2 + jnp.minimum((i % tiles_b) // tiles_c, 1)
    h2 = _moe_layer(u1.reshape(B * N, D // 2), h1.reshape(B * N, D), gate0, gate0_index, moe_router[0], moe_bias[0],
                    moe_w_in, moe_w_out, moe_ws_in[0], moe_ws_out[0],
                    post_ln_g[0, 1], post_ln_b[0, 1], 0).reshape(B, N, D)

    m_ctx, m_lat = mods[1][:, 0], mods[1][:, 1]
    h2_t = jnp.swapaxes(h2, 0, 1).reshape(N * B, D)
    rows_of = lambda m, j, n: jnp.tile(m[:, j], (n // B, 1))
    proj_mod = jnp.stack([jnp.stack([rows_of(m, 0, PROJ_TILE), rows_of(m, 1, PROJ_TILE)]) for m in (m_ctx, m_lat)])
    params = dict(mu=rk_mu[0], w_rkv=rk_w_rkv[0], gate1=rk_gate1[0], gate2=rk_gate2[0],
                  dec0=rk_decay0[0], dec1=rk_decay1[0], dec2=rk_decay2[0],
                  icl0=rk_iclr0[0], icl1=rk_iclr1[0], icl2=rk_iclr2[0],
                  k_k=rk_k_k[0], k_a=rk_k_a[0], r_k=rk_r_k[0])
    r, v, a, g, bonus, w2, k2, b2 = _rwkv_proj(h2_t, proj_mod, B, L, params)
    tmaj = lambda t: t.reshape(t.shape[:-2] + (N, B, D))
    y = _wkv_scan(tmaj(r), tmaj(w2), tmaj(k2), tmaj(v), tmaj(a), tmaj(b2), L)
    lat_rows = lambda j: rows_of(m_lat, j, tm)
    h3, u3 = _rwkv_out(y.reshape(2, N * B, D), bonus, g, rk_lnx[0], rk_w_out[0], h2_t,
                       jnp.stack([lat_rows(2), lat_rows(3), lat_rows(4)]),
                       post_ln_g[1, 0], post_ln_b[1, 0], L * B)
    out = _moe_layer(u3, h3, lat_rows(5)[None], lambda i: 0, moe_router[1], moe_bias[1],
                     moe_w_in, moe_w_out, moe_ws_in[1], moe_ws_out[1],
                     post_ln_g[1, 1], post_ln_b[1, 1], 1)
    return jnp.swapaxes(out.reshape(S, B, D), 0, 1)
```

```python
import functools
import math

import numpy as np
import jax
import jax.numpy as jnp
from jax import lax
from jax.experimental import pallas as pl
from jax.experimental.pallas import tpu as pltpu
from jax.experimental.pallas import tpu_sc as plsc

F32 = jnp.float32
BF16 = jnp.bfloat16
I32 = jnp.int32

HEAD_DIM = 64
GRID_W = 64
ROPE_AXIS_DIM = HEAD_DIM // 2
ROPE_THETA = 10000.0
Q_BLOCK = 128
A_Q_HEADS = 8
A_KV_HEADS = 2
A_GROUP = A_Q_HEADS // A_KV_HEADS
A_WIDTH = A_Q_HEADS * HEAD_DIM
A_KV_WIDTH = A_KV_HEADS * HEAD_DIM
B_HEADS = 4
B_V_DIM = 2 * HEAD_DIM
B_WIDTH = B_HEADS * B_V_DIM
LNX_EPS = 64e-5
N_EXPERTS = 256
TOP_K = 8
N_GROUPS = 8
TOPK_GROUPS = 4
ROUTED_SCALE = 2.5
MOE_BLOCK = 256
LN_EPS = 1e-5
SUBLN_EPS = 1e-5
NEG_INF = -1e30
DEPTH = 2
DEEPNORM_ALPHA = (2 * DEPTH) ** 0.25

LANES = 128
ROW_TILE = 256
PROJ_TILE = 128
SC_WINDOW = 128
SCAN_CHUNK = 64
SCAN_COLS = 1024
SCAN_TILE = 256
SCAN_GROUP = 8
SCAN_UNROLL = 4
VMEM_LIMIT = 56 * 1024 * 1024


def _cparams(sem):
    return pltpu.CompilerParams(dimension_semantics=sem, vmem_limit_bytes=VMEM_LIMIT)


def _silu(x):
    return x * jax.nn.sigmoid(x)


def _layer_norm(z, g, b):
    mu = jnp.mean(z, -1, keepdims=True)
    zc = z - mu
    var = jnp.mean(zc * zc, -1, keepdims=True)
    return zc * lax.rsqrt(var + LN_EPS) * g + b


def _dot(a, b):
    return jnp.dot(a, b, preferred_element_type=F32)


def _dot_nt(a, b):
    return lax.dot_general(a, b, (((1,), (1,)), ((), ())), preferred_element_type=F32)


def _pack_halves(x):
    half = x.shape[1] // 2
    bits = lambda v: lax.bitcast_convert_type(v.astype(BF16).astype(F32), I32)
    return lax.shift_right_logical(bits(x[:, :half]), 16) | bits(x[:, half:])


def _unpack_halves(p):
    lo = lax.bitcast_convert_type(lax.shift_left(p, 16), F32)
    hi = lax.bitcast_convert_type(p & jnp.int32(-65536), F32)
    return lo, hi


def _dot_halves(p, w_ref_or_array):
    lo, hi = _unpack_halves(p)
    half = p.shape[1]
    return _dot(lo.astype(BF16), w_ref_or_array[:half]) + _dot(hi.astype(BF16), w_ref_or_array[half:])


def _ada_kernel(c_ref, w_ref, b_ref, o_ref):
    c = c_ref[...]
    o_ref[...] = _dot(_silu(c).astype(BF16), w_ref[...].astype(BF16)) + b_ref[...]


def _ada_mod(cvec, w, bias):
    R, D = cvec.shape
    n_out = w.shape[1]
    tn = 768
    return pl.pallas_call(
        _ada_kernel,
        grid=(n_out // tn,),
        in_specs=[pl.BlockSpec((R, D), lambda j: (0, 0)),
                  pl.BlockSpec((D, tn), lambda j: (0, j)),
                  pl.BlockSpec((1, tn), lambda j: (0, j))],
        out_specs=pl.BlockSpec((R, tn), lambda j: (0, j)),
        out_shape=jax.ShapeDtypeStruct((R, n_out), F32),
        compiler_params=_cparams(("arbitrary",)),
    )(cvec, w, bias.reshape(1, n_out))


def _mod_table(m, batch, d):
    m_lat = m[:batch].reshape(batch, 6, d)
    m_ctx = jnp.broadcast_to(m[batch].reshape(1, 6, d), (batch, 6, d))
    return jnp.stack([m_ctx, m_lat], axis=1)


def _mod_spec(d, ctx_tiles):
    return pl.BlockSpec((1, 1, 6, d), lambda b, i: (b, jnp.minimum(i // ctx_tiles, 1), 0, 0))


def _rope_tables(n_ctx, n_lat):
    rows = n_lat // GRID_W
    row = np.repeat(np.arange(rows), GRID_W).astype(np.float32)
    col = np.tile(np.arange(GRID_W), rows).astype(np.float32)
    inv = (ROPE_THETA ** (-np.arange(0, ROPE_AXIS_DIM, 2, dtype=np.float32) / ROPE_AXIS_DIM)).astype(np.float32)
    ar = row[:, None] * inv
    ac = col[:, None] * inv
    ang = np.concatenate([ar, ar, ac, ac], -1)
    cos = np.cos(ang).astype(np.float32)
    sin = np.sin(ang).astype(np.float32)
    lower = (np.arange(HEAD_DIM) % ROPE_AXIS_DIM) < (ROPE_AXIS_DIM // 2)
    sin_up = np.where(lower[None, :], -sin, 0.0)
    sin_dn = np.where(lower[None, :], 0.0, sin)

    def full(t, ctx_fill):
        t = np.concatenate([np.full((n_ctx, HEAD_DIM), ctx_fill, np.float32), t], 0)
        return jnp.asarray(np.tile(t, (1, LANES // HEAD_DIM)))

    return full(cos, 1.0), full(sin_up, 0.0), full(sin_dn, 0.0)


def _inproj_kernel(h_ref, mod_ref, w_ref, cos_ref, su_ref, sd_ref,
                   qa_ref, ka_ref, va_ref, qb_ref, kb_ref, vb_ref):
    h = h_ref[0]
    shift = mod_ref[0, 0, 0:1, :]
    scale = mod_ref[0, 0, 1:2, :]
    u = (h * (1.0 + scale) + shift).astype(BF16)
    y = _dot(u, w_ref[...])
    cos, s_up, s_dn = cos_ref[...], su_ref[...], sd_ref[...]
    q_scale = HEAD_DIM ** -0.5

    def rope(xc):
        half = ROPE_AXIS_DIM // 2
        return xc * cos + pltpu.roll(xc, LANES - half, 1) * s_up + pltpu.roll(xc, half, 1) * s_dn

    def emit(out_ref, col0, width, roped, mul):
        for j in range(width // LANES):
            xc = y[:, col0 + j * LANES: col0 + (j + 1) * LANES]
            if roped:
                xc = rope(xc)
            if mul != 1.0:
                xc = xc * mul
            out_ref[0, :, j * LANES:(j + 1) * LANES] = xc.astype(out_ref.dtype)

    c = 0
    emit(qa_ref, c, A_WIDTH, True, q_scale); c += A_WIDTH
    emit(ka_ref, c, A_KV_WIDTH, True, 1.0); c += A_KV_WIDTH
    emit(va_ref, c, A_KV_WIDTH, False, 1.0); c += A_KV_WIDTH
    emit(qb_ref, c, B_WIDTH, True, q_scale); c += B_WIDTH
    emit(kb_ref, c, B_WIDTH, True, 1.0); c += B_WIDTH
    emit(vb_ref, c, B_WIDTH, False, 1.0)


def _attn_inproj(h, mod, w_in, n_ctx):
    B, N, D = h.shape
    tm = ROW_TILE
    cos, s_up, s_dn = _rope_tables(n_ctx, N - n_ctx)
    widths = (A_WIDTH, A_KV_WIDTH, A_KV_WIDTH, B_WIDTH, B_WIDTH, B_WIDTH)
    tab_spec = pl.BlockSpec((tm, LANES), lambda b, i: (i, 0))
    return pl.pallas_call(
        _inproj_kernel,
        grid=(B, N // tm),
        in_specs=[pl.BlockSpec((1, tm, D), lambda b, i: (b, i, 0)),
                  _mod_spec(D, n_ctx // tm),
                  pl.BlockSpec(w_in.shape, lambda b, i: (0, 0)),
                  tab_spec, tab_spec, tab_spec],
        out_specs=[pl.BlockSpec((1, tm, w), lambda b, i: (b, i, 0)) for w in widths],
        out_shape=[jax.ShapeDtypeStruct((B, N, w), BF16) for w in widths],
        compiler_params=_cparams(("arbitrary", "arbitrary")),
    )(h, mod, w_in.astype(BF16), cos, s_up, s_dn)


def _win_attn_kernel(n_ctx_blocks, n_blocks, q_ref, kc_ref, vc_ref, kl_ref, km_ref, kr_ref,
                     vl_ref, vm_ref, vr_ref, sink_ref, o_ref):
    j = pl.program_id(1)
    is_lat = j >= n_ctx_blocks
    qb = Q_BLOCK
    n_c = kc_ref.shape[1]
    rows = A_GROUP * qb
    n_keys = n_c + 3 * qb
    far = 1 << 20
    r_idx = lax.broadcasted_iota(I32, (rows, n_keys), 0) % qb
    cw = lax.broadcasted_iota(I32, (rows, n_keys), 1) - n_c
    off_l = jnp.where(jnp.logical_and(is_lat, j > n_ctx_blocks), 0, far)
    end_m = jnp.where(is_lat, 2 * qb, qb)
    off_r = jnp.where(jnp.logical_and(is_lat, j < n_blocks - 1), 0, far)
    valid = ((cw < 0)
             | ((cw >= 0) & (cw < qb) & (cw >= r_idx + off_l))
             | ((cw >= qb) & (cw < end_m))
             | ((cw >= 2 * qb) & (cw - 2 * qb + off_r <= r_idx)))
    outs = []
    for kv in range(A_KV_HEADS):
        cols = slice(kv * HEAD_DIM, (kv + 1) * HEAD_DIM)
        k_all = jnp.concatenate([kc_ref[0, :, cols], kl_ref[0, :, cols], km_ref[0, :, cols],
                                 kr_ref[0, :, cols]], axis=0)
        v_all = jnp.concatenate([vc_ref[0, :, cols], vl_ref[0, :, cols], vm_ref[0, :, cols],
                                 vr_ref[0, :, cols]], axis=0)
        q0 = kv * A_GROUP
        q = jnp.concatenate([q_ref[0, :, (q0 + g) * HEAD_DIM:(q0 + g + 1) * HEAD_DIM]
                             for g in range(A_GROUP)], axis=0)
        sink = jnp.concatenate([jnp.broadcast_to(sink_ref[q0 + g:q0 + g + 1, 0:1], (qb, 1))
                                for g in range(A_GROUP)], axis=0)
        s = jnp.where(valid, _dot_nt(q, k_all), NEG_INF)
        m = jnp.maximum(jnp.max(s, -1, keepdims=True), sink)
        e = jnp.exp(s - m)
        denom = jnp.sum(e, -1, keepdims=True) + jnp.exp(sink - m)
        o = _dot(e.astype(BF16), v_all) * (1.0 / denom)
        outs += [o[g * qb:(g + 1) * qb] for g in range(A_GROUP)]
    for j2 in range(A_Q_HEADS // 2):
        pair = jnp.concatenate([outs[2 * j2], outs[2 * j2 + 1]], axis=1)
        o_ref[0, :, j2 * LANES:(j2 + 1) * LANES] = pair.astype(o_ref.dtype)


def _win_attn(qa, ka, va, sink, n_ctx):
    B, N, _ = qa.shape
    qb = Q_BLOCK
    nb = N // qb
    ncb = n_ctx // qb
    sink_pad = jnp.broadcast_to(sink.reshape(A_Q_HEADS, 1).astype(F32), (A_Q_HEADS, LANES))

    def left(b, j):
        return (b, jnp.clip(j - 1, ncb, nb - 1), 0)

    def mid(b, j):
        return (b, jnp.clip(j, ncb, nb - 1), 0)

    def right(b, j):
        return (b, jnp.clip(j + 1, ncb, nb - 1), 0)

    kv_blk = lambda im: pl.BlockSpec((1, qb, A_KV_WIDTH), im)
    ctx_blk = pl.BlockSpec((1, n_ctx, A_KV_WIDTH), lambda b, j: (b, 0, 0))
    return pl.pallas_call(
        functools.partial(_win_attn_kernel, ncb, nb),
        grid=(B, nb),
        in_specs=[pl.BlockSpec((1, qb, A_WIDTH), lambda b, j: (b, j, 0)),
                  ctx_blk, ctx_blk,
                  kv_blk(left), kv_blk(mid), kv_blk(right),
                  kv_blk(left), kv_blk(mid), kv_blk(right),
                  pl.BlockSpec((A_Q_HEADS, LANES), lambda b, j: (0, 0))],
        out_specs=pl.BlockSpec((1, qb, A_WIDTH), lambda b, j: (b, j, 0)),
        out_shape=jax.ShapeDtypeStruct((B, N, A_WIDTH), BF16),
        compiler_params=_cparams(("arbitrary", "arbitrary")),
    )(qa, ka, va, ka, ka, ka, va, va, va, sink_pad)


def _diff_attn_kernel(n_ctx, lam_init, q_ref, k_ref, v_ref, lv_ref, g_ref, o_ref):
    j = pl.program_id(1)
    lv = lv_ref[...]
    lam = (jnp.exp(jnp.sum(lv[0:1] * lv[1:2], -1, keepdims=True))
           - jnp.exp(jnp.sum(lv[2:3] * lv[3:4], -1, keepdims=True)) + lam_init)
    gain = g_ref[...] * (1.0 - lam_init)

    def run(n_keys):
        for hd in range(B_HEADS):
            parts = []
            for mm in range(2):
                c0 = (hd * 2 + mm) * HEAD_DIM
                q = q_ref[0, :, c0:c0 + HEAD_DIM]
                k = k_ref[0, :n_keys, c0:c0 + HEAD_DIM]
                s = _dot_nt(q, k)
                e = jnp.exp(s - jnp.max(s, -1, keepdims=True))
                parts.append((e, jnp.sum(e, -1, keepdims=True)))
            (e0, l0), (e1, l1) = parts
            v = v_ref[0, :n_keys, hd * B_V_DIM:(hd + 1) * B_V_DIM]
            o = _dot(e0.astype(BF16), v) * (1.0 / l0) - _dot(e1.astype(BF16), v) * (lam / l1)
            o = o * lax.rsqrt(jnp.mean(o * o, -1, keepdims=True) + SUBLN_EPS) * gain
            o_ref[0, :, hd * B_V_DIM:(hd + 1) * B_V_DIM] = o.astype(o_ref.dtype)

    @pl.when(j == 0)
    def _():
        run(n_ctx)

    @pl.when(j > 0)
    def _():
        run(k_ref.shape[1])


def _diff_attn(qb, kb, vb, lam_vecs, subln_g, lam_init, n_ctx):
    B, N, _ = qb.shape
    tq = n_ctx
    return pl.pallas_call(
        functools.partial(_diff_attn_kernel, n_ctx, lam_init),
        grid=(B, N // tq),
        in_specs=[pl.BlockSpec((1, tq, B_WIDTH), lambda b, j: (b, j, 0)),
                  pl.BlockSpec((1, N, B_WIDTH), lambda b, j: (b, 0, 0)),
                  pl.BlockSpec((1, N, B_WIDTH), lambda b, j: (b, 0, 0)),
                  pl.BlockSpec((4, HEAD_DIM), lambda b, j: (0, 0)),
                  pl.BlockSpec((1, B_V_DIM), lambda b, j: (0, 0))],
        out_specs=pl.BlockSpec((1, tq, B_WIDTH), lambda b, j: (b, j, 0)),
        out_shape=jax.ShapeDtypeStruct((B, N, B_WIDTH), BF16),
        compiler_params=_cparams(("arbitrary", "arbitrary")),
    )(qb, kb, vb, lam_vecs.astype(F32), subln_g.reshape(1, B_V_DIM).astype(F32))


def _mix_out_kernel(n_in, *refs):
    xs = refs[:n_in]
    ws = refs[n_in:2 * n_in]
    h_ref, mod_ref, g_ref, b_ref, hn_ref, u_ref = refs[2 * n_in:]
    o = _dot(xs[0][0], ws[0][...])
    for x_ref, w_ref in zip(xs[1:], ws[1:]):
        o = o + _dot(x_ref[0], w_ref[...])
    z = DEEPNORM_ALPHA * h_ref[0] + mod_ref[0, 0, 2:3, :] * o
    hn = _layer_norm(z, g_ref[...], b_ref[...])
    hn_ref[0] = hn
    u_ref[0] = _pack_halves(hn * (1.0 + mod_ref[0, 0, 4:5, :]) + mod_ref[0, 0, 3:4, :])


def _mix_out(xs, ws, h, mod, ln_g, ln_b, n_ctx, row0):
    B, N, D = h.shape
    tm = ROW_TILE
    t0 = row0 // tm
    n_out = N - row0
    row_spec = lambda w: pl.BlockSpec((1, tm, w), lambda b, i: (b, i + t0, 0))
    out_spec = pl.BlockSpec((1, tm, D), lambda b, i: (b, i, 0))
    vec_spec = pl.BlockSpec((1, D), lambda b, i: (0, 0))
    return pl.pallas_call(
        functools.partial(_mix_out_kernel, len(xs)),
        grid=(B, n_out // tm),
        in_specs=([row_spec(x.shape[-1]) for x in xs]
                  + [pl.BlockSpec(w.shape, lambda b, i: (0, 0)) for w in ws]
                  + [row_spec(D),
                     pl.BlockSpec((1, 1, 6, D), lambda b, i: (b, jnp.minimum((i + t0) // (n_ctx // tm), 1), 0, 0)),
                     vec_spec, vec_spec]),
        out_specs=[out_spec, pl.BlockSpec((1, tm, D // 2), lambda b, i: (b, i, 0))],
        out_shape=[jax.ShapeDtypeStruct((B, n_out, D), F32), jax.ShapeDtypeStruct((B, n_out, D // 2), I32)],
        compiler_params=_cparams(("arbitrary", "arbitrary")),
    )(*xs, *[w.astype(BF16) for w in ws], h, mod, ln_g.reshape(1, D), ln_b.reshape(1, D))


def _router_kernel(u_ref, rt_ref, bias_ref, tri_ref, e_ref, gw_ref, rank_ref, cnt_ref, carry_ref):
    i = pl.program_id(0)

    @pl.when(i == 0)
    def _():
        carry_ref[...] = jnp.zeros_like(carry_ref)

    tm = u_ref.shape[0]
    per_group = N_EXPERTS // N_GROUPS
    neg = -jnp.inf
    u_lo, u_hi = _unpack_halves(u_ref[...])
    half = u_ref.shape[1]
    logits = (_dot_nt(rt_ref[:, :half], u_lo.astype(BF16))
              + _dot_nt(rt_ref[:, half:], u_hi.astype(BF16)))
    scores = jax.nn.sigmoid(logits)
    sel = scores + bias_ref[...]
    io_in = lax.broadcasted_iota(I32, (per_group, tm), 0)
    grp_rows = []
    for gi in range(N_GROUPS):
        sg = sel[gi * per_group:(gi + 1) * per_group]
        m1 = jnp.max(sg, axis=0, keepdims=True)
        i1 = jnp.min(jnp.where(sg == m1, io_in, per_group), axis=0, keepdims=True)
        m2 = jnp.max(jnp.where(io_in == i1, neg, sg), axis=0, keepdims=True)
        grp_rows.append(m1 + m2)
    grp = jnp.concatenate(grp_rows, axis=0)
    io_g = lax.broadcasted_iota(I32, grp.shape, 0)
    g_sel = jnp.zeros(grp.shape, F32)
    for _ in range(TOPK_GROUPS):
        m = jnp.max(grp, axis=0, keepdims=True)
        hit = io_g == jnp.min(jnp.where(grp == m, io_g, N_GROUPS), axis=0, keepdims=True)
        g_sel = jnp.where(hit, 1.0, g_sel)
        grp = jnp.where(hit, neg, grp)
    selm = jnp.concatenate(
        [jnp.where(g_sel[gi:gi + 1] > 0.5, sel[gi * per_group:(gi + 1) * per_group], NEG_INF)
         for gi in range(N_GROUPS)], axis=0)
    io_e = lax.broadcasted_iota(I32, selm.shape, 0)
    chosen_f = jnp.zeros(selm.shape, F32)
    idx, gws = [], []
    for _ in range(TOP_K):
        m = jnp.max(selm, axis=0, keepdims=True)
        ik = jnp.min(jnp.where(selm == m, io_e, N_EXPERTS), axis=0, keepdims=True)
        hit = io_e == ik
        idx.append(ik)
        gws.append(jnp.sum(jnp.where(hit, scores, 0.0), axis=0, keepdims=True))
        chosen_f = jnp.where(hit, 1.0, chosen_f)
        selm = jnp.where(hit, neg, selm)
    gw = jnp.concatenate(gws, axis=0)
    gw_ref[...] = gw / jnp.sum(gw, axis=0, keepdims=True) * ROUTED_SCALE
    e_ref[...] = jnp.concatenate(idx, axis=0)
    before = _dot(chosen_f.astype(BF16), tri_ref[...]) + carry_ref[...]
    ranks = [jnp.sum(jnp.where(io_e == ik, before, 0.0), axis=0, keepdims=True) for ik in idx]
    rank_ref[...] = jnp.concatenate(ranks, axis=0).astype(I32)
    carry_ref[...] = carry_ref[...] + jnp.sum(chosen_f, axis=1, keepdims=True)
    cnt_ref[...] = carry_ref[...].astype(I32)


def _router(u, router, bias):
    T = u.shape[0]
    D = router.shape[0]
    tm = ROW_TILE
    tri = jnp.asarray(np.triu(np.ones((tm, tm), np.float32), 1), BF16)
    tok_spec = pl.BlockSpec((TOP_K, tm), lambda i: (0, i))
    return pl.pallas_call(
        _router_kernel,
        grid=(T // tm,),
        in_specs=[pl.BlockSpec((tm, D // 2), lambda i: (i, 0)),
                  pl.BlockSpec((N_EXPERTS, D), lambda i: (0, 0)),
                  pl.BlockSpec((N_EXPERTS, 1), lambda i: (0, 0)),
                  pl.BlockSpec((tm, tm), lambda i: (0, 0))],
        out_specs=[tok_spec, tok_spec, tok_spec, pl.BlockSpec((N_EXPERTS, 1), lambda i: (0, 0))],
        out_shape=[jax.ShapeDtypeStruct((TOP_K, T), I32), jax.ShapeDtypeStruct((TOP_K, T), F32),
                   jax.ShapeDtypeStruct((TOP_K, T), I32), jax.ShapeDtypeStruct((N_EXPERTS, 1), I32)],
        scratch_shapes=[pltpu.VMEM((N_EXPERTS, 1), F32)],
        compiler_params=_cparams(("arbitrary",)),
    )(u, router.T.astype(BF16), bias.reshape(N_EXPERTS, 1).astype(F32), tri)


def _sc_mesh():
    return plsc.VectorSubcoreMesh(core_axis_name="c", subcore_axis_name="s")


def _sc_scatter_rows(x, dest, n_rows):
    T, W = x.shape
    K = dest.shape[0]
    win = SC_WINDOW
    n_win = T // win

    @functools.partial(pl.kernel, out_type=jax.ShapeDtypeStruct((n_rows, W), x.dtype), mesh=_sc_mesh(),
                       scratch_types=[])
    def scatter(x_hbm, i_hbm, o_hbm):
        def body(x_vmem, i_vmem):
            pltpu.sync_copy(x_vmem, o_hbm.at[i_vmem.at[0]])

        pltpu.emit_pipeline(
            body,
            grid=(K * n_win,),
            in_specs=[pl.BlockSpec((win, W), lambda j: (j % n_win, 0), pipeline_mode=pl.Buffered(1)),
                      pl.BlockSpec((1, win), lambda j: (0, j))],
            out_specs=[],
            core_axis_name=("c", "s"),
            dimension_semantics=(pltpu.PARALLEL,),
        )(x_hbm, i_hbm)

    return scatter(x, dest.reshape(1, K * T))


def _sc_gather_rows(y, dest):
    K, T = dest.shape
    W = y.shape[1]
    win = SC_WINDOW

    @functools.partial(pl.kernel, out_type=jax.ShapeDtypeStruct((K * T, W), y.dtype), mesh=_sc_mesh(),
                       scratch_types=[])
    def gather(y_hbm, i_hbm, o_hbm):
        def body(i_vmem, o_vmem):
            pltpu.sync_copy(y_hbm.at[i_vmem.at[0]], o_vmem)

        pltpu.emit_pipeline(
            body,
            grid=(K * T // win,),
            in_specs=[pl.BlockSpec((1, win), lambda j: (0, j))],
            out_specs=[pl.BlockSpec((win, W), lambda j: (j, 0), pipeline_mode=pl.Buffered(1))],
            core_axis_name=("c", "s"),
            dimension_semantics=(pltpu.PARALLEL,),
        )(i_hbm, o_hbm)

    return gather(y, dest.reshape(1, K * T)).reshape(K, T, W)


def _expert_kernel(first_ref, x_hbm, wi_ref, wo_ref, y_hbm, wi_b, wo_b, xbuf, ybuf, sem_in, sem_out):
    e = pl.program_id(0)
    blk = xbuf.shape[1]
    g0, g1 = first_ref[e], first_ref[e + 1]
    g_end = first_ref[pl.num_programs(0)]

    def x_copy(g):
        return pltpu.make_async_copy(x_hbm.at[pl.ds(g * blk, blk)], xbuf.at[g % 2], sem_in.at[g % 2])

    def y_copy(g):
        return pltpu.make_async_copy(ybuf.at[g % 2], y_hbm.at[pl.ds(g * blk, blk)], sem_out.at[g % 2])

    @pl.when(jnp.logical_and(e == 0, g_end > 0))
    def _():
        x_copy(0).start()

    @pl.when(g1 > g0)
    def _():
        wi_b[...] = wi_ref[0, 0].astype(BF16)
        wo_b[...] = wo_ref[0, 0].astype(BF16)

    ff = wo_b.shape[0]

    def block(g, carry):
        x_copy(g).wait()

        @pl.when(g + 1 < g_end)
        def _():
            x_copy(g + 1).start()

        @pl.when(g >= 2)
        def _():
            y_copy(g - 2).wait()

        hcat = _dot_halves(xbuf[g % 2], wi_b)
        act = (_silu(hcat[:, :ff]) * hcat[:, ff:]).astype(BF16)
        ybuf[g % 2] = _pack_halves(_dot(act, wo_b[...]))
        y_copy(g).start()
        return carry

    lax.fori_loop(g0, g1, block, 0)

    @pl.when(e == pl.num_programs(0) - 1)
    def _():
        @pl.when(g_end >= 2)
        def _():
            y_copy(g_end - 2).wait()

        @pl.when(g_end >= 1)
        def _():
            y_copy(g_end - 1).wait()


def _experts(xs, first_block, w_in, w_out, layer):
    P, half = xs.shape
    n_exp, D, ff2 = w_in.shape[-3:]
    return pl.pallas_call(
        _expert_kernel,
        grid_spec=pltpu.PrefetchScalarGridSpec(
            num_scalar_prefetch=1,
            grid=(n_exp,),
            in_specs=[pl.BlockSpec(memory_space=pl.ANY),
                      pl.BlockSpec((1, 1, D, ff2), lambda e, fb: (layer, e, 0, 0)),
                      pl.BlockSpec((1, 1, ff2 // 2, D), lambda e, fb: (layer, e, 0, 0))],
            out_specs=pl.BlockSpec(memory_space=pl.ANY),
            scratch_shapes=[pltpu.VMEM((D, ff2), BF16), pltpu.VMEM((ff2 // 2, D), BF16),
                            pltpu.VMEM((2, MOE_BLOCK, half), I32), pltpu.VMEM((2, MOE_BLOCK, half), I32),
                            pltpu.SemaphoreType.DMA((2,)), pltpu.SemaphoreType.DMA((2,))]),
        out_shape=jax.ShapeDtypeStruct((P, half), I32),
        compiler_params=_cparams(("arbitrary",)),
    )(first_block, xs, w_in, w_out)


def _combine_kernel(yg_ref, gw_ref, u_ref, wsi_ref, wso_ref, h_ref, mod_ref, g_ref, b_ref, o_ref):
    ff = wso_ref.shape[0]
    hcat = _dot_halves(u_ref[...], wsi_ref)
    shared = _dot((_silu(hcat[:, :ff]) * hcat[:, ff:]).astype(BF16), wso_ref[...])
    lo, hi = None, None
    for k in range(TOP_K):
        y_lo, y_hi = _unpack_halves(yg_ref[k])
        gk = gw_ref[:, k:k + 1]
        lo = y_lo * gk if lo is None else lo + y_lo * gk
        hi = y_hi * gk if hi is None else hi + y_hi * gk
    routed = jnp.concatenate([lo, hi], axis=1)
    z = DEEPNORM_ALPHA * h_ref[...] + mod_ref[0] * (routed + shared)
    o_ref[...] = _layer_norm(z, g_ref[...], b_ref[...])


def _combine(yg, gw_t, u, ws_in, ws_out, h, gate, gate_index, ln_g, ln_b):
    T, D = h.shape
    tm = ROW_TILE
    vec_spec = pl.BlockSpec((1, D), lambda i: (0, 0))
    row_spec = pl.BlockSpec((tm, D), lambda i: (i, 0))
    packed_spec = pl.BlockSpec((tm, D // 2), lambda i: (i, 0))
    return pl.pallas_call(
        _combine_kernel,
        grid=(T // tm,),
        in_specs=[pl.BlockSpec((TOP_K, tm, D // 2), lambda i: (0, i, 0)),
                  pl.BlockSpec((tm, TOP_K), lambda i: (i, 0)),
                  packed_spec,
                  pl.BlockSpec(ws_in.shape, lambda i: (0, 0)),
                  pl.BlockSpec(ws_out.shape, lambda i: (0, 0)),
                  row_spec,
                  pl.BlockSpec((1,) + gate.shape[1:], lambda i: (gate_index(i), 0, 0)),
                  vec_spec, vec_spec],
        out_specs=row_spec,
        out_shape=jax.ShapeDtypeStruct((T, D), F32),
        compiler_params=_cparams(("arbitrary",)),
    )(yg, gw_t, u, ws_in.astype(BF16), ws_out.astype(BF16), h, gate,
      ln_g.reshape(1, D), ln_b.reshape(1, D))


def _slots_kernel(e_ref, rank_ref, start_ref, dest_ref):
    io_e = lax.broadcasted_iota(I32, (N_EXPERTS, e_ref.shape[1]), 0)
    rows = [jnp.sum(jnp.where(io_e == e_ref[k:k + 1, :], start_ref[...], 0), axis=0, keepdims=True)
            for k in range(TOP_K)]
    dest_ref[...] = jnp.concatenate(rows, axis=0) + rank_ref[...]


def _slots(eidx, rank, pstart):
    T = eidx.shape[1]
    tm = ROW_TILE
    tok_spec = pl.BlockSpec((TOP_K, tm), lambda i: (0, i))
    return pl.pallas_call(
        _slots_kernel,
        grid=(T // tm,),
        in_specs=[tok_spec, tok_spec, pl.BlockSpec((N_EXPERTS, 1), lambda i: (0, 0))],
        out_specs=tok_spec,
        out_shape=jax.ShapeDtypeStruct((TOP_K, T), I32),
        compiler_params=_cparams(("arbitrary",)),
    )(eidx, rank, pstart.reshape(N_EXPERTS, 1))


def _moe_layer(u, h, gate, gate_index, router, bias, w_in, w_out, ws_in, ws_out, ln_g, ln_b, layer):
    T = u.shape[0]
    eidx, gw, rank, counts = _router(u, router, bias)
    counts = counts[:, 0]
    padded = (counts + MOE_BLOCK - 1) // MOE_BLOCK * MOE_BLOCK
    pend = jnp.cumsum(padded)
    pstart = (pend - padded).astype(I32)
    dest = _slots(eidx, rank, pstart)
    n_blocks = -(-(T * TOP_K + N_EXPERTS * (MOE_BLOCK - 1)) // MOE_BLOCK)
    first_block = jnp.concatenate([jnp.zeros((1,), I32), (pend // MOE_BLOCK).astype(I32)])
    xs = _sc_scatter_rows(u, dest, n_blocks * MOE_BLOCK)
    y = _experts(xs, first_block, w_in, w_out, layer)
    return _combine(_sc_gather_rows(y, dest), gw.T, u, ws_in, ws_out, h, gate, gate_index, ln_g, ln_b)


def _seg_ones(width=LANES):
    idx = np.arange(width) // HEAD_DIM
    return jnp.asarray((idx[:, None] == idx[None, :]).astype(np.float32), BF16)


def _head_sum(x, ones_ref):
    outs = []
    for j in range(x.shape[1] // LANES):
        xc = x[:, j * LANES:(j + 1) * LANES]
        hi = xc.astype(BF16)
        lo = (xc - hi.astype(F32)).astype(BF16)
        outs.append(_dot(hi, ones_ref[...]) + _dot(lo, ones_ref[...]))
    return jnp.concatenate(outs, axis=1)


def _rwkv_proj_kernel(seg_tiles, h_ref, hp_ref, hn_ref, mod_ref, mu_ref, wrkv_ref, g1_ref, g2_ref, d1_ref, d2_ref,
                      d0_ref, i1_ref, i2_ref, i0_ref, kk_ref, ka_ref, rk_ref, ones_ref,
                      r_ref, v_ref, a_ref, g_ref, bonus_ref, w_ref, k_ref, b_ref):
    i = pl.program_id(0)
    nb = hp_ref.shape[0]
    shift, scale = mod_ref[0, 0], mod_ref[0, 1]
    u = h_ref[...] * (1.0 + scale) + shift
    starts = jnp.logical_or(i == 0, i == seg_tiles)
    ends = jnp.logical_or(i == seg_tiles - 1, i == pl.num_programs(0) - 1)
    u_before = (hp_ref[...] * (1.0 + scale[:nb]) + shift[:nb]) * jnp.where(starts, 0.0, 1.0)
    u_after = (hn_ref[...] * (1.0 + scale[:nb]) + shift[:nb]) * jnp.where(ends, 0.0, 1.0)
    dx = 0.5 * (jnp.concatenate([u_before, u[:-nb]], axis=0) + jnp.concatenate([u[nb:], u_after], axis=0)) - u
    mix = lambda m: (u + dx * mu_ref[m:m + 1, :])
    xr, xw, xk, xv, xa, xg = [mix(m) for m in range(6)]
    r = _dot(xr.astype(BF16), wrkv_ref[0])
    k = _dot(xk.astype(BF16), wrkv_ref[1])
    v = _dot(xv.astype(BF16), wrkv_ref[2])
    g = _dot(jax.nn.sigmoid(_dot(xg.astype(BF16), g1_ref[...])).astype(BF16), g2_ref[...])
    kk = k * kk_ref[...]
    kk = kk * lax.rsqrt(jnp.maximum(_head_sum(kk * kk, ones_ref), 1e-24))
    r_ref[...] = r
    v_ref[...] = v
    a_ref[...] = -kk
    g_ref[...] = g
    k_sum = None
    xw_b = xw.astype(BF16)
    xa_b = xa.astype(BF16)
    for d in range(2):
        lw = d0_ref[d:d + 1, :] + _dot(jnp.tanh(_dot(xw_b, d1_ref[d])).astype(BF16), d2_ref[d])
        softplus = jnp.maximum(-lw, 0.0) + jnp.log(1.0 + jnp.exp(-jnp.abs(lw)))
        logw = -softplus - 0.5
        w_ref[d] = jnp.exp(-jnp.exp(logw))
        eta = jax.nn.sigmoid(i0_ref[d:d + 1, :] + _dot(_dot(xa_b, i1_ref[d]).astype(BF16), i2_ref[d]))
        k_d = k * (1.0 + (eta - 1.0) * ka_ref[...])
        k_ref[d] = k_d
        b_ref[d] = kk * eta
        k_sum = k_d if k_sum is None else k_sum + k_d
    bonus_ref[...] = _head_sum(r * k_sum * rk_ref[...], ones_ref) * v


def _rwkv_proj(h, mod_rows, batch, n_ctx, p):
    T, D = h.shape
    tm = PROJ_TILE
    per_tile = tm // batch
    seg_tiles = n_ctx // per_tile
    n_steps = T // batch
    row = pl.BlockSpec((tm, D), lambda i: (i, 0))
    before = pl.BlockSpec((batch, D), lambda i: (jnp.maximum(i * per_tile - 1, 0), 0))
    after = pl.BlockSpec((batch, D), lambda i: (jnp.minimum((i + 1) * per_tile, n_steps - 1), 0))
    mod_spec = pl.BlockSpec((1, 2, tm, D), lambda i: (jnp.minimum(i // seg_tiles, 1), 0, 0, 0))
    row2 = pl.BlockSpec((2, tm, D), lambda i: (0, i, 0))
    full = lambda a: pl.BlockSpec(a.shape, lambda i: (0,) * a.ndim)
    bf = lambda a: a.astype(BF16)
    consts = [p['mu'], bf(p['w_rkv']), bf(p['gate1']), bf(p['gate2']), bf(p['dec1']), bf(p['dec2']), p['dec0'],
              bf(p['icl1']), bf(p['icl2']), p['icl0'], p['k_k'].reshape(1, D), p['k_a'].reshape(1, D),
              p['r_k'].reshape(1, D), _seg_ones()]
    one = jax.ShapeDtypeStruct((T, D), F32)
    two = jax.ShapeDtypeStruct((2, T, D), F32)
    return pl.pallas_call(
        functools.partial(_rwkv_proj_kernel, seg_tiles),
        grid=(T // tm,),
        in_specs=[row, before, after, mod_spec] + [full(a) for a in consts],
        out_specs=[row, row, row, row, row, row2, row2, row2],
        out_shape=[one, one, one, one, one, two, two, two],
        compiler_params=_cparams(("arbitrary",)),
    )(h, h, h, mod_rows, *consts)


def _scan_kernel(r_ref, w_ref, k_ref, v_ref, a_ref, b_ref, ones_ref, hsel_ref,
                 y_ref, s_ref, vt_ref):
    d = pl.program_id(0)
    c = pl.program_id(2)
    tc, nb = r_ref.shape[0], r_ref.shape[1]
    tw = SCAN_TILE
    n_wide = r_ref.shape[2] // tw
    heads = tw // HEAD_DIM
    assert heads * tc == tw

    @pl.when(c == 0)
    def _():
        s_ref[...] = jnp.zeros_like(s_ref)

    for bb in range(nb):
        for q in range(n_wide):
            vt = v_ref[:, bb, q * tw:(q + 1) * tw].T
            vt_ref[bb * n_wide + q] = jnp.concatenate(
                [vt[h * HEAD_DIM:(h + 1) * HEAD_DIM] for h in range(heads)], axis=1)

    head_base = (lax.broadcasted_iota(I32, (HEAD_DIM, LANES), 1) // HEAD_DIM) * tc
    tiles = [(bb, q) for bb in range(nb) for q in range(n_wide)]
    groups = [tiles[i:i + SCAN_GROUP] for i in range(0, len(tiles), SCAN_GROUP)]

    def stacked(grp, get, dtype=F32):
        def wide(bb, q):
            return jnp.concatenate(
                [jnp.broadcast_to(get(bb, slice(q * tw + hf * LANES, q * tw + (hf + 1) * LANES)).astype(dtype),
                                  (HEAD_DIM, LANES)) for hf in range(tw // LANES)], axis=1)
        return jnp.concatenate([wide(bb, q) for bb, q in grp], axis=0)

    def load_state(grp):
        return jnp.concatenate([s_ref[bb * n_wide + q] for bb, q in grp], axis=0)

    def emit_y(grp, st_b, t_y):
        r_rows = stacked(grp, lambda bb, cols: r_ref[t_y, bb:bb + 1, cols], BF16)
        yh = _dot_nt(hsel_ref[...], st_b * r_rows)
        first = tiles.index(grp[0])
        y_ref[0, t_y, :, first * HEAD_DIM:(first + len(grp)) * HEAD_DIM] = yh[:heads]

    def step(s_i, carry):
        t = jnp.where(d == 0, s_i, tc - 1 - s_i)
        t_prev = jnp.where(s_i == 0, t, jnp.where(d == 0, t - 1, t + 1))
        pick = head_base + t
        for grp in groups:
            one = lambda ref: stacked(grp, lambda bb, cols: ref[t, bb:bb + 1, cols])
            two = lambda ref: stacked(grp, lambda bb, cols: ref[0, t, bb:bb + 1, cols])
            st = load_state(grp)
            st_b = st.astype(BF16)
            a_rows = stacked(grp, lambda bb, cols: a_ref[t, bb:bb + 1, cols], BF16)
            sa = _dot(st_b * a_rows, ones_ref[...])
            emit_y(grp, st_b, t_prev)
            vcol = jnp.concatenate(
                [jnp.concatenate([jnp.take_along_axis(vt_ref[bb * n_wide + q, :, hf * LANES:(hf + 1) * LANES],
                                                      pick, axis=1) for hf in range(tw // LANES)], axis=1)
                 for bb, q in grp], axis=0)
            st = st * two(w_ref) + sa * two(b_ref) + vcol * two(k_ref)
            for j, (bb, q) in enumerate(grp):
                s_ref[bb * n_wide + q] = st[j * HEAD_DIM:(j + 1) * HEAD_DIM]
        return carry

    lax.fori_loop(0, tc, step, 0, unroll=SCAN_UNROLL)
    t_last = jnp.where(d == 0, tc - 1, 0)
    for grp in groups:
        emit_y(grp, load_state(grp).astype(BF16), t_last)


def _wkv_scan(r, w, k, v, a, b, n_ctx):
    N, B, D = r.shape
    tc = SCAN_CHUNK
    wc = SCAN_COLS
    n_wide = wc // SCAN_TILE
    nc = N // tc
    ncc = n_ctx // tc

    def chunk(d, c):
        rev = jnp.where(c < ncc, ncc - 1 - c, nc - 1 - (c - ncc))
        return jnp.where(d == 0, c, rev)

    one = pl.BlockSpec((tc, B, wc), lambda d, g, c: (chunk(d, c), 0, g))
    two = pl.BlockSpec((1, tc, B, wc), lambda d, g, c: (d, chunk(d, c), 0, g))
    seg = np.arange(SCAN_TILE) // HEAD_DIM
    hsel = np.zeros((8, SCAN_TILE), np.float32)
    for hh in range(SCAN_TILE // HEAD_DIM):
        hsel[hh, seg == hh] = 1.0
    const = lambda a_: pl.BlockSpec(a_.shape, lambda d, g, c: (0, 0))
    consts = [_seg_ones(SCAN_TILE), jnp.asarray(hsel, BF16)]
    heads = SCAN_TILE // HEAD_DIM
    ncg = D // wc
    y = pl.pallas_call(
        _scan_kernel,
        grid=(2, ncg, nc),
        in_specs=[one, two, two, one, one, two] + [const(a_) for a_ in consts],
        out_specs=pl.BlockSpec((1, tc, heads, B * n_wide * HEAD_DIM), lambda d, g, c: (d, chunk(d, c), 0, g)),
        out_shape=jax.ShapeDtypeStruct((2, N, heads, ncg * B * n_wide * HEAD_DIM), F32),
        scratch_shapes=[pltpu.VMEM((B * n_wide, HEAD_DIM, SCAN_TILE), F32),
                        pltpu.VMEM((B * n_wide, HEAD_DIM, SCAN_TILE), F32)],
        compiler_params=_cparams(("arbitrary", "arbitrary", "arbitrary")),
    )(r, w, k, v, a, b, *consts)
    y = y.reshape(2, N, heads, ncg, B, n_wide * HEAD_DIM)
    return jnp.transpose(y, (0, 1, 4, 2, 3, 5)).reshape(2, N, B, D)


def _scan_head_order(d):
    heads = SCAN_TILE // HEAD_DIM
    n_wide = SCAN_COLS // SCAN_TILE
    ncg = d // SCAN_COLS
    order = []
    for g in range(ncg):
        for q in range(n_wide):
            for h in range(heads):
                order.append((h * ncg + g) * n_wide + q)
    return tuple(order)


def _rwkv_out_kernel(head_order, y0_ref, y1_ref, bonus_ref, g_ref, lnx_ref, ones_ref, w_ref, h_ref, mod_ref,
                     lg_ref, lb_ref, hn_ref, u_ref):
    y_in = y0_ref[0] + y1_ref[0]
    y = jnp.concatenate([y_in[:, p * HEAD_DIM:(p + 1) * HEAD_DIM] for p in head_order], axis=1)
    ym = _head_sum(y, ones_ref) * (1.0 / HEAD_DIM)
    yc = y - ym
    yv = _head_sum(yc * yc, ones_ref) * (1.0 / HEAD_DIM)
    yn = yc * lax.rsqrt(yv + LNX_EPS) * lnx_ref[0:1, :] + lnx_ref[1:2, :]
    x = ((yn + bonus_ref[...]) * g_ref[...]).astype(BF16)
    o = _dot(x, w_ref[...])
    z = DEEPNORM_ALPHA * h_ref[...] + mod_ref[0] * o
    hn = _layer_norm(z, lg_ref[...], lb_ref[...])
    hn_ref[...] = hn
    u_ref[...] = _pack_halves(hn * (1.0 + mod_ref[2]) + mod_ref[1])


def _rwkv_out(y, bonus, g, lnx, w_out, h, mod_rows, ln_g, ln_b, row0):
    T, D = h.shape
    tm = ROW_TILE
    t0 = row0 // tm
    off = pl.BlockSpec((tm, D), lambda i: (i + t0, 0))
    out = pl.BlockSpec((tm, D), lambda i: (i, 0))
    full = lambda a: pl.BlockSpec(a.shape, lambda i: (0,) * a.ndim)
    vec = pl.BlockSpec((1, D), lambda i: (0, 0))
    ones = _seg_ones()
    w_b = w_out.astype(BF16)
    return pl.pallas_call(
        functools.partial(_rwkv_out_kernel, _scan_head_order(D)),
        grid=((T - row0) // tm,),
        in_specs=[pl.BlockSpec((1, tm, D), lambda i: (0, i + t0, 0)),
                  pl.BlockSpec((1, tm, D), lambda i: (1, i + t0, 0)),
                  off, off, full(lnx), full(ones), full(w_b), off, full(mod_rows), vec, vec],
        out_specs=[out, pl.BlockSpec((tm, D // 2), lambda i: (i, 0))],
        out_shape=[jax.ShapeDtypeStruct((T - row0, D), F32), jax.ShapeDtypeStruct((T - row0, D // 2), I32)],
        compiler_params=_cparams(("arbitrary",)),
    )(y, y, bonus, g, lnx, ones, w_b, h, mod_rows, ln_g.reshape(1, D), ln_b.reshape(1, D))


def kernel(x, c, ctx, c_ctx, ada_w, ada_b, post_ln_g, post_ln_b, att_w_in, att_w_out, att_sink, diff_lambda_vecs, diff_subln_g, rk_mu, rk_w_rkv, rk_w_out, rk_decay0, rk_decay1, rk_decay2, rk_iclr0, rk_iclr1, rk_iclr2, rk_gate1, rk_gate2, rk_k_k, rk_k_a, rk_r_k, rk_lnx, moe_router, moe_bias, moe_w_in, moe_w_out, moe_ws_in, moe_ws_out):
    B, S, D = x.shape
    L = ctx.shape[1]
    N = L + S
    tm = ROW_TILE
    assert L % tm == 0 and S % tm == 0 and L % SCAN_CHUNK == 0 and S % SCAN_CHUNK == 0
    assert tm % B == 0 and PROJ_TILE % B == 0 and L % (PROJ_TILE // B) == 0 and D % SCAN_COLS == 0

    rows = -(-(B + 1) // 8) * 8
    cvec = jnp.concatenate([c, c_ctx[None, :], jnp.zeros((rows - B - 1, D), F32)], axis=0)
    mods = [_mod_table(_ada_mod(cvec, ada_w[i], ada_b[i]), B, D) for i in range(DEPTH)]

    h0 = jnp.concatenate([ctx, x], axis=1)
    lam_init = 0.8 - 0.6 * math.exp(-0.3 * 0)
    qa, ka, va, qb, kb, vb = _attn_inproj(h0, mods[0], att_w_in[0], L)
    oa = _win_attn(qa, ka, va, att_sink[0], L)
    ob = _diff_attn(qb, kb, vb, diff_lambda_vecs[0], diff_subln_g[0], lam_init, L)
    h1, u1 = _mix_out([oa, ob], [att_w_out[0][:A_WIDTH], att_w_out[0][A_WIDTH:]], h0, mods[0],
                      post_ln_g[0, 0], post_ln_b[0, 0], L, 0)
    tiles_b, tiles_c = N // tm, L // tm
    gate0 = mods[0][:, :, 5].reshape(B * 2, 1, D)
    gate0_index = lambda i: (i // tiles_b) * 2 + jnp.minimum((i % tiles_b) // tiles_c, 1)
    h2 = _moe_layer(u1.reshape(B * N, D // 2), h1.reshape(B * N, D), gate0, gate0_index, moe_router[0], moe_bias[0],
                    moe_w_in, moe_w_out, moe_ws_in[0], moe_ws_out[0],
                    post_ln_g[0, 1], post_ln_b[0, 1], 0).reshape(B, N, D)

    m_ctx, m_lat = mods[1][:, 0], mods[1][:, 1]
    h2_t = jnp.swapaxes(h2, 0, 1).reshape(N * B, D)
    rows_of = lambda m, j, n: jnp.tile(m[:, j], (n // B, 1))
    proj_mod = jnp.stack([jnp.stack([rows_of(m, 0, PROJ_TILE), rows_of(m, 1, PROJ_TILE)]) for m in (m_ctx, m_lat)])
    params = dict(mu=rk_mu[0], w_rkv=rk_w_rkv[0], gate1=rk_gate1[0], gate2=rk_gate2[0],
                  dec0=rk_decay0[0], dec1=rk_decay1[0], dec2=rk_decay2[0],
                  icl0=rk_iclr0[0], icl1=rk_iclr1[0], icl2=rk_iclr2[0],
                  k_k=rk_k_k[0], k_a=rk_k_a[0], r_k=rk_r_k[0])
    r, v, a, g, bonus, w2, k2, b2 = _rwkv_proj(h2_t, proj_mod, B, L, params)
    tmaj = lambda t: t.reshape(t.shape[:-2] + (N, B, D))
    y = _wkv_scan(tmaj(r), tmaj(w2), tmaj(k2), tmaj(v), tmaj(a), tmaj(b2), L)
    lat_rows = lambda j: rows_of(m_lat, j, tm)
    h3, u3 = _rwkv_out(y.reshape(2, N * B, D), bonus, g, rk_lnx[0], rk_w_out[0], h2_t,
                       jnp.stack([lat_rows(2), lat_rows(3), lat_rows(4)]),
                       post_ln_g[1, 0], post_ln_b[1, 0], L * B)
    out = _moe_layer(u3, h3, lat_rows(5)[None], lambda i: 0, moe_router[1], moe_bias[1],
                     moe_w_in, moe_w_out, moe_ws_in[1], moe_ws_out[1],
                     post_ln_g[1, 1], post_ln_b[1, 1], 1)
    return jnp.swapaxes(out.reshape(S, B, D), 0, 1)
```

```python
import functools
import math

import numpy as np
import jax
import jax.numpy as jnp
from jax import lax
from jax.experimental import pallas as pl
from jax.experimental.pallas import tpu as pltpu
from jax.experimental.pallas import tpu_sc as plsc

F32 = jnp.float32
BF16 = jnp.bfloat16
I32 = jnp.int32

HEAD_DIM = 64
GRID_W = 64
ROPE_AXIS_DIM = HEAD_DIM // 2
ROPE_THETA = 10000.0
Q_BLOCK = 128
A_Q_HEADS = 8
A_KV_HEADS = 2
A_GROUP = A_Q_HEADS // A_KV_HEADS
A_WIDTH = A_Q_HEADS * HEAD_DIM
A_KV_WIDTH = A_KV_HEADS * HEAD_DIM
B_HEADS = 4
B_V_DIM = 2 * HEAD_DIM
B_WIDTH = B_HEADS * B_V_DIM
LNX_EPS = 64e-5
N_EXPERTS = 256
TOP_K = 8
N_GROUPS = 8
TOPK_GROUPS = 4
ROUTED_SCALE = 2.5
MOE_BLOCK = 256
EXPERT_IN_SLOTS = 4
LN_EPS = 1e-5
SUBLN_EPS = 1e-5
NEG_INF = -1e30
DEPTH = 2
DEEPNORM_ALPHA = (2 * DEPTH) ** 0.25

LANES = 128
ROW_TILE = 256
PROJ_TILE = 128
SC_WINDOW = 128
SCAN_CHUNK = 64
SCAN_COLS = 1024
SCAN_TILE = 256
SCAN_GROUP = 8
SCAN_UNROLL = 4
VMEM_LIMIT = 56 * 1024 * 1024


def _cparams(sem):
    return pltpu.CompilerParams(dimension_semantics=sem, vmem_limit_bytes=VMEM_LIMIT)


def _silu(x):
    return x * jax.nn.sigmoid(x)


def _layer_norm(z, g, b):
    mu = jnp.mean(z, -1, keepdims=True)
    zc = z - mu
    var = jnp.mean(zc * zc, -1, keepdims=True)
    return zc * lax.rsqrt(var + LN_EPS) * g + b


def _dot(a, b):
    return jnp.dot(a, b, preferred_element_type=F32)


def _dot_nt(a, b):
    return lax.dot_general(a, b, (((1,), (1,)), ((), ())), preferred_element_type=F32)


def _pack_halves(x):
    half = x.shape[1] // 2
    bits = lambda v: lax.bitcast_convert_type(v.astype(BF16).astype(F32), I32)
    return lax.shift_right_logical(bits(x[:, :half]), 16) | bits(x[:, half:])


def _unpack_halves(p):
    lo = lax.bitcast_convert_type(lax.shift_left(p, 16), F32)
    hi = lax.bitcast_convert_type(p & jnp.int32(-65536), F32)
    return lo, hi


def _dot_halves(p, w_ref_or_array):
    lo, hi = _unpack_halves(p)
    half = p.shape[1]
    return _dot(lo.astype(BF16), w_ref_or_array[:half]) + _dot(hi.astype(BF16), w_ref_or_array[half:])


def _ada_kernel(c_ref, w_ref, b_ref, o_ref):
    c = c_ref[...]
    o_ref[...] = _dot(_silu(c).astype(BF16), w_ref[...].astype(BF16)) + b_ref[...]


def _ada_mod(cvec, w, bias):
    R, D = cvec.shape
    n_out = w.shape[1]
    tn = 768
    return pl.pallas_call(
        _ada_kernel,
        grid=(n_out // tn,),
        in_specs=[pl.BlockSpec((R, D), lambda j: (0, 0)),
                  pl.BlockSpec((D, tn), lambda j: (0, j)),
                  pl.BlockSpec((1, tn), lambda j: (0, j))],
        out_specs=pl.BlockSpec((R, tn), lambda j: (0, j)),
        out_shape=jax.ShapeDtypeStruct((R, n_out), F32),
        compiler_params=_cparams(("arbitrary",)),
    )(cvec, w, bias.reshape(1, n_out))


def _mod_table(m, batch, d):
    m_lat = m[:batch].reshape(batch, 6, d)
    m_ctx = jnp.broadcast_to(m[batch].reshape(1, 6, d), (batch, 6, d))
    return jnp.stack([m_ctx, m_lat], axis=1)


def _mod_spec(d, ctx_tiles):
    return pl.BlockSpec((1, 1, 6, d), lambda b, i: (b, jnp.minimum(i // ctx_tiles, 1), 0, 0))


def _rope_tables(n_ctx, n_lat):
    rows = n_lat // GRID_W
    row = np.repeat(np.arange(rows), GRID_W).astype(np.float32)
    col = np.tile(np.arange(GRID_W), rows).astype(np.float32)
    inv = (ROPE_THETA ** (-np.arange(0, ROPE_AXIS_DIM, 2, dtype=np.float32) / ROPE_AXIS_DIM)).astype(np.float32)
    ar = row[:, None] * inv
    ac = col[:, None] * inv
    ang = np.concatenate([ar, ar, ac, ac], -1)
    cos = np.cos(ang).astype(np.float32)
    sin = np.sin(ang).astype(np.float32)
    lower = (np.arange(HEAD_DIM) % ROPE_AXIS_DIM) < (ROPE_AXIS_DIM // 2)
    sin_up = np.where(lower[None, :], -sin, 0.0)
    sin_dn = np.where(lower[None, :], 0.0, sin)

    def full(t, ctx_fill):
        t = np.concatenate([np.full((n_ctx, HEAD_DIM), ctx_fill, np.float32), t], 0)
        return jnp.asarray(np.tile(t, (1, LANES // HEAD_DIM)))

    return full(cos, 1.0), full(sin_up, 0.0), full(sin_dn, 0.0)


def _inproj_kernel(h_ref, mod_ref, w_ref, cos_ref, su_ref, sd_ref,
                   qa_ref, ka_ref, va_ref, qb_ref, kb_ref, vb_ref):
    h = h_ref[0]
    shift = mod_ref[0, 0, 0:1, :]
    scale = mod_ref[0, 0, 1:2, :]
    u = (h * (1.0 + scale) + shift).astype(BF16)
    y = _dot(u, w_ref[...])
    cos, s_up, s_dn = cos_ref[...], su_ref[...], sd_ref[...]
    q_scale = HEAD_DIM ** -0.5

    def rope(xc):
        half = ROPE_AXIS_DIM // 2
        return xc * cos + pltpu.roll(xc, LANES - half, 1) * s_up + pltpu.roll(xc, half, 1) * s_dn

    def emit(out_ref, col0, width, roped, mul):
        for j in range(width // LANES):
            xc = y[:, col0 + j * LANES: col0 + (j + 1) * LANES]
            if roped:
                xc = rope(xc)
            if mul != 1.0:
                xc = xc * mul
            out_ref[0, :, j * LANES:(j + 1) * LANES] = xc.astype(out_ref.dtype)

    c = 0
    emit(qa_ref, c, A_WIDTH, True, q_scale); c += A_WIDTH
    emit(ka_ref, c, A_KV_WIDTH, True, 1.0); c += A_KV_WIDTH
    emit(va_ref, c, A_KV_WIDTH, False, 1.0); c += A_KV_WIDTH
    emit(qb_ref, c, B_WIDTH, True, q_scale); c += B_WIDTH
    emit(kb_ref, c, B_WIDTH, True, 1.0); c += B_WIDTH
    emit(vb_ref, c, B_WIDTH, False, 1.0)


def _attn_inproj(h, mod, w_in, n_ctx):
    B, N, D = h.shape
    tm = ROW_TILE
    cos, s_up, s_dn = _rope_tables(n_ctx, N - n_ctx)
    widths = (A_WIDTH, A_KV_WIDTH, A_KV_WIDTH, B_WIDTH, B_WIDTH, B_WIDTH)
    tab_spec = pl.BlockSpec((tm, LANES), lambda b, i: (i, 0))
    return pl.pallas_call(
        _inproj_kernel,
        grid=(B, N // tm),
        in_specs=[pl.BlockSpec((1, tm, D), lambda b, i: (b, i, 0)),
                  _mod_spec(D, n_ctx // tm),
                  pl.BlockSpec(w_in.shape, lambda b, i: (0, 0)),
                  tab_spec, tab_spec, tab_spec],
        out_specs=[pl.BlockSpec((1, tm, w), lambda b, i: (b, i, 0)) for w in widths],
        out_shape=[jax.ShapeDtypeStruct((B, N, w), BF16) for w in widths],
        compiler_params=_cparams(("arbitrary", "arbitrary")),
    )(h, mod, w_in.astype(BF16), cos, s_up, s_dn)


def _win_attn_kernel(n_ctx_blocks, n_blocks, q_ref, kc_ref, vc_ref, kl_ref, km_ref, kr_ref,
                     vl_ref, vm_ref, vr_ref, sink_ref, o_ref):
    j = pl.program_id(1)
    is_lat = j >= n_ctx_blocks
    qb = Q_BLOCK
    n_c = kc_ref.shape[1]
    rows = A_GROUP * qb
    n_keys = n_c + 3 * qb
    far = 1 << 20
    r_idx = lax.broadcasted_iota(I32, (rows, n_keys), 0) % qb
    cw = lax.broadcasted_iota(I32, (rows, n_keys), 1) - n_c
    off_l = jnp.where(jnp.logical_and(is_lat, j > n_ctx_blocks), 0, far)
    end_m = jnp.where(is_lat, 2 * qb, qb)
    off_r = jnp.where(jnp.logical_and(is_lat, j < n_blocks - 1), 0, far)
    valid = ((cw < 0)
             | ((cw >= 0) & (cw < qb) & (cw >= r_idx + off_l))
             | ((cw >= qb) & (cw < end_m))
             | ((cw >= 2 * qb) & (cw - 2 * qb + off_r <= r_idx)))
    outs = []
    for kv in range(A_KV_HEADS):
        cols = slice(kv * HEAD_DIM, (kv + 1) * HEAD_DIM)
        k_all = jnp.concatenate([kc_ref[0, :, cols], kl_ref[0, :, cols], km_ref[0, :, cols],
                                 kr_ref[0, :, cols]], axis=0)
        v_all = jnp.concatenate([vc_ref[0, :, cols], vl_ref[0, :, cols], vm_ref[0, :, cols],
                                 vr_ref[0, :, cols]], axis=0)
        q0 = kv * A_GROUP
        q = jnp.concatenate([q_ref[0, :, (q0 + g) * HEAD_DIM:(q0 + g + 1) * HEAD_DIM]
                             for g in range(A_GROUP)], axis=0)
        sink = jnp.concatenate([jnp.broadcast_to(sink_ref[q0 + g:q0 + g + 1, 0:1], (qb, 1))
                                for g in range(A_GROUP)], axis=0)
        s = jnp.where(valid, _dot_nt(q, k_all), NEG_INF)
        m = jnp.maximum(jnp.max(s, -1, keepdims=True), sink)
        e = jnp.exp(s - m)
        denom = jnp.sum(e, -1, keepdims=True) + jnp.exp(sink - m)
        o = _dot(e.astype(BF16), v_all) * (1.0 / denom)
        outs += [o[g * qb:(g + 1) * qb] for g in range(A_GROUP)]
    for j2 in range(A_Q_HEADS // 2):
        pair = jnp.concatenate([outs[2 * j2], outs[2 * j2 + 1]], axis=1)
        o_ref[0, :, j2 * LANES:(j2 + 1) * LANES] = pair.astype(o_ref.dtype)


def _win_attn(qa, ka, va, sink, n_ctx):
    B, N, _ = qa.shape
    qb = Q_BLOCK
    nb = N // qb
    ncb = n_ctx // qb
    sink_pad = jnp.broadcast_to(sink.reshape(A_Q_HEADS, 1).astype(F32), (A_Q_HEADS, LANES))

    def left(b, j):
        return (b, jnp.clip(j - 1, ncb, nb - 1), 0)

    def mid(b, j):
        return (b, jnp.clip(j, ncb, nb - 1), 0)

    def right(b, j):
        return (b, jnp.clip(j + 1, ncb, nb - 1), 0)

    kv_blk = lambda im: pl.BlockSpec((1, qb, A_KV_WIDTH), im)
    ctx_blk = pl.BlockSpec((1, n_ctx, A_KV_WIDTH), lambda b, j: (b, 0, 0))
    return pl.pallas_call(
        functools.partial(_win_attn_kernel, ncb, nb),
        grid=(B, nb),
        in_specs=[pl.BlockSpec((1, qb, A_WIDTH), lambda b, j: (b, j, 0)),
                  ctx_blk, ctx_blk,
                  kv_blk(left), kv_blk(mid), kv_blk(right),
                  kv_blk(left), kv_blk(mid), kv_blk(right),
                  pl.BlockSpec((A_Q_HEADS, LANES), lambda b, j: (0, 0))],
        out_specs=pl.BlockSpec((1, qb, A_WIDTH), lambda b, j: (b, j, 0)),
        out_shape=jax.ShapeDtypeStruct((B, N, A_WIDTH), BF16),
        compiler_params=_cparams(("arbitrary", "arbitrary")),
    )(qa, ka, va, ka, ka, ka, va, va, va, sink_pad)


def _diff_attn_kernel(n_ctx, lam_init, q_ref, k_ref, v_ref, lv_ref, g_ref, o_ref):
    j = pl.program_id(1)
    lv = lv_ref[...]
    lam = (jnp.exp(jnp.sum(lv[0:1] * lv[1:2], -1, keepdims=True))
           - jnp.exp(jnp.sum(lv[2:3] * lv[3:4], -1, keepdims=True)) + lam_init)
    gain = g_ref[...] * (1.0 - lam_init)

    def run(n_keys):
        for hd in range(B_HEADS):
            parts = []
            for mm in range(2):
                c0 = (hd * 2 + mm) * HEAD_DIM
                q = q_ref[0, :, c0:c0 + HEAD_DIM]
                k = k_ref[0, :n_keys, c0:c0 + HEAD_DIM]
                s = _dot_nt(q, k)
                e = jnp.exp(s - jnp.max(s, -1, keepdims=True))
                parts.append((e, jnp.sum(e, -1, keepdims=True)))
            (e0, l0), (e1, l1) = parts
            v = v_ref[0, :n_keys, hd * B_V_DIM:(hd + 1) * B_V_DIM]
            o = _dot(e0.astype(BF16), v) * (1.0 / l0) - _dot(e1.astype(BF16), v) * (lam / l1)
            o = o * lax.rsqrt(jnp.mean(o * o, -1, keepdims=True) + SUBLN_EPS) * gain
            o_ref[0, :, hd * B_V_DIM:(hd + 1) * B_V_DIM] = o.astype(o_ref.dtype)

    @pl.when(j == 0)
    def _():
        run(n_ctx)

    @pl.when(j > 0)
    def _():
        run(k_ref.shape[1])


def _diff_attn(qb, kb, vb, lam_vecs, subln_g, lam_init, n_ctx):
    B, N, _ = qb.shape
    tq = n_ctx
    return pl.pallas_call(
        functools.partial(_diff_attn_kernel, n_ctx, lam_init),
        grid=(B, N // tq),
        in_specs=[pl.BlockSpec((1, tq, B_WIDTH), lambda b, j: (b, j, 0)),
                  pl.BlockSpec((1, N, B_WIDTH), lambda b, j: (b, 0, 0)),
                  pl.BlockSpec((1, N, B_WIDTH), lambda b, j: (b, 0, 0)),
                  pl.BlockSpec((4, HEAD_DIM), lambda b, j: (0, 0)),
                  pl.BlockSpec((1, B_V_DIM), lambda b, j: (0, 0))],
        out_specs=pl.BlockSpec((1, tq, B_WIDTH), lambda b, j: (b, j, 0)),
        out_shape=jax.ShapeDtypeStruct((B, N, B_WIDTH), BF16),
        compiler_params=_cparams(("arbitrary", "arbitrary")),
    )(qb, kb, vb, lam_vecs.astype(F32), subln_g.reshape(1, B_V_DIM).astype(F32))


def _mix_out_kernel(n_in, *refs):
    xs = refs[:n_in]
    ws = refs[n_in:2 * n_in]
    h_ref, mod_ref, g_ref, b_ref, hn_ref, u_ref = refs[2 * n_in:]
    o = _dot(xs[0][0], ws[0][...])
    for x_ref, w_ref in zip(xs[1:], ws[1:]):
        o = o + _dot(x_ref[0], w_ref[...])
    z = DEEPNORM_ALPHA * h_ref[0] + mod_ref[0, 0, 2:3, :] * o
    hn = _layer_norm(z, g_ref[...], b_ref[...])
    hn_ref[0] = hn
    u_ref[0] = _pack_halves(hn * (1.0 + mod_ref[0, 0, 4:5, :]) + mod_ref[0, 0, 3:4, :])


def _mix_out(xs, ws, h, mod, ln_g, ln_b, n_ctx, row0):
    B, N, D = h.shape
    tm = ROW_TILE
    t0 = row0 // tm
    n_out = N - row0
    row_spec = lambda w: pl.BlockSpec((1, tm, w), lambda b, i: (b, i + t0, 0))
    out_spec = pl.BlockSpec((1, tm, D), lambda b, i: (b, i, 0))
    vec_spec = pl.BlockSpec((1, D), lambda b, i: (0, 0))
    return pl.pallas_call(
        functools.partial(_mix_out_kernel, len(xs)),
        grid=(B, n_out // tm),
        in_specs=([row_spec(x.shape[-1]) for x in xs]
                  + [pl.BlockSpec(w.shape, lambda b, i: (0, 0)) for w in ws]
                  + [row_spec(D),
                     pl.BlockSpec((1, 1, 6, D), lambda b, i: (b, jnp.minimum((i + t0) // (n_ctx // tm), 1), 0, 0)),
                     vec_spec, vec_spec]),
        out_specs=[out_spec, pl.BlockSpec((1, tm, D // 2), lambda b, i: (b, i, 0))],
        out_shape=[jax.ShapeDtypeStruct((B, n_out, D), F32), jax.ShapeDtypeStruct((B, n_out, D // 2), I32)],
        compiler_params=_cparams(("arbitrary", "arbitrary")),
    )(*xs, *[w.astype(BF16) for w in ws], h, mod, ln_g.reshape(1, D), ln_b.reshape(1, D))


def _router_kernel(u_ref, rt_ref, bias_ref, tri_ref, e_ref, gw_ref, rank_ref, cnt_ref, carry_ref):
    i = pl.program_id(0)

    @pl.when(i == 0)
    def _():
        carry_ref[...] = jnp.zeros_like(carry_ref)

    tm = u_ref.shape[0]
    per_group = N_EXPERTS // N_GROUPS
    neg = -jnp.inf
    u_lo, u_hi = _unpack_halves(u_ref[...])
    half = u_ref.shape[1]
    logits = (_dot_nt(rt_ref[:, :half], u_lo.astype(BF16))
              + _dot_nt(rt_ref[:, half:], u_hi.astype(BF16)))
    scores = jax.nn.sigmoid(logits)
    sel = scores + bias_ref[...]
    io_in = lax.broadcasted_iota(I32, (per_group, tm), 0)
    grp_rows = []
    for gi in range(N_GROUPS):
        sg = sel[gi * per_group:(gi + 1) * per_group]
        m1 = jnp.max(sg, axis=0, keepdims=True)
        i1 = jnp.min(jnp.where(sg == m1, io_in, per_group), axis=0, keepdims=True)
        m2 = jnp.max(jnp.where(io_in == i1, neg, sg), axis=0, keepdims=True)
        grp_rows.append(m1 + m2)
    grp = jnp.concatenate(grp_rows, axis=0)
    io_g = lax.broadcasted_iota(I32, grp.shape, 0)
    g_sel = jnp.zeros(grp.shape, F32)
    for _ in range(TOPK_GROUPS):
        m = jnp.max(grp, axis=0, keepdims=True)
        hit = io_g == jnp.min(jnp.where(grp == m, io_g, N_GROUPS), axis=0, keepdims=True)
        g_sel = jnp.where(hit, 1.0, g_sel)
        grp = jnp.where(hit, neg, grp)
    selm = jnp.concatenate(
        [jnp.where(g_sel[gi:gi + 1] > 0.5, sel[gi * per_group:(gi + 1) * per_group], NEG_INF)
         for gi in range(N_GROUPS)], axis=0)
    io_e = lax.broadcasted_iota(I32, selm.shape, 0)
    chosen_f = jnp.zeros(selm.shape, F32)
    idx, gws = [], []
    for _ in range(TOP_K):
        m = jnp.max(selm, axis=0, keepdims=True)
        ik = jnp.min(jnp.where(selm == m, io_e, N_EXPERTS), axis=0, keepdims=True)
        hit = io_e == ik
        idx.append(ik)
        gws.append(jnp.sum(jnp.where(hit, scores, 0.0), axis=0, keepdims=True))
        chosen_f = jnp.where(hit, 1.0, chosen_f)
        selm = jnp.where(hit, neg, selm)
    gw = jnp.concatenate(gws, axis=0)
    gw_ref[...] = gw / jnp.sum(gw, axis=0, keepdims=True) * ROUTED_SCALE
    e_ref[...] = jnp.concatenate(idx, axis=0)
    before = _dot(chosen_f.astype(BF16), tri_ref[...]) + carry_ref[...]
    ranks = [jnp.sum(jnp.where(io_e == ik, before, 0.0), axis=0, keepdims=True) for ik in idx]
    rank_ref[...] = jnp.concatenate(ranks, axis=0).astype(I32)
    carry_ref[...] = carry_ref[...] + jnp.sum(chosen_f, axis=1, keepdims=True)
    cnt_ref[...] = carry_ref[...].astype(I32)


def _router(u, router, bias):
    T = u.shape[0]
    D = router.shape[0]
    tm = ROW_TILE
    tri = jnp.asarray(np.triu(np.ones((tm, tm), np.float32), 1), BF16)
    tok_spec = pl.BlockSpec((TOP_K, tm), lambda i: (0, i))
    return pl.pallas_call(
        _router_kernel,
        grid=(T // tm,),
        in_specs=[pl.BlockSpec((tm, D // 2), lambda i: (i, 0)),
                  pl.BlockSpec((N_EXPERTS, D), lambda i: (0, 0)),
                  pl.BlockSpec((N_EXPERTS, 1), lambda i: (0, 0)),
                  pl.BlockSpec((tm, tm), lambda i: (0, 0))],
        out_specs=[tok_spec, tok_spec, tok_spec, pl.BlockSpec((N_EXPERTS, 1), lambda i: (0, 0))],
        out_shape=[jax.ShapeDtypeStruct((TOP_K, T), I32), jax.ShapeDtypeStruct((TOP_K, T), F32),
                   jax.ShapeDtypeStruct((TOP_K, T), I32), jax.ShapeDtypeStruct((N_EXPERTS, 1), I32)],
        scratch_shapes=[pltpu.VMEM((N_EXPERTS, 1), F32)],
        compiler_params=_cparams(("arbitrary",)),
    )(u, router.T.astype(BF16), bias.reshape(N_EXPERTS, 1).astype(F32), tri)


def _sc_mesh():
    return plsc.VectorSubcoreMesh(core_axis_name="c", subcore_axis_name="s")


def _sc_scatter_rows(x, dest, n_rows):
    T, W = x.shape
    K = dest.shape[0]
    win = SC_WINDOW
    n_win = T // win

    @functools.partial(pl.kernel, out_type=jax.ShapeDtypeStruct((n_rows, W), x.dtype), mesh=_sc_mesh(),
                       scratch_types=[])
    def scatter(x_hbm, i_hbm, o_hbm):
        def body(x_vmem, i_vmem):
            pltpu.sync_copy(x_vmem, o_hbm.at[i_vmem.at[0]])

        pltpu.emit_pipeline(
            body,
            grid=(K * n_win,),
            in_specs=[pl.BlockSpec((win, W), lambda j: (j % n_win, 0), pipeline_mode=pl.Buffered(1)),
                      pl.BlockSpec((1, win), lambda j: (0, j))],
            out_specs=[],
            core_axis_name=("c", "s"),
            dimension_semantics=(pltpu.PARALLEL,),
        )(x_hbm, i_hbm)

    return scatter(x, dest.reshape(1, K * T))


def _sc_gather_rows(y, dest):
    K, T = dest.shape
    W = y.shape[1]
    win = SC_WINDOW

    @functools.partial(pl.kernel, out_type=jax.ShapeDtypeStruct((K * T, W), y.dtype), mesh=_sc_mesh(),
                       scratch_types=[])
    def gather(y_hbm, i_hbm, o_hbm):
        def body(i_vmem, o_vmem):
            pltpu.sync_copy(y_hbm.at[i_vmem.at[0]], o_vmem)

        pltpu.emit_pipeline(
            body,
            grid=(K * T // win,),
            in_specs=[pl.BlockSpec((1, win), lambda j: (0, j))],
            out_specs=[pl.BlockSpec((win, W), lambda j: (j, 0), pipeline_mode=pl.Buffered(1))],
            core_axis_name=("c", "s"),
            dimension_semantics=(pltpu.PARALLEL,),
        )(i_hbm, o_hbm)

    return gather(y, dest.reshape(1, K * T)).reshape(K, T, W)


def _expert_kernel(first_ref, x_hbm, wi_ref, wo_ref, y_hbm, wi_b, wo_b, xbuf, ybuf, sem_in, sem_out):
    e = pl.program_id(0)
    n_in, blk = xbuf.shape[0], xbuf.shape[1]
    g0, g1 = first_ref[e], first_ref[e + 1]
    g_end = first_ref[pl.num_programs(0)]

    def x_copy(g):
        return pltpu.make_async_copy(x_hbm.at[pl.ds(g * blk, blk)], xbuf.at[g % n_in], sem_in.at[g % n_in])

    def y_copy(g):
        return pltpu.make_async_copy(ybuf.at[g % 2], y_hbm.at[pl.ds(g * blk, blk)], sem_out.at[g % 2])

    for ahead in range(n_in - 1):
        @pl.when(jnp.logical_and(e == 0, g_end > ahead))
        def _():
            x_copy(ahead).start()

    @pl.when(g1 > g0)
    def _():
        wi_b[...] = wi_ref[0, 0].astype(BF16)
        wo_b[...] = wo_ref[0, 0].astype(BF16)

    ff = wo_b.shape[0]

    def block(g, carry):
        x_copy(g).wait()

        @pl.when(g + n_in - 1 < g_end)
        def _():
            x_copy(g + n_in - 1).start()

        @pl.when(g >= 2)
        def _():
            y_copy(g - 2).wait()

        hcat = _dot_halves(xbuf[g % n_in], wi_b)
        act = (_silu(hcat[:, :ff]) * hcat[:, ff:]).astype(BF16)
        ybuf[g % 2] = _pack_halves(_dot(act, wo_b[...]))
        y_copy(g).start()
        return carry

    lax.fori_loop(g0, g1, block, 0)

    @pl.when(e == pl.num_programs(0) - 1)
    def _():
        @pl.when(g_end >= 2)
        def _():
            y_copy(g_end - 2).wait()

        @pl.when(g_end >= 1)
        def _():
            y_copy(g_end - 1).wait()


def _experts(xs, first_block, w_in, w_out, layer):
    P, half = xs.shape
    n_exp, D, ff2 = w_in.shape[-3:]
    return pl.pallas_call(
        _expert_kernel,
        grid_spec=pltpu.PrefetchScalarGridSpec(
            num_scalar_prefetch=1,
            grid=(n_exp,),
            in_specs=[pl.BlockSpec(memory_space=pl.ANY),
                      pl.BlockSpec((1, 1, D, ff2), lambda e, fb: (layer, e, 0, 0)),
                      pl.BlockSpec((1, 1, ff2 // 2, D), lambda e, fb: (layer, e, 0, 0))],
            out_specs=pl.BlockSpec(memory_space=pl.ANY),
            scratch_shapes=[pltpu.VMEM((D, ff2), BF16), pltpu.VMEM((ff2 // 2, D), BF16),
                            pltpu.VMEM((EXPERT_IN_SLOTS, MOE_BLOCK, half), I32), pltpu.VMEM((2, MOE_BLOCK, half), I32),
                            pltpu.SemaphoreType.DMA((EXPERT_IN_SLOTS,)), pltpu.SemaphoreType.DMA((2,))]),
        out_shape=jax.ShapeDtypeStruct((P, half), I32),
        compiler_params=_cparams(("arbitrary",)),
    )(first_block, xs, w_in, w_out)


def _combine_kernel(yg_ref, gw_ref, u_ref, wsi_ref, wso_ref, h_ref, mod_ref, g_ref, b_ref, o_ref):
    ff = wso_ref.shape[0]
    hcat = _dot_halves(u_ref[...], wsi_ref)
    shared = _dot((_silu(hcat[:, :ff]) * hcat[:, ff:]).astype(BF16), wso_ref[...])
    lo, hi = None, None
    for k in range(TOP_K):
        y_lo, y_hi = _unpack_halves(yg_ref[k])
        gk = gw_ref[:, k:k + 1]
        lo = y_lo * gk if lo is None else lo + y_lo * gk
        hi = y_hi * gk if hi is None else hi + y_hi * gk
    routed = jnp.concatenate([lo, hi], axis=1)
    z = DEEPNORM_ALPHA * h_ref[...] + mod_ref[0] * (routed + shared)
    o_ref[...] = _layer_norm(z, g_ref[...], b_ref[...])


def _combine(yg, gw_t, u, ws_in, ws_out, h, gate, gate_index, ln_g, ln_b):
    T, D = h.shape
    tm = ROW_TILE
    vec_spec = pl.BlockSpec((1, D), lambda i: (0, 0))
    row_spec = pl.BlockSpec((tm, D), lambda i: (i, 0))
    packed_spec = pl.BlockSpec((tm, D // 2), lambda i: (i, 0))
    return pl.pallas_call(
        _combine_kernel,
        grid=(T // tm,),
        in_specs=[pl.BlockSpec((TOP_K, tm, D // 2), lambda i: (0, i, 0)),
                  pl.BlockSpec((tm, TOP_K), lambda i: (i, 0)),
                  packed_spec,
                  pl.BlockSpec(ws_in.shape, lambda i: (0, 0)),
                  pl.BlockSpec(ws_out.shape, lambda i: (0, 0)),
                  row_spec,
                  pl.BlockSpec((1,) + gate.shape[1:], lambda i: (gate_index(i), 0, 0)),
                  vec_spec, vec_spec],
        out_specs=row_spec,
        out_shape=jax.ShapeDtypeStruct((T, D), F32),
        compiler_params=_cparams(("arbitrary",)),
    )(yg, gw_t, u, ws_in.astype(BF16), ws_out.astype(BF16), h, gate,
      ln_g.reshape(1, D), ln_b.reshape(1, D))


def _slots_kernel(e_ref, rank_ref, start_ref, dest_ref):
    io_e = lax.broadcasted_iota(I32, (N_EXPERTS, e_ref.shape[1]), 0)
    rows = [jnp.sum(jnp.where(io_e == e_ref[k:k + 1, :], start_ref[...], 0), axis=0, keepdims=True)
            for k in range(TOP_K)]
    dest_ref[...] = jnp.concatenate(rows, axis=0) + rank_ref[...]


def _slots(eidx, rank, pstart):
    T = eidx.shape[1]
    tm = ROW_TILE
    tok_spec = pl.BlockSpec((TOP_K, tm), lambda i: (0, i))
    return pl.pallas_call(
        _slots_kernel,
        grid=(T // tm,),
        in_specs=[tok_spec, tok_spec, pl.BlockSpec((N_EXPERTS, 1), lambda i: (0, 0))],
        out_specs=tok_spec,
        out_shape=jax.ShapeDtypeStruct((TOP_K, T), I32),
        compiler_params=_cparams(("arbitrary",)),
    )(eidx, rank, pstart.reshape(N_EXPERTS, 1))


def _moe_layer(u, h, gate, gate_index, router, bias, w_in, w_out, ws_in, ws_out, ln_g, ln_b, layer):
    T = u.shape[0]
    eidx, gw, rank, counts = _router(u, router, bias)
    counts = counts[:, 0]
    padded = (counts + MOE_BLOCK - 1) // MOE_BLOCK * MOE_BLOCK
    pend = jnp.cumsum(padded)
    pstart = (pend - padded).astype(I32)
    dest = _slots(eidx, rank, pstart)
    n_blocks = -(-(T * TOP_K + N_EXPERTS * (MOE_BLOCK - 1)) // MOE_BLOCK)
    first_block = jnp.concatenate([jnp.zeros((1,), I32), (pend // MOE_BLOCK).astype(I32)])
    xs = _sc_scatter_rows(u, dest, n_blocks * MOE_BLOCK)
    y = _experts(xs, first_block, w_in, w_out, layer)
    return _combine(_sc_gather_rows(y, dest), gw.T, u, ws_in, ws_out, h, gate, gate_index, ln_g, ln_b)


def _seg_ones(width=LANES):
    idx = np.arange(width) // HEAD_DIM
    return jnp.asarray((idx[:, None] == idx[None, :]).astype(np.float32), BF16)


def _head_sum(x, ones_ref):
    outs = []
    for j in range(x.shape[1] // LANES):
        xc = x[:, j * LANES:(j + 1) * LANES]
        hi = xc.astype(BF16)
        lo = (xc - hi.astype(F32)).astype(BF16)
        outs.append(_dot(hi, ones_ref[...]) + _dot(lo, ones_ref[...]))
    return jnp.concatenate(outs, axis=1)


def _rwkv_proj_kernel(seg_tiles, h_ref, hp_ref, hn_ref, mod_ref, mu_ref, wrkv_ref, g1_ref, g2_ref, d1_ref, d2_ref,
                      d0_ref, i1_ref, i2_ref, i0_ref, kk_ref, ka_ref, rk_ref, ones_ref,
                      r_ref, v_ref, a_ref, g_ref, bonus_ref, w_ref, k_ref, b_ref):
    i = pl.program_id(0)
    nb = hp_ref.shape[0]
    shift, scale = mod_ref[0, 0], mod_ref[0, 1]
    u = h_ref[...] * (1.0 + scale) + shift
    starts = jnp.logical_or(i == 0, i == seg_tiles)
    ends = jnp.logical_or(i == seg_tiles - 1, i == pl.num_programs(0) - 1)
    u_before = (hp_ref[...] * (1.0 + scale[:nb]) + shift[:nb]) * jnp.where(starts, 0.0, 1.0)
    u_after = (hn_ref[...] * (1.0 + scale[:nb]) + shift[:nb]) * jnp.where(ends, 0.0, 1.0)
    dx = 0.5 * (jnp.concatenate([u_before, u[:-nb]], axis=0) + jnp.concatenate([u[nb:], u_after], axis=0)) - u
    mix = lambda m: (u + dx * mu_ref[m:m + 1, :])
    xr, xw, xk, xv, xa, xg = [mix(m) for m in range(6)]
    r = _dot(xr.astype(BF16), wrkv_ref[0])
    k = _dot(xk.astype(BF16), wrkv_ref[1])
    v = _dot(xv.astype(BF16), wrkv_ref[2])
    g = _dot(jax.nn.sigmoid(_dot(xg.astype(BF16), g1_ref[...])).astype(BF16), g2_ref[...])
    kk = k * kk_ref[...]
    kk = kk * lax.rsqrt(jnp.maximum(_head_sum(kk * kk, ones_ref), 1e-24))
    r_ref[...] = r
    v_ref[...] = v
    a_ref[...] = -kk
    g_ref[...] = g
    k_sum = None
    xw_b = xw.astype(BF16)
    xa_b = xa.astype(BF16)
    for d in range(2):
        lw = d0_ref[d:d + 1, :] + _dot(jnp.tanh(_dot(xw_b, d1_ref[d])).astype(BF16), d2_ref[d])
        softplus = jnp.maximum(-lw, 0.0) + jnp.log(1.0 + jnp.exp(-jnp.abs(lw)))
        logw = -softplus - 0.5
        w_ref[d] = jnp.exp(-jnp.exp(logw))
        eta = jax.nn.sigmoid(i0_ref[d:d + 1, :] + _dot(_dot(xa_b, i1_ref[d]).astype(BF16), i2_ref[d]))
        k_d = k * (1.0 + (eta - 1.0) * ka_ref[...])
        k_ref[d] = k_d
        b_ref[d] = kk * eta
        k_sum = k_d if k_sum is None else k_sum + k_d
    bonus_ref[...] = _head_sum(r * k_sum * rk_ref[...], ones_ref) * v


def _rwkv_proj(h, mod_rows, batch, n_ctx, p):
    T, D = h.shape
    tm = PROJ_TILE
    per_tile = tm // batch
    seg_tiles = n_ctx // per_tile
    n_steps = T // batch
    row = pl.BlockSpec((tm, D), lambda i: (i, 0))
    before = pl.BlockSpec((batch, D), lambda i: (jnp.maximum(i * per_tile - 1, 0), 0))
    after = pl.BlockSpec((batch, D), lambda i: (jnp.minimum((i + 1) * per_tile, n_steps - 1), 0))
    mod_spec = pl.BlockSpec((1, 2, tm, D), lambda i: (jnp.minimum(i // seg_tiles, 1), 0, 0, 0))
    row2 = pl.BlockSpec((2, tm, D), lambda i: (0, i, 0))
    full = lambda a: pl.BlockSpec(a.shape, lambda i: (0,) * a.ndim)
    bf = lambda a: a.astype(BF16)
    consts = [p['mu'], bf(p['w_rkv']), bf(p['gate1']), bf(p['gate2']), bf(p['dec1']), bf(p['dec2']), p['dec0'],
              bf(p['icl1']), bf(p['icl2']), p['icl0'], p['k_k'].reshape(1, D), p['k_a'].reshape(1, D),
              p['r_k'].reshape(1, D), _seg_ones()]
    one = jax.ShapeDtypeStruct((T, D), F32)
    two = jax.ShapeDtypeStruct((2, T, D), F32)
    return pl.pallas_call(
        functools.partial(_rwkv_proj_kernel, seg_tiles),
        grid=(T // tm,),
        in_specs=[row, before, after, mod_spec] + [full(a) for a in consts],
        out_specs=[row, row, row, row, row, row2, row2, row2],
        out_shape=[one, one, one, one, one, two, two, two],
        compiler_params=_cparams(("arbitrary",)),
    )(h, h, h, mod_rows, *consts)


def _scan_kernel(r_ref, w_ref, k_ref, v_ref, a_ref, b_ref, ones_ref, hsel_ref,
                 y_ref, s_ref, vt_ref):
    d = pl.program_id(0)
    c = pl.program_id(2)
    tc, nb = r_ref.shape[0], r_ref.shape[1]
    tw = SCAN_TILE
    n_wide = r_ref.shape[2] // tw
    heads = tw // HEAD_DIM
    assert heads * tc == tw

    @pl.when(c == 0)
    def _():
        s_ref[...] = jnp.zeros_like(s_ref)

    for bb in range(nb):
        for q in range(n_wide):
            vt = v_ref[:, bb, q * tw:(q + 1) * tw].T
            vt_ref[bb * n_wide + q] = jnp.concatenate(
                [vt[h * HEAD_DIM:(h + 1) * HEAD_DIM] for h in range(heads)], axis=1)

    head_base = (lax.broadcasted_iota(I32, (HEAD_DIM, LANES), 1) // HEAD_DIM) * tc
    tiles = [(bb, q) for bb in range(nb) for q in range(n_wide)]
    groups = [tiles[i:i + SCAN_GROUP] for i in range(0, len(tiles), SCAN_GROUP)]

    def stacked(grp, get, dtype=F32):
        def wide(bb, q):
            return jnp.concatenate(
                [jnp.broadcast_to(get(bb, slice(q * tw + hf * LANES, q * tw + (hf + 1) * LANES)).astype(dtype),
                                  (HEAD_DIM, LANES)) for hf in range(tw // LANES)], axis=1)
        return jnp.concatenate([wide(bb, q) for bb, q in grp], axis=0)

    def load_state(grp):
        return jnp.concatenate([s_ref[bb * n_wide + q] for bb, q in grp], axis=0)

    def emit_y(grp, st_b, t_y):
        r_rows = stacked(grp, lambda bb, cols: r_ref[t_y, bb:bb + 1, cols], BF16)
        yh = _dot_nt(hsel_ref[...], st_b * r_rows)
        first = tiles.index(grp[0])
        y_ref[0, t_y, :, first * HEAD_DIM:(first + len(grp)) * HEAD_DIM] = yh[:heads]

    def step(s_i, carry):
        t = jnp.where(d == 0, s_i, tc - 1 - s_i)
        t_prev = jnp.where(s_i == 0, t, jnp.where(d == 0, t - 1, t + 1))
        pick = head_base + t
        for grp in groups:
            one = lambda ref: stacked(grp, lambda bb, cols: ref[t, bb:bb + 1, cols])
            two = lambda ref: stacked(grp, lambda bb, cols: ref[0, t, bb:bb + 1, cols])
            st = load_state(grp)
            st_b = st.astype(BF16)
            a_rows = stacked(grp, lambda bb, cols: a_ref[t, bb:bb + 1, cols], BF16)
            sa = _dot(st_b * a_rows, ones_ref[...])
            emit_y(grp, st_b, t_prev)
            vcol = jnp.concatenate(
                [jnp.concatenate([jnp.take_along_axis(vt_ref[bb * n_wide + q, :, hf * LANES:(hf + 1) * LANES],
                                                      pick, axis=1) for hf in range(tw // LANES)], axis=1)
                 for bb, q in grp], axis=0)
            st = st * two(w_ref) + sa * two(b_ref) + vcol * two(k_ref)
            for j, (bb, q) in enumerate(grp):
                s_ref[bb * n_wide + q] = st[j * HEAD_DIM:(j + 1) * HEAD_DIM]
        return carry

    lax.fori_loop(0, tc, step, 0, unroll=SCAN_UNROLL)
    t_last = jnp.where(d == 0, tc - 1, 0)
    for grp in groups:
        emit_y(grp, load_state(grp).astype(BF16), t_last)


def _wkv_scan(r, w, k, v, a, b, n_ctx):
    N, B, D = r.shape
    tc = SCAN_CHUNK
    wc = SCAN_COLS
    n_wide = wc // SCAN_TILE
    nc = N // tc
    ncc = n_ctx // tc

    def chunk(d, c):
        rev = jnp.where(c < ncc, ncc - 1 - c, nc - 1 - (c - ncc))
        return jnp.where(d == 0, c, rev)

    one = pl.BlockSpec((tc, B, wc), lambda d, g, c: (chunk(d, c), 0, g))
    two = pl.BlockSpec((1, tc, B, wc), lambda d, g, c: (d, chunk(d, c), 0, g))
    seg = np.arange(SCAN_TILE) // HEAD_DIM
    hsel = np.zeros((8, SCAN_TILE), np.float32)
    for hh in range(SCAN_TILE // HEAD_DIM):
        hsel[hh, seg == hh] = 1.0
    const = lambda a_: pl.BlockSpec(a_.shape, lambda d, g, c: (0, 0))
    consts = [_seg_ones(SCAN_TILE), jnp.asarray(hsel, BF16)]
    heads = SCAN_TILE // HEAD_DIM
    ncg = D // wc
    y = pl.pallas_call(
        _scan_kernel,
        grid=(2, ncg, nc),
        in_specs=[one, two, two, one, one, two] + [const(a_) for a_ in consts],
        out_specs=pl.BlockSpec((1, tc, heads, B * n_wide * HEAD_DIM), lambda d, g, c: (d, chunk(d, c), 0, g)),
        out_shape=jax.ShapeDtypeStruct((2, N, heads, ncg * B * n_wide * HEAD_DIM), F32),
        scratch_shapes=[pltpu.VMEM((B * n_wide, HEAD_DIM, SCAN_TILE), F32),
                        pltpu.VMEM((B * n_wide, HEAD_DIM, SCAN_TILE), F32)],
        compiler_params=_cparams(("arbitrary", "arbitrary", "arbitrary")),
    )(r, w, k, v, a, b, *consts)
    y = y.reshape(2, N, heads, ncg, B, n_wide * HEAD_DIM)
    return jnp.transpose(y, (0, 1, 4, 2, 3, 5)).reshape(2, N, B, D)


def _scan_head_order(d):
    heads = SCAN_TILE // HEAD_DIM
    n_wide = SCAN_COLS // SCAN_TILE
    ncg = d // SCAN_COLS
    order = []
    for g in range(ncg):
        for q in range(n_wide):
            for h in range(heads):
                order.append((h * ncg + g) * n_wide + q)
    return tuple(order)


def _rwkv_out_kernel(head_order, y0_ref, y1_ref, bonus_ref, g_ref, lnx_ref, ones_ref, w_ref, h_ref, mod_ref,
                     lg_ref, lb_ref, hn_ref, u_ref):
    y_in = y0_ref[0] + y1_ref[0]
    y = jnp.concatenate([y_in[:, p * HEAD_DIM:(p + 1) * HEAD_DIM] for p in head_order], axis=1)
    ym = _head_sum(y, ones_ref) * (1.0 / HEAD_DIM)
    yc = y - ym
    yv = _head_sum(yc * yc, ones_ref) * (1.0 / HEAD_DIM)
    yn = yc * lax.rsqrt(yv + LNX_EPS) * lnx_ref[0:1, :] + lnx_ref[1:2, :]
    x = ((yn + bonus_ref[...]) * g_ref[...]).astype(BF16)
    o = _dot(x, w_ref[...])
    z = DEEPNORM_ALPHA * h_ref[...] + mod_ref[0] * o
    hn = _layer_norm(z, lg_ref[...], lb_ref[...])
    hn_ref[...] = hn
    u_ref[...] = _pack_halves(hn * (1.0 + mod_ref[2]) + mod_ref[1])


def _rwkv_out(y, bonus, g, lnx, w_out, h, mod_rows, ln_g, ln_b, row0):
    T, D = h.shape
    tm = ROW_TILE
    t0 = row0 // tm
    off = pl.BlockSpec((tm, D), lambda i: (i + t0, 0))
    out = pl.BlockSpec((tm, D), lambda i: (i, 0))
    full = lambda a: pl.BlockSpec(a.shape, lambda i: (0,) * a.ndim)
    vec = pl.BlockSpec((1, D), lambda i: (0, 0))
    ones = _seg_ones()
    w_b = w_out.astype(BF16)
    return pl.pallas_call(
        functools.partial(_rwkv_out_kernel, _scan_head_order(D)),
        grid=((T - row0) // tm,),
        in_specs=[pl.BlockSpec((1, tm, D), lambda i: (0, i + t0, 0)),
                  pl.BlockSpec((1, tm, D), lambda i: (1, i + t0, 0)),
                  off, off, full(lnx), full(ones), full(w_b), off, full(mod_rows), vec, vec],
        out_specs=[out, pl.BlockSpec((tm, D // 2), lambda i: (i, 0))],
        out_shape=[jax.ShapeDtypeStruct((T - row0, D), F32), jax.ShapeDtypeStruct((T - row0, D // 2), I32)],
        compiler_params=_cparams(("arbitrary",)),
    )(y, y, bonus, g, lnx, ones, w_b, h, mod_rows, ln_g.reshape(1, D), ln_b.reshape(1, D))


def kernel(x, c, ctx, c_ctx, ada_w, ada_b, post_ln_g, post_ln_b, att_w_in, att_w_out, att_sink, diff_lambda_vecs, diff_subln_g, rk_mu, rk_w_rkv, rk_w_out, rk_decay0, rk_decay1, rk_decay2, rk_iclr0, rk_iclr1, rk_iclr2, rk_gate1, rk_gate2, rk_k_k, rk_k_a, rk_r_k, rk_lnx, moe_router, moe_bias, moe_w_in, moe_w_out, moe_ws_in, moe_ws_out):
    B, S, D = x.shape
    L = ctx.shape[1]
    N = L + S
    tm = ROW_TILE
    assert L % tm == 0 and S % tm == 0 and L % SCAN_CHUNK == 0 and S % SCAN_CHUNK == 0
    assert tm % B == 0 and PROJ_TILE % B == 0 and L % (PROJ_TILE // B) == 0 and D % SCAN_COLS == 0

    rows = -(-(B + 1) // 8) * 8
    cvec = jnp.concatenate([c, c_ctx[None, :], jnp.zeros((rows - B - 1, D), F32)], axis=0)
    mods = [_mod_table(_ada_mod(cvec, ada_w[i], ada_b[i]), B, D) for i in range(DEPTH)]

    h0 = jnp.concatenate([ctx, x], axis=1)
    lam_init = 0.8 - 0.6 * math.exp(-0.3 * 0)
    qa, ka, va, qb, kb, vb = _attn_inproj(h0, mods[0], att_w_in[0], L)
    oa = _win_attn(qa, ka, va, att_sink[0], L)
    ob = _diff_attn(qb, kb, vb, diff_lambda_vecs[0], diff_subln_g[0], lam_init, L)
    h1, u1 = _mix_out([oa, ob], [att_w_out[0][:A_WIDTH], att_w_out[0][A_WIDTH:]], h0, mods[0],
                      post_ln_g[0, 0], post_ln_b[0, 0], L, 0)
    tiles_b, tiles_c = N // tm, L // tm
    gate0 = mods[0][:, :, 5].reshape(B * 2, 1, D)
    gate0_index = lambda i: (i // tiles_b) * 2 + jnp.minimum((i % tiles_b) // tiles_c, 1)
    h2 = _moe_layer(u1.reshape(B * N, D // 2), h1.reshape(B * N, D), gate0, gate0_index, moe_router[0], moe_bias[0],
                    moe_w_in, moe_w_out, moe_ws_in[0], moe_ws_out[0],
                    post_ln_g[0, 1], post_ln_b[0, 1], 0).reshape(B, N, D)

    m_ctx, m_lat = mods[1][:, 0], mods[1][:, 1]
    h2_t = jnp.swapaxes(h2, 0, 1).reshape(N * B, D)
    rows_of = lambda m, j, n: jnp.tile(m[:, j], (n // B, 1))
    proj_mod = jnp.stack([jnp.stack([rows_of(m, 0, PROJ_TILE), rows_of(m, 1, PROJ_TILE)]) for m in (m_ctx, m_lat)])
    params = dict(mu=rk_mu[0], w_rkv=rk_w_rkv[0], gate1=rk_gate1[0], gate2=rk_gate2[0],
                  dec0=rk_decay0[0], dec1=rk_decay1[0], dec2=rk_decay2[0],
                  icl0=rk_iclr0[0], icl1=rk_iclr1[0], icl2=rk_iclr2[0],
                  k_k=rk_k_k[0], k_a=rk_k_a[0], r_k=rk_r_k[0])
    r, v, a, g, bonus, w2, k2, b2 = _rwkv_proj(h2_t, proj_mod, B, L, params)
    tmaj = lambda t: t.reshape(t.shape[:-2] + (N, B, D))
    y = _wkv_scan(tmaj(r), tmaj(w2), tmaj(k2), tmaj(v), tmaj(a), tmaj(b2), L)
    lat_rows = lambda j: rows_of(m_lat, j, tm)
    h3, u3 = _rwkv_out(y.reshape(2, N * B, D), bonus, g, rk_lnx[0], rk_w_out[0], h2_t,
                       jnp.stack([lat_rows(2), lat_rows(3), lat_rows(4)]),
                       post_ln_g[1, 0], post_ln_b[1, 0], L * B)
    out = _moe_layer(u3, h3, lat_rows(5)[None], lambda i: 0, moe_router[1], moe_bias[1],
                     moe_w_in, moe_w_out, moe_ws_in[1], moe_ws_out[1],
                     post_ln_g[1, 1], post_ln_b[1, 1], 1)
    return jnp.swapaxes(out.reshape(S, B, D), 0, 1)
```

```python
import functools
import math

import numpy as np
import jax
import jax.numpy as jnp
from jax import lax
from jax.experimental import pallas as pl
from jax.experimental.pallas import tpu as pltpu
from jax.experimental.pallas import tpu_sc as plsc

F32 = jnp.float32
BF16 = jnp.bfloat16
I32 = jnp.int32

HEAD_DIM = 64
GRID_W = 64
ROPE_AXIS_DIM = HEAD_DIM // 2
ROPE_THETA = 10000.0
Q_BLOCK = 128
A_Q_HEADS = 8
A_KV_HEADS = 2
A_GROUP = A_Q_HEADS // A_KV_HEADS
A_WIDTH = A_Q_HEADS * HEAD_DIM
A_KV_WIDTH = A_KV_HEADS * HEAD_DIM
B_HEADS = 4
B_V_DIM = 2 * HEAD_DIM
B_WIDTH = B_HEADS * B_V_DIM
LNX_EPS = 64e-5
N_EXPERTS = 256
TOP_K = 8
N_GROUPS = 8
TOPK_GROUPS = 4
ROUTED_SCALE = 2.5
MOE_BLOCK = 256
EXPERT_IN_SLOTS = 6
EXPERT_OUT_SLOTS = 4
LN_EPS = 1e-5
SUBLN_EPS = 1e-5
NEG_INF = -1e30
DEPTH = 2
DEEPNORM_ALPHA = (2 * DEPTH) ** 0.25

LANES = 128
ROW_TILE = 256
PROJ_TILE = 128
SC_WINDOW = 128
SCAN_CHUNK = 64
SCAN_COLS = 1024
SCAN_TILE = 256
SCAN_GROUP = 8
SCAN_UNROLL = 4
VMEM_LIMIT = 56 * 1024 * 1024


def _cparams(sem):
    return pltpu.CompilerParams(dimension_semantics=sem, vmem_limit_bytes=VMEM_LIMIT)


def _silu(x):
    return x * jax.nn.sigmoid(x)


def _layer_norm(z, g, b):
    mu = jnp.mean(z, -1, keepdims=True)
    zc = z - mu
    var = jnp.mean(zc * zc, -1, keepdims=True)
    return zc * lax.rsqrt(var + LN_EPS) * g + b


def _dot(a, b):
    return jnp.dot(a, b, preferred_element_type=F32)


def _dot_nt(a, b):
    return lax.dot_general(a, b, (((1,), (1,)), ((), ())), preferred_element_type=F32)


def _pack_halves(x):
    half = x.shape[1] // 2
    bits = lambda v: lax.bitcast_convert_type(v.astype(BF16).astype(F32), I32)
    return lax.shift_right_logical(bits(x[:, :half]), 16) | bits(x[:, half:])


def _unpack_halves(p):
    lo = lax.bitcast_convert_type(lax.shift_left(p, 16), F32)
    hi = lax.bitcast_convert_type(p & jnp.int32(-65536), F32)
    return lo, hi


def _dot_halves(p, w_ref_or_array):
    lo, hi = _unpack_halves(p)
    half = p.shape[1]
    return _dot(lo.astype(BF16), w_ref_or_array[:half]) + _dot(hi.astype(BF16), w_ref_or_array[half:])


def _ada_kernel(c_ref, w_ref, b_ref, o_ref):
    c = c_ref[...]
    o_ref[...] = _dot(_silu(c).astype(BF16), w_ref[...].astype(BF16)) + b_ref[...]


def _ada_mod(cvec, w, bias):
    R, D = cvec.shape
    n_out = w.shape[1]
    tn = 768
    return pl.pallas_call(
        _ada_kernel,
        grid=(n_out // tn,),
        in_specs=[pl.BlockSpec((R, D), lambda j: (0, 0)),
                  pl.BlockSpec((D, tn), lambda j: (0, j)),
                  pl.BlockSpec((1, tn), lambda j: (0, j))],
        out_specs=pl.BlockSpec((R, tn), lambda j: (0, j)),
        out_shape=jax.ShapeDtypeStruct((R, n_out), F32),
        compiler_params=_cparams(("arbitrary",)),
    )(cvec, w, bias.reshape(1, n_out))


def _mod_table(m, batch, d):
    m_lat = m[:batch].reshape(batch, 6, d)
    m_ctx = jnp.broadcast_to(m[batch].reshape(1, 6, d), (batch, 6, d))
    return jnp.stack([m_ctx, m_lat], axis=1)


def _mod_spec(d, ctx_tiles):
    return pl.BlockSpec((1, 1, 6, d), lambda b, i: (b, jnp.minimum(i // ctx_tiles, 1), 0, 0))


def _rope_tables(n_ctx, n_lat):
    rows = n_lat // GRID_W
    row = np.repeat(np.arange(rows), GRID_W).astype(np.float32)
    col = np.tile(np.arange(GRID_W), rows).astype(np.float32)
    inv = (ROPE_THETA ** (-np.arange(0, ROPE_AXIS_DIM, 2, dtype=np.float32) / ROPE_AXIS_DIM)).astype(np.float32)
    ar = row[:, None] * inv
    ac = col[:, None] * inv
    ang = np.concatenate([ar, ar, ac, ac], -1)
    cos = np.cos(ang).astype(np.float32)
    sin = np.sin(ang).astype(np.float32)
    lower = (np.arange(HEAD_DIM) % ROPE_AXIS_DIM) < (ROPE_AXIS_DIM // 2)
    sin_up = np.where(lower[None, :], -sin, 0.0)
    sin_dn = np.where(lower[None, :], 0.0, sin)

    def full(t, ctx_fill):
        t = np.concatenate([np.full((n_ctx, HEAD_DIM), ctx_fill, np.float32), t], 0)
        return jnp.asarray(np.tile(t, (1, LANES // HEAD_DIM)))

    return full(cos, 1.0), full(sin_up, 0.0), full(sin_dn, 0.0)


def _inproj_kernel(h_ref, mod_ref, w_ref, cos_ref, su_ref, sd_ref,
                   qa_ref, ka_ref, va_ref, qb_ref, kb_ref, vb_ref):
    h = h_ref[0]
    shift = mod_ref[0, 0, 0:1, :]
    scale = mod_ref[0, 0, 1:2, :]
    u = (h * (1.0 + scale) + shift).astype(BF16)
    y = _dot(u, w_ref[...])
    cos, s_up, s_dn = cos_ref[...], su_ref[...], sd_ref[...]
    q_scale = HEAD_DIM ** -0.5

    def rope(xc):
        half = ROPE_AXIS_DIM // 2
        return xc * cos + pltpu.roll(xc, LANES - half, 1) * s_up + pltpu.roll(xc, half, 1) * s_dn

    def emit(out_ref, col0, width, roped, mul):
        for j in range(width // LANES):
            xc = y[:, col0 + j * LANES: col0 + (j + 1) * LANES]
            if roped:
                xc = rope(xc)
            if mul != 1.0:
                xc = xc * mul
            out_ref[0, :, j * LANES:(j + 1) * LANES] = xc.astype(out_ref.dtype)

    c = 0
    emit(qa_ref, c, A_WIDTH, True, q_scale); c += A_WIDTH
    emit(ka_ref, c, A_KV_WIDTH, True, 1.0); c += A_KV_WIDTH
    emit(va_ref, c, A_KV_WIDTH, False, 1.0); c += A_KV_WIDTH
    emit(qb_ref, c, B_WIDTH, True, q_scale); c += B_WIDTH
    emit(kb_ref, c, B_WIDTH, True, 1.0); c += B_WIDTH
    emit(vb_ref, c, B_WIDTH, False, 1.0)


def _attn_inproj(h, mod, w_in, n_ctx):
    B, N, D = h.shape
    tm = ROW_TILE
    cos, s_up, s_dn = _rope_tables(n_ctx, N - n_ctx)
    widths = (A_WIDTH, A_KV_WIDTH, A_KV_WIDTH, B_WIDTH, B_WIDTH, B_WIDTH)
    tab_spec = pl.BlockSpec((tm, LANES), lambda b, i: (i, 0))
    return pl.pallas_call(
        _inproj_kernel,
        grid=(B, N // tm),
        in_specs=[pl.BlockSpec((1, tm, D), lambda b, i: (b, i, 0)),
                  _mod_spec(D, n_ctx // tm),
                  pl.BlockSpec(w_in.shape, lambda b, i: (0, 0)),
                  tab_spec, tab_spec, tab_spec],
        out_specs=[pl.BlockSpec((1, tm, w), lambda b, i: (b, i, 0)) for w in widths],
        out_shape=[jax.ShapeDtypeStruct((B, N, w), BF16) for w in widths],
        compiler_params=_cparams(("arbitrary", "arbitrary")),
    )(h, mod, w_in.astype(BF16), cos, s_up, s_dn)


def _win_attn_kernel(n_ctx_blocks, n_blocks, q_ref, kc_ref, vc_ref, kl_ref, km_ref, kr_ref,
                     vl_ref, vm_ref, vr_ref, sink_ref, o_ref):
    j = pl.program_id(1)
    is_lat = j >= n_ctx_blocks
    qb = Q_BLOCK
    n_c = kc_ref.shape[1]
    rows = A_GROUP * qb
    n_keys = n_c + 3 * qb
    far = 1 << 20
    r_idx = lax.broadcasted_iota(I32, (rows, n_keys), 0) % qb
    cw = lax.broadcasted_iota(I32, (rows, n_keys), 1) - n_c
    off_l = jnp.where(jnp.logical_and(is_lat, j > n_ctx_blocks), 0, far)
    end_m = jnp.where(is_lat, 2 * qb, qb)
    off_r = jnp.where(jnp.logical_and(is_lat, j < n_blocks - 1), 0, far)
    valid = ((cw < 0)
             | ((cw >= 0) & (cw < qb) & (cw >= r_idx + off_l))
             | ((cw >= qb) & (cw < end_m))
             | ((cw >= 2 * qb) & (cw - 2 * qb + off_r <= r_idx)))
    outs = []
    for kv in range(A_KV_HEADS):
        cols = slice(kv * HEAD_DIM, (kv + 1) * HEAD_DIM)
        k_all = jnp.concatenate([kc_ref[0, :, cols], kl_ref[0, :, cols], km_ref[0, :, cols],
                                 kr_ref[0, :, cols]], axis=0)
        v_all = jnp.concatenate([vc_ref[0, :, cols], vl_ref[0, :, cols], vm_ref[0, :, cols],
                                 vr_ref[0, :, cols]], axis=0)
        q0 = kv * A_GROUP
        q = jnp.concatenate([q_ref[0, :, (q0 + g) * HEAD_DIM:(q0 + g + 1) * HEAD_DIM]
                             for g in range(A_GROUP)], axis=0)
        sink = jnp.concatenate([jnp.broadcast_to(sink_ref[q0 + g:q0 + g + 1, 0:1], (qb, 1))
                                for g in range(A_GROUP)], axis=0)
        s = jnp.where(valid, _dot_nt(q, k_all), NEG_INF)
        m = jnp.maximum(jnp.max(s, -1, keepdims=True), sink)
        e = jnp.exp(s - m)
        denom = jnp.sum(e, -1, keepdims=True) + jnp.exp(sink - m)
        o = _dot(e.astype(BF16), v_all) * (1.0 / denom)
        outs += [o[g * qb:(g + 1) * qb] for g in range(A_GROUP)]
    for j2 in range(A_Q_HEADS // 2):
        pair = jnp.concatenate([outs[2 * j2], outs[2 * j2 + 1]], axis=1)
        o_ref[0, :, j2 * LANES:(j2 + 1) * LANES] = pair.astype(o_ref.dtype)


def _win_attn(qa, ka, va, sink, n_ctx):
    B, N, _ = qa.shape
    qb = Q_BLOCK
    nb = N // qb
    ncb = n_ctx // qb
    sink_pad = jnp.broadcast_to(sink.reshape(A_Q_HEADS, 1).astype(F32), (A_Q_HEADS, LANES))

    def left(b, j):
        return (b, jnp.clip(j - 1, ncb, nb - 1), 0)

    def mid(b, j):
        return (b, jnp.clip(j, ncb, nb - 1), 0)

    def right(b, j):
        return (b, jnp.clip(j + 1, ncb, nb - 1), 0)

    kv_blk = lambda im: pl.BlockSpec((1, qb, A_KV_WIDTH), im)
    ctx_blk = pl.BlockSpec((1, n_ctx, A_KV_WIDTH), lambda b, j: (b, 0, 0))
    return pl.pallas_call(
        functools.partial(_win_attn_kernel, ncb, nb),
        grid=(B, nb),
        in_specs=[pl.BlockSpec((1, qb, A_WIDTH), lambda b, j: (b, j, 0)),
                  ctx_blk, ctx_blk,
                  kv_blk(left), kv_blk(mid), kv_blk(right),
                  kv_blk(left), kv_blk(mid), kv_blk(right),
                  pl.BlockSpec((A_Q_HEADS, LANES), lambda b, j: (0, 0))],
        out_specs=pl.BlockSpec((1, qb, A_WIDTH), lambda b, j: (b, j, 0)),
        out_shape=jax.ShapeDtypeStruct((B, N, A_WIDTH), BF16),
        compiler_params=_cparams(("arbitrary", "arbitrary")),
    )(qa, ka, va, ka, ka, ka, va, va, va, sink_pad)


def _diff_attn_kernel(n_ctx, lam_init, q_ref, k_ref, v_ref, lv_ref, g_ref, o_ref):
    j = pl.program_id(1)
    lv = lv_ref[...]
    lam = (jnp.exp(jnp.sum(lv[0:1] * lv[1:2], -1, keepdims=True))
           - jnp.exp(jnp.sum(lv[2:3] * lv[3:4], -1, keepdims=True)) + lam_init)
    gain = g_ref[...] * (1.0 - lam_init)

    def run(n_keys):
        for hd in range(B_HEADS):
            parts = []
            for mm in range(2):
                c0 = (hd * 2 + mm) * HEAD_DIM
                q = q_ref[0, :, c0:c0 + HEAD_DIM]
                k = k_ref[0, :n_keys, c0:c0 + HEAD_DIM]
                s = _dot_nt(q, k)
                e = jnp.exp(s - jnp.max(s, -1, keepdims=True))
                parts.append((e, jnp.sum(e, -1, keepdims=True)))
            (e0, l0), (e1, l1) = parts
            v = v_ref[0, :n_keys, hd * B_V_DIM:(hd + 1) * B_V_DIM]
            o = _dot(e0.astype(BF16), v) * (1.0 / l0) - _dot(e1.astype(BF16), v) * (lam / l1)
            o = o * lax.rsqrt(jnp.mean(o * o, -1, keepdims=True) + SUBLN_EPS) * gain
            o_ref[0, :, hd * B_V_DIM:(hd + 1) * B_V_DIM] = o.astype(o_ref.dtype)

    @pl.when(j == 0)
    def _():
        run(n_ctx)

    @pl.when(j > 0)
    def _():
        run(k_ref.shape[1])


def _diff_attn(qb, kb, vb, lam_vecs, subln_g, lam_init, n_ctx):
    B, N, _ = qb.shape
    tq = n_ctx
    return pl.pallas_call(
        functools.partial(_diff_attn_kernel, n_ctx, lam_init),
        grid=(B, N // tq),
        in_specs=[pl.BlockSpec((1, tq, B_WIDTH), lambda b, j: (b, j, 0)),
                  pl.BlockSpec((1, N, B_WIDTH), lambda b, j: (b, 0, 0)),
                  pl.BlockSpec((1, N, B_WIDTH), lambda b, j: (b, 0, 0)),
                  pl.BlockSpec((4, HEAD_DIM), lambda b, j: (0, 0)),
                  pl.BlockSpec((1, B_V_DIM), lambda b, j: (0, 0))],
        out_specs=pl.BlockSpec((1, tq, B_WIDTH), lambda b, j: (b, j, 0)),
        out_shape=jax.ShapeDtypeStruct((B, N, B_WIDTH), BF16),
        compiler_params=_cparams(("arbitrary", "arbitrary")),
    )(qb, kb, vb, lam_vecs.astype(F32), subln_g.reshape(1, B_V_DIM).astype(F32))


def _mix_out_kernel(n_in, *refs):
    xs = refs[:n_in]
    ws = refs[n_in:2 * n_in]
    h_ref, mod_ref, g_ref, b_ref, hn_ref, u_ref = refs[2 * n_in:]
    o = _dot(xs[0][0], ws[0][...])
    for x_ref, w_ref in zip(xs[1:], ws[1:]):
        o = o + _dot(x_ref[0], w_ref[...])
    z = DEEPNORM_ALPHA * h_ref[0] + mod_ref[0, 0, 2:3, :] * o
    hn = _layer_norm(z, g_ref[...], b_ref[...])
    hn_ref[0] = hn
    u_ref[0] = _pack_halves(hn * (1.0 + mod_ref[0, 0, 4:5, :]) + mod_ref[0, 0, 3:4, :])


def _mix_out(xs, ws, h, mod, ln_g, ln_b, n_ctx, row0):
    B, N, D = h.shape
    tm = ROW_TILE
    t0 = row0 // tm
    n_out = N - row0
    row_spec = lambda w: pl.BlockSpec((1, tm, w), lambda b, i: (b, i + t0, 0))
    out_spec = pl.BlockSpec((1, tm, D), lambda b, i: (b, i, 0))
    vec_spec = pl.BlockSpec((1, D), lambda b, i: (0, 0))
    return pl.pallas_call(
        functools.partial(_mix_out_kernel, len(xs)),
        grid=(B, n_out // tm),
        in_specs=([row_spec(x.shape[-1]) for x in xs]
                  + [pl.BlockSpec(w.shape, lambda b, i: (0, 0)) for w in ws]
                  + [row_spec(D),
                     pl.BlockSpec((1, 1, 6, D), lambda b, i: (b, jnp.minimum((i + t0) // (n_ctx // tm), 1), 0, 0)),
                     vec_spec, vec_spec]),
        out_specs=[out_spec, pl.BlockSpec((1, tm, D // 2), lambda b, i: (b, i, 0))],
        out_shape=[jax.ShapeDtypeStruct((B, n_out, D), F32), jax.ShapeDtypeStruct((B, n_out, D // 2), I32)],
        compiler_params=_cparams(("arbitrary", "arbitrary")),
    )(*xs, *[w.astype(BF16) for w in ws], h, mod, ln_g.reshape(1, D), ln_b.reshape(1, D))


def _router_kernel(u_ref, rt_ref, bias_ref, tri_ref, e_ref, gw_ref, rank_ref, cnt_ref, carry_ref):
    i = pl.program_id(0)

    @pl.when(i == 0)
    def _():
        carry_ref[...] = jnp.zeros_like(carry_ref)

    tm = u_ref.shape[0]
    per_group = N_EXPERTS // N_GROUPS
    neg = -jnp.inf
    u_lo, u_hi = _unpack_halves(u_ref[...])
    half = u_ref.shape[1]
    logits = (_dot_nt(rt_ref[:, :half], u_lo.astype(BF16))
              + _dot_nt(rt_ref[:, half:], u_hi.astype(BF16)))
    scores = jax.nn.sigmoid(logits)
    sel = scores + bias_ref[...]
    io_in = lax.broadcasted_iota(I32, (per_group, tm), 0)
    grp_rows = []
    for gi in range(N_GROUPS):
        sg = sel[gi * per_group:(gi + 1) * per_group]
        m1 = jnp.max(sg, axis=0, keepdims=True)
        i1 = jnp.min(jnp.where(sg == m1, io_in, per_group), axis=0, keepdims=True)
        m2 = jnp.max(jnp.where(io_in == i1, neg, sg), axis=0, keepdims=True)
        grp_rows.append(m1 + m2)
    grp = jnp.concatenate(grp_rows, axis=0)
    io_g = lax.broadcasted_iota(I32, grp.shape, 0)
    g_sel = jnp.zeros(grp.shape, F32)
    for _ in range(TOPK_GROUPS):
        m = jnp.max(grp, axis=0, keepdims=True)
        hit = io_g == jnp.min(jnp.where(grp == m, io_g, N_GROUPS), axis=0, keepdims=True)
        g_sel = jnp.where(hit, 1.0, g_sel)
        grp = jnp.where(hit, neg, grp)
    selm = jnp.concatenate(
        [jnp.where(g_sel[gi:gi + 1] > 0.5, sel[gi * per_group:(gi + 1) * per_group], NEG_INF)
         for gi in range(N_GROUPS)], axis=0)
    io_e = lax.broadcasted_iota(I32, selm.shape, 0)
    chosen_f = jnp.zeros(selm.shape, F32)
    idx, gws = [], []
    for _ in range(TOP_K):
        m = jnp.max(selm, axis=0, keepdims=True)
        ik = jnp.min(jnp.where(selm == m, io_e, N_EXPERTS), axis=0, keepdims=True)
        hit = io_e == ik
        idx.append(ik)
        gws.append(jnp.sum(jnp.where(hit, scores, 0.0), axis=0, keepdims=True))
        chosen_f = jnp.where(hit, 1.0, chosen_f)
        selm = jnp.where(hit, neg, selm)
    gw = jnp.concatenate(gws, axis=0)
    gw_ref[...] = gw / jnp.sum(gw, axis=0, keepdims=True) * ROUTED_SCALE
    e_ref[...] = jnp.concatenate(idx, axis=0)
    before = _dot(chosen_f.astype(BF16), tri_ref[...]) + carry_ref[...]
    ranks = [jnp.sum(jnp.where(io_e == ik, before, 0.0), axis=0, keepdims=True) for ik in idx]
    rank_ref[...] = jnp.concatenate(ranks, axis=0).astype(I32)
    carry_ref[...] = carry_ref[...] + jnp.sum(chosen_f, axis=1, keepdims=True)
    cnt_ref[...] = carry_ref[...].astype(I32)


def _router(u, router, bias):
    T = u.shape[0]
    D = router.shape[0]
    tm = ROW_TILE
    tri = jnp.asarray(np.triu(np.ones((tm, tm), np.float32), 1), BF16)
    tok_spec = pl.BlockSpec((TOP_K, tm), lambda i: (0, i))
    return pl.pallas_call(
        _router_kernel,
        grid=(T // tm,),
        in_specs=[pl.BlockSpec((tm, D // 2), lambda i: (i, 0)),
                  pl.BlockSpec((N_EXPERTS, D), lambda i: (0, 0)),
                  pl.BlockSpec((N_EXPERTS, 1), lambda i: (0, 0)),
                  pl.BlockSpec((tm, tm), lambda i: (0, 0))],
        out_specs=[tok_spec, tok_spec, tok_spec, pl.BlockSpec((N_EXPERTS, 1), lambda i: (0, 0))],
        out_shape=[jax.ShapeDtypeStruct((TOP_K, T), I32), jax.ShapeDtypeStruct((TOP_K, T), F32),
                   jax.ShapeDtypeStruct((TOP_K, T), I32), jax.ShapeDtypeStruct((N_EXPERTS, 1), I32)],
        scratch_shapes=[pltpu.VMEM((N_EXPERTS, 1), F32)],
        compiler_params=_cparams(("arbitrary",)),
    )(u, router.T.astype(BF16), bias.reshape(N_EXPERTS, 1).astype(F32), tri)


def _sc_mesh():
    return plsc.VectorSubcoreMesh(core_axis_name="c", subcore_axis_name="s")


def _sc_scatter_rows(x, dest, n_rows):
    T, W = x.shape
    K = dest.shape[0]
    win = SC_WINDOW
    n_win = T // win

    @functools.partial(pl.kernel, out_type=jax.ShapeDtypeStruct((n_rows, W), x.dtype), mesh=_sc_mesh(),
                       scratch_types=[])
    def scatter(x_hbm, i_hbm, o_hbm):
        def body(x_vmem, i_vmem):
            pltpu.sync_copy(x_vmem, o_hbm.at[i_vmem.at[0]])

        pltpu.emit_pipeline(
            body,
            grid=(K * n_win,),
            in_specs=[pl.BlockSpec((win, W), lambda j: (j % n_win, 0), pipeline_mode=pl.Buffered(1)),
                      pl.BlockSpec((1, win), lambda j: (0, j))],
            out_specs=[],
            core_axis_name=("c", "s"),
            dimension_semantics=(pltpu.PARALLEL,),
        )(x_hbm, i_hbm)

    return scatter(x, dest.reshape(1, K * T))


def _sc_gather_rows(y, dest):
    K, T = dest.shape
    W = y.shape[1]
    win = SC_WINDOW

    @functools.partial(pl.kernel, out_type=jax.ShapeDtypeStruct((K * T, W), y.dtype), mesh=_sc_mesh(),
                       scratch_types=[])
    def gather(y_hbm, i_hbm, o_hbm):
        def body(i_vmem, o_vmem):
            pltpu.sync_copy(y_hbm.at[i_vmem.at[0]], o_vmem)

        pltpu.emit_pipeline(
            body,
            grid=(K * T // win,),
            in_specs=[pl.BlockSpec((1, win), lambda j: (0, j))],
            out_specs=[pl.BlockSpec((win, W), lambda j: (j, 0), pipeline_mode=pl.Buffered(1))],
            core_axis_name=("c", "s"),
            dimension_semantics=(pltpu.PARALLEL,),
        )(i_hbm, o_hbm)

    return gather(y, dest.reshape(1, K * T)).reshape(K, T, W)


def _expert_kernel(first_ref, x_hbm, wi_ref, wo_ref, y_hbm, wi_b, wo_b, xbuf, ybuf, sem_in, sem_out):
    e = pl.program_id(0)
    n_in, blk = xbuf.shape[0], xbuf.shape[1]
    n_out = ybuf.shape[0]
    g0, g1 = first_ref[e], first_ref[e + 1]
    g_end = first_ref[pl.num_programs(0)]

    def x_copy(g):
        return pltpu.make_async_copy(x_hbm.at[pl.ds(g * blk, blk)], xbuf.at[g % n_in], sem_in.at[g % n_in])

    def y_copy(g):
        return pltpu.make_async_copy(ybuf.at[g % n_out], y_hbm.at[pl.ds(g * blk, blk)], sem_out.at[g % n_out])

    for ahead in range(n_in - 1):
        @pl.when(jnp.logical_and(e == 0, g_end > ahead))
        def _():
            x_copy(ahead).start()

    @pl.when(g1 > g0)
    def _():
        wi_b[...] = wi_ref[0, 0].astype(BF16)
        wo_b[...] = wo_ref[0, 0].astype(BF16)

    ff = wo_b.shape[0]

    def block(g, carry):
        x_copy(g).wait()

        @pl.when(g + n_in - 1 < g_end)
        def _():
            x_copy(g + n_in - 1).start()

        @pl.when(g >= n_out)
        def _():
            y_copy(g - n_out).wait()

        hcat = _dot_halves(xbuf[g % n_in], wi_b)
        act = (_silu(hcat[:, :ff]) * hcat[:, ff:]).astype(BF16)
        ybuf[g % n_out] = _pack_halves(_dot(act, wo_b[...]))
        y_copy(g).start()
        return carry

    lax.fori_loop(g0, g1, block, 0)

    @pl.when(e == pl.num_programs(0) - 1)
    def _():
        for back in range(n_out, 0, -1):
            @pl.when(g_end >= back)
            def _():
                y_copy(g_end - back).wait()


def _experts(xs, first_block, w_in, w_out, layer):
    P, half = xs.shape
    n_exp, D, ff2 = w_in.shape[-3:]
    return pl.pallas_call(
        _expert_kernel,
        grid_spec=pltpu.PrefetchScalarGridSpec(
            num_scalar_prefetch=1,
            grid=(n_exp,),
            in_specs=[pl.BlockSpec(memory_space=pl.ANY),
                      pl.BlockSpec((1, 1, D, ff2), lambda e, fb: (layer, e, 0, 0)),
                      pl.BlockSpec((1, 1, ff2 // 2, D), lambda e, fb: (layer, e, 0, 0))],
            out_specs=pl.BlockSpec(memory_space=pl.ANY),
            scratch_shapes=[pltpu.VMEM((D, ff2), BF16), pltpu.VMEM((ff2 // 2, D), BF16),
                            pltpu.VMEM((EXPERT_IN_SLOTS, MOE_BLOCK, half), I32),
                            pltpu.VMEM((EXPERT_OUT_SLOTS, MOE_BLOCK, half), I32),
                            pltpu.SemaphoreType.DMA((EXPERT_IN_SLOTS,)),
                            pltpu.SemaphoreType.DMA((EXPERT_OUT_SLOTS,))]),
        out_shape=jax.ShapeDtypeStruct((P, half), I32),
        compiler_params=_cparams(("arbitrary",)),
    )(first_block, xs, w_in, w_out)


def _combine_kernel(yg_ref, gw_ref, u_ref, wsi_ref, wso_ref, h_ref, mod_ref, g_ref, b_ref, o_ref):
    ff = wso_ref.shape[0]
    hcat = _dot_halves(u_ref[...], wsi_ref)
    shared = _dot((_silu(hcat[:, :ff]) * hcat[:, ff:]).astype(BF16), wso_ref[...])
    lo, hi = None, None
    for k in range(TOP_K):
        y_lo, y_hi = _unpack_halves(yg_ref[k])
        gk = gw_ref[:, k:k + 1]
        lo = y_lo * gk if lo is None else lo + y_lo * gk
        hi = y_hi * gk if hi is None else hi + y_hi * gk
    routed = jnp.concatenate([lo, hi], axis=1)
    z = DEEPNORM_ALPHA * h_ref[...] + mod_ref[0] * (routed + shared)
    o_ref[...] = _layer_norm(z, g_ref[...], b_ref[...])


def _combine(yg, gw_t, u, ws_in, ws_out, h, gate, gate_index, ln_g, ln_b):
    T, D = h.shape
    tm = ROW_TILE
    vec_spec = pl.BlockSpec((1, D), lambda i: (0, 0))
    row_spec = pl.BlockSpec((tm, D), lambda i: (i, 0))
    packed_spec = pl.BlockSpec((tm, D // 2), lambda i: (i, 0))
    return pl.pallas_call(
        _combine_kernel,
        grid=(T // tm,),
        in_specs=[pl.BlockSpec((TOP_K, tm, D // 2), lambda i: (0, i, 0)),
                  pl.BlockSpec((tm, TOP_K), lambda i: (i, 0)),
                  packed_spec,
                  pl.BlockSpec(ws_in.shape, lambda i: (0, 0)),
                  pl.BlockSpec(ws_out.shape, lambda i: (0, 0)),
                  row_spec,
                  pl.BlockSpec((1,) + gate.shape[1:], lambda i: (gate_index(i), 0, 0)),
                  vec_spec, vec_spec],
        out_specs=row_spec,
        out_shape=jax.ShapeDtypeStruct((T, D), F32),
        compiler_params=_cparams(("arbitrary",)),
    )(yg, gw_t, u, ws_in.astype(BF16), ws_out.astype(BF16), h, gate,
      ln_g.reshape(1, D), ln_b.reshape(1, D))


def _slots_kernel(e_ref, rank_ref, start_ref, dest_ref):
    io_e = lax.broadcasted_iota(I32, (N_EXPERTS, e_ref.shape[1]), 0)
    rows = [jnp.sum(jnp.where(io_e == e_ref[k:k + 1, :], start_ref[...], 0), axis=0, keepdims=True)
            for k in range(TOP_K)]
    dest_ref[...] = jnp.concatenate(rows, axis=0) + rank_ref[...]


def _slots(eidx, rank, pstart):
    T = eidx.shape[1]
    tm = ROW_TILE
    tok_spec = pl.BlockSpec((TOP_K, tm), lambda i: (0, i))
    return pl.pallas_call(
        _slots_kernel,
        grid=(T // tm,),
        in_specs=[tok_spec, tok_spec, pl.BlockSpec((N_EXPERTS, 1), lambda i: (0, 0))],
        out_specs=tok_spec,
        out_shape=jax.ShapeDtypeStruct((TOP_K, T), I32),
        compiler_params=_cparams(("arbitrary",)),
    )(eidx, rank, pstart.reshape(N_EXPERTS, 1))


def _moe_layer(u, h, gate, gate_index, router, bias, w_in, w_out, ws_in, ws_out, ln_g, ln_b, layer):
    T = u.shape[0]
    eidx, gw, rank, counts = _router(u, router, bias)
    counts = counts[:, 0]
    padded = (counts + MOE_BLOCK - 1) // MOE_BLOCK * MOE_BLOCK
    pend = jnp.cumsum(padded)
    pstart = (pend - padded).astype(I32)
    dest = _slots(eidx, rank, pstart)
    n_blocks = -(-(T * TOP_K + N_EXPERTS * (MOE_BLOCK - 1)) // MOE_BLOCK)
    first_block = jnp.concatenate([jnp.zeros((1,), I32), (pend // MOE_BLOCK).astype(I32)])
    xs = _sc_scatter_rows(u, dest, n_blocks * MOE_BLOCK)
    y = _experts(xs, first_block, w_in, w_out, layer)
    return _combine(_sc_gather_rows(y, dest), gw.T, u, ws_in, ws_out, h, gate, gate_index, ln_g, ln_b)


def _seg_ones(width=LANES):
    idx = np.arange(width) // HEAD_DIM
    return jnp.asarray((idx[:, None] == idx[None, :]).astype(np.float32), BF16)


def _head_sum(x, ones_ref):
    outs = []
    for j in range(x.shape[1] // LANES):
        xc = x[:, j * LANES:(j + 1) * LANES]
        hi = xc.astype(BF16)
        lo = (xc - hi.astype(F32)).astype(BF16)
        outs.append(_dot(hi, ones_ref[...]) + _dot(lo, ones_ref[...]))
    return jnp.concatenate(outs, axis=1)


def _rwkv_proj_kernel(seg_tiles, h_ref, hp_ref, hn_ref, mod_ref, mu_ref, wrkv_ref, g1_ref, g2_ref, d1_ref, d2_ref,
                      d0_ref, i1_ref, i2_ref, i0_ref, kk_ref, ka_ref, rk_ref, ones_ref,
                      r_ref, v_ref, a_ref, g_ref, bonus_ref, w_ref, k_ref, b_ref):
    i = pl.program_id(0)
    nb = hp_ref.shape[0]
    shift, scale = mod_ref[0, 0], mod_ref[0, 1]
    u = h_ref[...] * (1.0 + scale) + shift
    starts = jnp.logical_or(i == 0, i == seg_tiles)
    ends = jnp.logical_or(i == seg_tiles - 1, i == pl.num_programs(0) - 1)
    u_before = (hp_ref[...] * (1.0 + scale[:nb]) + shift[:nb]) * jnp.where(starts, 0.0, 1.0)
    u_after = (hn_ref[...] * (1.0 + scale[:nb]) + shift[:nb]) * jnp.where(ends, 0.0, 1.0)
    dx = 0.5 * (jnp.concatenate([u_before, u[:-nb]], axis=0) + jnp.concatenate([u[nb:], u_after], axis=0)) - u
    mix = lambda m: (u + dx * mu_ref[m:m + 1, :])
    xr, xw, xk, xv, xa, xg = [mix(m) for m in range(6)]
    r = _dot(xr.astype(BF16), wrkv_ref[0])
    k = _dot(xk.astype(BF16), wrkv_ref[1])
    v = _dot(xv.astype(BF16), wrkv_ref[2])
    g = _dot(jax.nn.sigmoid(_dot(xg.astype(BF16), g1_ref[...])).astype(BF16), g2_ref[...])
    kk = k * kk_ref[...]
    kk = kk * lax.rsqrt(jnp.maximum(_head_sum(kk * kk, ones_ref), 1e-24))
    r_ref[...] = r
    v_ref[...] = v
    a_ref[...] = -kk
    g_ref[...] = g
    k_sum = None
    xw_b = xw.astype(BF16)
    xa_b = xa.astype(BF16)
    for d in range(2):
        lw = d0_ref[d:d + 1, :] + _dot(jnp.tanh(_dot(xw_b, d1_ref[d])).astype(BF16), d2_ref[d])
        softplus = jnp.maximum(-lw, 0.0) + jnp.log(1.0 + jnp.exp(-jnp.abs(lw)))
        logw = -softplus - 0.5
        w_ref[d] = jnp.exp(-jnp.exp(logw))
        eta = jax.nn.sigmoid(i0_ref[d:d + 1, :] + _dot(_dot(xa_b, i1_ref[d]).astype(BF16), i2_ref[d]))
        k_d = k * (1.0 + (eta - 1.0) * ka_ref[...])
        k_ref[d] = k_d
        b_ref[d] = kk * eta
        k_sum = k_d if k_sum is None else k_sum + k_d
    bonus_ref[...] = _head_sum(r * k_sum * rk_ref[...], ones_ref) * v


def _rwkv_proj(h, mod_rows, batch, n_ctx, p):
    T, D = h.shape
    tm = PROJ_TILE
    per_tile = tm // batch
    seg_tiles = n_ctx // per_tile
    n_steps = T // batch
    row = pl.BlockSpec((tm, D), lambda i: (i, 0))
    before = pl.BlockSpec((batch, D), lambda i: (jnp.maximum(i * per_tile - 1, 0), 0))
    after = pl.BlockSpec((batch, D), lambda i: (jnp.minimum((i + 1) * per_tile, n_steps - 1), 0))
    mod_spec = pl.BlockSpec((1, 2, tm, D), lambda i: (jnp.minimum(i // seg_tiles, 1), 0, 0, 0))
    row2 = pl.BlockSpec((2, tm, D), lambda i: (0, i, 0))
    full = lambda a: pl.BlockSpec(a.shape, lambda i: (0,) * a.ndim)
    bf = lambda a: a.astype(BF16)
    consts = [p['mu'], bf(p['w_rkv']), bf(p['gate1']), bf(p['gate2']), bf(p['dec1']), bf(p['dec2']), p['dec0'],
              bf(p['icl1']), bf(p['icl2']), p['icl0'], p['k_k'].reshape(1, D), p['k_a'].reshape(1, D),
              p['r_k'].reshape(1, D), _seg_ones()]
    one = jax.ShapeDtypeStruct((T, D), F32)
    two = jax.ShapeDtypeStruct((2, T, D), F32)
    return pl.pallas_call(
        functools.partial(_rwkv_proj_kernel, seg_tiles),
        grid=(T // tm,),
        in_specs=[row, before, after, mod_spec] + [full(a) for a in consts],
        out_specs=[row, row, row, row, row, row2, row2, row2],
        out_shape=[one, one, one, one, one, two, two, two],
        compiler_params=_cparams(("arbitrary",)),
    )(h, h, h, mod_rows, *consts)


def _scan_kernel(r_ref, w_ref, k_ref, v_ref, a_ref, b_ref, ones_ref, hsel_ref,
                 y_ref, s_ref, vt_ref):
    d = pl.program_id(0)
    c = pl.program_id(2)
    tc, nb = r_ref.shape[0], r_ref.shape[1]
    tw = SCAN_TILE
    n_wide = r_ref.shape[2] // tw
    heads = tw // HEAD_DIM
    assert heads * tc == tw

    @pl.when(c == 0)
    def _():
        s_ref[...] = jnp.zeros_like(s_ref)

    for bb in range(nb):
        for q in range(n_wide):
            vt = v_ref[:, bb, q * tw:(q + 1) * tw].T
            vt_ref[bb * n_wide + q] = jnp.concatenate(
                [vt[h * HEAD_DIM:(h + 1) * HEAD_DIM] for h in range(heads)], axis=1)

    head_base = (lax.broadcasted_iota(I32, (HEAD_DIM, LANES), 1) // HEAD_DIM) * tc
    tiles = [(bb, q) for bb in range(nb) for q in range(n_wide)]
    groups = [tiles[i:i + SCAN_GROUP] for i in range(0, len(tiles), SCAN_GROUP)]

    def stacked(grp, get, dtype=F32):
        def wide(bb, q):
            return jnp.concatenate(
                [jnp.broadcast_to(get(bb, slice(q * tw + hf * LANES, q * tw + (hf + 1) * LANES)).astype(dtype),
                                  (HEAD_DIM, LANES)) for hf in range(tw // LANES)], axis=1)
        return jnp.concatenate([wide(bb, q) for bb, q in grp], axis=0)

    def load_state(grp):
        return jnp.concatenate([s_ref[bb * n_wide + q] for bb, q in grp], axis=0)

    def emit_y(grp, st_b, t_y):
        r_rows = stacked(grp, lambda bb, cols: r_ref[t_y, bb:bb + 1, cols], BF16)
        yh = _dot_nt(hsel_ref[...], st_b * r_rows)
        first = tiles.index(grp[0])
        y_ref[0, t_y, :, first * HEAD_DIM:(first + len(grp)) * HEAD_DIM] = yh[:heads]

    def step(s_i, carry):
        t = jnp.where(d == 0, s_i, tc - 1 - s_i)
        t_prev = jnp.where(s_i == 0, t, jnp.where(d == 0, t - 1, t + 1))
        pick = head_base + t
        for grp in groups:
            one = lambda ref: stacked(grp, lambda bb, cols: ref[t, bb:bb + 1, cols])
            two = lambda ref: stacked(grp, lambda bb, cols: ref[0, t, bb:bb + 1, cols])
            st = load_state(grp)
            st_b = st.astype(BF16)
            a_rows = stacked(grp, lambda bb, cols: a_ref[t, bb:bb + 1, cols], BF16)
            sa = _dot(st_b * a_rows, ones_ref[...])
            emit_y(grp, st_b, t_prev)
            vcol = jnp.concatenate(
                [jnp.concatenate([jnp.take_along_axis(vt_ref[bb * n_wide + q, :, hf * LANES:(hf + 1) * LANES],
                                                      pick, axis=1) for hf in range(tw // LANES)], axis=1)
                 for bb, q in grp], axis=0)
            st = st * two(w_ref) + sa * two(b_ref) + vcol * two(k_ref)
            for j, (bb, q) in enumerate(grp):
                s_ref[bb * n_wide + q] = st[j * HEAD_DIM:(j + 1) * HEAD_DIM]
        return carry

    lax.fori_loop(0, tc, step, 0, unroll=SCAN_UNROLL)
    t_last = jnp.where(d == 0, tc - 1, 0)
    for grp in groups:
        emit_y(grp, load_state(grp).astype(BF16), t_last)


def _wkv_scan(r, w, k, v, a, b, n_ctx):
    N, B, D = r.shape
    tc = SCAN_CHUNK
    wc = SCAN_COLS
    n_wide = wc // SCAN_TILE
    nc = N // tc
    ncc = n_ctx // tc

    def chunk(d, c):
        rev = jnp.where(c < ncc, ncc - 1 - c, nc - 1 - (c - ncc))
        return jnp.where(d == 0, c, rev)

    one = pl.BlockSpec((tc, B, wc), lambda d, g, c: (chunk(d, c), 0, g))
    two = pl.BlockSpec((1, tc, B, wc), lambda d, g, c: (d, chunk(d, c), 0, g))
    seg = np.arange(SCAN_TILE) // HEAD_DIM
    hsel = np.zeros((8, SCAN_TILE), np.float32)
    for hh in range(SCAN_TILE // HEAD_DIM):
        hsel[hh, seg == hh] = 1.0
    const = lambda a_: pl.BlockSpec(a_.shape, lambda d, g, c: (0, 0))
    consts = [_seg_ones(SCAN_TILE), jnp.asarray(hsel, BF16)]
    heads = SCAN_TILE // HEAD_DIM
    ncg = D // wc
    y = pl.pallas_call(
        _scan_kernel,
        grid=(2, ncg, nc),
        in_specs=[one, two, two, one, one, two] + [const(a_) for a_ in consts],
        out_specs=pl.BlockSpec((1, tc, heads, B * n_wide * HEAD_DIM), lambda d, g, c: (d, chunk(d, c), 0, g)),
        out_shape=jax.ShapeDtypeStruct((2, N, heads, ncg * B * n_wide * HEAD_DIM), F32),
        scratch_shapes=[pltpu.VMEM((B * n_wide, HEAD_DIM, SCAN_TILE), F32),
                        pltpu.VMEM((B * n_wide, HEAD_DIM, SCAN_TILE), F32)],
        compiler_params=_cparams(("arbitrary", "arbitrary", "arbitrary")),
    )(r, w, k, v, a, b, *consts)
    y = y.reshape(2, N, heads, ncg, B, n_wide * HEAD_DIM)
    return jnp.transpose(y, (0, 1, 4, 2, 3, 5)).reshape(2, N, B, D)


def _scan_head_order(d):
    heads = SCAN_TILE // HEAD_DIM
    n_wide = SCAN_COLS // SCAN_TILE
    ncg = d // SCAN_COLS
    order = []
    for g in range(ncg):
        for q in range(n_wide):
            for h in range(heads):
                order.append((h * ncg + g) * n_wide + q)
    return tuple(order)


def _rwkv_out_kernel(head_order, y0_ref, y1_ref, bonus_ref, g_ref, lnx_ref, ones_ref, w_ref, h_ref, mod_ref,
                     lg_ref, lb_ref, hn_ref, u_ref):
    y_in = y0_ref[0] + y1_ref[0]
    y = jnp.concatenate([y_in[:, p * HEAD_DIM:(p + 1) * HEAD_DIM] for p in head_order], axis=1)
    ym = _head_sum(y, ones_ref) * (1.0 / HEAD_DIM)
    yc = y - ym
    yv = _head_sum(yc * yc, ones_ref) * (1.0 / HEAD_DIM)
    yn = yc * lax.rsqrt(yv + LNX_EPS) * lnx_ref[0:1, :] + lnx_ref[1:2, :]
    x = ((yn + bonus_ref[...]) * g_ref[...]).astype(BF16)
    o = _dot(x, w_ref[...])
    z = DEEPNORM_ALPHA * h_ref[...] + mod_ref[0] * o
    hn = _layer_norm(z, lg_ref[...], lb_ref[...])
    hn_ref[...] = hn
    u_ref[...] = _pack_halves(hn * (1.0 + mod_ref[2]) + mod_ref[1])


def _rwkv_out(y, bonus, g, lnx, w_out, h, mod_rows, ln_g, ln_b, row0):
    T, D = h.shape
    tm = ROW_TILE
    t0 = row0 // tm
    off = pl.BlockSpec((tm, D), lambda i: (i + t0, 0))
    out = pl.BlockSpec((tm, D), lambda i: (i, 0))
    full = lambda a: pl.BlockSpec(a.shape, lambda i: (0,) * a.ndim)
    vec = pl.BlockSpec((1, D), lambda i: (0, 0))
    ones = _seg_ones()
    w_b = w_out.astype(BF16)
    return pl.pallas_call(
        functools.partial(_rwkv_out_kernel, _scan_head_order(D)),
        grid=((T - row0) // tm,),
        in_specs=[pl.BlockSpec((1, tm, D), lambda i: (0, i + t0, 0)),
                  pl.BlockSpec((1, tm, D), lambda i: (1, i + t0, 0)),
                  off, off, full(lnx), full(ones), full(w_b), off, full(mod_rows), vec, vec],
        out_specs=[out, pl.BlockSpec((tm, D // 2), lambda i: (i, 0))],
        out_shape=[jax.ShapeDtypeStruct((T - row0, D), F32), jax.ShapeDtypeStruct((T - row0, D // 2), I32)],
        compiler_params=_cparams(("arbitrary",)),
    )(y, y, bonus, g, lnx, ones, w_b, h, mod_rows, ln_g.reshape(1, D), ln_b.reshape(1, D))


def kernel(x, c, ctx, c_ctx, ada_w, ada_b, post_ln_g, post_ln_b, att_w_in, att_w_out, att_sink, diff_lambda_vecs, diff_subln_g, rk_mu, rk_w_rkv, rk_w_out, rk_decay0, rk_decay1, rk_decay2, rk_iclr0, rk_iclr1, rk_iclr2, rk_gate1, rk_gate2, rk_k_k, rk_k_a, rk_r_k, rk_lnx, moe_router, moe_bias, moe_w_in, moe_w_out, moe_ws_in, moe_ws_out):
    B, S, D = x.shape
    L = ctx.shape[1]
    N = L + S
    tm = ROW_TILE
    assert L % tm == 0 and S % tm == 0 and L % SCAN_CHUNK == 0 and S % SCAN_CHUNK == 0
    assert tm % B == 0 and PROJ_TILE % B == 0 and L % (PROJ_TILE // B) == 0 and D % SCAN_COLS == 0

    rows = -(-(B + 1) // 8) * 8
    cvec = jnp.concatenate([c, c_ctx[None, :], jnp.zeros((rows - B - 1, D), F32)], axis=0)
    mods = [_mod_table(_ada_mod(cvec, ada_w[i], ada_b[i]), B, D) for i in range(DEPTH)]

    h0 = jnp.concatenate([ctx, x], axis=1)
    lam_init = 0.8 - 0.6 * math.exp(-0.3 * 0)
    qa, ka, va, qb, kb, vb = _attn_inproj(h0, mods[0], att_w_in[0], L)
    oa = _win_attn(qa, ka, va, att_sink[0], L)
    ob = _diff_attn(qb, kb, vb, diff_lambda_vecs[0], diff_subln_g[0], lam_init, L)
    h1, u1 = _mix_out([oa, ob], [att_w_out[0][:A_WIDTH], att_w_out[0][A_WIDTH:]], h0, mods[0],
                      post_ln_g[0, 0], post_ln_b[0, 0], L, 0)
    tiles_b, tiles_c = N // tm, L // tm
    gate0 = mods[0][:, :, 5].reshape(B * 2, 1, D)
    gate0_index = lambda i: (i // tiles_b) * 2 + jnp.minimum((i % tiles_b) // tiles_c, 1)
    h2 = _moe_layer(u1.reshape(B * N, D // 2), h1.reshape(B * N, D), gate0, gate0_index, moe_router[0], moe_bias[0],
                    moe_w_in, moe_w_out, moe_ws_in[0], moe_ws_out[0],
                    post_ln_g[0, 1], post_ln_b[0, 1], 0).reshape(B, N, D)

    m_ctx, m_lat = mods[1][:, 0], mods[1][:, 1]
    h2_t = jnp.swapaxes(h2, 0, 1).reshape(N * B, D)
    rows_of = lambda m, j, n: jnp.tile(m[:, j], (n // B, 1))
    proj_mod = jnp.stack([jnp.stack([rows_of(m, 0, PROJ_TILE), rows_of(m, 1, PROJ_TILE)]) for m in (m_ctx, m_lat)])
    params = dict(mu=rk_mu[0], w_rkv=rk_w_rkv[0], gate1=rk_gate1[0], gate2=rk_gate2[0],
                  dec0=rk_decay0[0], dec1=rk_decay1[0], dec2=rk_decay2[0],
                  icl0=rk_iclr0[0], icl1=rk_iclr1[0], icl2=rk_iclr2[0],
                  k_k=rk_k_k[0], k_a=rk_k_a[0], r_k=rk_r_k[0])
    r, v, a, g, bonus, w2, k2, b2 = _rwkv_proj(h2_t, proj_mod, B, L, params)
    tmaj = lambda t: t.reshape(t.shape[:-2] + (N, B, D))
    y = _wkv_scan(tmaj(r), tmaj(w2), tmaj(k2), tmaj(v), tmaj(a), tmaj(b2), L)
    lat_rows = lambda j: rows_of(m_lat, j, tm)
    h3, u3 = _rwkv_out(y.reshape(2, N * B, D), bonus, g, rk_lnx[0], rk_w_out[0], h2_t,
                       jnp.stack([lat_rows(2), lat_rows(3), lat_rows(4)]),
                       post_ln_g[1, 0], post_ln_b[1, 0], L * B)
    out = _moe_layer(u3, h3, lat_rows(5)[None], lambda i: 0, moe_router[1], moe_bias[1],
                     moe_w_in, moe_w_out, moe_ws_in[1], moe_ws_out[1],
                     post_ln_g[1, 1], post_ln_b[1, 1], 1)
    return jnp.swapaxes(out.reshape(S, B, D), 0, 1)
```

```python
import functools
import math

import numpy as np
import jax
import jax.numpy as jnp
from jax import lax
from jax.experimental import pallas as pl
from jax.experimental.pallas import tpu as pltpu
from jax.experimental.pallas import tpu_sc as plsc

F32 = jnp.float32
BF16 = jnp.bfloat16
I32 = jnp.int32

HEAD_DIM = 64
GRID_W = 64
ROPE_AXIS_DIM = HEAD_DIM // 2
ROPE_THETA = 10000.0
Q_BLOCK = 128
A_Q_HEADS = 8
A_KV_HEADS = 2
A_GROUP = A_Q_HEADS // A_KV_HEADS
A_WIDTH = A_Q_HEADS * HEAD_DIM
A_KV_WIDTH = A_KV_HEADS * HEAD_DIM
B_HEADS = 4
B_V_DIM = 2 * HEAD_DIM
B_WIDTH = B_HEADS * B_V_DIM
LNX_EPS = 64e-5
N_EXPERTS = 256
TOP_K = 8
N_GROUPS = 8
TOPK_GROUPS = 4
ROUTED_SCALE = 2.5
MOE_BLOCK = 256
EXPERT_IN_SLOTS = 4
EXPERT_OUT_SLOTS = 2
LN_EPS = 1e-5
SUBLN_EPS = 1e-5
NEG_INF = -1e30
DEPTH = 2
DEEPNORM_ALPHA = (2 * DEPTH) ** 0.25

LANES = 128
ADA_COLS = 768
ROW_TILE = 256
PROJ_TILE = 128
SC_WINDOW = 128
SCAN_CHUNK = 64
SCAN_COLS = 1024
SCAN_TILE = 256
SCAN_GROUP = 8
SCAN_UNROLL = 4
VMEM_LIMIT = 56 * 1024 * 1024


def _cparams(sem):
    return pltpu.CompilerParams(dimension_semantics=sem, vmem_limit_bytes=VMEM_LIMIT)


def _silu(x):
    return x * jax.nn.sigmoid(x)


def _layer_norm(z, g, b):
    mu = jnp.mean(z, -1, keepdims=True)
    zc = z - mu
    var = jnp.mean(zc * zc, -1, keepdims=True)
    return zc * lax.rsqrt(var + LN_EPS) * g + b


def _dot(a, b):
    return jnp.dot(a, b, preferred_element_type=F32)


def _dot_nt(a, b):
    return lax.dot_general(a, b, (((1,), (1,)), ((), ())), preferred_element_type=F32)


def _pack_halves(x):
    half = x.shape[1] // 2
    bits = lambda v: lax.bitcast_convert_type(v.astype(BF16).astype(F32), I32)
    return lax.shift_right_logical(bits(x[:, :half]), 16) | bits(x[:, half:])


def _unpack_halves(p):
    lo = lax.bitcast_convert_type(lax.shift_left(p, 16), F32)
    hi = lax.bitcast_convert_type(p & jnp.int32(-65536), F32)
    return lo, hi


def _dot_halves(p, w_ref_or_array):
    lo, hi = _unpack_halves(p)
    half = p.shape[1]
    return _dot(lo.astype(BF16), w_ref_or_array[:half]) + _dot(hi.astype(BF16), w_ref_or_array[half:])


def _ada_kernel(c_ref, w_ref, b_ref, o_ref):
    c = c_ref[...]
    o_ref[...] = _dot(_silu(c).astype(BF16), w_ref[...].astype(BF16)) + b_ref[...]


def _ada_mod(cvec, w, bias):
    R, D = cvec.shape
    n_out = w.shape[1]
    tn = ADA_COLS
    return pl.pallas_call(
        _ada_kernel,
        grid=(n_out // tn,),
        in_specs=[pl.BlockSpec((R, D), lambda j: (0, 0)),
                  pl.BlockSpec((D, tn), lambda j: (0, j)),
                  pl.BlockSpec((1, tn), lambda j: (0, j))],
        out_specs=pl.BlockSpec((R, tn), lambda j: (0, j)),
        out_shape=jax.ShapeDtypeStruct((R, n_out), F32),
        compiler_params=_cparams(("arbitrary",)),
    )(cvec, w, bias.reshape(1, n_out))


def _mod_table(m, batch, d):
    m_lat = m[:batch].reshape(batch, 6, d)
    m_ctx = jnp.broadcast_to(m[batch].reshape(1, 6, d), (batch, 6, d))
    return jnp.stack([m_ctx, m_lat], axis=1)


def _mod_spec(d, ctx_tiles):
    return pl.BlockSpec((1, 1, 6, d), lambda b, i: (b, jnp.minimum(i // ctx_tiles, 1), 0, 0))


def _rope_tables(n_ctx, n_lat):
    rows = n_lat // GRID_W
    row = np.repeat(np.arange(rows), GRID_W).astype(np.float32)
    col = np.tile(np.arange(GRID_W), rows).astype(np.float32)
    inv = (ROPE_THETA ** (-np.arange(0, ROPE_AXIS_DIM, 2, dtype=np.float32) / ROPE_AXIS_DIM)).astype(np.float32)
    ar = row[:, None] * inv
    ac = col[:, None] * inv
    ang = np.concatenate([ar, ar, ac, ac], -1)
    cos = np.cos(ang).astype(np.float32)
    sin = np.sin(ang).astype(np.float32)
    lower = (np.arange(HEAD_DIM) % ROPE_AXIS_DIM) < (ROPE_AXIS_DIM // 2)
    sin_up = np.where(lower[None, :], -sin, 0.0)
    sin_dn = np.where(lower[None, :], 0.0, sin)

    def full(t, ctx_fill):
        t = np.concatenate([np.full((n_ctx, HEAD_DIM), ctx_fill, np.float32), t], 0)
        return jnp.asarray(np.tile(t, (1, LANES // HEAD_DIM)))

    return full(cos, 1.0), full(sin_up, 0.0), full(sin_dn, 0.0)


def _inproj_kernel(h_ref, mod_ref, w_ref, cos_ref, su_ref, sd_ref,
                   qa_ref, ka_ref, va_ref, qb_ref, kb_ref, vb_ref):
    h = h_ref[0]
    shift = mod_ref[0, 0, 0:1, :]
    scale = mod_ref[0, 0, 1:2, :]
    u = (h * (1.0 + scale) + shift).astype(BF16)
    y = _dot(u, w_ref[...])
    cos, s_up, s_dn = cos_ref[...], su_ref[...], sd_ref[...]
    q_scale = HEAD_DIM ** -0.5

    def rope(xc):
        half = ROPE_AXIS_DIM // 2
        return xc * cos + pltpu.roll(xc, LANES - half, 1) * s_up + pltpu.roll(xc, half, 1) * s_dn

    def emit(out_ref, col0, width, roped, mul):
        for j in range(width // LANES):
            xc = y[:, col0 + j * LANES: col0 + (j + 1) * LANES]
            if roped:
                xc = rope(xc)
            if mul != 1.0:
                xc = xc * mul
            out_ref[0, :, j * LANES:(j + 1) * LANES] = xc.astype(out_ref.dtype)

    c = 0
    emit(qa_ref, c, A_WIDTH, True, q_scale); c += A_WIDTH
    emit(ka_ref, c, A_KV_WIDTH, True, 1.0); c += A_KV_WIDTH
    emit(va_ref, c, A_KV_WIDTH, False, 1.0); c += A_KV_WIDTH
    emit(qb_ref, c, B_WIDTH, True, q_scale); c += B_WIDTH
    emit(kb_ref, c, B_WIDTH, True, 1.0); c += B_WIDTH
    emit(vb_ref, c, B_WIDTH, False, 1.0)


def _attn_inproj(h, mod, w_in, n_ctx):
    B, N, D = h.shape
    tm = ROW_TILE
    cos, s_up, s_dn = _rope_tables(n_ctx, N - n_ctx)
    widths = (A_WIDTH, A_KV_WIDTH, A_KV_WIDTH, B_WIDTH, B_WIDTH, B_WIDTH)
    tab_spec = pl.BlockSpec((tm, LANES), lambda b, i: (i, 0))
    return pl.pallas_call(
        _inproj_kernel,
        grid=(B, N // tm),
        in_specs=[pl.BlockSpec((1, tm, D), lambda b, i: (b, i, 0)),
                  _mod_spec(D, n_ctx // tm),
                  pl.BlockSpec(w_in.shape, lambda b, i: (0, 0)),
                  tab_spec, tab_spec, tab_spec],
        out_specs=[pl.BlockSpec((1, tm, w), lambda b, i: (b, i, 0)) for w in widths],
        out_shape=[jax.ShapeDtypeStruct((B, N, w), BF16) for w in widths],
        compiler_params=_cparams(("arbitrary", "arbitrary")),
    )(h, mod, w_in.astype(BF16), cos, s_up, s_dn)


def _win_attn_kernel(n_ctx_blocks, n_blocks, q_ref, kc_ref, vc_ref, kl_ref, km_ref, kr_ref,
                     vl_ref, vm_ref, vr_ref, sink_ref, o_ref):
    j = pl.program_id(1)
    is_lat = j >= n_ctx_blocks
    qb = Q_BLOCK
    n_c = kc_ref.shape[1]
    rows = A_GROUP * qb
    n_keys = n_c + 3 * qb
    far = 1 << 20
    r_idx = lax.broadcasted_iota(I32, (rows, n_keys), 0) % qb
    cw = lax.broadcasted_iota(I32, (rows, n_keys), 1) - n_c
    off_l = jnp.where(jnp.logical_and(is_lat, j > n_ctx_blocks), 0, far)
    end_m = jnp.where(is_lat, 2 * qb, qb)
    off_r = jnp.where(jnp.logical_and(is_lat, j < n_blocks - 1), 0, far)
    valid = ((cw < 0)
             | ((cw >= 0) & (cw < qb) & (cw >= r_idx + off_l))
             | ((cw >= qb) & (cw < end_m))
             | ((cw >= 2 * qb) & (cw - 2 * qb + off_r <= r_idx)))
    outs = []
    for kv in range(A_KV_HEADS):
        cols = slice(kv * HEAD_DIM, (kv + 1) * HEAD_DIM)
        k_all = jnp.concatenate([kc_ref[0, :, cols], kl_ref[0, :, cols], km_ref[0, :, cols],
                                 kr_ref[0, :, cols]], axis=0)
        v_all = jnp.concatenate([vc_ref[0, :, cols], vl_ref[0, :, cols], vm_ref[0, :, cols],
                                 vr_ref[0, :, cols]], axis=0)
        q0 = kv * A_GROUP
        q = jnp.concatenate([q_ref[0, :, (q0 + g) * HEAD_DIM:(q0 + g + 1) * HEAD_DIM]
                             for g in range(A_GROUP)], axis=0)
        sink = jnp.concatenate([jnp.broadcast_to(sink_ref[q0 + g:q0 + g + 1, 0:1], (qb, 1))
                                for g in range(A_GROUP)], axis=0)
        s = jnp.where(valid, _dot_nt(q, k_all), NEG_INF)
        m = jnp.maximum(jnp.max(s, -1, keepdims=True), sink)
        e = jnp.exp(s - m)
        denom = jnp.sum(e, -1, keepdims=True) + jnp.exp(sink - m)
        o = _dot(e.astype(BF16), v_all) * (1.0 / denom)
        outs += [o[g * qb:(g + 1) * qb] for g in range(A_GROUP)]
    for j2 in range(A_Q_HEADS // 2):
        pair = jnp.concatenate([outs[2 * j2], outs[2 * j2 + 1]], axis=1)
        o_ref[0, :, j2 * LANES:(j2 + 1) * LANES] = pair.astype(o_ref.dtype)


def _win_attn(qa, ka, va, sink, n_ctx):
    B, N, _ = qa.shape
    qb = Q_BLOCK
    nb = N // qb
    ncb = n_ctx // qb
    sink_pad = jnp.broadcast_to(sink.reshape(A_Q_HEADS, 1).astype(F32), (A_Q_HEADS, LANES))

    def left(b, j):
        return (b, jnp.clip(j - 1, ncb, nb - 1), 0)

    def mid(b, j):
        return (b, jnp.clip(j, ncb, nb - 1), 0)

    def right(b, j):
        return (b, jnp.clip(j + 1, ncb, nb - 1), 0)

    kv_blk = lambda im: pl.BlockSpec((1, qb, A_KV_WIDTH), im)
    ctx_blk = pl.BlockSpec((1, n_ctx, A_KV_WIDTH), lambda b, j: (b, 0, 0))
    return pl.pallas_call(
        functools.partial(_win_attn_kernel, ncb, nb),
        grid=(B, nb),
        in_specs=[pl.BlockSpec((1, qb, A_WIDTH), lambda b, j: (b, j, 0)),
                  ctx_blk, ctx_blk,
                  kv_blk(left), kv_blk(mid), kv_blk(right),
                  kv_blk(left), kv_blk(mid), kv_blk(right),
                  pl.BlockSpec((A_Q_HEADS, LANES), lambda b, j: (0, 0))],
        out_specs=pl.BlockSpec((1, qb, A_WIDTH), lambda b, j: (b, j, 0)),
        out_shape=jax.ShapeDtypeStruct((B, N, A_WIDTH), BF16),
        compiler_params=_cparams(("arbitrary", "arbitrary")),
    )(qa, ka, va, ka, ka, ka, va, va, va, sink_pad)


def _diff_attn_kernel(n_ctx, lam_init, q_ref, k_ref, v_ref, lv_ref, g_ref, o_ref):
    j = pl.program_id(1)
    lv = lv_ref[...]
    lam = (jnp.exp(jnp.sum(lv[0:1] * lv[1:2], -1, keepdims=True))
           - jnp.exp(jnp.sum(lv[2:3] * lv[3:4], -1, keepdims=True)) + lam_init)
    gain = g_ref[...] * (1.0 - lam_init)

    def run(n_keys):
        for hd in range(B_HEADS):
            parts = []
            for mm in range(2):
                c0 = (hd * 2 + mm) * HEAD_DIM
                q = q_ref[0, :, c0:c0 + HEAD_DIM]
                k = k_ref[0, :n_keys, c0:c0 + HEAD_DIM]
                s = _dot_nt(q, k)
                e = jnp.exp(s - jnp.max(s, -1, keepdims=True))
                parts.append((e, jnp.sum(e, -1, keepdims=True)))
            (e0, l0), (e1, l1) = parts
            v = v_ref[0, :n_keys, hd * B_V_DIM:(hd + 1) * B_V_DIM]
            o = _dot(e0.astype(BF16), v) * (1.0 / l0) - _dot(e1.astype(BF16), v) * (lam / l1)
            o = o * lax.rsqrt(jnp.mean(o * o, -1, keepdims=True) + SUBLN_EPS) * gain
            o_ref[0, :, hd * B_V_DIM:(hd + 1) * B_V_DIM] = o.astype(o_ref.dtype)

    @pl.when(j == 0)
    def _():
        run(n_ctx)

    @pl.when(j > 0)
    def _():
        run(k_ref.shape[1])


def _diff_attn(qb, kb, vb, lam_vecs, subln_g, lam_init, n_ctx):
    B, N, _ = qb.shape
    tq = n_ctx
    return pl.pallas_call(
        functools.partial(_diff_attn_kernel, n_ctx, lam_init),
        grid=(B, N // tq),
        in_specs=[pl.BlockSpec((1, tq, B_WIDTH), lambda b, j: (b, j, 0)),
                  pl.BlockSpec((1, N, B_WIDTH), lambda b, j: (b, 0, 0)),
                  pl.BlockSpec((1, N, B_WIDTH), lambda b, j: (b, 0, 0)),
                  pl.BlockSpec((4, HEAD_DIM), lambda b, j: (0, 0)),
                  pl.BlockSpec((1, B_V_DIM), lambda b, j: (0, 0))],
        out_specs=pl.BlockSpec((1, tq, B_WIDTH), lambda b, j: (b, j, 0)),
        out_shape=jax.ShapeDtypeStruct((B, N, B_WIDTH), BF16),
        compiler_params=_cparams(("arbitrary", "arbitrary")),
    )(qb, kb, vb, lam_vecs.astype(F32), subln_g.reshape(1, B_V_DIM).astype(F32))


def _mix_out_kernel(n_in, *refs):
    xs = refs[:n_in]
    ws = refs[n_in:2 * n_in]
    h_ref, mod_ref, g_ref, b_ref, hn_ref, u_ref = refs[2 * n_in:]
    o = _dot(xs[0][0], ws[0][...])
    for x_ref, w_ref in zip(xs[1:], ws[1:]):
        o = o + _dot(x_ref[0], w_ref[...])
    z = DEEPNORM_ALPHA * h_ref[0] + mod_ref[0, 0, 2:3, :] * o
    hn = _layer_norm(z, g_ref[...], b_ref[...])
    hn_ref[0] = hn
    u_ref[0] = _pack_halves(hn * (1.0 + mod_ref[0, 0, 4:5, :]) + mod_ref[0, 0, 3:4, :])


def _mix_out(xs, ws, h, mod, ln_g, ln_b, n_ctx, row0):
    B, N, D = h.shape
    tm = ROW_TILE
    t0 = row0 // tm
    n_out = N - row0
    row_spec = lambda w: pl.BlockSpec((1, tm, w), lambda b, i: (b, i + t0, 0))
    out_spec = pl.BlockSpec((1, tm, D), lambda b, i: (b, i, 0))
    vec_spec = pl.BlockSpec((1, D), lambda b, i: (0, 0))
    return pl.pallas_call(
        functools.partial(_mix_out_kernel, len(xs)),
        grid=(B, n_out // tm),
        in_specs=([row_spec(x.shape[-1]) for x in xs]
                  + [pl.BlockSpec(w.shape, lambda b, i: (0, 0)) for w in ws]
                  + [row_spec(D),
                     pl.BlockSpec((1, 1, 6, D), lambda b, i: (b, jnp.minimum((i + t0) // (n_ctx // tm), 1), 0, 0)),
                     vec_spec, vec_spec]),
        out_specs=[out_spec, pl.BlockSpec((1, tm, D // 2), lambda b, i: (b, i, 0))],
        out_shape=[jax.ShapeDtypeStruct((B, n_out, D), F32), jax.ShapeDtypeStruct((B, n_out, D // 2), I32)],
        compiler_params=_cparams(("arbitrary", "arbitrary")),
    )(*xs, *[w.astype(BF16) for w in ws], h, mod, ln_g.reshape(1, D), ln_b.reshape(1, D))


def _router_kernel(u_ref, rt_ref, bias_ref, tri_ref, e_ref, gw_ref, rank_ref, cnt_ref, carry_ref):
    i = pl.program_id(0)

    @pl.when(i == 0)
    def _():
        carry_ref[...] = jnp.zeros_like(carry_ref)

    tm = u_ref.shape[0]
    per_group = N_EXPERTS // N_GROUPS
    neg = -jnp.inf
    u_lo, u_hi = _unpack_halves(u_ref[...])
    half = u_ref.shape[1]
    logits = (_dot_nt(rt_ref[:, :half], u_lo.astype(BF16))
              + _dot_nt(rt_ref[:, half:], u_hi.astype(BF16)))
    scores = jax.nn.sigmoid(logits)
    sel = scores + bias_ref[...]
    io_in = lax.broadcasted_iota(I32, (per_group, tm), 0)
    grp_rows = []
    for gi in range(N_GROUPS):
        sg = sel[gi * per_group:(gi + 1) * per_group]
        m1 = jnp.max(sg, axis=0, keepdims=True)
        i1 = jnp.min(jnp.where(sg == m1, io_in, per_group), axis=0, keepdims=True)
        m2 = jnp.max(jnp.where(io_in == i1, neg, sg), axis=0, keepdims=True)
        grp_rows.append(m1 + m2)
    grp = jnp.concatenate(grp_rows, axis=0)
    io_g = lax.broadcasted_iota(I32, grp.shape, 0)
    g_sel = jnp.zeros(grp.shape, F32)
    for _ in range(TOPK_GROUPS):
        m = jnp.max(grp, axis=0, keepdims=True)
        hit = io_g == jnp.min(jnp.where(grp == m, io_g, N_GROUPS), axis=0, keepdims=True)
        g_sel = jnp.where(hit, 1.0, g_sel)
        grp = jnp.where(hit, neg, grp)
    selm = jnp.concatenate(
        [jnp.where(g_sel[gi:gi + 1] > 0.5, sel[gi * per_group:(gi + 1) * per_group], NEG_INF)
         for gi in range(N_GROUPS)], axis=0)
    io_e = lax.broadcasted_iota(I32, selm.shape, 0)
    chosen_f = jnp.zeros(selm.shape, F32)
    idx, gws = [], []
    for _ in range(TOP_K):
        m = jnp.max(selm, axis=0, keepdims=True)
        ik = jnp.min(jnp.where(selm == m, io_e, N_EXPERTS), axis=0, keepdims=True)
        hit = io_e == ik
        idx.append(ik)
        gws.append(jnp.sum(jnp.where(hit, scores, 0.0), axis=0, keepdims=True))
        chosen_f = jnp.where(hit, 1.0, chosen_f)
        selm = jnp.where(hit, neg, selm)
    gw = jnp.concatenate(gws, axis=0)
    gw_ref[...] = gw / jnp.sum(gw, axis=0, keepdims=True) * ROUTED_SCALE
    e_ref[...] = jnp.concatenate(idx, axis=0)
    before = _dot(chosen_f.astype(BF16), tri_ref[...]) + carry_ref[...]
    ranks = [jnp.sum(jnp.where(io_e == ik, before, 0.0), axis=0, keepdims=True) for ik in idx]
    rank_ref[...] = jnp.concatenate(ranks, axis=0).astype(I32)
    carry_ref[...] = carry_ref[...] + jnp.sum(chosen_f, axis=1, keepdims=True)
    cnt_ref[...] = carry_ref[...].astype(I32)


def _router(u, router, bias):
    T = u.shape[0]
    D = router.shape[0]
    tm = ROW_TILE
    tri = jnp.asarray(np.triu(np.ones((tm, tm), np.float32), 1), BF16)
    tok_spec = pl.BlockSpec((TOP_K, tm), lambda i: (0, i))
    return pl.pallas_call(
        _router_kernel,
        grid=(T // tm,),
        in_specs=[pl.BlockSpec((tm, D // 2), lambda i: (i, 0)),
                  pl.BlockSpec((N_EXPERTS, D), lambda i: (0, 0)),
                  pl.BlockSpec((N_EXPERTS, 1), lambda i: (0, 0)),
                  pl.BlockSpec((tm, tm), lambda i: (0, 0))],
        out_specs=[tok_spec, tok_spec, tok_spec, pl.BlockSpec((N_EXPERTS, 1), lambda i: (0, 0))],
        out_shape=[jax.ShapeDtypeStruct((TOP_K, T), I32), jax.ShapeDtypeStruct((TOP_K, T), F32),
                   jax.ShapeDtypeStruct((TOP_K, T), I32), jax.ShapeDtypeStruct((N_EXPERTS, 1), I32)],
        scratch_shapes=[pltpu.VMEM((N_EXPERTS, 1), F32)],
        compiler_params=_cparams(("arbitrary",)),
    )(u, router.T.astype(BF16), bias.reshape(N_EXPERTS, 1).astype(F32), tri)


def _sc_mesh():
    return plsc.VectorSubcoreMesh(core_axis_name="c", subcore_axis_name="s")


def _sc_scatter_rows(x, dest, n_rows):
    T, W = x.shape
    K = dest.shape[0]
    win = SC_WINDOW
    n_win = T // win

    @functools.partial(pl.kernel, out_type=jax.ShapeDtypeStruct((n_rows, W), x.dtype), mesh=_sc_mesh(),
                       scratch_types=[])
    def scatter(x_hbm, i_hbm, o_hbm):
        def body(x_vmem, i_vmem):
            pltpu.sync_copy(x_vmem, o_hbm.at[i_vmem.at[0]])

        pltpu.emit_pipeline(
            body,
            grid=(K * n_win,),
            in_specs=[pl.BlockSpec((win, W), lambda j: (j % n_win, 0), pipeline_mode=pl.Buffered(1)),
                      pl.BlockSpec((1, win), lambda j: (0, j))],
            out_specs=[],
            core_axis_name=("c", "s"),
            dimension_semantics=(pltpu.PARALLEL,),
        )(x_hbm, i_hbm)

    return scatter(x, dest.reshape(1, K * T))


def _sc_gather_rows(y, dest):
    K, T = dest.shape
    W = y.shape[1]
    win = SC_WINDOW

    @functools.partial(pl.kernel, out_type=jax.ShapeDtypeStruct((K * T, W), y.dtype), mesh=_sc_mesh(),
                       scratch_types=[])
    def gather(y_hbm, i_hbm, o_hbm):
        def body(i_vmem, o_vmem):
            pltpu.sync_copy(y_hbm.at[i_vmem.at[0]], o_vmem)

        pltpu.emit_pipeline(
            body,
            grid=(K * T // win,),
            in_specs=[pl.BlockSpec((1, win), lambda j: (0, j))],
            out_specs=[pl.BlockSpec((win, W), lambda j: (j, 0), pipeline_mode=pl.Buffered(1))],
            core_axis_name=("c", "s"),
            dimension_semantics=(pltpu.PARALLEL,),
        )(i_hbm, o_hbm)

    return gather(y, dest.reshape(1, K * T)).reshape(K, T, W)


def _expert_kernel(first_ref, x_hbm, wi_ref, wo_ref, y_hbm, wi_b, wo_b, xbuf, ybuf, sem_in, sem_out):
    e = pl.program_id(0)
    n_in, blk = xbuf.shape[0], xbuf.shape[1]
    n_out = ybuf.shape[0]
    g0, g1 = first_ref[e], first_ref[e + 1]
    g_end = first_ref[pl.num_programs(0)]

    def x_copy(g):
        return pltpu.make_async_copy(x_hbm.at[pl.ds(g * blk, blk)], xbuf.at[g % n_in], sem_in.at[g % n_in])

    def y_copy(g):
        return pltpu.make_async_copy(ybuf.at[g % n_out], y_hbm.at[pl.ds(g * blk, blk)], sem_out.at[g % n_out])

    for ahead in range(n_in - 1):
        @pl.when(jnp.logical_and(e == 0, g_end > ahead))
        def _():
            x_copy(ahead).start()

    @pl.when(g1 > g0)
    def _():
        wi_b[...] = wi_ref[0, 0].astype(BF16)
        wo_b[...] = wo_ref[0, 0].astype(BF16)

    ff = wo_b.shape[0]

    def block(g, carry):
        x_copy(g).wait()

        @pl.when(g + n_in - 1 < g_end)
        def _():
            x_copy(g + n_in - 1).start()

        @pl.when(g >= n_out)
        def _():
            y_copy(g - n_out).wait()

        hcat = _dot_halves(xbuf[g % n_in], wi_b)
        act = (_silu(hcat[:, :ff]) * hcat[:, ff:]).astype(BF16)
        ybuf[g % n_out] = _pack_halves(_dot(act, wo_b[...]))
        y_copy(g).start()
        return carry

    lax.fori_loop(g0, g1, block, 0)

    @pl.when(e == pl.num_programs(0) - 1)
    def _():
        for back in range(n_out, 0, -1):
            @pl.when(g_end >= back)
            def _():
                y_copy(g_end - back).wait()


def _experts(xs, first_block, w_in, w_out, layer):
    P, half = xs.shape
    n_exp, D, ff2 = w_in.shape[-3:]
    return pl.pallas_call(
        _expert_kernel,
        grid_spec=pltpu.PrefetchScalarGridSpec(
            num_scalar_prefetch=1,
            grid=(n_exp,),
            in_specs=[pl.BlockSpec(memory_space=pl.ANY),
                      pl.BlockSpec((1, 1, D, ff2), lambda e, fb: (layer, e, 0, 0)),
                      pl.BlockSpec((1, 1, ff2 // 2, D), lambda e, fb: (layer, e, 0, 0))],
            out_specs=pl.BlockSpec(memory_space=pl.ANY),
            scratch_shapes=[pltpu.VMEM((D, ff2), BF16), pltpu.VMEM((ff2 // 2, D), BF16),
                            pltpu.VMEM((EXPERT_IN_SLOTS, MOE_BLOCK, half), I32),
                            pltpu.VMEM((EXPERT_OUT_SLOTS, MOE_BLOCK, half), I32),
                            pltpu.SemaphoreType.DMA((EXPERT_IN_SLOTS,)),
                            pltpu.SemaphoreType.DMA((EXPERT_OUT_SLOTS,))]),
        out_shape=jax.ShapeDtypeStruct((P, half), I32),
        compiler_params=_cparams(("arbitrary",)),
    )(first_block, xs, w_in, w_out)


def _combine_kernel(yg_ref, gw_ref, u_ref, wsi_ref, wso_ref, h_ref, mod_ref, g_ref, b_ref, o_ref):
    ff = wso_ref.shape[0]
    hcat = _dot_halves(u_ref[...], wsi_ref)
    shared = _dot((_silu(hcat[:, :ff]) * hcat[:, ff:]).astype(BF16), wso_ref[...])
    lo, hi = None, None
    for k in range(TOP_K):
        y_lo, y_hi = _unpack_halves(yg_ref[k])
        gk = gw_ref[:, k:k + 1]
        lo = y_lo * gk if lo is None else lo + y_lo * gk
        hi = y_hi * gk if hi is None else hi + y_hi * gk
    routed = jnp.concatenate([lo, hi], axis=1)
    z = DEEPNORM_ALPHA * h_ref[...] + mod_ref[0] * (routed + shared)
    o_ref[...] = _layer_norm(z, g_ref[...], b_ref[...])


def _combine(yg, gw_t, u, ws_in, ws_out, h, gate, gate_index, ln_g, ln_b):
    T, D = h.shape
    tm = ROW_TILE
    vec_spec = pl.BlockSpec((1, D), lambda i: (0, 0))
    row_spec = pl.BlockSpec((tm, D), lambda i: (i, 0))
    packed_spec = pl.BlockSpec((tm, D // 2), lambda i: (i, 0))
    return pl.pallas_call(
        _combine_kernel,
        grid=(T // tm,),
        in_specs=[pl.BlockSpec((TOP_K, tm, D // 2), lambda i: (0, i, 0)),
                  pl.BlockSpec((tm, TOP_K), lambda i: (i, 0)),
                  packed_spec,
                  pl.BlockSpec(ws_in.shape, lambda i: (0, 0)),
                  pl.BlockSpec(ws_out.shape, lambda i: (0, 0)),
                  row_spec,
                  pl.BlockSpec((1,) + gate.shape[1:], lambda i: (gate_index(i), 0, 0)),
                  vec_spec, vec_spec],
        out_specs=row_spec,
        out_shape=jax.ShapeDtypeStruct((T, D), F32),
        compiler_params=_cparams(("arbitrary",)),
    )(yg, gw_t, u, ws_in.astype(BF16), ws_out.astype(BF16), h, gate,
      ln_g.reshape(1, D), ln_b.reshape(1, D))


def _slots_kernel(e_ref, rank_ref, start_ref, dest_ref):
    io_e = lax.broadcasted_iota(I32, (N_EXPERTS, e_ref.shape[1]), 0)
    rows = [jnp.sum(jnp.where(io_e == e_ref[k:k + 1, :], start_ref[...], 0), axis=0, keepdims=True)
            for k in range(TOP_K)]
    dest_ref[...] = jnp.concatenate(rows, axis=0) + rank_ref[...]


def _slots(eidx, rank, pstart):
    T = eidx.shape[1]
    tm = ROW_TILE
    tok_spec = pl.BlockSpec((TOP_K, tm), lambda i: (0, i))
    return pl.pallas_call(
        _slots_kernel,
        grid=(T // tm,),
        in_specs=[tok_spec, tok_spec, pl.BlockSpec((N_EXPERTS, 1), lambda i: (0, 0))],
        out_specs=tok_spec,
        out_shape=jax.ShapeDtypeStruct((TOP_K, T), I32),
        compiler_params=_cparams(("arbitrary",)),
    )(eidx, rank, pstart.reshape(N_EXPERTS, 1))


def _moe_layer(u, h, gate, gate_index, router, bias, w_in, w_out, ws_in, ws_out, ln_g, ln_b, layer):
    T = u.shape[0]
    eidx, gw, rank, counts = _router(u, router, bias)
    counts = counts[:, 0]
    padded = (counts + MOE_BLOCK - 1) // MOE_BLOCK * MOE_BLOCK
    pend = jnp.cumsum(padded)
    pstart = (pend - padded).astype(I32)
    dest = _slots(eidx, rank, pstart)
    n_blocks = -(-(T * TOP_K + N_EXPERTS * (MOE_BLOCK - 1)) // MOE_BLOCK)
    first_block = jnp.concatenate([jnp.zeros((1,), I32), (pend // MOE_BLOCK).astype(I32)])
    xs = _sc_scatter_rows(u, dest, n_blocks * MOE_BLOCK)
    y = _experts(xs, first_block, w_in, w_out, layer)
    return _combine(_sc_gather_rows(y, dest), gw.T, u, ws_in, ws_out, h, gate, gate_index, ln_g, ln_b)


def _seg_ones(width=LANES):
    idx = np.arange(width) // HEAD_DIM
    return jnp.asarray((idx[:, None] == idx[None, :]).astype(np.float32), BF16)


def _head_sum(x, ones_ref):
    outs = []
    for j in range(x.shape[1] // LANES):
        xc = x[:, j * LANES:(j + 1) * LANES]
        hi = xc.astype(BF16)
        lo = (xc - hi.astype(F32)).astype(BF16)
        outs.append(_dot(hi, ones_ref[...]) + _dot(lo, ones_ref[...]))
    return jnp.concatenate(outs, axis=1)


def _rwkv_proj_kernel(seg_tiles, h_ref, hp_ref, hn_ref, mod_ref, mu_ref, wrkv_ref, g1_ref, g2_ref, d1_ref, d2_ref,
                      d0_ref, i1_ref, i2_ref, i0_ref, kk_ref, ka_ref, rk_ref, ones_ref,
                      r_ref, v_ref, a_ref, g_ref, bonus_ref, w_ref, k_ref, b_ref):
    i = pl.program_id(0)
    nb = hp_ref.shape[0]
    shift, scale = mod_ref[0, 0], mod_ref[0, 1]
    u = h_ref[...] * (1.0 + scale) + shift
    starts = jnp.logical_or(i == 0, i == seg_tiles)
    ends = jnp.logical_or(i == seg_tiles - 1, i == pl.num_programs(0) - 1)
    u_before = (hp_ref[...] * (1.0 + scale[:nb]) + shift[:nb]) * jnp.where(starts, 0.0, 1.0)
    u_after = (hn_ref[...] * (1.0 + scale[:nb]) + shift[:nb]) * jnp.where(ends, 0.0, 1.0)
    dx = 0.5 * (jnp.concatenate([u_before, u[:-nb]], axis=0) + jnp.concatenate([u[nb:], u_after], axis=0)) - u
    mix = lambda m: (u + dx * mu_ref[m:m + 1, :])
    xr, xw, xk, xv, xa, xg = [mix(m) for m in range(6)]
    r = _dot(xr.astype(BF16), wrkv_ref[0])
    k = _dot(xk.astype(BF16), wrkv_ref[1])
    v = _dot(xv.astype(BF16), wrkv_ref[2])
    g = _dot(jax.nn.sigmoid(_dot(xg.astype(BF16), g1_ref[...])).astype(BF16), g2_ref[...])
    kk = k * kk_ref[...]
    kk = kk * lax.rsqrt(jnp.maximum(_head_sum(kk * kk, ones_ref), 1e-24))
    r_ref[...] = r
    v_ref[...] = v
    a_ref[...] = -kk
    g_ref[...] = g
    k_sum = None
    xw_b = xw.astype(BF16)
    xa_b = xa.astype(BF16)
    for d in range(2):
        lw = d0_ref[d:d + 1, :] + _dot(jnp.tanh(_dot(xw_b, d1_ref[d])).astype(BF16), d2_ref[d])
        softplus = jnp.maximum(-lw, 0.0) + jnp.log(1.0 + jnp.exp(-jnp.abs(lw)))
        logw = -softplus - 0.5
        w_ref[d] = jnp.exp(-jnp.exp(logw))
        eta = jax.nn.sigmoid(i0_ref[d:d + 1, :] + _dot(_dot(xa_b, i1_ref[d]).astype(BF16), i2_ref[d]))
        k_d = k * (1.0 + (eta - 1.0) * ka_ref[...])
        k_ref[d] = k_d
        b_ref[d] = kk * eta
        k_sum = k_d if k_sum is None else k_sum + k_d
    bonus_ref[...] = _head_sum(r * k_sum * rk_ref[...], ones_ref) * v


def _rwkv_proj(h, mod_rows, batch, n_ctx, p):
    T, D = h.shape
    tm = PROJ_TILE
    per_tile = tm // batch
    seg_tiles = n_ctx // per_tile
    n_steps = T // batch
    row = pl.BlockSpec((tm, D), lambda i: (i, 0))
    before = pl.BlockSpec((batch, D), lambda i: (jnp.maximum(i * per_tile - 1, 0), 0))
    after = pl.BlockSpec((batch, D), lambda i: (jnp.minimum((i + 1) * per_tile, n_steps - 1), 0))
    mod_spec = pl.BlockSpec((1, 2, tm, D), lambda i: (jnp.minimum(i // seg_tiles, 1), 0, 0, 0))
    row2 = pl.BlockSpec((2, tm, D), lambda i: (0, i, 0))
    full = lambda a: pl.BlockSpec(a.shape, lambda i: (0,) * a.ndim)
    bf = lambda a: a.astype(BF16)
    consts = [p['mu'], bf(p['w_rkv']), bf(p['gate1']), bf(p['gate2']), bf(p['dec1']), bf(p['dec2']), p['dec0'],
              bf(p['icl1']), bf(p['icl2']), p['icl0'], p['k_k'].reshape(1, D), p['k_a'].reshape(1, D),
              p['r_k'].reshape(1, D), _seg_ones()]
    one = jax.ShapeDtypeStruct((T, D), F32)
    two = jax.ShapeDtypeStruct((2, T, D), F32)
    return pl.pallas_call(
        functools.partial(_rwkv_proj_kernel, seg_tiles),
        grid=(T // tm,),
        in_specs=[row, before, after, mod_spec] + [full(a) for a in consts],
        out_specs=[row, row, row, row, row, row2, row2, row2],
        out_shape=[one, one, one, one, one, two, two, two],
        compiler_params=_cparams(("arbitrary",)),
    )(h, h, h, mod_rows, *consts)


def _scan_kernel(r_ref, w_ref, k_ref, v_ref, a_ref, b_ref, ones_ref, hsel_ref,
                 y_ref, s_ref, vt_ref):
    d = pl.program_id(0)
    c = pl.program_id(2)
    tc, nb = r_ref.shape[0], r_ref.shape[1]
    tw = SCAN_TILE
    n_wide = r_ref.shape[2] // tw
    heads = tw // HEAD_DIM
    assert heads * tc == tw

    @pl.when(c == 0)
    def _():
        s_ref[...] = jnp.zeros_like(s_ref)

    for bb in range(nb):
        for q in range(n_wide):
            vt = v_ref[:, bb, q * tw:(q + 1) * tw].T
            vt_ref[bb * n_wide + q] = jnp.concatenate(
                [vt[h * HEAD_DIM:(h + 1) * HEAD_DIM] for h in range(heads)], axis=1)

    head_base = (lax.broadcasted_iota(I32, (HEAD_DIM, LANES), 1) // HEAD_DIM) * tc
    tiles = [(bb, q) for bb in range(nb) for q in range(n_wide)]
    groups = [tiles[i:i + SCAN_GROUP] for i in range(0, len(tiles), SCAN_GROUP)]

    def stacked(grp, get, dtype=F32):
        def wide(bb, q):
            return jnp.concatenate(
                [jnp.broadcast_to(get(bb, slice(q * tw + hf * LANES, q * tw + (hf + 1) * LANES)).astype(dtype),
                                  (HEAD_DIM, LANES)) for hf in range(tw // LANES)], axis=1)
        return jnp.concatenate([wide(bb, q) for bb, q in grp], axis=0)

    def load_state(grp):
        return jnp.concatenate([s_ref[bb * n_wide + q] for bb, q in grp], axis=0)

    def emit_y(grp, st_b, t_y):
        r_rows = stacked(grp, lambda bb, cols: r_ref[t_y, bb:bb + 1, cols], BF16)
        yh = _dot_nt(hsel_ref[...], st_b * r_rows)
        first = tiles.index(grp[0])
        y_ref[0, t_y, :, first * HEAD_DIM:(first + len(grp)) * HEAD_DIM] = yh[:heads]

    def step(s_i, carry):
        t = jnp.where(d == 0, s_i, tc - 1 - s_i)
        t_prev = jnp.where(s_i == 0, t, jnp.where(d == 0, t - 1, t + 1))
        pick = head_base + t
        for grp in groups:
            one = lambda ref: stacked(grp, lambda bb, cols: ref[t, bb:bb + 1, cols])
            two = lambda ref: stacked(grp, lambda bb, cols: ref[0, t, bb:bb + 1, cols])
            st = load_state(grp)
            st_b = st.astype(BF16)
            a_rows = stacked(grp, lambda bb, cols: a_ref[t, bb:bb + 1, cols], BF16)
            sa = _dot(st_b * a_rows, ones_ref[...])
            emit_y(grp, st_b, t_prev)
            vcol = jnp.concatenate(
                [jnp.concatenate([jnp.take_along_axis(vt_ref[bb * n_wide + q, :, hf * LANES:(hf + 1) * LANES],
                                                      pick, axis=1) for hf in range(tw // LANES)], axis=1)
                 for bb, q in grp], axis=0)
            st = st * two(w_ref) + sa * two(b_ref) + vcol * two(k_ref)
            for j, (bb, q) in enumerate(grp):
                s_ref[bb * n_wide + q] = st[j * HEAD_DIM:(j + 1) * HEAD_DIM]
        return carry

    lax.fori_loop(0, tc, step, 0, unroll=SCAN_UNROLL)
    t_last = jnp.where(d == 0, tc - 1, 0)
    for grp in groups:
        emit_y(grp, load_state(grp).astype(BF16), t_last)


def _wkv_scan(r, w, k, v, a, b, n_ctx):
    N, B, D = r.shape
    tc = SCAN_CHUNK
    wc = SCAN_COLS
    n_wide = wc // SCAN_TILE
    nc = N // tc
    ncc = n_ctx // tc

    def chunk(d, c):
        rev = jnp.where(c < ncc, ncc - 1 - c, nc - 1 - (c - ncc))
        return jnp.where(d == 0, c, rev)

    one = pl.BlockSpec((tc, B, wc), lambda d, g, c: (chunk(d, c), 0, g))
    two = pl.BlockSpec((1, tc, B, wc), lambda d, g, c: (d, chunk(d, c), 0, g))
    seg = np.arange(SCAN_TILE) // HEAD_DIM
    hsel = np.zeros((8, SCAN_TILE), np.float32)
    for hh in range(SCAN_TILE // HEAD_DIM):
        hsel[hh, seg == hh] = 1.0
    const = lambda a_: pl.BlockSpec(a_.shape, lambda d, g, c: (0, 0))
    consts = [_seg_ones(SCAN_TILE), jnp.asarray(hsel, BF16)]
    heads = SCAN_TILE // HEAD_DIM
    ncg = D // wc
    y = pl.pallas_call(
        _scan_kernel,
        grid=(2, ncg, nc),
        in_specs=[one, two, two, one, one, two] + [const(a_) for a_ in consts],
        out_specs=pl.BlockSpec((1, tc, heads, B * n_wide * HEAD_DIM), lambda d, g, c: (d, chunk(d, c), 0, g)),
        out_shape=jax.ShapeDtypeStruct((2, N, heads, ncg * B * n_wide * HEAD_DIM), F32),
        scratch_shapes=[pltpu.VMEM((B * n_wide, HEAD_DIM, SCAN_TILE), F32),
                        pltpu.VMEM((B * n_wide, HEAD_DIM, SCAN_TILE), F32)],
        compiler_params=_cparams(("arbitrary", "arbitrary", "arbitrary")),
    )(r, w, k, v, a, b, *consts)
    y = y.reshape(2, N, heads, ncg, B, n_wide * HEAD_DIM)
    return jnp.transpose(y, (0, 1, 4, 2, 3, 5)).reshape(2, N, B, D)


def _scan_head_order(d):
    heads = SCAN_TILE // HEAD_DIM
    n_wide = SCAN_COLS // SCAN_TILE
    ncg = d // SCAN_COLS
    order = []
    for g in range(ncg):
        for q in range(n_wide):
            for h in range(heads):
                order.append((h * ncg + g) * n_wide + q)
    return tuple(order)


def _rwkv_out_kernel(head_order, y0_ref, y1_ref, bonus_ref, g_ref, lnx_ref, ones_ref, w_ref, h_ref, mod_ref,
                     lg_ref, lb_ref, hn_ref, u_ref):
    y_in = y0_ref[0] + y1_ref[0]
    y = jnp.concatenate([y_in[:, p * HEAD_DIM:(p + 1) * HEAD_DIM] for p in head_order], axis=1)
    ym = _head_sum(y, ones_ref) * (1.0 / HEAD_DIM)
    yc = y - ym
    yv = _head_sum(yc * yc, ones_ref) * (1.0 / HEAD_DIM)
    yn = yc * lax.rsqrt(yv + LNX_EPS) * lnx_ref[0:1, :] + lnx_ref[1:2, :]
    x = ((yn + bonus_ref[...]) * g_ref[...]).astype(BF16)
    o = _dot(x, w_ref[...])
    z = DEEPNORM_ALPHA * h_ref[...] + mod_ref[0] * o
    hn = _layer_norm(z, lg_ref[...], lb_ref[...])
    hn_ref[...] = hn
    u_ref[...] = _pack_halves(hn * (1.0 + mod_ref[2]) + mod_ref[1])


def _rwkv_out(y, bonus, g, lnx, w_out, h, mod_rows, ln_g, ln_b, row0):
    T, D = h.shape
    tm = ROW_TILE
    t0 = row0 // tm
    off = pl.BlockSpec((tm, D), lambda i: (i + t0, 0))
    out = pl.BlockSpec((tm, D), lambda i: (i, 0))
    full = lambda a: pl.BlockSpec(a.shape, lambda i: (0,) * a.ndim)
    vec = pl.BlockSpec((1, D), lambda i: (0, 0))
    ones = _seg_ones()
    w_b = w_out.astype(BF16)
    return pl.pallas_call(
        functools.partial(_rwkv_out_kernel, _scan_head_order(D)),
        grid=((T - row0) // tm,),
        in_specs=[pl.BlockSpec((1, tm, D), lambda i: (0, i + t0, 0)),
                  pl.BlockSpec((1, tm, D), lambda i: (1, i + t0, 0)),
                  off, off, full(lnx), full(ones), full(w_b), off, full(mod_rows), vec, vec],
        out_specs=[out, pl.BlockSpec((tm, D // 2), lambda i: (i, 0))],
        out_shape=[jax.ShapeDtypeStruct((T - row0, D), F32), jax.ShapeDtypeStruct((T - row0, D // 2), I32)],
        compiler_params=_cparams(("arbitrary",)),
    )(y, y, bonus, g, lnx, ones, w_b, h, mod_rows, ln_g.reshape(1, D), ln_b.reshape(1, D))


def kernel(x, c, ctx, c_ctx, ada_w, ada_b, post_ln_g, post_ln_b, att_w_in, att_w_out, att_sink, diff_lambda_vecs, diff_subln_g, rk_mu, rk_w_rkv, rk_w_out, rk_decay0, rk_decay1, rk_decay2, rk_iclr0, rk_iclr1, rk_iclr2, rk_gate1, rk_gate2, rk_k_k, rk_k_a, rk_r_k, rk_lnx, moe_router, moe_bias, moe_w_in, moe_w_out, moe_ws_in, moe_ws_out):
    B, S, D = x.shape
    L = ctx.shape[1]
    N = L + S
    tm = ROW_TILE
    assert L % tm == 0 and S % tm == 0 and L % SCAN_CHUNK == 0 and S % SCAN_CHUNK == 0
    assert tm % B == 0 and PROJ_TILE % B == 0 and L % (PROJ_TILE // B) == 0 and D % SCAN_COLS == 0

    rows = -(-(B + 1) // 8) * 8
    cvec = jnp.concatenate([c, c_ctx[None, :], jnp.zeros((rows - B - 1, D), F32)], axis=0)
    mods = [_mod_table(_ada_mod(cvec, ada_w[i], ada_b[i]), B, D) for i in range(DEPTH)]

    h0 = jnp.concatenate([ctx, x], axis=1)
    lam_init = 0.8 - 0.6 * math.exp(-0.3 * 0)
    qa, ka, va, qb, kb, vb = _attn_inproj(h0, mods[0], att_w_in[0], L)
    oa = _win_attn(qa, ka, va, att_sink[0], L)
    ob = _diff_attn(qb, kb, vb, diff_lambda_vecs[0], diff_subln_g[0], lam_init, L)
    h1, u1 = _mix_out([oa, ob], [att_w_out[0][:A_WIDTH], att_w_out[0][A_WIDTH:]], h0, mods[0],
                      post_ln_g[0, 0], post_ln_b[0, 0], L, 0)
    tiles_b, tiles_c = N // tm, L // tm
    gate0 = mods[0][:, :, 5].reshape(B * 2, 1, D)
    gate0_index = lambda i: (i // tiles_b) * 2 + jnp.minimum((i % tiles_b) // tiles_c, 1)
    h2 = _moe_layer(u1.reshape(B * N, D // 2), h1.reshape(B * N, D), gate0, gate0_index, moe_router[0], moe_bias[0],
                    moe_w_in, moe_w_out, moe_ws_in[0], moe_ws_out[0],
                    post_ln_g[0, 1], post_ln_b[0, 1], 0).reshape(B, N, D)

    m_ctx, m_lat = mods[1][:, 0], mods[1][:, 1]
    h2_t = jnp.swapaxes(h2, 0, 1).reshape(N * B, D)
    rows_of = lambda m, j, n: jnp.tile(m[:, j], (n // B, 1))
    proj_mod = jnp.stack([jnp.stack([rows_of(m, 0, PROJ_TILE), rows_of(m, 1, PROJ_TILE)]) for m in (m_ctx, m_lat)])
    params = dict(mu=rk_mu[0], w_rkv=rk_w_rkv[0], gate1=rk_gate1[0], gate2=rk_gate2[0],
                  dec0=rk_decay0[0], dec1=rk_decay1[0], dec2=rk_decay2[0],
                  icl0=rk_iclr0[0], icl1=rk_iclr1[0], icl2=rk_iclr2[0],
                  k_k=rk_k_k[0], k_a=rk_k_a[0], r_k=rk_r_k[0])
    r, v, a, g, bonus, w2, k2, b2 = _rwkv_proj(h2_t, proj_mod, B, L, params)
    tmaj = lambda t: t.reshape(t.shape[:-2] + (N, B, D))
    y = _wkv_scan(tmaj(r), tmaj(w2), tmaj(k2), tmaj(v), tmaj(a), tmaj(b2), L)
    lat_rows = lambda j: rows_of(m_lat, j, tm)
    h3, u3 = _rwkv_out(y.reshape(2, N * B, D), bonus, g, rk_lnx[0], rk_w_out[0], h2_t,
                       jnp.stack([lat_rows(2), lat_rows(3), lat_rows(4)]),
                       post_ln_g[1, 0], post_ln_b[1, 0], L * B)
    out = _moe_layer(u3, h3, lat_rows(5)[None], lambda i: 0, moe_router[1], moe_bias[1],
                     moe_w_in, moe_w_out, moe_ws_in[1], moe_ws_out[1],
                     post_ln_g[1, 1], post_ln_b[1, 1], 1)
    return jnp.swapaxes(out.reshape(S, B, D), 0, 1)
```

```python
import functools
import math

import numpy as np
import jax
import jax.numpy as jnp
from jax import lax
from jax.experimental import pallas as pl
from jax.experimental.pallas import tpu as pltpu
from jax.experimental.pallas import tpu_sc as plsc

F32 = jnp.float32
BF16 = jnp.bfloat16
I32 = jnp.int32

HEAD_DIM = 64
GRID_W = 64
ROPE_AXIS_DIM = HEAD_DIM // 2
ROPE_THETA = 10000.0
Q_BLOCK = 128
A_Q_HEADS = 8
A_KV_HEADS = 2
A_GROUP = A_Q_HEADS // A_KV_HEADS
A_WIDTH = A_Q_HEADS * HEAD_DIM
A_KV_WIDTH = A_KV_HEADS * HEAD_DIM
B_HEADS = 4
B_V_DIM = 2 * HEAD_DIM
B_WIDTH = B_HEADS * B_V_DIM
LNX_EPS = 64e-5
N_EXPERTS = 256
TOP_K = 8
N_GROUPS = 8
TOPK_GROUPS = 4
ROUTED_SCALE = 2.5
MOE_BLOCK = 256
EXPERT_IN_SLOTS = 4
EXPERT_OUT_SLOTS = 2
LN_EPS = 1e-5
SUBLN_EPS = 1e-5
NEG_INF = -1e30
DEPTH = 2
DEEPNORM_ALPHA = (2 * DEPTH) ** 0.25

LANES = 128
ADA_COLS = 768
ROW_TILE = 256
PROJ_TILE = 128
SC_WINDOW = 128
SCAN_CHUNK = 64
SCAN_COLS = 1024
SCAN_TILE = 256
SCAN_GROUP = 8
SCAN_UNROLL = 4
VMEM_LIMIT = 56 * 1024 * 1024


def _cparams(sem):
    return pltpu.CompilerParams(dimension_semantics=sem, vmem_limit_bytes=VMEM_LIMIT)


def _silu(x):
    return x * jax.nn.sigmoid(x)


def _layer_norm(z, g, b):
    mu = jnp.mean(z, -1, keepdims=True)
    zc = z - mu
    var = jnp.mean(zc * zc, -1, keepdims=True)
    return zc * lax.rsqrt(var + LN_EPS) * g + b


def _dot(a, b):
    return jnp.dot(a, b, preferred_element_type=F32)


def _dot_nt(a, b):
    return lax.dot_general(a, b, (((1,), (1,)), ((), ())), preferred_element_type=F32)


def _pack_halves(x):
    half = x.shape[1] // 2
    bits = lambda v: lax.bitcast_convert_type(v.astype(BF16).astype(F32), I32)
    return lax.shift_right_logical(bits(x[:, :half]), 16) | bits(x[:, half:])


def _unpack_halves(p):
    lo = lax.bitcast_convert_type(lax.shift_left(p, 16), F32)
    hi = lax.bitcast_convert_type(p & jnp.int32(-65536), F32)
    return lo, hi


def _dot_halves(p, w_ref_or_array):
    lo, hi = _unpack_halves(p)
    half = p.shape[1]
    return _dot(lo.astype(BF16), w_ref_or_array[:half]) + _dot(hi.astype(BF16), w_ref_or_array[half:])


def _ada_kernel(c_ref, w_ref, b_ref, o_ref):
    c = c_ref[...]
    o_ref[...] = _dot(_silu(c).astype(BF16), w_ref[...].astype(BF16)) + b_ref[...]


def _ada_mod(cvec, w, bias):
    R, D = cvec.shape
    n_out = w.shape[1]
    tn = ADA_COLS
    return pl.pallas_call(
        _ada_kernel,
        grid=(n_out // tn,),
        in_specs=[pl.BlockSpec((R, D), lambda j: (0, 0)),
                  pl.BlockSpec((D, tn), lambda j: (0, j)),
                  pl.BlockSpec((1, tn), lambda j: (0, j))],
        out_specs=pl.BlockSpec((R, tn), lambda j: (0, j)),
        out_shape=jax.ShapeDtypeStruct((R, n_out), F32),
        compiler_params=_cparams(("arbitrary",)),
    )(cvec, w, bias.reshape(1, n_out))


def _mod_table(m, batch, d):
    m_lat = m[:batch].reshape(batch, 6, d)
    m_ctx = jnp.broadcast_to(m[batch].reshape(1, 6, d), (batch, 6, d))
    return jnp.stack([m_ctx, m_lat], axis=1)


def _mod_spec(d, ctx_tiles):
    return pl.BlockSpec((1, 1, 6, d), lambda b, i: (b, jnp.minimum(i // ctx_tiles, 1), 0, 0))


def _rope_tables(n_ctx, n_lat):
    rows = n_lat // GRID_W
    row = np.repeat(np.arange(rows), GRID_W).astype(np.float32)
    col = np.tile(np.arange(GRID_W), rows).astype(np.float32)
    inv = (ROPE_THETA ** (-np.arange(0, ROPE_AXIS_DIM, 2, dtype=np.float32) / ROPE_AXIS_DIM)).astype(np.float32)
    ar = row[:, None] * inv
    ac = col[:, None] * inv
    ang = np.concatenate([ar, ar, ac, ac], -1)
    cos = np.cos(ang).astype(np.float32)
    sin = np.sin(ang).astype(np.float32)
    lower = (np.arange(HEAD_DIM) % ROPE_AXIS_DIM) < (ROPE_AXIS_DIM // 2)
    sin_up = np.where(lower[None, :], -sin, 0.0)
    sin_dn = np.where(lower[None, :], 0.0, sin)

    def full(t, ctx_fill):
        t = np.concatenate([np.full((n_ctx, HEAD_DIM), ctx_fill, np.float32), t], 0)
        return jnp.asarray(np.tile(t, (1, LANES // HEAD_DIM)))

    return full(cos, 1.0), full(sin_up, 0.0), full(sin_dn, 0.0)


def _inproj_kernel(h_ref, mod_ref, w_ref, cos_ref, su_ref, sd_ref,
                   qa_ref, ka_ref, va_ref, qb_ref, kb_ref, vb_ref):
    h = h_ref[0]
    shift = mod_ref[0, 0, 0:1, :]
    scale = mod_ref[0, 0, 1:2, :]
    u = (h * (1.0 + scale) + shift).astype(BF16)
    y = _dot(u, w_ref[...])
    cos, s_up, s_dn = cos_ref[...], su_ref[...], sd_ref[...]
    q_scale = HEAD_DIM ** -0.5

    def rope(xc):
        half = ROPE_AXIS_DIM // 2
        return xc * cos + pltpu.roll(xc, LANES - half, 1) * s_up + pltpu.roll(xc, half, 1) * s_dn

    def emit(out_ref, col0, width, roped, mul):
        for j in range(width // LANES):
            xc = y[:, col0 + j * LANES: col0 + (j + 1) * LANES]
            if roped:
                xc = rope(xc)
            if mul != 1.0:
                xc = xc * mul
            out_ref[0, :, j * LANES:(j + 1) * LANES] = xc.astype(out_ref.dtype)

    c = 0
    emit(qa_ref, c, A_WIDTH, True, q_scale); c += A_WIDTH
    emit(ka_ref, c, A_KV_WIDTH, True, 1.0); c += A_KV_WIDTH
    emit(va_ref, c, A_KV_WIDTH, False, 1.0); c += A_KV_WIDTH
    emit(qb_ref, c, B_WIDTH, True, q_scale); c += B_WIDTH
    emit(kb_ref, c, B_WIDTH, True, 1.0); c += B_WIDTH
    emit(vb_ref, c, B_WIDTH, False, 1.0)


def _attn_inproj(h, mod, w_in, n_ctx):
    B, N, D = h.shape
    tm = ROW_TILE
    cos, s_up, s_dn = _rope_tables(n_ctx, N - n_ctx)
    widths = (A_WIDTH, A_KV_WIDTH, A_KV_WIDTH, B_WIDTH, B_WIDTH, B_WIDTH)
    tab_spec = pl.BlockSpec((tm, LANES), lambda b, i: (i, 0))
    return pl.pallas_call(
        _inproj_kernel,
        grid=(B, N // tm),
        in_specs=[pl.BlockSpec((1, tm, D), lambda b, i: (b, i, 0)),
                  _mod_spec(D, n_ctx // tm),
                  pl.BlockSpec(w_in.shape, lambda b, i: (0, 0)),
                  tab_spec, tab_spec, tab_spec],
        out_specs=[pl.BlockSpec((1, tm, w), lambda b, i: (b, i, 0)) for w in widths],
        out_shape=[jax.ShapeDtypeStruct((B, N, w), BF16) for w in widths],
        compiler_params=_cparams(("arbitrary", "arbitrary")),
    )(h, mod, w_in.astype(BF16), cos, s_up, s_dn)


def _win_attn_kernel(n_ctx_blocks, n_blocks, q_ref, kc_ref, vc_ref, kl_ref, km_ref, kr_ref,
                     vl_ref, vm_ref, vr_ref, sink_ref, o_ref):
    j = pl.program_id(1)
    is_lat = j >= n_ctx_blocks
    qb = Q_BLOCK
    n_c = kc_ref.shape[1]
    rows = A_GROUP * qb
    n_keys = n_c + 3 * qb
    far = 1 << 20
    r_idx = lax.broadcasted_iota(I32, (rows, n_keys), 0) % qb
    cw = lax.broadcasted_iota(I32, (rows, n_keys), 1) - n_c
    off_l = jnp.where(jnp.logical_and(is_lat, j > n_ctx_blocks), 0, far)
    end_m = jnp.where(is_lat, 2 * qb, qb)
    off_r = jnp.where(jnp.logical_and(is_lat, j < n_blocks - 1), 0, far)
    valid = ((cw < 0)
             | ((cw >= 0) & (cw < qb) & (cw >= r_idx + off_l))
             | ((cw >= qb) & (cw < end_m))
             | ((cw >= 2 * qb) & (cw - 2 * qb + off_r <= r_idx)))
    outs = []
    for kv in range(A_KV_HEADS):
        cols = slice(kv * HEAD_DIM, (kv + 1) * HEAD_DIM)
        k_all = jnp.concatenate([kc_ref[0, :, cols], kl_ref[0, :, cols], km_ref[0, :, cols],
                                 kr_ref[0, :, cols]], axis=0)
        v_all = jnp.concatenate([vc_ref[0, :, cols], vl_ref[0, :, cols], vm_ref[0, :, cols],
                                 vr_ref[0, :, cols]], axis=0)
        q0 = kv * A_GROUP
        q = jnp.concatenate([q_ref[0, :, (q0 + g) * HEAD_DIM:(q0 + g + 1) * HEAD_DIM]
                             for g in range(A_GROUP)], axis=0)
        sink = jnp.concatenate([jnp.broadcast_to(sink_ref[q0 + g:q0 + g + 1, 0:1], (qb, 1))
                                for g in range(A_GROUP)], axis=0)
        s = jnp.where(valid, _dot_nt(q, k_all), NEG_INF)
        m = jnp.maximum(jnp.max(s, -1, keepdims=True), sink)
        e = jnp.exp(s - m)
        denom = jnp.sum(e, -1, keepdims=True) + jnp.exp(sink - m)
        o = _dot(e.astype(BF16), v_all) * (1.0 / denom)
        outs += [o[g * qb:(g + 1) * qb] for g in range(A_GROUP)]
    for j2 in range(A_Q_HEADS // 2):
        pair = jnp.concatenate([outs[2 * j2], outs[2 * j2 + 1]], axis=1)
        o_ref[0, :, j2 * LANES:(j2 + 1) * LANES] = pair.astype(o_ref.dtype)


def _win_attn(qa, ka, va, sink, n_ctx):
    B, N, _ = qa.shape
    qb = Q_BLOCK
    nb = N // qb
    ncb = n_ctx // qb
    sink_pad = jnp.broadcast_to(sink.reshape(A_Q_HEADS, 1).astype(F32), (A_Q_HEADS, LANES))

    def left(b, j):
        return (b, jnp.clip(j - 1, ncb, nb - 1), 0)

    def mid(b, j):
        return (b, jnp.clip(j, ncb, nb - 1), 0)

    def right(b, j):
        return (b, jnp.clip(j + 1, ncb, nb - 1), 0)

    kv_blk = lambda im: pl.BlockSpec((1, qb, A_KV_WIDTH), im)
    ctx_blk = pl.BlockSpec((1, n_ctx, A_KV_WIDTH), lambda b, j: (b, 0, 0))
    return pl.pallas_call(
        functools.partial(_win_attn_kernel, ncb, nb),
        grid=(B, nb),
        in_specs=[pl.BlockSpec((1, qb, A_WIDTH), lambda b, j: (b, j, 0)),
                  ctx_blk, ctx_blk,
                  kv_blk(left), kv_blk(mid), kv_blk(right),
                  kv_blk(left), kv_blk(mid), kv_blk(right),
                  pl.BlockSpec((A_Q_HEADS, LANES), lambda b, j: (0, 0))],
        out_specs=pl.BlockSpec((1, qb, A_WIDTH), lambda b, j: (b, j, 0)),
        out_shape=jax.ShapeDtypeStruct((B, N, A_WIDTH), BF16),
        compiler_params=_cparams(("arbitrary", "arbitrary")),
    )(qa, ka, va, ka, ka, ka, va, va, va, sink_pad)


def _diff_attn_kernel(n_ctx, lam_init, q_ref, k_ref, v_ref, lv_ref, g_ref, o_ref):
    j = pl.program_id(1)
    lv = lv_ref[...]
    lam = (jnp.exp(jnp.sum(lv[0:1] * lv[1:2], -1, keepdims=True))
           - jnp.exp(jnp.sum(lv[2:3] * lv[3:4], -1, keepdims=True)) + lam_init)
    gain = g_ref[...] * (1.0 - lam_init)

    def run(n_keys):
        for hd in range(B_HEADS):
            parts = []
            for mm in range(2):
                c0 = (hd * 2 + mm) * HEAD_DIM
                q = q_ref[0, :, c0:c0 + HEAD_DIM]
                k = k_ref[0, :n_keys, c0:c0 + HEAD_DIM]
                s = _dot_nt(q, k)
                e = jnp.exp(s - jnp.max(s, -1, keepdims=True))
                parts.append((e, jnp.sum(e, -1, keepdims=True)))
            (e0, l0), (e1, l1) = parts
            v = v_ref[0, :n_keys, hd * B_V_DIM:(hd + 1) * B_V_DIM]
            o = _dot(e0.astype(BF16), v) * (1.0 / l0) - _dot(e1.astype(BF16), v) * (lam / l1)
            o = o * lax.rsqrt(jnp.mean(o * o, -1, keepdims=True) + SUBLN_EPS) * gain
            o_ref[0, :, hd * B_V_DIM:(hd + 1) * B_V_DIM] = o.astype(o_ref.dtype)

    @pl.when(j == 0)
    def _():
        run(n_ctx)

    @pl.when(j > 0)
    def _():
        run(k_ref.shape[1])


def _diff_attn(qb, kb, vb, lam_vecs, subln_g, lam_init, n_ctx):
    B, N, _ = qb.shape
    tq = n_ctx
    return pl.pallas_call(
        functools.partial(_diff_attn_kernel, n_ctx, lam_init),
        grid=(B, N // tq),
        in_specs=[pl.BlockSpec((1, tq, B_WIDTH), lambda b, j: (b, j, 0)),
                  pl.BlockSpec((1, N, B_WIDTH), lambda b, j: (b, 0, 0)),
                  pl.BlockSpec((1, N, B_WIDTH), lambda b, j: (b, 0, 0)),
                  pl.BlockSpec((4, HEAD_DIM), lambda b, j: (0, 0)),
                  pl.BlockSpec((1, B_V_DIM), lambda b, j: (0, 0))],
        out_specs=pl.BlockSpec((1, tq, B_WIDTH), lambda b, j: (b, j, 0)),
        out_shape=jax.ShapeDtypeStruct((B, N, B_WIDTH), BF16),
        compiler_params=_cparams(("arbitrary", "arbitrary")),
    )(qb, kb, vb, lam_vecs.astype(F32), subln_g.reshape(1, B_V_DIM).astype(F32))


def _mix_out_kernel(n_in, *refs):
    xs = refs[:n_in]
    ws = refs[n_in:2 * n_in]
    h_ref, mod_ref, g_ref, b_ref, hn_ref, u_ref = refs[2 * n_in:]
    o = _dot(xs[0][0], ws[0][...])
    for x_ref, w_ref in zip(xs[1:], ws[1:]):
        o = o + _dot(x_ref[0], w_ref[...])
    z = DEEPNORM_ALPHA * h_ref[0] + mod_ref[0, 0, 2:3, :] * o
    hn = _layer_norm(z, g_ref[...], b_ref[...])
    hn_ref[0] = hn
    u_ref[0] = _pack_halves(hn * (1.0 + mod_ref[0, 0, 4:5, :]) + mod_ref[0, 0, 3:4, :])


def _mix_out(xs, ws, h, mod, ln_g, ln_b, n_ctx, row0):
    B, N, D = h.shape
    tm = ROW_TILE
    t0 = row0 // tm
    n_out = N - row0
    row_spec = lambda w: pl.BlockSpec((1, tm, w), lambda b, i: (b, i + t0, 0))
    out_spec = pl.BlockSpec((1, tm, D), lambda b, i: (b, i, 0))
    vec_spec = pl.BlockSpec((1, D), lambda b, i: (0, 0))
    return pl.pallas_call(
        functools.partial(_mix_out_kernel, len(xs)),
        grid=(B, n_out // tm),
        in_specs=([row_spec(x.shape[-1]) for x in xs]
                  + [pl.BlockSpec(w.shape, lambda b, i: (0, 0)) for w in ws]
                  + [row_spec(D),
                     pl.BlockSpec((1, 1, 6, D), lambda b, i: (b, jnp.minimum((i + t0) // (n_ctx // tm), 1), 0, 0)),
                     vec_spec, vec_spec]),
        out_specs=[out_spec, pl.BlockSpec((1, tm, D // 2), lambda b, i: (b, i, 0))],
        out_shape=[jax.ShapeDtypeStruct((B, n_out, D), F32), jax.ShapeDtypeStruct((B, n_out, D // 2), I32)],
        compiler_params=_cparams(("arbitrary", "arbitrary")),
    )(*xs, *[w.astype(BF16) for w in ws], h, mod, ln_g.reshape(1, D), ln_b.reshape(1, D))


def _router_kernel(u_ref, rt_ref, bias_ref, tri_ref, e_ref, gw_ref, rank_ref, cnt_ref, carry_ref):
    i = pl.program_id(0)

    @pl.when(i == 0)
    def _():
        carry_ref[...] = jnp.zeros_like(carry_ref)

    tm = u_ref.shape[0]
    per_group = N_EXPERTS // N_GROUPS
    neg = -jnp.inf
    u_lo, u_hi = _unpack_halves(u_ref[...])
    half = u_ref.shape[1]
    logits = (_dot_nt(rt_ref[:, :half], u_lo.astype(BF16))
              + _dot_nt(rt_ref[:, half:], u_hi.astype(BF16)))
    scores = jax.nn.sigmoid(logits)
    sel = scores + bias_ref[...]
    io_in = lax.broadcasted_iota(I32, (per_group, tm), 0)
    grp_rows = []
    for gi in range(N_GROUPS):
        sg = sel[gi * per_group:(gi + 1) * per_group]
        m1 = jnp.max(sg, axis=0, keepdims=True)
        i1 = jnp.min(jnp.where(sg == m1, io_in, per_group), axis=0, keepdims=True)
        m2 = jnp.max(jnp.where(io_in == i1, neg, sg), axis=0, keepdims=True)
        grp_rows.append(m1 + m2)
    grp = jnp.concatenate(grp_rows, axis=0)
    io_g = lax.broadcasted_iota(I32, grp.shape, 0)
    g_sel = jnp.zeros(grp.shape, F32)
    for _ in range(TOPK_GROUPS):
        m = jnp.max(grp, axis=0, keepdims=True)
        hit = io_g == jnp.min(jnp.where(grp == m, io_g, N_GROUPS), axis=0, keepdims=True)
        g_sel = jnp.where(hit, 1.0, g_sel)
        grp = jnp.where(hit, neg, grp)
    selm = jnp.concatenate(
        [jnp.where(g_sel[gi:gi + 1] > 0.5, sel[gi * per_group:(gi + 1) * per_group], NEG_INF)
         for gi in range(N_GROUPS)], axis=0)
    io_e = lax.broadcasted_iota(I32, selm.shape, 0)
    chosen_f = jnp.zeros(selm.shape, F32)
    idx, gws = [], []
    for _ in range(TOP_K):
        m = jnp.max(selm, axis=0, keepdims=True)
        ik = jnp.min(jnp.where(selm == m, io_e, N_EXPERTS), axis=0, keepdims=True)
        hit = io_e == ik
        idx.append(ik)
        gws.append(jnp.sum(jnp.where(hit, scores, 0.0), axis=0, keepdims=True))
        chosen_f = jnp.where(hit, 1.0, chosen_f)
        selm = jnp.where(hit, neg, selm)
    gw = jnp.concatenate(gws, axis=0)
    gw_ref[...] = gw / jnp.sum(gw, axis=0, keepdims=True) * ROUTED_SCALE
    e_ref[...] = jnp.concatenate(idx, axis=0)
    before = _dot(chosen_f.astype(BF16), tri_ref[...]) + carry_ref[...]
    ranks = [jnp.sum(jnp.where(io_e == ik, before, 0.0), axis=0, keepdims=True) for ik in idx]
    rank_ref[...] = jnp.concatenate(ranks, axis=0).astype(I32)
    carry_ref[...] = carry_ref[...] + jnp.sum(chosen_f, axis=1, keepdims=True)
    cnt_ref[...] = carry_ref[...].astype(I32)


def _router(u, router, bias):
    T = u.shape[0]
    D = router.shape[0]
    tm = ROW_TILE
    tri = jnp.asarray(np.triu(np.ones((tm, tm), np.float32), 1), BF16)
    tok_spec = pl.BlockSpec((TOP_K, tm), lambda i: (0, i))
    return pl.pallas_call(
        _router_kernel,
        grid=(T // tm,),
        in_specs=[pl.BlockSpec((tm, D // 2), lambda i: (i, 0)),
                  pl.BlockSpec((N_EXPERTS, D), lambda i: (0, 0)),
                  pl.BlockSpec((N_EXPERTS, 1), lambda i: (0, 0)),
                  pl.BlockSpec((tm, tm), lambda i: (0, 0))],
        out_specs=[tok_spec, tok_spec, tok_spec, pl.BlockSpec((N_EXPERTS, 1), lambda i: (0, 0))],
        out_shape=[jax.ShapeDtypeStruct((TOP_K, T), I32), jax.ShapeDtypeStruct((TOP_K, T), F32),
                   jax.ShapeDtypeStruct((TOP_K, T), I32), jax.ShapeDtypeStruct((N_EXPERTS, 1), I32)],
        scratch_shapes=[pltpu.VMEM((N_EXPERTS, 1), F32)],
        compiler_params=_cparams(("arbitrary",)),
    )(u, router.T.astype(BF16), bias.reshape(N_EXPERTS, 1).astype(F32), tri)


def _sc_mesh():
    return plsc.VectorSubcoreMesh(core_axis_name="c", subcore_axis_name="s")


def _sc_scatter_rows(x, dest, n_rows):
    T, W = x.shape
    K = dest.shape[0]
    win = SC_WINDOW

    @functools.partial(pl.kernel, out_type=jax.ShapeDtypeStruct((n_rows, W), x.dtype), mesh=_sc_mesh(),
                       scratch_types=[])
    def scatter(x_hbm, i_hbm, o_hbm):
        def body(x_vmem, i_vmem):
            for k in range(K):
                pltpu.sync_copy(x_vmem, o_hbm.at[i_vmem.at[k]])

        pltpu.emit_pipeline(
            body,
            grid=(T // win,),
            in_specs=[pl.BlockSpec((win, W), lambda j: (j, 0), pipeline_mode=pl.Buffered(1)),
                      pl.BlockSpec((K, win), lambda j: (0, j))],
            out_specs=[],
            core_axis_name=("c", "s"),
            dimension_semantics=(pltpu.PARALLEL,),
        )(x_hbm, i_hbm)

    return scatter(x, dest)


def _sc_gather_rows(y, dest):
    K, T = dest.shape
    W = y.shape[1]
    win = SC_WINDOW

    @functools.partial(pl.kernel, out_type=jax.ShapeDtypeStruct((K * T, W), y.dtype), mesh=_sc_mesh(),
                       scratch_types=[])
    def gather(y_hbm, i_hbm, o_hbm):
        def body(i_vmem, o_vmem):
            pltpu.sync_copy(y_hbm.at[i_vmem.at[0]], o_vmem)

        pltpu.emit_pipeline(
            body,
            grid=(K * T // win,),
            in_specs=[pl.BlockSpec((1, win), lambda j: (0, j))],
            out_specs=[pl.BlockSpec((win, W), lambda j: (j, 0), pipeline_mode=pl.Buffered(1))],
            core_axis_name=("c", "s"),
            dimension_semantics=(pltpu.PARALLEL,),
        )(i_hbm, o_hbm)

    return gather(y, dest.reshape(1, K * T)).reshape(K, T, W)


def _expert_kernel(first_ref, x_hbm, wi_ref, wo_ref, y_hbm, wi_b, wo_b, xbuf, ybuf, sem_in, sem_out):
    e = pl.program_id(0)
    n_in, blk = xbuf.shape[0], xbuf.shape[1]
    n_out = ybuf.shape[0]
    g0, g1 = first_ref[e], first_ref[e + 1]
    g_end = first_ref[pl.num_programs(0)]

    def x_copy(g):
        return pltpu.make_async_copy(x_hbm.at[pl.ds(g * blk, blk)], xbuf.at[g % n_in], sem_in.at[g % n_in])

    def y_copy(g):
        return pltpu.make_async_copy(ybuf.at[g % n_out], y_hbm.at[pl.ds(g * blk, blk)], sem_out.at[g % n_out])

    for ahead in range(n_in - 1):
        @pl.when(jnp.logical_and(e == 0, g_end > ahead))
        def _():
            x_copy(ahead).start()

    @pl.when(g1 > g0)
    def _():
        wi_b[...] = wi_ref[0, 0].astype(BF16)
        wo_b[...] = wo_ref[0, 0].astype(BF16)

    ff = wo_b.shape[0]

    def block(g, carry):
        x_copy(g).wait()

        @pl.when(g + n_in - 1 < g_end)
        def _():
            x_copy(g + n_in - 1).start()

        @pl.when(g >= n_out)
        def _():
            y_copy(g - n_out).wait()

        hcat = _dot_halves(xbuf[g % n_in], wi_b)
        act = (_silu(hcat[:, :ff]) * hcat[:, ff:]).astype(BF16)
        ybuf[g % n_out] = _pack_halves(_dot(act, wo_b[...]))
        y_copy(g).start()
        return carry

    lax.fori_loop(g0, g1, block, 0)

    @pl.when(e == pl.num_programs(0) - 1)
    def _():
        for back in range(n_out, 0, -1):
            @pl.when(g_end >= back)
            def _():
                y_copy(g_end - back).wait()


def _experts(xs, first_block, w_in, w_out, layer):
    P, half = xs.shape
    n_exp, D, ff2 = w_in.shape[-3:]
    return pl.pallas_call(
        _expert_kernel,
        grid_spec=pltpu.PrefetchScalarGridSpec(
            num_scalar_prefetch=1,
            grid=(n_exp,),
            in_specs=[pl.BlockSpec(memory_space=pl.ANY),
                      pl.BlockSpec((1, 1, D, ff2), lambda e, fb: (layer, e, 0, 0)),
                      pl.BlockSpec((1, 1, ff2 // 2, D), lambda e, fb: (layer, e, 0, 0))],
            out_specs=pl.BlockSpec(memory_space=pl.ANY),
            scratch_shapes=[pltpu.VMEM((D, ff2), BF16), pltpu.VMEM((ff2 // 2, D), BF16),
                            pltpu.VMEM((EXPERT_IN_SLOTS, MOE_BLOCK, half), I32),
                            pltpu.VMEM((EXPERT_OUT_SLOTS, MOE_BLOCK, half), I32),
                            pltpu.SemaphoreType.DMA((EXPERT_IN_SLOTS,)),
                            pltpu.SemaphoreType.DMA((EXPERT_OUT_SLOTS,))]),
        out_shape=jax.ShapeDtypeStruct((P, half), I32),
        compiler_params=_cparams(("arbitrary",)),
    )(first_block, xs, w_in, w_out)


def _combine_kernel(yg_ref, gw_ref, u_ref, wsi_ref, wso_ref, h_ref, mod_ref, g_ref, b_ref, o_ref):
    ff = wso_ref.shape[0]
    hcat = _dot_halves(u_ref[...], wsi_ref)
    shared = _dot((_silu(hcat[:, :ff]) * hcat[:, ff:]).astype(BF16), wso_ref[...])
    lo, hi = None, None
    for k in range(TOP_K):
        y_lo, y_hi = _unpack_halves(yg_ref[k])
        gk = gw_ref[:, k:k + 1]
        lo = y_lo * gk if lo is None else lo + y_lo * gk
        hi = y_hi * gk if hi is None else hi + y_hi * gk
    routed = jnp.concatenate([lo, hi], axis=1)
    z = DEEPNORM_ALPHA * h_ref[...] + mod_ref[0] * (routed + shared)
    o_ref[...] = _layer_norm(z, g_ref[...], b_ref[...])


def _combine(yg, gw_t, u, ws_in, ws_out, h, gate, gate_index, ln_g, ln_b):
    T, D = h.shape
    tm = ROW_TILE
    vec_spec = pl.BlockSpec((1, D), lambda i: (0, 0))
    row_spec = pl.BlockSpec((tm, D), lambda i: (i, 0))
    packed_spec = pl.BlockSpec((tm, D // 2), lambda i: (i, 0))
    return pl.pallas_call(
        _combine_kernel,
        grid=(T // tm,),
        in_specs=[pl.BlockSpec((TOP_K, tm, D // 2), lambda i: (0, i, 0)),
                  pl.BlockSpec((tm, TOP_K), lambda i: (i, 0)),
                  packed_spec,
                  pl.BlockSpec(ws_in.shape, lambda i: (0, 0)),
                  pl.BlockSpec(ws_out.shape, lambda i: (0, 0)),
                  row_spec,
                  pl.BlockSpec((1,) + gate.shape[1:], lambda i: (gate_index(i), 0, 0)),
                  vec_spec, vec_spec],
        out_specs=row_spec,
        out_shape=jax.ShapeDtypeStruct((T, D), F32),
        compiler_params=_cparams(("arbitrary",)),
    )(yg, gw_t, u, ws_in.astype(BF16), ws_out.astype(BF16), h, gate,
      ln_g.reshape(1, D), ln_b.reshape(1, D))


def _slots_kernel(e_ref, rank_ref, start_ref, dest_ref):
    io_e = lax.broadcasted_iota(I32, (N_EXPERTS, e_ref.shape[1]), 0)
    rows = [jnp.sum(jnp.where(io_e == e_ref[k:k + 1, :], start_ref[...], 0), axis=0, keepdims=True)
            for k in range(TOP_K)]
    dest_ref[...] = jnp.concatenate(rows, axis=0) + rank_ref[...]


def _slots(eidx, rank, pstart):
    T = eidx.shape[1]
    tm = ROW_TILE
    tok_spec = pl.BlockSpec((TOP_K, tm), lambda i: (0, i))
    return pl.pallas_call(
        _slots_kernel,
        grid=(T // tm,),
        in_specs=[tok_spec, tok_spec, pl.BlockSpec((N_EXPERTS, 1), lambda i: (0, 0))],
        out_specs=tok_spec,
        out_shape=jax.ShapeDtypeStruct((TOP_K, T), I32),
        compiler_params=_cparams(("arbitrary",)),
    )(eidx, rank, pstart.reshape(N_EXPERTS, 1))


def _moe_layer(u, h, gate, gate_index, router, bias, w_in, w_out, ws_in, ws_out, ln_g, ln_b, layer):
    T = u.shape[0]
    eidx, gw, rank, counts = _router(u, router, bias)
    counts = counts[:, 0]
    padded = (counts + MOE_BLOCK - 1) // MOE_BLOCK * MOE_BLOCK
    pend = jnp.cumsum(padded)
    pstart = (pend - padded).astype(I32)
    dest = _slots(eidx, rank, pstart)
    n_blocks = -(-(T * TOP_K + N_EXPERTS * (MOE_BLOCK - 1)) // MOE_BLOCK)
    first_block = jnp.concatenate([jnp.zeros((1,), I32), (pend // MOE_BLOCK).astype(I32)])
    xs = _sc_scatter_rows(u, dest, n_blocks * MOE_BLOCK)
    y = _experts(xs, first_block, w_in, w_out, layer)
    return _combine(_sc_gather_rows(y, dest), gw.T, u, ws_in, ws_out, h, gate, gate_index, ln_g, ln_b)


def _seg_ones(width=LANES):
    idx = np.arange(width) // HEAD_DIM
    return jnp.asarray((idx[:, None] == idx[None, :]).astype(np.float32), BF16)


def _head_sum(x, ones_ref):
    outs = []
    for j in range(x.shape[1] // LANES):
        xc = x[:, j * LANES:(j + 1) * LANES]
        hi = xc.astype(BF16)
        lo = (xc - hi.astype(F32)).astype(BF16)
        outs.append(_dot(hi, ones_ref[...]) + _dot(lo, ones_ref[...]))
    return jnp.concatenate(outs, axis=1)


def _rwkv_proj_kernel(seg_tiles, h_ref, hp_ref, hn_ref, mod_ref, mu_ref, wrkv_ref, g1_ref, g2_ref, d1_ref, d2_ref,
                      d0_ref, i1_ref, i2_ref, i0_ref, kk_ref, ka_ref, rk_ref, ones_ref,
                      r_ref, v_ref, a_ref, g_ref, bonus_ref, w_ref, k_ref, b_ref):
    i = pl.program_id(0)
    nb = hp_ref.shape[0]
    shift, scale = mod_ref[0, 0], mod_ref[0, 1]
    u = h_ref[...] * (1.0 + scale) + shift
    starts = jnp.logical_or(i == 0, i == seg_tiles)
    ends = jnp.logical_or(i == seg_tiles - 1, i == pl.num_programs(0) - 1)
    u_before = (hp_ref[...] * (1.0 + scale[:nb]) + shift[:nb]) * jnp.where(starts, 0.0, 1.0)
    u_after = (hn_ref[...] * (1.0 + scale[:nb]) + shift[:nb]) * jnp.where(ends, 0.0, 1.0)
    dx = 0.5 * (jnp.concatenate([u_before, u[:-nb]], axis=0) + jnp.concatenate([u[nb:], u_after], axis=0)) - u
    mix = lambda m: (u + dx * mu_ref[m:m + 1, :])
    xr, xw, xk, xv, xa, xg = [mix(m) for m in range(6)]
    r = _dot(xr.astype(BF16), wrkv_ref[0])
    k = _dot(xk.astype(BF16), wrkv_ref[1])
    v = _dot(xv.astype(BF16), wrkv_ref[2])
    g = _dot(jax.nn.sigmoid(_dot(xg.astype(BF16), g1_ref[...])).astype(BF16), g2_ref[...])
    kk = k * kk_ref[...]
    kk = kk * lax.rsqrt(jnp.maximum(_head_sum(kk * kk, ones_ref), 1e-24))
    r_ref[...] = r
    v_ref[...] = v
    a_ref[...] = -kk
    g_ref[...] = g
    k_sum = None
    xw_b = xw.astype(BF16)
    xa_b = xa.astype(BF16)
    for d in range(2):
        lw = d0_ref[d:d + 1, :] + _dot(jnp.tanh(_dot(xw_b, d1_ref[d])).astype(BF16), d2_ref[d])
        softplus = jnp.maximum(-lw, 0.0) + jnp.log(1.0 + jnp.exp(-jnp.abs(lw)))
        logw = -softplus - 0.5
        w_ref[d] = jnp.exp(-jnp.exp(logw))
        eta = jax.nn.sigmoid(i0_ref[d:d + 1, :] + _dot(_dot(xa_b, i1_ref[d]).astype(BF16), i2_ref[d]))
        k_d = k * (1.0 + (eta - 1.0) * ka_ref[...])
        k_ref[d] = k_d
        b_ref[d] = kk * eta
        k_sum = k_d if k_sum is None else k_sum + k_d
    bonus_ref[...] = _head_sum(r * k_sum * rk_ref[...], ones_ref) * v


def _rwkv_proj(h, mod_rows, batch, n_ctx, p):
    T, D = h.shape
    tm = PROJ_TILE
    per_tile = tm // batch
    seg_tiles = n_ctx // per_tile
    n_steps = T // batch
    row = pl.BlockSpec((tm, D), lambda i: (i, 0))
    before = pl.BlockSpec((batch, D), lambda i: (jnp.maximum(i * per_tile - 1, 0), 0))
    after = pl.BlockSpec((batch, D), lambda i: (jnp.minimum((i + 1) * per_tile, n_steps - 1), 0))
    mod_spec = pl.BlockSpec((1, 2, tm, D), lambda i: (jnp.minimum(i // seg_tiles, 1), 0, 0, 0))
    row2 = pl.BlockSpec((2, tm, D), lambda i: (0, i, 0))
    full = lambda a: pl.BlockSpec(a.shape, lambda i: (0,) * a.ndim)
    bf = lambda a: a.astype(BF16)
    consts = [p['mu'], bf(p['w_rkv']), bf(p['gate1']), bf(p['gate2']), bf(p['dec1']), bf(p['dec2']), p['dec0'],
              bf(p['icl1']), bf(p['icl2']), p['icl0'], p['k_k'].reshape(1, D), p['k_a'].reshape(1, D),
              p['r_k'].reshape(1, D), _seg_ones()]
    one = jax.ShapeDtypeStruct((T, D), F32)
    two = jax.ShapeDtypeStruct((2, T, D), F32)
    return pl.pallas_call(
        functools.partial(_rwkv_proj_kernel, seg_tiles),
        grid=(T // tm,),
        in_specs=[row, before, after, mod_spec] + [full(a) for a in consts],
        out_specs=[row, row, row, row, row, row2, row2, row2],
        out_shape=[one, one, one, one, one, two, two, two],
        compiler_params=_cparams(("arbitrary",)),
    )(h, h, h, mod_rows, *consts)


def _scan_kernel(r_ref, w_ref, k_ref, v_ref, a_ref, b_ref, ones_ref, hsel_ref,
                 y_ref, s_ref, vt_ref):
    d = pl.program_id(0)
    c = pl.program_id(2)
    tc, nb = r_ref.shape[0], r_ref.shape[1]
    tw = SCAN_TILE
    n_wide = r_ref.shape[2] // tw
    heads = tw // HEAD_DIM
    assert heads * tc == tw

    @pl.when(c == 0)
    def _():
        s_ref[...] = jnp.zeros_like(s_ref)

    for bb in range(nb):
        for q in range(n_wide):
            vt = v_ref[:, bb, q * tw:(q + 1) * tw].T
            vt_ref[bb * n_wide + q] = jnp.concatenate(
                [vt[h * HEAD_DIM:(h + 1) * HEAD_DIM] for h in range(heads)], axis=1)

    head_base = (lax.broadcasted_iota(I32, (HEAD_DIM, LANES), 1) // HEAD_DIM) * tc
    tiles = [(bb, q) for bb in range(nb) for q in range(n_wide)]
    groups = [tiles[i:i + SCAN_GROUP] for i in range(0, len(tiles), SCAN_GROUP)]

    def stacked(grp, get, dtype=F32):
        def wide(bb, q):
            return jnp.concatenate(
                [jnp.broadcast_to(get(bb, slice(q * tw + hf * LANES, q * tw + (hf + 1) * LANES)).astype(dtype),
                                  (HEAD_DIM, LANES)) for hf in range(tw // LANES)], axis=1)
        return jnp.concatenate([wide(bb, q) for bb, q in grp], axis=0)

    def load_state(grp):
        return jnp.concatenate([s_ref[bb * n_wide + q] for bb, q in grp], axis=0)

    def emit_y(grp, st_b, t_y):
        r_rows = stacked(grp, lambda bb, cols: r_ref[t_y, bb:bb + 1, cols], BF16)
        yh = _dot_nt(hsel_ref[...], st_b * r_rows)
        first = tiles.index(grp[0])
        y_ref[0, t_y, :, first * HEAD_DIM:(first + len(grp)) * HEAD_DIM] = yh[:heads]

    def step(s_i, carry):
        t = jnp.where(d == 0, s_i, tc - 1 - s_i)
        t_prev = jnp.where(s_i == 0, t, jnp.where(d == 0, t - 1, t + 1))
        pick = head_base + t
        for grp in groups:
            one = lambda ref: stacked(grp, lambda bb, cols: ref[t, bb:bb + 1, cols])
            two = lambda ref: stacked(grp, lambda bb, cols: ref[0, t, bb:bb + 1, cols])
            st = load_state(grp)
            st_b = st.astype(BF16)
            a_rows = stacked(grp, lambda bb, cols: a_ref[t, bb:bb + 1, cols], BF16)
            sa = _dot(st_b * a_rows, ones_ref[...])
            emit_y(grp, st_b, t_prev)
            vcol = jnp.concatenate(
                [jnp.concatenate([jnp.take_along_axis(vt_ref[bb * n_wide + q, :, hf * LANES:(hf + 1) * LANES],
                                                      pick, axis=1) for hf in range(tw // LANES)], axis=1)
                 for bb, q in grp], axis=0)
            st = st * two(w_ref) + sa * two(b_ref) + vcol * two(k_ref)
            for j, (bb, q) in enumerate(grp):
                s_ref[bb * n_wide + q] = st[j * HEAD_DIM:(j + 1) * HEAD_DIM]
        return carry

    lax.fori_loop(0, tc, step, 0, unroll=SCAN_UNROLL)
    t_last = jnp.where(d == 0, tc - 1, 0)
    for grp in groups:
        emit_y(grp, load_state(grp).astype(BF16), t_last)


def _wkv_scan(r, w, k, v, a, b, n_ctx):
    N, B, D = r.shape
    tc = SCAN_CHUNK
    wc = SCAN_COLS
    n_wide = wc // SCAN_TILE
    nc = N // tc
    ncc = n_ctx // tc

    def chunk(d, c):
        rev = jnp.where(c < ncc, ncc - 1 - c, nc - 1 - (c - ncc))
        return jnp.where(d == 0, c, rev)

    one = pl.BlockSpec((tc, B, wc), lambda d, g, c: (chunk(d, c), 0, g))
    two = pl.BlockSpec((1, tc, B, wc), lambda d, g, c: (d, chunk(d, c), 0, g))
    seg = np.arange(SCAN_TILE) // HEAD_DIM
    hsel = np.zeros((8, SCAN_TILE), np.float32)
    for hh in range(SCAN_TILE // HEAD_DIM):
        hsel[hh, seg == hh] = 1.0
    const = lambda a_: pl.BlockSpec(a_.shape, lambda d, g, c: (0, 0))
    consts = [_seg_ones(SCAN_TILE), jnp.asarray(hsel, BF16)]
    heads = SCAN_TILE // HEAD_DIM
    ncg = D // wc
    y = pl.pallas_call(
        _scan_kernel,
        grid=(2, ncg, nc),
        in_specs=[one, two, two, one, one, two] + [const(a_) for a_ in consts],
        out_specs=pl.BlockSpec((1, tc, heads, B * n_wide * HEAD_DIM), lambda d, g, c: (d, chunk(d, c), 0, g)),
        out_shape=jax.ShapeDtypeStruct((2, N, heads, ncg * B * n_wide * HEAD_DIM), F32),
        scratch_shapes=[pltpu.VMEM((B * n_wide, HEAD_DIM, SCAN_TILE), F32),
                        pltpu.VMEM((B * n_wide, HEAD_DIM, SCAN_TILE), F32)],
        compiler_params=_cparams(("arbitrary", "arbitrary", "arbitrary")),
    )(r, w, k, v, a, b, *consts)
    y = y.reshape(2, N, heads, ncg, B, n_wide * HEAD_DIM)
    return jnp.transpose(y, (0, 1, 4, 2, 3, 5)).reshape(2, N, B, D)


def _scan_head_order(d):
    heads = SCAN_TILE // HEAD_DIM
    n_wide = SCAN_COLS // SCAN_TILE
    ncg = d // SCAN_COLS
    order = []
    for g in range(ncg):
        for q in range(n_wide):
            for h in range(heads):
                order.append((h * ncg + g) * n_wide + q)
    return tuple(order)


def _rwkv_out_kernel(head_order, y0_ref, y1_ref, bonus_ref, g_ref, lnx_ref, ones_ref, w_ref, h_ref, mod_ref,
                     lg_ref, lb_ref, hn_ref, u_ref):
    y_in = y0_ref[0] + y1_ref[0]
    y = jnp.concatenate([y_in[:, p * HEAD_DIM:(p + 1) * HEAD_DIM] for p in head_order], axis=1)
    ym = _head_sum(y, ones_ref) * (1.0 / HEAD_DIM)
    yc = y - ym
    yv = _head_sum(yc * yc, ones_ref) * (1.0 / HEAD_DIM)
    yn = yc * lax.rsqrt(yv + LNX_EPS) * lnx_ref[0:1, :] + lnx_ref[1:2, :]
    x = ((yn + bonus_ref[...]) * g_ref[...]).astype(BF16)
    o = _dot(x, w_ref[...])
    z = DEEPNORM_ALPHA * h_ref[...] + mod_ref[0] * o
    hn = _layer_norm(z, lg_ref[...], lb_ref[...])
    hn_ref[...] = hn
    u_ref[...] = _pack_halves(hn * (1.0 + mod_ref[2]) + mod_ref[1])


def _rwkv_out(y, bonus, g, lnx, w_out, h, mod_rows, ln_g, ln_b, row0):
    T, D = h.shape
    tm = ROW_TILE
    t0 = row0 // tm
    off = pl.BlockSpec((tm, D), lambda i: (i + t0, 0))
    out = pl.BlockSpec((tm, D), lambda i: (i, 0))
    full = lambda a: pl.BlockSpec(a.shape, lambda i: (0,) * a.ndim)
    vec = pl.BlockSpec((1, D), lambda i: (0, 0))
    ones = _seg_ones()
    w_b = w_out.astype(BF16)
    return pl.pallas_call(
        functools.partial(_rwkv_out_kernel, _scan_head_order(D)),
        grid=((T - row0) // tm,),
        in_specs=[pl.BlockSpec((1, tm, D), lambda i: (0, i + t0, 0)),
                  pl.BlockSpec((1, tm, D), lambda i: (1, i + t0, 0)),
                  off, off, full(lnx), full(ones), full(w_b), off, full(mod_rows), vec, vec],
        out_specs=[out, pl.BlockSpec((tm, D // 2), lambda i: (i, 0))],
        out_shape=[jax.ShapeDtypeStruct((T - row0, D), F32), jax.ShapeDtypeStruct((T - row0, D // 2), I32)],
        compiler_params=_cparams(("arbitrary",)),
    )(y, y, bonus, g, lnx, ones, w_b, h, mod_rows, ln_g.reshape(1, D), ln_b.reshape(1, D))


def kernel(x, c, ctx, c_ctx, ada_w, ada_b, post_ln_g, post_ln_b, att_w_in, att_w_out, att_sink, diff_lambda_vecs, diff_subln_g, rk_mu, rk_w_rkv, rk_w_out, rk_decay0, rk_decay1, rk_decay2, rk_iclr0, rk_iclr1, rk_iclr2, rk_gate1, rk_gate2, rk_k_k, rk_k_a, rk_r_k, rk_lnx, moe_router, moe_bias, moe_w_in, moe_w_out, moe_ws_in, moe_ws_out):
    B, S, D = x.shape
    L = ctx.shape[1]
    N = L + S
    tm = ROW_TILE
    assert L % tm == 0 and S % tm == 0 and L % SCAN_CHUNK == 0 and S % SCAN_CHUNK == 0
    assert tm % B == 0 and PROJ_TILE % B == 0 and L % (PROJ_TILE // B) == 0 and D % SCAN_COLS == 0

    rows = -(-(B + 1) // 8) * 8
    cvec = jnp.concatenate([c, c_ctx[None, :], jnp.zeros((rows - B - 1, D), F32)], axis=0)
    mods = [_mod_table(_ada_mod(cvec, ada_w[i], ada_b[i]), B, D) for i in range(DEPTH)]

    h0 = jnp.concatenate([ctx, x], axis=1)
    lam_init = 0.8 - 0.6 * math.exp(-0.3 * 0)
    qa, ka, va, qb, kb, vb = _attn_inproj(h0, mods[0], att_w_in[0], L)
    oa = _win_attn(qa, ka, va, att_sink[0], L)
    ob = _diff_attn(qb, kb, vb, diff_lambda_vecs[0], diff_subln_g[0], lam_init, L)
    h1, u1 = _mix_out([oa, ob], [att_w_out[0][:A_WIDTH], att_w_out[0][A_WIDTH:]], h0, mods[0],
                      post_ln_g[0, 0], post_ln_b[0, 0], L, 0)
    tiles_b, tiles_c = N // tm, L // tm
    gate0 = mods[0][:, :, 5].reshape(B * 2, 1, D)
    gate0_index = lambda i: (i // tiles_b) * 2 + jnp.minimum((i % tiles_b) // tiles_c, 1)
    h2 = _moe_layer(u1.reshape(B * N, D // 2), h1.reshape(B * N, D), gate0, gate0_index, moe_router[0], moe_bias[0],
                    moe_w_in, moe_w_out, moe_ws_in[0], moe_ws_out[0],
                    post_ln_g[0, 1], post_ln_b[0, 1], 0).reshape(B, N, D)

    m_ctx, m_lat = mods[1][:, 0], mods[1][:, 1]
    h2_t = jnp.swapaxes(h2, 0, 1).reshape(N * B, D)
    rows_of = lambda m, j, n: jnp.tile(m[:, j], (n // B, 1))
    proj_mod = jnp.stack([jnp.stack([rows_of(m, 0, PROJ_TILE), rows_of(m, 1, PROJ_TILE)]) for m in (m_ctx, m_lat)])
    params = dict(mu=rk_mu[0], w_rkv=rk_w_rkv[0], gate1=rk_gate1[0], gate2=rk_gate2[0],
                  dec0=rk_decay0[0], dec1=rk_decay1[0], dec2=rk_decay2[0],
                  icl0=rk_iclr0[0], icl1=rk_iclr1[0], icl2=rk_iclr2[0],
                  k_k=rk_k_k[0], k_a=rk_k_a[0], r_k=rk_r_k[0])
    r, v, a, g, bonus, w2, k2, b2 = _rwkv_proj(h2_t, proj_mod, B, L, params)
    tmaj = lambda t: t.reshape(t.shape[:-2] + (N, B, D))
    y = _wkv_scan(tmaj(r), tmaj(w2), tmaj(k2), tmaj(v), tmaj(a), tmaj(b2), L)
    lat_rows = lambda j: rows_of(m_lat, j, tm)
    h3, u3 = _rwkv_out(y.reshape(2, N * B, D), bonus, g, rk_lnx[0], rk_w_out[0], h2_t,
                       jnp.stack([lat_rows(2), lat_rows(3), lat_rows(4)]),
                       post_ln_g[1, 0], post_ln_b[1, 0], L * B)
    out = _moe_layer(u3, h3, lat_rows(5)[None], lambda i: 0, moe_router[1], moe_bias[1],
                     moe_w_in, moe_w_out, moe_ws_in[1], moe_ws_out[1],
                     post_ln_g[1, 1], post_ln_b[1, 1], 1)
    return jnp.swapaxes(out.reshape(S, B, D), 0, 1)
```

```python
import functools
import math

import numpy as np
import jax
import jax.numpy as jnp
from jax import lax
from jax.experimental import pallas as pl
from jax.experimental.pallas import tpu as pltpu
from jax.experimental.pallas import tpu_sc as plsc

F32 = jnp.float32
BF16 = jnp.bfloat16
I32 = jnp.int32

HEAD_DIM = 64
GRID_W = 64
ROPE_AXIS_DIM = HEAD_DIM // 2
ROPE_THETA = 10000.0
Q_BLOCK = 128
A_Q_HEADS = 8
A_KV_HEADS = 2
A_GROUP = A_Q_HEADS // A_KV_HEADS
A_WIDTH = A_Q_HEADS * HEAD_DIM
A_KV_WIDTH = A_KV_HEADS * HEAD_DIM
B_HEADS = 4
B_V_DIM = 2 * HEAD_DIM
B_WIDTH = B_HEADS * B_V_DIM
LNX_EPS = 64e-5
N_EXPERTS = 256
TOP_K = 8
N_GROUPS = 8
TOPK_GROUPS = 4
ROUTED_SCALE = 2.5
MOE_BLOCK = 256
EXPERT_IN_SLOTS = 4
EXPERT_OUT_SLOTS = 2
LN_EPS = 1e-5
SUBLN_EPS = 1e-5
NEG_INF = -1e30
DEPTH = 2
DEEPNORM_ALPHA = (2 * DEPTH) ** 0.25

LANES = 128
ADA_COLS = 768
ROW_TILE = 256
PROJ_TILE = 128
SC_WINDOW = 128
SCAN_CHUNK = 64
SCAN_COLS = 1024
SCAN_TILE = 256
SCAN_GROUP = 8
SCAN_UNROLL = 8
VMEM_LIMIT = 56 * 1024 * 1024


def _cparams(sem):
    return pltpu.CompilerParams(dimension_semantics=sem, vmem_limit_bytes=VMEM_LIMIT)


def _silu(x):
    return x * jax.nn.sigmoid(x)


def _layer_norm(z, g, b):
    mu = jnp.mean(z, -1, keepdims=True)
    zc = z - mu
    var = jnp.mean(zc * zc, -1, keepdims=True)
    return zc * lax.rsqrt(var + LN_EPS) * g + b


def _dot(a, b):
    return jnp.dot(a, b, preferred_element_type=F32)


def _dot_nt(a, b):
    return lax.dot_general(a, b, (((1,), (1,)), ((), ())), preferred_element_type=F32)


def _pack_halves(x):
    half = x.shape[1] // 2
    bits = lambda v: lax.bitcast_convert_type(v.astype(BF16).astype(F32), I32)
    return lax.shift_right_logical(bits(x[:, :half]), 16) | bits(x[:, half:])


def _unpack_halves(p):
    lo = lax.bitcast_convert_type(lax.shift_left(p, 16), F32)
    hi = lax.bitcast_convert_type(p & jnp.int32(-65536), F32)
    return lo, hi


def _dot_halves(p, w_ref_or_array):
    lo, hi = _unpack_halves(p)
    half = p.shape[1]
    return _dot(lo.astype(BF16), w_ref_or_array[:half]) + _dot(hi.astype(BF16), w_ref_or_array[half:])


def _ada_kernel(c_ref, w_ref, b_ref, o_ref):
    c = c_ref[...]
    o_ref[...] = _dot(_silu(c).astype(BF16), w_ref[...].astype(BF16)) + b_ref[...]


def _ada_mod(cvec, w, bias):
    R, D = cvec.shape
    n_out = w.shape[1]
    tn = ADA_COLS
    return pl.pallas_call(
        _ada_kernel,
        grid=(n_out // tn,),
        in_specs=[pl.BlockSpec((R, D), lambda j: (0, 0)),
                  pl.BlockSpec((D, tn), lambda j: (0, j)),
                  pl.BlockSpec((1, tn), lambda j: (0, j))],
        out_specs=pl.BlockSpec((R, tn), lambda j: (0, j)),
        out_shape=jax.ShapeDtypeStruct((R, n_out), F32),
        compiler_params=_cparams(("arbitrary",)),
    )(cvec, w, bias.reshape(1, n_out))


def _mod_table(m, batch, d):
    m_lat = m[:batch].reshape(batch, 6, d)
    m_ctx = jnp.broadcast_to(m[batch].reshape(1, 6, d), (batch, 6, d))
    return jnp.stack([m_ctx, m_lat], axis=1)


def _mod_spec(d, ctx_tiles):
    return pl.BlockSpec((1, 1, 6, d), lambda b, i: (b, jnp.minimum(i // ctx_tiles, 1), 0, 0))


def _rope_tables(n_ctx, n_lat):
    rows = n_lat // GRID_W
    row = np.repeat(np.arange(rows), GRID_W).astype(np.float32)
    col = np.tile(np.arange(GRID_W), rows).astype(np.float32)
    inv = (ROPE_THETA ** (-np.arange(0, ROPE_AXIS_DIM, 2, dtype=np.float32) / ROPE_AXIS_DIM)).astype(np.float32)
    ar = row[:, None] * inv
    ac = col[:, None] * inv
    ang = np.concatenate([ar, ar, ac, ac], -1)
    cos = np.cos(ang).astype(np.float32)
    sin = np.sin(ang).astype(np.float32)
    lower = (np.arange(HEAD_DIM) % ROPE_AXIS_DIM) < (ROPE_AXIS_DIM // 2)
    sin_up = np.where(lower[None, :], -sin, 0.0)
    sin_dn = np.where(lower[None, :], 0.0, sin)

    def full(t, ctx_fill):
        t = np.concatenate([np.full((n_ctx, HEAD_DIM), ctx_fill, np.float32), t], 0)
        return jnp.asarray(np.tile(t, (1, LANES // HEAD_DIM)))

    return full(cos, 1.0), full(sin_up, 0.0), full(sin_dn, 0.0)


def _inproj_kernel(h_ref, mod_ref, w_ref, cos_ref, su_ref, sd_ref,
                   qa_ref, ka_ref, va_ref, qb_ref, kb_ref, vb_ref):
    h = h_ref[0]
    shift = mod_ref[0, 0, 0:1, :]
    scale = mod_ref[0, 0, 1:2, :]
    u = (h * (1.0 + scale) + shift).astype(BF16)
    y = _dot(u, w_ref[...])
    cos, s_up, s_dn = cos_ref[...], su_ref[...], sd_ref[...]
    q_scale = HEAD_DIM ** -0.5

    def rope(xc):
        half = ROPE_AXIS_DIM // 2
        return xc * cos + pltpu.roll(xc, LANES - half, 1) * s_up + pltpu.roll(xc, half, 1) * s_dn

    def emit(out_ref, col0, width, roped, mul):
        for j in range(width // LANES):
            xc = y[:, col0 + j * LANES: col0 + (j + 1) * LANES]
            if roped:
                xc = rope(xc)
            if mul != 1.0:
                xc = xc * mul
            out_ref[0, :, j * LANES:(j + 1) * LANES] = xc.astype(out_ref.dtype)

    c = 0
    emit(qa_ref, c, A_WIDTH, True, q_scale); c += A_WIDTH
    emit(ka_ref, c, A_KV_WIDTH, True, 1.0); c += A_KV_WIDTH
    emit(va_ref, c, A_KV_WIDTH, False, 1.0); c += A_KV_WIDTH
    emit(qb_ref, c, B_WIDTH, True, q_scale); c += B_WIDTH
    emit(kb_ref, c, B_WIDTH, True, 1.0); c += B_WIDTH
    emit(vb_ref, c, B_WIDTH, False, 1.0)


def _attn_inproj(h, mod, w_in, n_ctx):
    B, N, D = h.shape
    tm = ROW_TILE
    cos, s_up, s_dn = _rope_tables(n_ctx, N - n_ctx)
    widths = (A_WIDTH, A_KV_WIDTH, A_KV_WIDTH, B_WIDTH, B_WIDTH, B_WIDTH)
    tab_spec = pl.BlockSpec((tm, LANES), lambda b, i: (i, 0))
    return pl.pallas_call(
        _inproj_kernel,
        grid=(B, N // tm),
        in_specs=[pl.BlockSpec((1, tm, D), lambda b, i: (b, i, 0)),
                  _mod_spec(D, n_ctx // tm),
                  pl.BlockSpec(w_in.shape, lambda b, i: (0, 0)),
                  tab_spec, tab_spec, tab_spec],
        out_specs=[pl.BlockSpec((1, tm, w), lambda b, i: (b, i, 0)) for w in widths],
        out_shape=[jax.ShapeDtypeStruct((B, N, w), BF16) for w in widths],
        compiler_params=_cparams(("arbitrary", "arbitrary")),
    )(h, mod, w_in.astype(BF16), cos, s_up, s_dn)


def _win_attn_kernel(n_ctx_blocks, n_blocks, q_ref, kc_ref, vc_ref, kl_ref, km_ref, kr_ref,
                     vl_ref, vm_ref, vr_ref, sink_ref, o_ref):
    j = pl.program_id(1)
    is_lat = j >= n_ctx_blocks
    qb = Q_BLOCK
    n_c = kc_ref.shape[1]
    rows = A_GROUP * qb
    n_keys = n_c + 3 * qb
    far = 1 << 20
    r_idx = lax.broadcasted_iota(I32, (rows, n_keys), 0) % qb
    cw = lax.broadcasted_iota(I32, (rows, n_keys), 1) - n_c
    off_l = jnp.where(jnp.logical_and(is_lat, j > n_ctx_blocks), 0, far)
    end_m = jnp.where(is_lat, 2 * qb, qb)
    off_r = jnp.where(jnp.logical_and(is_lat, j < n_blocks - 1), 0, far)
    valid = ((cw < 0)
             | ((cw >= 0) & (cw < qb) & (cw >= r_idx + off_l))
             | ((cw >= qb) & (cw < end_m))
             | ((cw >= 2 * qb) & (cw - 2 * qb + off_r <= r_idx)))
    outs = []
    for kv in range(A_KV_HEADS):
        cols = slice(kv * HEAD_DIM, (kv + 1) * HEAD_DIM)
        k_all = jnp.concatenate([kc_ref[0, :, cols], kl_ref[0, :, cols], km_ref[0, :, cols],
                                 kr_ref[0, :, cols]], axis=0)
        v_all = jnp.concatenate([vc_ref[0, :, cols], vl_ref[0, :, cols], vm_ref[0, :, cols],
                                 vr_ref[0, :, cols]], axis=0)
        q0 = kv * A_GROUP
        q = jnp.concatenate([q_ref[0, :, (q0 + g) * HEAD_DIM:(q0 + g + 1) * HEAD_DIM]
                             for g in range(A_GROUP)], axis=0)
        sink = jnp.concatenate([jnp.broadcast_to(sink_ref[q0 + g:q0 + g + 1, 0:1], (qb, 1))
                                for g in range(A_GROUP)], axis=0)
        s = jnp.where(valid, _dot_nt(q, k_all), NEG_INF)
        m = jnp.maximum(jnp.max(s, -1, keepdims=True), sink)
        e = jnp.exp(s - m)
        denom = jnp.sum(e, -1, keepdims=True) + jnp.exp(sink - m)
        o = _dot(e.astype(BF16), v_all) * (1.0 / denom)
        outs += [o[g * qb:(g + 1) * qb] for g in range(A_GROUP)]
    for j2 in range(A_Q_HEADS // 2):
        pair = jnp.concatenate([outs[2 * j2], outs[2 * j2 + 1]], axis=1)
        o_ref[0, :, j2 * LANES:(j2 + 1) * LANES] = pair.astype(o_ref.dtype)


def _win_attn(qa, ka, va, sink, n_ctx):
    B, N, _ = qa.shape
    qb = Q_BLOCK
    nb = N // qb
    ncb = n_ctx // qb
    sink_pad = jnp.broadcast_to(sink.reshape(A_Q_HEADS, 1).astype(F32), (A_Q_HEADS, LANES))

    def left(b, j):
        return (b, jnp.clip(j - 1, ncb, nb - 1), 0)

    def mid(b, j):
        return (b, jnp.clip(j, ncb, nb - 1), 0)

    def right(b, j):
        return (b, jnp.clip(j + 1, ncb, nb - 1), 0)

    kv_blk = lambda im: pl.BlockSpec((1, qb, A_KV_WIDTH), im)
    ctx_blk = pl.BlockSpec((1, n_ctx, A_KV_WIDTH), lambda b, j: (b, 0, 0))
    return pl.pallas_call(
        functools.partial(_win_attn_kernel, ncb, nb),
        grid=(B, nb),
        in_specs=[pl.BlockSpec((1, qb, A_WIDTH), lambda b, j: (b, j, 0)),
                  ctx_blk, ctx_blk,
                  kv_blk(left), kv_blk(mid), kv_blk(right),
                  kv_blk(left), kv_blk(mid), kv_blk(right),
                  pl.BlockSpec((A_Q_HEADS, LANES), lambda b, j: (0, 0))],
        out_specs=pl.BlockSpec((1, qb, A_WIDTH), lambda b, j: (b, j, 0)),
        out_shape=jax.ShapeDtypeStruct((B, N, A_WIDTH), BF16),
        compiler_params=_cparams(("arbitrary", "arbitrary")),
    )(qa, ka, va, ka, ka, ka, va, va, va, sink_pad)


def _diff_attn_kernel(n_ctx, lam_init, q_ref, k_ref, v_ref, lv_ref, g_ref, o_ref):
    j = pl.program_id(1)
    lv = lv_ref[...]
    lam = (jnp.exp(jnp.sum(lv[0:1] * lv[1:2], -1, keepdims=True))
           - jnp.exp(jnp.sum(lv[2:3] * lv[3:4], -1, keepdims=True)) + lam_init)
    gain = g_ref[...] * (1.0 - lam_init)

    def run(n_keys):
        for hd in range(B_HEADS):
            parts = []
            for mm in range(2):
                c0 = (hd * 2 + mm) * HEAD_DIM
                q = q_ref[0, :, c0:c0 + HEAD_DIM]
                k = k_ref[0, :n_keys, c0:c0 + HEAD_DIM]
                s = _dot_nt(q, k)
                e = jnp.exp(s - jnp.max(s, -1, keepdims=True))
                parts.append((e, jnp.sum(e, -1, keepdims=True)))
            (e0, l0), (e1, l1) = parts
            v = v_ref[0, :n_keys, hd * B_V_DIM:(hd + 1) * B_V_DIM]
            o = _dot(e0.astype(BF16), v) * (1.0 / l0) - _dot(e1.astype(BF16), v) * (lam / l1)
            o = o * lax.rsqrt(jnp.mean(o * o, -1, keepdims=True) + SUBLN_EPS) * gain
            o_ref[0, :, hd * B_V_DIM:(hd + 1) * B_V_DIM] = o.astype(o_ref.dtype)

    @pl.when(j == 0)
    def _():
        run(n_ctx)

    @pl.when(j > 0)
    def _():
        run(k_ref.shape[1])


def _diff_attn(qb, kb, vb, lam_vecs, subln_g, lam_init, n_ctx):
    B, N, _ = qb.shape
    tq = n_ctx
    return pl.pallas_call(
        functools.partial(_diff_attn_kernel, n_ctx, lam_init),
        grid=(B, N // tq),
        in_specs=[pl.BlockSpec((1, tq, B_WIDTH), lambda b, j: (b, j, 0)),
                  pl.BlockSpec((1, N, B_WIDTH), lambda b, j: (b, 0, 0)),
                  pl.BlockSpec((1, N, B_WIDTH), lambda b, j: (b, 0, 0)),
                  pl.BlockSpec((4, HEAD_DIM), lambda b, j: (0, 0)),
                  pl.BlockSpec((1, B_V_DIM), lambda b, j: (0, 0))],
        out_specs=pl.BlockSpec((1, tq, B_WIDTH), lambda b, j: (b, j, 0)),
        out_shape=jax.ShapeDtypeStruct((B, N, B_WIDTH), BF16),
        compiler_params=_cparams(("arbitrary", "arbitrary")),
    )(qb, kb, vb, lam_vecs.astype(F32), subln_g.reshape(1, B_V_DIM).astype(F32))


def _mix_out_kernel(n_in, *refs):
    xs = refs[:n_in]
    ws = refs[n_in:2 * n_in]
    h_ref, mod_ref, g_ref, b_ref, hn_ref, u_ref = refs[2 * n_in:]
    o = _dot(xs[0][0], ws[0][...])
    for x_ref, w_ref in zip(xs[1:], ws[1:]):
        o = o + _dot(x_ref[0], w_ref[...])
    z = DEEPNORM_ALPHA * h_ref[0] + mod_ref[0, 0, 2:3, :] * o
    hn = _layer_norm(z, g_ref[...], b_ref[...])
    hn_ref[0] = hn
    u_ref[0] = _pack_halves(hn * (1.0 + mod_ref[0, 0, 4:5, :]) + mod_ref[0, 0, 3:4, :])


def _mix_out(xs, ws, h, mod, ln_g, ln_b, n_ctx, row0):
    B, N, D = h.shape
    tm = ROW_TILE
    t0 = row0 // tm
    n_out = N - row0
    row_spec = lambda w: pl.BlockSpec((1, tm, w), lambda b, i: (b, i + t0, 0))
    out_spec = pl.BlockSpec((1, tm, D), lambda b, i: (b, i, 0))
    vec_spec = pl.BlockSpec((1, D), lambda b, i: (0, 0))
    return pl.pallas_call(
        functools.partial(_mix_out_kernel, len(xs)),
        grid=(B, n_out // tm),
        in_specs=([row_spec(x.shape[-1]) for x in xs]
                  + [pl.BlockSpec(w.shape, lambda b, i: (0, 0)) for w in ws]
                  + [row_spec(D),
                     pl.BlockSpec((1, 1, 6, D), lambda b, i: (b, jnp.minimum((i + t0) // (n_ctx // tm), 1), 0, 0)),
                     vec_spec, vec_spec]),
        out_specs=[out_spec, pl.BlockSpec((1, tm, D // 2), lambda b, i: (b, i, 0))],
        out_shape=[jax.ShapeDtypeStruct((B, n_out, D), F32), jax.ShapeDtypeStruct((B, n_out, D // 2), I32)],
        compiler_params=_cparams(("arbitrary", "arbitrary")),
    )(*xs, *[w.astype(BF16) for w in ws], h, mod, ln_g.reshape(1, D), ln_b.reshape(1, D))


def _router_kernel(u_ref, rt_ref, bias_ref, tri_ref, e_ref, gw_ref, rank_ref, cnt_ref, carry_ref):
    i = pl.program_id(0)

    @pl.when(i == 0)
    def _():
        carry_ref[...] = jnp.zeros_like(carry_ref)

    tm = u_ref.shape[0]
    per_group = N_EXPERTS // N_GROUPS
    neg = -jnp.inf
    u_lo, u_hi = _unpack_halves(u_ref[...])
    half = u_ref.shape[1]
    logits = (_dot_nt(rt_ref[:, :half], u_lo.astype(BF16))
              + _dot_nt(rt_ref[:, half:], u_hi.astype(BF16)))
    scores = jax.nn.sigmoid(logits)
    sel = scores + bias_ref[...]
    io_in = lax.broadcasted_iota(I32, (per_group, tm), 0)
    grp_rows = []
    for gi in range(N_GROUPS):
        sg = sel[gi * per_group:(gi + 1) * per_group]
        m1 = jnp.max(sg, axis=0, keepdims=True)
        i1 = jnp.min(jnp.where(sg == m1, io_in, per_group), axis=0, keepdims=True)
        m2 = jnp.max(jnp.where(io_in == i1, neg, sg), axis=0, keepdims=True)
        grp_rows.append(m1 + m2)
    grp = jnp.concatenate(grp_rows, axis=0)
    io_g = lax.broadcasted_iota(I32, grp.shape, 0)
    g_sel = jnp.zeros(grp.shape, F32)
    for _ in range(TOPK_GROUPS):
        m = jnp.max(grp, axis=0, keepdims=True)
        hit = io_g == jnp.min(jnp.where(grp == m, io_g, N_GROUPS), axis=0, keepdims=True)
        g_sel = jnp.where(hit, 1.0, g_sel)
        grp = jnp.where(hit, neg, grp)
    selm = jnp.concatenate(
        [jnp.where(g_sel[gi:gi + 1] > 0.5, sel[gi * per_group:(gi + 1) * per_group], NEG_INF)
         for gi in range(N_GROUPS)], axis=0)
    io_e = lax.broadcasted_iota(I32, selm.shape, 0)
    chosen_f = jnp.zeros(selm.shape, F32)
    idx, gws = [], []
    for _ in range(TOP_K):
        m = jnp.max(selm, axis=0, keepdims=True)
        ik = jnp.min(jnp.where(selm == m, io_e, N_EXPERTS), axis=0, keepdims=True)
        hit = io_e == ik
        idx.append(ik)
        gws.append(jnp.sum(jnp.where(hit, scores, 0.0), axis=0, keepdims=True))
        chosen_f = jnp.where(hit, 1.0, chosen_f)
        selm = jnp.where(hit, neg, selm)
    gw = jnp.concatenate(gws, axis=0)
    gw_ref[...] = gw / jnp.sum(gw, axis=0, keepdims=True) * ROUTED_SCALE
    e_ref[...] = jnp.concatenate(idx, axis=0)
    before = _dot(chosen_f.astype(BF16), tri_ref[...]) + carry_ref[...]
    ranks = [jnp.sum(jnp.where(io_e == ik, before, 0.0), axis=0, keepdims=True) for ik in idx]
    rank_ref[...] = jnp.concatenate(ranks, axis=0).astype(I32)
    carry_ref[...] = carry_ref[...] + jnp.sum(chosen_f, axis=1, keepdims=True)
    cnt_ref[...] = carry_ref[...].astype(I32)


def _router(u, router, bias):
    T = u.shape[0]
    D = router.shape[0]
    tm = ROW_TILE
    tri = jnp.asarray(np.triu(np.ones((tm, tm), np.float32), 1), BF16)
    tok_spec = pl.BlockSpec((TOP_K, tm), lambda i: (0, i))
    return pl.pallas_call(
        _router_kernel,
        grid=(T // tm,),
        in_specs=[pl.BlockSpec((tm, D // 2), lambda i: (i, 0)),
                  pl.BlockSpec((N_EXPERTS, D), lambda i: (0, 0)),
                  pl.BlockSpec((N_EXPERTS, 1), lambda i: (0, 0)),
                  pl.BlockSpec((tm, tm), lambda i: (0, 0))],
        out_specs=[tok_spec, tok_spec, tok_spec, pl.BlockSpec((N_EXPERTS, 1), lambda i: (0, 0))],
        out_shape=[jax.ShapeDtypeStruct((TOP_K, T), I32), jax.ShapeDtypeStruct((TOP_K, T), F32),
                   jax.ShapeDtypeStruct((TOP_K, T), I32), jax.ShapeDtypeStruct((N_EXPERTS, 1), I32)],
        scratch_shapes=[pltpu.VMEM((N_EXPERTS, 1), F32)],
        compiler_params=_cparams(("arbitrary",)),
    )(u, router.T.astype(BF16), bias.reshape(N_EXPERTS, 1).astype(F32), tri)


def _sc_mesh():
    return plsc.VectorSubcoreMesh(core_axis_name="c", subcore_axis_name="s")


def _sc_scatter_rows(x, dest, n_rows):
    T, W = x.shape
    K = dest.shape[0]
    win = SC_WINDOW

    @functools.partial(pl.kernel, out_type=jax.ShapeDtypeStruct((n_rows, W), x.dtype), mesh=_sc_mesh(),
                       scratch_types=[])
    def scatter(x_hbm, i_hbm, o_hbm):
        def body(x_vmem, i_vmem):
            for k in range(K):
                pltpu.sync_copy(x_vmem, o_hbm.at[i_vmem.at[k]])

        pltpu.emit_pipeline(
            body,
            grid=(T // win,),
            in_specs=[pl.BlockSpec((win, W), lambda j: (j, 0), pipeline_mode=pl.Buffered(1)),
                      pl.BlockSpec((K, win), lambda j: (0, j))],
            out_specs=[],
            core_axis_name=("c", "s"),
            dimension_semantics=(pltpu.PARALLEL,),
        )(x_hbm, i_hbm)

    return scatter(x, dest)


def _sc_gather_rows(y, dest):
    K, T = dest.shape
    W = y.shape[1]
    win = SC_WINDOW

    @functools.partial(pl.kernel, out_type=jax.ShapeDtypeStruct((K * T, W), y.dtype), mesh=_sc_mesh(),
                       scratch_types=[])
    def gather(y_hbm, i_hbm, o_hbm):
        def body(i_vmem, o_vmem):
            pltpu.sync_copy(y_hbm.at[i_vmem.at[0]], o_vmem)

        pltpu.emit_pipeline(
            body,
            grid=(K * T // win,),
            in_specs=[pl.BlockSpec((1, win), lambda j: (0, j))],
            out_specs=[pl.BlockSpec((win, W), lambda j: (j, 0), pipeline_mode=pl.Buffered(1))],
            core_axis_name=("c", "s"),
            dimension_semantics=(pltpu.PARALLEL,),
        )(i_hbm, o_hbm)

    return gather(y, dest.reshape(1, K * T)).reshape(K, T, W)


def _expert_kernel(first_ref, x_hbm, wi_ref, wo_ref, y_hbm, wi_b, wo_b, xbuf, ybuf, sem_in, sem_out):
    e = pl.program_id(0)
    n_in, blk = xbuf.shape[0], xbuf.shape[1]
    n_out = ybuf.shape[0]
    g0, g1 = first_ref[e], first_ref[e + 1]
    g_end = first_ref[pl.num_programs(0)]

    def x_copy(g):
        return pltpu.make_async_copy(x_hbm.at[pl.ds(g * blk, blk)], xbuf.at[g % n_in], sem_in.at[g % n_in])

    def y_copy(g):
        return pltpu.make_async_copy(ybuf.at[g % n_out], y_hbm.at[pl.ds(g * blk, blk)], sem_out.at[g % n_out])

    for ahead in range(n_in - 1):
        @pl.when(jnp.logical_and(e == 0, g_end > ahead))
        def _():
            x_copy(ahead).start()

    @pl.when(g1 > g0)
    def _():
        wi_b[...] = wi_ref[0, 0].astype(BF16)
        wo_b[...] = wo_ref[0, 0].astype(BF16)

    ff = wo_b.shape[0]

    def block(g, carry):
        x_copy(g).wait()

        @pl.when(g + n_in - 1 < g_end)
        def _():
            x_copy(g + n_in - 1).start()

        @pl.when(g >= n_out)
        def _():
            y_copy(g - n_out).wait()

        hcat = _dot_halves(xbuf[g % n_in], wi_b)
        act = (_silu(hcat[:, :ff]) * hcat[:, ff:]).astype(BF16)
        ybuf[g % n_out] = _pack_halves(_dot(act, wo_b[...]))
        y_copy(g).start()
        return carry

    lax.fori_loop(g0, g1, block, 0)

    @pl.when(e == pl.num_programs(0) - 1)
    def _():
        for back in range(n_out, 0, -1):
            @pl.when(g_end >= back)
            def _():
                y_copy(g_end - back).wait()


def _experts(xs, first_block, w_in, w_out, layer):
    P, half = xs.shape
    n_exp, D, ff2 = w_in.shape[-3:]
    return pl.pallas_call(
        _expert_kernel,
        grid_spec=pltpu.PrefetchScalarGridSpec(
            num_scalar_prefetch=1,
            grid=(n_exp,),
            in_specs=[pl.BlockSpec(memory_space=pl.ANY),
                      pl.BlockSpec((1, 1, D, ff2), lambda e, fb: (layer, e, 0, 0)),
                      pl.BlockSpec((1, 1, ff2 // 2, D), lambda e, fb: (layer, e, 0, 0))],
            out_specs=pl.BlockSpec(memory_space=pl.ANY),
            scratch_shapes=[pltpu.VMEM((D, ff2), BF16), pltpu.VMEM((ff2 // 2, D), BF16),
                            pltpu.VMEM((EXPERT_IN_SLOTS, MOE_BLOCK, half), I32),
                            pltpu.VMEM((EXPERT_OUT_SLOTS, MOE_BLOCK, half), I32),
                            pltpu.SemaphoreType.DMA((EXPERT_IN_SLOTS,)),
                            pltpu.SemaphoreType.DMA((EXPERT_OUT_SLOTS,))]),
        out_shape=jax.ShapeDtypeStruct((P, half), I32),
        compiler_params=_cparams(("arbitrary",)),
    )(first_block, xs, w_in, w_out)


def _combine_kernel(yg_ref, gw_ref, u_ref, wsi_ref, wso_ref, h_ref, mod_ref, g_ref, b_ref, o_ref):
    ff = wso_ref.shape[0]
    hcat = _dot_halves(u_ref[...], wsi_ref)
    shared = _dot((_silu(hcat[:, :ff]) * hcat[:, ff:]).astype(BF16), wso_ref[...])
    lo, hi = None, None
    for k in range(TOP_K):
        y_lo, y_hi = _unpack_halves(yg_ref[k])
        gk = gw_ref[:, k:k + 1]
        lo = y_lo * gk if lo is None else lo + y_lo * gk
        hi = y_hi * gk if hi is None else hi + y_hi * gk
    routed = jnp.concatenate([lo, hi], axis=1)
    z = DEEPNORM_ALPHA * h_ref[...] + mod_ref[0] * (routed + shared)
    o_ref[...] = _layer_norm(z, g_ref[...], b_ref[...])


def _combine(yg, gw_t, u, ws_in, ws_out, h, gate, gate_index, ln_g, ln_b):
    T, D = h.shape
    tm = ROW_TILE
    vec_spec = pl.BlockSpec((1, D), lambda i: (0, 0))
    row_spec = pl.BlockSpec((tm, D), lambda i: (i, 0))
    packed_spec = pl.BlockSpec((tm, D // 2), lambda i: (i, 0))
    return pl.pallas_call(
        _combine_kernel,
        grid=(T // tm,),
        in_specs=[pl.BlockSpec((TOP_K, tm, D // 2), lambda i: (0, i, 0)),
                  pl.BlockSpec((tm, TOP_K), lambda i: (i, 0)),
                  packed_spec,
                  pl.BlockSpec(ws_in.shape, lambda i: (0, 0)),
                  pl.BlockSpec(ws_out.shape, lambda i: (0, 0)),
                  row_spec,
                  pl.BlockSpec((1,) + gate.shape[1:], lambda i: (gate_index(i), 0, 0)),
                  vec_spec, vec_spec],
        out_specs=row_spec,
        out_shape=jax.ShapeDtypeStruct((T, D), F32),
        compiler_params=_cparams(("arbitrary",)),
    )(yg, gw_t, u, ws_in.astype(BF16), ws_out.astype(BF16), h, gate,
      ln_g.reshape(1, D), ln_b.reshape(1, D))


def _slots_kernel(e_ref, rank_ref, start_ref, dest_ref):
    io_e = lax.broadcasted_iota(I32, (N_EXPERTS, e_ref.shape[1]), 0)
    rows = [jnp.sum(jnp.where(io_e == e_ref[k:k + 1, :], start_ref[...], 0), axis=0, keepdims=True)
            for k in range(TOP_K)]
    dest_ref[...] = jnp.concatenate(rows, axis=0) + rank_ref[...]


def _slots(eidx, rank, pstart):
    T = eidx.shape[1]
    tm = ROW_TILE
    tok_spec = pl.BlockSpec((TOP_K, tm), lambda i: (0, i))
    return pl.pallas_call(
        _slots_kernel,
        grid=(T // tm,),
        in_specs=[tok_spec, tok_spec, pl.BlockSpec((N_EXPERTS, 1), lambda i: (0, 0))],
        out_specs=tok_spec,
        out_shape=jax.ShapeDtypeStruct((TOP_K, T), I32),
        compiler_params=_cparams(("arbitrary",)),
    )(eidx, rank, pstart.reshape(N_EXPERTS, 1))


def _moe_layer(u, h, gate, gate_index, router, bias, w_in, w_out, ws_in, ws_out, ln_g, ln_b, layer):
    T = u.shape[0]
    eidx, gw, rank, counts = _router(u, router, bias)
    counts = counts[:, 0]
    padded = (counts + MOE_BLOCK - 1) // MOE_BLOCK * MOE_BLOCK
    pend = jnp.cumsum(padded)
    pstart = (pend - padded).astype(I32)
    dest = _slots(eidx, rank, pstart)
    n_blocks = -(-(T * TOP_K + N_EXPERTS * (MOE_BLOCK - 1)) // MOE_BLOCK)
    first_block = jnp.concatenate([jnp.zeros((1,), I32), (pend // MOE_BLOCK).astype(I32)])
    xs = _sc_scatter_rows(u, dest, n_blocks * MOE_BLOCK)
    y = _experts(xs, first_block, w_in, w_out, layer)
    return _combine(_sc_gather_rows(y, dest), gw.T, u, ws_in, ws_out, h, gate, gate_index, ln_g, ln_b)


def _seg_ones(width=LANES):
    idx = np.arange(width) // HEAD_DIM
    return jnp.asarray((idx[:, None] == idx[None, :]).astype(np.float32), BF16)


def _head_sum(x, ones_ref):
    outs = []
    for j in range(x.shape[1] // LANES):
        xc = x[:, j * LANES:(j + 1) * LANES]
        hi = xc.astype(BF16)
        lo = (xc - hi.astype(F32)).astype(BF16)
        outs.append(_dot(hi, ones_ref[...]) + _dot(lo, ones_ref[...]))
    return jnp.concatenate(outs, axis=1)


def _rwkv_proj_kernel(seg_tiles, h_ref, hp_ref, hn_ref, mod_ref, mu_ref, wrkv_ref, g1_ref, g2_ref, d1_ref, d2_ref,
                      d0_ref, i1_ref, i2_ref, i0_ref, kk_ref, ka_ref, rk_ref, ones_ref,
                      r_ref, v_ref, a_ref, g_ref, bonus_ref, w_ref, k_ref, b_ref):
    i = pl.program_id(0)
    nb = hp_ref.shape[0]
    shift, scale = mod_ref[0, 0], mod_ref[0, 1]
    u = h_ref[...] * (1.0 + scale) + shift
    starts = jnp.logical_or(i == 0, i == seg_tiles)
    ends = jnp.logical_or(i == seg_tiles - 1, i == pl.num_programs(0) - 1)
    u_before = (hp_ref[...] * (1.0 + scale[:nb]) + shift[:nb]) * jnp.where(starts, 0.0, 1.0)
    u_after = (hn_ref[...] * (1.0 + scale[:nb]) + shift[:nb]) * jnp.where(ends, 0.0, 1.0)
    dx = 0.5 * (jnp.concatenate([u_before, u[:-nb]], axis=0) + jnp.concatenate([u[nb:], u_after], axis=0)) - u
    mix = lambda m: (u + dx * mu_ref[m:m + 1, :])
    xr, xw, xk, xv, xa, xg = [mix(m) for m in range(6)]
    r = _dot(xr.astype(BF16), wrkv_ref[0])
    k = _dot(xk.astype(BF16), wrkv_ref[1])
    v = _dot(xv.astype(BF16), wrkv_ref[2])
    g = _dot(jax.nn.sigmoid(_dot(xg.astype(BF16), g1_ref[...])).astype(BF16), g2_ref[...])
    kk = k * kk_ref[...]
    kk = kk * lax.rsqrt(jnp.maximum(_head_sum(kk * kk, ones_ref), 1e-24))
    r_ref[...] = r
    v_ref[...] = v
    a_ref[...] = -kk
    g_ref[...] = g
    k_sum = None
    xw_b = xw.astype(BF16)
    xa_b = xa.astype(BF16)
    for d in range(2):
        lw = d0_ref[d:d + 1, :] + _dot(jnp.tanh(_dot(xw_b, d1_ref[d])).astype(BF16), d2_ref[d])
        softplus = jnp.maximum(-lw, 0.0) + jnp.log(1.0 + jnp.exp(-jnp.abs(lw)))
        logw = -softplus - 0.5
        w_ref[d] = jnp.exp(-jnp.exp(logw))
        eta = jax.nn.sigmoid(i0_ref[d:d + 1, :] + _dot(_dot(xa_b, i1_ref[d]).astype(BF16), i2_ref[d]))
        k_d = k * (1.0 + (eta - 1.0) * ka_ref[...])
        k_ref[d] = k_d
        b_ref[d] = kk * eta
        k_sum = k_d if k_sum is None else k_sum + k_d
    bonus_ref[...] = _head_sum(r * k_sum * rk_ref[...], ones_ref) * v


def _rwkv_proj(h, mod_rows, batch, n_ctx, p):
    T, D = h.shape
    tm = PROJ_TILE
    per_tile = tm // batch
    seg_tiles = n_ctx // per_tile
    n_steps = T // batch
    row = pl.BlockSpec((tm, D), lambda i: (i, 0))
    before = pl.BlockSpec((batch, D), lambda i: (jnp.maximum(i * per_tile - 1, 0), 0))
    after = pl.BlockSpec((batch, D), lambda i: (jnp.minimum((i + 1) * per_tile, n_steps - 1), 0))
    mod_spec = pl.BlockSpec((1, 2, tm, D), lambda i: (jnp.minimum(i // seg_tiles, 1), 0, 0, 0))
    row2 = pl.BlockSpec((2, tm, D), lambda i: (0, i, 0))
    full = lambda a: pl.BlockSpec(a.shape, lambda i: (0,) * a.ndim)
    bf = lambda a: a.astype(BF16)
    consts = [p['mu'], bf(p['w_rkv']), bf(p['gate1']), bf(p['gate2']), bf(p['dec1']), bf(p['dec2']), p['dec0'],
              bf(p['icl1']), bf(p['icl2']), p['icl0'], p['k_k'].reshape(1, D), p['k_a'].reshape(1, D),
              p['r_k'].reshape(1, D), _seg_ones()]
    one = jax.ShapeDtypeStruct((T, D), F32)
    two = jax.ShapeDtypeStruct((2, T, D), F32)
    return pl.pallas_call(
        functools.partial(_rwkv_proj_kernel, seg_tiles),
        grid=(T // tm,),
        in_specs=[row, before, after, mod_spec] + [full(a) for a in consts],
        out_specs=[row, row, row, row, row, row2, row2, row2],
        out_shape=[one, one, one, one, one, two, two, two],
        compiler_params=_cparams(("arbitrary",)),
    )(h, h, h, mod_rows, *consts)


def _scan_kernel(r_ref, w_ref, k_ref, v_ref, a_ref, b_ref, ones_ref, hsel_ref,
                 y_ref, s_ref, vt_ref):
    d = pl.program_id(0)
    c = pl.program_id(2)
    tc, nb = r_ref.shape[0], r_ref.shape[1]
    tw = SCAN_TILE
    n_wide = r_ref.shape[2] // tw
    heads = tw // HEAD_DIM
    assert heads * tc == tw

    @pl.when(c == 0)
    def _():
        s_ref[...] = jnp.zeros_like(s_ref)

    for bb in range(nb):
        for q in range(n_wide):
            vt = v_ref[:, bb, q * tw:(q + 1) * tw].T
            vt_ref[bb * n_wide + q] = jnp.concatenate(
                [vt[h * HEAD_DIM:(h + 1) * HEAD_DIM] for h in range(heads)], axis=1)

    head_base = (lax.broadcasted_iota(I32, (HEAD_DIM, LANES), 1) // HEAD_DIM) * tc
    tiles = [(bb, q) for bb in range(nb) for q in range(n_wide)]
    groups = [tiles[i:i + SCAN_GROUP] for i in range(0, len(tiles), SCAN_GROUP)]

    def stacked(grp, get, dtype=F32):
        def wide(bb, q):
            return jnp.concatenate(
                [jnp.broadcast_to(get(bb, slice(q * tw + hf * LANES, q * tw + (hf + 1) * LANES)).astype(dtype),
                                  (HEAD_DIM, LANES)) for hf in range(tw // LANES)], axis=1)
        return jnp.concatenate([wide(bb, q) for bb, q in grp], axis=0)

    def load_state(grp):
        return jnp.concatenate([s_ref[bb * n_wide + q] for bb, q in grp], axis=0)

    def emit_y(grp, st_b, t_y):
        r_rows = stacked(grp, lambda bb, cols: r_ref[t_y, bb:bb + 1, cols], BF16)
        yh = _dot_nt(hsel_ref[...], st_b * r_rows)
        first = tiles.index(grp[0])
        y_ref[0, t_y, :, first * HEAD_DIM:(first + len(grp)) * HEAD_DIM] = yh[:heads]

    def step(s_i, carry):
        t = jnp.where(d == 0, s_i, tc - 1 - s_i)
        t_prev = jnp.where(s_i == 0, t, jnp.where(d == 0, t - 1, t + 1))
        pick = head_base + t
        for grp in groups:
            one = lambda ref: stacked(grp, lambda bb, cols: ref[t, bb:bb + 1, cols])
            two = lambda ref: stacked(grp, lambda bb, cols: ref[0, t, bb:bb + 1, cols])
            st = load_state(grp)
            st_b = st.astype(BF16)
            a_rows = stacked(grp, lambda bb, cols: a_ref[t, bb:bb + 1, cols], BF16)
            sa = _dot(st_b * a_rows, ones_ref[...])
            emit_y(grp, st_b, t_prev)
            vcol = jnp.concatenate(
                [jnp.concatenate([jnp.take_along_axis(vt_ref[bb * n_wide + q, :, hf * LANES:(hf + 1) * LANES],
                                                      pick, axis=1) for hf in range(tw // LANES)], axis=1)
                 for bb, q in grp], axis=0)
            st = st * two(w_ref) + sa * two(b_ref) + vcol * two(k_ref)
            for j, (bb, q) in enumerate(grp):
                s_ref[bb * n_wide + q] = st[j * HEAD_DIM:(j + 1) * HEAD_DIM]
        return carry

    lax.fori_loop(0, tc, step, 0, unroll=SCAN_UNROLL)
    t_last = jnp.where(d == 0, tc - 1, 0)
    for grp in groups:
        emit_y(grp, load_state(grp).astype(BF16), t_last)


def _wkv_scan(r, w, k, v, a, b, n_ctx):
    N, B, D = r.shape
    tc = SCAN_CHUNK
    wc = SCAN_COLS
    n_wide = wc // SCAN_TILE
    nc = N // tc
    ncc = n_ctx // tc

    def chunk(d, c):
        rev = jnp.where(c < ncc, ncc - 1 - c, nc - 1 - (c - ncc))
        return jnp.where(d == 0, c, rev)

    one = pl.BlockSpec((tc, B, wc), lambda d, g, c: (chunk(d, c), 0, g))
    two = pl.BlockSpec((1, tc, B, wc), lambda d, g, c: (d, chunk(d, c), 0, g))
    seg = np.arange(SCAN_TILE) // HEAD_DIM
    hsel = np.zeros((8, SCAN_TILE), np.float32)
    for hh in range(SCAN_TILE // HEAD_DIM):
        hsel[hh, seg == hh] = 1.0
    const = lambda a_: pl.BlockSpec(a_.shape, lambda d, g, c: (0, 0))
    consts = [_seg_ones(SCAN_TILE), jnp.asarray(hsel, BF16)]
    heads = SCAN_TILE // HEAD_DIM
    ncg = D // wc
    y = pl.pallas_call(
        _scan_kernel,
        grid=(2, ncg, nc),
        in_specs=[one, two, two, one, one, two] + [const(a_) for a_ in consts],
        out_specs=pl.BlockSpec((1, tc, heads, B * n_wide * HEAD_DIM), lambda d, g, c: (d, chunk(d, c), 0, g)),
        out_shape=jax.ShapeDtypeStruct((2, N, heads, ncg * B * n_wide * HEAD_DIM), F32),
        scratch_shapes=[pltpu.VMEM((B * n_wide, HEAD_DIM, SCAN_TILE), F32),
                        pltpu.VMEM((B * n_wide, HEAD_DIM, SCAN_TILE), F32)],
        compiler_params=_cparams(("arbitrary", "arbitrary", "arbitrary")),
    )(r, w, k, v, a, b, *consts)
    y = y.reshape(2, N, heads, ncg, B, n_wide * HEAD_DIM)
    return jnp.transpose(y, (0, 1, 4, 2, 3, 5)).reshape(2, N, B, D)


def _scan_head_order(d):
    heads = SCAN_TILE // HEAD_DIM
    n_wide = SCAN_COLS // SCAN_TILE
    ncg = d // SCAN_COLS
    order = []
    for g in range(ncg):
        for q in range(n_wide):
            for h in range(heads):
                order.append((h * ncg + g) * n_wide + q)
    return tuple(order)


def _rwkv_out_kernel(head_order, y0_ref, y1_ref, bonus_ref, g_ref, lnx_ref, ones_ref, w_ref, h_ref, mod_ref,
                     lg_ref, lb_ref, hn_ref, u_ref):
    y_in = y0_ref[0] + y1_ref[0]
    y = jnp.concatenate([y_in[:, p * HEAD_DIM:(p + 1) * HEAD_DIM] for p in head_order], axis=1)
    ym = _head_sum(y, ones_ref) * (1.0 / HEAD_DIM)
    yc = y - ym
    yv = _head_sum(yc * yc, ones_ref) * (1.0 / HEAD_DIM)
    yn = yc * lax.rsqrt(yv + LNX_EPS) * lnx_ref[0:1, :] + lnx_ref[1:2, :]
    x = ((yn + bonus_ref[...]) * g_ref[...]).astype(BF16)
    o = _dot(x, w_ref[...])
    z = DEEPNORM_ALPHA * h_ref[...] + mod_ref[0] * o
    hn = _layer_norm(z, lg_ref[...], lb_ref[...])
    hn_ref[...] = hn
    u_ref[...] = _pack_halves(hn * (1.0 + mod_ref[2]) + mod_ref[1])


def _rwkv_out(y, bonus, g, lnx, w_out, h, mod_rows, ln_g, ln_b, row0):
    T, D = h.shape
    tm = ROW_TILE
    t0 = row0 // tm
    off = pl.BlockSpec((tm, D), lambda i: (i + t0, 0))
    out = pl.BlockSpec((tm, D), lambda i: (i, 0))
    full = lambda a: pl.BlockSpec(a.shape, lambda i: (0,) * a.ndim)
    vec = pl.BlockSpec((1, D), lambda i: (0, 0))
    ones = _seg_ones()
    w_b = w_out.astype(BF16)
    return pl.pallas_call(
        functools.partial(_rwkv_out_kernel, _scan_head_order(D)),
        grid=((T - row0) // tm,),
        in_specs=[pl.BlockSpec((1, tm, D), lambda i: (0, i + t0, 0)),
                  pl.BlockSpec((1, tm, D), lambda i: (1, i + t0, 0)),
                  off, off, full(lnx), full(ones), full(w_b), off, full(mod_rows), vec, vec],
        out_specs=[out, pl.BlockSpec((tm, D // 2), lambda i: (i, 0))],
        out_shape=[jax.ShapeDtypeStruct((T - row0, D), F32), jax.ShapeDtypeStruct((T - row0, D // 2), I32)],
        compiler_params=_cparams(("arbitrary",)),
    )(y, y, bonus, g, lnx, ones, w_b, h, mod_rows, ln_g.reshape(1, D), ln_b.reshape(1, D))


def kernel(x, c, ctx, c_ctx, ada_w, ada_b, post_ln_g, post_ln_b, att_w_in, att_w_out, att_sink, diff_lambda_vecs, diff_subln_g, rk_mu, rk_w_rkv, rk_w_out, rk_decay0, rk_decay1, rk_decay2, rk_iclr0, rk_iclr1, rk_iclr2, rk_gate1, rk_gate2, rk_k_k, rk_k_a, rk_r_k, rk_lnx, moe_router, moe_bias, moe_w_in, moe_w_out, moe_ws_in, moe_ws_out):
    B, S, D = x.shape
    L = ctx.shape[1]
    N = L + S
    tm = ROW_TILE
    assert L % tm == 0 and S % tm == 0 and L % SCAN_CHUNK == 0 and S % SCAN_CHUNK == 0
    assert tm % B == 0 and PROJ_TILE % B == 0 and L % (PROJ_TILE // B) == 0 and D % SCAN_COLS == 0

    rows = -(-(B + 1) // 8) * 8
    cvec = jnp.concatenate([c, c_ctx[None, :], jnp.zeros((rows - B - 1, D), F32)], axis=0)
    mods = [_mod_table(_ada_mod(cvec, ada_w[i], ada_b[i]), B, D) for i in range(DEPTH)]

    h0 = jnp.concatenate([ctx, x], axis=1)
    lam_init = 0.8 - 0.6 * math.exp(-0.3 * 0)
    qa, ka, va, qb, kb, vb = _attn_inproj(h0, mods[0], att_w_in[0], L)
    oa = _win_attn(qa, ka, va, att_sink[0], L)
    ob = _diff_attn(qb, kb, vb, diff_lambda_vecs[0], diff_subln_g[0], lam_init, L)
    h1, u1 = _mix_out([oa, ob], [att_w_out[0][:A_WIDTH], att_w_out[0][A_WIDTH:]], h0, mods[0],
                      post_ln_g[0, 0], post_ln_b[0, 0], L, 0)
    tiles_b, tiles_c = N // tm, L // tm
    gate0 = mods[0][:, :, 5].reshape(B * 2, 1, D)
    gate0_index = lambda i: (i // tiles_b) * 2 + jnp.minimum((i % tiles_b) // tiles_c, 1)
    h2 = _moe_layer(u1.reshape(B * N, D // 2), h1.reshape(B * N, D), gate0, gate0_index, moe_router[0], moe_bias[0],
                    moe_w_in, moe_w_out, moe_ws_in[0], moe_ws_out[0],
                    post_ln_g[0, 1], post_ln_b[0, 1], 0).reshape(B, N, D)

    m_ctx, m_lat = mods[1][:, 0], mods[1][:, 1]
    h2_t = jnp.swapaxes(h2, 0, 1).reshape(N * B, D)
    rows_of = lambda m, j, n: jnp.tile(m[:, j], (n // B, 1))
    proj_mod = jnp.stack([jnp.stack([rows_of(m, 0, PROJ_TILE), rows_of(m, 1, PROJ_TILE)]) for m in (m_ctx, m_lat)])
    params = dict(mu=rk_mu[0], w_rkv=rk_w_rkv[0], gate1=rk_gate1[0], gate2=rk_gate2[0],
                  dec0=rk_decay0[0], dec1=rk_decay1[0], dec2=rk_decay2[0],
                  icl0=rk_iclr0[0], icl1=rk_iclr1[0], icl2=rk_iclr2[0],
                  k_k=rk_k_k[0], k_a=rk_k_a[0], r_k=rk_r_k[0])
    r, v, a, g, bonus, w2, k2, b2 = _rwkv_proj(h2_t, proj_mod, B, L, params)
    tmaj = lambda t: t.reshape(t.shape[:-2] + (N, B, D))
    y = _wkv_scan(tmaj(r), tmaj(w2), tmaj(k2), tmaj(v), tmaj(a), tmaj(b2), L)
    lat_rows = lambda j: rows_of(m_lat, j, tm)
    h3, u3 = _rwkv_out(y.reshape(2, N * B, D), bonus, g, rk_lnx[0], rk_w_out[0], h2_t,
                       jnp.stack([lat_rows(2), lat_rows(3), lat_rows(4)]),
                       post_ln_g[1, 0], post_ln_b[1, 0], L * B)
    out = _moe_layer(u3, h3, lat_rows(5)[None], lambda i: 0, moe_router[1], moe_bias[1],
                     moe_w_in, moe_w_out, moe_ws_in[1], moe_ws_out[1],
                     post_ln_g[1, 1], post_ln_b[1, 1], 1)
    return jnp.swapaxes(out.reshape(S, B, D), 0, 1)
```

```python
import functools
import math

import numpy as np
import jax
import jax.numpy as jnp
from jax import lax
from jax.experimental import pallas as pl
from jax.experimental.pallas import tpu as pltpu
from jax.experimental.pallas import tpu_sc as plsc

F32 = jnp.float32
BF16 = jnp.bfloat16
I32 = jnp.int32

HEAD_DIM = 64
GRID_W = 64
ROPE_AXIS_DIM = HEAD_DIM // 2
ROPE_THETA = 10000.0
Q_BLOCK = 128
A_Q_HEADS = 8
A_KV_HEADS = 2
A_GROUP = A_Q_HEADS // A_KV_HEADS
A_WIDTH = A_Q_HEADS * HEAD_DIM
A_KV_WIDTH = A_KV_HEADS * HEAD_DIM
B_HEADS = 4
B_V_DIM = 2 * HEAD_DIM
B_WIDTH = B_HEADS * B_V_DIM
LNX_EPS = 64e-5
N_EXPERTS = 256
TOP_K = 8
N_GROUPS = 8
TOPK_GROUPS = 4
ROUTED_SCALE = 2.5
MOE_BLOCK = 256
EXPERT_IN_SLOTS = 4
EXPERT_OUT_SLOTS = 2
LN_EPS = 1e-5
SUBLN_EPS = 1e-5
NEG_INF = -1e30
DEPTH = 2
DEEPNORM_ALPHA = (2 * DEPTH) ** 0.25

LANES = 128
ADA_COLS = 768
ROW_TILE = 256
PROJ_TILE = 128
SC_WINDOW = 128
SCAN_CHUNK = 64
SCAN_COLS = 1024
SCAN_TILE = 256
SCAN_GROUP = 8
SCAN_UNROLL = 8
VMEM_LIMIT = 56 * 1024 * 1024


def _cparams(sem):
    return pltpu.CompilerParams(dimension_semantics=sem, vmem_limit_bytes=VMEM_LIMIT)


def _silu(x):
    return x * jax.nn.sigmoid(x)


def _layer_norm(z, g, b):
    mu = jnp.mean(z, -1, keepdims=True)
    zc = z - mu
    var = jnp.mean(zc * zc, -1, keepdims=True)
    return zc * lax.rsqrt(var + LN_EPS) * g + b


def _dot(a, b):
    return jnp.dot(a, b, preferred_element_type=F32)


def _dot_nt(a, b):
    return lax.dot_general(a, b, (((1,), (1,)), ((), ())), preferred_element_type=F32)


def _pack_halves(x):
    half = x.shape[1] // 2
    bits = lambda v: lax.bitcast_convert_type(v.astype(BF16).astype(F32), I32)
    return lax.shift_right_logical(bits(x[:, :half]), 16) | bits(x[:, half:])


def _unpack_halves(p):
    lo = lax.bitcast_convert_type(lax.shift_left(p, 16), F32)
    hi = lax.bitcast_convert_type(p & jnp.int32(-65536), F32)
    return lo, hi


def _dot_halves(p, w_ref_or_array):
    lo, hi = _unpack_halves(p)
    half = p.shape[1]
    return _dot(lo.astype(BF16), w_ref_or_array[:half]) + _dot(hi.astype(BF16), w_ref_or_array[half:])


def _ada_kernel(c_ref, w_ref, b_ref, o_ref):
    c = c_ref[...]
    o_ref[...] = _dot(_silu(c).astype(BF16), w_ref[...].astype(BF16)) + b_ref[...]


def _ada_mod(cvec, w, bias):
    R, D = cvec.shape
    n_out = w.shape[1]
    tn = ADA_COLS
    return pl.pallas_call(
        _ada_kernel,
        grid=(n_out // tn,),
        in_specs=[pl.BlockSpec((R, D), lambda j: (0, 0)),
                  pl.BlockSpec((D, tn), lambda j: (0, j)),
                  pl.BlockSpec((1, tn), lambda j: (0, j))],
        out_specs=pl.BlockSpec((R, tn), lambda j: (0, j)),
        out_shape=jax.ShapeDtypeStruct((R, n_out), F32),
        compiler_params=_cparams(("arbitrary",)),
    )(cvec, w, bias.reshape(1, n_out))


def _mod_table(m, batch, d):
    m_lat = m[:batch].reshape(batch, 6, d)
    m_ctx = jnp.broadcast_to(m[batch].reshape(1, 6, d), (batch, 6, d))
    return jnp.stack([m_ctx, m_lat], axis=1)


def _mod_spec(d, ctx_tiles):
    return pl.BlockSpec((1, 1, 6, d), lambda b, i: (b, jnp.minimum(i // ctx_tiles, 1), 0, 0))


def _rope_tables(n_ctx, n_lat):
    rows = n_lat // GRID_W
    row = np.repeat(np.arange(rows), GRID_W).astype(np.float32)
    col = np.tile(np.arange(GRID_W), rows).astype(np.float32)
    inv = (ROPE_THETA ** (-np.arange(0, ROPE_AXIS_DIM, 2, dtype=np.float32) / ROPE_AXIS_DIM)).astype(np.float32)
    ar = row[:, None] * inv
    ac = col[:, None] * inv
    ang = np.concatenate([ar, ar, ac, ac], -1)
    cos = np.cos(ang).astype(np.float32)
    sin = np.sin(ang).astype(np.float32)
    lower = (np.arange(HEAD_DIM) % ROPE_AXIS_DIM) < (ROPE_AXIS_DIM // 2)
    sin_up = np.where(lower[None, :], -sin, 0.0)
    sin_dn = np.where(lower[None, :], 0.0, sin)

    def full(t, ctx_fill):
        t = np.concatenate([np.full((n_ctx, HEAD_DIM), ctx_fill, np.float32), t], 0)
        return jnp.asarray(np.tile(t, (1, LANES // HEAD_DIM)))

    return full(cos, 1.0), full(sin_up, 0.0), full(sin_dn, 0.0)


def _inproj_kernel(h_ref, mod_ref, w_ref, cos_ref, su_ref, sd_ref,
                   qa_ref, ka_ref, va_ref, qb_ref, kb_ref, vb_ref):
    h = h_ref[0]
    shift = mod_ref[0, 0, 0:1, :]
    scale = mod_ref[0, 0, 1:2, :]
    u = (h * (1.0 + scale) + shift).astype(BF16)
    y = _dot(u, w_ref[...])
    cos, s_up, s_dn = cos_ref[...], su_ref[...], sd_ref[...]
    q_scale = HEAD_DIM ** -0.5

    def rope(xc):
        half = ROPE_AXIS_DIM // 2
        return xc * cos + pltpu.roll(xc, LANES - half, 1) * s_up + pltpu.roll(xc, half, 1) * s_dn

    def emit(out_ref, col0, width, roped, mul):
        for j in range(width // LANES):
            xc = y[:, col0 + j * LANES: col0 + (j + 1) * LANES]
            if roped:
                xc = rope(xc)
            if mul != 1.0:
                xc = xc * mul
            out_ref[0, :, j * LANES:(j + 1) * LANES] = xc.astype(out_ref.dtype)

    c = 0
    emit(qa_ref, c, A_WIDTH, True, q_scale); c += A_WIDTH
    emit(ka_ref, c, A_KV_WIDTH, True, 1.0); c += A_KV_WIDTH
    emit(va_ref, c, A_KV_WIDTH, False, 1.0); c += A_KV_WIDTH
    emit(qb_ref, c, B_WIDTH, True, q_scale); c += B_WIDTH
    emit(kb_ref, c, B_WIDTH, True, 1.0); c += B_WIDTH
    emit(vb_ref, c, B_WIDTH, False, 1.0)


def _attn_inproj(h, mod, w_in, n_ctx):
    B, N, D = h.shape
    tm = ROW_TILE
    cos, s_up, s_dn = _rope_tables(n_ctx, N - n_ctx)
    widths = (A_WIDTH, A_KV_WIDTH, A_KV_WIDTH, B_WIDTH, B_WIDTH, B_WIDTH)
    tab_spec = pl.BlockSpec((tm, LANES), lambda b, i: (i, 0))
    return pl.pallas_call(
        _inproj_kernel,
        grid=(B, N // tm),
        in_specs=[pl.BlockSpec((1, tm, D), lambda b, i: (b, i, 0)),
                  _mod_spec(D, n_ctx // tm),
                  pl.BlockSpec(w_in.shape, lambda b, i: (0, 0)),
                  tab_spec, tab_spec, tab_spec],
        out_specs=[pl.BlockSpec((1, tm, w), lambda b, i: (b, i, 0)) for w in widths],
        out_shape=[jax.ShapeDtypeStruct((B, N, w), BF16) for w in widths],
        compiler_params=_cparams(("arbitrary", "arbitrary")),
    )(h, mod, w_in.astype(BF16), cos, s_up, s_dn)


def _win_attn_kernel(n_ctx_blocks, n_blocks, q_ref, kc_ref, vc_ref, kl_ref, km_ref, kr_ref,
                     vl_ref, vm_ref, vr_ref, sink_ref, o_ref):
    j = pl.program_id(1)
    is_lat = j >= n_ctx_blocks
    qb = Q_BLOCK
    n_c = kc_ref.shape[1]
    rows = A_GROUP * qb
    n_keys = n_c + 3 * qb
    far = 1 << 20
    r_idx = lax.broadcasted_iota(I32, (rows, n_keys), 0) % qb
    cw = lax.broadcasted_iota(I32, (rows, n_keys), 1) - n_c
    off_l = jnp.where(jnp.logical_and(is_lat, j > n_ctx_blocks), 0, far)
    end_m = jnp.where(is_lat, 2 * qb, qb)
    off_r = jnp.where(jnp.logical_and(is_lat, j < n_blocks - 1), 0, far)
    valid = ((cw < 0)
             | ((cw >= 0) & (cw < qb) & (cw >= r_idx + off_l))
             | ((cw >= qb) & (cw < end_m))
             | ((cw >= 2 * qb) & (cw - 2 * qb + off_r <= r_idx)))
    outs = []
    for kv in range(A_KV_HEADS):
        cols = slice(kv * HEAD_DIM, (kv + 1) * HEAD_DIM)
        k_all = jnp.concatenate([kc_ref[0, :, cols], kl_ref[0, :, cols], km_ref[0, :, cols],
                                 kr_ref[0, :, cols]], axis=0)
        v_all = jnp.concatenate([vc_ref[0, :, cols], vl_ref[0, :, cols], vm_ref[0, :, cols],
                                 vr_ref[0, :, cols]], axis=0)
        q0 = kv * A_GROUP
        q = jnp.concatenate([q_ref[0, :, (q0 + g) * HEAD_DIM:(q0 + g + 1) * HEAD_DIM]
                             for g in range(A_GROUP)], axis=0)
        sink = jnp.concatenate([jnp.broadcast_to(sink_ref[q0 + g:q0 + g + 1, 0:1], (qb, 1))
                                for g in range(A_GROUP)], axis=0)
        s = jnp.where(valid, _dot_nt(q, k_all), NEG_INF)
        m = jnp.maximum(jnp.max(s, -1, keepdims=True), sink)
        e = jnp.exp(s - m)
        denom = jnp.sum(e, -1, keepdims=True) + jnp.exp(sink - m)
        o = _dot(e.astype(BF16), v_all) * (1.0 / denom)
        outs += [o[g * qb:(g + 1) * qb] for g in range(A_GROUP)]
    for j2 in range(A_Q_HEADS // 2):
        pair = jnp.concatenate([outs[2 * j2], outs[2 * j2 + 1]], axis=1)
        o_ref[0, :, j2 * LANES:(j2 + 1) * LANES] = pair.astype(o_ref.dtype)


def _win_attn(qa, ka, va, sink, n_ctx):
    B, N, _ = qa.shape
    qb = Q_BLOCK
    nb = N // qb
    ncb = n_ctx // qb
    sink_pad = jnp.broadcast_to(sink.reshape(A_Q_HEADS, 1).astype(F32), (A_Q_HEADS, LANES))

    def left(b, j):
        return (b, jnp.clip(j - 1, ncb, nb - 1), 0)

    def mid(b, j):
        return (b, jnp.clip(j, ncb, nb - 1), 0)

    def right(b, j):
        return (b, jnp.clip(j + 1, ncb, nb - 1), 0)

    kv_blk = lambda im: pl.BlockSpec((1, qb, A_KV_WIDTH), im)
    ctx_blk = pl.BlockSpec((1, n_ctx, A_KV_WIDTH), lambda b, j: (b, 0, 0))
    return pl.pallas_call(
        functools.partial(_win_attn_kernel, ncb, nb),
        grid=(B, nb),
        in_specs=[pl.BlockSpec((1, qb, A_WIDTH), lambda b, j: (b, j, 0)),
                  ctx_blk, ctx_blk,
                  kv_blk(left), kv_blk(mid), kv_blk(right),
                  kv_blk(left), kv_blk(mid), kv_blk(right),
                  pl.BlockSpec((A_Q_HEADS, LANES), lambda b, j: (0, 0))],
        out_specs=pl.BlockSpec((1, qb, A_WIDTH), lambda b, j: (b, j, 0)),
        out_shape=jax.ShapeDtypeStruct((B, N, A_WIDTH), BF16),
        compiler_params=_cparams(("arbitrary", "arbitrary")),
    )(qa, ka, va, ka, ka, ka, va, va, va, sink_pad)


def _diff_attn_kernel(n_ctx, lam_init, q_ref, k_ref, v_ref, lv_ref, g_ref, o_ref):
    j = pl.program_id(1)
    lv = lv_ref[...]
    lam = (jnp.exp(jnp.sum(lv[0:1] * lv[1:2], -1, keepdims=True))
           - jnp.exp(jnp.sum(lv[2:3] * lv[3:4], -1, keepdims=True)) + lam_init)
    gain = g_ref[...] * (1.0 - lam_init)

    def run(n_keys):
        for hd in range(B_HEADS):
            parts = []
            for mm in range(2):
                c0 = (hd * 2 + mm) * HEAD_DIM
                q = q_ref[0, :, c0:c0 + HEAD_DIM]
                k = k_ref[0, :n_keys, c0:c0 + HEAD_DIM]
                s = _dot_nt(q, k)
                e = jnp.exp(s - jnp.max(s, -1, keepdims=True))
                parts.append((e, jnp.sum(e, -1, keepdims=True)))
            (e0, l0), (e1, l1) = parts
            v = v_ref[0, :n_keys, hd * B_V_DIM:(hd + 1) * B_V_DIM]
            o = _dot(e0.astype(BF16), v) * (1.0 / l0) - _dot(e1.astype(BF16), v) * (lam / l1)
            o = o * lax.rsqrt(jnp.mean(o * o, -1, keepdims=True) + SUBLN_EPS) * gain
            o_ref[0, :, hd * B_V_DIM:(hd + 1) * B_V_DIM] = o.astype(o_ref.dtype)

    @pl.when(j == 0)
    def _():
        run(n_ctx)

    @pl.when(j > 0)
    def _():
        run(k_ref.shape[1])


def _diff_attn(qb, kb, vb, lam_vecs, subln_g, lam_init, n_ctx):
    B, N, _ = qb.shape
    tq = n_ctx
    return pl.pallas_call(
        functools.partial(_diff_attn_kernel, n_ctx, lam_init),
        grid=(B, N // tq),
        in_specs=[pl.BlockSpec((1, tq, B_WIDTH), lambda b, j: (b, j, 0)),
                  pl.BlockSpec((1, N, B_WIDTH), lambda b, j: (b, 0, 0)),
                  pl.BlockSpec((1, N, B_WIDTH), lambda b, j: (b, 0, 0)),
                  pl.BlockSpec((4, HEAD_DIM), lambda b, j: (0, 0)),
                  pl.BlockSpec((1, B_V_DIM), lambda b, j: (0, 0))],
        out_specs=pl.BlockSpec((1, tq, B_WIDTH), lambda b, j: (b, j, 0)),
        out_shape=jax.ShapeDtypeStruct((B, N, B_WIDTH), BF16),
        compiler_params=_cparams(("arbitrary", "arbitrary")),
    )(qb, kb, vb, lam_vecs.astype(F32), subln_g.reshape(1, B_V_DIM).astype(F32))


def _mix_out_kernel(n_in, *refs):
    xs = refs[:n_in]
    ws = refs[n_in:2 * n_in]
    h_ref, mod_ref, g_ref, b_ref, hn_ref, u_ref = refs[2 * n_in:]
    o = _dot(xs[0][0], ws[0][...])
    for x_ref, w_ref in zip(xs[1:], ws[1:]):
        o = o + _dot(x_ref[0], w_ref[...])
    z = DEEPNORM_ALPHA * h_ref[0] + mod_ref[0, 0, 2:3, :] * o
    hn = _layer_norm(z, g_ref[...], b_ref[...])
    hn_ref[0] = hn
    u_ref[0] = _pack_halves(hn * (1.0 + mod_ref[0, 0, 4:5, :]) + mod_ref[0, 0, 3:4, :])


def _mix_out(xs, ws, h, mod, ln_g, ln_b, n_ctx, row0):
    B, N, D = h.shape
    tm = ROW_TILE
    t0 = row0 // tm
    n_out = N - row0
    row_spec = lambda w: pl.BlockSpec((1, tm, w), lambda b, i: (b, i + t0, 0))
    out_spec = pl.BlockSpec((1, tm, D), lambda b, i: (b, i, 0))
    vec_spec = pl.BlockSpec((1, D), lambda b, i: (0, 0))
    return pl.pallas_call(
        functools.partial(_mix_out_kernel, len(xs)),
        grid=(B, n_out // tm),
        in_specs=([row_spec(x.shape[-1]) for x in xs]
                  + [pl.BlockSpec(w.shape, lambda b, i: (0, 0)) for w in ws]
                  + [row_spec(D),
                     pl.BlockSpec((1, 1, 6, D), lambda b, i: (b, jnp.minimum((i + t0) // (n_ctx // tm), 1), 0, 0)),
                     vec_spec, vec_spec]),
        out_specs=[out_spec, pl.BlockSpec((1, tm, D // 2), lambda b, i: (b, i, 0))],
        out_shape=[jax.ShapeDtypeStruct((B, n_out, D), F32), jax.ShapeDtypeStruct((B, n_out, D // 2), I32)],
        compiler_params=_cparams(("arbitrary", "arbitrary")),
    )(*xs, *[w.astype(BF16) for w in ws], h, mod, ln_g.reshape(1, D), ln_b.reshape(1, D))


def _router_kernel(u_ref, rt_ref, bias_ref, tri_ref, e_ref, gw_ref, rank_ref, cnt_ref, carry_ref):
    i = pl.program_id(0)

    @pl.when(i == 0)
    def _():
        carry_ref[...] = jnp.zeros_like(carry_ref)

    tm = u_ref.shape[0]
    per_group = N_EXPERTS // N_GROUPS
    neg = -jnp.inf
    u_lo, u_hi = _unpack_halves(u_ref[...])
    half = u_ref.shape[1]
    logits = (_dot_nt(rt_ref[:, :half], u_lo.astype(BF16))
              + _dot_nt(rt_ref[:, half:], u_hi.astype(BF16)))
    scores = jax.nn.sigmoid(logits)
    sel = scores + bias_ref[...]
    io_in = lax.broadcasted_iota(I32, (per_group, tm), 0)
    grp_rows = []
    for gi in range(N_GROUPS):
        sg = sel[gi * per_group:(gi + 1) * per_group]
        m1 = jnp.max(sg, axis=0, keepdims=True)
        i1 = jnp.min(jnp.where(sg == m1, io_in, per_group), axis=0, keepdims=True)
        m2 = jnp.max(jnp.where(io_in == i1, neg, sg), axis=0, keepdims=True)
        grp_rows.append(m1 + m2)
    grp = jnp.concatenate(grp_rows, axis=0)
    io_g = lax.broadcasted_iota(I32, grp.shape, 0)
    g_sel = jnp.zeros(grp.shape, F32)
    for _ in range(TOPK_GROUPS):
        m = jnp.max(grp, axis=0, keepdims=True)
        hit = io_g == jnp.min(jnp.where(grp == m, io_g, N_GROUPS), axis=0, keepdims=True)
        g_sel = jnp.where(hit, 1.0, g_sel)
        grp = jnp.where(hit, neg, grp)
    selm = jnp.concatenate(
        [jnp.where(g_sel[gi:gi + 1] > 0.5, sel[gi * per_group:(gi + 1) * per_group], NEG_INF)
         for gi in range(N_GROUPS)], axis=0)
    io_e = lax.broadcasted_iota(I32, selm.shape, 0)
    chosen_f = jnp.zeros(selm.shape, F32)
    idx, gws = [], []
    for _ in range(TOP_K):
        m = jnp.max(selm, axis=0, keepdims=True)
        ik = jnp.min(jnp.where(selm == m, io_e, N_EXPERTS), axis=0, keepdims=True)
        hit = io_e == ik
        idx.append(ik)
        gws.append(jnp.sum(jnp.where(hit, scores, 0.0), axis=0, keepdims=True))
        chosen_f = jnp.where(hit, 1.0, chosen_f)
        selm = jnp.where(hit, neg, selm)
    gw = jnp.concatenate(gws, axis=0)
    gw_ref[...] = gw / jnp.sum(gw, axis=0, keepdims=True) * ROUTED_SCALE
    e_ref[...] = jnp.concatenate(idx, axis=0)
    before = _dot(chosen_f.astype(BF16), tri_ref[...]) + carry_ref[...]
    ranks = [jnp.sum(jnp.where(io_e == ik, before, 0.0), axis=0, keepdims=True) for ik in idx]
    rank_ref[...] = jnp.concatenate(ranks, axis=0).astype(I32)
    carry_ref[...] = carry_ref[...] + jnp.sum(chosen_f, axis=1, keepdims=True)
    cnt_ref[...] = carry_ref[...].astype(I32)


def _router(u, router, bias):
    T = u.shape[0]
    D = router.shape[0]
    tm = ROW_TILE
    tri = jnp.asarray(np.triu(np.ones((tm, tm), np.float32), 1), BF16)
    tok_spec = pl.BlockSpec((TOP_K, tm), lambda i: (0, i))
    return pl.pallas_call(
        _router_kernel,
        grid=(T // tm,),
        in_specs=[pl.BlockSpec((tm, D // 2), lambda i: (i, 0)),
                  pl.BlockSpec((N_EXPERTS, D), lambda i: (0, 0)),
                  pl.BlockSpec((N_EXPERTS, 1), lambda i: (0, 0)),
                  pl.BlockSpec((tm, tm), lambda i: (0, 0))],
        out_specs=[tok_spec, tok_spec, tok_spec, pl.BlockSpec((N_EXPERTS, 1), lambda i: (0, 0))],
        out_shape=[jax.ShapeDtypeStruct((TOP_K, T), I32), jax.ShapeDtypeStruct((TOP_K, T), F32),
                   jax.ShapeDtypeStruct((TOP_K, T), I32), jax.ShapeDtypeStruct((N_EXPERTS, 1), I32)],
        scratch_shapes=[pltpu.VMEM((N_EXPERTS, 1), F32)],
        compiler_params=_cparams(("arbitrary",)),
    )(u, router.T.astype(BF16), bias.reshape(N_EXPERTS, 1).astype(F32), tri)


def _sc_mesh():
    return plsc.VectorSubcoreMesh(core_axis_name="c", subcore_axis_name="s")


def _sc_scatter_rows(x, dest, n_rows):
    T, W = x.shape
    K = dest.shape[0]
    win = SC_WINDOW

    @functools.partial(pl.kernel, out_type=jax.ShapeDtypeStruct((n_rows, W), x.dtype), mesh=_sc_mesh(),
                       scratch_types=[])
    def scatter(x_hbm, i_hbm, o_hbm):
        def body(x_vmem, i_vmem):
            for k in range(K):
                pltpu.sync_copy(x_vmem, o_hbm.at[i_vmem.at[k]])

        pltpu.emit_pipeline(
            body,
            grid=(T // win,),
            in_specs=[pl.BlockSpec((win, W), lambda j: (j, 0), pipeline_mode=pl.Buffered(1)),
                      pl.BlockSpec((K, win), lambda j: (0, j))],
            out_specs=[],
            core_axis_name=("c", "s"),
            dimension_semantics=(pltpu.PARALLEL,),
        )(x_hbm, i_hbm)

    return scatter(x, dest)


def _sc_gather_rows(y, dest):
    K, T = dest.shape
    W = y.shape[1]
    win = SC_WINDOW

    @functools.partial(pl.kernel, out_type=jax.ShapeDtypeStruct((K * T, W), y.dtype), mesh=_sc_mesh(),
                       scratch_types=[])
    def gather(y_hbm, i_hbm, o_hbm):
        def body(i_vmem, o_vmem):
            pltpu.sync_copy(y_hbm.at[i_vmem.at[0]], o_vmem)

        pltpu.emit_pipeline(
            body,
            grid=(K * T // win,),
            in_specs=[pl.BlockSpec((1, win), lambda j: (0, j))],
            out_specs=[pl.BlockSpec((win, W), lambda j: (j, 0), pipeline_mode=pl.Buffered(1))],
            core_axis_name=("c", "s"),
            dimension_semantics=(pltpu.PARALLEL,),
        )(i_hbm, o_hbm)

    return gather(y, dest.reshape(1, K * T)).reshape(K, T, W)


def _expert_kernel(first_ref, count_ref, x_hbm, wi_ref, wo_ref, y_hbm, wi_b, wo_b, xbuf, ybuf, sem_in, sem_out):
    e = pl.program_id(0)
    n_in, blk = xbuf.shape[0], xbuf.shape[1]
    n_out = ybuf.shape[0]
    g0, g1 = first_ref[e], first_ref[e + 1]
    g_end = first_ref[pl.num_programs(0)]

    def x_copy(g):
        return pltpu.make_async_copy(x_hbm.at[pl.ds(g * blk, blk)], xbuf.at[g % n_in], sem_in.at[g % n_in])

    def y_copy(g):
        return pltpu.make_async_copy(ybuf.at[g % n_out], y_hbm.at[pl.ds(g * blk, blk)], sem_out.at[g % n_out])

    for ahead in range(n_in - 1):
        @pl.when(jnp.logical_and(e == 0, g_end > ahead))
        def _():
            x_copy(ahead).start()

    @pl.when(g1 > g0)
    def _():
        wi_b[...] = wi_ref[0, 0].astype(BF16)
        wo_b[...] = wo_ref[0, 0].astype(BF16)

    ff = wo_b.shape[0]

    def block(g, carry):
        x_copy(g).wait()

        @pl.when(g + n_in - 1 < g_end)
        def _():
            x_copy(g + n_in - 1).start()

        @pl.when(g >= n_out)
        def _():
            y_copy(g - n_out).wait()

        row = lax.broadcasted_iota(I32, (blk, xbuf.shape[2]), 0) + (g - g0) * blk
        hcat = _dot_halves(jnp.where(row < count_ref[e], xbuf[g % n_in], 0), wi_b)
        act = (_silu(hcat[:, :ff]) * hcat[:, ff:]).astype(BF16)
        ybuf[g % n_out] = _pack_halves(_dot(act, wo_b[...]))
        y_copy(g).start()
        return carry

    lax.fori_loop(g0, g1, block, 0)

    @pl.when(e == pl.num_programs(0) - 1)
    def _():
        for back in range(n_out, 0, -1):
            @pl.when(g_end >= back)
            def _():
                y_copy(g_end - back).wait()


def _experts(xs, first_block, counts, w_in, w_out, layer):
    P, half = xs.shape
    n_exp, D, ff2 = w_in.shape[-3:]
    return pl.pallas_call(
        _expert_kernel,
        grid_spec=pltpu.PrefetchScalarGridSpec(
            num_scalar_prefetch=2,
            grid=(n_exp,),
            in_specs=[pl.BlockSpec(memory_space=pl.ANY),
                      pl.BlockSpec((1, 1, D, ff2), lambda e, fb, cnt: (layer, e, 0, 0)),
                      pl.BlockSpec((1, 1, ff2 // 2, D), lambda e, fb, cnt: (layer, e, 0, 0))],
            out_specs=pl.BlockSpec(memory_space=pl.ANY),
            scratch_shapes=[pltpu.VMEM((D, ff2), BF16), pltpu.VMEM((ff2 // 2, D), BF16),
                            pltpu.VMEM((EXPERT_IN_SLOTS, MOE_BLOCK, half), I32),
                            pltpu.VMEM((EXPERT_OUT_SLOTS, MOE_BLOCK, half), I32),
                            pltpu.SemaphoreType.DMA((EXPERT_IN_SLOTS,)),
                            pltpu.SemaphoreType.DMA((EXPERT_OUT_SLOTS,))]),
        out_shape=jax.ShapeDtypeStruct((P, half), I32),
        compiler_params=_cparams(("arbitrary",)),
    )(first_block, counts.astype(I32), xs, w_in, w_out)


def _combine_kernel(yg_ref, gw_ref, u_ref, wsi_ref, wso_ref, h_ref, mod_ref, g_ref, b_ref, o_ref):
    ff = wso_ref.shape[0]
    hcat = _dot_halves(u_ref[...], wsi_ref)
    shared = _dot((_silu(hcat[:, :ff]) * hcat[:, ff:]).astype(BF16), wso_ref[...])
    lo, hi = None, None
    for k in range(TOP_K):
        y_lo, y_hi = _unpack_halves(yg_ref[k])
        gk = gw_ref[:, k:k + 1]
        lo = y_lo * gk if lo is None else lo + y_lo * gk
        hi = y_hi * gk if hi is None else hi + y_hi * gk
    routed = jnp.concatenate([lo, hi], axis=1)
    z = DEEPNORM_ALPHA * h_ref[...] + mod_ref[0] * (routed + shared)
    o_ref[...] = _layer_norm(z, g_ref[...], b_ref[...])


def _combine(yg, gw_t, u, ws_in, ws_out, h, gate, gate_index, ln_g, ln_b):
    T, D = h.shape
    tm = ROW_TILE
    vec_spec = pl.BlockSpec((1, D), lambda i: (0, 0))
    row_spec = pl.BlockSpec((tm, D), lambda i: (i, 0))
    packed_spec = pl.BlockSpec((tm, D // 2), lambda i: (i, 0))
    return pl.pallas_call(
        _combine_kernel,
        grid=(T // tm,),
        in_specs=[pl.BlockSpec((TOP_K, tm, D // 2), lambda i: (0, i, 0)),
                  pl.BlockSpec((tm, TOP_K), lambda i: (i, 0)),
                  packed_spec,
                  pl.BlockSpec(ws_in.shape, lambda i: (0, 0)),
                  pl.BlockSpec(ws_out.shape, lambda i: (0, 0)),
                  row_spec,
                  pl.BlockSpec((1,) + gate.shape[1:], lambda i: (gate_index(i), 0, 0)),
                  vec_spec, vec_spec],
        out_specs=row_spec,
        out_shape=jax.ShapeDtypeStruct((T, D), F32),
        compiler_params=_cparams(("arbitrary",)),
    )(yg, gw_t, u, ws_in.astype(BF16), ws_out.astype(BF16), h, gate,
      ln_g.reshape(1, D), ln_b.reshape(1, D))


def _slots_kernel(e_ref, rank_ref, start_ref, dest_ref):
    io_e = lax.broadcasted_iota(I32, (N_EXPERTS, e_ref.shape[1]), 0)
    rows = [jnp.sum(jnp.where(io_e == e_ref[k:k + 1, :], start_ref[...], 0), axis=0, keepdims=True)
            for k in range(TOP_K)]
    dest_ref[...] = jnp.concatenate(rows, axis=0) + rank_ref[...]


def _slots(eidx, rank, pstart):
    T = eidx.shape[1]
    tm = ROW_TILE
    tok_spec = pl.BlockSpec((TOP_K, tm), lambda i: (0, i))
    return pl.pallas_call(
        _slots_kernel,
        grid=(T // tm,),
        in_specs=[tok_spec, tok_spec, pl.BlockSpec((N_EXPERTS, 1), lambda i: (0, 0))],
        out_specs=tok_spec,
        out_shape=jax.ShapeDtypeStruct((TOP_K, T), I32),
        compiler_params=_cparams(("arbitrary",)),
    )(eidx, rank, pstart.reshape(N_EXPERTS, 1))


def _moe_layer(u, h, gate, gate_index, router, bias, w_in, w_out, ws_in, ws_out, ln_g, ln_b, layer):
    T = u.shape[0]
    eidx, gw, rank, counts = _router(u, router, bias)
    counts = counts[:, 0]
    padded = (counts + MOE_BLOCK - 1) // MOE_BLOCK * MOE_BLOCK
    pend = jnp.cumsum(padded)
    pstart = (pend - padded).astype(I32)
    dest = _slots(eidx, rank, pstart)
    n_blocks = -(-(T * TOP_K + N_EXPERTS * (MOE_BLOCK - 1)) // MOE_BLOCK)
    first_block = jnp.concatenate([jnp.zeros((1,), I32), (pend // MOE_BLOCK).astype(I32)])
    xs = _sc_scatter_rows(u, dest, n_blocks * MOE_BLOCK)
    y = _experts(xs, first_block, counts, w_in, w_out, layer)
    return _combine(_sc_gather_rows(y, dest), gw.T, u, ws_in, ws_out, h, gate, gate_index, ln_g, ln_b)


def _seg_ones(width=LANES):
    idx = np.arange(width) // HEAD_DIM
    return jnp.asarray((idx[:, None] == idx[None, :]).astype(np.float32), BF16)


def _head_sum(x, ones_ref):
    outs = []
    for j in range(x.shape[1] // LANES):
        xc = x[:, j * LANES:(j + 1) * LANES]
        hi = xc.astype(BF16)
        lo = (xc - hi.astype(F32)).astype(BF16)
        outs.append(_dot(hi, ones_ref[...]) + _dot(lo, ones_ref[...]))
    return jnp.concatenate(outs, axis=1)


def _rwkv_proj_kernel(seg_tiles, h_ref, hp_ref, hn_ref, mod_ref, mu_ref, wrkv_ref, g1_ref, g2_ref, d1_ref, d2_ref,
                      d0_ref, i1_ref, i2_ref, i0_ref, kk_ref, ka_ref, rk_ref, ones_ref,
                      r_ref, v_ref, a_ref, g_ref, bonus_ref, w_ref, k_ref, b_ref):
    i = pl.program_id(0)
    nb = hp_ref.shape[0]
    shift, scale = mod_ref[0, 0], mod_ref[0, 1]
    u = h_ref[...] * (1.0 + scale) + shift
    starts = jnp.logical_or(i == 0, i == seg_tiles)
    ends = jnp.logical_or(i == seg_tiles - 1, i == pl.num_programs(0) - 1)
    u_before = (hp_ref[...] * (1.0 + scale[:nb]) + shift[:nb]) * jnp.where(starts, 0.0, 1.0)
    u_after = (hn_ref[...] * (1.0 + scale[:nb]) + shift[:nb]) * jnp.where(ends, 0.0, 1.0)
    dx = 0.5 * (jnp.concatenate([u_before, u[:-nb]], axis=0) + jnp.concatenate([u[nb:], u_after], axis=0)) - u
    mix = lambda m: (u + dx * mu_ref[m:m + 1, :])
    xr, xw, xk, xv, xa, xg = [mix(m) for m in range(6)]
    r = _dot(xr.astype(BF16), wrkv_ref[0])
    k = _dot(xk.astype(BF16), wrkv_ref[1])
    v = _dot(xv.astype(BF16), wrkv_ref[2])
    g = _dot(jax.nn.sigmoid(_dot(xg.astype(BF16), g1_ref[...])).astype(BF16), g2_ref[...])
    kk = k * kk_ref[...]
    kk = kk * lax.rsqrt(jnp.maximum(_head_sum(kk * kk, ones_ref), 1e-24))
    r_ref[...] = r
    v_ref[...] = v
    a_ref[...] = -kk
    g_ref[...] = g
    k_sum = None
    xw_b = xw.astype(BF16)
    xa_b = xa.astype(BF16)
    for d in range(2):
        lw = d0_ref[d:d + 1, :] + _dot(jnp.tanh(_dot(xw_b, d1_ref[d])).astype(BF16), d2_ref[d])
        softplus = jnp.maximum(-lw, 0.0) + jnp.log(1.0 + jnp.exp(-jnp.abs(lw)))
        logw = -softplus - 0.5
        w_ref[d] = jnp.exp(-jnp.exp(logw))
        eta = jax.nn.sigmoid(i0_ref[d:d + 1, :] + _dot(_dot(xa_b, i1_ref[d]).astype(BF16), i2_ref[d]))
        k_d = k * (1.0 + (eta - 1.0) * ka_ref[...])
        k_ref[d] = k_d
        b_ref[d] = kk * eta
        k_sum = k_d if k_sum is None else k_sum + k_d
    bonus_ref[...] = _head_sum(r * k_sum * rk_ref[...], ones_ref) * v


def _rwkv_proj(h, mod_rows, batch, n_ctx, p):
    T, D = h.shape
    tm = PROJ_TILE
    per_tile = tm // batch
    seg_tiles = n_ctx // per_tile
    n_steps = T // batch
    row = pl.BlockSpec((tm, D), lambda i: (i, 0))
    before = pl.BlockSpec((batch, D), lambda i: (jnp.maximum(i * per_tile - 1, 0), 0))
    after = pl.BlockSpec((batch, D), lambda i: (jnp.minimum((i + 1) * per_tile, n_steps - 1), 0))
    mod_spec = pl.BlockSpec((1, 2, tm, D), lambda i: (jnp.minimum(i // seg_tiles, 1), 0, 0, 0))
    row2 = pl.BlockSpec((2, tm, D), lambda i: (0, i, 0))
    full = lambda a: pl.BlockSpec(a.shape, lambda i: (0,) * a.ndim)
    bf = lambda a: a.astype(BF16)
    consts = [p['mu'], bf(p['w_rkv']), bf(p['gate1']), bf(p['gate2']), bf(p['dec1']), bf(p['dec2']), p['dec0'],
              bf(p['icl1']), bf(p['icl2']), p['icl0'], p['k_k'].reshape(1, D), p['k_a'].reshape(1, D),
              p['r_k'].reshape(1, D), _seg_ones()]
    one = jax.ShapeDtypeStruct((T, D), F32)
    two = jax.ShapeDtypeStruct((2, T, D), F32)
    return pl.pallas_call(
        functools.partial(_rwkv_proj_kernel, seg_tiles),
        grid=(T // tm,),
        in_specs=[row, before, after, mod_spec] + [full(a) for a in consts],
        out_specs=[row, row, row, row, row, row2, row2, row2],
        out_shape=[one, one, one, one, one, two, two, two],
        compiler_params=_cparams(("arbitrary",)),
    )(h, h, h, mod_rows, *consts)


def _scan_kernel(r_ref, w_ref, k_ref, v_ref, a_ref, b_ref, ones_ref, hsel_ref,
                 y_ref, s_ref, vt_ref):
    d = pl.program_id(0)
    c = pl.program_id(2)
    tc, nb = r_ref.shape[0], r_ref.shape[1]
    tw = SCAN_TILE
    n_wide = r_ref.shape[2] // tw
    heads = tw // HEAD_DIM
    assert heads * tc == tw

    @pl.when(c == 0)
    def _():
        s_ref[...] = jnp.zeros_like(s_ref)

    for bb in range(nb):
        for q in range(n_wide):
            vt = v_ref[:, bb, q * tw:(q + 1) * tw].T
            vt_ref[bb * n_wide + q] = jnp.concatenate(
                [vt[h * HEAD_DIM:(h + 1) * HEAD_DIM] for h in range(heads)], axis=1)

    head_base = (lax.broadcasted_iota(I32, (HEAD_DIM, LANES), 1) // HEAD_DIM) * tc
    tiles = [(bb, q) for bb in range(nb) for q in range(n_wide)]
    groups = [tiles[i:i + SCAN_GROUP] for i in range(0, len(tiles), SCAN_GROUP)]

    def stacked(grp, get, dtype=F32):
        def wide(bb, q):
            return jnp.concatenate(
                [jnp.broadcast_to(get(bb, slice(q * tw + hf * LANES, q * tw + (hf + 1) * LANES)).astype(dtype),
                                  (HEAD_DIM, LANES)) for hf in range(tw // LANES)], axis=1)
        return jnp.concatenate([wide(bb, q) for bb, q in grp], axis=0)

    def load_state(grp):
        return jnp.concatenate([s_ref[bb * n_wide + q] for bb, q in grp], axis=0)

    def emit_y(grp, st_b, t_y):
        r_rows = stacked(grp, lambda bb, cols: r_ref[t_y, bb:bb + 1, cols], BF16)
        yh = _dot_nt(hsel_ref[...], st_b * r_rows)
        first = tiles.index(grp[0])
        y_ref[0, t_y, :, first * HEAD_DIM:(first + len(grp)) * HEAD_DIM] = yh[:heads]

    def step(s_i, carry):
        t = jnp.where(d == 0, s_i, tc - 1 - s_i)
        t_prev = jnp.where(s_i == 0, t, jnp.where(d == 0, t - 1, t + 1))
        pick = head_base + t
        for grp in groups:
            one = lambda ref: stacked(grp, lambda bb, cols: ref[t, bb:bb + 1, cols])
            two = lambda ref: stacked(grp, lambda bb, cols: ref[0, t, bb:bb + 1, cols])
            st = load_state(grp)
            st_b = st.astype(BF16)
            a_rows = stacked(grp, lambda bb, cols: a_ref[t, bb:bb + 1, cols], BF16)
            sa = _dot(st_b * a_rows, ones_ref[...])
            emit_y(grp, st_b, t_prev)
            vcol = jnp.concatenate(
                [jnp.concatenate([jnp.take_along_axis(vt_ref[bb * n_wide + q, :, hf * LANES:(hf + 1) * LANES],
                                                      pick, axis=1) for hf in range(tw // LANES)], axis=1)
                 for bb, q in grp], axis=0)
            st = st * two(w_ref) + sa * two(b_ref) + vcol * two(k_ref)
            for j, (bb, q) in enumerate(grp):
                s_ref[bb * n_wide + q] = st[j * HEAD_DIM:(j + 1) * HEAD_DIM]
        return carry

    lax.fori_loop(0, tc, step, 0, unroll=SCAN_UNROLL)
    t_last = jnp.where(d == 0, tc - 1, 0)
    for grp in groups:
        emit_y(grp, load_state(grp).astype(BF16), t_last)


def _wkv_scan(r, w, k, v, a, b, n_ctx):
    N, B, D = r.shape
    tc = SCAN_CHUNK
    wc = SCAN_COLS
    n_wide = wc // SCAN_TILE
    nc = N // tc
    ncc = n_ctx // tc

    def chunk(d, c):
        rev = jnp.where(c < ncc, ncc - 1 - c, nc - 1 - (c - ncc))
        return jnp.where(d == 0, c, rev)

    one = pl.BlockSpec((tc, B, wc), lambda d, g, c: (chunk(d, c), 0, g))
    two = pl.BlockSpec((1, tc, B, wc), lambda d, g, c: (d, chunk(d, c), 0, g))
    seg = np.arange(SCAN_TILE) // HEAD_DIM
    hsel = np.zeros((8, SCAN_TILE), np.float32)
    for hh in range(SCAN_TILE // HEAD_DIM):
        hsel[hh, seg == hh] = 1.0
    const = lambda a_: pl.BlockSpec(a_.shape, lambda d, g, c: (0, 0))
    consts = [_seg_ones(SCAN_TILE), jnp.asarray(hsel, BF16)]
    heads = SCAN_TILE // HEAD_DIM
    ncg = D // wc
    y = pl.pallas_call(
        _scan_kernel,
        grid=(2, ncg, nc),
        in_specs=[one, two, two, one, one, two] + [const(a_) for a_ in consts],
        out_specs=pl.BlockSpec((1, tc, heads, B * n_wide * HEAD_DIM), lambda d, g, c: (d, chunk(d, c), 0, g)),
        out_shape=jax.ShapeDtypeStruct((2, N, heads, ncg * B * n_wide * HEAD_DIM), F32),
        scratch_shapes=[pltpu.VMEM((B * n_wide, HEAD_DIM, SCAN_TILE), F32),
                        pltpu.VMEM((B * n_wide, HEAD_DIM, SCAN_TILE), F32)],
        compiler_params=_cparams(("arbitrary", "arbitrary", "arbitrary")),
    )(r, w, k, v, a, b, *consts)
    y = y.reshape(2, N, heads, ncg, B, n_wide * HEAD_DIM)
    return jnp.transpose(y, (0, 1, 4, 2, 3, 5)).reshape(2, N, B, D)


def _scan_head_order(d):
    heads = SCAN_TILE // HEAD_DIM
    n_wide = SCAN_COLS // SCAN_TILE
    ncg = d // SCAN_COLS
    order = []
    for g in range(ncg):
        for q in range(n_wide):
            for h in range(heads):
                order.append((h * ncg + g) * n_wide + q)
    return tuple(order)


def _rwkv_out_kernel(head_order, y0_ref, y1_ref, bonus_ref, g_ref, lnx_ref, ones_ref, w_ref, h_ref, mod_ref,
                     lg_ref, lb_ref, hn_ref, u_ref):
    y_in = y0_ref[0] + y1_ref[0]
    y = jnp.concatenate([y_in[:, p * HEAD_DIM:(p + 1) * HEAD_DIM] for p in head_order], axis=1)
    ym = _head_sum(y, ones_ref) * (1.0 / HEAD_DIM)
    yc = y - ym
    yv = _head_sum(yc * yc, ones_ref) * (1.0 / HEAD_DIM)
    yn = yc * lax.rsqrt(yv + LNX_EPS) * lnx_ref[0:1, :] + lnx_ref[1:2, :]
    x = ((yn + bonus_ref[...]) * g_ref[...]).astype(BF16)
    o = _dot(x, w_ref[...])
    z = DEEPNORM_ALPHA * h_ref[...] + mod_ref[0] * o
    hn = _layer_norm(z, lg_ref[...], lb_ref[...])
    hn_ref[...] = hn
    u_ref[...] = _pack_halves(hn * (1.0 + mod_ref[2]) + mod_ref[1])


def _rwkv_out(y, bonus, g, lnx, w_out, h, mod_rows, ln_g, ln_b, row0):
    T, D = h.shape
    tm = ROW_TILE
    t0 = row0 // tm
    off = pl.BlockSpec((tm, D), lambda i: (i + t0, 0))
    out = pl.BlockSpec((tm, D), lambda i: (i, 0))
    full = lambda a: pl.BlockSpec(a.shape, lambda i: (0,) * a.ndim)
    vec = pl.BlockSpec((1, D), lambda i: (0, 0))
    ones = _seg_ones()
    w_b = w_out.astype(BF16)
    return pl.pallas_call(
        functools.partial(_rwkv_out_kernel, _scan_head_order(D)),
        grid=((T - row0) // tm,),
        in_specs=[pl.BlockSpec((1, tm, D), lambda i: (0, i + t0, 0)),
                  pl.BlockSpec((1, tm, D), lambda i: (1, i + t0, 0)),
                  off, off, full(lnx), full(ones), full(w_b), off, full(mod_rows), vec, vec],
        out_specs=[out, pl.BlockSpec((tm, D // 2), lambda i: (i, 0))],
        out_shape=[jax.ShapeDtypeStruct((T - row0, D), F32), jax.ShapeDtypeStruct((T - row0, D // 2), I32)],
        compiler_params=_cparams(("arbitrary",)),
    )(y, y, bonus, g, lnx, ones, w_b, h, mod_rows, ln_g.reshape(1, D), ln_b.reshape(1, D))


def kernel(x, c, ctx, c_ctx, ada_w, ada_b, post_ln_g, post_ln_b, att_w_in, att_w_out, att_sink, diff_lambda_vecs, diff_subln_g, rk_mu, rk_w_rkv, rk_w_out, rk_decay0, rk_decay1, rk_decay2, rk_iclr0, rk_iclr1, rk_iclr2, rk_gate1, rk_gate2, rk_k_k, rk_k_a, rk_r_k, rk_lnx, moe_router, moe_bias, moe_w_in, moe_w_out, moe_ws_in, moe_ws_out):
    B, S, D = x.shape
    L = ctx.shape[1]
    N = L + S
    tm = ROW_TILE
    assert L % tm == 0 and S % tm == 0 and L % SCAN_CHUNK == 0 and S % SCAN_CHUNK == 0
    assert tm % B == 0 and PROJ_TILE % B == 0 and L % (PROJ_TILE // B) == 0 and D % SCAN_COLS == 0

    rows = -(-(B + 1) // 8) * 8
    cvec = jnp.concatenate([c, c_ctx[None, :], jnp.zeros((rows - B - 1, D), F32)], axis=0)
    mods = [_mod_table(_ada_mod(cvec, ada_w[i], ada_b[i]), B, D) for i in range(DEPTH)]

    h0 = jnp.concatenate([ctx, x], axis=1)
    lam_init = 0.8 - 0.6 * math.exp(-0.3 * 0)
    qa, ka, va, qb, kb, vb = _attn_inproj(h0, mods[0], att_w_in[0], L)
    oa = _win_attn(qa, ka, va, att_sink[0], L)
    ob = _diff_attn(qb, kb, vb, diff_lambda_vecs[0], diff_subln_g[0], lam_init, L)
    h1, u1 = _mix_out([oa, ob], [att_w_out[0][:A_WIDTH], att_w_out[0][A_WIDTH:]], h0, mods[0],
                      post_ln_g[0, 0], post_ln_b[0, 0], L, 0)
    tiles_b, tiles_c = N // tm, L // tm
    gate0 = mods[0][:, :, 5].reshape(B * 2, 1, D)
    gate0_index = lambda i: (i // tiles_b) * 2 + jnp.minimum((i % tiles_b) // tiles_c, 1)
    h2 = _moe_layer(u1.reshape(B * N, D // 2), h1.reshape(B * N, D), gate0, gate0_index, moe_router[0], moe_bias[0],
                    moe_w_in, moe_w_out, moe_ws_in[0], moe_ws_out[0],
                    post_ln_g[0, 1], post_ln_b[0, 1], 0).reshape(B, N, D)

    m_ctx, m_lat = mods[1][:, 0], mods[1][:, 1]
    h2_t = jnp.swapaxes(h2, 0, 1).reshape(N * B, D)
    rows_of = lambda m, j, n: jnp.tile(m[:, j], (n // B, 1))
    proj_mod = jnp.stack([jnp.stack([rows_of(m, 0, PROJ_TILE), rows_of(m, 1, PROJ_TILE)]) for m in (m_ctx, m_lat)])
    params = dict(mu=rk_mu[0], w_rkv=rk_w_rkv[0], gate1=rk_gate1[0], gate2=rk_gate2[0],
                  dec0=rk_decay0[0], dec1=rk_decay1[0], dec2=rk_decay2[0],
                  icl0=rk_iclr0[0], icl1=rk_iclr1[0], icl2=rk_iclr2[0],
                  k_k=rk_k_k[0], k_a=rk_k_a[0], r_k=rk_r_k[0])
    r, v, a, g, bonus, w2, k2, b2 = _rwkv_proj(h2_t, proj_mod, B, L, params)
    tmaj = lambda t: t.reshape(t.shape[:-2] + (N, B, D))
    y = _wkv_scan(tmaj(r), tmaj(w2), tmaj(k2), tmaj(v), tmaj(a), tmaj(b2), L)
    lat_rows = lambda j: rows_of(m_lat, j, tm)
    h3, u3 = _rwkv_out(y.reshape(2, N * B, D), bonus, g, rk_lnx[0], rk_w_out[0], h2_t,
                       jnp.stack([lat_rows(2), lat_rows(3), lat_rows(4)]),
                       post_ln_g[1, 0], post_ln_b[1, 0], L * B)
    out = _moe_layer(u3, h3, lat_rows(5)[None], lambda i: 0, moe_router[1], moe_bias[1],
                     moe_w_in, moe_w_out, moe_ws_in[1], moe_ws_out[1],
                     post_ln_g[1, 1], post_ln_b[1, 1], 1)
    return jnp.swapaxes(out.reshape(S, B, D), 0, 1)
```

```python
import functools
import math

import numpy as np
import jax
import jax.numpy as jnp
from jax import lax
from jax.experimental import pallas as pl
from jax.experimental.pallas import tpu as pltpu
from jax.experimental.pallas import tpu_sc as plsc

F32 = jnp.float32
BF16 = jnp.bfloat16
I32 = jnp.int32

HEAD_DIM = 64
GRID_W = 64
ROPE_AXIS_DIM = HEAD_DIM // 2
ROPE_THETA = 10000.0
Q_BLOCK = 128
A_Q_HEADS = 8
A_KV_HEADS = 2
A_GROUP = A_Q_HEADS // A_KV_HEADS
A_WIDTH = A_Q_HEADS * HEAD_DIM
A_KV_WIDTH = A_KV_HEADS * HEAD_DIM
B_HEADS = 4
B_V_DIM = 2 * HEAD_DIM
B_WIDTH = B_HEADS * B_V_DIM
LNX_EPS = 64e-5
N_EXPERTS = 256
TOP_K = 8
N_GROUPS = 8
TOPK_GROUPS = 4
ROUTED_SCALE = 2.5
MOE_BLOCK = 256
EXPERT_IN_SLOTS = 4
EXPERT_OUT_SLOTS = 2
LN_EPS = 1e-5
SUBLN_EPS = 1e-5
NEG_INF = -1e30
DEPTH = 2
DEEPNORM_ALPHA = (2 * DEPTH) ** 0.25

LANES = 128
ADA_COLS = 768
ROW_TILE = 256
PROJ_TILE = 256
SC_WINDOW = 128
SCAN_CHUNK = 64
SCAN_COLS = 1024
SCAN_TILE = 256
SCAN_GROUP = 8
SCAN_UNROLL = 8
VMEM_LIMIT = 56 * 1024 * 1024


def _cparams(sem):
    return pltpu.CompilerParams(dimension_semantics=sem, vmem_limit_bytes=VMEM_LIMIT)


def _silu(x):
    return x * jax.nn.sigmoid(x)


def _layer_norm(z, g, b):
    mu = jnp.mean(z, -1, keepdims=True)
    zc = z - mu
    var = jnp.mean(zc * zc, -1, keepdims=True)
    return zc * lax.rsqrt(var + LN_EPS) * g + b


def _dot(a, b):
    return jnp.dot(a, b, preferred_element_type=F32)


def _dot_nt(a, b):
    return lax.dot_general(a, b, (((1,), (1,)), ((), ())), preferred_element_type=F32)


def _pack_halves(x):
    half = x.shape[1] // 2
    bits = lambda v: lax.bitcast_convert_type(v.astype(BF16).astype(F32), I32)
    return lax.shift_right_logical(bits(x[:, :half]), 16) | bits(x[:, half:])


def _unpack_halves(p):
    lo = lax.bitcast_convert_type(lax.shift_left(p, 16), F32)
    hi = lax.bitcast_convert_type(p & jnp.int32(-65536), F32)
    return lo, hi


def _dot_halves(p, w_ref_or_array):
    lo, hi = _unpack_halves(p)
    half = p.shape[1]
    return _dot(lo.astype(BF16), w_ref_or_array[:half]) + _dot(hi.astype(BF16), w_ref_or_array[half:])


def _ada_kernel(c_ref, w_ref, b_ref, o_ref):
    c = c_ref[...]
    o_ref[...] = _dot(_silu(c).astype(BF16), w_ref[...].astype(BF16)) + b_ref[...]


def _ada_mod(cvec, w, bias):
    R, D = cvec.shape
    n_out = w.shape[1]
    tn = ADA_COLS
    return pl.pallas_call(
        _ada_kernel,
        grid=(n_out // tn,),
        in_specs=[pl.BlockSpec((R, D), lambda j: (0, 0)),
                  pl.BlockSpec((D, tn), lambda j: (0, j)),
                  pl.BlockSpec((1, tn), lambda j: (0, j))],
        out_specs=pl.BlockSpec((R, tn), lambda j: (0, j)),
        out_shape=jax.ShapeDtypeStruct((R, n_out), F32),
        compiler_params=_cparams(("arbitrary",)),
    )(cvec, w, bias.reshape(1, n_out))


def _mod_table(m, batch, d):
    m_lat = m[:batch].reshape(batch, 6, d)
    m_ctx = jnp.broadcast_to(m[batch].reshape(1, 6, d), (batch, 6, d))
    return jnp.stack([m_ctx, m_lat], axis=1)


def _mod_spec(d, ctx_tiles):
    return pl.BlockSpec((1, 1, 6, d), lambda b, i: (b, jnp.minimum(i // ctx_tiles, 1), 0, 0))


def _rope_tables(n_ctx, n_lat):
    rows = n_lat // GRID_W
    row = np.repeat(np.arange(rows), GRID_W).astype(np.float32)
    col = np.tile(np.arange(GRID_W), rows).astype(np.float32)
    inv = (ROPE_THETA ** (-np.arange(0, ROPE_AXIS_DIM, 2, dtype=np.float32) / ROPE_AXIS_DIM)).astype(np.float32)
    ar = row[:, None] * inv
    ac = col[:, None] * inv
    ang = np.concatenate([ar, ar, ac, ac], -1)
    cos = np.cos(ang).astype(np.float32)
    sin = np.sin(ang).astype(np.float32)
    lower = (np.arange(HEAD_DIM) % ROPE_AXIS_DIM) < (ROPE_AXIS_DIM // 2)
    sin_up = np.where(lower[None, :], -sin, 0.0)
    sin_dn = np.where(lower[None, :], 0.0, sin)

    def full(t, ctx_fill):
        t = np.concatenate([np.full((n_ctx, HEAD_DIM), ctx_fill, np.float32), t], 0)
        return jnp.asarray(np.tile(t, (1, LANES // HEAD_DIM)))

    return full(cos, 1.0), full(sin_up, 0.0), full(sin_dn, 0.0)


def _inproj_kernel(h_ref, mod_ref, w_ref, cos_ref, su_ref, sd_ref,
                   qa_ref, ka_ref, va_ref, qb_ref, kb_ref, vb_ref):
    h = h_ref[0]
    shift = mod_ref[0, 0, 0:1, :]
    scale = mod_ref[0, 0, 1:2, :]
    u = (h * (1.0 + scale) + shift).astype(BF16)
    y = _dot(u, w_ref[...])
    cos, s_up, s_dn = cos_ref[...], su_ref[...], sd_ref[...]
    q_scale = HEAD_DIM ** -0.5

    def rope(xc):
        half = ROPE_AXIS_DIM // 2
        return xc * cos + pltpu.roll(xc, LANES - half, 1) * s_up + pltpu.roll(xc, half, 1) * s_dn

    def emit(out_ref, col0, width, roped, mul):
        for j in range(width // LANES):
            xc = y[:, col0 + j * LANES: col0 + (j + 1) * LANES]
            if roped:
                xc = rope(xc)
            if mul != 1.0:
                xc = xc * mul
            out_ref[0, :, j * LANES:(j + 1) * LANES] = xc.astype(out_ref.dtype)

    c = 0
    emit(qa_ref, c, A_WIDTH, True, q_scale); c += A_WIDTH
    emit(ka_ref, c, A_KV_WIDTH, True, 1.0); c += A_KV_WIDTH
    emit(va_ref, c, A_KV_WIDTH, False, 1.0); c += A_KV_WIDTH
    emit(qb_ref, c, B_WIDTH, True, q_scale); c += B_WIDTH
    emit(kb_ref, c, B_WIDTH, True, 1.0); c += B_WIDTH
    emit(vb_ref, c, B_WIDTH, False, 1.0)


def _attn_inproj(h, mod, w_in, n_ctx):
    B, N, D = h.shape
    tm = ROW_TILE
    cos, s_up, s_dn = _rope_tables(n_ctx, N - n_ctx)
    widths = (A_WIDTH, A_KV_WIDTH, A_KV_WIDTH, B_WIDTH, B_WIDTH, B_WIDTH)
    tab_spec = pl.BlockSpec((tm, LANES), lambda b, i: (i, 0))
    return pl.pallas_call(
        _inproj_kernel,
        grid=(B, N // tm),
        in_specs=[pl.BlockSpec((1, tm, D), lambda b, i: (b, i, 0)),
                  _mod_spec(D, n_ctx // tm),
                  pl.BlockSpec(w_in.shape, lambda b, i: (0, 0)),
                  tab_spec, tab_spec, tab_spec],
        out_specs=[pl.BlockSpec((1, tm, w), lambda b, i: (b, i, 0)) for w in widths],
        out_shape=[jax.ShapeDtypeStruct((B, N, w), BF16) for w in widths],
        compiler_params=_cparams(("arbitrary", "arbitrary")),
    )(h, mod, w_in.astype(BF16), cos, s_up, s_dn)


def _win_attn_kernel(n_ctx_blocks, n_blocks, q_ref, kc_ref, vc_ref, kl_ref, km_ref, kr_ref,
                     vl_ref, vm_ref, vr_ref, sink_ref, o_ref):
    j = pl.program_id(1)
    is_lat = j >= n_ctx_blocks
    qb = Q_BLOCK
    n_c = kc_ref.shape[1]
    rows = A_GROUP * qb
    n_keys = n_c + 3 * qb
    far = 1 << 20
    r_idx = lax.broadcasted_iota(I32, (rows, n_keys), 0) % qb
    cw = lax.broadcasted_iota(I32, (rows, n_keys), 1) - n_c
    off_l = jnp.where(jnp.logical_and(is_lat, j > n_ctx_blocks), 0, far)
    end_m = jnp.where(is_lat, 2 * qb, qb)
    off_r = jnp.where(jnp.logical_and(is_lat, j < n_blocks - 1), 0, far)
    valid = ((cw < 0)
             | ((cw >= 0) & (cw < qb) & (cw >= r_idx + off_l))
             | ((cw >= qb) & (cw < end_m))
             | ((cw >= 2 * qb) & (cw - 2 * qb + off_r <= r_idx)))
    outs = []
    for kv in range(A_KV_HEADS):
        cols = slice(kv * HEAD_DIM, (kv + 1) * HEAD_DIM)
        k_all = jnp.concatenate([kc_ref[0, :, cols], kl_ref[0, :, cols], km_ref[0, :, cols],
                                 kr_ref[0, :, cols]], axis=0)
        v_all = jnp.concatenate([vc_ref[0, :, cols], vl_ref[0, :, cols], vm_ref[0, :, cols],
                                 vr_ref[0, :, cols]], axis=0)
        q0 = kv * A_GROUP
        q = jnp.concatenate([q_ref[0, :, (q0 + g) * HEAD_DIM:(q0 + g + 1) * HEAD_DIM]
                             for g in range(A_GROUP)], axis=0)
        sink = jnp.concatenate([jnp.broadcast_to(sink_ref[q0 + g:q0 + g + 1, 0:1], (qb, 1))
                                for g in range(A_GROUP)], axis=0)
        s = jnp.where(valid, _dot_nt(q, k_all), NEG_INF)
        m = jnp.maximum(jnp.max(s, -1, keepdims=True), sink)
        e = jnp.exp(s - m)
        denom = jnp.sum(e, -1, keepdims=True) + jnp.exp(sink - m)
        o = _dot(e.astype(BF16), v_all) * (1.0 / denom)
        outs += [o[g * qb:(g + 1) * qb] for g in range(A_GROUP)]
    for j2 in range(A_Q_HEADS // 2):
        pair = jnp.concatenate([outs[2 * j2], outs[2 * j2 + 1]], axis=1)
        o_ref[0, :, j2 * LANES:(j2 + 1) * LANES] = pair.astype(o_ref.dtype)


def _win_attn(qa, ka, va, sink, n_ctx):
    B, N, _ = qa.shape
    qb = Q_BLOCK
    nb = N // qb
    ncb = n_ctx // qb
    sink_pad = jnp.broadcast_to(sink.reshape(A_Q_HEADS, 1).astype(F32), (A_Q_HEADS, LANES))

    def left(b, j):
        return (b, jnp.clip(j - 1, ncb, nb - 1), 0)

    def mid(b, j):
        return (b, jnp.clip(j, ncb, nb - 1), 0)

    def right(b, j):
        return (b, jnp.clip(j + 1, ncb, nb - 1), 0)

    kv_blk = lambda im: pl.BlockSpec((1, qb, A_KV_WIDTH), im)
    ctx_blk = pl.BlockSpec((1, n_ctx, A_KV_WIDTH), lambda b, j: (b, 0, 0))
    return pl.pallas_call(
        functools.partial(_win_attn_kernel, ncb, nb),
        grid=(B, nb),
        in_specs=[pl.BlockSpec((1, qb, A_WIDTH), lambda b, j: (b, j, 0)),
                  ctx_blk, ctx_blk,
                  kv_blk(left), kv_blk(mid), kv_blk(right),
                  kv_blk(left), kv_blk(mid), kv_blk(right),
                  pl.BlockSpec((A_Q_HEADS, LANES), lambda b, j: (0, 0))],
        out_specs=pl.BlockSpec((1, qb, A_WIDTH), lambda b, j: (b, j, 0)),
        out_shape=jax.ShapeDtypeStruct((B, N, A_WIDTH), BF16),
        compiler_params=_cparams(("arbitrary", "arbitrary")),
    )(qa, ka, va, ka, ka, ka, va, va, va, sink_pad)


def _diff_attn_kernel(n_ctx, lam_init, q_ref, k_ref, v_ref, lv_ref, g_ref, o_ref):
    j = pl.program_id(1)
    lv = lv_ref[...]
    lam = (jnp.exp(jnp.sum(lv[0:1] * lv[1:2], -1, keepdims=True))
           - jnp.exp(jnp.sum(lv[2:3] * lv[3:4], -1, keepdims=True)) + lam_init)
    gain = g_ref[...] * (1.0 - lam_init)

    def run(n_keys):
        for hd in range(B_HEADS):
            parts = []
            for mm in range(2):
                c0 = (hd * 2 + mm) * HEAD_DIM
                q = q_ref[0, :, c0:c0 + HEAD_DIM]
                k = k_ref[0, :n_keys, c0:c0 + HEAD_DIM]
                s = _dot_nt(q, k)
                e = jnp.exp(s - jnp.max(s, -1, keepdims=True))
                parts.append((e, jnp.sum(e, -1, keepdims=True)))
            (e0, l0), (e1, l1) = parts
            v = v_ref[0, :n_keys, hd * B_V_DIM:(hd + 1) * B_V_DIM]
            o = _dot(e0.astype(BF16), v) * (1.0 / l0) - _dot(e1.astype(BF16), v) * (lam / l1)
            o = o * lax.rsqrt(jnp.mean(o * o, -1, keepdims=True) + SUBLN_EPS) * gain
            o_ref[0, :, hd * B_V_DIM:(hd + 1) * B_V_DIM] = o.astype(o_ref.dtype)

    @pl.when(j == 0)
    def _():
        run(n_ctx)

    @pl.when(j > 0)
    def _():
        run(k_ref.shape[1])


def _diff_attn(qb, kb, vb, lam_vecs, subln_g, lam_init, n_ctx):
    B, N, _ = qb.shape
    tq = n_ctx
    return pl.pallas_call(
        functools.partial(_diff_attn_kernel, n_ctx, lam_init),
        grid=(B, N // tq),
        in_specs=[pl.BlockSpec((1, tq, B_WIDTH), lambda b, j: (b, j, 0)),
                  pl.BlockSpec((1, N, B_WIDTH), lambda b, j: (b, 0, 0)),
                  pl.BlockSpec((1, N, B_WIDTH), lambda b, j: (b, 0, 0)),
                  pl.BlockSpec((4, HEAD_DIM), lambda b, j: (0, 0)),
                  pl.BlockSpec((1, B_V_DIM), lambda b, j: (0, 0))],
        out_specs=pl.BlockSpec((1, tq, B_WIDTH), lambda b, j: (b, j, 0)),
        out_shape=jax.ShapeDtypeStruct((B, N, B_WIDTH), BF16),
        compiler_params=_cparams(("arbitrary", "arbitrary")),
    )(qb, kb, vb, lam_vecs.astype(F32), subln_g.reshape(1, B_V_DIM).astype(F32))


def _mix_out_kernel(n_in, *refs):
    xs = refs[:n_in]
    ws = refs[n_in:2 * n_in]
    h_ref, mod_ref, g_ref, b_ref, hn_ref, u_ref = refs[2 * n_in:]
    o = _dot(xs[0][0], ws[0][...])
    for x_ref, w_ref in zip(xs[1:], ws[1:]):
        o = o + _dot(x_ref[0], w_ref[...])
    z = DEEPNORM_ALPHA * h_ref[0] + mod_ref[0, 0, 2:3, :] * o
    hn = _layer_norm(z, g_ref[...], b_ref[...])
    hn_ref[0] = hn
    u_ref[0] = _pack_halves(hn * (1.0 + mod_ref[0, 0, 4:5, :]) + mod_ref[0, 0, 3:4, :])


def _mix_out(xs, ws, h, mod, ln_g, ln_b, n_ctx, row0):
    B, N, D = h.shape
    tm = ROW_TILE
    t0 = row0 // tm
    n_out = N - row0
    row_spec = lambda w: pl.BlockSpec((1, tm, w), lambda b, i: (b, i + t0, 0))
    out_spec = pl.BlockSpec((1, tm, D), lambda b, i: (b, i, 0))
    vec_spec = pl.BlockSpec((1, D), lambda b, i: (0, 0))
    return pl.pallas_call(
        functools.partial(_mix_out_kernel, len(xs)),
        grid=(B, n_out // tm),
        in_specs=([row_spec(x.shape[-1]) for x in xs]
                  + [pl.BlockSpec(w.shape, lambda b, i: (0, 0)) for w in ws]
                  + [row_spec(D),
                     pl.BlockSpec((1, 1, 6, D), lambda b, i: (b, jnp.minimum((i + t0) // (n_ctx // tm), 1), 0, 0)),
                     vec_spec, vec_spec]),
        out_specs=[out_spec, pl.BlockSpec((1, tm, D // 2), lambda b, i: (b, i, 0))],
        out_shape=[jax.ShapeDtypeStruct((B, n_out, D), F32), jax.ShapeDtypeStruct((B, n_out, D // 2), I32)],
        compiler_params=_cparams(("arbitrary", "arbitrary")),
    )(*xs, *[w.astype(BF16) for w in ws], h, mod, ln_g.reshape(1, D), ln_b.reshape(1, D))


def _router_kernel(u_ref, rt_ref, bias_ref, tri_ref, e_ref, gw_ref, rank_ref, cnt_ref, carry_ref):
    i = pl.program_id(0)

    @pl.when(i == 0)
    def _():
        carry_ref[...] = jnp.zeros_like(carry_ref)

    tm = u_ref.shape[0]
    per_group = N_EXPERTS // N_GROUPS
    neg = -jnp.inf
    u_lo, u_hi = _unpack_halves(u_ref[...])
    half = u_ref.shape[1]
    logits = (_dot_nt(rt_ref[:, :half], u_lo.astype(BF16))
              + _dot_nt(rt_ref[:, half:], u_hi.astype(BF16)))
    scores = jax.nn.sigmoid(logits)
    sel = scores + bias_ref[...]
    io_in = lax.broadcasted_iota(I32, (per_group, tm), 0)
    grp_rows = []
    for gi in range(N_GROUPS):
        sg = sel[gi * per_group:(gi + 1) * per_group]
        m1 = jnp.max(sg, axis=0, keepdims=True)
        i1 = jnp.min(jnp.where(sg == m1, io_in, per_group), axis=0, keepdims=True)
        m2 = jnp.max(jnp.where(io_in == i1, neg, sg), axis=0, keepdims=True)
        grp_rows.append(m1 + m2)
    grp = jnp.concatenate(grp_rows, axis=0)
    io_g = lax.broadcasted_iota(I32, grp.shape, 0)
    g_sel = jnp.zeros(grp.shape, F32)
    for _ in range(TOPK_GROUPS):
        m = jnp.max(grp, axis=0, keepdims=True)
        hit = io_g == jnp.min(jnp.where(grp == m, io_g, N_GROUPS), axis=0, keepdims=True)
        g_sel = jnp.where(hit, 1.0, g_sel)
        grp = jnp.where(hit, neg, grp)
    selm = jnp.concatenate(
        [jnp.where(g_sel[gi:gi + 1] > 0.5, sel[gi * per_group:(gi + 1) * per_group], NEG_INF)
         for gi in range(N_GROUPS)], axis=0)
    io_e = lax.broadcasted_iota(I32, selm.shape, 0)
    chosen_f = jnp.zeros(selm.shape, F32)
    idx, gws = [], []
    for _ in range(TOP_K):
        m = jnp.max(selm, axis=0, keepdims=True)
        ik = jnp.min(jnp.where(selm == m, io_e, N_EXPERTS), axis=0, keepdims=True)
        hit = io_e == ik
        idx.append(ik)
        gws.append(jnp.sum(jnp.where(hit, scores, 0.0), axis=0, keepdims=True))
        chosen_f = jnp.where(hit, 1.0, chosen_f)
        selm = jnp.where(hit, neg, selm)
    gw = jnp.concatenate(gws, axis=0)
    gw_ref[...] = gw / jnp.sum(gw, axis=0, keepdims=True) * ROUTED_SCALE
    e_ref[...] = jnp.concatenate(idx, axis=0)
    before = _dot(chosen_f.astype(BF16), tri_ref[...]) + carry_ref[...]
    ranks = [jnp.sum(jnp.where(io_e == ik, before, 0.0), axis=0, keepdims=True) for ik in idx]
    rank_ref[...] = jnp.concatenate(ranks, axis=0).astype(I32)
    carry_ref[...] = carry_ref[...] + jnp.sum(chosen_f, axis=1, keepdims=True)
    cnt_ref[...] = carry_ref[...].astype(I32)


def _router(u, router, bias):
    T = u.shape[0]
    D = router.shape[0]
    tm = ROW_TILE
    tri = jnp.asarray(np.triu(np.ones((tm, tm), np.float32), 1), BF16)
    tok_spec = pl.BlockSpec((TOP_K, tm), lambda i: (0, i))
    return pl.pallas_call(
        _router_kernel,
        grid=(T // tm,),
        in_specs=[pl.BlockSpec((tm, D // 2), lambda i: (i, 0)),
                  pl.BlockSpec((N_EXPERTS, D), lambda i: (0, 0)),
                  pl.BlockSpec((N_EXPERTS, 1), lambda i: (0, 0)),
                  pl.BlockSpec((tm, tm), lambda i: (0, 0))],
        out_specs=[tok_spec, tok_spec, tok_spec, pl.BlockSpec((N_EXPERTS, 1), lambda i: (0, 0))],
        out_shape=[jax.ShapeDtypeStruct((TOP_K, T), I32), jax.ShapeDtypeStruct((TOP_K, T), F32),
                   jax.ShapeDtypeStruct((TOP_K, T), I32), jax.ShapeDtypeStruct((N_EXPERTS, 1), I32)],
        scratch_shapes=[pltpu.VMEM((N_EXPERTS, 1), F32)],
        compiler_params=_cparams(("arbitrary",)),
    )(u, router.T.astype(BF16), bias.reshape(N_EXPERTS, 1).astype(F32), tri)


def _sc_mesh():
    return plsc.VectorSubcoreMesh(core_axis_name="c", subcore_axis_name="s")


def _sc_scatter_rows(x, dest, n_rows):
    T, W = x.shape
    K = dest.shape[0]
    win = SC_WINDOW

    @functools.partial(pl.kernel, out_type=jax.ShapeDtypeStruct((n_rows, W), x.dtype), mesh=_sc_mesh(),
                       scratch_types=[])
    def scatter(x_hbm, i_hbm, o_hbm):
        def body(x_vmem, i_vmem):
            for k in range(K):
                pltpu.sync_copy(x_vmem, o_hbm.at[i_vmem.at[k]])

        pltpu.emit_pipeline(
            body,
            grid=(T // win,),
            in_specs=[pl.BlockSpec((win, W), lambda j: (j, 0), pipeline_mode=pl.Buffered(1)),
                      pl.BlockSpec((K, win), lambda j: (0, j))],
            out_specs=[],
            core_axis_name=("c", "s"),
            dimension_semantics=(pltpu.PARALLEL,),
        )(x_hbm, i_hbm)

    return scatter(x, dest)


def _sc_gather_rows(y, dest):
    K, T = dest.shape
    W = y.shape[1]
    win = SC_WINDOW

    @functools.partial(pl.kernel, out_type=jax.ShapeDtypeStruct((K * T, W), y.dtype), mesh=_sc_mesh(),
                       scratch_types=[])
    def gather(y_hbm, i_hbm, o_hbm):
        def body(i_vmem, o_vmem):
            pltpu.sync_copy(y_hbm.at[i_vmem.at[0]], o_vmem)

        pltpu.emit_pipeline(
            body,
            grid=(K * T // win,),
            in_specs=[pl.BlockSpec((1, win), lambda j: (0, j))],
            out_specs=[pl.BlockSpec((win, W), lambda j: (j, 0), pipeline_mode=pl.Buffered(1))],
            core_axis_name=("c", "s"),
            dimension_semantics=(pltpu.PARALLEL,),
        )(i_hbm, o_hbm)

    return gather(y, dest.reshape(1, K * T)).reshape(K, T, W)


def _expert_kernel(first_ref, count_ref, x_hbm, wi_ref, wo_ref, y_hbm, wi_b, wo_b, xbuf, ybuf, sem_in, sem_out):
    e = pl.program_id(0)
    n_in, blk = xbuf.shape[0], xbuf.shape[1]
    n_out = ybuf.shape[0]
    g0, g1 = first_ref[e], first_ref[e + 1]
    g_end = first_ref[pl.num_programs(0)]

    def x_copy(g):
        return pltpu.make_async_copy(x_hbm.at[pl.ds(g * blk, blk)], xbuf.at[g % n_in], sem_in.at[g % n_in])

    def y_copy(g):
        return pltpu.make_async_copy(ybuf.at[g % n_out], y_hbm.at[pl.ds(g * blk, blk)], sem_out.at[g % n_out])

    for ahead in range(n_in - 1):
        @pl.when(jnp.logical_and(e == 0, g_end > ahead))
        def _():
            x_copy(ahead).start()

    @pl.when(g1 > g0)
    def _():
        wi_b[...] = wi_ref[0, 0].astype(BF16)
        wo_b[...] = wo_ref[0, 0].astype(BF16)

    ff = wo_b.shape[0]

    def block(g, carry):
        x_copy(g).wait()

        @pl.when(g + n_in - 1 < g_end)
        def _():
            x_copy(g + n_in - 1).start()

        @pl.when(g >= n_out)
        def _():
            y_copy(g - n_out).wait()

        row = lax.broadcasted_iota(I32, (blk, xbuf.shape[2]), 0) + (g - g0) * blk
        hcat = _dot_halves(jnp.where(row < count_ref[e], xbuf[g % n_in], 0), wi_b)
        act = (_silu(hcat[:, :ff]) * hcat[:, ff:]).astype(BF16)
        ybuf[g % n_out] = _pack_halves(_dot(act, wo_b[...]))
        y_copy(g).start()
        return carry

    lax.fori_loop(g0, g1, block, 0)

    @pl.when(e == pl.num_programs(0) - 1)
    def _():
        for back in range(n_out, 0, -1):
            @pl.when(g_end >= back)
            def _():
                y_copy(g_end - back).wait()


def _experts(xs, first_block, counts, w_in, w_out, layer):
    P, half = xs.shape
    n_exp, D, ff2 = w_in.shape[-3:]
    return pl.pallas_call(
        _expert_kernel,
        grid_spec=pltpu.PrefetchScalarGridSpec(
            num_scalar_prefetch=2,
            grid=(n_exp,),
            in_specs=[pl.BlockSpec(memory_space=pl.ANY),
                      pl.BlockSpec((1, 1, D, ff2), lambda e, fb, cnt: (layer, e, 0, 0)),
                      pl.BlockSpec((1, 1, ff2 // 2, D), lambda e, fb, cnt: (layer, e, 0, 0))],
            out_specs=pl.BlockSpec(memory_space=pl.ANY),
            scratch_shapes=[pltpu.VMEM((D, ff2), BF16), pltpu.VMEM((ff2 // 2, D), BF16),
                            pltpu.VMEM((EXPERT_IN_SLOTS, MOE_BLOCK, half), I32),
                            pltpu.VMEM((EXPERT_OUT_SLOTS, MOE_BLOCK, half), I32),
                            pltpu.SemaphoreType.DMA((EXPERT_IN_SLOTS,)),
                            pltpu.SemaphoreType.DMA((EXPERT_OUT_SLOTS,))]),
        out_shape=jax.ShapeDtypeStruct((P, half), I32),
        compiler_params=_cparams(("arbitrary",)),
    )(first_block, counts.astype(I32), xs, w_in, w_out)


def _combine_kernel(yg_ref, gw_ref, u_ref, wsi_ref, wso_ref, h_ref, mod_ref, g_ref, b_ref, o_ref):
    ff = wso_ref.shape[0]
    hcat = _dot_halves(u_ref[...], wsi_ref)
    shared = _dot((_silu(hcat[:, :ff]) * hcat[:, ff:]).astype(BF16), wso_ref[...])
    lo, hi = None, None
    for k in range(TOP_K):
        y_lo, y_hi = _unpack_halves(yg_ref[k])
        gk = gw_ref[:, k:k + 1]
        lo = y_lo * gk if lo is None else lo + y_lo * gk
        hi = y_hi * gk if hi is None else hi + y_hi * gk
    routed = jnp.concatenate([lo, hi], axis=1)
    z = DEEPNORM_ALPHA * h_ref[...] + mod_ref[0] * (routed + shared)
    o_ref[...] = _layer_norm(z, g_ref[...], b_ref[...])


def _combine(yg, gw_t, u, ws_in, ws_out, h, gate, gate_index, ln_g, ln_b):
    T, D = h.shape
    tm = ROW_TILE
    vec_spec = pl.BlockSpec((1, D), lambda i: (0, 0))
    row_spec = pl.BlockSpec((tm, D), lambda i: (i, 0))
    packed_spec = pl.BlockSpec((tm, D // 2), lambda i: (i, 0))
    return pl.pallas_call(
        _combine_kernel,
        grid=(T // tm,),
        in_specs=[pl.BlockSpec((TOP_K, tm, D // 2), lambda i: (0, i, 0)),
                  pl.BlockSpec((tm, TOP_K), lambda i: (i, 0)),
                  packed_spec,
                  pl.BlockSpec(ws_in.shape, lambda i: (0, 0)),
                  pl.BlockSpec(ws_out.shape, lambda i: (0, 0)),
                  row_spec,
                  pl.BlockSpec((1,) + gate.shape[1:], lambda i: (gate_index(i), 0, 0)),
                  vec_spec, vec_spec],
        out_specs=row_spec,
        out_shape=jax.ShapeDtypeStruct((T, D), F32),
        compiler_params=_cparams(("arbitrary",)),
    )(yg, gw_t, u, ws_in.astype(BF16), ws_out.astype(BF16), h, gate,
      ln_g.reshape(1, D), ln_b.reshape(1, D))


def _slots_kernel(e_ref, rank_ref, start_ref, dest_ref):
    io_e = lax.broadcasted_iota(I32, (N_EXPERTS, e_ref.shape[1]), 0)
    rows = [jnp.sum(jnp.where(io_e == e_ref[k:k + 1, :], start_ref[...], 0), axis=0, keepdims=True)
            for k in range(TOP_K)]
    dest_ref[...] = jnp.concatenate(rows, axis=0) + rank_ref[...]


def _slots(eidx, rank, pstart):
    T = eidx.shape[1]
    tm = ROW_TILE
    tok_spec = pl.BlockSpec((TOP_K, tm), lambda i: (0, i))
    return pl.pallas_call(
        _slots_kernel,
        grid=(T // tm,),
        in_specs=[tok_spec, tok_spec, pl.BlockSpec((N_EXPERTS, 1), lambda i: (0, 0))],
        out_specs=tok_spec,
        out_shape=jax.ShapeDtypeStruct((TOP_K, T), I32),
        compiler_params=_cparams(("arbitrary",)),
    )(eidx, rank, pstart.reshape(N_EXPERTS, 1))


def _moe_layer(u, h, gate, gate_index, router, bias, w_in, w_out, ws_in, ws_out, ln_g, ln_b, layer):
    T = u.shape[0]
    eidx, gw, rank, counts = _router(u, router, bias)
    counts = counts[:, 0]
    padded = (counts + MOE_BLOCK - 1) // MOE_BLOCK * MOE_BLOCK
    pend = jnp.cumsum(padded)
    pstart = (pend - padded).astype(I32)
    dest = _slots(eidx, rank, pstart)
    n_blocks = -(-(T * TOP_K + N_EXPERTS * (MOE_BLOCK - 1)) // MOE_BLOCK)
    first_block = jnp.concatenate([jnp.zeros((1,), I32), (pend // MOE_BLOCK).astype(I32)])
    xs = _sc_scatter_rows(u, dest, n_blocks * MOE_BLOCK)
    y = _experts(xs, first_block, counts, w_in, w_out, layer)
    return _combine(_sc_gather_rows(y, dest), gw.T, u, ws_in, ws_out, h, gate, gate_index, ln_g, ln_b)


def _seg_ones(width=LANES):
    idx = np.arange(width) // HEAD_DIM
    return jnp.asarray((idx[:, None] == idx[None, :]).astype(np.float32), BF16)


def _head_sum(x, ones_ref):
    outs = []
    for j in range(x.shape[1] // LANES):
        xc = x[:, j * LANES:(j + 1) * LANES]
        hi = xc.astype(BF16)
        lo = (xc - hi.astype(F32)).astype(BF16)
        outs.append(_dot(hi, ones_ref[...]) + _dot(lo, ones_ref[...]))
    return jnp.concatenate(outs, axis=1)


def _rwkv_proj_kernel(seg_tiles, h_ref, hp_ref, hn_ref, mod_ref, mu_ref, wrkv_ref, g1_ref, g2_ref, d1_ref, d2_ref,
                      d0_ref, i1_ref, i2_ref, i0_ref, kk_ref, ka_ref, rk_ref, ones_ref,
                      r_ref, v_ref, a_ref, g_ref, bonus_ref, w_ref, k_ref, b_ref):
    i = pl.program_id(0)
    nb = hp_ref.shape[0]
    shift, scale = mod_ref[0, 0], mod_ref[0, 1]
    u = h_ref[...] * (1.0 + scale) + shift
    starts = jnp.logical_or(i == 0, i == seg_tiles)
    ends = jnp.logical_or(i == seg_tiles - 1, i == pl.num_programs(0) - 1)
    u_before = (hp_ref[...] * (1.0 + scale[:nb]) + shift[:nb]) * jnp.where(starts, 0.0, 1.0)
    u_after = (hn_ref[...] * (1.0 + scale[:nb]) + shift[:nb]) * jnp.where(ends, 0.0, 1.0)
    dx = 0.5 * (jnp.concatenate([u_before, u[:-nb]], axis=0) + jnp.concatenate([u[nb:], u_after], axis=0)) - u
    mix = lambda m: (u + dx * mu_ref[m:m + 1, :])
    xr, xw, xk, xv, xa, xg = [mix(m) for m in range(6)]
    r = _dot(xr.astype(BF16), wrkv_ref[0])
    k = _dot(xk.astype(BF16), wrkv_ref[1])
    v = _dot(xv.astype(BF16), wrkv_ref[2])
    g = _dot(jax.nn.sigmoid(_dot(xg.astype(BF16), g1_ref[...])).astype(BF16), g2_ref[...])
    kk = k * kk_ref[...]
    kk = kk * lax.rsqrt(jnp.maximum(_head_sum(kk * kk, ones_ref), 1e-24))
    r_ref[...] = r
    v_ref[...] = v
    a_ref[...] = -kk
    g_ref[...] = g
    k_sum = None
    xw_b = xw.astype(BF16)
    xa_b = xa.astype(BF16)
    for d in range(2):
        lw = d0_ref[d:d + 1, :] + _dot(jnp.tanh(_dot(xw_b, d1_ref[d])).astype(BF16), d2_ref[d])
        softplus = jnp.maximum(-lw, 0.0) + jnp.log(1.0 + jnp.exp(-jnp.abs(lw)))
        logw = -softplus - 0.5
        w_ref[d] = jnp.exp(-jnp.exp(logw))
        eta = jax.nn.sigmoid(i0_ref[d:d + 1, :] + _dot(_dot(xa_b, i1_ref[d]).astype(BF16), i2_ref[d]))
        k_d = k * (1.0 + (eta - 1.0) * ka_ref[...])
        k_ref[d] = k_d
        b_ref[d] = kk * eta
        k_sum = k_d if k_sum is None else k_sum + k_d
    bonus_ref[...] = _head_sum(r * k_sum * rk_ref[...], ones_ref) * v


def _rwkv_proj(h, mod_rows, batch, n_ctx, p):
    T, D = h.shape
    tm = PROJ_TILE
    per_tile = tm // batch
    seg_tiles = n_ctx // per_tile
    n_steps = T // batch
    row = pl.BlockSpec((tm, D), lambda i: (i, 0))
    before = pl.BlockSpec((batch, D), lambda i: (jnp.maximum(i * per_tile - 1, 0), 0))
    after = pl.BlockSpec((batch, D), lambda i: (jnp.minimum((i + 1) * per_tile, n_steps - 1), 0))
    mod_spec = pl.BlockSpec((1, 2, tm, D), lambda i: (jnp.minimum(i // seg_tiles, 1), 0, 0, 0))
    row2 = pl.BlockSpec((2, tm, D), lambda i: (0, i, 0))
    full = lambda a: pl.BlockSpec(a.shape, lambda i: (0,) * a.ndim)
    bf = lambda a: a.astype(BF16)
    consts = [p['mu'], bf(p['w_rkv']), bf(p['gate1']), bf(p['gate2']), bf(p['dec1']), bf(p['dec2']), p['dec0'],
              bf(p['icl1']), bf(p['icl2']), p['icl0'], p['k_k'].reshape(1, D), p['k_a'].reshape(1, D),
              p['r_k'].reshape(1, D), _seg_ones()]
    one = jax.ShapeDtypeStruct((T, D), F32)
    two = jax.ShapeDtypeStruct((2, T, D), F32)
    return pl.pallas_call(
        functools.partial(_rwkv_proj_kernel, seg_tiles),
        grid=(T // tm,),
        in_specs=[row, before, after, mod_spec] + [full(a) for a in consts],
        out_specs=[row, row, row, row, row, row2, row2, row2],
        out_shape=[one, one, one, one, one, two, two, two],
        compiler_params=_cparams(("arbitrary",)),
    )(h, h, h, mod_rows, *consts)


def _scan_kernel(r_ref, w_ref, k_ref, v_ref, a_ref, b_ref, ones_ref, hsel_ref,
                 y_ref, s_ref, vt_ref):
    d = pl.program_id(0)
    c = pl.program_id(2)
    tc, nb = r_ref.shape[0], r_ref.shape[1]
    tw = SCAN_TILE
    n_wide = r_ref.shape[2] // tw
    heads = tw // HEAD_DIM
    assert heads * tc == tw

    @pl.when(c == 0)
    def _():
        s_ref[...] = jnp.zeros_like(s_ref)

    for bb in range(nb):
        for q in range(n_wide):
            vt = v_ref[:, bb, q * tw:(q + 1) * tw].T
            vt_ref[bb * n_wide + q] = jnp.concatenate(
                [vt[h * HEAD_DIM:(h + 1) * HEAD_DIM] for h in range(heads)], axis=1)

    head_base = (lax.broadcasted_iota(I32, (HEAD_DIM, LANES), 1) // HEAD_DIM) * tc
    tiles = [(bb, q) for bb in range(nb) for q in range(n_wide)]
    groups = [tiles[i:i + SCAN_GROUP] for i in range(0, len(tiles), SCAN_GROUP)]

    def stacked(grp, get, dtype=F32):
        def wide(bb, q):
            return jnp.concatenate(
                [jnp.broadcast_to(get(bb, slice(q * tw + hf * LANES, q * tw + (hf + 1) * LANES)).astype(dtype),
                                  (HEAD_DIM, LANES)) for hf in range(tw // LANES)], axis=1)
        return jnp.concatenate([wide(bb, q) for bb, q in grp], axis=0)

    def load_state(grp):
        return jnp.concatenate([s_ref[bb * n_wide + q] for bb, q in grp], axis=0)

    def emit_y(grp, st_b, t_y):
        r_rows = stacked(grp, lambda bb, cols: r_ref[t_y, bb:bb + 1, cols], BF16)
        yh = _dot_nt(hsel_ref[...], st_b * r_rows)
        first = tiles.index(grp[0])
        y_ref[0, t_y, :, first * HEAD_DIM:(first + len(grp)) * HEAD_DIM] = yh[:heads]

    def step(s_i, carry):
        t = jnp.where(d == 0, s_i, tc - 1 - s_i)
        t_prev = jnp.where(s_i == 0, t, jnp.where(d == 0, t - 1, t + 1))
        pick = head_base + t
        for grp in groups:
            one = lambda ref: stacked(grp, lambda bb, cols: ref[t, bb:bb + 1, cols])
            two = lambda ref: stacked(grp, lambda bb, cols: ref[0, t, bb:bb + 1, cols])
            st = load_state(grp)
            st_b = st.astype(BF16)
            a_rows = stacked(grp, lambda bb, cols: a_ref[t, bb:bb + 1, cols], BF16)
            sa = _dot(st_b * a_rows, ones_ref[...])
            emit_y(grp, st_b, t_prev)
            vcol = jnp.concatenate(
                [jnp.concatenate([jnp.take_along_axis(vt_ref[bb * n_wide + q, :, hf * LANES:(hf + 1) * LANES],
                                                      pick, axis=1) for hf in range(tw // LANES)], axis=1)
                 for bb, q in grp], axis=0)
            st = st * two(w_ref) + sa * two(b_ref) + vcol * two(k_ref)
            for j, (bb, q) in enumerate(grp):
                s_ref[bb * n_wide + q] = st[j * HEAD_DIM:(j + 1) * HEAD_DIM]
        return carry

    lax.fori_loop(0, tc, step, 0, unroll=SCAN_UNROLL)
    t_last = jnp.where(d == 0, tc - 1, 0)
    for grp in groups:
        emit_y(grp, load_state(grp).astype(BF16), t_last)


def _wkv_scan(r, w, k, v, a, b, n_ctx):
    N, B, D = r.shape
    tc = SCAN_CHUNK
    wc = SCAN_COLS
    n_wide = wc // SCAN_TILE
    nc = N // tc
    ncc = n_ctx // tc

    def chunk(d, c):
        rev = jnp.where(c < ncc, ncc - 1 - c, nc - 1 - (c - ncc))
        return jnp.where(d == 0, c, rev)

    one = pl.BlockSpec((tc, B, wc), lambda d, g, c: (chunk(d, c), 0, g))
    two = pl.BlockSpec((1, tc, B, wc), lambda d, g, c: (d, chunk(d, c), 0, g))
    seg = np.arange(SCAN_TILE) // HEAD_DIM
    hsel = np.zeros((8, SCAN_TILE), np.float32)
    for hh in range(SCAN_TILE // HEAD_DIM):
        hsel[hh, seg == hh] = 1.0
    const = lambda a_: pl.BlockSpec(a_.shape, lambda d, g, c: (0, 0))
    consts = [_seg_ones(SCAN_TILE), jnp.asarray(hsel, BF16)]
    heads = SCAN_TILE // HEAD_DIM
    ncg = D // wc
    y = pl.pallas_call(
        _scan_kernel,
        grid=(2, ncg, nc),
        in_specs=[one, two, two, one, one, two] + [const(a_) for a_ in consts],
        out_specs=pl.BlockSpec((1, tc, heads, B * n_wide * HEAD_DIM), lambda d, g, c: (d, chunk(d, c), 0, g)),
        out_shape=jax.ShapeDtypeStruct((2, N, heads, ncg * B * n_wide * HEAD_DIM), F32),
        scratch_shapes=[pltpu.VMEM((B * n_wide, HEAD_DIM, SCAN_TILE), F32),
                        pltpu.VMEM((B * n_wide, HEAD_DIM, SCAN_TILE), F32)],
        compiler_params=_cparams(("arbitrary", "arbitrary", "arbitrary")),
    )(r, w, k, v, a, b, *consts)
    y = y.reshape(2, N, heads, ncg, B, n_wide * HEAD_DIM)
    return jnp.transpose(y, (0, 1, 4, 2, 3, 5)).reshape(2, N, B, D)


def _scan_head_order(d):
    heads = SCAN_TILE // HEAD_DIM
    n_wide = SCAN_COLS // SCAN_TILE
    ncg = d // SCAN_COLS
    order = []
    for g in range(ncg):
        for q in range(n_wide):
            for h in range(heads):
                order.append((h * ncg + g) * n_wide + q)
    return tuple(order)


def _rwkv_out_kernel(head_order, y0_ref, y1_ref, bonus_ref, g_ref, lnx_ref, ones_ref, w_ref, h_ref, mod_ref,
                     lg_ref, lb_ref, hn_ref, u_ref):
    y_in = y0_ref[0] + y1_ref[0]
    y = jnp.concatenate([y_in[:, p * HEAD_DIM:(p + 1) * HEAD_DIM] for p in head_order], axis=1)
    ym = _head_sum(y, ones_ref) * (1.0 / HEAD_DIM)
    yc = y - ym
    yv = _head_sum(yc * yc, ones_ref) * (1.0 / HEAD_DIM)
    yn = yc * lax.rsqrt(yv + LNX_EPS) * lnx_ref[0:1, :] + lnx_ref[1:2, :]
    x = ((yn + bonus_ref[...]) * g_ref[...]).astype(BF16)
    o = _dot(x, w_ref[...])
    z = DEEPNORM_ALPHA * h_ref[...] + mod_ref[0] * o
    hn = _layer_norm(z, lg_ref[...], lb_ref[...])
    hn_ref[...] = hn
    u_ref[...] = _pack_halves(hn * (1.0 + mod_ref[2]) + mod_ref[1])


def _rwkv_out(y, bonus, g, lnx, w_out, h, mod_rows, ln_g, ln_b, row0):
    T, D = h.shape
    tm = ROW_TILE
    t0 = row0 // tm
    off = pl.BlockSpec((tm, D), lambda i: (i + t0, 0))
    out = pl.BlockSpec((tm, D), lambda i: (i, 0))
    full = lambda a: pl.BlockSpec(a.shape, lambda i: (0,) * a.ndim)
    vec = pl.BlockSpec((1, D), lambda i: (0, 0))
    ones = _seg_ones()
    w_b = w_out.astype(BF16)
    return pl.pallas_call(
        functools.partial(_rwkv_out_kernel, _scan_head_order(D)),
        grid=((T - row0) // tm,),
        in_specs=[pl.BlockSpec((1, tm, D), lambda i: (0, i + t0, 0)),
                  pl.BlockSpec((1, tm, D), lambda i: (1, i + t0, 0)),
                  off, off, full(lnx), full(ones), full(w_b), off, full(mod_rows), vec, vec],
        out_specs=[out, pl.BlockSpec((tm, D // 2), lambda i: (i, 0))],
        out_shape=[jax.ShapeDtypeStruct((T - row0, D), F32), jax.ShapeDtypeStruct((T - row0, D // 2), I32)],
        compiler_params=_cparams(("arbitrary",)),
    )(y, y, bonus, g, lnx, ones, w_b, h, mod_rows, ln_g.reshape(1, D), ln_b.reshape(1, D))


def kernel(x, c, ctx, c_ctx, ada_w, ada_b, post_ln_g, post_ln_b, att_w_in, att_w_out, att_sink, diff_lambda_vecs, diff_subln_g, rk_mu, rk_w_rkv, rk_w_out, rk_decay0, rk_decay1, rk_decay2, rk_iclr0, rk_iclr1, rk_iclr2, rk_gate1, rk_gate2, rk_k_k, rk_k_a, rk_r_k, rk_lnx, moe_router, moe_bias, moe_w_in, moe_w_out, moe_ws_in, moe_ws_out):
    B, S, D = x.shape
    L = ctx.shape[1]
    N = L + S
    tm = ROW_TILE
    assert L % tm == 0 and S % tm == 0 and L % SCAN_CHUNK == 0 and S % SCAN_CHUNK == 0
    assert tm % B == 0 and PROJ_TILE % B == 0 and L % (PROJ_TILE // B) == 0 and D % SCAN_COLS == 0

    rows = -(-(B + 1) // 8) * 8
    cvec = jnp.concatenate([c, c_ctx[None, :], jnp.zeros((rows - B - 1, D), F32)], axis=0)
    mods = [_mod_table(_ada_mod(cvec, ada_w[i], ada_b[i]), B, D) for i in range(DEPTH)]

    h0 = jnp.concatenate([ctx, x], axis=1)
    lam_init = 0.8 - 0.6 * math.exp(-0.3 * 0)
    qa, ka, va, qb, kb, vb = _attn_inproj(h0, mods[0], att_w_in[0], L)
    oa = _win_attn(qa, ka, va, att_sink[0], L)
    ob = _diff_attn(qb, kb, vb, diff_lambda_vecs[0], diff_subln_g[0], lam_init, L)
    h1, u1 = _mix_out([oa, ob], [att_w_out[0][:A_WIDTH], att_w_out[0][A_WIDTH:]], h0, mods[0],
                      post_ln_g[0, 0], post_ln_b[0, 0], L, 0)
    tiles_b, tiles_c = N // tm, L // tm
    gate0 = mods[0][:, :, 5].reshape(B * 2, 1, D)
    gate0_index = lambda i: (i // tiles_b) * 2 + jnp.minimum((i % tiles_b) // tiles_c, 1)
    h2 = _moe_layer(u1.reshape(B * N, D // 2), h1.reshape(B * N, D), gate0, gate0_index, moe_router[0], moe_bias[0],
                    moe_w_in, moe_w_out, moe_ws_in[0], moe_ws_out[0],
                    post_ln_g[0, 1], post_ln_b[0, 1], 0).reshape(B, N, D)

    m_ctx, m_lat = mods[1][:, 0], mods[1][:, 1]
    h2_t = jnp.swapaxes(h2, 0, 1).reshape(N * B, D)
    rows_of = lambda m, j, n: jnp.tile(m[:, j], (n // B, 1))
    proj_mod = jnp.stack([jnp.stack([rows_of(m, 0, PROJ_TILE), rows_of(m, 1, PROJ_TILE)]) for m in (m_ctx, m_lat)])
    params = dict(mu=rk_mu[0], w_rkv=rk_w_rkv[0], gate1=rk_gate1[0], gate2=rk_gate2[0],
                  dec0=rk_decay0[0], dec1=rk_decay1[0], dec2=rk_decay2[0],
                  icl0=rk_iclr0[0], icl1=rk_iclr1[0], icl2=rk_iclr2[0],
                  k_k=rk_k_k[0], k_a=rk_k_a[0], r_k=rk_r_k[0])
    r, v, a, g, bonus, w2, k2, b2 = _rwkv_proj(h2_t, proj_mod, B, L, params)
    tmaj = lambda t: t.reshape(t.shape[:-2] + (N, B, D))
    y = _wkv_scan(tmaj(r), tmaj(w2), tmaj(k2), tmaj(v), tmaj(a), tmaj(b2), L)
    lat_rows = lambda j: rows_of(m_lat, j, tm)
    h3, u3 = _rwkv_out(y.reshape(2, N * B, D), bonus, g, rk_lnx[0], rk_w_out[0], h2_t,
                       jnp.stack([lat_rows(2), lat_rows(3), lat_rows(4)]),
                       post_ln_g[1, 0], post_ln_b[1, 0], L * B)
    out = _moe_layer(u3, h3, lat_rows(5)[None], lambda i: 0, moe_router[1], moe_bias[1],
                     moe_w_in, moe_w_out, moe_ws_in[1], moe_ws_out[1],
                     post_ln_g[1, 1], post_ln_b[1, 1], 1)
    return jnp.swapaxes(out.reshape(S, B, D), 0, 1)
```

```python
import functools
import math

import numpy as np
import jax
import jax.numpy as jnp
from jax import lax
from jax.experimental import pallas as pl
from jax.experimental.pallas import tpu as pltpu
from jax.experimental.pallas import tpu_sc as plsc

F32 = jnp.float32
BF16 = jnp.bfloat16
I32 = jnp.int32

HEAD_DIM = 64
GRID_W = 64
ROPE_AXIS_DIM = HEAD_DIM // 2
ROPE_THETA = 10000.0
Q_BLOCK = 128
A_Q_HEADS = 8
A_KV_HEADS = 2
A_GROUP = A_Q_HEADS // A_KV_HEADS
A_WIDTH = A_Q_HEADS * HEAD_DIM
A_KV_WIDTH = A_KV_HEADS * HEAD_DIM
B_HEADS = 4
B_V_DIM = 2 * HEAD_DIM
B_WIDTH = B_HEADS * B_V_DIM
LNX_EPS = 64e-5
N_EXPERTS = 256
TOP_K = 8
N_GROUPS = 8
TOPK_GROUPS = 4
ROUTED_SCALE = 2.5
MOE_BLOCK = 256
EXPERT_IN_SLOTS = 4
EXPERT_OUT_SLOTS = 2
LN_EPS = 1e-5
SUBLN_EPS = 1e-5
NEG_INF = -1e30
DEPTH = 2
DEEPNORM_ALPHA = (2 * DEPTH) ** 0.25

LANES = 128
ADA_COLS = 768
ROW_TILE = 256
PROJ_TILE = 256
SC_WINDOW = 128
SCAN_CHUNK = 64
SCAN_COLS = 1024
SCAN_TILE = 256
SCAN_GROUP = 8
SCAN_UNROLL = 8
VMEM_LIMIT = 56 * 1024 * 1024


def _cparams(sem):
    return pltpu.CompilerParams(dimension_semantics=sem, vmem_limit_bytes=VMEM_LIMIT)


def _silu(x):
    return x * jax.nn.sigmoid(x)


def _layer_norm(z, g, b):
    mu = jnp.mean(z, -1, keepdims=True)
    zc = z - mu
    var = jnp.mean(zc * zc, -1, keepdims=True)
    return zc * lax.rsqrt(var + LN_EPS) * g + b


def _dot(a, b):
    return jnp.dot(a, b, preferred_element_type=F32)


def _dot_nt(a, b):
    return lax.dot_general(a, b, (((1,), (1,)), ((), ())), preferred_element_type=F32)


def _pack_halves(x):
    half = x.shape[1] // 2
    bits = lambda v: lax.bitcast_convert_type(v.astype(BF16).astype(F32), I32)
    return lax.shift_right_logical(bits(x[:, :half]), 16) | bits(x[:, half:])


def _unpack_halves(p):
    lo = lax.bitcast_convert_type(lax.shift_left(p, 16), F32)
    hi = lax.bitcast_convert_type(p & jnp.int32(-65536), F32)
    return lo, hi


def _dot_halves(p, w_ref_or_array):
    lo, hi = _unpack_halves(p)
    half = p.shape[1]
    return _dot(lo.astype(BF16), w_ref_or_array[:half]) + _dot(hi.astype(BF16), w_ref_or_array[half:])


def _ada_kernel(c_ref, w_ref, b_ref, o_ref):
    c = c_ref[...]
    o_ref[...] = _dot(_silu(c).astype(BF16), w_ref[...].astype(BF16)) + b_ref[...]


def _ada_mod(cvec, w, bias):
    R, D = cvec.shape
    n_out = w.shape[1]
    tn = ADA_COLS
    return pl.pallas_call(
        _ada_kernel,
        grid=(n_out // tn,),
        in_specs=[pl.BlockSpec((R, D), lambda j: (0, 0)),
                  pl.BlockSpec((D, tn), lambda j: (0, j)),
                  pl.BlockSpec((1, tn), lambda j: (0, j))],
        out_specs=pl.BlockSpec((R, tn), lambda j: (0, j)),
        out_shape=jax.ShapeDtypeStruct((R, n_out), F32),
        compiler_params=_cparams(("arbitrary",)),
    )(cvec, w, bias.reshape(1, n_out))


def _mod_table(m, batch, d):
    m_lat = m[:batch].reshape(batch, 6, d)
    m_ctx = jnp.broadcast_to(m[batch].reshape(1, 6, d), (batch, 6, d))
    return jnp.stack([m_ctx, m_lat], axis=1)


def _mod_spec(d, ctx_tiles):
    return pl.BlockSpec((1, 1, 6, d), lambda b, i: (b, jnp.minimum(i // ctx_tiles, 1), 0, 0))


def _rope_tables(n_ctx, n_lat):
    rows = n_lat // GRID_W
    row = np.repeat(np.arange(rows), GRID_W).astype(np.float32)
    col = np.tile(np.arange(GRID_W), rows).astype(np.float32)
    inv = (ROPE_THETA ** (-np.arange(0, ROPE_AXIS_DIM, 2, dtype=np.float32) / ROPE_AXIS_DIM)).astype(np.float32)
    ar = row[:, None] * inv
    ac = col[:, None] * inv
    ang = np.concatenate([ar, ar, ac, ac], -1)
    cos = np.cos(ang).astype(np.float32)
    sin = np.sin(ang).astype(np.float32)
    lower = (np.arange(HEAD_DIM) % ROPE_AXIS_DIM) < (ROPE_AXIS_DIM // 2)
    sin_up = np.where(lower[None, :], -sin, 0.0)
    sin_dn = np.where(lower[None, :], 0.0, sin)

    def full(t, ctx_fill):
        t = np.concatenate([np.full((n_ctx, HEAD_DIM), ctx_fill, np.float32), t], 0)
        return jnp.asarray(np.tile(t, (1, LANES // HEAD_DIM)))

    return full(cos, 1.0), full(sin_up, 0.0), full(sin_dn, 0.0)


def _inproj_kernel(h_ref, mod_ref, w_ref, cos_ref, su_ref, sd_ref,
                   qa_ref, ka_ref, va_ref, qb_ref, kb_ref, vb_ref):
    h = h_ref[0]
    shift = mod_ref[0, 0, 0:1, :]
    scale = mod_ref[0, 0, 1:2, :]
    u = (h * (1.0 + scale) + shift).astype(BF16)
    y = _dot(u, w_ref[...])
    cos, s_up, s_dn = cos_ref[...], su_ref[...], sd_ref[...]
    q_scale = HEAD_DIM ** -0.5

    def rope(xc):
        half = ROPE_AXIS_DIM // 2
        return xc * cos + pltpu.roll(xc, LANES - half, 1) * s_up + pltpu.roll(xc, half, 1) * s_dn

    def emit(out_ref, col0, width, roped, mul):
        for j in range(width // LANES):
            xc = y[:, col0 + j * LANES: col0 + (j + 1) * LANES]
            if roped:
                xc = rope(xc)
            if mul != 1.0:
                xc = xc * mul
            out_ref[0, :, j * LANES:(j + 1) * LANES] = xc.astype(out_ref.dtype)

    c = 0
    emit(qa_ref, c, A_WIDTH, True, q_scale); c += A_WIDTH
    emit(ka_ref, c, A_KV_WIDTH, True, 1.0); c += A_KV_WIDTH
    emit(va_ref, c, A_KV_WIDTH, False, 1.0); c += A_KV_WIDTH
    emit(qb_ref, c, B_WIDTH, True, q_scale); c += B_WIDTH
    emit(kb_ref, c, B_WIDTH, True, 1.0); c += B_WIDTH
    emit(vb_ref, c, B_WIDTH, False, 1.0)


def _attn_inproj(h, mod, w_in, n_ctx):
    B, N, D = h.shape
    tm = ROW_TILE
    cos, s_up, s_dn = _rope_tables(n_ctx, N - n_ctx)
    widths = (A_WIDTH, A_KV_WIDTH, A_KV_WIDTH, B_WIDTH, B_WIDTH, B_WIDTH)
    tab_spec = pl.BlockSpec((tm, LANES), lambda b, i: (i, 0))
    return pl.pallas_call(
        _inproj_kernel,
        grid=(B, N // tm),
        in_specs=[pl.BlockSpec((1, tm, D), lambda b, i: (b, i, 0)),
                  _mod_spec(D, n_ctx // tm),
                  pl.BlockSpec(w_in.shape, lambda b, i: (0, 0)),
                  tab_spec, tab_spec, tab_spec],
        out_specs=[pl.BlockSpec((1, tm, w), lambda b, i: (b, i, 0)) for w in widths],
        out_shape=[jax.ShapeDtypeStruct((B, N, w), BF16) for w in widths],
        compiler_params=_cparams(("arbitrary", "arbitrary")),
    )(h, mod, w_in.astype(BF16), cos, s_up, s_dn)


def _win_attn_kernel(n_ctx_blocks, n_blocks, q_ref, kc_ref, vc_ref, kl_ref, km_ref, kr_ref,
                     vl_ref, vm_ref, vr_ref, sink_ref, o_ref):
    j = pl.program_id(1)
    is_lat = j >= n_ctx_blocks
    qb = Q_BLOCK
    n_c = kc_ref.shape[1]
    rows = A_GROUP * qb
    n_keys = n_c + 3 * qb
    far = 1 << 20
    r_idx = lax.broadcasted_iota(I32, (rows, n_keys), 0) % qb
    cw = lax.broadcasted_iota(I32, (rows, n_keys), 1) - n_c
    off_l = jnp.where(jnp.logical_and(is_lat, j > n_ctx_blocks), 0, far)
    end_m = jnp.where(is_lat, 2 * qb, qb)
    off_r = jnp.where(jnp.logical_and(is_lat, j < n_blocks - 1), 0, far)
    valid = ((cw < 0)
             | ((cw >= 0) & (cw < qb) & (cw >= r_idx + off_l))
             | ((cw >= qb) & (cw < end_m))
             | ((cw >= 2 * qb) & (cw - 2 * qb + off_r <= r_idx)))
    outs = []
    for kv in range(A_KV_HEADS):
        cols = slice(kv * HEAD_DIM, (kv + 1) * HEAD_DIM)
        k_all = jnp.concatenate([kc_ref[0, :, cols], kl_ref[0, :, cols], km_ref[0, :, cols],
                                 kr_ref[0, :, cols]], axis=0)
        v_all = jnp.concatenate([vc_ref[0, :, cols], vl_ref[0, :, cols], vm_ref[0, :, cols],
                                 vr_ref[0, :, cols]], axis=0)
        q0 = kv * A_GROUP
        q = jnp.concatenate([q_ref[0, :, (q0 + g) * HEAD_DIM:(q0 + g + 1) * HEAD_DIM]
                             for g in range(A_GROUP)], axis=0)
        sink = jnp.concatenate([jnp.broadcast_to(sink_ref[q0 + g:q0 + g + 1, 0:1], (qb, 1))
                                for g in range(A_GROUP)], axis=0)
        s = jnp.where(valid, _dot_nt(q, k_all), NEG_INF)
        m = jnp.maximum(jnp.max(s, -1, keepdims=True), sink)
        e = jnp.exp(s - m)
        denom = jnp.sum(e, -1, keepdims=True) + jnp.exp(sink - m)
        o = _dot(e.astype(BF16), v_all) * (1.0 / denom)
        outs += [o[g * qb:(g + 1) * qb] for g in range(A_GROUP)]
    for j2 in range(A_Q_HEADS // 2):
        pair = jnp.concatenate([outs[2 * j2], outs[2 * j2 + 1]], axis=1)
        o_ref[0, :, j2 * LANES:(j2 + 1) * LANES] = pair.astype(o_ref.dtype)


def _win_attn(qa, ka, va, sink, n_ctx):
    B, N, _ = qa.shape
    qb = Q_BLOCK
    nb = N // qb
    ncb = n_ctx // qb
    sink_pad = jnp.broadcast_to(sink.reshape(A_Q_HEADS, 1).astype(F32), (A_Q_HEADS, LANES))

    def left(b, j):
        return (b, jnp.clip(j - 1, ncb, nb - 1), 0)

    def mid(b, j):
        return (b, jnp.clip(j, ncb, nb - 1), 0)

    def right(b, j):
        return (b, jnp.clip(j + 1, ncb, nb - 1), 0)

    kv_blk = lambda im: pl.BlockSpec((1, qb, A_KV_WIDTH), im)
    ctx_blk = pl.BlockSpec((1, n_ctx, A_KV_WIDTH), lambda b, j: (b, 0, 0))
    return pl.pallas_call(
        functools.partial(_win_attn_kernel, ncb, nb),
        grid=(B, nb),
        in_specs=[pl.BlockSpec((1, qb, A_WIDTH), lambda b, j: (b, j, 0)),
                  ctx_blk, ctx_blk,
                  kv_blk(left), kv_blk(mid), kv_blk(right),
                  kv_blk(left), kv_blk(mid), kv_blk(right),
                  pl.BlockSpec((A_Q_HEADS, LANES), lambda b, j: (0, 0))],
        out_specs=pl.BlockSpec((1, qb, A_WIDTH), lambda b, j: (b, j, 0)),
        out_shape=jax.ShapeDtypeStruct((B, N, A_WIDTH), BF16),
        compiler_params=_cparams(("arbitrary", "arbitrary")),
    )(qa, ka, va, ka, ka, ka, va, va, va, sink_pad)


def _diff_attn_kernel(n_ctx, lam_init, q_ref, k_ref, v_ref, lv_ref, g_ref, o_ref):
    j = pl.program_id(1)
    lv = lv_ref[...]
    lam = (jnp.exp(jnp.sum(lv[0:1] * lv[1:2], -1, keepdims=True))
           - jnp.exp(jnp.sum(lv[2:3] * lv[3:4], -1, keepdims=True)) + lam_init)
    gain = g_ref[...] * (1.0 - lam_init)

    def run(n_keys):
        for hd in range(B_HEADS):
            v = v_ref[0, :n_keys, hd * B_V_DIM:(hd + 1) * B_V_DIM]
            maps = []
            for mm in range(2):
                c0 = (hd * 2 + mm) * HEAD_DIM
                q = q_ref[0, :, c0:c0 + HEAD_DIM]
                k = k_ref[0, :n_keys, c0:c0 + HEAD_DIM]
                s = _dot_nt(q, k)
                e = jnp.exp(s - jnp.max(s, -1, keepdims=True))
                maps.append(_dot(e.astype(BF16), v) * (1.0 / jnp.sum(e, -1, keepdims=True)))
            o = maps[0] - lam * maps[1]
            o = o * lax.rsqrt(jnp.mean(o * o, -1, keepdims=True) + SUBLN_EPS) * gain
            o_ref[0, :, hd * B_V_DIM:(hd + 1) * B_V_DIM] = o.astype(o_ref.dtype)

    @pl.when(j == 0)
    def _():
        run(n_ctx)

    @pl.when(j > 0)
    def _():
        run(k_ref.shape[1])


def _diff_attn(qb, kb, vb, lam_vecs, subln_g, lam_init, n_ctx):
    B, N, _ = qb.shape
    tq = n_ctx
    return pl.pallas_call(
        functools.partial(_diff_attn_kernel, n_ctx, lam_init),
        grid=(B, N // tq),
        in_specs=[pl.BlockSpec((1, tq, B_WIDTH), lambda b, j: (b, j, 0)),
                  pl.BlockSpec((1, N, B_WIDTH), lambda b, j: (b, 0, 0)),
                  pl.BlockSpec((1, N, B_WIDTH), lambda b, j: (b, 0, 0)),
                  pl.BlockSpec((4, HEAD_DIM), lambda b, j: (0, 0)),
                  pl.BlockSpec((1, B_V_DIM), lambda b, j: (0, 0))],
        out_specs=pl.BlockSpec((1, tq, B_WIDTH), lambda b, j: (b, j, 0)),
        out_shape=jax.ShapeDtypeStruct((B, N, B_WIDTH), BF16),
        compiler_params=_cparams(("arbitrary", "arbitrary")),
    )(qb, kb, vb, lam_vecs.astype(F32), subln_g.reshape(1, B_V_DIM).astype(F32))


def _mix_out_kernel(n_in, *refs):
    xs = refs[:n_in]
    ws = refs[n_in:2 * n_in]
    h_ref, mod_ref, g_ref, b_ref, hn_ref, u_ref = refs[2 * n_in:]
    o = _dot(xs[0][0], ws[0][...])
    for x_ref, w_ref in zip(xs[1:], ws[1:]):
        o = o + _dot(x_ref[0], w_ref[...])
    z = DEEPNORM_ALPHA * h_ref[0] + mod_ref[0, 0, 2:3, :] * o
    hn = _layer_norm(z, g_ref[...], b_ref[...])
    hn_ref[0] = hn
    u_ref[0] = _pack_halves(hn * (1.0 + mod_ref[0, 0, 4:5, :]) + mod_ref[0, 0, 3:4, :])


def _mix_out(xs, ws, h, mod, ln_g, ln_b, n_ctx, row0):
    B, N, D = h.shape
    tm = ROW_TILE
    t0 = row0 // tm
    n_out = N - row0
    row_spec = lambda w: pl.BlockSpec((1, tm, w), lambda b, i: (b, i + t0, 0))
    out_spec = pl.BlockSpec((1, tm, D), lambda b, i: (b, i, 0))
    vec_spec = pl.BlockSpec((1, D), lambda b, i: (0, 0))
    return pl.pallas_call(
        functools.partial(_mix_out_kernel, len(xs)),
        grid=(B, n_out // tm),
        in_specs=([row_spec(x.shape[-1]) for x in xs]
                  + [pl.BlockSpec(w.shape, lambda b, i: (0, 0)) for w in ws]
                  + [row_spec(D),
                     pl.BlockSpec((1, 1, 6, D), lambda b, i: (b, jnp.minimum((i + t0) // (n_ctx // tm), 1), 0, 0)),
                     vec_spec, vec_spec]),
        out_specs=[out_spec, pl.BlockSpec((1, tm, D // 2), lambda b, i: (b, i, 0))],
        out_shape=[jax.ShapeDtypeStruct((B, n_out, D), F32), jax.ShapeDtypeStruct((B, n_out, D // 2), I32)],
        compiler_params=_cparams(("arbitrary", "arbitrary")),
    )(*xs, *[w.astype(BF16) for w in ws], h, mod, ln_g.reshape(1, D), ln_b.reshape(1, D))


def _router_kernel(u_ref, rt_ref, bias_ref, tri_ref, e_ref, gw_ref, rank_ref, cnt_ref, carry_ref):
    i = pl.program_id(0)

    @pl.when(i == 0)
    def _():
        carry_ref[...] = jnp.zeros_like(carry_ref)

    tm = u_ref.shape[0]
    per_group = N_EXPERTS // N_GROUPS
    neg = -jnp.inf
    u_lo, u_hi = _unpack_halves(u_ref[...])
    half = u_ref.shape[1]
    logits = (_dot_nt(rt_ref[:, :half], u_lo.astype(BF16))
              + _dot_nt(rt_ref[:, half:], u_hi.astype(BF16)))
    scores = jax.nn.sigmoid(logits)
    sel = scores + bias_ref[...]
    io_in = lax.broadcasted_iota(I32, (per_group, tm), 0)
    grp_rows = []
    for gi in range(N_GROUPS):
        sg = sel[gi * per_group:(gi + 1) * per_group]
        m1 = jnp.max(sg, axis=0, keepdims=True)
        i1 = jnp.min(jnp.where(sg == m1, io_in, per_group), axis=0, keepdims=True)
        m2 = jnp.max(jnp.where(io_in == i1, neg, sg), axis=0, keepdims=True)
        grp_rows.append(m1 + m2)
    grp = jnp.concatenate(grp_rows, axis=0)
    io_g = lax.broadcasted_iota(I32, grp.shape, 0)
    g_sel = jnp.zeros(grp.shape, F32)
    for _ in range(TOPK_GROUPS):
        m = jnp.max(grp, axis=0, keepdims=True)
        hit = io_g == jnp.min(jnp.where(grp == m, io_g, N_GROUPS), axis=0, keepdims=True)
        g_sel = jnp.where(hit, 1.0, g_sel)
        grp = jnp.where(hit, neg, grp)
    selm = jnp.concatenate(
        [jnp.where(g_sel[gi:gi + 1] > 0.5, sel[gi * per_group:(gi + 1) * per_group], NEG_INF)
         for gi in range(N_GROUPS)], axis=0)
    io_e = lax.broadcasted_iota(I32, selm.shape, 0)
    chosen_f = jnp.zeros(selm.shape, F32)
    idx, gws = [], []
    for _ in range(TOP_K):
        m = jnp.max(selm, axis=0, keepdims=True)
        ik = jnp.min(jnp.where(selm == m, io_e, N_EXPERTS), axis=0, keepdims=True)
        hit = io_e == ik
        idx.append(ik)
        gws.append(jnp.sum(jnp.where(hit, scores, 0.0), axis=0, keepdims=True))
        chosen_f = jnp.where(hit, 1.0, chosen_f)
        selm = jnp.where(hit, neg, selm)
    gw = jnp.concatenate(gws, axis=0)
    gw_ref[...] = gw / jnp.sum(gw, axis=0, keepdims=True) * ROUTED_SCALE
    e_ref[...] = jnp.concatenate(idx, axis=0)
    before = _dot(chosen_f.astype(BF16), tri_ref[...]) + carry_ref[...]
    ranks = [jnp.sum(jnp.where(io_e == ik, before, 0.0), axis=0, keepdims=True) for ik in idx]
    rank_ref[...] = jnp.concatenate(ranks, axis=0).astype(I32)
    carry_ref[...] = carry_ref[...] + jnp.sum(chosen_f, axis=1, keepdims=True)
    cnt_ref[...] = carry_ref[...].astype(I32)


def _router(u, router, bias):
    T = u.shape[0]
    D = router.shape[0]
    tm = ROW_TILE
    tri = jnp.asarray(np.triu(np.ones((tm, tm), np.float32), 1), BF16)
    tok_spec = pl.BlockSpec((TOP_K, tm), lambda i: (0, i))
    return pl.pallas_call(
        _router_kernel,
        grid=(T // tm,),
        in_specs=[pl.BlockSpec((tm, D // 2), lambda i: (i, 0)),
                  pl.BlockSpec((N_EXPERTS, D), lambda i: (0, 0)),
                  pl.BlockSpec((N_EXPERTS, 1), lambda i: (0, 0)),
                  pl.BlockSpec((tm, tm), lambda i: (0, 0))],
        out_specs=[tok_spec, tok_spec, tok_spec, pl.BlockSpec((N_EXPERTS, 1), lambda i: (0, 0))],
        out_shape=[jax.ShapeDtypeStruct((TOP_K, T), I32), jax.ShapeDtypeStruct((TOP_K, T), F32),
                   jax.ShapeDtypeStruct((TOP_K, T), I32), jax.ShapeDtypeStruct((N_EXPERTS, 1), I32)],
        scratch_shapes=[pltpu.VMEM((N_EXPERTS, 1), F32)],
        compiler_params=_cparams(("arbitrary",)),
    )(u, router.T.astype(BF16), bias.reshape(N_EXPERTS, 1).astype(F32), tri)


def _sc_mesh():
    return plsc.VectorSubcoreMesh(core_axis_name="c", subcore_axis_name="s")


def _sc_scatter_rows(x, dest, n_rows):
    T, W = x.shape
    K = dest.shape[0]
    win = SC_WINDOW

    @functools.partial(pl.kernel, out_type=jax.ShapeDtypeStruct((n_rows, W), x.dtype), mesh=_sc_mesh(),
                       scratch_types=[])
    def scatter(x_hbm, i_hbm, o_hbm):
        def body(x_vmem, i_vmem):
            for k in range(K):
                pltpu.sync_copy(x_vmem, o_hbm.at[i_vmem.at[k]])

        pltpu.emit_pipeline(
            body,
            grid=(T // win,),
            in_specs=[pl.BlockSpec((win, W), lambda j: (j, 0), pipeline_mode=pl.Buffered(1)),
                      pl.BlockSpec((K, win), lambda j: (0, j))],
            out_specs=[],
            core_axis_name=("c", "s"),
            dimension_semantics=(pltpu.PARALLEL,),
        )(x_hbm, i_hbm)

    return scatter(x, dest)


def _sc_gather_rows(y, dest):
    K, T = dest.shape
    W = y.shape[1]
    win = SC_WINDOW

    @functools.partial(pl.kernel, out_type=jax.ShapeDtypeStruct((K * T, W), y.dtype), mesh=_sc_mesh(),
                       scratch_types=[])
    def gather(y_hbm, i_hbm, o_hbm):
        def body(i_vmem, o_vmem):
            pltpu.sync_copy(y_hbm.at[i_vmem.at[0]], o_vmem)

        pltpu.emit_pipeline(
            body,
            grid=(K * T // win,),
            in_specs=[pl.BlockSpec((1, win), lambda j: (0, j))],
            out_specs=[pl.BlockSpec((win, W), lambda j: (j, 0), pipeline_mode=pl.Buffered(1))],
            core_axis_name=("c", "s"),
            dimension_semantics=(pltpu.PARALLEL,),
        )(i_hbm, o_hbm)

    return gather(y, dest.reshape(1, K * T)).reshape(K, T, W)


def _expert_kernel(first_ref, count_ref, x_hbm, wi_ref, wo_ref, y_hbm, wi_b, wo_b, xbuf, ybuf, sem_in, sem_out):
    e = pl.program_id(0)
    n_in, blk = xbuf.shape[0], xbuf.shape[1]
    n_out = ybuf.shape[0]
    g0, g1 = first_ref[e], first_ref[e + 1]
    g_end = first_ref[pl.num_programs(0)]

    def x_copy(g):
        return pltpu.make_async_copy(x_hbm.at[pl.ds(g * blk, blk)], xbuf.at[g % n_in], sem_in.at[g % n_in])

    def y_copy(g):
        return pltpu.make_async_copy(ybuf.at[g % n_out], y_hbm.at[pl.ds(g * blk, blk)], sem_out.at[g % n_out])

    for ahead in range(n_in - 1):
        @pl.when(jnp.logical_and(e == 0, g_end > ahead))
        def _():
            x_copy(ahead).start()

    @pl.when(g1 > g0)
    def _():
        wi_b[...] = wi_ref[0, 0].astype(BF16)
        wo_b[...] = wo_ref[0, 0].astype(BF16)

    ff = wo_b.shape[0]

    def block(g, carry):
        x_copy(g).wait()

        @pl.when(g + n_in - 1 < g_end)
        def _():
            x_copy(g + n_in - 1).start()

        @pl.when(g >= n_out)
        def _():
            y_copy(g - n_out).wait()

        row = lax.broadcasted_iota(I32, (blk, xbuf.shape[2]), 0) + (g - g0) * blk
        hcat = _dot_halves(jnp.where(row < count_ref[e], xbuf[g % n_in], 0), wi_b)
        act = (_silu(hcat[:, :ff]) * hcat[:, ff:]).astype(BF16)
        ybuf[g % n_out] = _pack_halves(_dot(act, wo_b[...]))
        y_copy(g).start()
        return carry

    lax.fori_loop(g0, g1, block, 0)

    @pl.when(e == pl.num_programs(0) - 1)
    def _():
        for back in range(n_out, 0, -1):
            @pl.when(g_end >= back)
            def _():
                y_copy(g_end - back).wait()


def _experts(xs, first_block, counts, w_in, w_out, layer):
    P, half = xs.shape
    n_exp, D, ff2 = w_in.shape[-3:]
    return pl.pallas_call(
        _expert_kernel,
        grid_spec=pltpu.PrefetchScalarGridSpec(
            num_scalar_prefetch=2,
            grid=(n_exp,),
            in_specs=[pl.BlockSpec(memory_space=pl.ANY),
                      pl.BlockSpec((1, 1, D, ff2), lambda e, fb, cnt: (layer, e, 0, 0)),
                      pl.BlockSpec((1, 1, ff2 // 2, D), lambda e, fb, cnt: (layer, e, 0, 0))],
            out_specs=pl.BlockSpec(memory_space=pl.ANY),
            scratch_shapes=[pltpu.VMEM((D, ff2), BF16), pltpu.VMEM((ff2 // 2, D), BF16),
                            pltpu.VMEM((EXPERT_IN_SLOTS, MOE_BLOCK, half), I32),
                            pltpu.VMEM((EXPERT_OUT_SLOTS, MOE_BLOCK, half), I32),
                            pltpu.SemaphoreType.DMA((EXPERT_IN_SLOTS,)),
                            pltpu.SemaphoreType.DMA((EXPERT_OUT_SLOTS,))]),
        out_shape=jax.ShapeDtypeStruct((P, half), I32),
        compiler_params=_cparams(("arbitrary",)),
    )(first_block, counts.astype(I32), xs, w_in, w_out)


def _combine_kernel(yg_ref, gw_ref, u_ref, wsi_ref, wso_ref, h_ref, mod_ref, g_ref, b_ref, o_ref):
    ff = wso_ref.shape[0]
    hcat = _dot_halves(u_ref[...], wsi_ref)
    shared = _dot((_silu(hcat[:, :ff]) * hcat[:, ff:]).astype(BF16), wso_ref[...])
    lo, hi = None, None
    for k in range(TOP_K):
        y_lo, y_hi = _unpack_halves(yg_ref[k])
        gk = gw_ref[:, k:k + 1]
        lo = y_lo * gk if lo is None else lo + y_lo * gk
        hi = y_hi * gk if hi is None else hi + y_hi * gk
    routed = jnp.concatenate([lo, hi], axis=1)
    z = DEEPNORM_ALPHA * h_ref[...] + mod_ref[0] * (routed + shared)
    o_ref[...] = _layer_norm(z, g_ref[...], b_ref[...])


def _combine(yg, gw_t, u, ws_in, ws_out, h, gate, gate_index, ln_g, ln_b):
    T, D = h.shape
    tm = ROW_TILE
    vec_spec = pl.BlockSpec((1, D), lambda i: (0, 0))
    row_spec = pl.BlockSpec((tm, D), lambda i: (i, 0))
    packed_spec = pl.BlockSpec((tm, D // 2), lambda i: (i, 0))
    return pl.pallas_call(
        _combine_kernel,
        grid=(T // tm,),
        in_specs=[pl.BlockSpec((TOP_K, tm, D // 2), lambda i: (0, i, 0)),
                  pl.BlockSpec((tm, TOP_K), lambda i: (i, 0)),
                  packed_spec,
                  pl.BlockSpec(ws_in.shape, lambda i: (0, 0)),
                  pl.BlockSpec(ws_out.shape, lambda i: (0, 0)),
                  row_spec,
                  pl.BlockSpec((1,) + gate.shape[1:], lambda i: (gate_index(i), 0, 0)),
                  vec_spec, vec_spec],
        out_specs=row_spec,
        out_shape=jax.ShapeDtypeStruct((T, D), F32),
        compiler_params=_cparams(("arbitrary",)),
    )(yg, gw_t, u, ws_in.astype(BF16), ws_out.astype(BF16), h, gate,
      ln_g.reshape(1, D), ln_b.reshape(1, D))


def _slots_kernel(e_ref, rank_ref, start_ref, dest_ref):
    io_e = lax.broadcasted_iota(I32, (N_EXPERTS, e_ref.shape[1]), 0)
    rows = [jnp.sum(jnp.where(io_e == e_ref[k:k + 1, :], start_ref[...], 0), axis=0, keepdims=True)
            for k in range(TOP_K)]
    dest_ref[...] = jnp.concatenate(rows, axis=0) + rank_ref[...]


def _slots(eidx, rank, pstart):
    T = eidx.shape[1]
    tm = ROW_TILE
    tok_spec = pl.BlockSpec((TOP_K, tm), lambda i: (0, i))
    return pl.pallas_call(
        _slots_kernel,
        grid=(T // tm,),
        in_specs=[tok_spec, tok_spec, pl.BlockSpec((N_EXPERTS, 1), lambda i: (0, 0))],
        out_specs=tok_spec,
        out_shape=jax.ShapeDtypeStruct((TOP_K, T), I32),
        compiler_params=_cparams(("arbitrary",)),
    )(eidx, rank, pstart.reshape(N_EXPERTS, 1))


def _moe_layer(u, h, gate, gate_index, router, bias, w_in, w_out, ws_in, ws_out, ln_g, ln_b, layer):
    T = u.shape[0]
    eidx, gw, rank, counts = _router(u, router, bias)
    counts = counts[:, 0]
    padded = (counts + MOE_BLOCK - 1) // MOE_BLOCK * MOE_BLOCK
    pend = jnp.cumsum(padded)
    pstart = (pend - padded).astype(I32)
    dest = _slots(eidx, rank, pstart)
    n_blocks = -(-(T * TOP_K + N_EXPERTS * (MOE_BLOCK - 1)) // MOE_BLOCK)
    first_block = jnp.concatenate([jnp.zeros((1,), I32), (pend // MOE_BLOCK).astype(I32)])
    xs = _sc_scatter_rows(u, dest, n_blocks * MOE_BLOCK)
    y = _experts(xs, first_block, counts, w_in, w_out, layer)
    return _combine(_sc_gather_rows(y, dest), gw.T, u, ws_in, ws_out, h, gate, gate_index, ln_g, ln_b)


def _seg_ones(width=LANES):
    idx = np.arange(width) // HEAD_DIM
    return jnp.asarray((idx[:, None] == idx[None, :]).astype(np.float32), BF16)


def _head_sum(x, ones_ref):
    outs = []
    for j in range(x.shape[1] // LANES):
        xc = x[:, j * LANES:(j + 1) * LANES]
        hi = xc.astype(BF16)
        lo = (xc - hi.astype(F32)).astype(BF16)
        outs.append(_dot(hi, ones_ref[...]) + _dot(lo, ones_ref[...]))
    return jnp.concatenate(outs, axis=1)


def _rwkv_proj_kernel(seg_tiles, h_ref, hp_ref, hn_ref, mod_ref, mu_ref, wrkv_ref, g1_ref, g2_ref, d1_ref, d2_ref,
                      d0_ref, i1_ref, i2_ref, i0_ref, kk_ref, ka_ref, rk_ref, ones_ref,
                      r_ref, v_ref, a_ref, g_ref, bonus_ref, w_ref, k_ref, b_ref):
    i = pl.program_id(0)
    nb = hp_ref.shape[0]
    shift, scale = mod_ref[0, 0], mod_ref[0, 1]
    u = h_ref[...] * (1.0 + scale) + shift
    starts = jnp.logical_or(i == 0, i == seg_tiles)
    ends = jnp.logical_or(i == seg_tiles - 1, i == pl.num_programs(0) - 1)
    u_before = (hp_ref[...] * (1.0 + scale[:nb]) + shift[:nb]) * jnp.where(starts, 0.0, 1.0)
    u_after = (hn_ref[...] * (1.0 + scale[:nb]) + shift[:nb]) * jnp.where(ends, 0.0, 1.0)
    dx = 0.5 * (jnp.concatenate([u_before, u[:-nb]], axis=0) + jnp.concatenate([u[nb:], u_after], axis=0)) - u
    mix = lambda m: (u + dx * mu_ref[m:m + 1, :])
    xr, xw, xk, xv, xa, xg = [mix(m) for m in range(6)]
    r = _dot(xr.astype(BF16), wrkv_ref[0])
    k = _dot(xk.astype(BF16), wrkv_ref[1])
    v = _dot(xv.astype(BF16), wrkv_ref[2])
    g = _dot(jax.nn.sigmoid(_dot(xg.astype(BF16), g1_ref[...])).astype(BF16), g2_ref[...])
    kk = k * kk_ref[...]
    kk = kk * lax.rsqrt(jnp.maximum(_head_sum(kk * kk, ones_ref), 1e-24))
    r_ref[...] = r
    v_ref[...] = v
    a_ref[...] = -kk
    g_ref[...] = g
    k_sum = None
    xw_b = xw.astype(BF16)
    xa_b = xa.astype(BF16)
    for d in range(2):
        lw = d0_ref[d:d + 1, :] + _dot(jnp.tanh(_dot(xw_b, d1_ref[d])).astype(BF16), d2_ref[d])
        softplus = jnp.maximum(-lw, 0.0) + jnp.log(1.0 + jnp.exp(-jnp.abs(lw)))
        logw = -softplus - 0.5
        w_ref[d] = jnp.exp(-jnp.exp(logw))
        eta = jax.nn.sigmoid(i0_ref[d:d + 1, :] + _dot(_dot(xa_b, i1_ref[d]).astype(BF16), i2_ref[d]))
        k_d = k * (1.0 + (eta - 1.0) * ka_ref[...])
        k_ref[d] = k_d
        b_ref[d] = kk * eta
        k_sum = k_d if k_sum is None else k_sum + k_d
    bonus_ref[...] = _head_sum(r * k_sum * rk_ref[...], ones_ref) * v


def _rwkv_proj(h, mod_rows, batch, n_ctx, p):
    T, D = h.shape
    tm = PROJ_TILE
    per_tile = tm // batch
    seg_tiles = n_ctx // per_tile
    n_steps = T // batch
    row = pl.BlockSpec((tm, D), lambda i: (i, 0))
    before = pl.BlockSpec((batch, D), lambda i: (jnp.maximum(i * per_tile - 1, 0), 0))
    after = pl.BlockSpec((batch, D), lambda i: (jnp.minimum((i + 1) * per_tile, n_steps - 1), 0))
    mod_spec = pl.BlockSpec((1, 2, tm, D), lambda i: (jnp.minimum(i // seg_tiles, 1), 0, 0, 0))
    row2 = pl.BlockSpec((2, tm, D), lambda i: (0, i, 0))
    full = lambda a: pl.BlockSpec(a.shape, lambda i: (0,) * a.ndim)
    bf = lambda a: a.astype(BF16)
    consts = [p['mu'], bf(p['w_rkv']), bf(p['gate1']), bf(p['gate2']), bf(p['dec1']), bf(p['dec2']), p['dec0'],
              bf(p['icl1']), bf(p['icl2']), p['icl0'], p['k_k'].reshape(1, D), p['k_a'].reshape(1, D),
              p['r_k'].reshape(1, D), _seg_ones()]
    one = jax.ShapeDtypeStruct((T, D), F32)
    two = jax.ShapeDtypeStruct((2, T, D), F32)
    return pl.pallas_call(
        functools.partial(_rwkv_proj_kernel, seg_tiles),
        grid=(T // tm,),
        in_specs=[row, before, after, mod_spec] + [full(a) for a in consts],
        out_specs=[row, row, row, row, row, row2, row2, row2],
        out_shape=[one, one, one, one, one, two, two, two],
        compiler_params=_cparams(("arbitrary",)),
    )(h, h, h, mod_rows, *consts)


def _scan_kernel(r_ref, w_ref, k_ref, v_ref, a_ref, b_ref, ones_ref, hsel_ref,
                 y_ref, s_ref, vt_ref):
    d = pl.program_id(0)
    c = pl.program_id(2)
    tc, nb = r_ref.shape[0], r_ref.shape[1]
    tw = SCAN_TILE
    n_wide = r_ref.shape[2] // tw
    heads = tw // HEAD_DIM
    assert heads * tc == tw

    @pl.when(c == 0)
    def _():
        s_ref[...] = jnp.zeros_like(s_ref)

    for bb in range(nb):
        for q in range(n_wide):
            vt = v_ref[:, bb, q * tw:(q + 1) * tw].T
            vt_ref[bb * n_wide + q] = jnp.concatenate(
                [vt[h * HEAD_DIM:(h + 1) * HEAD_DIM] for h in range(heads)], axis=1)

    head_base = (lax.broadcasted_iota(I32, (HEAD_DIM, LANES), 1) // HEAD_DIM) * tc
    tiles = [(bb, q) for bb in range(nb) for q in range(n_wide)]
    groups = [tiles[i:i + SCAN_GROUP] for i in range(0, len(tiles), SCAN_GROUP)]

    def stacked(grp, get, dtype=F32):
        def wide(bb, q):
            return jnp.concatenate(
                [jnp.broadcast_to(get(bb, slice(q * tw + hf * LANES, q * tw + (hf + 1) * LANES)).astype(dtype),
                                  (HEAD_DIM, LANES)) for hf in range(tw // LANES)], axis=1)
        return jnp.concatenate([wide(bb, q) for bb, q in grp], axis=0)

    def load_state(grp):
        return jnp.concatenate([s_ref[bb * n_wide + q] for bb, q in grp], axis=0)

    def emit_y(grp, st_b, t_y):
        r_rows = stacked(grp, lambda bb, cols: r_ref[t_y, bb:bb + 1, cols], BF16)
        yh = _dot_nt(hsel_ref[...], st_b * r_rows)
        first = tiles.index(grp[0])
        y_ref[0, t_y, :, first * HEAD_DIM:(first + len(grp)) * HEAD_DIM] = yh[:heads]

    def step(s_i, carry):
        t = jnp.where(d == 0, s_i, tc - 1 - s_i)
        t_prev = jnp.where(s_i == 0, t, jnp.where(d == 0, t - 1, t + 1))
        pick = head_base + t
        for grp in groups:
            one = lambda ref: stacked(grp, lambda bb, cols: ref[t, bb:bb + 1, cols])
            two = lambda ref: stacked(grp, lambda bb, cols: ref[0, t, bb:bb + 1, cols])
            st = load_state(grp)
            st_b = st.astype(BF16)
            a_rows = stacked(grp, lambda bb, cols: a_ref[t, bb:bb + 1, cols], BF16)
            sa = _dot(st_b * a_rows, ones_ref[...])
            emit_y(grp, st_b, t_prev)
            vcol = jnp.concatenate(
                [jnp.concatenate([jnp.take_along_axis(vt_ref[bb * n_wide + q, :, hf * LANES:(hf + 1) * LANES],
                                                      pick, axis=1) for hf in range(tw // LANES)], axis=1)
                 for bb, q in grp], axis=0)
            st = st * two(w_ref) + sa * two(b_ref) + vcol * two(k_ref)
            for j, (bb, q) in enumerate(grp):
                s_ref[bb * n_wide + q] = st[j * HEAD_DIM:(j + 1) * HEAD_DIM]
        return carry

    lax.fori_loop(0, tc, step, 0, unroll=SCAN_UNROLL)
    t_last = jnp.where(d == 0, tc - 1, 0)
    for grp in groups:
        emit_y(grp, load_state(grp).astype(BF16), t_last)


def _wkv_scan(r, w, k, v, a, b, n_ctx):
    N, B, D = r.shape
    tc = SCAN_CHUNK
    wc = SCAN_COLS
    n_wide = wc // SCAN_TILE
    nc = N // tc
    ncc = n_ctx // tc

    def chunk(d, c):
        rev = jnp.where(c < ncc, ncc - 1 - c, nc - 1 - (c - ncc))
        return jnp.where(d == 0, c, rev)

    one = pl.BlockSpec((tc, B, wc), lambda d, g, c: (chunk(d, c), 0, g))
    two = pl.BlockSpec((1, tc, B, wc), lambda d, g, c: (d, chunk(d, c), 0, g))
    seg = np.arange(SCAN_TILE) // HEAD_DIM
    hsel = np.zeros((8, SCAN_TILE), np.float32)
    for hh in range(SCAN_TILE // HEAD_DIM):
        hsel[hh, seg == hh] = 1.0
    const = lambda a_: pl.BlockSpec(a_.shape, lambda d, g, c: (0, 0))
    consts = [_seg_ones(SCAN_TILE), jnp.asarray(hsel, BF16)]
    heads = SCAN_TILE // HEAD_DIM
    ncg = D // wc
    y = pl.pallas_call(
        _scan_kernel,
        grid=(2, ncg, nc),
        in_specs=[one, two, two, one, one, two] + [const(a_) for a_ in consts],
        out_specs=pl.BlockSpec((1, tc, heads, B * n_wide * HEAD_DIM), lambda d, g, c: (d, chunk(d, c), 0, g)),
        out_shape=jax.ShapeDtypeStruct((2, N, heads, ncg * B * n_wide * HEAD_DIM), F32),
        scratch_shapes=[pltpu.VMEM((B * n_wide, HEAD_DIM, SCAN_TILE), F32),
                        pltpu.VMEM((B * n_wide, HEAD_DIM, SCAN_TILE), F32)],
        compiler_params=_cparams(("arbitrary", "arbitrary", "arbitrary")),
    )(r, w, k, v, a, b, *consts)
    y = y.reshape(2, N, heads, ncg, B, n_wide * HEAD_DIM)
    return jnp.transpose(y, (0, 1, 4, 2, 3, 5)).reshape(2, N, B, D)


def _scan_head_order(d):
    heads = SCAN_TILE // HEAD_DIM
    n_wide = SCAN_COLS // SCAN_TILE
    ncg = d // SCAN_COLS
    order = []
    for g in range(ncg):
        for q in range(n_wide):
            for h in range(heads):
                order.append((h * ncg + g) * n_wide + q)
    return tuple(order)


def _rwkv_out_kernel(head_order, y0_ref, y1_ref, bonus_ref, g_ref, lnx_ref, ones_ref, w_ref, h_ref, mod_ref,
                     lg_ref, lb_ref, hn_ref, u_ref):
    y_in = y0_ref[0] + y1_ref[0]
    y = jnp.concatenate([y_in[:, p * HEAD_DIM:(p + 1) * HEAD_DIM] for p in head_order], axis=1)
    ym = _head_sum(y, ones_ref) * (1.0 / HEAD_DIM)
    yc = y - ym
    yv = _head_sum(yc * yc, ones_ref) * (1.0 / HEAD_DIM)
    yn = yc * lax.rsqrt(yv + LNX_EPS) * lnx_ref[0:1, :] + lnx_ref[1:2, :]
    x = ((yn + bonus_ref[...]) * g_ref[...]).astype(BF16)
    o = _dot(x, w_ref[...])
    z = DEEPNORM_ALPHA * h_ref[...] + mod_ref[0] * o
    hn = _layer_norm(z, lg_ref[...], lb_ref[...])
    hn_ref[...] = hn
    u_ref[...] = _pack_halves(hn * (1.0 + mod_ref[2]) + mod_ref[1])


def _rwkv_out(y, bonus, g, lnx, w_out, h, mod_rows, ln_g, ln_b, row0):
    T, D = h.shape
    tm = ROW_TILE
    t0 = row0 // tm
    off = pl.BlockSpec((tm, D), lambda i: (i + t0, 0))
    out = pl.BlockSpec((tm, D), lambda i: (i, 0))
    full = lambda a: pl.BlockSpec(a.shape, lambda i: (0,) * a.ndim)
    vec = pl.BlockSpec((1, D), lambda i: (0, 0))
    ones = _seg_ones()
    w_b = w_out.astype(BF16)
    return pl.pallas_call(
        functools.partial(_rwkv_out_kernel, _scan_head_order(D)),
        grid=((T - row0) // tm,),
        in_specs=[pl.BlockSpec((1, tm, D), lambda i: (0, i + t0, 0)),
                  pl.BlockSpec((1, tm, D), lambda i: (1, i + t0, 0)),
                  off, off, full(lnx), full(ones), full(w_b), off, full(mod_rows), vec, vec],
        out_specs=[out, pl.BlockSpec((tm, D // 2), lambda i: (i, 0))],
        out_shape=[jax.ShapeDtypeStruct((T - row0, D), F32), jax.ShapeDtypeStruct((T - row0, D // 2), I32)],
        compiler_params=_cparams(("arbitrary",)),
    )(y, y, bonus, g, lnx, ones, w_b, h, mod_rows, ln_g.reshape(1, D), ln_b.reshape(1, D))


def kernel(x, c, ctx, c_ctx, ada_w, ada_b, post_ln_g, post_ln_b, att_w_in, att_w_out, att_sink, diff_lambda_vecs, diff_subln_g, rk_mu, rk_w_rkv, rk_w_out, rk_decay0, rk_decay1, rk_decay2, rk_iclr0, rk_iclr1, rk_iclr2, rk_gate1, rk_gate2, rk_k_k, rk_k_a, rk_r_k, rk_lnx, moe_router, moe_bias, moe_w_in, moe_w_out, moe_ws_in, moe_ws_out):
    B, S, D = x.shape
    L = ctx.shape[1]
    N = L + S
    tm = ROW_TILE
    assert L % tm == 0 and S % tm == 0 and L % SCAN_CHUNK == 0 and S % SCAN_CHUNK == 0
    assert tm % B == 0 and PROJ_TILE % B == 0 and L % (PROJ_TILE // B) == 0 and D % SCAN_COLS == 0

    rows = -(-(B + 1) // 8) * 8
    cvec = jnp.concatenate([c, c_ctx[None, :], jnp.zeros((rows - B - 1, D), F32)], axis=0)
    mods = [_mod_table(_ada_mod(cvec, ada_w[i], ada_b[i]), B, D) for i in range(DEPTH)]

    h0 = jnp.concatenate([ctx, x], axis=1)
    lam_init = 0.8 - 0.6 * math.exp(-0.3 * 0)
    qa, ka, va, qb, kb, vb = _attn_inproj(h0, mods[0], att_w_in[0], L)
    oa = _win_attn(qa, ka, va, att_sink[0], L)
    ob = _diff_attn(qb, kb, vb, diff_lambda_vecs[0], diff_subln_g[0], lam_init, L)
    h1, u1 = _mix_out([oa, ob], [att_w_out[0][:A_WIDTH], att_w_out[0][A_WIDTH:]], h0, mods[0],
                      post_ln_g[0, 0], post_ln_b[0, 0], L, 0)
    tiles_b, tiles_c = N // tm, L // tm
    gate0 = mods[0][:, :, 5].reshape(B * 2, 1, D)
    gate0_index = lambda i: (i // tiles_b) * 2 + jnp.minimum((i % tiles_b) // tiles_c, 1)
    h2 = _moe_layer(u1.reshape(B * N, D // 2), h1.reshape(B * N, D), gate0, gate0_index, moe_router[0], moe_bias[0],
                    moe_w_in, moe_w_out, moe_ws_in[0], moe_ws_out[0],
                    post_ln_g[0, 1], post_ln_b[0, 1], 0).reshape(B, N, D)

    m_ctx, m_lat = mods[1][:, 0], mods[1][:, 1]
    h2_t = jnp.swapaxes(h2, 0, 1).reshape(N * B, D)
    rows_of = lambda m, j, n: jnp.tile(m[:, j], (n // B, 1))
    proj_mod = jnp.stack([jnp.stack([rows_of(m, 0, PROJ_TILE), rows_of(m, 1, PROJ_TILE)]) for m in (m_ctx, m_lat)])
    params = dict(mu=rk_mu[0], w_rkv=rk_w_rkv[0], gate1=rk_gate1[0], gate2=rk_gate2[0],
                  dec0=rk_decay0[0], dec1=rk_decay1[0], dec2=rk_decay2[0],
                  icl0=rk_iclr0[0], icl1=rk_iclr1[0], icl2=rk_iclr2[0],
                  k_k=rk_k_k[0], k_a=rk_k_a[0], r_k=rk_r_k[0])
    r, v, a, g, bonus, w2, k2, b2 = _rwkv_proj(h2_t, proj_mod, B, L, params)
    tmaj = lambda t: t.reshape(t.shape[:-2] + (N, B, D))
    y = _wkv_scan(tmaj(r), tmaj(w2), tmaj(k2), tmaj(v), tmaj(a), tmaj(b2), L)
    lat_rows = lambda j: rows_of(m_lat, j, tm)
    h3, u3 = _rwkv_out(y.reshape(2, N * B, D), bonus, g, rk_lnx[0], rk_w_out[0], h2_t,
                       jnp.stack([lat_rows(2), lat_rows(3), lat_rows(4)]),
                       post_ln_g[1, 0], post_ln_b[1, 0], L * B)
    out = _moe_layer(u3, h3, lat_rows(5)[None], lambda i: 0, moe_router[1], moe_bias[1],
                     moe_w_in, moe_w_out, moe_ws_in[1], moe_ws_out[1],
                     post_ln_g[1, 1], post_ln_b[1, 1], 1)
    return jnp.swapaxes(out.reshape(S, B, D), 0, 1)
```

```python
import functools
import math

import numpy as np
import jax
import jax.numpy as jnp
from jax import lax
from jax.experimental import pallas as pl
from jax.experimental.pallas import tpu as pltpu
from jax.experimental.pallas import tpu_sc as plsc

F32 = jnp.float32
BF16 = jnp.bfloat16
I32 = jnp.int32

HEAD_DIM = 64
GRID_W = 64
ROPE_AXIS_DIM = HEAD_DIM // 2
ROPE_THETA = 10000.0
Q_BLOCK = 128
A_Q_HEADS = 8
A_KV_HEADS = 2
A_GROUP = A_Q_HEADS // A_KV_HEADS
A_WIDTH = A_Q_HEADS * HEAD_DIM
A_KV_WIDTH = A_KV_HEADS * HEAD_DIM
B_HEADS = 4
B_V_DIM = 2 * HEAD_DIM
B_WIDTH = B_HEADS * B_V_DIM
LNX_EPS = 64e-5
N_EXPERTS = 256
TOP_K = 8
N_GROUPS = 8
TOPK_GROUPS = 4
ROUTED_SCALE = 2.5
MOE_BLOCK = 256
EXPERT_IN_SLOTS = 4
EXPERT_OUT_SLOTS = 2
LN_EPS = 1e-5
SUBLN_EPS = 1e-5
NEG_INF = -1e30
DEPTH = 2
DEEPNORM_ALPHA = (2 * DEPTH) ** 0.25

LANES = 128
ADA_COLS = 768
ROW_TILE = 256
DIFF_Q_TILE = 128
PROJ_TILE = 256
SC_WINDOW = 128
SCAN_CHUNK = 64
SCAN_COLS = 1024
SCAN_TILE = 256
SCAN_GROUP = 8
SCAN_UNROLL = 8
VMEM_LIMIT = 56 * 1024 * 1024


def _cparams(sem):
    return pltpu.CompilerParams(dimension_semantics=sem, vmem_limit_bytes=VMEM_LIMIT)


def _silu(x):
    return x * jax.nn.sigmoid(x)


def _layer_norm(z, g, b):
    mu = jnp.mean(z, -1, keepdims=True)
    zc = z - mu
    var = jnp.mean(zc * zc, -1, keepdims=True)
    return zc * lax.rsqrt(var + LN_EPS) * g + b


def _dot(a, b):
    return jnp.dot(a, b, preferred_element_type=F32)


def _dot_nt(a, b):
    return lax.dot_general(a, b, (((1,), (1,)), ((), ())), preferred_element_type=F32)


def _pack_halves(x):
    half = x.shape[1] // 2
    bits = lambda v: lax.bitcast_convert_type(v.astype(BF16).astype(F32), I32)
    return lax.shift_right_logical(bits(x[:, :half]), 16) | bits(x[:, half:])


def _unpack_halves(p):
    lo = lax.bitcast_convert_type(lax.shift_left(p, 16), F32)
    hi = lax.bitcast_convert_type(p & jnp.int32(-65536), F32)
    return lo, hi


def _dot_halves(p, w_ref_or_array):
    lo, hi = _unpack_halves(p)
    half = p.shape[1]
    return _dot(lo.astype(BF16), w_ref_or_array[:half]) + _dot(hi.astype(BF16), w_ref_or_array[half:])


def _ada_kernel(c_ref, w_ref, b_ref, o_ref):
    c = c_ref[...]
    o_ref[...] = _dot(_silu(c).astype(BF16), w_ref[...].astype(BF16)) + b_ref[...]


def _ada_mod(cvec, w, bias):
    R, D = cvec.shape
    n_out = w.shape[1]
    tn = ADA_COLS
    return pl.pallas_call(
        _ada_kernel,
        grid=(n_out // tn,),
        in_specs=[pl.BlockSpec((R, D), lambda j: (0, 0)),
                  pl.BlockSpec((D, tn), lambda j: (0, j)),
                  pl.BlockSpec((1, tn), lambda j: (0, j))],
        out_specs=pl.BlockSpec((R, tn), lambda j: (0, j)),
        out_shape=jax.ShapeDtypeStruct((R, n_out), F32),
        compiler_params=_cparams(("arbitrary",)),
    )(cvec, w, bias.reshape(1, n_out))


def _mod_table(m, batch, d):
    m_lat = m[:batch].reshape(batch, 6, d)
    m_ctx = jnp.broadcast_to(m[batch].reshape(1, 6, d), (batch, 6, d))
    return jnp.stack([m_ctx, m_lat], axis=1)


def _mod_spec(d, ctx_tiles):
    return pl.BlockSpec((1, 1, 6, d), lambda b, i: (b, jnp.minimum(i // ctx_tiles, 1), 0, 0))


def _rope_tables(n_ctx, n_lat):
    rows = n_lat // GRID_W
    row = np.repeat(np.arange(rows), GRID_W).astype(np.float32)
    col = np.tile(np.arange(GRID_W), rows).astype(np.float32)
    inv = (ROPE_THETA ** (-np.arange(0, ROPE_AXIS_DIM, 2, dtype=np.float32) / ROPE_AXIS_DIM)).astype(np.float32)
    ar = row[:, None] * inv
    ac = col[:, None] * inv
    ang = np.concatenate([ar, ar, ac, ac], -1)
    cos = np.cos(ang).astype(np.float32)
    sin = np.sin(ang).astype(np.float32)
    lower = (np.arange(HEAD_DIM) % ROPE_AXIS_DIM) < (ROPE_AXIS_DIM // 2)
    sin_up = np.where(lower[None, :], -sin, 0.0)
    sin_dn = np.where(lower[None, :], 0.0, sin)

    def full(t, ctx_fill):
        t = np.concatenate([np.full((n_ctx, HEAD_DIM), ctx_fill, np.float32), t], 0)
        return jnp.asarray(np.tile(t, (1, LANES // HEAD_DIM)))

    return full(cos, 1.0), full(sin_up, 0.0), full(sin_dn, 0.0)


def _inproj_kernel(h_ref, mod_ref, w_ref, cos_ref, su_ref, sd_ref,
                   qa_ref, ka_ref, va_ref, qb_ref, kb_ref, vb_ref):
    h = h_ref[0]
    shift = mod_ref[0, 0, 0:1, :]
    scale = mod_ref[0, 0, 1:2, :]
    u = (h * (1.0 + scale) + shift).astype(BF16)
    y = _dot(u, w_ref[...])
    cos, s_up, s_dn = cos_ref[...], su_ref[...], sd_ref[...]
    q_scale = HEAD_DIM ** -0.5

    def rope(xc):
        half = ROPE_AXIS_DIM // 2
        return xc * cos + pltpu.roll(xc, LANES - half, 1) * s_up + pltpu.roll(xc, half, 1) * s_dn

    def emit(out_ref, col0, width, roped, mul):
        for j in range(width // LANES):
            xc = y[:, col0 + j * LANES: col0 + (j + 1) * LANES]
            if roped:
                xc = rope(xc)
            if mul != 1.0:
                xc = xc * mul
            out_ref[0, :, j * LANES:(j + 1) * LANES] = xc.astype(out_ref.dtype)

    c = 0
    emit(qa_ref, c, A_WIDTH, True, q_scale); c += A_WIDTH
    emit(ka_ref, c, A_KV_WIDTH, True, 1.0); c += A_KV_WIDTH
    emit(va_ref, c, A_KV_WIDTH, False, 1.0); c += A_KV_WIDTH
    emit(qb_ref, c, B_WIDTH, True, q_scale); c += B_WIDTH
    emit(kb_ref, c, B_WIDTH, True, 1.0); c += B_WIDTH
    emit(vb_ref, c, B_WIDTH, False, 1.0)


def _attn_inproj(h, mod, w_in, n_ctx):
    B, N, D = h.shape
    tm = ROW_TILE
    cos, s_up, s_dn = _rope_tables(n_ctx, N - n_ctx)
    widths = (A_WIDTH, A_KV_WIDTH, A_KV_WIDTH, B_WIDTH, B_WIDTH, B_WIDTH)
    tab_spec = pl.BlockSpec((tm, LANES), lambda b, i: (i, 0))
    return pl.pallas_call(
        _inproj_kernel,
        grid=(B, N // tm),
        in_specs=[pl.BlockSpec((1, tm, D), lambda b, i: (b, i, 0)),
                  _mod_spec(D, n_ctx // tm),
                  pl.BlockSpec(w_in.shape, lambda b, i: (0, 0)),
                  tab_spec, tab_spec, tab_spec],
        out_specs=[pl.BlockSpec((1, tm, w), lambda b, i: (b, i, 0)) for w in widths],
        out_shape=[jax.ShapeDtypeStruct((B, N, w), BF16) for w in widths],
        compiler_params=_cparams(("arbitrary", "arbitrary")),
    )(h, mod, w_in.astype(BF16), cos, s_up, s_dn)


def _win_attn_kernel(n_ctx_blocks, n_blocks, q_ref, kc_ref, vc_ref, kl_ref, km_ref, kr_ref,
                     vl_ref, vm_ref, vr_ref, sink_ref, o_ref):
    j = pl.program_id(1)
    is_lat = j >= n_ctx_blocks
    qb = Q_BLOCK
    n_c = kc_ref.shape[1]
    rows = A_GROUP * qb
    n_keys = n_c + 3 * qb
    far = 1 << 20
    r_idx = lax.broadcasted_iota(I32, (rows, n_keys), 0) % qb
    cw = lax.broadcasted_iota(I32, (rows, n_keys), 1) - n_c
    off_l = jnp.where(jnp.logical_and(is_lat, j > n_ctx_blocks), 0, far)
    end_m = jnp.where(is_lat, 2 * qb, qb)
    off_r = jnp.where(jnp.logical_and(is_lat, j < n_blocks - 1), 0, far)
    valid = ((cw < 0)
             | ((cw >= 0) & (cw < qb) & (cw >= r_idx + off_l))
             | ((cw >= qb) & (cw < end_m))
             | ((cw >= 2 * qb) & (cw - 2 * qb + off_r <= r_idx)))
    outs = []
    for kv in range(A_KV_HEADS):
        cols = slice(kv * HEAD_DIM, (kv + 1) * HEAD_DIM)
        k_all = jnp.concatenate([kc_ref[0, :, cols], kl_ref[0, :, cols], km_ref[0, :, cols],
                                 kr_ref[0, :, cols]], axis=0)
        v_all = jnp.concatenate([vc_ref[0, :, cols], vl_ref[0, :, cols], vm_ref[0, :, cols],
                                 vr_ref[0, :, cols]], axis=0)
        q0 = kv * A_GROUP
        q = jnp.concatenate([q_ref[0, :, (q0 + g) * HEAD_DIM:(q0 + g + 1) * HEAD_DIM]
                             for g in range(A_GROUP)], axis=0)
        sink = jnp.concatenate([jnp.broadcast_to(sink_ref[q0 + g:q0 + g + 1, 0:1], (qb, 1))
                                for g in range(A_GROUP)], axis=0)
        s = jnp.where(valid, _dot_nt(q, k_all), NEG_INF)
        m = jnp.maximum(jnp.max(s, -1, keepdims=True), sink)
        e = jnp.exp(s - m)
        denom = jnp.sum(e, -1, keepdims=True) + jnp.exp(sink - m)
        o = _dot(e.astype(BF16), v_all) * (1.0 / denom)
        outs += [o[g * qb:(g + 1) * qb] for g in range(A_GROUP)]
    for j2 in range(A_Q_HEADS // 2):
        pair = jnp.concatenate([outs[2 * j2], outs[2 * j2 + 1]], axis=1)
        o_ref[0, :, j2 * LANES:(j2 + 1) * LANES] = pair.astype(o_ref.dtype)


def _win_attn(qa, ka, va, sink, n_ctx):
    B, N, _ = qa.shape
    qb = Q_BLOCK
    nb = N // qb
    ncb = n_ctx // qb
    sink_pad = jnp.broadcast_to(sink.reshape(A_Q_HEADS, 1).astype(F32), (A_Q_HEADS, LANES))

    def left(b, j):
        return (b, jnp.clip(j - 1, ncb, nb - 1), 0)

    def mid(b, j):
        return (b, jnp.clip(j, ncb, nb - 1), 0)

    def right(b, j):
        return (b, jnp.clip(j + 1, ncb, nb - 1), 0)

    kv_blk = lambda im: pl.BlockSpec((1, qb, A_KV_WIDTH), im)
    ctx_blk = pl.BlockSpec((1, n_ctx, A_KV_WIDTH), lambda b, j: (b, 0, 0))
    return pl.pallas_call(
        functools.partial(_win_attn_kernel, ncb, nb),
        grid=(B, nb),
        in_specs=[pl.BlockSpec((1, qb, A_WIDTH), lambda b, j: (b, j, 0)),
                  ctx_blk, ctx_blk,
                  kv_blk(left), kv_blk(mid), kv_blk(right),
                  kv_blk(left), kv_blk(mid), kv_blk(right),
                  pl.BlockSpec((A_Q_HEADS, LANES), lambda b, j: (0, 0))],
        out_specs=pl.BlockSpec((1, qb, A_WIDTH), lambda b, j: (b, j, 0)),
        out_shape=jax.ShapeDtypeStruct((B, N, A_WIDTH), BF16),
        compiler_params=_cparams(("arbitrary", "arbitrary")),
    )(qa, ka, va, ka, ka, ka, va, va, va, sink_pad)


def _diff_attn_kernel(n_ctx, lam_init, q_ref, k_ref, v_ref, lv_ref, g_ref, o_ref):
    j = pl.program_id(1)
    lv = lv_ref[...]
    lam = (jnp.exp(jnp.sum(lv[0:1] * lv[1:2], -1, keepdims=True))
           - jnp.exp(jnp.sum(lv[2:3] * lv[3:4], -1, keepdims=True)) + lam_init)
    gain = g_ref[...] * (1.0 - lam_init)

    def run(n_keys):
        for hd in range(B_HEADS):
            parts = []
            for mm in range(2):
                c0 = (hd * 2 + mm) * HEAD_DIM
                q = q_ref[0, :, c0:c0 + HEAD_DIM]
                k = k_ref[0, :n_keys, c0:c0 + HEAD_DIM]
                s = _dot_nt(q, k)
                e = jnp.exp(s - jnp.max(s, -1, keepdims=True))
                parts.append((e, jnp.sum(e, -1, keepdims=True)))
            (e0, l0), (e1, l1) = parts
            v = v_ref[0, :n_keys, hd * B_V_DIM:(hd + 1) * B_V_DIM]
            o = _dot(e0.astype(BF16), v) * (1.0 / l0) - _dot(e1.astype(BF16), v) * (lam / l1)
            o = o * lax.rsqrt(jnp.mean(o * o, -1, keepdims=True) + SUBLN_EPS) * gain
            o_ref[0, :, hd * B_V_DIM:(hd + 1) * B_V_DIM] = o.astype(o_ref.dtype)

    ctx_tiles = n_ctx // q_ref.shape[1]

    @pl.when(j < ctx_tiles)
    def _():
        run(n_ctx)

    @pl.when(j >= ctx_tiles)
    def _():
        run(k_ref.shape[1])


def _diff_attn(qb, kb, vb, lam_vecs, subln_g, lam_init, n_ctx):
    B, N, _ = qb.shape
    tq = DIFF_Q_TILE
    return pl.pallas_call(
        functools.partial(_diff_attn_kernel, n_ctx, lam_init),
        grid=(B, N // tq),
        in_specs=[pl.BlockSpec((1, tq, B_WIDTH), lambda b, j: (b, j, 0)),
                  pl.BlockSpec((1, N, B_WIDTH), lambda b, j: (b, 0, 0)),
                  pl.BlockSpec((1, N, B_WIDTH), lambda b, j: (b, 0, 0)),
                  pl.BlockSpec((4, HEAD_DIM), lambda b, j: (0, 0)),
                  pl.BlockSpec((1, B_V_DIM), lambda b, j: (0, 0))],
        out_specs=pl.BlockSpec((1, tq, B_WIDTH), lambda b, j: (b, j, 0)),
        out_shape=jax.ShapeDtypeStruct((B, N, B_WIDTH), BF16),
        compiler_params=_cparams(("arbitrary", "arbitrary")),
    )(qb, kb, vb, lam_vecs.astype(F32), subln_g.reshape(1, B_V_DIM).astype(F32))


def _mix_out_kernel(n_in, *refs):
    xs = refs[:n_in]
    ws = refs[n_in:2 * n_in]
    h_ref, mod_ref, g_ref, b_ref, hn_ref, u_ref = refs[2 * n_in:]
    o = _dot(xs[0][0], ws[0][...])
    for x_ref, w_ref in zip(xs[1:], ws[1:]):
        o = o + _dot(x_ref[0], w_ref[...])
    z = DEEPNORM_ALPHA * h_ref[0] + mod_ref[0, 0, 2:3, :] * o
    hn = _layer_norm(z, g_ref[...], b_ref[...])
    hn_ref[0] = hn
    u_ref[0] = _pack_halves(hn * (1.0 + mod_ref[0, 0, 4:5, :]) + mod_ref[0, 0, 3:4, :])


def _mix_out(xs, ws, h, mod, ln_g, ln_b, n_ctx, row0):
    B, N, D = h.shape
    tm = ROW_TILE
    t0 = row0 // tm
    n_out = N - row0
    row_spec = lambda w: pl.BlockSpec((1, tm, w), lambda b, i: (b, i + t0, 0))
    out_spec = pl.BlockSpec((1, tm, D), lambda b, i: (b, i, 0))
    vec_spec = pl.BlockSpec((1, D), lambda b, i: (0, 0))
    return pl.pallas_call(
        functools.partial(_mix_out_kernel, len(xs)),
        grid=(B, n_out // tm),
        in_specs=([row_spec(x.shape[-1]) for x in xs]
                  + [pl.BlockSpec(w.shape, lambda b, i: (0, 0)) for w in ws]
                  + [row_spec(D),
                     pl.BlockSpec((1, 1, 6, D), lambda b, i: (b, jnp.minimum((i + t0) // (n_ctx // tm), 1), 0, 0)),
                     vec_spec, vec_spec]),
        out_specs=[out_spec, pl.BlockSpec((1, tm, D // 2), lambda b, i: (b, i, 0))],
        out_shape=[jax.ShapeDtypeStruct((B, n_out, D), F32), jax.ShapeDtypeStruct((B, n_out, D // 2), I32)],
        compiler_params=_cparams(("arbitrary", "arbitrary")),
    )(*xs, *[w.astype(BF16) for w in ws], h, mod, ln_g.reshape(1, D), ln_b.reshape(1, D))


def _router_kernel(u_ref, rt_ref, bias_ref, tri_ref, e_ref, gw_ref, rank_ref, cnt_ref, carry_ref):
    i = pl.program_id(0)

    @pl.when(i == 0)
    def _():
        carry_ref[...] = jnp.zeros_like(carry_ref)

    tm = u_ref.shape[0]
    per_group = N_EXPERTS // N_GROUPS
    neg = -jnp.inf
    u_lo, u_hi = _unpack_halves(u_ref[...])
    half = u_ref.shape[1]
    logits = (_dot_nt(rt_ref[:, :half], u_lo.astype(BF16))
              + _dot_nt(rt_ref[:, half:], u_hi.astype(BF16)))
    scores = jax.nn.sigmoid(logits)
    sel = scores + bias_ref[...]
    io_in = lax.broadcasted_iota(I32, (per_group, tm), 0)
    grp_rows = []
    for gi in range(N_GROUPS):
        sg = sel[gi * per_group:(gi + 1) * per_group]
        m1 = jnp.max(sg, axis=0, keepdims=True)
        i1 = jnp.min(jnp.where(sg == m1, io_in, per_group), axis=0, keepdims=True)
        m2 = jnp.max(jnp.where(io_in == i1, neg, sg), axis=0, keepdims=True)
        grp_rows.append(m1 + m2)
    grp = jnp.concatenate(grp_rows, axis=0)
    io_g = lax.broadcasted_iota(I32, grp.shape, 0)
    g_sel = jnp.zeros(grp.shape, F32)
    for _ in range(TOPK_GROUPS):
        m = jnp.max(grp, axis=0, keepdims=True)
        hit = io_g == jnp.min(jnp.where(grp == m, io_g, N_GROUPS), axis=0, keepdims=True)
        g_sel = jnp.where(hit, 1.0, g_sel)
        grp = jnp.where(hit, neg, grp)
    selm = jnp.concatenate(
        [jnp.where(g_sel[gi:gi + 1] > 0.5, sel[gi * per_group:(gi + 1) * per_group], NEG_INF)
         for gi in range(N_GROUPS)], axis=0)
    io_e = lax.broadcasted_iota(I32, selm.shape, 0)
    chosen_f = jnp.zeros(selm.shape, F32)
    idx, gws = [], []
    for _ in range(TOP_K):
        m = jnp.max(selm, axis=0, keepdims=True)
        ik = jnp.min(jnp.where(selm == m, io_e, N_EXPERTS), axis=0, keepdims=True)
        hit = io_e == ik
        idx.append(ik)
        gws.append(jnp.sum(jnp.where(hit, scores, 0.0), axis=0, keepdims=True))
        chosen_f = jnp.where(hit, 1.0, chosen_f)
        selm = jnp.where(hit, neg, selm)
    gw = jnp.concatenate(gws, axis=0)
    gw_ref[...] = gw / jnp.sum(gw, axis=0, keepdims=True) * ROUTED_SCALE
    e_ref[...] = jnp.concatenate(idx, axis=0)
    before = _dot(chosen_f.astype(BF16), tri_ref[...]) + carry_ref[...]
    ranks = [jnp.sum(jnp.where(io_e == ik, before, 0.0), axis=0, keepdims=True) for ik in idx]
    rank_ref[...] = jnp.concatenate(ranks, axis=0).astype(I32)
    carry_ref[...] = carry_ref[...] + jnp.sum(chosen_f, axis=1, keepdims=True)
    cnt_ref[...] = carry_ref[...].astype(I32)


def _router(u, router, bias):
    T = u.shape[0]
    D = router.shape[0]
    tm = ROW_TILE
    tri = jnp.asarray(np.triu(np.ones((tm, tm), np.float32), 1), BF16)
    tok_spec = pl.BlockSpec((TOP_K, tm), lambda i: (0, i))
    return pl.pallas_call(
        _router_kernel,
        grid=(T // tm,),
        in_specs=[pl.BlockSpec((tm, D // 2), lambda i: (i, 0)),
                  pl.BlockSpec((N_EXPERTS, D), lambda i: (0, 0)),
                  pl.BlockSpec((N_EXPERTS, 1), lambda i: (0, 0)),
                  pl.BlockSpec((tm, tm), lambda i: (0, 0))],
        out_specs=[tok_spec, tok_spec, tok_spec, pl.BlockSpec((N_EXPERTS, 1), lambda i: (0, 0))],
        out_shape=[jax.ShapeDtypeStruct((TOP_K, T), I32), jax.ShapeDtypeStruct((TOP_K, T), F32),
                   jax.ShapeDtypeStruct((TOP_K, T), I32), jax.ShapeDtypeStruct((N_EXPERTS, 1), I32)],
        scratch_shapes=[pltpu.VMEM((N_EXPERTS, 1), F32)],
        compiler_params=_cparams(("arbitrary",)),
    )(u, router.T.astype(BF16), bias.reshape(N_EXPERTS, 1).astype(F32), tri)


def _sc_mesh():
    return plsc.VectorSubcoreMesh(core_axis_name="c", subcore_axis_name="s")


def _sc_scatter_rows(x, dest, n_rows):
    T, W = x.shape
    K = dest.shape[0]
    win = SC_WINDOW

    @functools.partial(pl.kernel, out_type=jax.ShapeDtypeStruct((n_rows, W), x.dtype), mesh=_sc_mesh(),
                       scratch_types=[])
    def scatter(x_hbm, i_hbm, o_hbm):
        def body(x_vmem, i_vmem):
            for k in range(K):
                pltpu.sync_copy(x_vmem, o_hbm.at[i_vmem.at[k]])

        pltpu.emit_pipeline(
            body,
            grid=(T // win,),
            in_specs=[pl.BlockSpec((win, W), lambda j: (j, 0), pipeline_mode=pl.Buffered(1)),
                      pl.BlockSpec((K, win), lambda j: (0, j))],
            out_specs=[],
            core_axis_name=("c", "s"),
            dimension_semantics=(pltpu.PARALLEL,),
        )(x_hbm, i_hbm)

    return scatter(x, dest)


def _sc_gather_rows(y, dest):
    K, T = dest.shape
    W = y.shape[1]
    win = SC_WINDOW

    @functools.partial(pl.kernel, out_type=jax.ShapeDtypeStruct((K * T, W), y.dtype), mesh=_sc_mesh(),
                       scratch_types=[])
    def gather(y_hbm, i_hbm, o_hbm):
        def body(i_vmem, o_vmem):
            pltpu.sync_copy(y_hbm.at[i_vmem.at[0]], o_vmem)

        pltpu.emit_pipeline(
            body,
            grid=(K * T // win,),
            in_specs=[pl.BlockSpec((1, win), lambda j: (0, j))],
            out_specs=[pl.BlockSpec((win, W), lambda j: (j, 0), pipeline_mode=pl.Buffered(1))],
            core_axis_name=("c", "s"),
            dimension_semantics=(pltpu.PARALLEL,),
        )(i_hbm, o_hbm)

    return gather(y, dest.reshape(1, K * T)).reshape(K, T, W)


def _expert_kernel(first_ref, count_ref, x_hbm, wi_ref, wo_ref, y_hbm, wi_b, wo_b, xbuf, ybuf, sem_in, sem_out):
    e = pl.program_id(0)
    n_in, blk = xbuf.shape[0], xbuf.shape[1]
    n_out = ybuf.shape[0]
    g0, g1 = first_ref[e], first_ref[e + 1]
    g_end = first_ref[pl.num_programs(0)]

    def x_copy(g):
        return pltpu.make_async_copy(x_hbm.at[pl.ds(g * blk, blk)], xbuf.at[g % n_in], sem_in.at[g % n_in])

    def y_copy(g):
        return pltpu.make_async_copy(ybuf.at[g % n_out], y_hbm.at[pl.ds(g * blk, blk)], sem_out.at[g % n_out])

    for ahead in range(n_in - 1):
        @pl.when(jnp.logical_and(e == 0, g_end > ahead))
        def _():
            x_copy(ahead).start()

    @pl.when(g1 > g0)
    def _():
        wi_b[...] = wi_ref[0, 0].astype(BF16)
        wo_b[...] = wo_ref[0, 0].astype(BF16)

    ff = wo_b.shape[0]

    def block(g, carry):
        x_copy(g).wait()

        @pl.when(g + n_in - 1 < g_end)
        def _():
            x_copy(g + n_in - 1).start()

        @pl.when(g >= n_out)
        def _():
            y_copy(g - n_out).wait()

        row = lax.broadcasted_iota(I32, (blk, xbuf.shape[2]), 0) + (g - g0) * blk
        hcat = _dot_halves(jnp.where(row < count_ref[e], xbuf[g % n_in], 0), wi_b)
        act = (_silu(hcat[:, :ff]) * hcat[:, ff:]).astype(BF16)
        ybuf[g % n_out] = _pack_halves(_dot(act, wo_b[...]))
        y_copy(g).start()
        return carry

    lax.fori_loop(g0, g1, block, 0)

    @pl.when(e == pl.num_programs(0) - 1)
    def _():
        for back in range(n_out, 0, -1):
            @pl.when(g_end >= back)
            def _():
                y_copy(g_end - back).wait()


def _experts(xs, first_block, counts, w_in, w_out, layer):
    P, half = xs.shape
    n_exp, D, ff2 = w_in.shape[-3:]
    return pl.pallas_call(
        _expert_kernel,
        grid_spec=pltpu.PrefetchScalarGridSpec(
            num_scalar_prefetch=2,
            grid=(n_exp,),
            in_specs=[pl.BlockSpec(memory_space=pl.ANY),
                      pl.BlockSpec((1, 1, D, ff2), lambda e, fb, cnt: (layer, e, 0, 0)),
                      pl.BlockSpec((1, 1, ff2 // 2, D), lambda e, fb, cnt: (layer, e, 0, 0))],
            out_specs=pl.BlockSpec(memory_space=pl.ANY),
            scratch_shapes=[pltpu.VMEM((D, ff2), BF16), pltpu.VMEM((ff2 // 2, D), BF16),
                            pltpu.VMEM((EXPERT_IN_SLOTS, MOE_BLOCK, half), I32),
                            pltpu.VMEM((EXPERT_OUT_SLOTS, MOE_BLOCK, half), I32),
                            pltpu.SemaphoreType.DMA((EXPERT_IN_SLOTS,)),
                            pltpu.SemaphoreType.DMA((EXPERT_OUT_SLOTS,))]),
        out_shape=jax.ShapeDtypeStruct((P, half), I32),
        compiler_params=_cparams(("arbitrary",)),
    )(first_block, counts.astype(I32), xs, w_in, w_out)


def _combine_kernel(yg_ref, gw_ref, u_ref, wsi_ref, wso_ref, h_ref, mod_ref, g_ref, b_ref, o_ref):
    ff = wso_ref.shape[0]
    hcat = _dot_halves(u_ref[...], wsi_ref)
    shared = _dot((_silu(hcat[:, :ff]) * hcat[:, ff:]).astype(BF16), wso_ref[...])
    lo, hi = None, None
    for k in range(TOP_K):
        y_lo, y_hi = _unpack_halves(yg_ref[k])
        gk = gw_ref[:, k:k + 1]
        lo = y_lo * gk if lo is None else lo + y_lo * gk
        hi = y_hi * gk if hi is None else hi + y_hi * gk
    routed = jnp.concatenate([lo, hi], axis=1)
    z = DEEPNORM_ALPHA * h_ref[...] + mod_ref[0] * (routed + shared)
    o_ref[...] = _layer_norm(z, g_ref[...], b_ref[...])


def _combine(yg, gw_t, u, ws_in, ws_out, h, gate, gate_index, ln_g, ln_b):
    T, D = h.shape
    tm = ROW_TILE
    vec_spec = pl.BlockSpec((1, D), lambda i: (0, 0))
    row_spec = pl.BlockSpec((tm, D), lambda i: (i, 0))
    packed_spec = pl.BlockSpec((tm, D // 2), lambda i: (i, 0))
    return pl.pallas_call(
        _combine_kernel,
        grid=(T // tm,),
        in_specs=[pl.BlockSpec((TOP_K, tm, D // 2), lambda i: (0, i, 0)),
                  pl.BlockSpec((tm, TOP_K), lambda i: (i, 0)),
                  packed_spec,
                  pl.BlockSpec(ws_in.shape, lambda i: (0, 0)),
                  pl.BlockSpec(ws_out.shape, lambda i: (0, 0)),
                  row_spec,
                  pl.BlockSpec((1,) + gate.shape[1:], lambda i: (gate_index(i), 0, 0)),
                  vec_spec, vec_spec],
        out_specs=row_spec,
        out_shape=jax.ShapeDtypeStruct((T, D), F32),
        compiler_params=_cparams(("arbitrary",)),
    )(yg, gw_t, u, ws_in.astype(BF16), ws_out.astype(BF16), h, gate,
      ln_g.reshape(1, D), ln_b.reshape(1, D))


def _slots_kernel(e_ref, rank_ref, start_ref, dest_ref):
    io_e = lax.broadcasted_iota(I32, (N_EXPERTS, e_ref.shape[1]), 0)
    rows = [jnp.sum(jnp.where(io_e == e_ref[k:k + 1, :], start_ref[...], 0), axis=0, keepdims=True)
            for k in range(TOP_K)]
    dest_ref[...] = jnp.concatenate(rows, axis=0) + rank_ref[...]


def _slots(eidx, rank, pstart):
    T = eidx.shape[1]
    tm = ROW_TILE
    tok_spec = pl.BlockSpec((TOP_K, tm), lambda i: (0, i))
    return pl.pallas_call(
        _slots_kernel,
        grid=(T // tm,),
        in_specs=[tok_spec, tok_spec, pl.BlockSpec((N_EXPERTS, 1), lambda i: (0, 0))],
        out_specs=tok_spec,
        out_shape=jax.ShapeDtypeStruct((TOP_K, T), I32),
        compiler_params=_cparams(("arbitrary",)),
    )(eidx, rank, pstart.reshape(N_EXPERTS, 1))


def _moe_layer(u, h, gate, gate_index, router, bias, w_in, w_out, ws_in, ws_out, ln_g, ln_b, layer):
    T = u.shape[0]
    eidx, gw, rank, counts = _router(u, router, bias)
    counts = counts[:, 0]
    padded = (counts + MOE_BLOCK - 1) // MOE_BLOCK * MOE_BLOCK
    pend = jnp.cumsum(padded)
    pstart = (pend - padded).astype(I32)
    dest = _slots(eidx, rank, pstart)
    n_blocks = -(-(T * TOP_K + N_EXPERTS * (MOE_BLOCK - 1)) // MOE_BLOCK)
    first_block = jnp.concatenate([jnp.zeros((1,), I32), (pend // MOE_BLOCK).astype(I32)])
    xs = _sc_scatter_rows(u, dest, n_blocks * MOE_BLOCK)
    y = _experts(xs, first_block, counts, w_in, w_out, layer)
    return _combine(_sc_gather_rows(y, dest), gw.T, u, ws_in, ws_out, h, gate, gate_index, ln_g, ln_b)


def _seg_ones(width=LANES):
    idx = np.arange(width) // HEAD_DIM
    return jnp.asarray((idx[:, None] == idx[None, :]).astype(np.float32), BF16)


def _head_sum(x, ones_ref):
    outs = []
    for j in range(x.shape[1] // LANES):
        xc = x[:, j * LANES:(j + 1) * LANES]
        hi = xc.astype(BF16)
        lo = (xc - hi.astype(F32)).astype(BF16)
        outs.append(_dot(hi, ones_ref[...]) + _dot(lo, ones_ref[...]))
    return jnp.concatenate(outs, axis=1)


def _rwkv_proj_kernel(seg_tiles, h_ref, hp_ref, hn_ref, mod_ref, mu_ref, wrkv_ref, g1_ref, g2_ref, d1_ref, d2_ref,
                      d0_ref, i1_ref, i2_ref, i0_ref, kk_ref, ka_ref, rk_ref, ones_ref,
                      r_ref, v_ref, a_ref, g_ref, bonus_ref, w_ref, k_ref, b_ref):
    i = pl.program_id(0)
    nb = hp_ref.shape[0]
    shift, scale = mod_ref[0, 0], mod_ref[0, 1]
    u = h_ref[...] * (1.0 + scale) + shift
    starts = jnp.logical_or(i == 0, i == seg_tiles)
    ends = jnp.logical_or(i == seg_tiles - 1, i == pl.num_programs(0) - 1)
    u_before = (hp_ref[...] * (1.0 + scale[:nb]) + shift[:nb]) * jnp.where(starts, 0.0, 1.0)
    u_after = (hn_ref[...] * (1.0 + scale[:nb]) + shift[:nb]) * jnp.where(ends, 0.0, 1.0)
    dx = 0.5 * (jnp.concatenate([u_before, u[:-nb]], axis=0) + jnp.concatenate([u[nb:], u_after], axis=0)) - u
    mix = lambda m: (u + dx * mu_ref[m:m + 1, :])
    xr, xw, xk, xv, xa, xg = [mix(m) for m in range(6)]
    r = _dot(xr.astype(BF16), wrkv_ref[0])
    k = _dot(xk.astype(BF16), wrkv_ref[1])
    v = _dot(xv.astype(BF16), wrkv_ref[2])
    g = _dot(jax.nn.sigmoid(_dot(xg.astype(BF16), g1_ref[...])).astype(BF16), g2_ref[...])
    kk = k * kk_ref[...]
    kk = kk * lax.rsqrt(jnp.maximum(_head_sum(kk * kk, ones_ref), 1e-24))
    r_ref[...] = r
    v_ref[...] = v
    a_ref[...] = -kk
    g_ref[...] = g
    k_sum = None
    xw_b = xw.astype(BF16)
    xa_b = xa.astype(BF16)
    for d in range(2):
        lw = d0_ref[d:d + 1, :] + _dot(jnp.tanh(_dot(xw_b, d1_ref[d])).astype(BF16), d2_ref[d])
        softplus = jnp.maximum(-lw, 0.0) + jnp.log(1.0 + jnp.exp(-jnp.abs(lw)))
        logw = -softplus - 0.5
        w_ref[d] = jnp.exp(-jnp.exp(logw))
        eta = jax.nn.sigmoid(i0_ref[d:d + 1, :] + _dot(_dot(xa_b, i1_ref[d]).astype(BF16), i2_ref[d]))
        k_d = k * (1.0 + (eta - 1.0) * ka_ref[...])
        k_ref[d] = k_d
        b_ref[d] = kk * eta
        k_sum = k_d if k_sum is None else k_sum + k_d
    bonus_ref[...] = _head_sum(r * k_sum * rk_ref[...], ones_ref) * v


def _rwkv_proj(h, mod_rows, batch, n_ctx, p):
    T, D = h.shape
    tm = PROJ_TILE
    per_tile = tm // batch
    seg_tiles = n_ctx // per_tile
    n_steps = T // batch
    row = pl.BlockSpec((tm, D), lambda i: (i, 0))
    before = pl.BlockSpec((batch, D), lambda i: (jnp.maximum(i * per_tile - 1, 0), 0))
    after = pl.BlockSpec((batch, D), lambda i: (jnp.minimum((i + 1) * per_tile, n_steps - 1), 0))
    mod_spec = pl.BlockSpec((1, 2, tm, D), lambda i: (jnp.minimum(i // seg_tiles, 1), 0, 0, 0))
    row2 = pl.BlockSpec((2, tm, D), lambda i: (0, i, 0))
    full = lambda a: pl.BlockSpec(a.shape, lambda i: (0,) * a.ndim)
    bf = lambda a: a.astype(BF16)
    consts = [p['mu'], bf(p['w_rkv']), bf(p['gate1']), bf(p['gate2']), bf(p['dec1']), bf(p['dec2']), p['dec0'],
              bf(p['icl1']), bf(p['icl2']), p['icl0'], p['k_k'].reshape(1, D), p['k_a'].reshape(1, D),
              p['r_k'].reshape(1, D), _seg_ones()]
    one = jax.ShapeDtypeStruct((T, D), F32)
    two = jax.ShapeDtypeStruct((2, T, D), F32)
    return pl.pallas_call(
        functools.partial(_rwkv_proj_kernel, seg_tiles),
        grid=(T // tm,),
        in_specs=[row, before, after, mod_spec] + [full(a) for a in consts],
        out_specs=[row, row, row, row, row, row2, row2, row2],
        out_shape=[one, one, one, one, one, two, two, two],
        compiler_params=_cparams(("arbitrary",)),
    )(h, h, h, mod_rows, *consts)


def _scan_kernel(r_ref, w_ref, k_ref, v_ref, a_ref, b_ref, ones_ref, hsel_ref,
                 y_ref, s_ref, vt_ref):
    d = pl.program_id(0)
    c = pl.program_id(2)
    tc, nb = r_ref.shape[0], r_ref.shape[1]
    tw = SCAN_TILE
    n_wide = r_ref.shape[2] // tw
    heads = tw // HEAD_DIM
    assert heads * tc == tw

    @pl.when(c == 0)
    def _():
        s_ref[...] = jnp.zeros_like(s_ref)

    for bb in range(nb):
        for q in range(n_wide):
            vt = v_ref[:, bb, q * tw:(q + 1) * tw].T
            vt_ref[bb * n_wide + q] = jnp.concatenate(
                [vt[h * HEAD_DIM:(h + 1) * HEAD_DIM] for h in range(heads)], axis=1)

    head_base = (lax.broadcasted_iota(I32, (HEAD_DIM, LANES), 1) // HEAD_DIM) * tc
    tiles = [(bb, q) for bb in range(nb) for q in range(n_wide)]
    groups = [tiles[i:i + SCAN_GROUP] for i in range(0, len(tiles), SCAN_GROUP)]

    def stacked(grp, get, dtype=F32):
        def wide(bb, q):
            return jnp.concatenate(
                [jnp.broadcast_to(get(bb, slice(q * tw + hf * LANES, q * tw + (hf + 1) * LANES)).astype(dtype),
                                  (HEAD_DIM, LANES)) for hf in range(tw // LANES)], axis=1)
        return jnp.concatenate([wide(bb, q) for bb, q in grp], axis=0)

    def load_state(grp):
        return jnp.concatenate([s_ref[bb * n_wide + q] for bb, q in grp], axis=0)

    def emit_y(grp, st_b, t_y):
        r_rows = stacked(grp, lambda bb, cols: r_ref[t_y, bb:bb + 1, cols], BF16)
        yh = _dot_nt(hsel_ref[...], st_b * r_rows)
        first = tiles.index(grp[0])
        y_ref[0, t_y, :, first * HEAD_DIM:(first + len(grp)) * HEAD_DIM] = yh[:heads]

    def step(s_i, carry):
        t = jnp.where(d == 0, s_i, tc - 1 - s_i)
        t_prev = jnp.where(s_i == 0, t, jnp.where(d == 0, t - 1, t + 1))
        pick = head_base + t
        for grp in groups:
            one = lambda ref: stacked(grp, lambda bb, cols: ref[t, bb:bb + 1, cols])
            two = lambda ref: stacked(grp, lambda bb, cols: ref[0, t, bb:bb + 1, cols])
            st = load_state(grp)
            st_b = st.astype(BF16)
            a_rows = stacked(grp, lambda bb, cols: a_ref[t, bb:bb + 1, cols], BF16)
            sa = _dot(st_b * a_rows, ones_ref[...])
            emit_y(grp, st_b, t_prev)
            vcol = jnp.concatenate(
                [jnp.concatenate([jnp.take_along_axis(vt_ref[bb * n_wide + q, :, hf * LANES:(hf + 1) * LANES],
                                                      pick, axis=1) for hf in range(tw // LANES)], axis=1)
                 for bb, q in grp], axis=0)
            st = st * two(w_ref) + sa * two(b_ref) + vcol * two(k_ref)
            for j, (bb, q) in enumerate(grp):
                s_ref[bb * n_wide + q] = st[j * HEAD_DIM:(j + 1) * HEAD_DIM]
        return carry

    lax.fori_loop(0, tc, step, 0, unroll=SCAN_UNROLL)
    t_last = jnp.where(d == 0, tc - 1, 0)
    for grp in groups:
        emit_y(grp, load_state(grp).astype(BF16), t_last)


def _wkv_scan(r, w, k, v, a, b, n_ctx):
    N, B, D = r.shape
    tc = SCAN_CHUNK
    wc = SCAN_COLS
    n_wide = wc // SCAN_TILE
    nc = N // tc
    ncc = n_ctx // tc

    def chunk(d, c):
        rev = jnp.where(c < ncc, ncc - 1 - c, nc - 1 - (c - ncc))
        return jnp.where(d == 0, c, rev)

    one = pl.BlockSpec((tc, B, wc), lambda d, g, c: (chunk(d, c), 0, g))
    two = pl.BlockSpec((1, tc, B, wc), lambda d, g, c: (d, chunk(d, c), 0, g))
    seg = np.arange(SCAN_TILE) // HEAD_DIM
    hsel = np.zeros((8, SCAN_TILE), np.float32)
    for hh in range(SCAN_TILE // HEAD_DIM):
        hsel[hh, seg == hh] = 1.0
    const = lambda a_: pl.BlockSpec(a_.shape, lambda d, g, c: (0, 0))
    consts = [_seg_ones(SCAN_TILE), jnp.asarray(hsel, BF16)]
    heads = SCAN_TILE // HEAD_DIM
    ncg = D // wc
    y = pl.pallas_call(
        _scan_kernel,
        grid=(2, ncg, nc),
        in_specs=[one, two, two, one, one, two] + [const(a_) for a_ in consts],
        out_specs=pl.BlockSpec((1, tc, heads, B * n_wide * HEAD_DIM), lambda d, g, c: (d, chunk(d, c), 0, g)),
        out_shape=jax.ShapeDtypeStruct((2, N, heads, ncg * B * n_wide * HEAD_DIM), F32),
        scratch_shapes=[pltpu.VMEM((B * n_wide, HEAD_DIM, SCAN_TILE), F32),
                        pltpu.VMEM((B * n_wide, HEAD_DIM, SCAN_TILE), F32)],
        compiler_params=_cparams(("arbitrary", "arbitrary", "arbitrary")),
    )(r, w, k, v, a, b, *consts)
    y = y.reshape(2, N, heads, ncg, B, n_wide * HEAD_DIM)
    return jnp.transpose(y, (0, 1, 4, 2, 3, 5)).reshape(2, N, B, D)


def _scan_head_order(d):
    heads = SCAN_TILE // HEAD_DIM
    n_wide = SCAN_COLS // SCAN_TILE
    ncg = d // SCAN_COLS
    order = []
    for g in range(ncg):
        for q in range(n_wide):
            for h in range(heads):
                order.append((h * ncg + g) * n_wide + q)
    return tuple(order)


def _rwkv_out_kernel(head_order, y0_ref, y1_ref, bonus_ref, g_ref, lnx_ref, ones_ref, w_ref, h_ref, mod_ref,
                     lg_ref, lb_ref, hn_ref, u_ref):
    y_in = y0_ref[0] + y1_ref[0]
    y = jnp.concatenate([y_in[:, p * HEAD_DIM:(p + 1) * HEAD_DIM] for p in head_order], axis=1)
    ym = _head_sum(y, ones_ref) * (1.0 / HEAD_DIM)
    yc = y - ym
    yv = _head_sum(yc * yc, ones_ref) * (1.0 / HEAD_DIM)
    yn = yc * lax.rsqrt(yv + LNX_EPS) * lnx_ref[0:1, :] + lnx_ref[1:2, :]
    x = ((yn + bonus_ref[...]) * g_ref[...]).astype(BF16)
    o = _dot(x, w_ref[...])
    z = DEEPNORM_ALPHA * h_ref[...] + mod_ref[0] * o
    hn = _layer_norm(z, lg_ref[...], lb_ref[...])
    hn_ref[...] = hn
    u_ref[...] = _pack_halves(hn * (1.0 + mod_ref[2]) + mod_ref[1])


def _rwkv_out(y, bonus, g, lnx, w_out, h, mod_rows, ln_g, ln_b, row0):
    T, D = h.shape
    tm = ROW_TILE
    t0 = row0 // tm
    off = pl.BlockSpec((tm, D), lambda i: (i + t0, 0))
    out = pl.BlockSpec((tm, D), lambda i: (i, 0))
    full = lambda a: pl.BlockSpec(a.shape, lambda i: (0,) * a.ndim)
    vec = pl.BlockSpec((1, D), lambda i: (0, 0))
    ones = _seg_ones()
    w_b = w_out.astype(BF16)
    return pl.pallas_call(
        functools.partial(_rwkv_out_kernel, _scan_head_order(D)),
        grid=((T - row0) // tm,),
        in_specs=[pl.BlockSpec((1, tm, D), lambda i: (0, i + t0, 0)),
                  pl.BlockSpec((1, tm, D), lambda i: (1, i + t0, 0)),
                  off, off, full(lnx), full(ones), full(w_b), off, full(mod_rows), vec, vec],
        out_specs=[out, pl.BlockSpec((tm, D // 2), lambda i: (i, 0))],
        out_shape=[jax.ShapeDtypeStruct((T - row0, D), F32), jax.ShapeDtypeStruct((T - row0, D // 2), I32)],
        compiler_params=_cparams(("arbitrary",)),
    )(y, y, bonus, g, lnx, ones, w_b, h, mod_rows, ln_g.reshape(1, D), ln_b.reshape(1, D))


def kernel(x, c, ctx, c_ctx, ada_w, ada_b, post_ln_g, post_ln_b, att_w_in, att_w_out, att_sink, diff_lambda_vecs, diff_subln_g, rk_mu, rk_w_rkv, rk_w_out, rk_decay0, rk_decay1, rk_decay2, rk_iclr0, rk_iclr1, rk_iclr2, rk_gate1, rk_gate2, rk_k_k, rk_k_a, rk_r_k, rk_lnx, moe_router, moe_bias, moe_w_in, moe_w_out, moe_ws_in, moe_ws_out):
    B, S, D = x.shape
    L = ctx.shape[1]
    N = L + S
    tm = ROW_TILE
    assert L % tm == 0 and S % tm == 0 and L % SCAN_CHUNK == 0 and S % SCAN_CHUNK == 0
    assert tm % B == 0 and PROJ_TILE % B == 0 and L % (PROJ_TILE // B) == 0 and D % SCAN_COLS == 0

    rows = -(-(B + 1) // 8) * 8
    cvec = jnp.concatenate([c, c_ctx[None, :], jnp.zeros((rows - B - 1, D), F32)], axis=0)
    mods = [_mod_table(_ada_mod(cvec, ada_w[i], ada_b[i]), B, D) for i in range(DEPTH)]

    h0 = jnp.concatenate([ctx, x], axis=1)
    lam_init = 0.8 - 0.6 * math.exp(-0.3 * 0)
    qa, ka, va, qb, kb, vb = _attn_inproj(h0, mods[0], att_w_in[0], L)
    oa = _win_attn(qa, ka, va, att_sink[0], L)
    ob = _diff_attn(qb, kb, vb, diff_lambda_vecs[0], diff_subln_g[0], lam_init, L)
    h1, u1 = _mix_out([oa, ob], [att_w_out[0][:A_WIDTH], att_w_out[0][A_WIDTH:]], h0, mods[0],
                      post_ln_g[0, 0], post_ln_b[0, 0], L, 0)
    tiles_b, tiles_c = N // tm, L // tm
    gate0 = mods[0][:, :, 5].reshape(B * 2, 1, D)
    gate0_index = lambda i: (i // tiles_b) * 2 + jnp.minimum((i % tiles_b) // tiles_c, 1)
    h2 = _moe_layer(u1.reshape(B * N, D // 2), h1.reshape(B * N, D), gate0, gate0_index, moe_router[0], moe_bias[0],
                    moe_w_in, moe_w_out, moe_ws_in[0], moe_ws_out[0],
                    post_ln_g[0, 1], post_ln_b[0, 1], 0).reshape(B, N, D)

    m_ctx, m_lat = mods[1][:, 0], mods[1][:, 1]
    h2_t = jnp.swapaxes(h2, 0, 1).reshape(N * B, D)
    rows_of = lambda m, j, n: jnp.tile(m[:, j], (n // B, 1))
    proj_mod = jnp.stack([jnp.stack([rows_of(m, 0, PROJ_TILE), rows_of(m, 1, PROJ_TILE)]) for m in (m_ctx, m_lat)])
    params = dict(mu=rk_mu[0], w_rkv=rk_w_rkv[0], gate1=rk_gate1[0], gate2=rk_gate2[0],
                  dec0=rk_decay0[0], dec1=rk_decay1[0], dec2=rk_decay2[0],
                  icl0=rk_iclr0[0], icl1=rk_iclr1[0], icl2=rk_iclr2[0],
                  k_k=rk_k_k[0], k_a=rk_k_a[0], r_k=rk_r_k[0])
    r, v, a, g, bonus, w2, k2, b2 = _rwkv_proj(h2_t, proj_mod, B, L, params)
    tmaj = lambda t: t.reshape(t.shape[:-2] + (N, B, D))
    y = _wkv_scan(tmaj(r), tmaj(w2), tmaj(k2), tmaj(v), tmaj(a), tmaj(b2), L)
    lat_rows = lambda j: rows_of(m_lat, j, tm)
    h3, u3 = _rwkv_out(y.reshape(2, N * B, D), bonus, g, rk_lnx[0], rk_w_out[0], h2_t,
                       jnp.stack([lat_rows(2), lat_rows(3), lat_rows(4)]),
                       post_ln_g[1, 0], post_ln_b[1, 0], L * B)
    out = _moe_layer(u3, h3, lat_rows(5)[None], lambda i: 0, moe_router[1], moe_bias[1],
                     moe_w_in, moe_w_out, moe_ws_in[1], moe_ws_out[1],
                     post_ln_g[1, 1], post_ln_b[1, 1], 1)
    return jnp.swapaxes(out.reshape(S, B, D), 0, 1)
```

```python
import functools
import math

import numpy as np
import jax
import jax.numpy as jnp
from jax import lax
from jax.experimental import pallas as pl
from jax.experimental.pallas import tpu as pltpu
from jax.experimental.pallas import tpu_sc as plsc

F32 = jnp.float32
BF16 = jnp.bfloat16
I32 = jnp.int32

HEAD_DIM = 64
GRID_W = 64
ROPE_AXIS_DIM = HEAD_DIM // 2
ROPE_THETA = 10000.0
Q_BLOCK = 128
A_Q_HEADS = 8
A_KV_HEADS = 2
A_GROUP = A_Q_HEADS // A_KV_HEADS
A_WIDTH = A_Q_HEADS * HEAD_DIM
A_KV_WIDTH = A_KV_HEADS * HEAD_DIM
B_HEADS = 4
B_V_DIM = 2 * HEAD_DIM
B_WIDTH = B_HEADS * B_V_DIM
LNX_EPS = 64e-5
N_EXPERTS = 256
TOP_K = 8
N_GROUPS = 8
TOPK_GROUPS = 4
ROUTED_SCALE = 2.5
MOE_BLOCK = 256
EXPERT_IN_SLOTS = 4
EXPERT_OUT_SLOTS = 2
LN_EPS = 1e-5
SUBLN_EPS = 1e-5
NEG_INF = -1e30
DEPTH = 2
DEEPNORM_ALPHA = (2 * DEPTH) ** 0.25

LANES = 128
ADA_COLS = 768
ROW_TILE = 256
PROJ_TILE = 256
SC_WINDOW = 128
SCAN_CHUNK = 64
SCAN_COLS = 1024
SCAN_TILE = 256
SCAN_GROUP = 8
SCAN_UNROLL = 8
VMEM_LIMIT = 56 * 1024 * 1024


def _cparams(sem):
    return pltpu.CompilerParams(dimension_semantics=sem, vmem_limit_bytes=VMEM_LIMIT)


def _silu(x):
    return x * jax.nn.sigmoid(x)


def _layer_norm(z, g, b):
    mu = jnp.mean(z, -1, keepdims=True)
    zc = z - mu
    var = jnp.mean(zc * zc, -1, keepdims=True)
    return zc * lax.rsqrt(var + LN_EPS) * g + b


def _dot(a, b):
    return jnp.dot(a, b, preferred_element_type=F32)


def _dot_nt(a, b):
    return lax.dot_general(a, b, (((1,), (1,)), ((), ())), preferred_element_type=F32)


def _pack_halves(x):
    half = x.shape[1] // 2
    bits = lambda v: lax.bitcast_convert_type(v.astype(BF16).astype(F32), I32)
    return lax.shift_right_logical(bits(x[:, :half]), 16) | bits(x[:, half:])


def _unpack_halves(p):
    lo = lax.bitcast_convert_type(lax.shift_left(p, 16), F32)
    hi = lax.bitcast_convert_type(p & jnp.int32(-65536), F32)
    return lo, hi


def _dot_halves(p, w_ref_or_array):
    lo, hi = _unpack_halves(p)
    half = p.shape[1]
    return _dot(lo.astype(BF16), w_ref_or_array[:half]) + _dot(hi.astype(BF16), w_ref_or_array[half:])


def _ada_kernel(c_ref, w_ref, b_ref, o_ref):
    c = c_ref[...]
    o_ref[...] = _dot(_silu(c).astype(BF16), w_ref[...].astype(BF16)) + b_ref[...]


def _ada_mod(cvec, w, bias):
    R, D = cvec.shape
    n_out = w.shape[1]
    tn = ADA_COLS
    return pl.pallas_call(
        _ada_kernel,
        grid=(n_out // tn,),
        in_specs=[pl.BlockSpec((R, D), lambda j: (0, 0)),
                  pl.BlockSpec((D, tn), lambda j: (0, j)),
                  pl.BlockSpec((1, tn), lambda j: (0, j))],
        out_specs=pl.BlockSpec((R, tn), lambda j: (0, j)),
        out_shape=jax.ShapeDtypeStruct((R, n_out), F32),
        compiler_params=_cparams(("arbitrary",)),
    )(cvec, w, bias.reshape(1, n_out))


def _mod_table(m, batch, d):
    m_lat = m[:batch].reshape(batch, 6, d)
    m_ctx = jnp.broadcast_to(m[batch].reshape(1, 6, d), (batch, 6, d))
    return jnp.stack([m_ctx, m_lat], axis=1)


def _mod_spec(d, ctx_tiles):
    return pl.BlockSpec((1, 1, 6, d), lambda b, i: (b, jnp.minimum(i // ctx_tiles, 1), 0, 0))


def _rope_tables(n_ctx, n_lat):
    rows = n_lat // GRID_W
    row = np.repeat(np.arange(rows), GRID_W).astype(np.float32)
    col = np.tile(np.arange(GRID_W), rows).astype(np.float32)
    inv = (ROPE_THETA ** (-np.arange(0, ROPE_AXIS_DIM, 2, dtype=np.float32) / ROPE_AXIS_DIM)).astype(np.float32)
    ar = row[:, None] * inv
    ac = col[:, None] * inv
    ang = np.concatenate([ar, ar, ac, ac], -1)
    cos = np.cos(ang).astype(np.float32)
    sin = np.sin(ang).astype(np.float32)
    lower = (np.arange(HEAD_DIM) % ROPE_AXIS_DIM) < (ROPE_AXIS_DIM // 2)
    sin_up = np.where(lower[None, :], -sin, 0.0)
    sin_dn = np.where(lower[None, :], 0.0, sin)

    def full(t, ctx_fill):
        t = np.concatenate([np.full((n_ctx, HEAD_DIM), ctx_fill, np.float32), t], 0)
        return jnp.asarray(np.tile(t, (1, LANES // HEAD_DIM)))

    return full(cos, 1.0), full(sin_up, 0.0), full(sin_dn, 0.0)


def _inproj_kernel(h_ref, mod_ref, w_ref, cos_ref, su_ref, sd_ref,
                   qa_ref, ka_ref, va_ref, qb_ref, kb_ref, vb_ref):
    h = h_ref[0]
    shift = mod_ref[0, 0, 0:1, :]
    scale = mod_ref[0, 0, 1:2, :]
    u = (h * (1.0 + scale) + shift).astype(BF16)
    y = _dot(u, w_ref[...])
    cos, s_up, s_dn = cos_ref[...], su_ref[...], sd_ref[...]
    q_scale = HEAD_DIM ** -0.5

    def rope(xc):
        half = ROPE_AXIS_DIM // 2
        return xc * cos + pltpu.roll(xc, LANES - half, 1) * s_up + pltpu.roll(xc, half, 1) * s_dn

    def emit(out_ref, col0, width, roped, mul):
        for j in range(width // LANES):
            xc = y[:, col0 + j * LANES: col0 + (j + 1) * LANES]
            if roped:
                xc = rope(xc)
            if mul != 1.0:
                xc = xc * mul
            out_ref[0, :, j * LANES:(j + 1) * LANES] = xc.astype(out_ref.dtype)

    c = 0
    emit(qa_ref, c, A_WIDTH, True, q_scale); c += A_WIDTH
    emit(ka_ref, c, A_KV_WIDTH, True, 1.0); c += A_KV_WIDTH
    emit(va_ref, c, A_KV_WIDTH, False, 1.0); c += A_KV_WIDTH
    emit(qb_ref, c, B_WIDTH, True, q_scale); c += B_WIDTH
    emit(kb_ref, c, B_WIDTH, True, 1.0); c += B_WIDTH
    emit(vb_ref, c, B_WIDTH, False, 1.0)


def _attn_inproj(h, mod, w_in, n_ctx):
    B, N, D = h.shape
    tm = ROW_TILE
    cos, s_up, s_dn = _rope_tables(n_ctx, N - n_ctx)
    widths = (A_WIDTH, A_KV_WIDTH, A_KV_WIDTH, B_WIDTH, B_WIDTH, B_WIDTH)
    tab_spec = pl.BlockSpec((tm, LANES), lambda b, i: (i, 0))
    return pl.pallas_call(
        _inproj_kernel,
        grid=(B, N // tm),
        in_specs=[pl.BlockSpec((1, tm, D), lambda b, i: (b, i, 0)),
                  _mod_spec(D, n_ctx // tm),
                  pl.BlockSpec(w_in.shape, lambda b, i: (0, 0)),
                  tab_spec, tab_spec, tab_spec],
        out_specs=[pl.BlockSpec((1, tm, w), lambda b, i: (b, i, 0)) for w in widths],
        out_shape=[jax.ShapeDtypeStruct((B, N, w), BF16) for w in widths],
        compiler_params=_cparams(("arbitrary", "arbitrary")),
    )(h, mod, w_in.astype(BF16), cos, s_up, s_dn)


def _win_attn_kernel(q_ref, kc_ref, vc_ref, kl_ref, km_ref, kr_ref, vl_ref, vm_ref, vr_ref, sink_ref, bias_ref,
                     o_ref):
    qb = Q_BLOCK
    bias = bias_ref[0]
    outs = []
    for kv in range(A_KV_HEADS):
        cols = slice(kv * HEAD_DIM, (kv + 1) * HEAD_DIM)
        k_all = jnp.concatenate([kc_ref[0, :, cols], kl_ref[0, :, cols], km_ref[0, :, cols],
                                 kr_ref[0, :, cols]], axis=0)
        v_all = jnp.concatenate([vc_ref[0, :, cols], vl_ref[0, :, cols], vm_ref[0, :, cols],
                                 vr_ref[0, :, cols]], axis=0)
        q0 = kv * A_GROUP
        q = jnp.concatenate([q_ref[0, :, (q0 + g) * HEAD_DIM:(q0 + g + 1) * HEAD_DIM]
                             for g in range(A_GROUP)], axis=0)
        sink = jnp.concatenate([jnp.broadcast_to(sink_ref[q0 + g:q0 + g + 1, 0:1], (qb, 1))
                                for g in range(A_GROUP)], axis=0)
        s = _dot_nt(q, k_all) + bias
        m = jnp.maximum(jnp.max(s, -1, keepdims=True), sink)
        e = jnp.exp(s - m)
        denom = jnp.sum(e, -1, keepdims=True) + jnp.exp(sink - m)
        o = _dot(e.astype(BF16), v_all) * (1.0 / denom)
        outs += [o[g * qb:(g + 1) * qb] for g in range(A_GROUP)]
    for j2 in range(A_Q_HEADS // 2):
        pair = jnp.concatenate([outs[2 * j2], outs[2 * j2 + 1]], axis=1)
        o_ref[0, :, j2 * LANES:(j2 + 1) * LANES] = pair.astype(o_ref.dtype)


def _win_attn(qa, ka, va, sink, n_ctx):
    B, N, _ = qa.shape
    qb = Q_BLOCK
    nb = N // qb
    ncb = n_ctx // qb
    sink_pad = jnp.broadcast_to(sink.reshape(A_Q_HEADS, 1).astype(F32), (A_Q_HEADS, LANES))

    def left(b, j):
        return (b, jnp.clip(j - 1, ncb, nb - 1), 0)

    def mid(b, j):
        return (b, jnp.clip(j, ncb, nb - 1), 0)

    def right(b, j):
        return (b, jnp.clip(j + 1, ncb, nb - 1), 0)

    rows = A_GROUP * qb
    r_idx = (np.arange(rows) % qb)[:, None]
    c_idx = np.arange(qb)[None, :]
    never = np.full((rows, qb), NEG_INF, np.float32)
    always = np.zeros((rows, qb), np.float32)
    prev_ok = np.where(c_idx >= r_idx, 0.0, NEG_INF).astype(np.float32)
    next_ok = np.where(c_idx <= r_idx, 0.0, NEG_INF).astype(np.float32)
    ctx_ok = np.zeros((rows, n_ctx), np.float32)
    kinds = [np.concatenate([ctx_ok, never, never, never], 1)]
    for has_prev in (False, True):
        for has_next in (False, True):
            kinds.append(np.concatenate([ctx_ok, prev_ok if has_prev else never, always,
                                         next_ok if has_next else never], 1))
    bias = jnp.asarray(np.stack(kinds))

    def kind(b, j):
        lat = 1 + 2 * (j > ncb).astype(I32) + (j < nb - 1).astype(I32)
        return (jnp.where(j >= ncb, lat, 0), 0, 0)

    kv_blk = lambda im: pl.BlockSpec((1, qb, A_KV_WIDTH), im)
    ctx_blk = pl.BlockSpec((1, n_ctx, A_KV_WIDTH), lambda b, j: (b, 0, 0))
    return pl.pallas_call(
        _win_attn_kernel,
        grid=(B, nb),
        in_specs=[pl.BlockSpec((1, qb, A_WIDTH), lambda b, j: (b, j, 0)),
                  ctx_blk, ctx_blk,
                  kv_blk(left), kv_blk(mid), kv_blk(right),
                  kv_blk(left), kv_blk(mid), kv_blk(right),
                  pl.BlockSpec((A_Q_HEADS, LANES), lambda b, j: (0, 0)),
                  pl.BlockSpec((1,) + bias.shape[1:], kind)],
        out_specs=pl.BlockSpec((1, qb, A_WIDTH), lambda b, j: (b, j, 0)),
        out_shape=jax.ShapeDtypeStruct((B, N, A_WIDTH), BF16),
        compiler_params=_cparams(("arbitrary", "arbitrary")),
    )(qa, ka, va, ka, ka, ka, va, va, va, sink_pad, bias)


def _diff_attn_kernel(n_ctx, lam_init, q_ref, k_ref, v_ref, lv_ref, g_ref, o_ref):
    j = pl.program_id(1)
    lv = lv_ref[...]
    lam = (jnp.exp(jnp.sum(lv[0:1] * lv[1:2], -1, keepdims=True))
           - jnp.exp(jnp.sum(lv[2:3] * lv[3:4], -1, keepdims=True)) + lam_init)
    gain = g_ref[...] * (1.0 - lam_init)

    def run(n_keys):
        for hd in range(B_HEADS):
            parts = []
            for mm in range(2):
                c0 = (hd * 2 + mm) * HEAD_DIM
                q = q_ref[0, :, c0:c0 + HEAD_DIM]
                k = k_ref[0, :n_keys, c0:c0 + HEAD_DIM]
                s = _dot_nt(q, k)
                e = jnp.exp(s - jnp.max(s, -1, keepdims=True))
                parts.append((e, jnp.sum(e, -1, keepdims=True)))
            (e0, l0), (e1, l1) = parts
            v = v_ref[0, :n_keys, hd * B_V_DIM:(hd + 1) * B_V_DIM]
            o = _dot(e0.astype(BF16), v) * (1.0 / l0) - _dot(e1.astype(BF16), v) * (lam / l1)
            o = o * lax.rsqrt(jnp.mean(o * o, -1, keepdims=True) + SUBLN_EPS) * gain
            o_ref[0, :, hd * B_V_DIM:(hd + 1) * B_V_DIM] = o.astype(o_ref.dtype)

    @pl.when(j == 0)
    def _():
        run(n_ctx)

    @pl.when(j > 0)
    def _():
        run(k_ref.shape[1])


def _diff_attn(qb, kb, vb, lam_vecs, subln_g, lam_init, n_ctx):
    B, N, _ = qb.shape
    tq = n_ctx
    return pl.pallas_call(
        functools.partial(_diff_attn_kernel, n_ctx, lam_init),
        grid=(B, N // tq),
        in_specs=[pl.BlockSpec((1, tq, B_WIDTH), lambda b, j: (b, j, 0)),
                  pl.BlockSpec((1, N, B_WIDTH), lambda b, j: (b, 0, 0)),
                  pl.BlockSpec((1, N, B_WIDTH), lambda b, j: (b, 0, 0)),
                  pl.BlockSpec((4, HEAD_DIM), lambda b, j: (0, 0)),
                  pl.BlockSpec((1, B_V_DIM), lambda b, j: (0, 0))],
        out_specs=pl.BlockSpec((1, tq, B_WIDTH), lambda b, j: (b, j, 0)),
        out_shape=jax.ShapeDtypeStruct((B, N, B_WIDTH), BF16),
        compiler_params=_cparams(("arbitrary", "arbitrary")),
    )(qb, kb, vb, lam_vecs.astype(F32), subln_g.reshape(1, B_V_DIM).astype(F32))


def _mix_out_kernel(n_in, *refs):
    xs = refs[:n_in]
    ws = refs[n_in:2 * n_in]
    h_ref, mod_ref, g_ref, b_ref, hn_ref, u_ref = refs[2 * n_in:]
    o = _dot(xs[0][0], ws[0][...])
    for x_ref, w_ref in zip(xs[1:], ws[1:]):
        o = o + _dot(x_ref[0], w_ref[...])
    z = DEEPNORM_ALPHA * h_ref[0] + mod_ref[0, 0, 2:3, :] * o
    hn = _layer_norm(z, g_ref[...], b_ref[...])
    hn_ref[0] = hn
    u_ref[0] = _pack_halves(hn * (1.0 + mod_ref[0, 0, 4:5, :]) + mod_ref[0, 0, 3:4, :])


def _mix_out(xs, ws, h, mod, ln_g, ln_b, n_ctx, row0):
    B, N, D = h.shape
    tm = ROW_TILE
    t0 = row0 // tm
    n_out = N - row0
    row_spec = lambda w: pl.BlockSpec((1, tm, w), lambda b, i: (b, i + t0, 0))
    out_spec = pl.BlockSpec((1, tm, D), lambda b, i: (b, i, 0))
    vec_spec = pl.BlockSpec((1, D), lambda b, i: (0, 0))
    return pl.pallas_call(
        functools.partial(_mix_out_kernel, len(xs)),
        grid=(B, n_out // tm),
        in_specs=([row_spec(x.shape[-1]) for x in xs]
                  + [pl.BlockSpec(w.shape, lambda b, i: (0, 0)) for w in ws]
                  + [row_spec(D),
                     pl.BlockSpec((1, 1, 6, D), lambda b, i: (b, jnp.minimum((i + t0) // (n_ctx // tm), 1), 0, 0)),
                     vec_spec, vec_spec]),
        out_specs=[out_spec, pl.BlockSpec((1, tm, D // 2), lambda b, i: (b, i, 0))],
        out_shape=[jax.ShapeDtypeStruct((B, n_out, D), F32), jax.ShapeDtypeStruct((B, n_out, D // 2), I32)],
        compiler_params=_cparams(("arbitrary", "arbitrary")),
    )(*xs, *[w.astype(BF16) for w in ws], h, mod, ln_g.reshape(1, D), ln_b.reshape(1, D))


def _router_kernel(u_ref, rt_ref, bias_ref, tri_ref, e_ref, gw_ref, rank_ref, cnt_ref, carry_ref):
    i = pl.program_id(0)

    @pl.when(i == 0)
    def _():
        carry_ref[...] = jnp.zeros_like(carry_ref)

    tm = u_ref.shape[0]
    per_group = N_EXPERTS // N_GROUPS
    neg = -jnp.inf
    u_lo, u_hi = _unpack_halves(u_ref[...])
    half = u_ref.shape[1]
    logits = (_dot_nt(rt_ref[:, :half], u_lo.astype(BF16))
              + _dot_nt(rt_ref[:, half:], u_hi.astype(BF16)))
    scores = jax.nn.sigmoid(logits)
    sel = scores + bias_ref[...]
    io_in = lax.broadcasted_iota(I32, (per_group, tm), 0)
    grp_rows = []
    for gi in range(N_GROUPS):
        sg = sel[gi * per_group:(gi + 1) * per_group]
        m1 = jnp.max(sg, axis=0, keepdims=True)
        i1 = jnp.min(jnp.where(sg == m1, io_in, per_group), axis=0, keepdims=True)
        m2 = jnp.max(jnp.where(io_in == i1, neg, sg), axis=0, keepdims=True)
        grp_rows.append(m1 + m2)
    grp = jnp.concatenate(grp_rows, axis=0)
    io_g = lax.broadcasted_iota(I32, grp.shape, 0)
    g_sel = jnp.zeros(grp.shape, F32)
    for _ in range(TOPK_GROUPS):
        m = jnp.max(grp, axis=0, keepdims=True)
        hit = io_g == jnp.min(jnp.where(grp == m, io_g, N_GROUPS), axis=0, keepdims=True)
        g_sel = jnp.where(hit, 1.0, g_sel)
        grp = jnp.where(hit, neg, grp)
    selm = jnp.concatenate(
        [jnp.where(g_sel[gi:gi + 1] > 0.5, sel[gi * per_group:(gi + 1) * per_group], NEG_INF)
         for gi in range(N_GROUPS)], axis=0)
    io_e = lax.broadcasted_iota(I32, selm.shape, 0)
    chosen_f = jnp.zeros(selm.shape, F32)
    idx, gws = [], []
    for _ in range(TOP_K):
        m = jnp.max(selm, axis=0, keepdims=True)
        ik = jnp.min(jnp.where(selm == m, io_e, N_EXPERTS), axis=0, keepdims=True)
        hit = io_e == ik
        idx.append(ik)
        gws.append(jnp.sum(jnp.where(hit, scores, 0.0), axis=0, keepdims=True))
        chosen_f = jnp.where(hit, 1.0, chosen_f)
        selm = jnp.where(hit, neg, selm)
    gw = jnp.concatenate(gws, axis=0)
    gw_ref[...] = gw / jnp.sum(gw, axis=0, keepdims=True) * ROUTED_SCALE
    e_ref[...] = jnp.concatenate(idx, axis=0)
    before = _dot(chosen_f.astype(BF16), tri_ref[...]) + carry_ref[...]
    ranks = [jnp.sum(jnp.where(io_e == ik, before, 0.0), axis=0, keepdims=True) for ik in idx]
    rank_ref[...] = jnp.concatenate(ranks, axis=0).astype(I32)
    carry_ref[...] = carry_ref[...] + jnp.sum(chosen_f, axis=1, keepdims=True)
    cnt_ref[...] = carry_ref[...].astype(I32)


def _router(u, router, bias):
    T = u.shape[0]
    D = router.shape[0]
    tm = ROW_TILE
    tri = jnp.asarray(np.triu(np.ones((tm, tm), np.float32), 1), BF16)
    tok_spec = pl.BlockSpec((TOP_K, tm), lambda i: (0, i))
    return pl.pallas_call(
        _router_kernel,
        grid=(T // tm,),
        in_specs=[pl.BlockSpec((tm, D // 2), lambda i: (i, 0)),
                  pl.BlockSpec((N_EXPERTS, D), lambda i: (0, 0)),
                  pl.BlockSpec((N_EXPERTS, 1), lambda i: (0, 0)),
                  pl.BlockSpec((tm, tm), lambda i: (0, 0))],
        out_specs=[tok_spec, tok_spec, tok_spec, pl.BlockSpec((N_EXPERTS, 1), lambda i: (0, 0))],
        out_shape=[jax.ShapeDtypeStruct((TOP_K, T), I32), jax.ShapeDtypeStruct((TOP_K, T), F32),
                   jax.ShapeDtypeStruct((TOP_K, T), I32), jax.ShapeDtypeStruct((N_EXPERTS, 1), I32)],
        scratch_shapes=[pltpu.VMEM((N_EXPERTS, 1), F32)],
        compiler_params=_cparams(("arbitrary",)),
    )(u, router.T.astype(BF16), bias.reshape(N_EXPERTS, 1).astype(F32), tri)


def _sc_mesh():
    return plsc.VectorSubcoreMesh(core_axis_name="c", subcore_axis_name="s")


def _sc_scatter_rows(x, dest, n_rows):
    T, W = x.shape
    K = dest.shape[0]
    win = SC_WINDOW

    @functools.partial(pl.kernel, out_type=jax.ShapeDtypeStruct((n_rows, W), x.dtype), mesh=_sc_mesh(),
                       scratch_types=[])
    def scatter(x_hbm, i_hbm, o_hbm):
        def body(x_vmem, i_vmem):
            for k in range(K):
                pltpu.sync_copy(x_vmem, o_hbm.at[i_vmem.at[k]])

        pltpu.emit_pipeline(
            body,
            grid=(T // win,),
            in_specs=[pl.BlockSpec((win, W), lambda j: (j, 0), pipeline_mode=pl.Buffered(1)),
                      pl.BlockSpec((K, win), lambda j: (0, j))],
            out_specs=[],
            core_axis_name=("c", "s"),
            dimension_semantics=(pltpu.PARALLEL,),
        )(x_hbm, i_hbm)

    return scatter(x, dest)


def _sc_gather_rows(y, dest):
    K, T = dest.shape
    W = y.shape[1]
    win = SC_WINDOW

    @functools.partial(pl.kernel, out_type=jax.ShapeDtypeStruct((K * T, W), y.dtype), mesh=_sc_mesh(),
                       scratch_types=[])
    def gather(y_hbm, i_hbm, o_hbm):
        def body(i_vmem, o_vmem):
            pltpu.sync_copy(y_hbm.at[i_vmem.at[0]], o_vmem)

        pltpu.emit_pipeline(
            body,
            grid=(K * T // win,),
            in_specs=[pl.BlockSpec((1, win), lambda j: (0, j))],
            out_specs=[pl.BlockSpec((win, W), lambda j: (j, 0), pipeline_mode=pl.Buffered(1))],
            core_axis_name=("c", "s"),
            dimension_semantics=(pltpu.PARALLEL,),
        )(i_hbm, o_hbm)

    return gather(y, dest.reshape(1, K * T)).reshape(K, T, W)


def _expert_kernel(first_ref, count_ref, x_hbm, wi_ref, wo_ref, y_hbm, wi_b, wo_b, xbuf, ybuf, sem_in, sem_out):
    e = pl.program_id(0)
    n_in, blk = xbuf.shape[0], xbuf.shape[1]
    n_out = ybuf.shape[0]
    g0, g1 = first_ref[e], first_ref[e + 1]
    g_end = first_ref[pl.num_programs(0)]

    def x_copy(g):
        return pltpu.make_async_copy(x_hbm.at[pl.ds(g * blk, blk)], xbuf.at[g % n_in], sem_in.at[g % n_in])

    def y_copy(g):
        return pltpu.make_async_copy(ybuf.at[g % n_out], y_hbm.at[pl.ds(g * blk, blk)], sem_out.at[g % n_out])

    for ahead in range(n_in - 1):
        @pl.when(jnp.logical_and(e == 0, g_end > ahead))
        def _():
            x_copy(ahead).start()

    @pl.when(g1 > g0)
    def _():
        wi_b[...] = wi_ref[0, 0].astype(BF16)
        wo_b[...] = wo_ref[0, 0].astype(BF16)

    ff = wo_b.shape[0]

    def block(g, carry):
        x_copy(g).wait()

        @pl.when(g + n_in - 1 < g_end)
        def _():
            x_copy(g + n_in - 1).start()

        @pl.when(g >= n_out)
        def _():
            y_copy(g - n_out).wait()

        row = lax.broadcasted_iota(I32, (blk, xbuf.shape[2]), 0) + (g - g0) * blk
        hcat = _dot_halves(jnp.where(row < count_ref[e], xbuf[g % n_in], 0), wi_b)
        act = (_silu(hcat[:, :ff]) * hcat[:, ff:]).astype(BF16)
        ybuf[g % n_out] = _pack_halves(_dot(act, wo_b[...]))
        y_copy(g).start()
        return carry

    lax.fori_loop(g0, g1, block, 0)

    @pl.when(e == pl.num_programs(0) - 1)
    def _():
        for back in range(n_out, 0, -1):
            @pl.when(g_end >= back)
            def _():
                y_copy(g_end - back).wait()


def _experts(xs, first_block, counts, w_in, w_out, layer):
    P, half = xs.shape
    n_exp, D, ff2 = w_in.shape[-3:]
    return pl.pallas_call(
        _expert_kernel,
        grid_spec=pltpu.PrefetchScalarGridSpec(
            num_scalar_prefetch=2,
            grid=(n_exp,),
            in_specs=[pl.BlockSpec(memory_space=pl.ANY),
                      pl.BlockSpec((1, 1, D, ff2), lambda e, fb, cnt: (layer, e, 0, 0)),
                      pl.BlockSpec((1, 1, ff2 // 2, D), lambda e, fb, cnt: (layer, e, 0, 0))],
            out_specs=pl.BlockSpec(memory_space=pl.ANY),
            scratch_shapes=[pltpu.VMEM((D, ff2), BF16), pltpu.VMEM((ff2 // 2, D), BF16),
                            pltpu.VMEM((EXPERT_IN_SLOTS, MOE_BLOCK, half), I32),
                            pltpu.VMEM((EXPERT_OUT_SLOTS, MOE_BLOCK, half), I32),
                            pltpu.SemaphoreType.DMA((EXPERT_IN_SLOTS,)),
                            pltpu.SemaphoreType.DMA((EXPERT_OUT_SLOTS,))]),
        out_shape=jax.ShapeDtypeStruct((P, half), I32),
        compiler_params=_cparams(("arbitrary",)),
    )(first_block, counts.astype(I32), xs, w_in, w_out)


def _combine_kernel(yg_ref, gw_ref, u_ref, wsi_ref, wso_ref, h_ref, mod_ref, g_ref, b_ref, o_ref):
    ff = wso_ref.shape[0]
    hcat = _dot_halves(u_ref[...], wsi_ref)
    shared = _dot((_silu(hcat[:, :ff]) * hcat[:, ff:]).astype(BF16), wso_ref[...])
    lo, hi = None, None
    for k in range(TOP_K):
        y_lo, y_hi = _unpack_halves(yg_ref[k])
        gk = gw_ref[:, k:k + 1]
        lo = y_lo * gk if lo is None else lo + y_lo * gk
        hi = y_hi * gk if hi is None else hi + y_hi * gk
    routed = jnp.concatenate([lo, hi], axis=1)
    z = DEEPNORM_ALPHA * h_ref[...] + mod_ref[0] * (routed + shared)
    o_ref[...] = _layer_norm(z, g_ref[...], b_ref[...])


def _combine(yg, gw_t, u, ws_in, ws_out, h, gate, gate_index, ln_g, ln_b):
    T, D = h.shape
    tm = ROW_TILE
    vec_spec = pl.BlockSpec((1, D), lambda i: (0, 0))
    row_spec = pl.BlockSpec((tm, D), lambda i: (i, 0))
    packed_spec = pl.BlockSpec((tm, D // 2), lambda i: (i, 0))
    return pl.pallas_call(
        _combine_kernel,
        grid=(T // tm,),
        in_specs=[pl.BlockSpec((TOP_K, tm, D // 2), lambda i: (0, i, 0)),
                  pl.BlockSpec((tm, TOP_K), lambda i: (i, 0)),
                  packed_spec,
                  pl.BlockSpec(ws_in.shape, lambda i: (0, 0)),
                  pl.BlockSpec(ws_out.shape, lambda i: (0, 0)),
                  row_spec,
                  pl.BlockSpec((1,) + gate.shape[1:], lambda i: (gate_index(i), 0, 0)),
                  vec_spec, vec_spec],
        out_specs=row_spec,
        out_shape=jax.ShapeDtypeStruct((T, D), F32),
        compiler_params=_cparams(("arbitrary",)),
    )(yg, gw_t, u, ws_in.astype(BF16), ws_out.astype(BF16), h, gate,
      ln_g.reshape(1, D), ln_b.reshape(1, D))


def _slots_kernel(e_ref, rank_ref, start_ref, dest_ref):
    io_e = lax.broadcasted_iota(I32, (N_EXPERTS, e_ref.shape[1]), 0)
    rows = [jnp.sum(jnp.where(io_e == e_ref[k:k + 1, :], start_ref[...], 0), axis=0, keepdims=True)
            for k in range(TOP_K)]
    dest_ref[...] = jnp.concatenate(rows, axis=0) + rank_ref[...]


def _slots(eidx, rank, pstart):
    T = eidx.shape[1]
    tm = ROW_TILE
    tok_spec = pl.BlockSpec((TOP_K, tm), lambda i: (0, i))
    return pl.pallas_call(
        _slots_kernel,
        grid=(T // tm,),
        in_specs=[tok_spec, tok_spec, pl.BlockSpec((N_EXPERTS, 1), lambda i: (0, 0))],
        out_specs=tok_spec,
        out_shape=jax.ShapeDtypeStruct((TOP_K, T), I32),
        compiler_params=_cparams(("arbitrary",)),
    )(eidx, rank, pstart.reshape(N_EXPERTS, 1))


def _moe_layer(u, h, gate, gate_index, router, bias, w_in, w_out, ws_in, ws_out, ln_g, ln_b, layer):
    T = u.shape[0]
    eidx, gw, rank, counts = _router(u, router, bias)
    counts = counts[:, 0]
    padded = (counts + MOE_BLOCK - 1) // MOE_BLOCK * MOE_BLOCK
    pend = jnp.cumsum(padded)
    pstart = (pend - padded).astype(I32)
    dest = _slots(eidx, rank, pstart)
    n_blocks = -(-(T * TOP_K + N_EXPERTS * (MOE_BLOCK - 1)) // MOE_BLOCK)
    first_block = jnp.concatenate([jnp.zeros((1,), I32), (pend // MOE_BLOCK).astype(I32)])
    xs = _sc_scatter_rows(u, dest, n_blocks * MOE_BLOCK)
    y = _experts(xs, first_block, counts, w_in, w_out, layer)
    return _combine(_sc_gather_rows(y, dest), gw.T, u, ws_in, ws_out, h, gate, gate_index, ln_g, ln_b)


def _seg_ones(width=LANES):
    idx = np.arange(width) // HEAD_DIM
    return jnp.asarray((idx[:, None] == idx[None, :]).astype(np.float32), BF16)


def _head_sum(x, ones_ref):
    outs = []
    for j in range(x.shape[1] // LANES):
        xc = x[:, j * LANES:(j + 1) * LANES]
        hi = xc.astype(BF16)
        lo = (xc - hi.astype(F32)).astype(BF16)
        outs.append(_dot(hi, ones_ref[...]) + _dot(lo, ones_ref[...]))
    return jnp.concatenate(outs, axis=1)


def _rwkv_proj_kernel(seg_tiles, h_ref, hp_ref, hn_ref, mod_ref, mu_ref, wrkv_ref, g1_ref, g2_ref, d1_ref, d2_ref,
                      d0_ref, i1_ref, i2_ref, i0_ref, kk_ref, ka_ref, rk_ref, ones_ref,
                      r_ref, v_ref, a_ref, g_ref, bonus_ref, w_ref, k_ref, b_ref):
    i = pl.program_id(0)
    nb = hp_ref.shape[0]
    shift, scale = mod_ref[0, 0], mod_ref[0, 1]
    u = h_ref[...] * (1.0 + scale) + shift
    starts = jnp.logical_or(i == 0, i == seg_tiles)
    ends = jnp.logical_or(i == seg_tiles - 1, i == pl.num_programs(0) - 1)
    u_before = (hp_ref[...] * (1.0 + scale[:nb]) + shift[:nb]) * jnp.where(starts, 0.0, 1.0)
    u_after = (hn_ref[...] * (1.0 + scale[:nb]) + shift[:nb]) * jnp.where(ends, 0.0, 1.0)
    dx = 0.5 * (jnp.concatenate([u_before, u[:-nb]], axis=0) + jnp.concatenate([u[nb:], u_after], axis=0)) - u
    mix = lambda m: (u + dx * mu_ref[m:m + 1, :])
    xr, xw, xk, xv, xa, xg = [mix(m) for m in range(6)]
    r = _dot(xr.astype(BF16), wrkv_ref[0])
    k = _dot(xk.astype(BF16), wrkv_ref[1])
    v = _dot(xv.astype(BF16), wrkv_ref[2])
    g = _dot(jax.nn.sigmoid(_dot(xg.astype(BF16), g1_ref[...])).astype(BF16), g2_ref[...])
    kk = k * kk_ref[...]
    kk = kk * lax.rsqrt(jnp.maximum(_head_sum(kk * kk, ones_ref), 1e-24))
    r_ref[...] = r
    v_ref[...] = v
    a_ref[...] = -kk
    g_ref[...] = g
    k_sum = None
    xw_b = xw.astype(BF16)
    xa_b = xa.astype(BF16)
    for d in range(2):
        lw = d0_ref[d:d + 1, :] + _dot(jnp.tanh(_dot(xw_b, d1_ref[d])).astype(BF16), d2_ref[d])
        softplus = jnp.maximum(-lw, 0.0) + jnp.log(1.0 + jnp.exp(-jnp.abs(lw)))
        logw = -softplus - 0.5
        w_ref[d] = jnp.exp(-jnp.exp(logw))
        eta = jax.nn.sigmoid(i0_ref[d:d + 1, :] + _dot(_dot(xa_b, i1_ref[d]).astype(BF16), i2_ref[d]))
        k_d = k * (1.0 + (eta - 1.0) * ka_ref[...])
        k_ref[d] = k_d
        b_ref[d] = kk * eta
        k_sum = k_d if k_sum is None else k_sum + k_d
    bonus_ref[...] = _head_sum(r * k_sum * rk_ref[...], ones_ref) * v


def _rwkv_proj(h, mod_rows, batch, n_ctx, p):
    T, D = h.shape
    tm = PROJ_TILE
    per_tile = tm // batch
    seg_tiles = n_ctx // per_tile
    n_steps = T // batch
    row = pl.BlockSpec((tm, D), lambda i: (i, 0))
    before = pl.BlockSpec((batch, D), lambda i: (jnp.maximum(i * per_tile - 1, 0), 0))
    after = pl.BlockSpec((batch, D), lambda i: (jnp.minimum((i + 1) * per_tile, n_steps - 1), 0))
    mod_spec = pl.BlockSpec((1, 2, tm, D), lambda i: (jnp.minimum(i // seg_tiles, 1), 0, 0, 0))
    row2 = pl.BlockSpec((2, tm, D), lambda i: (0, i, 0))
    full = lambda a: pl.BlockSpec(a.shape, lambda i: (0,) * a.ndim)
    bf = lambda a: a.astype(BF16)
    consts = [p['mu'], bf(p['w_rkv']), bf(p['gate1']), bf(p['gate2']), bf(p['dec1']), bf(p['dec2']), p['dec0'],
              bf(p['icl1']), bf(p['icl2']), p['icl0'], p['k_k'].reshape(1, D), p['k_a'].reshape(1, D),
              p['r_k'].reshape(1, D), _seg_ones()]
    one = jax.ShapeDtypeStruct((T, D), F32)
    two = jax.ShapeDtypeStruct((2, T, D), F32)
    return pl.pallas_call(
        functools.partial(_rwkv_proj_kernel, seg_tiles),
        grid=(T // tm,),
        in_specs=[row, before, after, mod_spec] + [full(a) for a in consts],
        out_specs=[row, row, row, row, row, row2, row2, row2],
        out_shape=[one, one, one, one, one, two, two, two],
        compiler_params=_cparams(("arbitrary",)),
    )(h, h, h, mod_rows, *consts)


def _scan_kernel(r_ref, w_ref, k_ref, v_ref, a_ref, b_ref, ones_ref, hsel_ref,
                 y_ref, s_ref, vt_ref):
    d = pl.program_id(0)
    c = pl.program_id(2)
    tc, nb = r_ref.shape[0], r_ref.shape[1]
    tw = SCAN_TILE
    n_wide = r_ref.shape[2] // tw
    heads = tw // HEAD_DIM
    assert heads * tc == tw

    @pl.when(c == 0)
    def _():
        s_ref[...] = jnp.zeros_like(s_ref)

    for bb in range(nb):
        for q in range(n_wide):
            vt = v_ref[:, bb, q * tw:(q + 1) * tw].T
            vt_ref[bb * n_wide + q] = jnp.concatenate(
                [vt[h * HEAD_DIM:(h + 1) * HEAD_DIM] for h in range(heads)], axis=1)

    head_base = (lax.broadcasted_iota(I32, (HEAD_DIM, LANES), 1) // HEAD_DIM) * tc
    tiles = [(bb, q) for bb in range(nb) for q in range(n_wide)]
    groups = [tiles[i:i + SCAN_GROUP] for i in range(0, len(tiles), SCAN_GROUP)]

    def stacked(grp, get, dtype=F32):
        def wide(bb, q):
            return jnp.concatenate(
                [jnp.broadcast_to(get(bb, slice(q * tw + hf * LANES, q * tw + (hf + 1) * LANES)).astype(dtype),
                                  (HEAD_DIM, LANES)) for hf in range(tw // LANES)], axis=1)
        return jnp.concatenate([wide(bb, q) for bb, q in grp], axis=0)

    def load_state(grp):
        return jnp.concatenate([s_ref[bb * n_wide + q] for bb, q in grp], axis=0)

    def emit_y(grp, st_b, t_y):
        r_rows = stacked(grp, lambda bb, cols: r_ref[t_y, bb:bb + 1, cols], BF16)
        yh = _dot_nt(hsel_ref[...], st_b * r_rows)
        first = tiles.index(grp[0])
        y_ref[0, t_y, :, first * HEAD_DIM:(first + len(grp)) * HEAD_DIM] = yh[:heads]

    def step(s_i, carry):
        t = jnp.where(d == 0, s_i, tc - 1 - s_i)
        t_prev = jnp.where(s_i == 0, t, jnp.where(d == 0, t - 1, t + 1))
        pick = head_base + t
        for grp in groups:
            one = lambda ref: stacked(grp, lambda bb, cols: ref[t, bb:bb + 1, cols])
            two = lambda ref: stacked(grp, lambda bb, cols: ref[0, t, bb:bb + 1, cols])
            st = load_state(grp)
            st_b = st.astype(BF16)
            a_rows = stacked(grp, lambda bb, cols: a_ref[t, bb:bb + 1, cols], BF16)
            sa = _dot(st_b * a_rows, ones_ref[...])
            emit_y(grp, st_b, t_prev)
            vcol = jnp.concatenate(
                [jnp.concatenate([jnp.take_along_axis(vt_ref[bb * n_wide + q, :, hf * LANES:(hf + 1) * LANES],
                                                      pick, axis=1) for hf in range(tw // LANES)], axis=1)
                 for bb, q in grp], axis=0)
            st = st * two(w_ref) + sa * two(b_ref) + vcol * two(k_ref)
            for j, (bb, q) in enumerate(grp):
                s_ref[bb * n_wide + q] = st[j * HEAD_DIM:(j + 1) * HEAD_DIM]
        return carry

    lax.fori_loop(0, tc, step, 0, unroll=SCAN_UNROLL)
    t_last = jnp.where(d == 0, tc - 1, 0)
    for grp in groups:
        emit_y(grp, load_state(grp).astype(BF16), t_last)


def _wkv_scan(r, w, k, v, a, b, n_ctx):
    N, B, D = r.shape
    tc = SCAN_CHUNK
    wc = SCAN_COLS
    n_wide = wc // SCAN_TILE
    nc = N // tc
    ncc = n_ctx // tc

    def chunk(d, c):
        rev = jnp.where(c < ncc, ncc - 1 - c, nc - 1 - (c - ncc))
        return jnp.where(d == 0, c, rev)

    one = pl.BlockSpec((tc, B, wc), lambda d, g, c: (chunk(d, c), 0, g))
    two = pl.BlockSpec((1, tc, B, wc), lambda d, g, c: (d, chunk(d, c), 0, g))
    seg = np.arange(SCAN_TILE) // HEAD_DIM
    hsel = np.zeros((8, SCAN_TILE), np.float32)
    for hh in range(SCAN_TILE // HEAD_DIM):
        hsel[hh, seg == hh] = 1.0
    const = lambda a_: pl.BlockSpec(a_.shape, lambda d, g, c: (0, 0))
    consts = [_seg_ones(SCAN_TILE), jnp.asarray(hsel, BF16)]
    heads = SCAN_TILE // HEAD_DIM
    ncg = D // wc
    y = pl.pallas_call(
        _scan_kernel,
        grid=(2, ncg, nc),
        in_specs=[one, two, two, one, one, two] + [const(a_) for a_ in consts],
        out_specs=pl.BlockSpec((1, tc, heads, B * n_wide * HEAD_DIM), lambda d, g, c: (d, chunk(d, c), 0, g)),
        out_shape=jax.ShapeDtypeStruct((2, N, heads, ncg * B * n_wide * HEAD_DIM), F32),
        scratch_shapes=[pltpu.VMEM((B * n_wide, HEAD_DIM, SCAN_TILE), F32),
                        pltpu.VMEM((B * n_wide, HEAD_DIM, SCAN_TILE), F32)],
        compiler_params=_cparams(("arbitrary", "arbitrary", "arbitrary")),
    )(r, w, k, v, a, b, *consts)
    y = y.reshape(2, N, heads, ncg, B, n_wide * HEAD_DIM)
    return jnp.transpose(y, (0, 1, 4, 2, 3, 5)).reshape(2, N, B, D)


def _scan_head_order(d):
    heads = SCAN_TILE // HEAD_DIM
    n_wide = SCAN_COLS // SCAN_TILE
    ncg = d // SCAN_COLS
    order = []
    for g in range(ncg):
        for q in range(n_wide):
            for h in range(heads):
                order.append((h * ncg + g) * n_wide + q)
    return tuple(order)


def _rwkv_out_kernel(head_order, y0_ref, y1_ref, bonus_ref, g_ref, lnx_ref, ones_ref, w_ref, h_ref, mod_ref,
                     lg_ref, lb_ref, hn_ref, u_ref):
    y_in = y0_ref[0] + y1_ref[0]
    y = jnp.concatenate([y_in[:, p * HEAD_DIM:(p + 1) * HEAD_DIM] for p in head_order], axis=1)
    ym = _head_sum(y, ones_ref) * (1.0 / HEAD_DIM)
    yc = y - ym
    yv = _head_sum(yc * yc, ones_ref) * (1.0 / HEAD_DIM)
    yn = yc * lax.rsqrt(yv + LNX_EPS) * lnx_ref[0:1, :] + lnx_ref[1:2, :]
    x = ((yn + bonus_ref[...]) * g_ref[...]).astype(BF16)
    o = _dot(x, w_ref[...])
    z = DEEPNORM_ALPHA * h_ref[...] + mod_ref[0] * o
    hn = _layer_norm(z, lg_ref[...], lb_ref[...])
    hn_ref[...] = hn
    u_ref[...] = _pack_halves(hn * (1.0 + mod_ref[2]) + mod_ref[1])


def _rwkv_out(y, bonus, g, lnx, w_out, h, mod_rows, ln_g, ln_b, row0):
    T, D = h.shape
    tm = ROW_TILE
    t0 = row0 // tm
    off = pl.BlockSpec((tm, D), lambda i: (i + t0, 0))
    out = pl.BlockSpec((tm, D), lambda i: (i, 0))
    full = lambda a: pl.BlockSpec(a.shape, lambda i: (0,) * a.ndim)
    vec = pl.BlockSpec((1, D), lambda i: (0, 0))
    ones = _seg_ones()
    w_b = w_out.astype(BF16)
    return pl.pallas_call(
        functools.partial(_rwkv_out_kernel, _scan_head_order(D)),
        grid=((T - row0) // tm,),
        in_specs=[pl.BlockSpec((1, tm, D), lambda i: (0, i + t0, 0)),
                  pl.BlockSpec((1, tm, D), lambda i: (1, i + t0, 0)),
                  off, off, full(lnx), full(ones), full(w_b), off, full(mod_rows), vec, vec],
        out_specs=[out, pl.BlockSpec((tm, D // 2), lambda i: (i, 0))],
        out_shape=[jax.ShapeDtypeStruct((T - row0, D), F32), jax.ShapeDtypeStruct((T - row0, D // 2), I32)],
        compiler_params=_cparams(("arbitrary",)),
    )(y, y, bonus, g, lnx, ones, w_b, h, mod_rows, ln_g.reshape(1, D), ln_b.reshape(1, D))


def kernel(x, c, ctx, c_ctx, ada_w, ada_b, post_ln_g, post_ln_b, att_w_in, att_w_out, att_sink, diff_lambda_vecs, diff_subln_g, rk_mu, rk_w_rkv, rk_w_out, rk_decay0, rk_decay1, rk_decay2, rk_iclr0, rk_iclr1, rk_iclr2, rk_gate1, rk_gate2, rk_k_k, rk_k_a, rk_r_k, rk_lnx, moe_router, moe_bias, moe_w_in, moe_w_out, moe_ws_in, moe_ws_out):
    B, S, D = x.shape
    L = ctx.shape[1]
    N = L + S
    tm = ROW_TILE
    assert L % tm == 0 and S % tm == 0 and L % SCAN_CHUNK == 0 and S % SCAN_CHUNK == 0
    assert tm % B == 0 and PROJ_TILE % B == 0 and L % (PROJ_TILE // B) == 0 and D % SCAN_COLS == 0

    rows = -(-(B + 1) // 8) * 8
    cvec = jnp.concatenate([c, c_ctx[None, :], jnp.zeros((rows - B - 1, D), F32)], axis=0)
    mods = [_mod_table(_ada_mod(cvec, ada_w[i], ada_b[i]), B, D) for i in range(DEPTH)]

    h0 = jnp.concatenate([ctx, x], axis=1)
    lam_init = 0.8 - 0.6 * math.exp(-0.3 * 0)
    qa, ka, va, qb, kb, vb = _attn_inproj(h0, mods[0], att_w_in[0], L)
    oa = _win_attn(qa, ka, va, att_sink[0], L)
    ob = _diff_attn(qb, kb, vb, diff_lambda_vecs[0], diff_subln_g[0], lam_init, L)
    h1, u1 = _mix_out([oa, ob], [att_w_out[0][:A_WIDTH], att_w_out[0][A_WIDTH:]], h0, mods[0],
                      post_ln_g[0, 0], post_ln_b[0, 0], L, 0)
    tiles_b, tiles_c = N // tm, L // tm
    gate0 = mods[0][:, :, 5].reshape(B * 2, 1, D)
    gate0_index = lambda i: (i // tiles_b) * 2 + jnp.minimum((i % tiles_b) // tiles_c, 1)
    h2 = _moe_layer(u1.reshape(B * N, D // 2), h1.reshape(B * N, D), gate0, gate0_index, moe_router[0], moe_bias[0],
                    moe_w_in, moe_w_out, moe_ws_in[0], moe_ws_out[0],
                    post_ln_g[0, 1], post_ln_b[0, 1], 0).reshape(B, N, D)

    m_ctx, m_lat = mods[1][:, 0], mods[1][:, 1]
    h2_t = jnp.swapaxes(h2, 0, 1).reshape(N * B, D)
    rows_of = lambda m, j, n: jnp.tile(m[:, j], (n // B, 1))
    proj_mod = jnp.stack([jnp.stack([rows_of(m, 0, PROJ_TILE), rows_of(m, 1, PROJ_TILE)]) for m in (m_ctx, m_lat)])
    params = dict(mu=rk_mu[0], w_rkv=rk_w_rkv[0], gate1=rk_gate1[0], gate2=rk_gate2[0],
                  dec0=rk_decay0[0], dec1=rk_decay1[0], dec2=rk_decay2[0],
                  icl0=rk_iclr0[0], icl1=rk_iclr1[0], icl2=rk_iclr2[0],
                  k_k=rk_k_k[0], k_a=rk_k_a[0], r_k=rk_r_k[0])
    r, v, a, g, bonus, w2, k2, b2 = _rwkv_proj(h2_t, proj_mod, B, L, params)
    tmaj = lambda t: t.reshape(t.shape[:-2] + (N, B, D))
    y = _wkv_scan(tmaj(r), tmaj(w2), tmaj(k2), tmaj(v), tmaj(a), tmaj(b2), L)
    lat_rows = lambda j: rows_of(m_lat, j, tm)
    h3, u3 = _rwkv_out(y.reshape(2, N * B, D), bonus, g, rk_lnx[0], rk_w_out[0], h2_t,
                       jnp.stack([lat_rows(2), lat_rows(3), lat_rows(4)]),
                       post_ln_g[1, 0], post_ln_b[1, 0], L * B)
    out = _moe_layer(u3, h3, lat_rows(5)[None], lambda i: 0, moe_router[1], moe_bias[1],
                     moe_w_in, moe_w_out, moe_ws_in[1], moe_ws_out[1],
                     post_ln_g[1, 1], post_ln_b[1, 1], 1)
    return jnp.swapaxes(out.reshape(S, B, D), 0, 1)
```
